```python
import jax, jax.numpy as jnp
from jax import lax
import numpy as np

D_MODEL = 1024
BATCH = 8
SEQ = 4096
DEPTH = 2

CHUNK = 64
BRANCH_WIDTH = D_MODEL // 2
N_BRANCHES = 3
SB_HEADS = 8
SB_HEAD_DIM = BRANCH_WIDTH // SB_HEADS
SB_BLOCK = 128
SGU_LEN = 128
SGU_GROUPS = 4
SGU_GROUP_DIM = BRANCH_WIDTH // SGU_GROUPS
CONV_WIDTH = 3
MEM_TOKENS = 256
XA_HEADS = 4
XA_HEAD_DIM = D_MODEL // XA_HEADS
FFN_HIDDEN = ((8 * D_MODEL // 3 + 255) // 256) * 256

W_QKV = 3 * BRANCH_WIDTH
W_SGU = 2 * BRANCH_WIDTH
W_CONV = 3 * BRANCH_WIDTH
W_GATES = N_BRANCHES * D_MODEL
IN_COLS = W_QKV + W_SGU + W_CONV + W_GATES
SPLIT_IDX = [BRANCH_WIDTH, 2 * BRANCH_WIDTH, W_QKV,
             W_QKV + W_SGU,
             W_QKV + W_SGU + BRANCH_WIDTH, W_QKV + W_SGU + 2 * BRANCH_WIDTH,
             W_QKV + W_SGU + W_CONV]

kernel_name = "hybrid_stickbreak_gmlp_shortconv_block"


def rms_norm(x, g, eps=1e-6):
    xf = x.astype(jnp.float32)
    y = xf * lax.rsqrt(jnp.mean(xf * xf, axis=-1, keepdims=True) + eps)
    return (y * g.astype(jnp.float32)).astype(x.dtype)


def layer_norm(x, g, b, eps=1e-5):
    xf = x.astype(jnp.float32)
    mu = jnp.mean(xf, axis=-1, keepdims=True)
    xc = xf - mu
    y = xc * lax.rsqrt(jnp.mean(xc * xc, axis=-1, keepdims=True) + eps)
    return (y * g.astype(jnp.float32) + b.astype(jnp.float32)).astype(x.dtype)


def stick_breaking_attention(q, k, v):
    seq = q.shape[2]
    scale = SB_HEAD_DIM ** -0.5
    outs = []
    for i in range(seq // SB_BLOCK):
        q0 = i * SB_BLOCK
        kend = q0 + SB_BLOCK
        qb = q[:, :, q0:kend].astype(jnp.float32)
        kb = k[:, :, :kend].astype(jnp.float32)
        z = jnp.einsum("bhqd,bhkd->bhqk", qb, kb) * scale
        t_pos = q0 + jnp.arange(SB_BLOCK)[:, None]
        s_pos = jnp.arange(kend)[None, :]
        valid = s_pos < t_pos
        log_1m = jnp.where(valid, jax.nn.log_sigmoid(-z), 0.0)
        log_a = jax.nn.log_sigmoid(z) + lax.cumsum(log_1m, axis=3, reverse=True) - log_1m
        a = jnp.where(valid, jnp.exp(log_a), 0.0)
        outs.append(jnp.einsum("bhqk,bhkd->bhqd", a.astype(v.dtype), v[:, :, :kend]))
    return jnp.concatenate(outs, axis=2)


def spatial_gating(z, ln_g, ln_b, w_s, b_s):
    bsz, seq, _ = z.shape
    u, v = jnp.split(z, 2, axis=-1)
    v = layer_norm(v, ln_g, ln_b)
    v = v.reshape(bsz, seq // SGU_LEN, SGU_LEN, SGU_GROUPS, SGU_GROUP_DIM)
    pos = jnp.arange(SGU_LEN)
    allowed = (pos[:, None] // CHUNK) >= (pos[None, :] // CHUNK)
    w = jnp.where(allowed[None], w_s, 0.0).astype(v.dtype)
    vm = jnp.einsum("gts,bnsgc->bntgc", w, v) + b_s.T[None, None, :, :, None].astype(v.dtype)
    return u * vm.reshape(bsz, seq, BRANCH_WIDTH)


def gated_short_conv(gate_b, gate_c, xin, conv_w):
    y = gate_c * xin
    ch = y.shape[-1]
    conv = lax.conv_general_dilated(
        y, conv_w[:, None, :].astype(y.dtype), window_strides=(1,),
        padding=((CONV_WIDTH - 1, 0),), dimension_numbers=("NWC", "WIO", "NWC"),
        feature_group_count=ch)
    return gate_b * conv


def hybrid_mixer(h, w_in, sgu_ln_g, sgu_ln_b, w_spatial, b_spatial, conv_w, w_branch, w_out):
    bsz, seq, _ = h.shape
    p = h @ w_in
    q, k, v, z, cb, cc, cx, gates = jnp.split(p, SPLIT_IDX, axis=-1)

    def heads(t):
        return t.reshape(bsz, seq, SB_HEADS, SB_HEAD_DIM).transpose(0, 2, 1, 3)

    ya = stick_breaking_attention(heads(q), heads(k), heads(v))
    ya = ya.transpose(0, 2, 1, 3).reshape(bsz, seq, BRANCH_WIDTH)
    yb = spatial_gating(jax.nn.gelu(z, approximate=False), sgu_ln_g, sgu_ln_b, w_spatial, b_spatial)
    yc = gated_short_conv(cb, cc, cx, conv_w)

    br = jnp.stack([ya, yb, yc], axis=2)
    br_d = jnp.einsum("bsnc,ncd->bsnd", br, w_branch)
    g = jax.nn.sigmoid(gates.reshape(bsz, seq, N_BRANCHES, D_MODEL))
    merged = jnp.sum(g * br_d, axis=2)
    return merged @ w_out


def memory_cross_attention(h, mem, mem_g, wq, wk, wv, wo):
    bsz, seq, _ = h.shape
    m = rms_norm(mem, mem_g)
    q = (h @ wq).reshape(bsz, seq, XA_HEADS, XA_HEAD_DIM)
    k = (m @ wk).reshape(bsz, MEM_TOKENS, XA_HEADS, XA_HEAD_DIM)
    v = (m @ wv).reshape(bsz, MEM_TOKENS, XA_HEADS, XA_HEAD_DIM)
    s = jnp.einsum("bqhd,bkhd->bhqk", q.astype(jnp.float32), k.astype(jnp.float32)) * (XA_HEAD_DIM ** -0.5)
    pr = jax.nn.softmax(s, axis=-1)
    o = jnp.einsum("bhqk,bkhd->bqhd", pr.astype(v.dtype), v).reshape(bsz, seq, D_MODEL)
    return o @ wo


def swiglu(h, w_gate, w_up, w_down):
    return (jax.nn.silu(h @ w_gate) * (h @ w_up)) @ w_down


def _fwd_setup_inputs(seed: int = 0) -> dict:
    key = jax.random.key(seed)
    ks = jax.random.split(key, 24)
    f32 = jnp.float32

    def nrm(k, shape, fan_in):
        return jax.random.normal(k, shape, f32) * (fan_in ** -0.5)

    def gain(k, shape):
        return 1.0 + 0.02 * jax.random.normal(k, shape, f32)

    L, D, W = DEPTH, D_MODEL, BRANCH_WIDTH
    return {
        "x": jax.random.normal(ks[0], (BATCH, SEQ, D), f32),
        "mem": jax.random.normal(ks[1], (BATCH, MEM_TOKENS, D), f32),
        "norm_mix_g": gain(ks[2], (L, D)),
        "w_in": nrm(ks[3], (L, D, IN_COLS), D),
        "sgu_ln_g": gain(ks[4], (L, W)),
        "sgu_ln_b": 0.02 * jax.random.normal(ks[5], (L, W), f32),
        "w_spatial": nrm(ks[6], (L, SGU_GROUPS, SGU_LEN, SGU_LEN), SGU_LEN),
        "b_spatial": gain(ks[7], (L, SGU_GROUPS, SGU_LEN)),
        "conv_w": nrm(ks[8], (L, CONV_WIDTH, W), CONV_WIDTH),
        "w_branch": nrm(ks[9], (L, N_BRANCHES, W, D), W),
        "w_out": nrm(ks[10], (L, D, D), D),
        "norm_xa_g": gain(ks[11], (L, D)),
        "mem_norm_g": gain(ks[12], (L, D)),
        "w_q_xa": nrm(ks[13], (L, D, D), D),
        "w_k_xa": nrm(ks[14], (L, D, D), D),
        "w_v_xa": nrm(ks[15], (L, D, D), D),
        "w_o_xa": nrm(ks[16], (L, D, D), D),
        "norm_ffn_g": gain(ks[17], (L, D)),
        "w_gate_ffn": nrm(ks[18], (L, D, FFN_HIDDEN), D),
        "w_up_ffn": nrm(ks[19], (L, D, FFN_HIDDEN), D),
        "w_down_ffn": nrm(ks[20], (L, FFN_HIDDEN, D), FFN_HIDDEN),
        "final_g": gain(ks[21], (D,)),
    }


def _fwd_reference(x, mem, norm_mix_g, w_in, sgu_ln_g, sgu_ln_b, w_spatial, b_spatial, conv_w,
              w_branch, w_out, norm_xa_g, mem_norm_g, w_q_xa, w_k_xa, w_v_xa, w_o_xa,
              norm_ffn_g, w_gate_ffn, w_up_ffn, w_down_ffn, final_g):
    for l in range(DEPTH):
        x = x + hybrid_mixer(rms_norm(x, norm_mix_g[l]), w_in[l], sgu_ln_g[l], sgu_ln_b[l],
                             w_spatial[l], b_spatial[l], conv_w[l], w_branch[l], w_out[l])
        x = x + memory_cross_attention(rms_norm(x, norm_xa_g[l]), mem, mem_norm_g[l],
                                       w_q_xa[l], w_k_xa[l], w_v_xa[l], w_o_xa[l])
        x = x + swiglu(rms_norm(x, norm_ffn_g[l]), w_gate_ffn[l], w_up_ffn[l], w_down_ffn[l])
    return rms_norm(x, final_g)


import jax as _jax
import jax.numpy as _jnp

TWIN_FORMAT = 'train_step'
FWD_PARAMS = ['x', 'mem', 'norm_mix_g', 'w_in', 'sgu_ln_g', 'sgu_ln_b', 'w_spatial', 'b_spatial', 'conv_w', 'w_branch', 'w_out', 'norm_xa_g', 'mem_norm_g', 'w_q_xa', 'w_k_xa', 'w_v_xa', 'w_o_xa', 'norm_ffn_g', 'w_gate_ffn', 'w_up_ffn', 'w_down_ffn', 'final_g']
TWIN_WEIGHTS = ['norm_mix_g', 'w_in', 'sgu_ln_g', 'sgu_ln_b', 'w_spatial', 'b_spatial', 'conv_w', 'w_branch', 'w_out', 'norm_xa_g', 'mem_norm_g', 'w_q_xa', 'w_k_xa', 'w_v_xa', 'w_o_xa', 'norm_ffn_g', 'w_gate_ffn', 'w_up_ffn', 'w_down_ffn', 'final_g']
TWIN_DIFF_INPUT = 'x'
TWIN_INPUTS = ['x', 'mem', 'norm_mix_g', 'w_in', 'sgu_ln_g', 'sgu_ln_b', 'w_spatial', 'b_spatial', 'conv_w', 'w_branch', 'w_out', 'norm_xa_g', 'mem_norm_g', 'w_q_xa', 'w_k_xa', 'w_v_xa', 'w_o_xa', 'norm_ffn_g', 'w_gate_ffn', 'w_up_ffn', 'w_down_ffn', 'final_g', 'loss_target', 'm_norm_mix_g', 'm_w_in', 'm_sgu_ln_g', 'm_sgu_ln_b', 'm_w_spatial', 'm_b_spatial', 'm_conv_w', 'm_w_branch', 'm_w_out', 'm_norm_xa_g', 'm_mem_norm_g', 'm_w_q_xa', 'm_w_k_xa', 'm_w_v_xa', 'm_w_o_xa', 'm_norm_ffn_g', 'm_w_gate_ffn', 'm_w_up_ffn', 'm_w_down_ffn', 'm_final_g', 'v_norm_mix_g', 'v_w_in', 'v_sgu_ln_g', 'v_sgu_ln_b', 'v_w_spatial', 'v_b_spatial', 'v_conv_w', 'v_w_branch', 'v_w_out', 'v_norm_xa_g', 'v_mem_norm_g', 'v_w_q_xa', 'v_w_k_xa', 'v_w_v_xa', 'v_w_o_xa', 'v_norm_ffn_g', 'v_w_gate_ffn', 'v_w_up_ffn', 'v_w_down_ffn', 'v_final_g']
TWIN_OUTPUTS = ['loss', 'grad_x', 'grad_norm_mix_g', 'grad_w_in', 'grad_sgu_ln_g', 'grad_sgu_ln_b', 'grad_w_spatial', 'grad_b_spatial', 'grad_conv_w', 'grad_w_branch', 'grad_w_out', 'grad_norm_xa_g', 'grad_mem_norm_g', 'grad_w_q_xa', 'grad_w_k_xa', 'grad_w_v_xa', 'grad_w_o_xa', 'grad_norm_ffn_g', 'grad_w_gate_ffn', 'grad_w_up_ffn', 'grad_w_down_ffn', 'grad_final_g', 'delta_norm_mix_g', 'delta_w_in', 'delta_sgu_ln_g', 'delta_sgu_ln_b', 'delta_w_spatial', 'delta_b_spatial', 'delta_conv_w', 'delta_w_branch', 'delta_w_out', 'delta_norm_xa_g', 'delta_mem_norm_g', 'delta_w_q_xa', 'delta_w_k_xa', 'delta_w_v_xa', 'delta_w_o_xa', 'delta_norm_ffn_g', 'delta_w_gate_ffn', 'delta_w_up_ffn', 'delta_w_down_ffn', 'delta_final_g', 'new_m_norm_mix_g', 'new_m_w_in', 'new_m_sgu_ln_g', 'new_m_sgu_ln_b', 'new_m_w_spatial', 'new_m_b_spatial', 'new_m_conv_w', 'new_m_w_branch', 'new_m_w_out', 'new_m_norm_xa_g', 'new_m_mem_norm_g', 'new_m_w_q_xa', 'new_m_w_k_xa', 'new_m_w_v_xa', 'new_m_w_o_xa', 'new_m_norm_ffn_g', 'new_m_w_gate_ffn', 'new_m_w_up_ffn', 'new_m_w_down_ffn', 'new_m_final_g', 'new_v_norm_mix_g', 'new_v_w_in', 'new_v_sgu_ln_g', 'new_v_sgu_ln_b', 'new_v_w_spatial', 'new_v_b_spatial', 'new_v_conv_w', 'new_v_w_branch', 'new_v_w_out', 'new_v_norm_xa_g', 'new_v_mem_norm_g', 'new_v_w_q_xa', 'new_v_w_k_xa', 'new_v_w_v_xa', 'new_v_w_o_xa', 'new_v_norm_ffn_g', 'new_v_w_gate_ffn', 'new_v_w_up_ffn', 'new_v_w_down_ffn', 'new_v_final_g']
TWIN_LEAF_KINDS = {'loss': 'loss', 'grad_x': 'grad_x', 'grad_norm_mix_g': 'grad_w', 'grad_w_in': 'grad_w', 'grad_sgu_ln_g': 'grad_w', 'grad_sgu_ln_b': 'grad_w', 'grad_w_spatial': 'grad_w', 'grad_b_spatial': 'grad_w', 'grad_conv_w': 'grad_w', 'grad_w_branch': 'grad_w', 'grad_w_out': 'grad_w', 'grad_norm_xa_g': 'grad_w', 'grad_mem_norm_g': 'grad_w', 'grad_w_q_xa': 'grad_w', 'grad_w_k_xa': 'grad_w', 'grad_w_v_xa': 'grad_w', 'grad_w_o_xa': 'grad_w', 'grad_norm_ffn_g': 'grad_w', 'grad_w_gate_ffn': 'grad_w', 'grad_w_up_ffn': 'grad_w', 'grad_w_down_ffn': 'grad_w', 'grad_final_g': 'grad_w', 'delta_norm_mix_g': 'delta_w', 'delta_w_in': 'delta_w', 'delta_sgu_ln_g': 'delta_w', 'delta_sgu_ln_b': 'delta_w', 'delta_w_spatial': 'delta_w', 'delta_b_spatial': 'delta_w', 'delta_conv_w': 'delta_w', 'delta_w_branch': 'delta_w', 'delta_w_out': 'delta_w', 'delta_norm_xa_g': 'delta_w', 'delta_mem_norm_g': 'delta_w', 'delta_w_q_xa': 'delta_w', 'delta_w_k_xa': 'delta_w', 'delta_w_v_xa': 'delta_w', 'delta_w_o_xa': 'delta_w', 'delta_norm_ffn_g': 'delta_w', 'delta_w_gate_ffn': 'delta_w', 'delta_w_up_ffn': 'delta_w', 'delta_w_down_ffn': 'delta_w', 'delta_final_g': 'delta_w', 'new_m_norm_mix_g': 'new_m', 'new_m_w_in': 'new_m', 'new_m_sgu_ln_g': 'new_m', 'new_m_sgu_ln_b': 'new_m', 'new_m_w_spatial': 'new_m', 'new_m_b_spatial': 'new_m', 'new_m_conv_w': 'new_m', 'new_m_w_branch': 'new_m', 'new_m_w_out': 'new_m', 'new_m_norm_xa_g': 'new_m', 'new_m_mem_norm_g': 'new_m', 'new_m_w_q_xa': 'new_m', 'new_m_w_k_xa': 'new_m', 'new_m_w_v_xa': 'new_m', 'new_m_w_o_xa': 'new_m', 'new_m_norm_ffn_g': 'new_m', 'new_m_w_gate_ffn': 'new_m', 'new_m_w_up_ffn': 'new_m', 'new_m_w_down_ffn': 'new_m', 'new_m_final_g': 'new_m', 'new_v_norm_mix_g': 'new_v', 'new_v_w_in': 'new_v', 'new_v_sgu_ln_g': 'new_v', 'new_v_sgu_ln_b': 'new_v', 'new_v_w_spatial': 'new_v', 'new_v_b_spatial': 'new_v', 'new_v_conv_w': 'new_v', 'new_v_w_branch': 'new_v', 'new_v_w_out': 'new_v', 'new_v_norm_xa_g': 'new_v', 'new_v_mem_norm_g': 'new_v', 'new_v_w_q_xa': 'new_v', 'new_v_w_k_xa': 'new_v', 'new_v_w_v_xa': 'new_v', 'new_v_w_o_xa': 'new_v', 'new_v_norm_ffn_g': 'new_v', 'new_v_w_gate_ffn': 'new_v', 'new_v_w_up_ffn': 'new_v', 'new_v_w_down_ffn': 'new_v', 'new_v_final_g': 'new_v'}


def _forward(args):
    return _fwd_reference(*[args[k] for k in FWD_PARAMS])


def _output_shape():
    def fwd():
        inp = _fwd_setup_inputs(0)
        return _fwd_reference(*[inp[k] for k in FWD_PARAMS])
    out = _jax.eval_shape(fwd)
    return out.shape, out.dtype

N_MICROBATCH = 1
ADAM_LR = 0.001
ADAM_B1 = 0.9
ADAM_B2 = 0.999
ADAM_EPS = 1e-08
ADAM_WD = 0.01
ADAM_STEP = 10
PER_EXAMPLE_BATCH_AXIS = {'x': 0, 'mem': 0, 'loss_target': 0}
SHARED_INPUTS = []
_WEIGHT_DTYPES = {'norm_mix_g': _jnp.float32, 'w_in': _jnp.float32, 'sgu_ln_g': _jnp.float32, 'sgu_ln_b': _jnp.float32, 'w_spatial': _jnp.float32, 'b_spatial': _jnp.float32, 'conv_w': _jnp.float32, 'w_branch': _jnp.float32, 'w_out': _jnp.float32, 'norm_xa_g': _jnp.float32, 'mem_norm_g': _jnp.float32, 'w_q_xa': _jnp.float32, 'w_k_xa': _jnp.float32, 'w_v_xa': _jnp.float32, 'w_o_xa': _jnp.float32, 'norm_ffn_g': _jnp.float32, 'w_gate_ffn': _jnp.float32, 'w_up_ffn': _jnp.float32, 'w_down_ffn': _jnp.float32, 'final_g': _jnp.float32}
MOMENT_SCALE = {'norm_mix_g': 1.904502e-01, 'w_in': 7.145218e-02, 'sgu_ln_g': 6.891391e-02, 'sgu_ln_b': 7.058223e-02, 'w_spatial': 6.684805e-02, 'b_spatial': 7.851557e-02, 'conv_w': 1.188682e-01, 'w_branch': 7.137495e-02, 'w_out': 1.236690e-01, 'norm_xa_g': 1.625921e-02, 'mem_norm_g': 2.481882e-02, 'w_q_xa': 1.634812e-02, 'w_k_xa': 1.638761e-02, 'w_v_xa': 1.675770e-02, 'w_o_xa': 1.684981e-02, 'norm_ffn_g': 1.153874e-01, 'w_gate_ffn': 4.877584e-02, 'w_up_ffn': 4.734614e-02, 'w_down_ffn': 7.857330e-02, 'final_g': 3.202451e+01}


def _to_microbatches(a, axis):
    t = _jnp.moveaxis(a, axis, 0)
    t = t.reshape((N_MICROBATCH, t.shape[0] // N_MICROBATCH) + t.shape[1:])
    return _jnp.moveaxis(t, 1, axis + 1)


def setup_inputs(seed: int = 0) -> dict:
    inp = _fwd_setup_inputs(seed)
    key = _jax.random.fold_in(_jax.random.key(seed), 7919)
    shape, _ = _output_shape()
    out = dict(inp)
    out["loss_target"] = _jax.random.normal(_jax.random.fold_in(key, 0), shape, _jnp.float32)
    for i, name in enumerate(TWIN_WEIGHTS):
        w = inp[name].astype(_jnp.float32)
        if MOMENT_SCALE is None:
            s = _jnp.sqrt(_jnp.mean(_jnp.square(w)) + 1e-30)
        else:
            s = MOMENT_SCALE[name]
        km, kv = _jax.random.split(_jax.random.fold_in(key, i + 1))
        out[name] = w
        out["m_" + name] = s * _jax.random.normal(km, w.shape, _jnp.float32)
        out["v_" + name] = (s * s) * _jax.random.uniform(kv, w.shape, _jnp.float32, 0.5, 1.5)
    if N_MICROBATCH > 1:
        for name, axis in PER_EXAMPLE_BATCH_AXIS.items():
            out[name] = _to_microbatches(out[name], axis)
    return {'x': out['x'], 'mem': out['mem'], 'norm_mix_g': out['norm_mix_g'], 'w_in': out['w_in'], 'sgu_ln_g': out['sgu_ln_g'], 'sgu_ln_b': out['sgu_ln_b'], 'w_spatial': out['w_spatial'], 'b_spatial': out['b_spatial'], 'conv_w': out['conv_w'], 'w_branch': out['w_branch'], 'w_out': out['w_out'], 'norm_xa_g': out['norm_xa_g'], 'mem_norm_g': out['mem_norm_g'], 'w_q_xa': out['w_q_xa'], 'w_k_xa': out['w_k_xa'], 'w_v_xa': out['w_v_xa'], 'w_o_xa': out['w_o_xa'], 'norm_ffn_g': out['norm_ffn_g'], 'w_gate_ffn': out['w_gate_ffn'], 'w_up_ffn': out['w_up_ffn'], 'w_down_ffn': out['w_down_ffn'], 'final_g': out['final_g'], 'loss_target': out['loss_target'], 'm_norm_mix_g': out['m_norm_mix_g'], 'm_w_in': out['m_w_in'], 'm_sgu_ln_g': out['m_sgu_ln_g'], 'm_sgu_ln_b': out['m_sgu_ln_b'], 'm_w_spatial': out['m_w_spatial'], 'm_b_spatial': out['m_b_spatial'], 'm_conv_w': out['m_conv_w'], 'm_w_branch': out['m_w_branch'], 'm_w_out': out['m_w_out'], 'm_norm_xa_g': out['m_norm_xa_g'], 'm_mem_norm_g': out['m_mem_norm_g'], 'm_w_q_xa': out['m_w_q_xa'], 'm_w_k_xa': out['m_w_k_xa'], 'm_w_v_xa': out['m_w_v_xa'], 'm_w_o_xa': out['m_w_o_xa'], 'm_norm_ffn_g': out['m_norm_ffn_g'], 'm_w_gate_ffn': out['m_w_gate_ffn'], 'm_w_up_ffn': out['m_w_up_ffn'], 'm_w_down_ffn': out['m_w_down_ffn'], 'm_final_g': out['m_final_g'], 'v_norm_mix_g': out['v_norm_mix_g'], 'v_w_in': out['v_w_in'], 'v_sgu_ln_g': out['v_sgu_ln_g'], 'v_sgu_ln_b': out['v_sgu_ln_b'], 'v_w_spatial': out['v_w_spatial'], 'v_b_spatial': out['v_b_spatial'], 'v_conv_w': out['v_conv_w'], 'v_w_branch': out['v_w_branch'], 'v_w_out': out['v_w_out'], 'v_norm_xa_g': out['v_norm_xa_g'], 'v_mem_norm_g': out['v_mem_norm_g'], 'v_w_q_xa': out['v_w_q_xa'], 'v_w_k_xa': out['v_w_k_xa'], 'v_w_v_xa': out['v_w_v_xa'], 'v_w_o_xa': out['v_w_o_xa'], 'v_norm_ffn_g': out['v_norm_ffn_g'], 'v_w_gate_ffn': out['v_w_gate_ffn'], 'v_w_up_ffn': out['v_w_up_ffn'], 'v_w_down_ffn': out['v_w_down_ffn'], 'v_final_g': out['v_final_g']}


def _loss(weights, diff, rest, loss_target):
    with _jax.named_scope("forward"):
        args = {**rest, TWIN_DIFF_INPUT: diff, **{k: w.astype(_WEIGHT_DTYPES[k]) for k, w in weights.items()}}
        y = _forward(args)
    with _jax.named_scope("loss_head"):
        err = _jnp.square(y.astype(_jnp.float32) - loss_target)
        return 0.5 * _jnp.sum(_jnp.mean(err, axis=-1)) if err.ndim else 0.5 * err


def _adamw(w, g, m, v):
    m = ADAM_B1 * m + (1.0 - ADAM_B1) * g
    v = ADAM_B2 * v + (1.0 - ADAM_B2) * _jnp.square(g)
    m_hat = m / (1.0 - ADAM_B1 ** ADAM_STEP)
    v_hat = v / (1.0 - ADAM_B2 ** ADAM_STEP)
    delta = -ADAM_LR * (m_hat / (_jnp.sqrt(v_hat) + ADAM_EPS) + ADAM_WD * w)
    return delta, m, v


def reference(x, mem, norm_mix_g, w_in, sgu_ln_g, sgu_ln_b, w_spatial, b_spatial, conv_w, w_branch, w_out, norm_xa_g, mem_norm_g, w_q_xa, w_k_xa, w_v_xa, w_o_xa, norm_ffn_g, w_gate_ffn, w_up_ffn, w_down_ffn, final_g, loss_target, m_norm_mix_g, m_w_in, m_sgu_ln_g, m_sgu_ln_b, m_w_spatial, m_b_spatial, m_conv_w, m_w_branch, m_w_out, m_norm_xa_g, m_mem_norm_g, m_w_q_xa, m_w_k_xa, m_w_v_xa, m_w_o_xa, m_norm_ffn_g, m_w_gate_ffn, m_w_up_ffn, m_w_down_ffn, m_final_g, v_norm_mix_g, v_w_in, v_sgu_ln_g, v_sgu_ln_b, v_w_spatial, v_b_spatial, v_conv_w, v_w_branch, v_w_out, v_norm_xa_g, v_mem_norm_g, v_w_q_xa, v_w_k_xa, v_w_v_xa, v_w_o_xa, v_norm_ffn_g, v_w_gate_ffn, v_w_up_ffn, v_w_down_ffn, v_final_g):
    given = dict(x=x, mem=mem, norm_mix_g=norm_mix_g, w_in=w_in, sgu_ln_g=sgu_ln_g, sgu_ln_b=sgu_ln_b, w_spatial=w_spatial, b_spatial=b_spatial, conv_w=conv_w, w_branch=w_branch, w_out=w_out, norm_xa_g=norm_xa_g, mem_norm_g=mem_norm_g, w_q_xa=w_q_xa, w_k_xa=w_k_xa, w_v_xa=w_v_xa, w_o_xa=w_o_xa, norm_ffn_g=norm_ffn_g, w_gate_ffn=w_gate_ffn, w_up_ffn=w_up_ffn, w_down_ffn=w_down_ffn, final_g=final_g, loss_target=loss_target, m_norm_mix_g=m_norm_mix_g, m_w_in=m_w_in, m_sgu_ln_g=m_sgu_ln_g, m_sgu_ln_b=m_sgu_ln_b, m_w_spatial=m_w_spatial, m_b_spatial=m_b_spatial, m_conv_w=m_conv_w, m_w_branch=m_w_branch, m_w_out=m_w_out, m_norm_xa_g=m_norm_xa_g, m_mem_norm_g=m_mem_norm_g, m_w_q_xa=m_w_q_xa, m_w_k_xa=m_w_k_xa, m_w_v_xa=m_w_v_xa, m_w_o_xa=m_w_o_xa, m_norm_ffn_g=m_norm_ffn_g, m_w_gate_ffn=m_w_gate_ffn, m_w_up_ffn=m_w_up_ffn, m_w_down_ffn=m_w_down_ffn, m_final_g=m_final_g, v_norm_mix_g=v_norm_mix_g, v_w_in=v_w_in, v_sgu_ln_g=v_sgu_ln_g, v_sgu_ln_b=v_sgu_ln_b, v_w_spatial=v_w_spatial, v_b_spatial=v_b_spatial, v_conv_w=v_conv_w, v_w_branch=v_w_branch, v_w_out=v_w_out, v_norm_xa_g=v_norm_xa_g, v_mem_norm_g=v_mem_norm_g, v_w_q_xa=v_w_q_xa, v_w_k_xa=v_w_k_xa, v_w_v_xa=v_w_v_xa, v_w_o_xa=v_w_o_xa, v_norm_ffn_g=v_norm_ffn_g, v_w_gate_ffn=v_w_gate_ffn, v_w_up_ffn=v_w_up_ffn, v_w_down_ffn=v_w_down_ffn, v_final_g=v_final_g)
    weights = {n: given[n] for n in TWIN_WEIGHTS}
    shared = {n: given[n] for n in SHARED_INPUTS}
    per_example = {n: given[n] for n in ['x', 'mem']}
    grad_fn = _jax.value_and_grad(_loss, argnums=(0, 1))

    def one_microbatch(ex, loss_target):
        ex = dict(ex)
        diff = ex.pop(TWIN_DIFF_INPUT)
        return grad_fn(weights, diff, {**shared, **ex}, loss_target)

    if N_MICROBATCH == 1:
        loss, (grad_w, grad_x) = one_microbatch(per_example, given["loss_target"])
    else:
        def body(carry, xs):
            loss_sum, grad_sum = carry
            l_k, (gw_k, gx_k) = one_microbatch(xs[0], xs[1])
            with _jax.named_scope("update"):
                return (loss_sum + l_k, _jax.tree.map(_jnp.add, grad_sum, gw_k)), gx_k

        init = (_jnp.zeros((), _jnp.float32), _jax.tree.map(_jnp.zeros_like, weights))
        (loss, grad_w), grad_x = _jax.lax.scan(body, init, (per_example, given["loss_target"]))
    with _jax.named_scope("update"):
        delta_w, new_m, new_v = {}, {}, {}
        for n in TWIN_WEIGHTS:
            delta_w[n], new_m[n], new_v[n] = _adamw(weights[n], grad_w[n], given["m_" + n], given["v_" + n])
    return (loss, grad_x, *[grad_w[n] for n in TWIN_WEIGHTS], *[delta_w[n] for n in TWIN_WEIGHTS],
            *[new_m[n] for n in TWIN_WEIGHTS], *[new_v[n] for n in TWIN_WEIGHTS])
```

```python
import functools

import jax
import jax.numpy as jnp
from jax import lax
from jax.experimental import pallas as pl
from jax.experimental.pallas import tpu as pltpu

F32 = jnp.float32
BF16 = jnp.bfloat16
MESH = pl.DeviceIdType.MESH

D_MODEL = 1024
BRANCH_W = 512
IN_COLS = 7168
FFN = 2816
N_DEV = 8
DEPTH = 2
SB_BLOCK = 128
SB_SCALE = 0.125
XA_HEAD = 256
XA_SCALE = 0.0625
SGU_LEN = 128
SGU_GROUPS = 4
RMS_EPS = 1e-6
LN_EPS = 1e-5
HALO = 8

ADAM_LR = 0.001
ADAM_B1 = 0.9
ADAM_B2 = 0.999
ADAM_EPS = 1e-08
ADAM_WD = 0.01
ADAM_STEP = 10

VMEM_LIMIT_BYTES = 52 * 1024 * 1024

AXES = ("x", "y", "c")


def _cparams(*sem):
    return pltpu.CompilerParams(dimension_semantics=sem, vmem_limit_bytes=VMEM_LIMIT_BYTES)


def _pick(n, target, align):
    t = (min(target, n) // align) * align
    while t >= align:
        if n % t == 0:
            return t
        t -= align
    return n


def _dot(a, b):
    return jnp.dot(a, b, preferred_element_type=F32)


def _dot_nt(a, b):
    return lax.dot_general(a, b, (((1,), (1,)), ((), ())), preferred_element_type=F32)


def _dot_tn(a, b):
    return lax.dot_general(a, b, (((0,), (0,)), ((), ())), preferred_element_type=F32)


def _sigmoid(x):
    return 1.0 / (1.0 + jnp.exp(-x))


def _mm(a, b, *, name, ta=False, tb=False, out_dtype=F32, add=None, tm=512, tn=512, tk=2048):
    m, k = (a.shape[1], a.shape[0]) if ta else a.shape
    n = b.shape[0] if tb else b.shape[1]
    assert k == (b.shape[1] if tb else b.shape[0])
    tm = _pick(m, tm, 128)
    tn = _pick(n, tn, 128)
    tk = _pick(k, tk, 128)
    nk = k // tk
    ca = 0 if ta else 1
    cb = 1 if tb else 0

    def body(*refs):
        if add is None:
            a_ref, b_ref, o_ref, acc_ref = refs
            add_ref = None
        else:
            a_ref, b_ref, add_ref, o_ref, acc_ref = refs
        kk = pl.program_id(2)

        @pl.when(kk == 0)
        def _():
            acc_ref[...] = jnp.zeros_like(acc_ref)

        acc_ref[...] += lax.dot_general(
            a_ref[...].astype(BF16), b_ref[...].astype(BF16),
            (((ca,), (cb,)), ((), ())), preferred_element_type=F32)

        @pl.when(kk == nk - 1)
        def _():
            r = acc_ref[...]
            if add_ref is not None:
                r = r + add_ref[...]
            o_ref[...] = r.astype(out_dtype)

    a_spec = pl.BlockSpec((tk, tm), lambda i, j, kk: (kk, i)) if ta else pl.BlockSpec((tm, tk), lambda i, j, kk: (i, kk))
    b_spec = pl.BlockSpec((tn, tk), lambda i, j, kk: (j, kk)) if tb else pl.BlockSpec((tk, tn), lambda i, j, kk: (kk, j))
    in_specs = [a_spec, b_spec]
    operands = [a, b]
    if add is not None:
        in_specs.append(pl.BlockSpec((tm, tn), lambda i, j, kk: (i, j)))
        operands.append(add)
    return pl.pallas_call(
        body, name=name,
        grid=(m // tm, n // tn, nk),
        in_specs=in_specs,
        out_specs=pl.BlockSpec((tm, tn), lambda i, j, kk: (i, j)),
        out_shape=jax.ShapeDtypeStruct((m, n), out_dtype),
        scratch_shapes=[pltpu.VMEM((tm, tn), F32)],
        compiler_params=_cparams("parallel", "parallel", "arbitrary"),
    )(*operands)


def _rms_fwd(x, g, *, name):
    r, d = x.shape
    tr = _pick(r, 512, 16)

    def body(x_ref, g_ref, o_ref):
        xv = x_ref[...]
        rs = lax.rsqrt(jnp.mean(xv * xv, axis=-1, keepdims=True) + RMS_EPS)
        o_ref[...] = (xv * rs * g_ref[...]).astype(BF16)

    return pl.pallas_call(
        body, name=name, grid=(r // tr,),
        in_specs=[pl.BlockSpec((tr, d), lambda i: (i, 0)), pl.BlockSpec((1, d), lambda i: (0, 0))],
        out_specs=pl.BlockSpec((tr, d), lambda i: (i, 0)),
        out_shape=jax.ShapeDtypeStruct((r, d), BF16),
        compiler_params=_cparams("parallel"),
    )(x, g)


def _rms_bwd(x, g, dh, dres, *, name):
    r, d = x.shape
    tr = _pick(r, 256, 8)

    def body(x_ref, g_ref, dh_ref, dres_ref, dx_ref, dg_ref):
        @pl.when(pl.program_id(0) == 0)
        def _():
            dg_ref[...] = jnp.zeros_like(dg_ref)

        xv = x_ref[...]
        dhv = dh_ref[...].astype(F32)
        rs = lax.rsqrt(jnp.mean(xv * xv, axis=-1, keepdims=True) + RMS_EPS)
        xh = xv * rs
        dg_ref[...] += jnp.sum(dhv * xh, axis=0, keepdims=True)
        dxh = dhv * g_ref[...]
        dx_ref[...] = dres_ref[...] + rs * (dxh - xh * jnp.mean(dxh * xh, axis=-1, keepdims=True))

    return pl.pallas_call(
        body, name=name, grid=(r // tr,),
        in_specs=[pl.BlockSpec((tr, d), lambda i: (i, 0)), pl.BlockSpec((1, d), lambda i: (0, 0)),
                  pl.BlockSpec((tr, d), lambda i: (i, 0)), pl.BlockSpec((tr, d), lambda i: (i, 0))],
        out_specs=[pl.BlockSpec((tr, d), lambda i: (i, 0)), pl.BlockSpec((1, d), lambda i: (0, 0))],
        out_shape=[jax.ShapeDtypeStruct((r, d), F32), jax.ShapeDtypeStruct((1, d), F32)],
        compiler_params=_cparams("arbitrary"),
    )(x, g, dh, dres)


def _final_loss(x, g, target, *, name):
    r, d = x.shape
    tr = _pick(r, 256, 8)

    def body(x_ref, g_ref, t_ref, dx_ref, loss_ref, dg_ref):
        @pl.when(pl.program_id(0) == 0)
        def _():
            dg_ref[...] = jnp.zeros_like(dg_ref)
            loss_ref[...] = jnp.zeros_like(loss_ref)

        xv = x_ref[...]
        gv = g_ref[...]
        rs = lax.rsqrt(jnp.mean(xv * xv, axis=-1, keepdims=True) + RMS_EPS)
        xh = xv * rs
        err = xh * gv - t_ref[...]
        row_loss = jnp.mean(err * err, axis=-1, keepdims=True)
        loss_ref[...] += 0.5 * jnp.sum(row_loss, axis=0, keepdims=True)
        dy = err * (1.0 / d)
        dg_ref[...] += jnp.sum(dy * xh, axis=0, keepdims=True)
        dxh = dy * gv
        dx_ref[...] = rs * (dxh - xh * jnp.mean(dxh * xh, axis=-1, keepdims=True))

    return pl.pallas_call(
        body, name=name, grid=(r // tr,),
        in_specs=[pl.BlockSpec((tr, d), lambda i: (i, 0)), pl.BlockSpec((1, d), lambda i: (0, 0)),
                  pl.BlockSpec((tr, d), lambda i: (i, 0))],
        out_specs=[pl.BlockSpec((tr, d), lambda i: (i, 0)), pl.BlockSpec((1, 128), lambda i: (0, 0)),
                   pl.BlockSpec((1, d), lambda i: (0, 0))],
        out_shape=[jax.ShapeDtypeStruct((r, d), F32), jax.ShapeDtypeStruct((1, 128), F32),
                   jax.ShapeDtypeStruct((1, d), F32)],
        compiler_params=_cparams("arbitrary"),
    )(x, g, target)


def _cumsum_operand(strict_after):
    r = lax.broadcasted_iota(jnp.int32, (SB_BLOCK, 2 * SB_BLOCK), 0)
    c = lax.broadcasted_iota(jnp.int32, (SB_BLOCK, 2 * SB_BLOCK), 1)
    tri = (r > c) if strict_after else (r < c)
    return jnp.where((c >= SB_BLOCK) | tri, 1.0, 0.0).astype(BF16)


def _sb_scores(qh, kj, run, tri, after_ones):
    z = _dot_nt(qh, kj)
    lsp = jnp.minimum(z, 0.0) - jnp.log(1.0 + jnp.exp(-jnp.abs(z)))
    l1m = lsp - z
    if tri is not None:
        l1m = jnp.where(tri, l1m, 0.0)
    hi = l1m.astype(BF16)
    lo = (l1m - hi.astype(F32)).astype(BF16)
    ct = _dot(hi, after_ones) + _dot(lo, after_ones)
    a = jnp.exp(lsp + run + ct[:, :SB_BLOCK])
    if tri is not None:
        a = jnp.where(tri, a, 0.0)
    return lsp, a, run + ct[:, SB_BLOCK:]


def _sb_fwd(p, *, name):
    s = p.shape[0]
    nq = s // SB_BLOCK
    kcol = BRANCH_W // SB_BLOCK

    def body(q_ref, k_ref, v_ref, o_ref):
        qi = pl.program_id(1)
        lane = lax.broadcasted_iota(jnp.int32, (SB_BLOCK, SB_BLOCK), 1)
        row = lax.broadcasted_iota(jnp.int32, (SB_BLOCK, SB_BLOCK), 0)
        tri = lane < row
        after_ones = _cumsum_operand(True)
        q = q_ref[...] * SB_SCALE
        zero = jnp.zeros((SB_BLOCK, SB_BLOCK), F32)
        total = zero
        for h in range(2):
            hm = (lane >= 64 * h) & (lane < 64 * (h + 1))
            qh = jnp.where(hm, q, 0.0).astype(BF16)

            def tile(j, run, acc, tri_mask):
                rows = pl.ds(pl.multiple_of(j * SB_BLOCK, SB_BLOCK), SB_BLOCK)
                kj = k_ref[rows, :].astype(BF16)
                vj = jnp.where(hm, v_ref[rows, :], 0.0).astype(BF16)
                _, a, run = _sb_scores(qh, kj, run, tri_mask, after_ones)
                return run, acc + _dot(a.astype(BF16), vj)

            run, acc = tile(qi, zero, zero, tri)

            def step(t, carry):
                return tile(qi - 1 - t, carry[0], carry[1], None)

            run, acc = lax.fori_loop(0, qi, step, (run, acc))
            total = total + acc
        o_ref[...] = total

    return pl.pallas_call(
        body, name=name, grid=(kcol, nq),
        in_specs=[pl.BlockSpec((SB_BLOCK, SB_BLOCK), lambda hp, qi: (qi, hp)),
                  pl.BlockSpec((s, SB_BLOCK), lambda hp, qi: (0, kcol + hp)),
                  pl.BlockSpec((s, SB_BLOCK), lambda hp, qi: (0, 2 * kcol + hp))],
        out_specs=pl.BlockSpec((SB_BLOCK, SB_BLOCK), lambda hp, qi: (qi, hp)),
        out_shape=jax.ShapeDtypeStruct((s, BRANCH_W), F32),
        compiler_params=_cparams("parallel", "arbitrary"),
    )(p, p, p)


def _sb_bwd(p, dya, *, name):
    s = p.shape[0]
    nq = s // SB_BLOCK
    kcol = BRANCH_W // SB_BLOCK

    def body(q_ref, k_ref, v_ref, do_ref, dq_ref, dk_ref, dv_ref, a_s, b_s, dk_acc, dv_acc):
        qi = pl.program_id(1)

        @pl.when(qi == 0)
        def _():
            dk_acc[...] = jnp.zeros_like(dk_acc)
            dv_acc[...] = jnp.zeros_like(dv_acc)

        lane = lax.broadcasted_iota(jnp.int32, (SB_BLOCK, SB_BLOCK), 1)
        row = lax.broadcasted_iota(jnp.int32, (SB_BLOCK, SB_BLOCK), 0)
        tri = lane < row
        after_ones = _cumsum_operand(True)
        before_ones = _cumsum_operand(False)
        q = q_ref[...] * SB_SCALE
        do = do_ref[...]
        zero = jnp.zeros((SB_BLOCK, SB_BLOCK), F32)
        dq_total = zero
        for h in range(2):
            hm = (lane >= 64 * h) & (lane < 64 * (h + 1))
            qh = jnp.where(hm, q, 0.0).astype(BF16)
            doh = jnp.where(hm, do, 0.0).astype(BF16)

            def rebuild(j, run, tri_mask):
                rows = pl.ds(pl.multiple_of(j * SB_BLOCK, SB_BLOCK), SB_BLOCK)
                kj = k_ref[rows, :].astype(BF16)
                lsp, a, run = _sb_scores(qh, kj, run, tri_mask, after_ones)
                beta = jnp.exp(lsp)
                if tri_mask is not None:
                    beta = jnp.where(tri_mask, beta, 0.0)
                a_s[j] = a
                b_s[j] = beta
                return run

            run = rebuild(qi, zero, tri)
            lax.fori_loop(0, qi, lambda t, r: rebuild(qi - 1 - t, r, None), run)

            def accumulate(j, carry):
                pg, dq = carry
                rows = pl.ds(pl.multiple_of(j * SB_BLOCK, SB_BLOCK), SB_BLOCK)
                a = a_s[j]
                beta = b_s[j]
                kj = jnp.where(hm, k_ref[rows, :], 0.0).astype(BF16)
                vj = jnp.where(hm, v_ref[rows, :], 0.0).astype(BF16)
                g = a * _dot_nt(doh, vj)
                gt = _dot(g.astype(BF16), before_ones)
                dz = (g * (1.0 - beta) - beta * (pg + gt[:, :SB_BLOCK])).astype(BF16)
                dk_acc[rows, :] += _dot_tn(dz, qh)
                dv_acc[rows, :] += _dot_tn(a.astype(BF16), doh)
                return pg + gt[:, SB_BLOCK:], dq + _dot(dz, kj)

            _, dq = lax.fori_loop(0, qi + 1, accumulate, (zero, zero))
            dq_total = dq_total + dq
        dq_ref[...] = (dq_total * SB_SCALE).astype(BF16)

        @pl.when(qi == nq - 1)
        def _():
            dk_ref[...] = dk_acc[...].astype(BF16)
            dv_ref[...] = dv_acc[...].astype(BF16)

    blk = pl.BlockSpec((SB_BLOCK, SB_BLOCK), lambda hp, qi: (qi, hp))
    col = pl.BlockSpec((s, SB_BLOCK), lambda hp, qi: (0, hp))
    out = jax.ShapeDtypeStruct((s, BRANCH_W), BF16)
    return pl.pallas_call(
        body, name=name, grid=(kcol, nq),
        in_specs=[blk,
                  pl.BlockSpec((s, SB_BLOCK), lambda hp, qi: (0, kcol + hp)),
                  pl.BlockSpec((s, SB_BLOCK), lambda hp, qi: (0, 2 * kcol + hp)),
                  blk],
        out_specs=[blk, col, col],
        out_shape=[out, out, out],
        scratch_shapes=[pltpu.VMEM((nq, SB_BLOCK, SB_BLOCK), F32), pltpu.VMEM((nq, SB_BLOCK, SB_BLOCK), F32),
                        pltpu.VMEM((s, SB_BLOCK), F32), pltpu.VMEM((s, SB_BLOCK), F32)],
        compiler_params=_cparams("parallel", "arbitrary"),
    )(p, p, p, dya)


_INV_SQRT2 = 0.7071067811865476
_INV_SQRT2PI = 0.3989422804014327


def _gelu(x):
    return 0.5 * x * (1.0 + lax.erf(x * _INV_SQRT2))


def _gelu_grad(x):
    return 0.5 * (1.0 + lax.erf(x * _INV_SQRT2)) + x * _INV_SQRT2PI * jnp.exp(-0.5 * x * x)


def _chunk_mask(transposed=False):
    r = lax.broadcasted_iota(jnp.int32, (SGU_LEN, SGU_LEN), 0)
    c = lax.broadcasted_iota(jnp.int32, (SGU_LEN, SGU_LEN), 1)
    return (c // 64) >= (r // 64) if transposed else (r // 64) >= (c // 64)


def _sgu_norm(v_raw, g, b):
    zv = _gelu(v_raw)
    xc = zv - jnp.mean(zv, axis=-1, keepdims=True)
    rs = lax.rsqrt(jnp.mean(xc * xc, axis=-1, keepdims=True) + LN_EPS)
    xh = xc * rs
    return xh, rs, xh * g + b


def _sgu_fwd(p, ln_g, ln_b, w, b_col, *, name):
    s = p.shape[0]
    tr = _pick(s, 512, SGU_LEN)

    def body(u_ref, v_ref, g_ref, b_ref, w_ref, bc_ref, o_ref):
        mask = _chunk_mask()
        zu = _gelu(u_ref[...])
        _, _, vn = _sgu_norm(v_ref[...], g_ref[...], b_ref[...])
        vnb = vn.astype(BF16)
        for gi in range(SGU_GROUPS):
            wg = jnp.where(mask, w_ref[gi], 0.0).astype(BF16)
            cs = slice(gi * SGU_LEN, (gi + 1) * SGU_LEN)
            for c in range(tr // SGU_LEN):
                rs_ = slice(c * SGU_LEN, (c + 1) * SGU_LEN)
                vm = _dot(wg, vnb[rs_, cs]) + bc_ref[gi]
                o_ref[rs_, cs] = zu[rs_, cs] * vm

    vec = pl.BlockSpec((1, BRANCH_W), lambda i: (0, 0))
    return pl.pallas_call(
        body, name=name, grid=(s // tr,),
        in_specs=[pl.BlockSpec((tr, BRANCH_W), lambda i: (i, 3)), pl.BlockSpec((tr, BRANCH_W), lambda i: (i, 4)),
                  vec, vec,
                  pl.BlockSpec((SGU_GROUPS, SGU_LEN, SGU_LEN), lambda i: (0, 0, 0)),
                  pl.BlockSpec((SGU_GROUPS, SGU_LEN, 1), lambda i: (0, 0, 0))],
        out_specs=pl.BlockSpec((tr, BRANCH_W), lambda i: (i, 0)),
        out_shape=jax.ShapeDtypeStruct((s, BRANCH_W), F32),
        compiler_params=_cparams("parallel"),
    )(p, p, ln_g, ln_b, w, b_col)


def _sgu_bwd(p, dyb, ln_g, ln_b, w, w_t, b_col, *, name):
    s = p.shape[0]
    tr = _pick(s, 256, SGU_LEN)

    def body(u_ref, v_ref, dy_ref, g_ref, b_ref, w_ref, wt_ref, bc_ref,
             dz_ref, dg_ref, db_ref, dw_ref, dbc_ref, dvn_s):
        @pl.when(pl.program_id(0) == 0)
        def _():
            dg_ref[...] = jnp.zeros_like(dg_ref)
            db_ref[...] = jnp.zeros_like(db_ref)
            dw_ref[...] = jnp.zeros_like(dw_ref)
            dbc_ref[...] = jnp.zeros_like(dbc_ref)

        mask = _chunk_mask()
        mask_t = _chunk_mask(transposed=True)
        u_raw = u_ref[...]
        v_raw = v_ref[...]
        dy = dy_ref[...]
        zu = _gelu(u_raw)
        xh, rs, vn = _sgu_norm(v_raw, g_ref[...], b_ref[...])
        vnb = vn.astype(BF16)
        dvm_all = dy * zu
        for gi in range(SGU_GROUPS):
            wg = jnp.where(mask, w_ref[gi], 0.0).astype(BF16)
            wgt = jnp.where(mask_t, wt_ref[gi], 0.0).astype(BF16)
            cs = slice(gi * SGU_LEN, (gi + 1) * SGU_LEN)
            dw_g = jnp.zeros((SGU_LEN, SGU_LEN), F32)
            db_g = jnp.zeros((SGU_LEN, 1), F32)
            for c in range(tr // SGU_LEN):
                rs_ = slice(c * SGU_LEN, (c + 1) * SGU_LEN)
                vm = _dot(wg, vnb[rs_, cs]) + bc_ref[gi]
                dz_ref[rs_, cs] = (dy[rs_, cs] * vm * _gelu_grad(u_raw[rs_, cs])).astype(BF16)
                dvm = dvm_all[rs_, cs]
                dvmb = dvm.astype(BF16)
                dw_g = dw_g + _dot_nt(dvmb, vnb[rs_, cs])
                db_g = db_g + jnp.sum(dvm, axis=1, keepdims=True)
                dvn_s[rs_, cs] = _dot(wgt, dvmb)
            dw_ref[gi] += jnp.where(mask, dw_g, 0.0)
            dbc_ref[gi] += db_g
        dvn = dvn_s[...]
        dg_ref[...] += jnp.sum(dvn * xh, axis=0, keepdims=True)
        db_ref[...] += jnp.sum(dvn, axis=0, keepdims=True)
        dxh = dvn * g_ref[...]
        dzv = rs * (dxh - jnp.mean(dxh, axis=-1, keepdims=True) - xh * jnp.mean(dxh * xh, axis=-1, keepdims=True))
        dz_ref[:, BRANCH_W:] = (dzv * _gelu_grad(v_raw)).astype(BF16)

    vec = pl.BlockSpec((1, BRANCH_W), lambda i: (0, 0))
    wspec = pl.BlockSpec((SGU_GROUPS, SGU_LEN, SGU_LEN), lambda i: (0, 0, 0))
    bspec = pl.BlockSpec((SGU_GROUPS, SGU_LEN, 1), lambda i: (0, 0, 0))
    return pl.pallas_call(
        body, name=name, grid=(s // tr,),
        in_specs=[pl.BlockSpec((tr, BRANCH_W), lambda i: (i, 3)), pl.BlockSpec((tr, BRANCH_W), lambda i: (i, 4)),
                  pl.BlockSpec((tr, BRANCH_W), lambda i: (i, 0)), vec, vec, wspec, wspec, bspec],
        out_specs=[pl.BlockSpec((tr, 2 * BRANCH_W), lambda i: (i, 0)), vec, vec, wspec, bspec],
        out_shape=[jax.ShapeDtypeStruct((s, 2 * BRANCH_W), BF16),
                   jax.ShapeDtypeStruct((1, BRANCH_W), F32), jax.ShapeDtypeStruct((1, BRANCH_W), F32),
                   jax.ShapeDtypeStruct((SGU_GROUPS, SGU_LEN, SGU_LEN), F32),
                   jax.ShapeDtypeStruct((SGU_GROUPS, SGU_LEN, 1), F32)],
        scratch_shapes=[pltpu.VMEM((tr, BRANCH_W), F32)],
        compiler_params=_cparams("arbitrary"),
    )(p, p, dyb, ln_g, ln_b, w, w_t, b_col)


def _shift_down(x, prev8, k):
    rolled = pltpu.roll(x, k, 0)
    r8 = lax.broadcasted_iota(jnp.int32, prev8.shape, 0)
    head = jnp.where(r8 < k, pltpu.roll(prev8, k, 0), rolled[:HALO])
    return jnp.concatenate([head, rolled[HALO:]], axis=0)


def _shift_up(x, next8, k):
    n = x.shape[0]
    rolled = pltpu.roll(x, n - k, 0)
    r8 = lax.broadcasted_iota(jnp.int32, next8.shape, 0)
    tail = jnp.where(r8 >= HALO - k, pltpu.roll(next8, HALO - k, 0), rolled[n - HALO:])
    return jnp.concatenate([rolled[:n - HALO], tail], axis=0)


def _conv_specs(s, tr):
    nb = tr // HALO
    last = s // HALO - 1
    tile = lambda cb: pl.BlockSpec((tr, 128), lambda j, i: (i, cb * 4 + j))
    above = lambda cb: pl.BlockSpec((HALO, 128), lambda j, i: (jnp.maximum(i * nb - 1, 0), cb * 4 + j))
    below = lambda cb: pl.BlockSpec((HALO, 128), lambda j, i: (jnp.minimum((i + 1) * nb, last), cb * 4 + j))
    return tile, above, below


def _conv_fwd(p, cw, *, name):
    s = p.shape[0]
    tr = _pick(s, 512, HALO)
    tile, above, _ = _conv_specs(s, tr)

    def body(cb_ref, cc_ref, cx_ref, ccp_ref, cxp_ref, w_ref, o_ref):
        first = pl.program_id(1) == 0
        y = cc_ref[...] * cx_ref[...]
        yp = jnp.where(first, 0.0, ccp_ref[...] * cxp_ref[...])
        conv = w_ref[2:3, :] * y + w_ref[1:2, :] * _shift_down(y, yp, 1) + w_ref[0:1, :] * _shift_down(y, yp, 2)
        o_ref[...] = cb_ref[...] * conv

    return pl.pallas_call(
        body, name=name, grid=(4, s // tr),
        in_specs=[tile(5), tile(6), tile(7), above(6), above(7), pl.BlockSpec((3, 128), lambda j, i: (0, j))],
        out_specs=pl.BlockSpec((tr, 128), lambda j, i: (i, j)),
        out_shape=jax.ShapeDtypeStruct((s, BRANCH_W), F32),
        compiler_params=_cparams("parallel", "parallel"),
    )(p, p, p, p, p, cw)


def _conv_bwd(p, dyc, cw, *, name):
    s = p.shape[0]
    tr = _pick(s, 512, HALO)
    nt = s // tr
    nb = tr // HALO
    last = s // HALO - 1
    tile, above, below = _conv_specs(s, tr)

    def body(cb_ref, cc_ref, cx_ref, ccp_ref, cxp_ref, cbn_ref, dy_ref, dyn_ref, w_ref,
             dcb_ref, dcc_ref, dcx_ref, dw_ref):
        i = pl.program_id(1)

        @pl.when(i == 0)
        def _():
            dw_ref[...] = jnp.zeros_like(dw_ref)

        cb = cb_ref[...]
        cc = cc_ref[...]
        cx = cx_ref[...]
        y = cc * cx
        yp = jnp.where(i == 0, 0.0, ccp_ref[...] * cxp_ref[...])
        y1 = _shift_down(y, yp, 1)
        y2 = _shift_down(y, yp, 2)
        w0, w1, w2 = w_ref[0:1, :], w_ref[1:2, :], w_ref[2:3, :]
        conv = w2 * y + w1 * y1 + w0 * y2
        dyc_v = dy_ref[...]
        dconv = dyc_v * cb
        dn = jnp.where(i == nt - 1, 0.0, dyn_ref[...] * cbn_ref[...])
        dyv = w2 * dconv + w1 * _shift_up(dconv, dn, 1) + w0 * _shift_up(dconv, dn, 2)
        dcb_ref[...] = (dyc_v * conv).astype(BF16)
        dcc_ref[...] = (dyv * cx).astype(BF16)
        dcx_ref[...] = (dyv * cc).astype(BF16)
        dw_ref[0:1, :] += jnp.sum(dconv * y2, axis=0, keepdims=True)
        dw_ref[1:2, :] += jnp.sum(dconv * y1, axis=0, keepdims=True)
        dw_ref[2:3, :] += jnp.sum(dconv * y, axis=0, keepdims=True)

    dy_tile = pl.BlockSpec((tr, 128), lambda j, i: (i, j))
    dy_below = pl.BlockSpec((HALO, 128), lambda j, i: (jnp.minimum((i + 1) * nb, last), j))
    out_tile = lambda cb: pl.BlockSpec((tr, 128), lambda j, i: (i, cb * 4 + j))
    w_spec = pl.BlockSpec((3, 128), lambda j, i: (0, j))
    dcb, dcc, dcx, dw = pl.pallas_call(
        body, name=name, grid=(4, nt),
        in_specs=[tile(5), tile(6), tile(7), above(6), above(7), below(5), dy_tile, dy_below, w_spec],
        out_specs=[dy_tile, dy_tile, dy_tile, w_spec],
        out_shape=[jax.ShapeDtypeStruct((s, BRANCH_W), BF16)] * 3 + [jax.ShapeDtypeStruct((3, BRANCH_W), F32)],
        compiler_params=_cparams("parallel", "arbitrary"),
    )(p, p, p, p, p, p, dyc, dyc, cw)
    return dcb, dcc, dcx, dw


def _merge_fwd(ya, yb, yc, wb, p, *, name):
    s = p.shape[0]
    tr = _pick(s, 256, 16)

    def body(ya_ref, yb_ref, yc_ref, wb_ref, g0_ref, g1_ref, g2_ref, o_ref):
        acc = jnp.zeros((tr, D_MODEL), F32)
        for n, (y_ref, g_ref) in enumerate(((ya_ref, g0_ref), (yb_ref, g1_ref), (yc_ref, g2_ref))):
            acc = acc + _sigmoid(g_ref[...]) * _dot(y_ref[...].astype(BF16), wb_ref[n])
        o_ref[...] = acc.astype(BF16)

    yspec = pl.BlockSpec((tr, BRANCH_W), lambda i: (i, 0))
    gate = lambda n: pl.BlockSpec((tr, D_MODEL), lambda i: (i, 4 + n))
    return pl.pallas_call(
        body, name=name, grid=(s // tr,),
        in_specs=[yspec, yspec, yspec, pl.BlockSpec((3, BRANCH_W, D_MODEL), lambda i: (0, 0, 0)),
                  gate(0), gate(1), gate(2)],
        out_specs=pl.BlockSpec((tr, D_MODEL), lambda i: (i, 0)),
        out_shape=jax.ShapeDtypeStruct((s, D_MODEL), BF16),
        compiler_params=_cparams("parallel"),
    )(ya, yb, yc, wb, p, p, p)


def _merge_bwd(dm, ya, yb, yc, wb, p, *, name):
    s = p.shape[0]
    tr = _pick(s, 256, 16)

    def body(dm_ref, ya_ref, yb_ref, yc_ref, wb_ref, g0_ref, g1_ref, g2_ref,
             dya_ref, dyb_ref, dyc_ref, dg_ref, dbrd_ref):
        dmv = dm_ref[...]
        ys = (ya_ref, yb_ref, yc_ref)
        gs = (g0_ref, g1_ref, g2_ref)
        dys = (dya_ref, dyb_ref, dyc_ref)
        for n in range(3):
            brd = _dot(ys[n][...].astype(BF16), wb_ref[n])
            sg = _sigmoid(gs[n][...])
            dbrd = (sg * dmv).astype(BF16)
            dbrd_ref[n] = dbrd
            dg_ref[:, n * D_MODEL:(n + 1) * D_MODEL] = (dmv * brd * sg * (1.0 - sg)).astype(BF16)
            dys[n][...] = _dot_nt(dbrd, wb_ref[n])

    yspec = pl.BlockSpec((tr, BRANCH_W), lambda i: (i, 0))
    gate = lambda n: pl.BlockSpec((tr, D_MODEL), lambda i: (i, 4 + n))
    row = pl.BlockSpec((tr, D_MODEL), lambda i: (i, 0))
    return pl.pallas_call(
        body, name=name, grid=(s // tr,),
        in_specs=[row, yspec, yspec, yspec, pl.BlockSpec((3, BRANCH_W, D_MODEL), lambda i: (0, 0, 0)),
                  gate(0), gate(1), gate(2)],
        out_specs=[yspec, yspec, yspec, pl.BlockSpec((tr, 3 * D_MODEL), lambda i: (i, 0)),
                   pl.BlockSpec((3, tr, D_MODEL), lambda i: (0, i, 0))],
        out_shape=[jax.ShapeDtypeStruct((s, BRANCH_W), F32)] * 3
                  + [jax.ShapeDtypeStruct((s, 3 * D_MODEL), BF16), jax.ShapeDtypeStruct((3, s, D_MODEL), BF16)],
        compiler_params=_cparams("parallel"),
    )(dm, ya, yb, yc, wb, p, p, p)


def _xa_probs(q, k):
    sc = _dot_nt(q, k) * XA_SCALE
    e = jnp.exp(sc - jnp.max(sc, axis=-1, keepdims=True))
    return e / jnp.sum(e, axis=-1, keepdims=True)


def _xa_fwd(q, k, v, *, name):
    s = q.shape[0]
    mt = k.shape[0]
    tr = _pick(s, 512, 16)

    def body(q_ref, k_ref, v_ref, o_ref):
        pr = _xa_probs(q_ref[...], k_ref[...])
        o_ref[...] = _dot(pr.astype(BF16), v_ref[...]).astype(BF16)

    qs = pl.BlockSpec((tr, XA_HEAD), lambda h, i: (i, h))
    ks = pl.BlockSpec((mt, XA_HEAD), lambda h, i: (0, h))
    return pl.pallas_call(
        body, name=name, grid=(D_MODEL // XA_HEAD, s // tr),
        in_specs=[qs, ks, ks], out_specs=qs,
        out_shape=jax.ShapeDtypeStruct((s, D_MODEL), BF16),
        compiler_params=_cparams("parallel", "parallel"),
    )(q, k, v)


def _xa_bwd(q, k, v, do, *, name):
    s = q.shape[0]
    mt = k.shape[0]
    tr = _pick(s, 512, 16)

    def body(q_ref, k_ref, v_ref, do_ref, dq_ref, dk_ref, dv_ref):
        @pl.when(pl.program_id(1) == 0)
        def _():
            dk_ref[...] = jnp.zeros_like(dk_ref)
            dv_ref[...] = jnp.zeros_like(dv_ref)

        qv = q_ref[...]
        kv = k_ref[...]
        dov = do_ref[...]
        pr = _xa_probs(qv, kv)
        dpr = _dot_nt(dov, v_ref[...])
        ds = (pr * (dpr - jnp.sum(dpr * pr, axis=-1, keepdims=True)) * XA_SCALE).astype(BF16)
        dq_ref[...] = _dot(ds, kv).astype(BF16)
        dk_ref[...] += _dot_tn(ds, qv)
        dv_ref[...] += _dot_tn(pr.astype(BF16), dov)

    qs = pl.BlockSpec((tr, XA_HEAD), lambda h, i: (i, h))
    ks = pl.BlockSpec((mt, XA_HEAD), lambda h, i: (0, h))
    return pl.pallas_call(
        body, name=name, grid=(D_MODEL // XA_HEAD, s // tr),
        in_specs=[qs, ks, ks, qs], out_specs=[qs, ks, ks],
        out_shape=[jax.ShapeDtypeStruct((s, D_MODEL), BF16), jax.ShapeDtypeStruct((mt, D_MODEL), F32),
                   jax.ShapeDtypeStruct((mt, D_MODEL), F32)],
        compiler_params=_cparams("parallel", "arbitrary"),
    )(q, k, v, do)


def _swiglu_fwd(a, b, *, name):
    s, f = a.shape
    tr = _pick(s, 256, 16)

    def body(a_ref, b_ref, o_ref):
        av = a_ref[...]
        o_ref[...] = (av * _sigmoid(av) * b_ref[...]).astype(BF16)

    spec = pl.BlockSpec((tr, f), lambda i: (i, 0))
    return pl.pallas_call(
        body, name=name, grid=(s // tr,), in_specs=[spec, spec], out_specs=spec,
        out_shape=jax.ShapeDtypeStruct((s, f), BF16), compiler_params=_cparams("parallel"),
    )(a, b)


def _swiglu_bwd(a, b, dh, *, name):
    s, f = a.shape
    tr = _pick(s, 256, 16)

    def body(a_ref, b_ref, dh_ref, da_ref, db_ref):
        av = a_ref[...]
        dhv = dh_ref[...]
        sg = _sigmoid(av)
        silu = av * sg
        da_ref[...] = (dhv * b_ref[...] * (sg + silu * (1.0 - sg))).astype(BF16)
        db_ref[...] = (dhv * silu).astype(BF16)

    spec = pl.BlockSpec((tr, f), lambda i: (i, 0))
    return pl.pallas_call(
        body, name=name, grid=(s // tr,), in_specs=[spec, spec, spec], out_specs=[spec, spec],
        out_shape=[jax.ShapeDtypeStruct((s, f), BF16)] * 2, compiler_params=_cparams("parallel"),
    )(a, b, dh)


def _adamw(w, g, m, v, *, name):
    r, c = w.shape
    tr = _pick(r, 512, 8)

    def body(w_ref, g_ref, m_ref, v_ref, d_ref, mo_ref, vo_ref):
        gv = g_ref[...]
        mn = ADAM_B1 * m_ref[...] + (1.0 - ADAM_B1) * gv
        vn = ADAM_B2 * v_ref[...] + (1.0 - ADAM_B2) * (gv * gv)
        m_hat = mn / (1.0 - ADAM_B1 ** ADAM_STEP)
        v_hat = vn / (1.0 - ADAM_B2 ** ADAM_STEP)
        d_ref[...] = -ADAM_LR * (m_hat / (jnp.sqrt(v_hat) + ADAM_EPS) + ADAM_WD * w_ref[...])
        mo_ref[...] = mn
        vo_ref[...] = vn

    spec = pl.BlockSpec((tr, c), lambda i: (i, 0))
    shp = jax.ShapeDtypeStruct((r, c), F32)
    return pl.pallas_call(
        body, name=name, grid=(r // tr,), in_specs=[spec] * 4, out_specs=[spec] * 3,
        out_shape=[shp] * 3, compiler_params=_cparams("parallel"),
    )(w, g, m, v)


def _position():
    return lax.axis_index("x"), lax.axis_index("y"), lax.axis_index("c")


def _all_gather(x, *, name):
    t, c_ = x.shape

    def body(x_ref, out_ref, send_sems, recv_sems, local_sem):
        x_, y_, c = _position()
        me, sibling = (x_, y_, c), (x_, y_, 1 - c)
        chips = [(1 - x_, y_), (x_, 1 - y_), (1 - x_, 1 - y_)]

        def block(px, py, pc):
            return out_ref.at[4 * px + 2 * py + pc]

        def copy(k, blk, to, src=None):
            return pltpu.make_async_remote_copy(
                src_ref=block(*blk) if src is None else src, dst_ref=block(*blk),
                send_sem=send_sems.at[k], recv_sem=recv_sems.at[k], device_id=to, device_id_type=MESH)

        mine = pltpu.make_async_copy(x_ref, block(*me), local_sem)
        mine.start()
        first = [copy(0, me, sibling, src=x_ref)]
        first += [copy(1 + j, me, (*chip, c), src=x_ref) for j, chip in enumerate(chips)]
        for cp in first:
            cp.start()
        passed = [copy(4 + j, (*chip, c), sibling) for j, chip in enumerate(chips)]
        for j, chip in enumerate(chips):
            copy(1 + j, (*chip, c), me).wait_recv()
            passed[j].start()
        copy(0, sibling, me).wait_recv()
        for j, chip in enumerate(chips):
            copy(4 + j, (*chip, 1 - c), me).wait_recv()
        for cp in first + passed:
            cp.wait_send()
        mine.wait()

    return pl.pallas_call(
        body, name=name,
        out_shape=jax.ShapeDtypeStruct((N_DEV, t, c_), x.dtype),
        in_specs=[pl.BlockSpec(memory_space=pl.ANY)],
        out_specs=pl.BlockSpec(memory_space=pl.ANY),
        scratch_shapes=[pltpu.SemaphoreType.DMA((7,)), pltpu.SemaphoreType.DMA((7,)), pltpu.SemaphoreType.DMA],
    )(x)


def _all_reduce_small(x, *, name):
    r, c_ = x.shape

    def body(x_ref, o_ref, buf, send_sems, recv_sems):
        x_, y_, c = _position()
        me = 4 * x_ + 2 * y_ + c
        buf[me] = x_ref[...]
        copies = []
        for k in range(1, N_DEV):
            to = (x_ ^ (k >> 2), y_ ^ ((k >> 1) & 1), c ^ (k & 1))
            copies.append(pltpu.make_async_remote_copy(
                src_ref=x_ref, dst_ref=buf.at[me], send_sem=send_sems.at[k - 1], recv_sem=recv_sems.at[k - 1],
                device_id=to, device_id_type=MESH))
        for cp in copies:
            cp.start()
        for k in range(1, N_DEV):
            src = me ^ k
            pltpu.make_async_remote_copy(
                src_ref=x_ref, dst_ref=buf.at[src], send_sem=send_sems.at[k - 1], recv_sem=recv_sems.at[k - 1],
                device_id=(x_, y_, c), device_id_type=MESH).wait_recv()
        for cp in copies:
            cp.wait_send()
        acc = buf[0]
        for d in range(1, N_DEV):
            acc = acc + buf[d]
        o_ref[...] = acc

    return pl.pallas_call(
        body, name=name,
        out_shape=jax.ShapeDtypeStruct((r, c_), F32),
        in_specs=[pl.BlockSpec(memory_space=pltpu.VMEM)],
        out_specs=pl.BlockSpec(memory_space=pltpu.VMEM),
        scratch_shapes=[pltpu.VMEM((N_DEV, r, c_), F32), pltpu.SemaphoreType.DMA((7,)), pltpu.SemaphoreType.DMA((7,))],
    )(x)


def _rs_pair_exchange(g8, *, name):
    _, t, c_ = g8.shape

    def body(g_ref, r_ref, send_sems, recv_sems):
        x_, y_, c = _position()
        copies = [pltpu.make_async_remote_copy(
            src_ref=g_ref.at[2 * ch + (1 - c)], dst_ref=r_ref.at[ch],
            send_sem=send_sems.at[ch], recv_sem=recv_sems.at[ch],
            device_id=(x_, y_, 1 - c), device_id_type=MESH) for ch in range(4)]
        for cp in copies:
            cp.start()
        for cp in copies:
            cp.wait()

    return pl.pallas_call(
        body, name=name,
        out_shape=jax.ShapeDtypeStruct((4, t, c_), g8.dtype),
        in_specs=[pl.BlockSpec(memory_space=pl.ANY)],
        out_specs=pl.BlockSpec(memory_space=pl.ANY),
        scratch_shapes=[pltpu.SemaphoreType.DMA((4,)), pltpu.SemaphoreType.DMA((4,))],
    )(g8)


def _pair_add(core, g8, recv, *, name):
    _, t, c_ = g8.shape
    tr = _pick(t, 512, 16)

    def body(core_ref, g_ref, r_ref, o_ref):
        o_ref[...] = (g_ref[...].astype(F32) + r_ref[...].astype(F32)).astype(o_ref.dtype)

    grid_spec = pltpu.PrefetchScalarGridSpec(
        num_scalar_prefetch=1, grid=(4, t // tr),
        in_specs=[pl.BlockSpec((None, tr, c_), lambda ch, i, core_ref: (2 * ch + core_ref[0], i, 0)),
                  pl.BlockSpec((None, tr, c_), lambda ch, i, core_ref: (ch, i, 0))],
        out_specs=pl.BlockSpec((None, tr, c_), lambda ch, i, core_ref: (ch, i, 0)))
    return pl.pallas_call(
        body, name=name, grid_spec=grid_spec,
        out_shape=jax.ShapeDtypeStruct((4, t, c_), g8.dtype),
        compiler_params=_cparams("parallel", "parallel"),
    )(core, g8, recv)


def _rs_chip_exchange(part, *, name):
    _, t, c_ = part.shape

    def body(p_ref, r_ref, send_sems, recv_sems, local_sem):
        x_, y_, c = _position()
        mine = 2 * x_ + y_
        local = pltpu.make_async_copy(p_ref.at[mine], r_ref.at[mine], local_sem)
        local.start()
        chips = [(1 - x_, y_), (x_, 1 - y_), (1 - x_, 1 - y_)]
        copies = [pltpu.make_async_remote_copy(
            src_ref=p_ref.at[2 * px + py], dst_ref=r_ref.at[mine],
            send_sem=send_sems.at[k], recv_sem=recv_sems.at[k],
            device_id=(px, py, c), device_id_type=MESH) for k, (px, py) in enumerate(chips)]
        for cp in copies:
            cp.start()
        for k, (px, py) in enumerate(chips):
            pltpu.make_async_remote_copy(
                src_ref=p_ref.at[mine], dst_ref=r_ref.at[2 * px + py],
                send_sem=send_sems.at[k], recv_sem=recv_sems.at[k],
                device_id=(x_, y_, c), device_id_type=MESH).wait_recv()
        for cp in copies:
            cp.wait_send()
        local.wait()

    return pl.pallas_call(
        body, name=name,
        out_shape=jax.ShapeDtypeStruct((4, t, c_), part.dtype),
        in_specs=[pl.BlockSpec(memory_space=pl.ANY)],
        out_specs=pl.BlockSpec(memory_space=pl.ANY),
        scratch_shapes=[pltpu.SemaphoreType.DMA((3,)), pltpu.SemaphoreType.DMA((3,)), pltpu.SemaphoreType.DMA],
    )(part)


def _sum_chips(r4, *, name):
    _, t, c_ = r4.shape
    tr = _pick(t, 512, 16)

    def body(r_ref, o_ref):
        acc = r_ref[0].astype(F32)
        for ch in range(1, 4):
            acc = acc + r_ref[ch].astype(F32)
        o_ref[...] = acc

    return pl.pallas_call(
        body, name=name, grid=(t // tr,),
        in_specs=[pl.BlockSpec((4, tr, c_), lambda i: (0, i, 0))],
        out_specs=pl.BlockSpec((tr, c_), lambda i: (i, 0)),
        out_shape=jax.ShapeDtypeStruct((t, c_), F32),
        compiler_params=_cparams("parallel"),
    )(r4)


BIG = (
    ("w_in", (DEPTH, D_MODEL, IN_COLS // N_DEV), 2),
    ("w_branch", (DEPTH, 3, BRANCH_W, D_MODEL // N_DEV), 3),
    ("w_out", (DEPTH, D_MODEL // N_DEV, D_MODEL), 1),
    ("w_q_xa", (DEPTH, D_MODEL // N_DEV, D_MODEL), 1),
    ("w_k_xa", (DEPTH, D_MODEL // N_DEV, D_MODEL), 1),
    ("w_v_xa", (DEPTH, D_MODEL // N_DEV, D_MODEL), 1),
    ("w_o_xa", (DEPTH, D_MODEL // N_DEV, D_MODEL), 1),
    ("w_gate_ffn", (DEPTH, D_MODEL, FFN // N_DEV), 2),
    ("w_up_ffn", (DEPTH, D_MODEL, FFN // N_DEV), 2),
    ("w_down_ffn", (DEPTH, FFN // N_DEV, D_MODEL), 1),
)
PACK_COLS = 1024


def _size(shape):
    n = 1
    for d in shape:
        n *= d
    return n


def _pack_shards(shards):
    return jnp.concatenate([shards[n].reshape(-1, PACK_COLS) for n, _, _ in BIG], axis=0)


def _unpack_gathered(g):
    out = {}
    r0 = 0
    for n, shp, ax in BIG:
        rows = _size(shp) // PACK_COLS
        blk = g[:, r0:r0 + rows].reshape((N_DEV,) + shp)
        r0 += rows
        blk = jnp.moveaxis(blk, 0, ax)
        full = list(shp)
        full[ax] = shp[ax] * N_DEV
        out[n] = blk.reshape(full)
    return out


def _pack_full(full):
    parts = []
    for n, shp, ax in BIG:
        t = full[n].reshape(shp[:ax] + (N_DEV, shp[ax]) + shp[ax + 1:])
        t = jnp.moveaxis(t, ax, 0)
        parts.append(t.reshape(N_DEV, -1, PACK_COLS))
    return jnp.concatenate(parts, axis=1)


def _unpack_shard(flat):
    out = {}
    r0 = 0
    for n, shp, _ in BIG:
        rows = _size(shp) // PACK_COLS
        out[n] = flat[r0:r0 + rows].reshape(shp)
        r0 += rows
    return out


SMALL = (
    ("norm_mix_g", (DEPTH, D_MODEL)),
    ("sgu_ln_g", (DEPTH, BRANCH_W)),
    ("sgu_ln_b", (DEPTH, BRANCH_W)),
    ("w_spatial", (DEPTH, SGU_GROUPS, SGU_LEN, SGU_LEN)),
    ("b_spatial", (DEPTH, SGU_GROUPS, SGU_LEN)),
    ("conv_w", (DEPTH, 3, BRANCH_W)),
    ("norm_xa_g", (DEPTH, D_MODEL)),
    ("mem_norm_g", (DEPTH, D_MODEL)),
    ("norm_ffn_g", (DEPTH, D_MODEL)),
    ("final_g", (D_MODEL,)),
)


def _pack_small(grads):
    flat = jnp.concatenate([grads[n].reshape(-1) for n, _ in SMALL])
    rows = -(-flat.shape[0] // PACK_COLS)
    rows = -(-rows // 8) * 8
    flat = jnp.pad(flat, (0, rows * PACK_COLS - flat.shape[0]))
    return flat.reshape(rows, PACK_COLS)


def _unpack_small(buf):
    flat = buf.reshape(-1)
    out = {}
    o = 0
    for n, shp in SMALL:
        out[n] = flat[o:o + _size(shp)].reshape(shp)
        o += _size(shp)
    return out


def _layer_fwd(l, x, mem, wt, sm):
    t = f"l{l}_"
    sv = {"x0": x}
    h = _rms_fwd(x, sm["norm_mix_g"][l][None], name=t + "rms_mix")
    p = _mm(h, wt["w_in"][l], name=t + "in_proj", tn=1024)
    ya = _sb_fwd(p, name=t + "sb_fwd")
    w_sp = sm["w_spatial"][l]
    b_col = sm["b_spatial"][l][:, :, None]
    ln_g, ln_b = sm["sgu_ln_g"][l][None], sm["sgu_ln_b"][l][None]
    yb = _sgu_fwd(p, ln_g, ln_b, w_sp, b_col, name=t + "sgu_fwd")
    yc = _conv_fwd(p, sm["conv_w"][l], name=t + "conv_fwd")
    merged = _merge_fwd(ya, yb, yc, wt["w_branch"][l], p, name=t + "merge_fwd")
    x1 = _mm(merged, wt["w_out"][l], add=x, name=t + "out_proj")
    sv.update(h=h, p=p, ya=ya, yb=yb, yc=yc, merged=merged, x1=x1)

    h2 = _rms_fwd(x1, sm["norm_xa_g"][l][None], name=t + "rms_xa")
    mn = _rms_fwd(mem, sm["mem_norm_g"][l][None], name=t + "rms_mem")
    q = _mm(h2, wt["w_q_xa"][l], out_dtype=BF16, name=t + "xa_q")
    k = _mm(mn, wt["w_k_xa"][l], out_dtype=BF16, name=t + "xa_k")
    v = _mm(mn, wt["w_v_xa"][l], out_dtype=BF16, name=t + "xa_v")
    o = _xa_fwd(q, k, v, name=t + "xa_fwd")
    x2 = _mm(o, wt["w_o_xa"][l], add=x1, name=t + "xa_o")
    sv.update(h2=h2, mn=mn, q=q, k=k, v=v, o=o, x2=x2)

    h3 = _rms_fwd(x2, sm["norm_ffn_g"][l][None], name=t + "rms_ffn")
    a = _mm(h3, wt["w_gate_ffn"][l], name=t + "ffn_gate", tn=1408)
    b = _mm(h3, wt["w_up_ffn"][l], name=t + "ffn_up", tn=1408)
    hd = _swiglu_fwd(a, b, name=t + "swiglu_fwd")
    x3 = _mm(hd, wt["w_down_ffn"][l], add=x2, name=t + "ffn_down")
    sv.update(h3=h3, a=a, b=b, hd=hd)
    return x3, sv


def _layer_bwd(l, dx3, mem, wt, sm, sv):
    t = f"l{l}_b_"
    gb, gs = {}, {}
    dhd = _mm(dx3, wt["w_down_ffn"][l], tb=True, name=t + "ffn_down_dx", tn=1408)
    gb["w_down_ffn"] = _mm(sv["hd"], dx3, ta=True, out_dtype=BF16, name=t + "ffn_down_dw", tm=1408)
    da, db = _swiglu_bwd(sv["a"], sv["b"], dhd, name=t + "swiglu_bwd")
    dh3 = _mm(da, wt["w_gate_ffn"][l], tb=True, name=t + "ffn_gate_dx")
    dh3 = _mm(db, wt["w_up_ffn"][l], tb=True, add=dh3, name=t + "ffn_up_dx")
    gb["w_gate_ffn"] = _mm(sv["h3"], da, ta=True, out_dtype=BF16, name=t + "ffn_gate_dw", tn=1408)
    gb["w_up_ffn"] = _mm(sv["h3"], db, ta=True, out_dtype=BF16, name=t + "ffn_up_dw", tn=1408)
    dx2, dg = _rms_bwd(sv["x2"], sm["norm_ffn_g"][l][None], dh3, dx3, name=t + "rms_ffn")
    gs["norm_ffn_g"] = dg[0]
    do = _mm(dx2, wt["w_o_xa"][l], tb=True, out_dtype=BF16, name=t + "xa_o_dx")
    gb["w_o_xa"] = _mm(sv["o"], dx2, ta=True, out_dtype=BF16, name=t + "xa_o_dw")
    dq, dk, dv = _xa_bwd(sv["q"], sv["k"], sv["v"], do, name=t + "xa_bwd")
    dh2 = _mm(dq, wt["w_q_xa"][l], tb=True, name=t + "xa_q_dx")
    gb["w_q_xa"] = _mm(sv["h2"], dq, ta=True, out_dtype=BF16, name=t + "xa_q_dw")
    gb["w_k_xa"] = _mm(sv["mn"], dk, ta=True, out_dtype=BF16, name=t + "xa_k_dw")
    gb["w_v_xa"] = _mm(sv["mn"], dv, ta=True, out_dtype=BF16, name=t + "xa_v_dw")
    dmn = _mm(dk, wt["w_k_xa"][l], tb=True, name=t + "xa_k_dx")
    dmn = _mm(dv, wt["w_v_xa"][l], tb=True, add=dmn, name=t + "xa_v_dx")
    _, dg = _rms_bwd(mem, sm["mem_norm_g"][l][None], dmn, jnp.zeros_like(mem), name=t + "rms_mem")
    gs["mem_norm_g"] = dg[0]
    dx1, dg = _rms_bwd(sv["x1"], sm["norm_xa_g"][l][None], dh2, dx2, name=t + "rms_xa")
    gs["norm_xa_g"] = dg[0]
    dm = _mm(dx1, wt["w_out"][l], tb=True, name=t + "out_proj_dx")
    gb["w_out"] = _mm(sv["merged"], dx1, ta=True, out_dtype=BF16, name=t + "out_proj_dw")
    p = sv["p"]
    dya, dyb, dyc, dgates, dbrd = _merge_bwd(dm, sv["ya"], sv["yb"], sv["yc"], wt["w_branch"][l], p,
                                             name=t + "merge_bwd")
    gb["w_branch"] = jnp.stack([
        _mm(sv[y], dbrd[n], ta=True, out_dtype=BF16, name=t + f"branch{n}_dw")
        for n, y in enumerate(("ya", "yb", "yc"))])
    dcb, dcc, dcx, dcw = _conv_bwd(p, dyc, sm["conv_w"][l], name=t + "conv_bwd")
    gs["conv_w"] = dcw
    w_sp = sm["w_spatial"][l]
    dz, dlg, dlb, dwsp, dbsp = _sgu_bwd(p, dyb, sm["sgu_ln_g"][l][None], sm["sgu_ln_b"][l][None], w_sp,
                                        jnp.swapaxes(w_sp, 1, 2), sm["b_spatial"][l][:, :, None],
                                        name=t + "sgu_bwd")
    gs.update(sgu_ln_g=dlg[0], sgu_ln_b=dlb[0], w_spatial=dwsp, b_spatial=dbsp[:, :, 0])
    dq_a, dk_a, dv_a = _sb_bwd(p, dya, name=t + "sb_bwd")
    dp = jnp.concatenate([dq_a, dk_a, dv_a, dz, dcb, dcc, dcx, dgates], axis=1)
    dh = _mm(dp, wt["w_in"][l], tb=True, name=t + "in_proj_dx")
    gb["w_in"] = _mm(sv["h"], dp, ta=True, out_dtype=BF16, name=t + "in_proj_dw", tn=1024)
    dx, dg = _rms_bwd(sv["x0"], sm["norm_mix_g"][l][None], dh, dx1, name=t + "rms_mix")
    gs["norm_mix_g"] = dg[0]
    return dx, gb, gs


def _local_step(x, mem, target, wt, sm):
    saved = []
    cur = x
    for l in range(DEPTH):
        cur, sv = _layer_fwd(l, cur, mem, wt, sm)
        saved.append(sv)
    dcur, loss, dfinal = _final_loss(cur, sm["final_g"][None], target, name="final_loss")
    gbs, gss = [None] * DEPTH, [None] * DEPTH
    for l in reversed(range(DEPTH)):
        dcur, gbs[l], gss[l] = _layer_bwd(l, dcur, mem, wt, sm, saved[l])
    big = {n: jnp.stack([gbs[l][n] for l in range(DEPTH)]) for n, _, _ in BIG}
    small = {n: jnp.stack([gss[l][n] for l in range(DEPTH)]) for n, _ in SMALL if n != "final_g"}
    small["final_g"] = dfinal[0]
    return loss, dcur, big, small


_WEIGHTS = ("norm_mix_g", "w_in", "sgu_ln_g", "sgu_ln_b", "w_spatial", "b_spatial", "conv_w", "w_branch", "w_out",
            "norm_xa_g", "mem_norm_g", "w_q_xa", "w_k_xa", "w_v_xa", "w_o_xa", "norm_ffn_g", "w_gate_ffn",
            "w_up_ffn", "w_down_ffn", "final_g")


def kernel(x, mem, norm_mix_g, w_in, sgu_ln_g, sgu_ln_b, w_spatial, b_spatial, conv_w, w_branch, w_out, norm_xa_g, mem_norm_g, w_q_xa, w_k_xa, w_v_xa, w_o_xa, norm_ffn_g, w_gate_ffn, w_up_ffn, w_down_ffn, final_g, loss_target, m_norm_mix_g, m_w_in, m_sgu_ln_g, m_sgu_ln_b, m_w_spatial, m_b_spatial, m_conv_w, m_w_branch, m_w_out, m_norm_xa_g, m_mem_norm_g, m_w_q_xa, m_w_k_xa, m_w_v_xa, m_w_o_xa, m_norm_ffn_g, m_w_gate_ffn, m_w_up_ffn, m_w_down_ffn, m_final_g, v_norm_mix_g, v_w_in, v_sgu_ln_g, v_sgu_ln_b, v_w_spatial, v_b_spatial, v_conv_w, v_w_branch, v_w_out, v_norm_xa_g, v_mem_norm_g, v_w_q_xa, v_w_k_xa, v_w_v_xa, v_w_o_xa, v_norm_ffn_g, v_w_gate_ffn, v_w_up_ffn, v_w_down_ffn, v_final_g):
    w = dict(norm_mix_g=norm_mix_g, w_in=w_in, sgu_ln_g=sgu_ln_g, sgu_ln_b=sgu_ln_b, w_spatial=w_spatial,
             b_spatial=b_spatial, conv_w=conv_w, w_branch=w_branch, w_out=w_out, norm_xa_g=norm_xa_g,
             mem_norm_g=mem_norm_g, w_q_xa=w_q_xa, w_k_xa=w_k_xa, w_v_xa=w_v_xa, w_o_xa=w_o_xa,
             norm_ffn_g=norm_ffn_g, w_gate_ffn=w_gate_ffn, w_up_ffn=w_up_ffn, w_down_ffn=w_down_ffn, final_g=final_g)
    m = dict(norm_mix_g=m_norm_mix_g, w_in=m_w_in, sgu_ln_g=m_sgu_ln_g, sgu_ln_b=m_sgu_ln_b, w_spatial=m_w_spatial,
             b_spatial=m_b_spatial, conv_w=m_conv_w, w_branch=m_w_branch, w_out=m_w_out, norm_xa_g=m_norm_xa_g,
             mem_norm_g=m_mem_norm_g, w_q_xa=m_w_q_xa, w_k_xa=m_w_k_xa, w_v_xa=m_w_v_xa, w_o_xa=m_w_o_xa,
             norm_ffn_g=m_norm_ffn_g, w_gate_ffn=m_w_gate_ffn, w_up_ffn=m_w_up_ffn, w_down_ffn=m_w_down_ffn,
             final_g=m_final_g)
    v = dict(norm_mix_g=v_norm_mix_g, w_in=v_w_in, sgu_ln_g=v_sgu_ln_g, sgu_ln_b=v_sgu_ln_b, w_spatial=v_w_spatial,
             b_spatial=v_b_spatial, conv_w=v_conv_w, w_branch=v_w_branch, w_out=v_w_out, norm_xa_g=v_norm_xa_g,
             mem_norm_g=v_mem_norm_g, w_q_xa=v_w_q_xa, w_k_xa=v_w_k_xa, w_v_xa=v_w_v_xa, w_o_xa=v_w_o_xa,
             norm_ffn_g=v_norm_ffn_g, w_gate_ffn=v_w_gate_ffn, w_up_ffn=v_w_up_ffn, w_down_ffn=v_w_down_ffn,
             final_g=v_final_g)

    packed = _pack_shards({n: w[n].astype(BF16) for n, _, _ in BIG})
    wt = _unpack_gathered(_all_gather(packed, name="gather_weights"))
    cw_pad = jnp.zeros((8, 128), F32).at[:DEPTH * 3, :BRANCH_W // N_DEV].set(conv_w.reshape(DEPTH * 3, -1))
    cw_all = _all_gather(cw_pad, name="gather_conv_w")[:, :DEPTH * 3, :BRANCH_W // N_DEV]
    conv_full = jnp.moveaxis(cw_all.reshape(N_DEV, DEPTH, 3, BRANCH_W // N_DEV), 0, 2).reshape(DEPTH, 3, BRANCH_W)
    sm = {n: w[n] for n, _ in SMALL}
    sm["conv_w"] = conv_full

    loss, dx, big, small = _local_step(x[0], mem[0], loss_target[0], wt, sm)
    loss = lax.psum(loss[0, 0], AXES)

    g8 = _pack_full(big)
    core = lax.axis_index("c").astype(jnp.int32).reshape(1)
    from_sibling = _rs_pair_exchange(g8, name="rs_pair_exchange")
    part = _pair_add(core, g8, from_sibling, name="rs_pair_add")
    by_chip = _rs_chip_exchange(part, name="rs_chip_exchange")
    grads = _unpack_shard(_sum_chips(by_chip, name="rs_sum_chips"))
    small_sum = _unpack_small(_all_reduce_small(_pack_small(small), name="all_reduce_small"))
    width = BRANCH_W // N_DEV
    dev = 4 * lax.axis_index("x") + 2 * lax.axis_index("y") + lax.axis_index("c")
    for n, _ in SMALL:
        grads[n] = small_sum[n]
    grads["conv_w"] = lax.dynamic_slice_in_dim(small_sum["conv_w"], dev * width, width, axis=2)

    delta, new_m, new_v = {}, {}, {}
    for n in _WEIGHTS:
        shp = w[n].shape
        two_d = (-1, shp[-1])
        d_, m_, v_ = _adamw(w[n].reshape(two_d), grads[n].reshape(two_d), m[n].reshape(two_d), v[n].reshape(two_d),
                            name="adamw_" + n)
        delta[n], new_m[n], new_v[n] = d_.reshape(shp), m_.reshape(shp), v_.reshape(shp)

    return (loss, dx[None], *[grads[n] for n in _WEIGHTS], *[delta[n] for n in _WEIGHTS],
            *[new_m[n] for n in _WEIGHTS], *[new_v[n] for n in _WEIGHTS])
```

```python
import functools

import jax
import jax.numpy as jnp
from jax import lax
from jax.experimental import pallas as pl
from jax.experimental.pallas import tpu as pltpu

F32 = jnp.float32
BF16 = jnp.bfloat16
MESH = pl.DeviceIdType.MESH

D_MODEL = 1024
BRANCH_W = 512
IN_COLS = 7168
FFN = 2816
N_DEV = 8
DEPTH = 2
SB_BLOCK = 128
SB_SPAN = 1024
SB_SCALE = 0.125
XA_HEAD = 256
XA_SCALE = 0.0625
SGU_LEN = 128
SGU_GROUPS = 4
RMS_EPS = 1e-6
LN_EPS = 1e-5
HALO = 8

ADAM_LR = 0.001
ADAM_B1 = 0.9
ADAM_B2 = 0.999
ADAM_EPS = 1e-08
ADAM_WD = 0.01
ADAM_STEP = 10

VMEM_LIMIT_BYTES = 52 * 1024 * 1024

AXES = ("x", "y", "c")


def _cparams(*sem):
    return pltpu.CompilerParams(dimension_semantics=sem, vmem_limit_bytes=VMEM_LIMIT_BYTES)


def _pick(n, target, align):
    t = (min(target, n) // align) * align
    while t >= align:
        if n % t == 0:
            return t
        t -= align
    return n


def _dot(a, b):
    return jnp.dot(a, b, preferred_element_type=F32)


def _dot_nt(a, b):
    return lax.dot_general(a, b, (((1,), (1,)), ((), ())), preferred_element_type=F32)


def _dot_tn(a, b):
    return lax.dot_general(a, b, (((0,), (0,)), ((), ())), preferred_element_type=F32)


def _sigmoid(x):
    return 1.0 / (1.0 + jnp.exp(-x))


def _mm(a, b, *, name, ta=False, tb=False, out_dtype=F32, add=None, tm=512, tn=512, tk=2048):
    m, k = (a.shape[1], a.shape[0]) if ta else a.shape
    n = b.shape[0] if tb else b.shape[1]
    assert k == (b.shape[1] if tb else b.shape[0])
    tm = _pick(m, tm, 128)
    tn = _pick(n, tn, 128)
    tk = _pick(k, tk, 128)
    nk = k // tk
    ca = 0 if ta else 1
    cb = 1 if tb else 0

    def body(*refs):
        if add is None:
            a_ref, b_ref, o_ref, acc_ref = refs
            add_ref = None
        else:
            a_ref, b_ref, add_ref, o_ref, acc_ref = refs
        kk = pl.program_id(2)

        @pl.when(kk == 0)
        def _():
            acc_ref[...] = jnp.zeros_like(acc_ref)

        acc_ref[...] += lax.dot_general(
            a_ref[...].astype(BF16), b_ref[...].astype(BF16),
            (((ca,), (cb,)), ((), ())), preferred_element_type=F32)

        @pl.when(kk == nk - 1)
        def _():
            r = acc_ref[...]
            if add_ref is not None:
                r = r + add_ref[...]
            o_ref[...] = r.astype(out_dtype)

    a_spec = pl.BlockSpec((tk, tm), lambda i, j, kk: (kk, i)) if ta else pl.BlockSpec((tm, tk), lambda i, j, kk: (i, kk))
    b_spec = pl.BlockSpec((tn, tk), lambda i, j, kk: (j, kk)) if tb else pl.BlockSpec((tk, tn), lambda i, j, kk: (kk, j))
    in_specs = [a_spec, b_spec]
    operands = [a, b]
    if add is not None:
        in_specs.append(pl.BlockSpec((tm, tn), lambda i, j, kk: (i, j)))
        operands.append(add)
    return pl.pallas_call(
        body, name=name,
        grid=(m // tm, n // tn, nk),
        in_specs=in_specs,
        out_specs=pl.BlockSpec((tm, tn), lambda i, j, kk: (i, j)),
        out_shape=jax.ShapeDtypeStruct((m, n), out_dtype),
        scratch_shapes=[pltpu.VMEM((tm, tn), F32)],
        compiler_params=_cparams("parallel", "parallel", "arbitrary"),
    )(*operands)


def _rms_fwd(x, g, *, name):
    r, d = x.shape
    tr = _pick(r, 512, 16)

    def body(x_ref, g_ref, o_ref):
        xv = x_ref[...]
        rs = lax.rsqrt(jnp.mean(xv * xv, axis=-1, keepdims=True) + RMS_EPS)
        o_ref[...] = (xv * rs * g_ref[...]).astype(BF16)

    return pl.pallas_call(
        body, name=name, grid=(r // tr,),
        in_specs=[pl.BlockSpec((tr, d), lambda i: (i, 0)), pl.BlockSpec((1, d), lambda i: (0, 0))],
        out_specs=pl.BlockSpec((tr, d), lambda i: (i, 0)),
        out_shape=jax.ShapeDtypeStruct((r, d), BF16),
        compiler_params=_cparams("parallel"),
    )(x, g)


def _rms_bwd(x, g, dh, dres, *, name):
    r, d = x.shape
    tr = _pick(r, 256, 8)

    def body(x_ref, g_ref, dh_ref, dres_ref, dx_ref, dg_ref):
        @pl.when(pl.program_id(0) == 0)
        def _():
            dg_ref[...] = jnp.zeros_like(dg_ref)

        xv = x_ref[...]
        dhv = dh_ref[...].astype(F32)
        rs = lax.rsqrt(jnp.mean(xv * xv, axis=-1, keepdims=True) + RMS_EPS)
        xh = xv * rs
        dg_ref[...] += jnp.sum(dhv * xh, axis=0, keepdims=True)
        dxh = dhv * g_ref[...]
        dx_ref[...] = dres_ref[...] + rs * (dxh - xh * jnp.mean(dxh * xh, axis=-1, keepdims=True))

    return pl.pallas_call(
        body, name=name, grid=(r // tr,),
        in_specs=[pl.BlockSpec((tr, d), lambda i: (i, 0)), pl.BlockSpec((1, d), lambda i: (0, 0)),
                  pl.BlockSpec((tr, d), lambda i: (i, 0)), pl.BlockSpec((tr, d), lambda i: (i, 0))],
        out_specs=[pl.BlockSpec((tr, d), lambda i: (i, 0)), pl.BlockSpec((1, d), lambda i: (0, 0))],
        out_shape=[jax.ShapeDtypeStruct((r, d), F32), jax.ShapeDtypeStruct((1, d), F32)],
        compiler_params=_cparams("arbitrary"),
    )(x, g, dh, dres)


def _final_loss(x, g, target, *, name):
    r, d = x.shape
    tr = _pick(r, 256, 8)

    def body(x_ref, g_ref, t_ref, dx_ref, loss_ref, dg_ref):
        @pl.when(pl.program_id(0) == 0)
        def _():
            dg_ref[...] = jnp.zeros_like(dg_ref)
            loss_ref[...] = jnp.zeros_like(loss_ref)

        xv = x_ref[...]
        gv = g_ref[...]
        rs = lax.rsqrt(jnp.mean(xv * xv, axis=-1, keepdims=True) + RMS_EPS)
        xh = xv * rs
        err = xh * gv - t_ref[...]
        row_loss = jnp.mean(err * err, axis=-1, keepdims=True)
        loss_ref[...] += 0.5 * jnp.sum(row_loss, axis=0, keepdims=True)
        dy = err * (1.0 / d)
        dg_ref[...] += jnp.sum(dy * xh, axis=0, keepdims=True)
        dxh = dy * gv
        dx_ref[...] = rs * (dxh - xh * jnp.mean(dxh * xh, axis=-1, keepdims=True))

    return pl.pallas_call(
        body, name=name, grid=(r // tr,),
        in_specs=[pl.BlockSpec((tr, d), lambda i: (i, 0)), pl.BlockSpec((1, d), lambda i: (0, 0)),
                  pl.BlockSpec((tr, d), lambda i: (i, 0))],
        out_specs=[pl.BlockSpec((tr, d), lambda i: (i, 0)), pl.BlockSpec((1, 128), lambda i: (0, 0)),
                   pl.BlockSpec((1, d), lambda i: (0, 0))],
        out_shape=[jax.ShapeDtypeStruct((r, d), F32), jax.ShapeDtypeStruct((1, 128), F32),
                   jax.ShapeDtypeStruct((1, d), F32)],
        compiler_params=_cparams("arbitrary"),
    )(x, g, target)


def _cumsum_operand(strict_after, terms=1):
    r = lax.broadcasted_iota(jnp.int32, (terms * SB_BLOCK, 2 * SB_BLOCK), 0) % SB_BLOCK
    c = lax.broadcasted_iota(jnp.int32, (terms * SB_BLOCK, 2 * SB_BLOCK), 1)
    tri = (r > c) if strict_after else (r < c)
    return jnp.where((c >= SB_BLOCK) | tri, 1.0, 0.0).astype(BF16)


def _sb_scores(qh, kw, run, valid, after_ones):
    nb = kw.shape[0] // SB_BLOCK
    z = _dot_nt(qh, kw)
    lsp = jnp.minimum(z, 0.0) - jnp.log(1.0 + jnp.exp(-jnp.abs(z)))
    l1m = lsp - z
    if valid is not None:
        l1m = jnp.where(valid, l1m, 0.0)
    hi = l1m.astype(BF16)
    lo = (l1m - hi.astype(F32)).astype(BF16)
    later = [None] * nb
    for b in reversed(range(nb)):
        cols = slice(b * SB_BLOCK, (b + 1) * SB_BLOCK)
        ct = _dot(jnp.concatenate([hi[:, cols], lo[:, cols]], axis=1), after_ones)
        later[b] = run + ct[:, :SB_BLOCK]
        run = run + ct[:, SB_BLOCK:]
    a = jnp.exp(lsp + jnp.concatenate(later, axis=1))
    if valid is not None:
        a = jnp.where(valid, a, 0.0)
    return lsp, a, run


def _sb_setup(q_ref, span):
    qi = pl.program_id(1)
    per = span // SB_BLOCK
    sd = qi // per
    lane = lax.broadcasted_iota(jnp.int32, (SB_BLOCK, SB_BLOCK), 1)
    col = lax.broadcasted_iota(jnp.int32, (SB_BLOCK, span), 1)
    row = lax.broadcasted_iota(jnp.int32, (SB_BLOCK, span), 0)
    valid = col < (qi - sd * per) * SB_BLOCK + row
    q = q_ref[...] * SB_SCALE
    qhs = (jnp.where(lane < 64, q, 0.0).astype(BF16), jnp.where(lane >= 64, q, 0.0).astype(BF16))
    return lane, sd, valid, qhs


def _sb_fwd(p, *, name):
    s = p.shape[0]
    nq = s // SB_BLOCK
    kcol = BRANCH_W // SB_BLOCK
    span = min(SB_SPAN, s)

    def body(q_ref, k_ref, v_ref, o_ref):
        lane, sd, valid, qhs = _sb_setup(q_ref, span)
        after_ones = _cumsum_operand(True, terms=2)
        zero = jnp.zeros((SB_BLOCK, SB_BLOCK), F32)

        def span_step(sb, carry, mask):
            rows = pl.ds(pl.multiple_of(sb * span, span), span)
            kw = k_ref[rows, :].astype(BF16)
            vw = v_ref[rows, :].astype(BF16)
            out = []
            for h in range(2):
                run, acc = carry[h]
                _, a, run = _sb_scores(qhs[h], kw, run, mask, after_ones)
                out.append((run, acc + _dot(a.astype(BF16), vw)))
            return tuple(out)

        carry = span_step(sd, ((zero, zero), (zero, zero)), valid)
        carry = lax.fori_loop(0, sd, lambda t, c: span_step(sd - 1 - t, c, None), carry)
        o_ref[...] = jnp.where(lane < 64, carry[0][1], carry[1][1])

    return pl.pallas_call(
        body, name=name, grid=(kcol, nq),
        in_specs=[pl.BlockSpec((SB_BLOCK, SB_BLOCK), lambda hp, qi: (qi, hp)),
                  pl.BlockSpec((s, SB_BLOCK), lambda hp, qi: (0, kcol + hp)),
                  pl.BlockSpec((s, SB_BLOCK), lambda hp, qi: (0, 2 * kcol + hp))],
        out_specs=pl.BlockSpec((SB_BLOCK, SB_BLOCK), lambda hp, qi: (qi, hp)),
        out_shape=jax.ShapeDtypeStruct((s, BRANCH_W), F32),
        compiler_params=_cparams("parallel", "arbitrary"),
    )(p, p, p)


def _sb_bwd(p, dya, *, name):
    s = p.shape[0]
    nq = s // SB_BLOCK
    kcol = BRANCH_W // SB_BLOCK
    span = min(SB_SPAN, s)
    per = span // SB_BLOCK

    def body(q_ref, k_ref, v_ref, do_ref, dq_ref, dk_ref, dv_ref, a_s, b_s, dk_acc, dv_acc):
        qi = pl.program_id(1)

        @pl.when(qi == 0)
        def _():
            dk_acc[...] = jnp.zeros_like(dk_acc)
            dv_acc[...] = jnp.zeros_like(dv_acc)

        lane, sd, valid, qhs = _sb_setup(q_ref, span)
        after_ones = _cumsum_operand(True, terms=2)
        before_ones = _cumsum_operand(False)
        do = do_ref[...]
        dohs = (jnp.where(lane < 64, do, 0.0).astype(BF16), jnp.where(lane >= 64, do, 0.0).astype(BF16))
        zero = jnp.zeros((SB_BLOCK, SB_BLOCK), F32)

        def rebuild(sb, runs, mask):
            rows = pl.ds(pl.multiple_of(sb * span, span), span)
            kw = k_ref[rows, :].astype(BF16)
            out = []
            for h in range(2):
                lsp, a, run = _sb_scores(qhs[h], kw, runs[h], mask, after_ones)
                beta = jnp.exp(lsp)
                if mask is not None:
                    beta = jnp.where(mask, beta, 0.0)
                a_s[h, sb] = a
                b_s[h, sb] = beta
                out.append(run)
            return tuple(out)

        runs = rebuild(sd, (zero, zero), valid)
        lax.fori_loop(0, sd, lambda t, r: rebuild(sd - 1 - t, r, None), runs)

        def accumulate(sb, carry):
            rows = pl.ds(pl.multiple_of(sb * span, span), span)
            kw = k_ref[rows, :].astype(BF16)
            vw = v_ref[rows, :].astype(BF16)
            out = []
            dk_span = jnp.zeros((span, SB_BLOCK), F32)
            dv_span = jnp.zeros((span, SB_BLOCK), F32)
            for h in range(2):
                pg, dq = carry[h]
                a = a_s[h, sb]
                beta = b_s[h, sb]
                g = a * _dot_nt(dohs[h], vw)
                gb = g.astype(BF16)
                before = [None] * per
                for b in range(per):
                    cols = slice(b * SB_BLOCK, (b + 1) * SB_BLOCK)
                    gt = _dot(gb[:, cols], before_ones)
                    before[b] = pg + gt[:, :SB_BLOCK]
                    pg = pg + gt[:, SB_BLOCK:]
                dz = (g * (1.0 - beta) - beta * jnp.concatenate(before, axis=1)).astype(BF16)
                dk_span = dk_span + _dot_tn(dz, qhs[h])
                dv_span = dv_span + _dot_tn(a.astype(BF16), dohs[h])
                out.append((pg, dq + _dot(dz, kw)))
            dk_acc[rows, :] += dk_span
            dv_acc[rows, :] += dv_span
            return tuple(out)

        carry = lax.fori_loop(0, sd + 1, accumulate, ((zero, zero), (zero, zero)))
        dq_ref[...] = (jnp.where(lane < 64, carry[0][1], carry[1][1]) * SB_SCALE).astype(BF16)

        @pl.when(qi == nq - 1)
        def _():
            dk_ref[...] = dk_acc[...].astype(BF16)
            dv_ref[...] = dv_acc[...].astype(BF16)

    blk = pl.BlockSpec((SB_BLOCK, SB_BLOCK), lambda hp, qi: (qi, hp))
    col = pl.BlockSpec((s, SB_BLOCK), lambda hp, qi: (0, hp))
    out = jax.ShapeDtypeStruct((s, BRANCH_W), BF16)
    return pl.pallas_call(
        body, name=name, grid=(kcol, nq),
        in_specs=[blk,
                  pl.BlockSpec((s, SB_BLOCK), lambda hp, qi: (0, kcol + hp)),
                  pl.BlockSpec((s, SB_BLOCK), lambda hp, qi: (0, 2 * kcol + hp)),
                  blk],
        out_specs=[blk, col, col],
        out_shape=[out, out, out],
        scratch_shapes=[pltpu.VMEM((2, s // span, SB_BLOCK, span), F32), pltpu.VMEM((2, s // span, SB_BLOCK, span), F32),
                        pltpu.VMEM((s, SB_BLOCK), F32), pltpu.VMEM((s, SB_BLOCK), F32)],
        compiler_params=_cparams("parallel", "arbitrary"),
    )(p, p, p, dya)


_INV_SQRT2 = 0.7071067811865476
_INV_SQRT2PI = 0.3989422804014327


def _gelu(x):
    return 0.5 * x * (1.0 + lax.erf(x * _INV_SQRT2))


def _gelu_grad(x):
    return 0.5 * (1.0 + lax.erf(x * _INV_SQRT2)) + x * _INV_SQRT2PI * jnp.exp(-0.5 * x * x)


def _chunk_mask(transposed=False):
    r = lax.broadcasted_iota(jnp.int32, (SGU_LEN, SGU_LEN), 0)
    c = lax.broadcasted_iota(jnp.int32, (SGU_LEN, SGU_LEN), 1)
    return (c // 64) >= (r // 64) if transposed else (r // 64) >= (c // 64)


def _sgu_norm(v_raw, g, b):
    zv = _gelu(v_raw)
    xc = zv - jnp.mean(zv, axis=-1, keepdims=True)
    rs = lax.rsqrt(jnp.mean(xc * xc, axis=-1, keepdims=True) + LN_EPS)
    xh = xc * rs
    return xh, rs, xh * g + b


def _sgu_fwd(p, ln_g, ln_b, w, b_col, *, name):
    s = p.shape[0]
    tr = _pick(s, 512, SGU_LEN)

    def body(u_ref, v_ref, g_ref, b_ref, w_ref, bc_ref, o_ref):
        mask = _chunk_mask()
        zu = _gelu(u_ref[...])
        _, _, vn = _sgu_norm(v_ref[...], g_ref[...], b_ref[...])
        vnb = vn.astype(BF16)
        for gi in range(SGU_GROUPS):
            wg = jnp.where(mask, w_ref[gi], 0.0).astype(BF16)
            cs = slice(gi * SGU_LEN, (gi + 1) * SGU_LEN)
            for c in range(tr // SGU_LEN):
                rs_ = slice(c * SGU_LEN, (c + 1) * SGU_LEN)
                vm = _dot(wg, vnb[rs_, cs]) + bc_ref[gi]
                o_ref[rs_, cs] = zu[rs_, cs] * vm

    vec = pl.BlockSpec((1, BRANCH_W), lambda i: (0, 0))
    return pl.pallas_call(
        body, name=name, grid=(s // tr,),
        in_specs=[pl.BlockSpec((tr, BRANCH_W), lambda i: (i, 3)), pl.BlockSpec((tr, BRANCH_W), lambda i: (i, 4)),
                  vec, vec,
                  pl.BlockSpec((SGU_GROUPS, SGU_LEN, SGU_LEN), lambda i: (0, 0, 0)),
                  pl.BlockSpec((SGU_GROUPS, SGU_LEN, 1), lambda i: (0, 0, 0))],
        out_specs=pl.BlockSpec((tr, BRANCH_W), lambda i: (i, 0)),
        out_shape=jax.ShapeDtypeStruct((s, BRANCH_W), F32),
        compiler_params=_cparams("parallel"),
    )(p, p, ln_g, ln_b, w, b_col)


def _sgu_bwd(p, dyb, ln_g, ln_b, w, w_t, b_col, *, name):
    s = p.shape[0]
    tr = _pick(s, 256, SGU_LEN)

    def body(u_ref, v_ref, dy_ref, g_ref, b_ref, w_ref, wt_ref, bc_ref,
             dz_ref, dg_ref, db_ref, dw_ref, dbc_ref, dvn_s):
        @pl.when(pl.program_id(0) == 0)
        def _():
            dg_ref[...] = jnp.zeros_like(dg_ref)
            db_ref[...] = jnp.zeros_like(db_ref)
            dw_ref[...] = jnp.zeros_like(dw_ref)
            dbc_ref[...] = jnp.zeros_like(dbc_ref)

        mask = _chunk_mask()
        mask_t = _chunk_mask(transposed=True)
        u_raw = u_ref[...]
        v_raw = v_ref[...]
        dy = dy_ref[...]
        zu = _gelu(u_raw)
        xh, rs, vn = _sgu_norm(v_raw, g_ref[...], b_ref[...])
        vnb = vn.astype(BF16)
        dvm_all = dy * zu
        for gi in range(SGU_GROUPS):
            wg = jnp.where(mask, w_ref[gi], 0.0).astype(BF16)
            wgt = jnp.where(mask_t, wt_ref[gi], 0.0).astype(BF16)
            cs = slice(gi * SGU_LEN, (gi + 1) * SGU_LEN)
            dw_g = jnp.zeros((SGU_LEN, SGU_LEN), F32)
            db_g = jnp.zeros((SGU_LEN, 1), F32)
            for c in range(tr // SGU_LEN):
                rs_ = slice(c * SGU_LEN, (c + 1) * SGU_LEN)
                vm = _dot(wg, vnb[rs_, cs]) + bc_ref[gi]
                dz_ref[rs_, cs] = (dy[rs_, cs] * vm * _gelu_grad(u_raw[rs_, cs])).astype(BF16)
                dvm = dvm_all[rs_, cs]
                dvmb = dvm.astype(BF16)
                dw_g = dw_g + _dot_nt(dvmb, vnb[rs_, cs])
                db_g = db_g + jnp.sum(dvm, axis=1, keepdims=True)
                dvn_s[rs_, cs] = _dot(wgt, dvmb)
            dw_ref[gi] += jnp.where(mask, dw_g, 0.0)
            dbc_ref[gi] += db_g
        dvn = dvn_s[...]
        dg_ref[...] += jnp.sum(dvn * xh, axis=0, keepdims=True)
        db_ref[...] += jnp.sum(dvn, axis=0, keepdims=True)
        dxh = dvn * g_ref[...]
        dzv = rs * (dxh - jnp.mean(dxh, axis=-1, keepdims=True) - xh * jnp.mean(dxh * xh, axis=-1, keepdims=True))
        dz_ref[:, BRANCH_W:] = (dzv * _gelu_grad(v_raw)).astype(BF16)

    vec = pl.BlockSpec((1, BRANCH_W), lambda i: (0, 0))
    wspec = pl.BlockSpec((SGU_GROUPS, SGU_LEN, SGU_LEN), lambda i: (0, 0, 0))
    bspec = pl.BlockSpec((SGU_GROUPS, SGU_LEN, 1), lambda i: (0, 0, 0))
    return pl.pallas_call(
        body, name=name, grid=(s // tr,),
        in_specs=[pl.BlockSpec((tr, BRANCH_W), lambda i: (i, 3)), pl.BlockSpec((tr, BRANCH_W), lambda i: (i, 4)),
                  pl.BlockSpec((tr, BRANCH_W), lambda i: (i, 0)), vec, vec, wspec, wspec, bspec],
        out_specs=[pl.BlockSpec((tr, 2 * BRANCH_W), lambda i: (i, 0)), vec, vec, wspec, bspec],
        out_shape=[jax.ShapeDtypeStruct((s, 2 * BRANCH_W), BF16),
                   jax.ShapeDtypeStruct((1, BRANCH_W), F32), jax.ShapeDtypeStruct((1, BRANCH_W), F32),
                   jax.ShapeDtypeStruct((SGU_GROUPS, SGU_LEN, SGU_LEN), F32),
                   jax.ShapeDtypeStruct((SGU_GROUPS, SGU_LEN, 1), F32)],
        scratch_shapes=[pltpu.VMEM((tr, BRANCH_W), F32)],
        compiler_params=_cparams("arbitrary"),
    )(p, p, dyb, ln_g, ln_b, w, w_t, b_col)


def _shift_down(x, prev8, k):
    rolled = pltpu.roll(x, k, 0)
    r8 = lax.broadcasted_iota(jnp.int32, prev8.shape, 0)
    head = jnp.where(r8 < k, pltpu.roll(prev8, k, 0), rolled[:HALO])
    return jnp.concatenate([head, rolled[HALO:]], axis=0)


def _shift_up(x, next8, k):
    n = x.shape[0]
    rolled = pltpu.roll(x, n - k, 0)
    r8 = lax.broadcasted_iota(jnp.int32, next8.shape, 0)
    tail = jnp.where(r8 >= HALO - k, pltpu.roll(next8, HALO - k, 0), rolled[n - HALO:])
    return jnp.concatenate([rolled[:n - HALO], tail], axis=0)


def _conv_specs(s, tr):
    nb = tr // HALO
    last = s // HALO - 1
    tile = lambda cb: pl.BlockSpec((tr, 128), lambda j, i: (i, cb * 4 + j))
    above = lambda cb: pl.BlockSpec((HALO, 128), lambda j, i: (jnp.maximum(i * nb - 1, 0), cb * 4 + j))
    below = lambda cb: pl.BlockSpec((HALO, 128), lambda j, i: (jnp.minimum((i + 1) * nb, last), cb * 4 + j))
    return tile, above, below


def _conv_fwd(p, cw, *, name):
    s = p.shape[0]
    tr = _pick(s, 512, HALO)
    tile, above, _ = _conv_specs(s, tr)

    def body(cb_ref, cc_ref, cx_ref, ccp_ref, cxp_ref, w_ref, o_ref):
        first = pl.program_id(1) == 0
        y = cc_ref[...] * cx_ref[...]
        yp = jnp.where(first, 0.0, ccp_ref[...] * cxp_ref[...])
        conv = w_ref[2:3, :] * y + w_ref[1:2, :] * _shift_down(y, yp, 1) + w_ref[0:1, :] * _shift_down(y, yp, 2)
        o_ref[...] = cb_ref[...] * conv

    return pl.pallas_call(
        body, name=name, grid=(4, s // tr),
        in_specs=[tile(5), tile(6), tile(7), above(6), above(7), pl.BlockSpec((3, 128), lambda j, i: (0, j))],
        out_specs=pl.BlockSpec((tr, 128), lambda j, i: (i, j)),
        out_shape=jax.ShapeDtypeStruct((s, BRANCH_W), F32),
        compiler_params=_cparams("parallel", "parallel"),
    )(p, p, p, p, p, cw)


def _conv_bwd(p, dyc, cw, *, name):
    s = p.shape[0]
    tr = _pick(s, 512, HALO)
    nt = s // tr
    nb = tr // HALO
    last = s // HALO - 1
    tile, above, below = _conv_specs(s, tr)

    def body(cb_ref, cc_ref, cx_ref, ccp_ref, cxp_ref, cbn_ref, dy_ref, dyn_ref, w_ref,
             dcb_ref, dcc_ref, dcx_ref, dw_ref):
        i = pl.program_id(1)

        @pl.when(i == 0)
        def _():
            dw_ref[...] = jnp.zeros_like(dw_ref)

        cb = cb_ref[...]
        cc = cc_ref[...]
        cx = cx_ref[...]
        y = cc * cx
        yp = jnp.where(i == 0, 0.0, ccp_ref[...] * cxp_ref[...])
        y1 = _shift_down(y, yp, 1)
        y2 = _shift_down(y, yp, 2)
        w0, w1, w2 = w_ref[0:1, :], w_ref[1:2, :], w_ref[2:3, :]
        conv = w2 * y + w1 * y1 + w0 * y2
        dyc_v = dy_ref[...]
        dconv = dyc_v * cb
        dn = jnp.where(i == nt - 1, 0.0, dyn_ref[...] * cbn_ref[...])
        dyv = w2 * dconv + w1 * _shift_up(dconv, dn, 1) + w0 * _shift_up(dconv, dn, 2)
        dcb_ref[...] = (dyc_v * conv).astype(BF16)
        dcc_ref[...] = (dyv * cx).astype(BF16)
        dcx_ref[...] = (dyv * cc).astype(BF16)
        dw_ref[0:1, :] += jnp.sum(dconv * y2, axis=0, keepdims=True)
        dw_ref[1:2, :] += jnp.sum(dconv * y1, axis=0, keepdims=True)
        dw_ref[2:3, :] += jnp.sum(dconv * y, axis=0, keepdims=True)

    dy_tile = pl.BlockSpec((tr, 128), lambda j, i: (i, j))
    dy_below = pl.BlockSpec((HALO, 128), lambda j, i: (jnp.minimum((i + 1) * nb, last), j))
    out_tile = lambda cb: pl.BlockSpec((tr, 128), lambda j, i: (i, cb * 4 + j))
    w_spec = pl.BlockSpec((3, 128), lambda j, i: (0, j))
    dcb, dcc, dcx, dw = pl.pallas_call(
        body, name=name, grid=(4, nt),
        in_specs=[tile(5), tile(6), tile(7), above(6), above(7), below(5), dy_tile, dy_below, w_spec],
        out_specs=[dy_tile, dy_tile, dy_tile, w_spec],
        out_shape=[jax.ShapeDtypeStruct((s, BRANCH_W), BF16)] * 3 + [jax.ShapeDtypeStruct((3, BRANCH_W), F32)],
        compiler_params=_cparams("parallel", "arbitrary"),
    )(p, p, p, p, p, p, dyc, dyc, cw)
    return dcb, dcc, dcx, dw


def _merge_fwd(ya, yb, yc, wb, p, *, name):
    s = p.shape[0]
    tr = _pick(s, 256, 16)

    def body(ya_ref, yb_ref, yc_ref, wb_ref, g0_ref, g1_ref, g2_ref, o_ref):
        acc = jnp.zeros((tr, D_MODEL), F32)
        for n, (y_ref, g_ref) in enumerate(((ya_ref, g0_ref), (yb_ref, g1_ref), (yc_ref, g2_ref))):
            acc = acc + _sigmoid(g_ref[...]) * _dot(y_ref[...].astype(BF16), wb_ref[n])
        o_ref[...] = acc.astype(BF16)

    yspec = pl.BlockSpec((tr, BRANCH_W), lambda i: (i, 0))
    gate = lambda n: pl.BlockSpec((tr, D_MODEL), lambda i: (i, 4 + n))
    return pl.pallas_call(
        body, name=name, grid=(s // tr,),
        in_specs=[yspec, yspec, yspec, pl.BlockSpec((3, BRANCH_W, D_MODEL), lambda i: (0, 0, 0)),
                  gate(0), gate(1), gate(2)],
        out_specs=pl.BlockSpec((tr, D_MODEL), lambda i: (i, 0)),
        out_shape=jax.ShapeDtypeStruct((s, D_MODEL), BF16),
        compiler_params=_cparams("parallel"),
    )(ya, yb, yc, wb, p, p, p)


def _merge_bwd(dm, ya, yb, yc, wb, p, *, name):
    s = p.shape[0]
    tr = _pick(s, 256, 16)

    def body(dm_ref, ya_ref, yb_ref, yc_ref, wb_ref, g0_ref, g1_ref, g2_ref,
             dya_ref, dyb_ref, dyc_ref, dg_ref, dbrd_ref):
        dmv = dm_ref[...]
        ys = (ya_ref, yb_ref, yc_ref)
        gs = (g0_ref, g1_ref, g2_ref)
        dys = (dya_ref, dyb_ref, dyc_ref)
        for n in range(3):
            brd = _dot(ys[n][...].astype(BF16), wb_ref[n])
            sg = _sigmoid(gs[n][...])
            dbrd = (sg * dmv).astype(BF16)
            dbrd_ref[n] = dbrd
            dg_ref[:, n * D_MODEL:(n + 1) * D_MODEL] = (dmv * brd * sg * (1.0 - sg)).astype(BF16)
            dys[n][...] = _dot_nt(dbrd, wb_ref[n])

    yspec = pl.BlockSpec((tr, BRANCH_W), lambda i: (i, 0))
    gate = lambda n: pl.BlockSpec((tr, D_MODEL), lambda i: (i, 4 + n))
    row = pl.BlockSpec((tr, D_MODEL), lambda i: (i, 0))
    return pl.pallas_call(
        body, name=name, grid=(s // tr,),
        in_specs=[row, yspec, yspec, yspec, pl.BlockSpec((3, BRANCH_W, D_MODEL), lambda i: (0, 0, 0)),
                  gate(0), gate(1), gate(2)],
        out_specs=[yspec, yspec, yspec, pl.BlockSpec((tr, 3 * D_MODEL), lambda i: (i, 0)),
                   pl.BlockSpec((3, tr, D_MODEL), lambda i: (0, i, 0))],
        out_shape=[jax.ShapeDtypeStruct((s, BRANCH_W), F32)] * 3
                  + [jax.ShapeDtypeStruct((s, 3 * D_MODEL), BF16), jax.ShapeDtypeStruct((3, s, D_MODEL), BF16)],
        compiler_params=_cparams("parallel"),
    )(dm, ya, yb, yc, wb, p, p, p)


def _xa_probs(q, k):
    sc = _dot_nt(q, k) * XA_SCALE
    e = jnp.exp(sc - jnp.max(sc, axis=-1, keepdims=True))
    return e / jnp.sum(e, axis=-1, keepdims=True)


def _xa_fwd(q, k, v, *, name):
    s = q.shape[0]
    mt = k.shape[0]
    tr = _pick(s, 512, 16)

    def body(q_ref, k_ref, v_ref, o_ref):
        pr = _xa_probs(q_ref[...], k_ref[...])
        o_ref[...] = _dot(pr.astype(BF16), v_ref[...]).astype(BF16)

    qs = pl.BlockSpec((tr, XA_HEAD), lambda h, i: (i, h))
    ks = pl.BlockSpec((mt, XA_HEAD), lambda h, i: (0, h))
    return pl.pallas_call(
        body, name=name, grid=(D_MODEL // XA_HEAD, s // tr),
        in_specs=[qs, ks, ks], out_specs=qs,
        out_shape=jax.ShapeDtypeStruct((s, D_MODEL), BF16),
        compiler_params=_cparams("parallel", "parallel"),
    )(q, k, v)


def _xa_bwd(q, k, v, do, *, name):
    s = q.shape[0]
    mt = k.shape[0]
    tr = _pick(s, 512, 16)

    def body(q_ref, k_ref, v_ref, do_ref, dq_ref, dk_ref, dv_ref):
        @pl.when(pl.program_id(1) == 0)
        def _():
            dk_ref[...] = jnp.zeros_like(dk_ref)
            dv_ref[...] = jnp.zeros_like(dv_ref)

        qv = q_ref[...]
        kv = k_ref[...]
        dov = do_ref[...]
        pr = _xa_probs(qv, kv)
        dpr = _dot_nt(dov, v_ref[...])
        ds = (pr * (dpr - jnp.sum(dpr * pr, axis=-1, keepdims=True)) * XA_SCALE).astype(BF16)
        dq_ref[...] = _dot(ds, kv).astype(BF16)
        dk_ref[...] += _dot_tn(ds, qv)
        dv_ref[...] += _dot_tn(pr.astype(BF16), dov)

    qs = pl.BlockSpec((tr, XA_HEAD), lambda h, i: (i, h))
    ks = pl.BlockSpec((mt, XA_HEAD), lambda h, i: (0, h))
    return pl.pallas_call(
        body, name=name, grid=(D_MODEL // XA_HEAD, s // tr),
        in_specs=[qs, ks, ks, qs], out_specs=[qs, ks, ks],
        out_shape=[jax.ShapeDtypeStruct((s, D_MODEL), BF16), jax.ShapeDtypeStruct((mt, D_MODEL), F32),
                   jax.ShapeDtypeStruct((mt, D_MODEL), F32)],
        compiler_params=_cparams("parallel", "arbitrary"),
    )(q, k, v, do)


def _swiglu_fwd(a, b, *, name):
    s, f = a.shape
    tr = _pick(s, 256, 16)

    def body(a_ref, b_ref, o_ref):
        av = a_ref[...]
        o_ref[...] = (av * _sigmoid(av) * b_ref[...]).astype(BF16)

    spec = pl.BlockSpec((tr, f), lambda i: (i, 0))
    return pl.pallas_call(
        body, name=name, grid=(s // tr,), in_specs=[spec, spec], out_specs=spec,
        out_shape=jax.ShapeDtypeStruct((s, f), BF16), compiler_params=_cparams("parallel"),
    )(a, b)


def _swiglu_bwd(a, b, dh, *, name):
    s, f = a.shape
    tr = _pick(s, 256, 16)

    def body(a_ref, b_ref, dh_ref, da_ref, db_ref):
        av = a_ref[...]
        dhv = dh_ref[...]
        sg = _sigmoid(av)
        silu = av * sg
        da_ref[...] = (dhv * b_ref[...] * (sg + silu * (1.0 - sg))).astype(BF16)
        db_ref[...] = (dhv * silu).astype(BF16)

    spec = pl.BlockSpec((tr, f), lambda i: (i, 0))
    return pl.pallas_call(
        body, name=name, grid=(s // tr,), in_specs=[spec, spec, spec], out_specs=[spec, spec],
        out_shape=[jax.ShapeDtypeStruct((s, f), BF16)] * 2, compiler_params=_cparams("parallel"),
    )(a, b, dh)


def _adamw(w, g, m, v, *, name):
    r, c = w.shape
    tr = _pick(r, 512, 8)

    def body(w_ref, g_ref, m_ref, v_ref, d_ref, mo_ref, vo_ref):
        gv = g_ref[...]
        mn = ADAM_B1 * m_ref[...] + (1.0 - ADAM_B1) * gv
        vn = ADAM_B2 * v_ref[...] + (1.0 - ADAM_B2) * (gv * gv)
        m_hat = mn / (1.0 - ADAM_B1 ** ADAM_STEP)
        v_hat = vn / (1.0 - ADAM_B2 ** ADAM_STEP)
        d_ref[...] = -ADAM_LR * (m_hat / (jnp.sqrt(v_hat) + ADAM_EPS) + ADAM_WD * w_ref[...])
        mo_ref[...] = mn
        vo_ref[...] = vn

    spec = pl.BlockSpec((tr, c), lambda i: (i, 0))
    shp = jax.ShapeDtypeStruct((r, c), F32)
    return pl.pallas_call(
        body, name=name, grid=(r // tr,), in_specs=[spec] * 4, out_specs=[spec] * 3,
        out_shape=[shp] * 3, compiler_params=_cparams("parallel"),
    )(w, g, m, v)


def _position():
    return lax.axis_index("x"), lax.axis_index("y"), lax.axis_index("c")


def _all_gather(x, *, name):
    t, c_ = x.shape

    def body(x_ref, out_ref, send_sems, recv_sems, local_sem):
        x_, y_, c = _position()
        me, sibling = (x_, y_, c), (x_, y_, 1 - c)
        chips = [(1 - x_, y_), (x_, 1 - y_), (1 - x_, 1 - y_)]

        def block(px, py, pc):
            return out_ref.at[4 * px + 2 * py + pc]

        def copy(k, blk, to, src=None):
            return pltpu.make_async_remote_copy(
                src_ref=block(*blk) if src is None else src, dst_ref=block(*blk),
                send_sem=send_sems.at[k], recv_sem=recv_sems.at[k], device_id=to, device_id_type=MESH)

        mine = pltpu.make_async_copy(x_ref, block(*me), local_sem)
        mine.start()
        first = [copy(0, me, sibling, src=x_ref)]
        first += [copy(1 + j, me, (*chip, c), src=x_ref) for j, chip in enumerate(chips)]
        for cp in first:
            cp.start()
        passed = [copy(4 + j, (*chip, c), sibling) for j, chip in enumerate(chips)]
        for j, chip in enumerate(chips):
            copy(1 + j, (*chip, c), me).wait_recv()
            passed[j].start()
        copy(0, sibling, me).wait_recv()
        for j, chip in enumerate(chips):
            copy(4 + j, (*chip, 1 - c), me).wait_recv()
        for cp in first + passed:
            cp.wait_send()
        mine.wait()

    return pl.pallas_call(
        body, name=name,
        out_shape=jax.ShapeDtypeStruct((N_DEV, t, c_), x.dtype),
        in_specs=[pl.BlockSpec(memory_space=pl.ANY)],
        out_specs=pl.BlockSpec(memory_space=pl.ANY),
        scratch_shapes=[pltpu.SemaphoreType.DMA((7,)), pltpu.SemaphoreType.DMA((7,)), pltpu.SemaphoreType.DMA],
    )(x)


def _all_reduce_small(x, *, name):
    r, c_ = x.shape

    def body(x_ref, o_ref, buf, send_sems, recv_sems):
        x_, y_, c = _position()
        me = 4 * x_ + 2 * y_ + c
        buf[me] = x_ref[...]
        copies = []
        for k in range(1, N_DEV):
            to = (x_ ^ (k >> 2), y_ ^ ((k >> 1) & 1), c ^ (k & 1))
            copies.append(pltpu.make_async_remote_copy(
                src_ref=x_ref, dst_ref=buf.at[me], send_sem=send_sems.at[k - 1], recv_sem=recv_sems.at[k - 1],
                device_id=to, device_id_type=MESH))
        for cp in copies:
            cp.start()
        for k in range(1, N_DEV):
            src = me ^ k
            pltpu.make_async_remote_copy(
                src_ref=x_ref, dst_ref=buf.at[src], send_sem=send_sems.at[k - 1], recv_sem=recv_sems.at[k - 1],
                device_id=(x_, y_, c), device_id_type=MESH).wait_recv()
        for cp in copies:
            cp.wait_send()
        acc = buf[0]
        for d in range(1, N_DEV):
            acc = acc + buf[d]
        o_ref[...] = acc

    return pl.pallas_call(
        body, name=name,
        out_shape=jax.ShapeDtypeStruct((r, c_), F32),
        in_specs=[pl.BlockSpec(memory_space=pltpu.VMEM)],
        out_specs=pl.BlockSpec(memory_space=pltpu.VMEM),
        scratch_shapes=[pltpu.VMEM((N_DEV, r, c_), F32), pltpu.SemaphoreType.DMA((7,)), pltpu.SemaphoreType.DMA((7,))],
    )(x)


def _rs_pair_exchange(g8, *, name):
    _, t, c_ = g8.shape

    def body(g_ref, r_ref, send_sems, recv_sems):
        x_, y_, c = _position()
        copies = [pltpu.make_async_remote_copy(
            src_ref=g_ref.at[2 * ch + (1 - c)], dst_ref=r_ref.at[ch],
            send_sem=send_sems.at[ch], recv_sem=recv_sems.at[ch],
            device_id=(x_, y_, 1 - c), device_id_type=MESH) for ch in range(4)]
        for cp in copies:
            cp.start()
        for cp in copies:
            cp.wait()

    return pl.pallas_call(
        body, name=name,
        out_shape=jax.ShapeDtypeStruct((4, t, c_), g8.dtype),
        in_specs=[pl.BlockSpec(memory_space=pl.ANY)],
        out_specs=pl.BlockSpec(memory_space=pl.ANY),
        scratch_shapes=[pltpu.SemaphoreType.DMA((4,)), pltpu.SemaphoreType.DMA((4,))],
    )(g8)


def _pair_add(core, g8, recv, *, name):
    _, t, c_ = g8.shape
    tr = _pick(t, 512, 16)

    def body(core_ref, g_ref, r_ref, o_ref):
        o_ref[...] = (g_ref[...].astype(F32) + r_ref[...].astype(F32)).astype(o_ref.dtype)

    grid_spec = pltpu.PrefetchScalarGridSpec(
        num_scalar_prefetch=1, grid=(4, t // tr),
        in_specs=[pl.BlockSpec((None, tr, c_), lambda ch, i, core_ref: (2 * ch + core_ref[0], i, 0)),
                  pl.BlockSpec((None, tr, c_), lambda ch, i, core_ref: (ch, i, 0))],
        out_specs=pl.BlockSpec((None, tr, c_), lambda ch, i, core_ref: (ch, i, 0)))
    return pl.pallas_call(
        body, name=name, grid_spec=grid_spec,
        out_shape=jax.ShapeDtypeStruct((4, t, c_), g8.dtype),
        compiler_params=_cparams("parallel", "parallel"),
    )(core, g8, recv)


def _rs_chip_exchange(part, *, name):
    _, t, c_ = part.shape

    def body(p_ref, r_ref, send_sems, recv_sems, local_sem):
        x_, y_, c = _position()
        mine = 2 * x_ + y_
        local = pltpu.make_async_copy(p_ref.at[mine], r_ref.at[mine], local_sem)
        local.start()
        chips = [(1 - x_, y_), (x_, 1 - y_), (1 - x_, 1 - y_)]
        copies = [pltpu.make_async_remote_copy(
            src_ref=p_ref.at[2 * px + py], dst_ref=r_ref.at[mine],
            send_sem=send_sems.at[k], recv_sem=recv_sems.at[k],
            device_id=(px, py, c), device_id_type=MESH) for k, (px, py) in enumerate(chips)]
        for cp in copies:
            cp.start()
        for k, (px, py) in enumerate(chips):
            pltpu.make_async_remote_copy(
                src_ref=p_ref.at[mine], dst_ref=r_ref.at[2 * px + py],
                send_sem=send_sems.at[k], recv_sem=recv_sems.at[k],
                device_id=(x_, y_, c), device_id_type=MESH).wait_recv()
        for cp in copies:
            cp.wait_send()
        local.wait()

    return pl.pallas_call(
        body, name=name,
        out_shape=jax.ShapeDtypeStruct((4, t, c_), part.dtype),
        in_specs=[pl.BlockSpec(memory_space=pl.ANY)],
        out_specs=pl.BlockSpec(memory_space=pl.ANY),
        scratch_shapes=[pltpu.SemaphoreType.DMA((3,)), pltpu.SemaphoreType.DMA((3,)), pltpu.SemaphoreType.DMA],
    )(part)


def _sum_chips(r4, *, name):
    _, t, c_ = r4.shape
    tr = _pick(t, 512, 16)

    def body(r_ref, o_ref):
        acc = r_ref[0].astype(F32)
        for ch in range(1, 4):
            acc = acc + r_ref[ch].astype(F32)
        o_ref[...] = acc

    return pl.pallas_call(
        body, name=name, grid=(t // tr,),
        in_specs=[pl.BlockSpec((4, tr, c_), lambda i: (0, i, 0))],
        out_specs=pl.BlockSpec((tr, c_), lambda i: (i, 0)),
        out_shape=jax.ShapeDtypeStruct((t, c_), F32),
        compiler_params=_cparams("parallel"),
    )(r4)


BIG = (
    ("w_in", (DEPTH, D_MODEL, IN_COLS // N_DEV), 2),
    ("w_branch", (DEPTH, 3, BRANCH_W, D_MODEL // N_DEV), 3),
    ("w_out", (DEPTH, D_MODEL // N_DEV, D_MODEL), 1),
    ("w_q_xa", (DEPTH, D_MODEL // N_DEV, D_MODEL), 1),
    ("w_k_xa", (DEPTH, D_MODEL // N_DEV, D_MODEL), 1),
    ("w_v_xa", (DEPTH, D_MODEL // N_DEV, D_MODEL), 1),
    ("w_o_xa", (DEPTH, D_MODEL // N_DEV, D_MODEL), 1),
    ("w_gate_ffn", (DEPTH, D_MODEL, FFN // N_DEV), 2),
    ("w_up_ffn", (DEPTH, D_MODEL, FFN // N_DEV), 2),
    ("w_down_ffn", (DEPTH, FFN // N_DEV, D_MODEL), 1),
)
PACK_COLS = 1024


def _size(shape):
    n = 1
    for d in shape:
        n *= d
    return n


def _pack_shards(shards):
    return jnp.concatenate([shards[n].reshape(-1, PACK_COLS) for n, _, _ in BIG], axis=0)


def _unpack_gathered(g):
    out = {}
    r0 = 0
    for n, shp, ax in BIG:
        rows = _size(shp) // PACK_COLS
        blk = g[:, r0:r0 + rows].reshape((N_DEV,) + shp)
        r0 += rows
        blk = jnp.moveaxis(blk, 0, ax)
        full = list(shp)
        full[ax] = shp[ax] * N_DEV
        out[n] = blk.reshape(full)
    return out


def _pack_full(full):
    parts = []
    for n, shp, ax in BIG:
        t = full[n].reshape(shp[:ax] + (N_DEV, shp[ax]) + shp[ax + 1:])
        t = jnp.moveaxis(t, ax, 0)
        parts.append(t.reshape(N_DEV, -1, PACK_COLS))
    return jnp.concatenate(parts, axis=1)


def _unpack_shard(flat):
    out = {}
    r0 = 0
    for n, shp, _ in BIG:
        rows = _size(shp) // PACK_COLS
        out[n] = flat[r0:r0 + rows].reshape(shp)
        r0 += rows
    return out


SMALL = (
    ("norm_mix_g", (DEPTH, D_MODEL)),
    ("sgu_ln_g", (DEPTH, BRANCH_W)),
    ("sgu_ln_b", (DEPTH, BRANCH_W)),
    ("w_spatial", (DEPTH, SGU_GROUPS, SGU_LEN, SGU_LEN)),
    ("b_spatial", (DEPTH, SGU_GROUPS, SGU_LEN)),
    ("conv_w", (DEPTH, 3, BRANCH_W)),
    ("norm_xa_g", (DEPTH, D_MODEL)),
    ("mem_norm_g", (DEPTH, D_MODEL)),
    ("norm_ffn_g", (DEPTH, D_MODEL)),
    ("final_g", (D_MODEL,)),
)


def _pack_small(grads):
    flat = jnp.concatenate([grads[n].reshape(-1) for n, _ in SMALL])
    rows = -(-flat.shape[0] // PACK_COLS)
    rows = -(-rows // 8) * 8
    flat = jnp.pad(flat, (0, rows * PACK_COLS - flat.shape[0]))
    return flat.reshape(rows, PACK_COLS)


def _unpack_small(buf):
    flat = buf.reshape(-1)
    out = {}
    o = 0
    for n, shp in SMALL:
        out[n] = flat[o:o + _size(shp)].reshape(shp)
        o += _size(shp)
    return out


def _layer_fwd(l, x, mem, wt, sm):
    t = f"l{l}_"
    sv = {"x0": x}
    h = _rms_fwd(x, sm["norm_mix_g"][l][None], name=t + "rms_mix")
    p = _mm(h, wt["w_in"][l], name=t + "in_proj", tn=1024)
    ya = _sb_fwd(p, name=t + "sb_fwd")
    w_sp = sm["w_spatial"][l]
    b_col = sm["b_spatial"][l][:, :, None]
    ln_g, ln_b = sm["sgu_ln_g"][l][None], sm["sgu_ln_b"][l][None]
    yb = _sgu_fwd(p, ln_g, ln_b, w_sp, b_col, name=t + "sgu_fwd")
    yc = _conv_fwd(p, sm["conv_w"][l], name=t + "conv_fwd")
    merged = _merge_fwd(ya, yb, yc, wt["w_branch"][l], p, name=t + "merge_fwd")
    x1 = _mm(merged, wt["w_out"][l], add=x, name=t + "out_proj")
    sv.update(h=h, p=p, ya=ya, yb=yb, yc=yc, merged=merged, x1=x1)

    h2 = _rms_fwd(x1, sm["norm_xa_g"][l][None], name=t + "rms_xa")
    mn = _rms_fwd(mem, sm["mem_norm_g"][l][None], name=t + "rms_mem")
    q = _mm(h2, wt["w_q_xa"][l], out_dtype=BF16, name=t + "xa_q")
    k = _mm(mn, wt["w_k_xa"][l], out_dtype=BF16, name=t + "xa_k")
    v = _mm(mn, wt["w_v_xa"][l], out_dtype=BF16, name=t + "xa_v")
    o = _xa_fwd(q, k, v, name=t + "xa_fwd")
    x2 = _mm(o, wt["w_o_xa"][l], add=x1, name=t + "xa_o")
    sv.update(h2=h2, mn=mn, q=q, k=k, v=v, o=o, x2=x2)

    h3 = _rms_fwd(x2, sm["norm_ffn_g"][l][None], name=t + "rms_ffn")
    a = _mm(h3, wt["w_gate_ffn"][l], name=t + "ffn_gate", tn=1408)
    b = _mm(h3, wt["w_up_ffn"][l], name=t + "ffn_up", tn=1408)
    hd = _swiglu_fwd(a, b, name=t + "swiglu_fwd")
    x3 = _mm(hd, wt["w_down_ffn"][l], add=x2, name=t + "ffn_down")
    sv.update(h3=h3, a=a, b=b, hd=hd)
    return x3, sv


def _layer_bwd(l, dx3, mem, wt, sm, sv):
    t = f"l{l}_b_"
    gb, gs = {}, {}
    dhd = _mm(dx3, wt["w_down_ffn"][l], tb=True, name=t + "ffn_down_dx", tn=1408)
    gb["w_down_ffn"] = _mm(sv["hd"], dx3, ta=True, out_dtype=BF16, name=t + "ffn_down_dw", tm=1408)
    da, db = _swiglu_bwd(sv["a"], sv["b"], dhd, name=t + "swiglu_bwd")
    dh3 = _mm(da, wt["w_gate_ffn"][l], tb=True, name=t + "ffn_gate_dx")
    dh3 = _mm(db, wt["w_up_ffn"][l], tb=True, add=dh3, name=t + "ffn_up_dx")
    gb["w_gate_ffn"] = _mm(sv["h3"], da, ta=True, out_dtype=BF16, name=t + "ffn_gate_dw", tn=1408)
    gb["w_up_ffn"] = _mm(sv["h3"], db, ta=True, out_dtype=BF16, name=t + "ffn_up_dw", tn=1408)
    dx2, dg = _rms_bwd(sv["x2"], sm["norm_ffn_g"][l][None], dh3, dx3, name=t + "rms_ffn")
    gs["norm_ffn_g"] = dg[0]
    do = _mm(dx2, wt["w_o_xa"][l], tb=True, out_dtype=BF16, name=t + "xa_o_dx")
    gb["w_o_xa"] = _mm(sv["o"], dx2, ta=True, out_dtype=BF16, name=t + "xa_o_dw")
    dq, dk, dv = _xa_bwd(sv["q"], sv["k"], sv["v"], do, name=t + "xa_bwd")
    dh2 = _mm(dq, wt["w_q_xa"][l], tb=True, name=t + "xa_q_dx")
    gb["w_q_xa"] = _mm(sv["h2"], dq, ta=True, out_dtype=BF16, name=t + "xa_q_dw")
    gb["w_k_xa"] = _mm(sv["mn"], dk, ta=True, out_dtype=BF16, name=t + "xa_k_dw")
    gb["w_v_xa"] = _mm(sv["mn"], dv, ta=True, out_dtype=BF16, name=t + "xa_v_dw")
    dmn = _mm(dk, wt["w_k_xa"][l], tb=True, name=t + "xa_k_dx")
    dmn = _mm(dv, wt["w_v_xa"][l], tb=True, add=dmn, name=t + "xa_v_dx")
    _, dg = _rms_bwd(mem, sm["mem_norm_g"][l][None], dmn, jnp.zeros_like(mem), name=t + "rms_mem")
    gs["mem_norm_g"] = dg[0]
    dx1, dg = _rms_bwd(sv["x1"], sm["norm_xa_g"][l][None], dh2, dx2, name=t + "rms_xa")
    gs["norm_xa_g"] = dg[0]
    dm = _mm(dx1, wt["w_out"][l], tb=True, name=t + "out_proj_dx")
    gb["w_out"] = _mm(sv["merged"], dx1, ta=True, out_dtype=BF16, name=t + "out_proj_dw")
    p = sv["p"]
    dya, dyb, dyc, dgates, dbrd = _merge_bwd(dm, sv["ya"], sv["yb"], sv["yc"], wt["w_branch"][l], p,
                                             name=t + "merge_bwd")
    gb["w_branch"] = jnp.stack([
        _mm(sv[y], dbrd[n], ta=True, out_dtype=BF16, name=t + f"branch{n}_dw")
        for n, y in enumerate(("ya", "yb", "yc"))])
    dcb, dcc, dcx, dcw = _conv_bwd(p, dyc, sm["conv_w"][l], name=t + "conv_bwd")
    gs["conv_w"] = dcw
    w_sp = sm["w_spatial"][l]
    dz, dlg, dlb, dwsp, dbsp = _sgu_bwd(p, dyb, sm["sgu_ln_g"][l][None], sm["sgu_ln_b"][l][None], w_sp,
                                        jnp.swapaxes(w_sp, 1, 2), sm["b_spatial"][l][:, :, None],
                                        name=t + "sgu_bwd")
    gs.update(sgu_ln_g=dlg[0], sgu_ln_b=dlb[0], w_spatial=dwsp, b_spatial=dbsp[:, :, 0])
    dq_a, dk_a, dv_a = _sb_bwd(p, dya, name=t + "sb_bwd")
    dp = jnp.concatenate([dq_a, dk_a, dv_a, dz, dcb, dcc, dcx, dgates], axis=1)
    dh = _mm(dp, wt["w_in"][l], tb=True, name=t + "in_proj_dx")
    gb["w_in"] = _mm(sv["h"], dp, ta=True, out_dtype=BF16, name=t + "in_proj_dw", tn=1024)
    dx, dg = _rms_bwd(sv["x0"], sm["norm_mix_g"][l][None], dh, dx1, name=t + "rms_mix")
    gs["norm_mix_g"] = dg[0]
    return dx, gb, gs


def _local_step(x, mem, target, wt, sm):
    saved = []
    cur = x
    for l in range(DEPTH):
        cur, sv = _layer_fwd(l, cur, mem, wt, sm)
        saved.append(sv)
    dcur, loss, dfinal = _final_loss(cur, sm["final_g"][None], target, name="final_loss")
    gbs, gss = [None] * DEPTH, [None] * DEPTH
    for l in reversed(range(DEPTH)):
        dcur, gbs[l], gss[l] = _layer_bwd(l, dcur, mem, wt, sm, saved[l])
    big = {n: jnp.stack([gbs[l][n] for l in range(DEPTH)]) for n, _, _ in BIG}
    small = {n: jnp.stack([gss[l][n] for l in range(DEPTH)]) for n, _ in SMALL if n != "final_g"}
    small["final_g"] = dfinal[0]
    return loss, dcur, big, small


_WEIGHTS = ("norm_mix_g", "w_in", "sgu_ln_g", "sgu_ln_b", "w_spatial", "b_spatial", "conv_w", "w_branch", "w_out",
            "norm_xa_g", "mem_norm_g", "w_q_xa", "w_k_xa", "w_v_xa", "w_o_xa", "norm_ffn_g", "w_gate_ffn",
            "w_up_ffn", "w_down_ffn", "final_g")


def kernel(x, mem, norm_mix_g, w_in, sgu_ln_g, sgu_ln_b, w_spatial, b_spatial, conv_w, w_branch, w_out, norm_xa_g, mem_norm_g, w_q_xa, w_k_xa, w_v_xa, w_o_xa, norm_ffn_g, w_gate_ffn, w_up_ffn, w_down_ffn, final_g, loss_target, m_norm_mix_g, m_w_in, m_sgu_ln_g, m_sgu_ln_b, m_w_spatial, m_b_spatial, m_conv_w, m_w_branch, m_w_out, m_norm_xa_g, m_mem_norm_g, m_w_q_xa, m_w_k_xa, m_w_v_xa, m_w_o_xa, m_norm_ffn_g, m_w_gate_ffn, m_w_up_ffn, m_w_down_ffn, m_final_g, v_norm_mix_g, v_w_in, v_sgu_ln_g, v_sgu_ln_b, v_w_spatial, v_b_spatial, v_conv_w, v_w_branch, v_w_out, v_norm_xa_g, v_mem_norm_g, v_w_q_xa, v_w_k_xa, v_w_v_xa, v_w_o_xa, v_norm_ffn_g, v_w_gate_ffn, v_w_up_ffn, v_w_down_ffn, v_final_g):
    w = dict(norm_mix_g=norm_mix_g, w_in=w_in, sgu_ln_g=sgu_ln_g, sgu_ln_b=sgu_ln_b, w_spatial=w_spatial,
             b_spatial=b_spatial, conv_w=conv_w, w_branch=w_branch, w_out=w_out, norm_xa_g=norm_xa_g,
             mem_norm_g=mem_norm_g, w_q_xa=w_q_xa, w_k_xa=w_k_xa, w_v_xa=w_v_xa, w_o_xa=w_o_xa,
             norm_ffn_g=norm_ffn_g, w_gate_ffn=w_gate_ffn, w_up_ffn=w_up_ffn, w_down_ffn=w_down_ffn, final_g=final_g)
    m = dict(norm_mix_g=m_norm_mix_g, w_in=m_w_in, sgu_ln_g=m_sgu_ln_g, sgu_ln_b=m_sgu_ln_b, w_spatial=m_w_spatial,
             b_spatial=m_b_spatial, conv_w=m_conv_w, w_branch=m_w_branch, w_out=m_w_out, norm_xa_g=m_norm_xa_g,
             mem_norm_g=m_mem_norm_g, w_q_xa=m_w_q_xa, w_k_xa=m_w_k_xa, w_v_xa=m_w_v_xa, w_o_xa=m_w_o_xa,
             norm_ffn_g=m_norm_ffn_g, w_gate_ffn=m_w_gate_ffn, w_up_ffn=m_w_up_ffn, w_down_ffn=m_w_down_ffn,
             final_g=m_final_g)
    v = dict(norm_mix_g=v_norm_mix_g, w_in=v_w_in, sgu_ln_g=v_sgu_ln_g, sgu_ln_b=v_sgu_ln_b, w_spatial=v_w_spatial,
             b_spatial=v_b_spatial, conv_w=v_conv_w, w_branch=v_w_branch, w_out=v_w_out, norm_xa_g=v_norm_xa_g,
             mem_norm_g=v_mem_norm_g, w_q_xa=v_w_q_xa, w_k_xa=v_w_k_xa, w_v_xa=v_w_v_xa, w_o_xa=v_w_o_xa,
             norm_ffn_g=v_norm_ffn_g, w_gate_ffn=v_w_gate_ffn, w_up_ffn=v_w_up_ffn, w_down_ffn=v_w_down_ffn,
             final_g=v_final_g)

    packed = _pack_shards({n: w[n].astype(BF16) for n, _, _ in BIG})
    wt = _unpack_gathered(_all_gather(packed, name="gather_weights"))
    cw_pad = jnp.zeros((8, 128), F32).at[:DEPTH * 3, :BRANCH_W // N_DEV].set(conv_w.reshape(DEPTH * 3, -1))
    cw_all = _all_gather(cw_pad, name="gather_conv_w")[:, :DEPTH * 3, :BRANCH_W // N_DEV]
    conv_full = jnp.moveaxis(cw_all.reshape(N_DEV, DEPTH, 3, BRANCH_W // N_DEV), 0, 2).reshape(DEPTH, 3, BRANCH_W)
    sm = {n: w[n] for n, _ in SMALL}
    sm["conv_w"] = conv_full

    loss, dx, big, small = _local_step(x[0], mem[0], loss_target[0], wt, sm)
    loss = lax.psum(loss[0, 0], AXES)

    g8 = _pack_full(big)
    core = lax.axis_index("c").astype(jnp.int32).reshape(1)
    from_sibling = _rs_pair_exchange(g8, name="rs_pair_exchange")
    part = _pair_add(core, g8, from_sibling, name="rs_pair_add")
    by_chip = _rs_chip_exchange(part, name="rs_chip_exchange")
    grads = _unpack_shard(_sum_chips(by_chip, name="rs_sum_chips"))
    small_sum = _unpack_small(_all_reduce_small(_pack_small(small), name="all_reduce_small"))
    width = BRANCH_W // N_DEV
    dev = 4 * lax.axis_index("x") + 2 * lax.axis_index("y") + lax.axis_index("c")
    for n, _ in SMALL:
        grads[n] = small_sum[n]
    grads["conv_w"] = lax.dynamic_slice_in_dim(small_sum["conv_w"], dev * width, width, axis=2)

    delta, new_m, new_v = {}, {}, {}
    for n in _WEIGHTS:
        shp = w[n].shape
        two_d = (-1, shp[-1])
        d_, m_, v_ = _adamw(w[n].reshape(two_d), grads[n].reshape(two_d), m[n].reshape(two_d), v[n].reshape(two_d),
                            name="adamw_" + n)
        delta[n], new_m[n], new_v[n] = d_.reshape(shp), m_.reshape(shp), v_.reshape(shp)

    return (loss, dx[None], *[grads[n] for n in _WEIGHTS], *[delta[n] for n in _WEIGHTS],
            *[new_m[n] for n in _WEIGHTS], *[new_v[n] for n in _WEIGHTS])
```

```python
import functools

import jax
import jax.numpy as jnp
from jax import lax
from jax.experimental import pallas as pl
from jax.experimental.pallas import tpu as pltpu

F32 = jnp.float32
BF16 = jnp.bfloat16
MESH = pl.DeviceIdType.MESH

D_MODEL = 1024
BRANCH_W = 512
IN_COLS = 7168
FFN = 2816
N_DEV = 8
DEPTH = 2
SB_BLOCK = 128
SB_SPAN = 1024
SB_SCALE = 0.125
XA_HEAD = 256
XA_SCALE = 0.0625
SGU_LEN = 128
SGU_GROUPS = 4
RMS_EPS = 1e-6
LN_EPS = 1e-5
HALO = 8

ADAM_LR = 0.001
ADAM_B1 = 0.9
ADAM_B2 = 0.999
ADAM_EPS = 1e-08
ADAM_WD = 0.01
ADAM_STEP = 10

VMEM_LIMIT_BYTES = 52 * 1024 * 1024

AXES = ("x", "y", "c")


def _cparams(*sem):
    return pltpu.CompilerParams(dimension_semantics=sem, vmem_limit_bytes=VMEM_LIMIT_BYTES)


def _pick(n, target, align):
    t = (min(target, n) // align) * align
    while t >= align:
        if n % t == 0:
            return t
        t -= align
    return n


def _dot(a, b):
    return jnp.dot(a, b, preferred_element_type=F32)


def _dot_nt(a, b):
    return lax.dot_general(a, b, (((1,), (1,)), ((), ())), preferred_element_type=F32)


def _dot_tn(a, b):
    return lax.dot_general(a, b, (((0,), (0,)), ((), ())), preferred_element_type=F32)


def _sigmoid(x):
    return 1.0 / (1.0 + jnp.exp(-x))


def _mm(a, b, *, name, ta=False, tb=False, out_dtype=F32, add=None, tm=512, tn=512, tk=2048):
    m, k = (a.shape[1], a.shape[0]) if ta else a.shape
    n = b.shape[0] if tb else b.shape[1]
    assert k == (b.shape[1] if tb else b.shape[0])
    tm = _pick(m, tm, 128)
    tn = _pick(n, tn, 128)
    tk = _pick(k, tk, 128)
    nk = k // tk
    ca = 0 if ta else 1
    cb = 1 if tb else 0

    def body(*refs):
        if add is None:
            a_ref, b_ref, o_ref, acc_ref = refs
            add_ref = None
        else:
            a_ref, b_ref, add_ref, o_ref, acc_ref = refs
        kk = pl.program_id(2)

        @pl.when(kk == 0)
        def _():
            acc_ref[...] = jnp.zeros_like(acc_ref)

        acc_ref[...] += lax.dot_general(
            a_ref[...].astype(BF16), b_ref[...].astype(BF16),
            (((ca,), (cb,)), ((), ())), preferred_element_type=F32)

        @pl.when(kk == nk - 1)
        def _():
            r = acc_ref[...]
            if add_ref is not None:
                r = r + add_ref[...]
            o_ref[...] = r.astype(out_dtype)

    a_spec = pl.BlockSpec((tk, tm), lambda i, j, kk: (kk, i)) if ta else pl.BlockSpec((tm, tk), lambda i, j, kk: (i, kk))
    b_spec = pl.BlockSpec((tn, tk), lambda i, j, kk: (j, kk)) if tb else pl.BlockSpec((tk, tn), lambda i, j, kk: (kk, j))
    in_specs = [a_spec, b_spec]
    operands = [a, b]
    if add is not None:
        in_specs.append(pl.BlockSpec((tm, tn), lambda i, j, kk: (i, j)))
        operands.append(add)
    return pl.pallas_call(
        body, name=name,
        grid=(m // tm, n // tn, nk),
        in_specs=in_specs,
        out_specs=pl.BlockSpec((tm, tn), lambda i, j, kk: (i, j)),
        out_shape=jax.ShapeDtypeStruct((m, n), out_dtype),
        scratch_shapes=[pltpu.VMEM((tm, tn), F32)],
        compiler_params=_cparams("parallel", "parallel", "arbitrary"),
    )(*operands)


def _rms_fwd(x, g, *, name):
    r, d = x.shape
    tr = _pick(r, 512, 16)

    def body(x_ref, g_ref, o_ref):
        xv = x_ref[...]
        rs = lax.rsqrt(jnp.mean(xv * xv, axis=-1, keepdims=True) + RMS_EPS)
        o_ref[...] = (xv * rs * g_ref[...]).astype(BF16)

    return pl.pallas_call(
        body, name=name, grid=(r // tr,),
        in_specs=[pl.BlockSpec((tr, d), lambda i: (i, 0)), pl.BlockSpec((1, d), lambda i: (0, 0))],
        out_specs=pl.BlockSpec((tr, d), lambda i: (i, 0)),
        out_shape=jax.ShapeDtypeStruct((r, d), BF16),
        compiler_params=_cparams("parallel"),
    )(x, g)


def _rms_bwd(x, g, dh, dres, *, name):
    r, d = x.shape
    tr = _pick(r, 256, 8)

    def body(x_ref, g_ref, dh_ref, dres_ref, dx_ref, dg_ref):
        @pl.when(pl.program_id(0) == 0)
        def _():
            dg_ref[...] = jnp.zeros_like(dg_ref)

        xv = x_ref[...]
        dhv = dh_ref[...].astype(F32)
        rs = lax.rsqrt(jnp.mean(xv * xv, axis=-1, keepdims=True) + RMS_EPS)
        xh = xv * rs
        dg_ref[...] += jnp.sum(dhv * xh, axis=0, keepdims=True)
        dxh = dhv * g_ref[...]
        dx_ref[...] = dres_ref[...] + rs * (dxh - xh * jnp.mean(dxh * xh, axis=-1, keepdims=True))

    return pl.pallas_call(
        body, name=name, grid=(r // tr,),
        in_specs=[pl.BlockSpec((tr, d), lambda i: (i, 0)), pl.BlockSpec((1, d), lambda i: (0, 0)),
                  pl.BlockSpec((tr, d), lambda i: (i, 0)), pl.BlockSpec((tr, d), lambda i: (i, 0))],
        out_specs=[pl.BlockSpec((tr, d), lambda i: (i, 0)), pl.BlockSpec((1, d), lambda i: (0, 0))],
        out_shape=[jax.ShapeDtypeStruct((r, d), F32), jax.ShapeDtypeStruct((1, d), F32)],
        compiler_params=_cparams("arbitrary"),
    )(x, g, dh, dres)


def _final_loss(x, g, target, *, name):
    r, d = x.shape
    tr = _pick(r, 256, 8)

    def body(x_ref, g_ref, t_ref, dx_ref, loss_ref, dg_ref):
        @pl.when(pl.program_id(0) == 0)
        def _():
            dg_ref[...] = jnp.zeros_like(dg_ref)
            loss_ref[...] = jnp.zeros_like(loss_ref)

        xv = x_ref[...]
        gv = g_ref[...]
        rs = lax.rsqrt(jnp.mean(xv * xv, axis=-1, keepdims=True) + RMS_EPS)
        xh = xv * rs
        err = xh * gv - t_ref[...]
        row_loss = jnp.mean(err * err, axis=-1, keepdims=True)
        loss_ref[...] += 0.5 * jnp.sum(row_loss, axis=0, keepdims=True)
        dy = err * (1.0 / d)
        dg_ref[...] += jnp.sum(dy * xh, axis=0, keepdims=True)
        dxh = dy * gv
        dx_ref[...] = rs * (dxh - xh * jnp.mean(dxh * xh, axis=-1, keepdims=True))

    return pl.pallas_call(
        body, name=name, grid=(r // tr,),
        in_specs=[pl.BlockSpec((tr, d), lambda i: (i, 0)), pl.BlockSpec((1, d), lambda i: (0, 0)),
                  pl.BlockSpec((tr, d), lambda i: (i, 0))],
        out_specs=[pl.BlockSpec((tr, d), lambda i: (i, 0)), pl.BlockSpec((1, 128), lambda i: (0, 0)),
                   pl.BlockSpec((1, d), lambda i: (0, 0))],
        out_shape=[jax.ShapeDtypeStruct((r, d), F32), jax.ShapeDtypeStruct((1, 128), F32),
                   jax.ShapeDtypeStruct((1, d), F32)],
        compiler_params=_cparams("arbitrary"),
    )(x, g, target)


def _cumsum_operand(strict_after, terms=1):
    r = lax.broadcasted_iota(jnp.int32, (terms * SB_BLOCK, 2 * SB_BLOCK), 0) % SB_BLOCK
    c = lax.broadcasted_iota(jnp.int32, (terms * SB_BLOCK, 2 * SB_BLOCK), 1)
    tri = (r > c) if strict_after else (r < c)
    return jnp.where((c >= SB_BLOCK) | tri, 1.0, 0.0).astype(BF16)


def _sb_scores(qh, kw, run, valid, after_ones):
    nb = kw.shape[0] // SB_BLOCK
    z = _dot_nt(qh, kw)
    lsp = jnp.minimum(z, 0.0) - jnp.log(1.0 + jnp.exp(-jnp.abs(z)))
    l1m = lsp - z
    if valid is not None:
        l1m = jnp.where(valid, l1m, 0.0)
    hi = l1m.astype(BF16)
    lo = (l1m - hi.astype(F32)).astype(BF16)
    later = [None] * nb
    for b in reversed(range(nb)):
        cols = slice(b * SB_BLOCK, (b + 1) * SB_BLOCK)
        ct = _dot(jnp.concatenate([hi[:, cols], lo[:, cols]], axis=1), after_ones)
        later[b] = run + ct[:, :SB_BLOCK]
        run = run + ct[:, SB_BLOCK:]
    a = jnp.exp(lsp + jnp.concatenate(later, axis=1))
    if valid is not None:
        a = jnp.where(valid, a, 0.0)
    return lsp, a, run


def _sb_setup(q_ref, span):
    qi = pl.program_id(1)
    per = span // SB_BLOCK
    sd = qi // per
    lane = lax.broadcasted_iota(jnp.int32, (SB_BLOCK, SB_BLOCK), 1)
    col = lax.broadcasted_iota(jnp.int32, (SB_BLOCK, span), 1)
    row = lax.broadcasted_iota(jnp.int32, (SB_BLOCK, span), 0)
    valid = col < (qi - sd * per) * SB_BLOCK + row
    q = q_ref[...] * SB_SCALE
    qhs = (jnp.where(lane < 64, q, 0.0).astype(BF16), jnp.where(lane >= 64, q, 0.0).astype(BF16))
    return lane, sd, valid, qhs


def _sb_fwd(p, *, name, gather=None):
    s = p.shape[0]
    nq = s // SB_BLOCK
    kcol = BRANCH_W // SB_BLOCK
    span = min(SB_SPAN, s)

    def body(*refs):
        if gather is None:
            q_ref, k_ref, v_ref, o_ref = refs
        else:
            q_ref, k_ref, v_ref, x_ref, o_ref, g_ref, send_sems, recv_sems, local_sem = refs
            start, forward, finish = _gather_phases(x_ref, g_ref, send_sems, recv_sems, local_sem)
            step = pl.program_id(0) * nq + pl.program_id(1)
            pl.when(step == 0)(start)
        lane, sd, valid, qhs = _sb_setup(q_ref, span)
        after_ones = _cumsum_operand(True, terms=2)
        zero = jnp.zeros((SB_BLOCK, SB_BLOCK), F32)

        def span_step(sb, carry, mask):
            rows = pl.ds(pl.multiple_of(sb * span, span), span)
            kw = k_ref[rows, :].astype(BF16)
            vw = v_ref[rows, :].astype(BF16)
            out = []
            for h in range(2):
                run, acc = carry[h]
                _, a, run = _sb_scores(qhs[h], kw, run, mask, after_ones)
                out.append((run, acc + _dot(a.astype(BF16), vw)))
            return tuple(out)

        carry = span_step(sd, ((zero, zero), (zero, zero)), valid)
        carry = lax.fori_loop(0, sd, lambda t, c: span_step(sd - 1 - t, c, None), carry)
        o_ref[...] = jnp.where(lane < 64, carry[0][1], carry[1][1])
        if gather is not None:
            pl.when(step == (kcol - 1) * nq + (3 * nq) // 4)(forward)
            pl.when(step == kcol * nq - 1)(finish)

    in_specs = [pl.BlockSpec((SB_BLOCK, SB_BLOCK), lambda hp, qi: (qi, hp)),
                pl.BlockSpec((s, SB_BLOCK), lambda hp, qi: (0, kcol + hp)),
                pl.BlockSpec((s, SB_BLOCK), lambda hp, qi: (0, 2 * kcol + hp))]
    out_specs = [pl.BlockSpec((SB_BLOCK, SB_BLOCK), lambda hp, qi: (qi, hp))]
    out_shape = [jax.ShapeDtypeStruct((s, BRANCH_W), F32)]
    operands = [p, p, p]
    scratch = []
    if gather is not None:
        in_specs.append(pl.BlockSpec(memory_space=pl.ANY))
        out_specs.append(pl.BlockSpec(memory_space=pl.ANY))
        out_shape.append(jax.ShapeDtypeStruct((N_DEV,) + gather.shape, gather.dtype))
        operands.append(gather)
        scratch = _GATHER_SEMS
    out = pl.pallas_call(
        body, name=name, grid=(kcol, nq), in_specs=in_specs, out_specs=out_specs, out_shape=out_shape,
        scratch_shapes=scratch, compiler_params=_cparams("arbitrary", "arbitrary"),
    )(*operands)
    return out[0] if gather is None else out


def _sb_bwd(p, dya, *, name, scatter=None):
    s = p.shape[0]
    nq = s // SB_BLOCK
    kcol = BRANCH_W // SB_BLOCK
    span = min(SB_SPAN, s)
    per = span // SB_BLOCK

    def body(*refs):
        if scatter is None:
            q_ref, k_ref, v_ref, do_ref, dq_ref, dk_ref, dv_ref, a_s, b_s, dk_acc, dv_acc = refs
        else:
            (q_ref, k_ref, v_ref, do_ref, g_ref, dq_ref, dk_ref, dv_ref, r_ref,
             a_s, b_s, dk_acc, dv_acc, send_sems, recv_sems, local_sem) = refs
            start, finish = _scatter_phases(g_ref, r_ref, send_sems, recv_sems, local_sem)
            step = pl.program_id(0) * nq + pl.program_id(1)
            pl.when(step == 0)(start)
        qi = pl.program_id(1)

        @pl.when(qi == 0)
        def _():
            dk_acc[...] = jnp.zeros_like(dk_acc)
            dv_acc[...] = jnp.zeros_like(dv_acc)

        lane, sd, valid, qhs = _sb_setup(q_ref, span)
        after_ones = _cumsum_operand(True, terms=2)
        before_ones = _cumsum_operand(False)
        do = do_ref[...]
        dohs = (jnp.where(lane < 64, do, 0.0).astype(BF16), jnp.where(lane >= 64, do, 0.0).astype(BF16))
        zero = jnp.zeros((SB_BLOCK, SB_BLOCK), F32)

        def rebuild(sb, runs, mask):
            rows = pl.ds(pl.multiple_of(sb * span, span), span)
            kw = k_ref[rows, :].astype(BF16)
            out = []
            for h in range(2):
                lsp, a, run = _sb_scores(qhs[h], kw, runs[h], mask, after_ones)
                beta = jnp.exp(lsp)
                if mask is not None:
                    beta = jnp.where(mask, beta, 0.0)
                a_s[h, sb] = a
                b_s[h, sb] = beta
                out.append(run)
            return tuple(out)

        runs = rebuild(sd, (zero, zero), valid)
        lax.fori_loop(0, sd, lambda t, r: rebuild(sd - 1 - t, r, None), runs)

        def accumulate(sb, carry):
            rows = pl.ds(pl.multiple_of(sb * span, span), span)
            kw = k_ref[rows, :].astype(BF16)
            vw = v_ref[rows, :].astype(BF16)
            out = []
            dk_span = jnp.zeros((span, SB_BLOCK), F32)
            dv_span = jnp.zeros((span, SB_BLOCK), F32)
            for h in range(2):
                pg, dq = carry[h]
                a = a_s[h, sb]
                beta = b_s[h, sb]
                g = a * _dot_nt(dohs[h], vw)
                gb = g.astype(BF16)
                before = [None] * per
                for b in range(per):
                    cols = slice(b * SB_BLOCK, (b + 1) * SB_BLOCK)
                    gt = _dot(gb[:, cols], before_ones)
                    before[b] = pg + gt[:, :SB_BLOCK]
                    pg = pg + gt[:, SB_BLOCK:]
                dz = (g * (1.0 - beta) - beta * jnp.concatenate(before, axis=1)).astype(BF16)
                dk_span = dk_span + _dot_tn(dz, qhs[h])
                dv_span = dv_span + _dot_tn(a.astype(BF16), dohs[h])
                out.append((pg, dq + _dot(dz, kw)))
            dk_acc[rows, :] += dk_span
            dv_acc[rows, :] += dv_span
            return tuple(out)

        carry = lax.fori_loop(0, sd + 1, accumulate, ((zero, zero), (zero, zero)))
        dq_ref[...] = (jnp.where(lane < 64, carry[0][1], carry[1][1]) * SB_SCALE).astype(BF16)

        @pl.when(qi == nq - 1)
        def _():
            dk_ref[...] = dk_acc[...].astype(BF16)
            dv_ref[...] = dv_acc[...].astype(BF16)

        if scatter is not None:
            pl.when(step == kcol * nq - 1)(finish)

    blk = pl.BlockSpec((SB_BLOCK, SB_BLOCK), lambda hp, qi: (qi, hp))
    col = pl.BlockSpec((s, SB_BLOCK), lambda hp, qi: (0, hp))
    out = jax.ShapeDtypeStruct((s, BRANCH_W), BF16)
    in_specs = [blk,
                pl.BlockSpec((s, SB_BLOCK), lambda hp, qi: (0, kcol + hp)),
                pl.BlockSpec((s, SB_BLOCK), lambda hp, qi: (0, 2 * kcol + hp)),
                blk]
    out_specs = [blk, col, col]
    out_shape = [out, out, out]
    operands = [p, p, p, dya]
    scratch = [pltpu.VMEM((2, s // span, SB_BLOCK, span), F32), pltpu.VMEM((2, s // span, SB_BLOCK, span), F32),
               pltpu.VMEM((s, SB_BLOCK), F32), pltpu.VMEM((s, SB_BLOCK), F32)]
    if scatter is not None:
        in_specs.append(pl.BlockSpec(memory_space=pl.ANY))
        out_specs.append(pl.BlockSpec(memory_space=pl.ANY))
        out_shape.append(jax.ShapeDtypeStruct(scatter.shape, scatter.dtype))
        operands.append(scatter)
        scratch = scratch + _SCATTER_SEMS
    return pl.pallas_call(
        body, name=name, grid=(kcol, nq), in_specs=in_specs, out_specs=out_specs, out_shape=out_shape,
        scratch_shapes=scratch, compiler_params=_cparams("arbitrary", "arbitrary"),
    )(*operands)


_INV_SQRT2 = 0.7071067811865476
_INV_SQRT2PI = 0.3989422804014327


def _gelu(x):
    return 0.5 * x * (1.0 + lax.erf(x * _INV_SQRT2))


def _gelu_grad(x):
    return 0.5 * (1.0 + lax.erf(x * _INV_SQRT2)) + x * _INV_SQRT2PI * jnp.exp(-0.5 * x * x)


def _chunk_mask(transposed=False):
    r = lax.broadcasted_iota(jnp.int32, (SGU_LEN, SGU_LEN), 0)
    c = lax.broadcasted_iota(jnp.int32, (SGU_LEN, SGU_LEN), 1)
    return (c // 64) >= (r // 64) if transposed else (r // 64) >= (c // 64)


def _sgu_norm(v_raw, g, b):
    zv = _gelu(v_raw)
    xc = zv - jnp.mean(zv, axis=-1, keepdims=True)
    rs = lax.rsqrt(jnp.mean(xc * xc, axis=-1, keepdims=True) + LN_EPS)
    xh = xc * rs
    return xh, rs, xh * g + b


def _sgu_fwd(p, ln_g, ln_b, w, b_col, *, name):
    s = p.shape[0]
    tr = _pick(s, 512, SGU_LEN)

    def body(u_ref, v_ref, g_ref, b_ref, w_ref, bc_ref, o_ref):
        mask = _chunk_mask()
        zu = _gelu(u_ref[...])
        _, _, vn = _sgu_norm(v_ref[...], g_ref[...], b_ref[...])
        vnb = vn.astype(BF16)
        for gi in range(SGU_GROUPS):
            wg = jnp.where(mask, w_ref[gi], 0.0).astype(BF16)
            cs = slice(gi * SGU_LEN, (gi + 1) * SGU_LEN)
            for c in range(tr // SGU_LEN):
                rs_ = slice(c * SGU_LEN, (c + 1) * SGU_LEN)
                vm = _dot(wg, vnb[rs_, cs]) + bc_ref[gi]
                o_ref[rs_, cs] = zu[rs_, cs] * vm

    vec = pl.BlockSpec((1, BRANCH_W), lambda i: (0, 0))
    return pl.pallas_call(
        body, name=name, grid=(s // tr,),
        in_specs=[pl.BlockSpec((tr, BRANCH_W), lambda i: (i, 3)), pl.BlockSpec((tr, BRANCH_W), lambda i: (i, 4)),
                  vec, vec,
                  pl.BlockSpec((SGU_GROUPS, SGU_LEN, SGU_LEN), lambda i: (0, 0, 0)),
                  pl.BlockSpec((SGU_GROUPS, SGU_LEN, 1), lambda i: (0, 0, 0))],
        out_specs=pl.BlockSpec((tr, BRANCH_W), lambda i: (i, 0)),
        out_shape=jax.ShapeDtypeStruct((s, BRANCH_W), F32),
        compiler_params=_cparams("parallel"),
    )(p, p, ln_g, ln_b, w, b_col)


def _sgu_bwd(p, dyb, ln_g, ln_b, w, w_t, b_col, *, name):
    s = p.shape[0]
    tr = _pick(s, 256, SGU_LEN)

    def body(u_ref, v_ref, dy_ref, g_ref, b_ref, w_ref, wt_ref, bc_ref,
             dz_ref, dg_ref, db_ref, dw_ref, dbc_ref, dvn_s):
        @pl.when(pl.program_id(0) == 0)
        def _():
            dg_ref[...] = jnp.zeros_like(dg_ref)
            db_ref[...] = jnp.zeros_like(db_ref)
            dw_ref[...] = jnp.zeros_like(dw_ref)
            dbc_ref[...] = jnp.zeros_like(dbc_ref)

        mask = _chunk_mask()
        mask_t = _chunk_mask(transposed=True)
        u_raw = u_ref[...]
        v_raw = v_ref[...]
        dy = dy_ref[...]
        zu = _gelu(u_raw)
        xh, rs, vn = _sgu_norm(v_raw, g_ref[...], b_ref[...])
        vnb = vn.astype(BF16)
        dvm_all = dy * zu
        for gi in range(SGU_GROUPS):
            wg = jnp.where(mask, w_ref[gi], 0.0).astype(BF16)
            wgt = jnp.where(mask_t, wt_ref[gi], 0.0).astype(BF16)
            cs = slice(gi * SGU_LEN, (gi + 1) * SGU_LEN)
            dw_g = jnp.zeros((SGU_LEN, SGU_LEN), F32)
            db_g = jnp.zeros((SGU_LEN, 1), F32)
            for c in range(tr // SGU_LEN):
                rs_ = slice(c * SGU_LEN, (c + 1) * SGU_LEN)
                vm = _dot(wg, vnb[rs_, cs]) + bc_ref[gi]
                dz_ref[rs_, cs] = (dy[rs_, cs] * vm * _gelu_grad(u_raw[rs_, cs])).astype(BF16)
                dvm = dvm_all[rs_, cs]
                dvmb = dvm.astype(BF16)
                dw_g = dw_g + _dot_nt(dvmb, vnb[rs_, cs])
                db_g = db_g + jnp.sum(dvm, axis=1, keepdims=True)
                dvn_s[rs_, cs] = _dot(wgt, dvmb)
            dw_ref[gi] += jnp.where(mask, dw_g, 0.0)
            dbc_ref[gi] += db_g
        dvn = dvn_s[...]
        dg_ref[...] += jnp.sum(dvn * xh, axis=0, keepdims=True)
        db_ref[...] += jnp.sum(dvn, axis=0, keepdims=True)
        dxh = dvn * g_ref[...]
        dzv = rs * (dxh - jnp.mean(dxh, axis=-1, keepdims=True) - xh * jnp.mean(dxh * xh, axis=-1, keepdims=True))
        dz_ref[:, BRANCH_W:] = (dzv * _gelu_grad(v_raw)).astype(BF16)

    vec = pl.BlockSpec((1, BRANCH_W), lambda i: (0, 0))
    wspec = pl.BlockSpec((SGU_GROUPS, SGU_LEN, SGU_LEN), lambda i: (0, 0, 0))
    bspec = pl.BlockSpec((SGU_GROUPS, SGU_LEN, 1), lambda i: (0, 0, 0))
    return pl.pallas_call(
        body, name=name, grid=(s // tr,),
        in_specs=[pl.BlockSpec((tr, BRANCH_W), lambda i: (i, 3)), pl.BlockSpec((tr, BRANCH_W), lambda i: (i, 4)),
                  pl.BlockSpec((tr, BRANCH_W), lambda i: (i, 0)), vec, vec, wspec, wspec, bspec],
        out_specs=[pl.BlockSpec((tr, 2 * BRANCH_W), lambda i: (i, 0)), vec, vec, wspec, bspec],
        out_shape=[jax.ShapeDtypeStruct((s, 2 * BRANCH_W), BF16),
                   jax.ShapeDtypeStruct((1, BRANCH_W), F32), jax.ShapeDtypeStruct((1, BRANCH_W), F32),
                   jax.ShapeDtypeStruct((SGU_GROUPS, SGU_LEN, SGU_LEN), F32),
                   jax.ShapeDtypeStruct((SGU_GROUPS, SGU_LEN, 1), F32)],
        scratch_shapes=[pltpu.VMEM((tr, BRANCH_W), F32)],
        compiler_params=_cparams("arbitrary"),
    )(p, p, dyb, ln_g, ln_b, w, w_t, b_col)


def _shift_down(x, prev8, k):
    rolled = pltpu.roll(x, k, 0)
    r8 = lax.broadcasted_iota(jnp.int32, prev8.shape, 0)
    head = jnp.where(r8 < k, pltpu.roll(prev8, k, 0), rolled[:HALO])
    return jnp.concatenate([head, rolled[HALO:]], axis=0)


def _shift_up(x, next8, k):
    n = x.shape[0]
    rolled = pltpu.roll(x, n - k, 0)
    r8 = lax.broadcasted_iota(jnp.int32, next8.shape, 0)
    tail = jnp.where(r8 >= HALO - k, pltpu.roll(next8, HALO - k, 0), rolled[n - HALO:])
    return jnp.concatenate([rolled[:n - HALO], tail], axis=0)


def _conv_specs(s, tr):
    nb = tr // HALO
    last = s // HALO - 1
    tile = lambda cb: pl.BlockSpec((tr, 128), lambda j, i: (i, cb * 4 + j))
    above = lambda cb: pl.BlockSpec((HALO, 128), lambda j, i: (jnp.maximum(i * nb - 1, 0), cb * 4 + j))
    below = lambda cb: pl.BlockSpec((HALO, 128), lambda j, i: (jnp.minimum((i + 1) * nb, last), cb * 4 + j))
    return tile, above, below


def _conv_fwd(p, cw, *, name):
    s = p.shape[0]
    tr = _pick(s, 512, HALO)
    tile, above, _ = _conv_specs(s, tr)

    def body(cb_ref, cc_ref, cx_ref, ccp_ref, cxp_ref, w_ref, o_ref):
        first = pl.program_id(1) == 0
        y = cc_ref[...] * cx_ref[...]
        yp = jnp.where(first, 0.0, ccp_ref[...] * cxp_ref[...])
        conv = w_ref[2:3, :] * y + w_ref[1:2, :] * _shift_down(y, yp, 1) + w_ref[0:1, :] * _shift_down(y, yp, 2)
        o_ref[...] = cb_ref[...] * conv

    return pl.pallas_call(
        body, name=name, grid=(4, s // tr),
        in_specs=[tile(5), tile(6), tile(7), above(6), above(7), pl.BlockSpec((3, 128), lambda j, i: (0, j))],
        out_specs=pl.BlockSpec((tr, 128), lambda j, i: (i, j)),
        out_shape=jax.ShapeDtypeStruct((s, BRANCH_W), F32),
        compiler_params=_cparams("parallel", "parallel"),
    )(p, p, p, p, p, cw)


def _conv_bwd(p, dyc, cw, *, name):
    s = p.shape[0]
    tr = _pick(s, 512, HALO)
    nt = s // tr
    nb = tr // HALO
    last = s // HALO - 1
    tile, above, below = _conv_specs(s, tr)

    def body(cb_ref, cc_ref, cx_ref, ccp_ref, cxp_ref, cbn_ref, dy_ref, dyn_ref, w_ref,
             dcb_ref, dcc_ref, dcx_ref, dw_ref):
        i = pl.program_id(1)

        @pl.when(i == 0)
        def _():
            dw_ref[...] = jnp.zeros_like(dw_ref)

        cb = cb_ref[...]
        cc = cc_ref[...]
        cx = cx_ref[...]
        y = cc * cx
        yp = jnp.where(i == 0, 0.0, ccp_ref[...] * cxp_ref[...])
        y1 = _shift_down(y, yp, 1)
        y2 = _shift_down(y, yp, 2)
        w0, w1, w2 = w_ref[0:1, :], w_ref[1:2, :], w_ref[2:3, :]
        conv = w2 * y + w1 * y1 + w0 * y2
        dyc_v = dy_ref[...]
        dconv = dyc_v * cb
        dn = jnp.where(i == nt - 1, 0.0, dyn_ref[...] * cbn_ref[...])
        dyv = w2 * dconv + w1 * _shift_up(dconv, dn, 1) + w0 * _shift_up(dconv, dn, 2)
        dcb_ref[...] = (dyc_v * conv).astype(BF16)
        dcc_ref[...] = (dyv * cx).astype(BF16)
        dcx_ref[...] = (dyv * cc).astype(BF16)
        dw_ref[0:1, :] += jnp.sum(dconv * y2, axis=0, keepdims=True)
        dw_ref[1:2, :] += jnp.sum(dconv * y1, axis=0, keepdims=True)
        dw_ref[2:3, :] += jnp.sum(dconv * y, axis=0, keepdims=True)

    dy_tile = pl.BlockSpec((tr, 128), lambda j, i: (i, j))
    dy_below = pl.BlockSpec((HALO, 128), lambda j, i: (jnp.minimum((i + 1) * nb, last), j))
    out_tile = lambda cb: pl.BlockSpec((tr, 128), lambda j, i: (i, cb * 4 + j))
    w_spec = pl.BlockSpec((3, 128), lambda j, i: (0, j))
    dcb, dcc, dcx, dw = pl.pallas_call(
        body, name=name, grid=(4, nt),
        in_specs=[tile(5), tile(6), tile(7), above(6), above(7), below(5), dy_tile, dy_below, w_spec],
        out_specs=[dy_tile, dy_tile, dy_tile, w_spec],
        out_shape=[jax.ShapeDtypeStruct((s, BRANCH_W), BF16)] * 3 + [jax.ShapeDtypeStruct((3, BRANCH_W), F32)],
        compiler_params=_cparams("parallel", "arbitrary"),
    )(p, p, p, p, p, p, dyc, dyc, cw)
    return dcb, dcc, dcx, dw


def _merge_fwd(ya, yb, yc, wb, p, *, name):
    s = p.shape[0]
    tr = _pick(s, 256, 16)

    def body(ya_ref, yb_ref, yc_ref, wb_ref, g0_ref, g1_ref, g2_ref, o_ref):
        acc = jnp.zeros((tr, D_MODEL), F32)
        for n, (y_ref, g_ref) in enumerate(((ya_ref, g0_ref), (yb_ref, g1_ref), (yc_ref, g2_ref))):
            acc = acc + _sigmoid(g_ref[...]) * _dot(y_ref[...].astype(BF16), wb_ref[n])
        o_ref[...] = acc.astype(BF16)

    yspec = pl.BlockSpec((tr, BRANCH_W), lambda i: (i, 0))
    gate = lambda n: pl.BlockSpec((tr, D_MODEL), lambda i: (i, 4 + n))
    return pl.pallas_call(
        body, name=name, grid=(s // tr,),
        in_specs=[yspec, yspec, yspec, pl.BlockSpec((3, BRANCH_W, D_MODEL), lambda i: (0, 0, 0)),
                  gate(0), gate(1), gate(2)],
        out_specs=pl.BlockSpec((tr, D_MODEL), lambda i: (i, 0)),
        out_shape=jax.ShapeDtypeStruct((s, D_MODEL), BF16),
        compiler_params=_cparams("parallel"),
    )(ya, yb, yc, wb, p, p, p)


def _merge_bwd(dm, ya, yb, yc, wb, p, *, name):
    s = p.shape[0]
    tr = _pick(s, 256, 16)

    def body(dm_ref, ya_ref, yb_ref, yc_ref, wb_ref, g0_ref, g1_ref, g2_ref,
             dya_ref, dyb_ref, dyc_ref, dg_ref, dbrd_ref):
        dmv = dm_ref[...]
        ys = (ya_ref, yb_ref, yc_ref)
        gs = (g0_ref, g1_ref, g2_ref)
        dys = (dya_ref, dyb_ref, dyc_ref)
        for n in range(3):
            brd = _dot(ys[n][...].astype(BF16), wb_ref[n])
            sg = _sigmoid(gs[n][...])
            dbrd = (sg * dmv).astype(BF16)
            dbrd_ref[n] = dbrd
            dg_ref[:, n * D_MODEL:(n + 1) * D_MODEL] = (dmv * brd * sg * (1.0 - sg)).astype(BF16)
            dys[n][...] = _dot_nt(dbrd, wb_ref[n])

    yspec = pl.BlockSpec((tr, BRANCH_W), lambda i: (i, 0))
    gate = lambda n: pl.BlockSpec((tr, D_MODEL), lambda i: (i, 4 + n))
    row = pl.BlockSpec((tr, D_MODEL), lambda i: (i, 0))
    return pl.pallas_call(
        body, name=name, grid=(s // tr,),
        in_specs=[row, yspec, yspec, yspec, pl.BlockSpec((3, BRANCH_W, D_MODEL), lambda i: (0, 0, 0)),
                  gate(0), gate(1), gate(2)],
        out_specs=[yspec, yspec, yspec, pl.BlockSpec((tr, 3 * D_MODEL), lambda i: (i, 0)),
                   pl.BlockSpec((3, tr, D_MODEL), lambda i: (0, i, 0))],
        out_shape=[jax.ShapeDtypeStruct((s, BRANCH_W), F32)] * 3
                  + [jax.ShapeDtypeStruct((s, 3 * D_MODEL), BF16), jax.ShapeDtypeStruct((3, s, D_MODEL), BF16)],
        compiler_params=_cparams("parallel"),
    )(dm, ya, yb, yc, wb, p, p, p)


def _xa_probs(q, k):
    sc = _dot_nt(q, k) * XA_SCALE
    e = jnp.exp(sc - jnp.max(sc, axis=-1, keepdims=True))
    return e / jnp.sum(e, axis=-1, keepdims=True)


def _xa_fwd(q, k, v, *, name):
    s = q.shape[0]
    mt = k.shape[0]
    tr = _pick(s, 512, 16)

    def body(q_ref, k_ref, v_ref, o_ref):
        pr = _xa_probs(q_ref[...], k_ref[...])
        o_ref[...] = _dot(pr.astype(BF16), v_ref[...]).astype(BF16)

    qs = pl.BlockSpec((tr, XA_HEAD), lambda h, i: (i, h))
    ks = pl.BlockSpec((mt, XA_HEAD), lambda h, i: (0, h))
    return pl.pallas_call(
        body, name=name, grid=(D_MODEL // XA_HEAD, s // tr),
        in_specs=[qs, ks, ks], out_specs=qs,
        out_shape=jax.ShapeDtypeStruct((s, D_MODEL), BF16),
        compiler_params=_cparams("parallel", "parallel"),
    )(q, k, v)


def _xa_bwd(q, k, v, do, *, name):
    s = q.shape[0]
    mt = k.shape[0]
    tr = _pick(s, 512, 16)

    def body(q_ref, k_ref, v_ref, do_ref, dq_ref, dk_ref, dv_ref):
        @pl.when(pl.program_id(1) == 0)
        def _():
            dk_ref[...] = jnp.zeros_like(dk_ref)
            dv_ref[...] = jnp.zeros_like(dv_ref)

        qv = q_ref[...]
        kv = k_ref[...]
        dov = do_ref[...]
        pr = _xa_probs(qv, kv)
        dpr = _dot_nt(dov, v_ref[...])
        ds = (pr * (dpr - jnp.sum(dpr * pr, axis=-1, keepdims=True)) * XA_SCALE).astype(BF16)
        dq_ref[...] = _dot(ds, kv).astype(BF16)
        dk_ref[...] += _dot_tn(ds, qv)
        dv_ref[...] += _dot_tn(pr.astype(BF16), dov)

    qs = pl.BlockSpec((tr, XA_HEAD), lambda h, i: (i, h))
    ks = pl.BlockSpec((mt, XA_HEAD), lambda h, i: (0, h))
    return pl.pallas_call(
        body, name=name, grid=(D_MODEL // XA_HEAD, s // tr),
        in_specs=[qs, ks, ks, qs], out_specs=[qs, ks, ks],
        out_shape=[jax.ShapeDtypeStruct((s, D_MODEL), BF16), jax.ShapeDtypeStruct((mt, D_MODEL), F32),
                   jax.ShapeDtypeStruct((mt, D_MODEL), F32)],
        compiler_params=_cparams("parallel", "arbitrary"),
    )(q, k, v, do)


def _swiglu_fwd(a, b, *, name):
    s, f = a.shape
    tr = _pick(s, 256, 16)

    def body(a_ref, b_ref, o_ref):
        av = a_ref[...]
        o_ref[...] = (av * _sigmoid(av) * b_ref[...]).astype(BF16)

    spec = pl.BlockSpec((tr, f), lambda i: (i, 0))
    return pl.pallas_call(
        body, name=name, grid=(s // tr,), in_specs=[spec, spec], out_specs=spec,
        out_shape=jax.ShapeDtypeStruct((s, f), BF16), compiler_params=_cparams("parallel"),
    )(a, b)


def _swiglu_bwd(a, b, dh, *, name):
    s, f = a.shape
    tr = _pick(s, 256, 16)

    def body(a_ref, b_ref, dh_ref, da_ref, db_ref):
        av = a_ref[...]
        dhv = dh_ref[...]
        sg = _sigmoid(av)
        silu = av * sg
        da_ref[...] = (dhv * b_ref[...] * (sg + silu * (1.0 - sg))).astype(BF16)
        db_ref[...] = (dhv * silu).astype(BF16)

    spec = pl.BlockSpec((tr, f), lambda i: (i, 0))
    return pl.pallas_call(
        body, name=name, grid=(s // tr,), in_specs=[spec, spec, spec], out_specs=[spec, spec],
        out_shape=[jax.ShapeDtypeStruct((s, f), BF16)] * 2, compiler_params=_cparams("parallel"),
    )(a, b, dh)


def _adamw(w, g, m, v, *, name):
    r, c = w.shape
    tr = _pick(r, 512, 8)

    def body(w_ref, g_ref, m_ref, v_ref, d_ref, mo_ref, vo_ref):
        gv = g_ref[...]
        mn = ADAM_B1 * m_ref[...] + (1.0 - ADAM_B1) * gv
        vn = ADAM_B2 * v_ref[...] + (1.0 - ADAM_B2) * (gv * gv)
        m_hat = mn / (1.0 - ADAM_B1 ** ADAM_STEP)
        v_hat = vn / (1.0 - ADAM_B2 ** ADAM_STEP)
        d_ref[...] = -ADAM_LR * (m_hat / (jnp.sqrt(v_hat) + ADAM_EPS) + ADAM_WD * w_ref[...])
        mo_ref[...] = mn
        vo_ref[...] = vn

    spec = pl.BlockSpec((tr, c), lambda i: (i, 0))
    shp = jax.ShapeDtypeStruct((r, c), F32)
    return pl.pallas_call(
        body, name=name, grid=(r // tr,), in_specs=[spec] * 4, out_specs=[spec] * 3,
        out_shape=[shp] * 3, compiler_params=_cparams("parallel"),
    )(w, g, m, v)


def _position():
    return lax.axis_index("x"), lax.axis_index("y"), lax.axis_index("c")


def _all_gather(x, *, name):
    t, c_ = x.shape

    def body(x_ref, out_ref, send_sems, recv_sems, local_sem):
        start, forward, finish = _gather_phases(x_ref, out_ref, send_sems, recv_sems, local_sem)
        start()
        forward()
        finish()

    return pl.pallas_call(
        body, name=name,
        out_shape=jax.ShapeDtypeStruct((N_DEV, t, c_), x.dtype),
        in_specs=[pl.BlockSpec(memory_space=pl.ANY)],
        out_specs=pl.BlockSpec(memory_space=pl.ANY),
        scratch_shapes=_GATHER_SEMS,
    )(x)


_GATHER_SEMS = [pltpu.SemaphoreType.DMA((7,)), pltpu.SemaphoreType.DMA((7,)), pltpu.SemaphoreType.DMA]


def _gather_phases(x_ref, out_ref, send_sems, recv_sems, local_sem):
    x_, y_, c = _position()
    me, sibling = (x_, y_, c), (x_, y_, 1 - c)
    chips = [(1 - x_, y_), (x_, 1 - y_), (1 - x_, 1 - y_)]

    def block(px, py, pc):
        return out_ref.at[4 * px + 2 * py + pc]

    def copy(k, blk, to, src=None):
        return pltpu.make_async_remote_copy(
            src_ref=block(*blk) if src is None else src, dst_ref=block(*blk),
            send_sem=send_sems.at[k], recv_sem=recv_sems.at[k], device_id=to, device_id_type=MESH)

    mine = pltpu.make_async_copy(x_ref, block(*me), local_sem)
    first = [copy(0, me, sibling, src=x_ref)]
    first += [copy(1 + j, me, (*chip, c), src=x_ref) for j, chip in enumerate(chips)]
    passed = [copy(4 + j, (*chip, c), sibling) for j, chip in enumerate(chips)]

    def start():
        mine.start()
        for cp in first:
            cp.start()

    def forward():
        for j, chip in enumerate(chips):
            copy(1 + j, (*chip, c), me).wait_recv()
            passed[j].start()

    def finish():
        copy(0, sibling, me).wait_recv()
        for j, chip in enumerate(chips):
            copy(4 + j, (*chip, 1 - c), me).wait_recv()
        for cp in first + passed:
            cp.wait_send()
        mine.wait()

    return start, forward, finish


_SCATTER_SEMS = [pltpu.SemaphoreType.DMA((7,)), pltpu.SemaphoreType.DMA((7,)), pltpu.SemaphoreType.DMA]


def _scatter_phases(g_ref, r_ref, send_sems, recv_sems, local_sem):
    x_, y_, c = _position()
    me = 4 * x_ + 2 * y_ + c
    local = pltpu.make_async_copy(g_ref.at[me], r_ref.at[me], local_sem)
    copies = []
    for k in range(1, N_DEV):
        to = (x_ ^ (k >> 2), y_ ^ ((k >> 1) & 1), c ^ (k & 1))
        copies.append(pltpu.make_async_remote_copy(
            src_ref=g_ref.at[me ^ k], dst_ref=r_ref.at[me], send_sem=send_sems.at[k - 1],
            recv_sem=recv_sems.at[k - 1], device_id=to, device_id_type=MESH))

    def start():
        local.start()
        for cp in copies:
            cp.start()

    def finish():
        for k in range(1, N_DEV):
            pltpu.make_async_remote_copy(
                src_ref=g_ref.at[me], dst_ref=r_ref.at[me ^ k], send_sem=send_sems.at[k - 1],
                recv_sem=recv_sems.at[k - 1], device_id=(x_, y_, c), device_id_type=MESH).wait_recv()
        for cp in copies:
            cp.wait_send()
        local.wait()

    return start, finish


def _sum_devices(r8, *, name):
    _, t, c_ = r8.shape
    tr = _pick(t, 256, 16)

    def body(r_ref, o_ref):
        acc = r_ref[0].astype(F32)
        for d in range(1, N_DEV):
            acc = acc + r_ref[d].astype(F32)
        o_ref[...] = acc

    return pl.pallas_call(
        body, name=name, grid=(t // tr,),
        in_specs=[pl.BlockSpec((N_DEV, tr, c_), lambda i: (0, i, 0))],
        out_specs=pl.BlockSpec((tr, c_), lambda i: (i, 0)),
        out_shape=jax.ShapeDtypeStruct((t, c_), F32),
        compiler_params=_cparams("parallel"),
    )(r8)


def _all_reduce_small(x, *, name):
    r, c_ = x.shape

    def body(x_ref, o_ref, buf, send_sems, recv_sems):
        x_, y_, c = _position()
        me = 4 * x_ + 2 * y_ + c
        buf[me] = x_ref[...]
        copies = []
        for k in range(1, N_DEV):
            to = (x_ ^ (k >> 2), y_ ^ ((k >> 1) & 1), c ^ (k & 1))
            copies.append(pltpu.make_async_remote_copy(
                src_ref=x_ref, dst_ref=buf.at[me], send_sem=send_sems.at[k - 1], recv_sem=recv_sems.at[k - 1],
                device_id=to, device_id_type=MESH))
        for cp in copies:
            cp.start()
        for k in range(1, N_DEV):
            src = me ^ k
            pltpu.make_async_remote_copy(
                src_ref=x_ref, dst_ref=buf.at[src], send_sem=send_sems.at[k - 1], recv_sem=recv_sems.at[k - 1],
                device_id=(x_, y_, c), device_id_type=MESH).wait_recv()
        for cp in copies:
            cp.wait_send()
        acc = buf[0]
        for d in range(1, N_DEV):
            acc = acc + buf[d]
        o_ref[...] = acc

    return pl.pallas_call(
        body, name=name,
        out_shape=jax.ShapeDtypeStruct((r, c_), F32),
        in_specs=[pl.BlockSpec(memory_space=pltpu.VMEM)],
        out_specs=pl.BlockSpec(memory_space=pltpu.VMEM),
        scratch_shapes=[pltpu.VMEM((N_DEV, r, c_), F32), pltpu.SemaphoreType.DMA((7,)), pltpu.SemaphoreType.DMA((7,))],
    )(x)


def _rs_pair_exchange(g8, *, name):
    _, t, c_ = g8.shape

    def body(g_ref, r_ref, send_sems, recv_sems):
        x_, y_, c = _position()
        copies = [pltpu.make_async_remote_copy(
            src_ref=g_ref.at[2 * ch + (1 - c)], dst_ref=r_ref.at[ch],
            send_sem=send_sems.at[ch], recv_sem=recv_sems.at[ch],
            device_id=(x_, y_, 1 - c), device_id_type=MESH) for ch in range(4)]
        for cp in copies:
            cp.start()
        for cp in copies:
            cp.wait()

    return pl.pallas_call(
        body, name=name,
        out_shape=jax.ShapeDtypeStruct((4, t, c_), g8.dtype),
        in_specs=[pl.BlockSpec(memory_space=pl.ANY)],
        out_specs=pl.BlockSpec(memory_space=pl.ANY),
        scratch_shapes=[pltpu.SemaphoreType.DMA((4,)), pltpu.SemaphoreType.DMA((4,))],
    )(g8)


def _pair_add(core, g8, recv, *, name):
    _, t, c_ = g8.shape
    tr = _pick(t, 512, 16)

    def body(core_ref, g_ref, r_ref, o_ref):
        o_ref[...] = (g_ref[...].astype(F32) + r_ref[...].astype(F32)).astype(o_ref.dtype)

    grid_spec = pltpu.PrefetchScalarGridSpec(
        num_scalar_prefetch=1, grid=(4, t // tr),
        in_specs=[pl.BlockSpec((None, tr, c_), lambda ch, i, core_ref: (2 * ch + core_ref[0], i, 0)),
                  pl.BlockSpec((None, tr, c_), lambda ch, i, core_ref: (ch, i, 0))],
        out_specs=pl.BlockSpec((None, tr, c_), lambda ch, i, core_ref: (ch, i, 0)))
    return pl.pallas_call(
        body, name=name, grid_spec=grid_spec,
        out_shape=jax.ShapeDtypeStruct((4, t, c_), g8.dtype),
        compiler_params=_cparams("parallel", "parallel"),
    )(core, g8, recv)


def _rs_chip_exchange(part, *, name):
    _, t, c_ = part.shape

    def body(p_ref, r_ref, send_sems, recv_sems, local_sem):
        x_, y_, c = _position()
        mine = 2 * x_ + y_
        local = pltpu.make_async_copy(p_ref.at[mine], r_ref.at[mine], local_sem)
        local.start()
        chips = [(1 - x_, y_), (x_, 1 - y_), (1 - x_, 1 - y_)]
        copies = [pltpu.make_async_remote_copy(
            src_ref=p_ref.at[2 * px + py], dst_ref=r_ref.at[mine],
            send_sem=send_sems.at[k], recv_sem=recv_sems.at[k],
            device_id=(px, py, c), device_id_type=MESH) for k, (px, py) in enumerate(chips)]
        for cp in copies:
            cp.start()
        for k, (px, py) in enumerate(chips):
            pltpu.make_async_remote_copy(
                src_ref=p_ref.at[mine], dst_ref=r_ref.at[2 * px + py],
                send_sem=send_sems.at[k], recv_sem=recv_sems.at[k],
                device_id=(x_, y_, c), device_id_type=MESH).wait_recv()
        for cp in copies:
            cp.wait_send()
        local.wait()

    return pl.pallas_call(
        body, name=name,
        out_shape=jax.ShapeDtypeStruct((4, t, c_), part.dtype),
        in_specs=[pl.BlockSpec(memory_space=pl.ANY)],
        out_specs=pl.BlockSpec(memory_space=pl.ANY),
        scratch_shapes=[pltpu.SemaphoreType.DMA((3,)), pltpu.SemaphoreType.DMA((3,)), pltpu.SemaphoreType.DMA],
    )(part)


def _sum_chips(r4, *, name):
    _, t, c_ = r4.shape
    tr = _pick(t, 512, 16)

    def body(r_ref, o_ref):
        acc = r_ref[0].astype(F32)
        for ch in range(1, 4):
            acc = acc + r_ref[ch].astype(F32)
        o_ref[...] = acc

    return pl.pallas_call(
        body, name=name, grid=(t // tr,),
        in_specs=[pl.BlockSpec((4, tr, c_), lambda i: (0, i, 0))],
        out_specs=pl.BlockSpec((tr, c_), lambda i: (i, 0)),
        out_shape=jax.ShapeDtypeStruct((t, c_), F32),
        compiler_params=_cparams("parallel"),
    )(r4)


BIG = (
    ("w_in", (D_MODEL, IN_COLS // N_DEV), 1),
    ("w_branch", (3, BRANCH_W, D_MODEL // N_DEV), 2),
    ("w_out", (D_MODEL // N_DEV, D_MODEL), 0),
    ("w_q_xa", (D_MODEL // N_DEV, D_MODEL), 0),
    ("w_k_xa", (D_MODEL // N_DEV, D_MODEL), 0),
    ("w_v_xa", (D_MODEL // N_DEV, D_MODEL), 0),
    ("w_o_xa", (D_MODEL // N_DEV, D_MODEL), 0),
    ("w_gate_ffn", (D_MODEL, FFN // N_DEV), 1),
    ("w_up_ffn", (D_MODEL, FFN // N_DEV), 1),
    ("w_down_ffn", (FFN // N_DEV, D_MODEL), 0),
)
_BIG_LAYOUT = {n: (shp, ax) for n, shp, ax in BIG}
PACK_COLS = 1024


def _size(shape):
    n = 1
    for d in shape:
        n *= d
    return n


def _pack_shards(items, shards):
    return jnp.concatenate([shards[it].reshape(-1, PACK_COLS) for it in items], axis=0)


def _unpack_gathered(items, g):
    out = {}
    r0 = 0
    for it in items:
        shp, ax = _BIG_LAYOUT[it[0]]
        rows = _size(shp) // PACK_COLS
        blk = g[:, r0:r0 + rows].reshape((N_DEV,) + shp)
        r0 += rows
        blk = jnp.moveaxis(blk, 0, ax)
        full = list(shp)
        full[ax] = shp[ax] * N_DEV
        out[it] = blk.reshape(full)
    return out


def _pack_full(items, full):
    parts = []
    for it in items:
        shp, ax = _BIG_LAYOUT[it[0]]
        t = full[it].reshape(shp[:ax] + (N_DEV, shp[ax]) + shp[ax + 1:])
        t = jnp.moveaxis(t, ax, 0)
        parts.append(t.reshape(N_DEV, -1, PACK_COLS))
    return jnp.concatenate(parts, axis=1)


def _unpack_shard(items, flat):
    out = {}
    r0 = 0
    for it in items:
        shp, _ = _BIG_LAYOUT[it[0]]
        rows = _size(shp) // PACK_COLS
        out[it] = flat[r0:r0 + rows].reshape(shp)
        r0 += rows
    return out


SMALL = (
    ("norm_mix_g", (DEPTH, D_MODEL)),
    ("sgu_ln_g", (DEPTH, BRANCH_W)),
    ("sgu_ln_b", (DEPTH, BRANCH_W)),
    ("w_spatial", (DEPTH, SGU_GROUPS, SGU_LEN, SGU_LEN)),
    ("b_spatial", (DEPTH, SGU_GROUPS, SGU_LEN)),
    ("conv_w", (DEPTH, 3, BRANCH_W)),
    ("norm_xa_g", (DEPTH, D_MODEL)),
    ("mem_norm_g", (DEPTH, D_MODEL)),
    ("norm_ffn_g", (DEPTH, D_MODEL)),
    ("final_g", (D_MODEL,)),
)


def _pack_small(grads):
    flat = jnp.concatenate([grads[n].reshape(-1) for n, _ in SMALL])
    rows = -(-flat.shape[0] // PACK_COLS)
    rows = -(-rows // 8) * 8
    flat = jnp.pad(flat, (0, rows * PACK_COLS - flat.shape[0]))
    return flat.reshape(rows, PACK_COLS)


def _unpack_small(buf):
    flat = buf.reshape(-1)
    out = {}
    o = 0
    for n, shp in SMALL:
        out[n] = flat[o:o + _size(shp)].reshape(shp)
        o += _size(shp)
    return out


def _layer_fwd(l, x, mem, wt, sm, gather=None):
    t = f"l{l}_"
    sv = {"x0": x}
    h = _rms_fwd(x, sm["norm_mix_g"][l][None], name=t + "rms_mix")
    p = _mm(h, wt["w_in", l], name=t + "in_proj", tn=1024)
    if gather is None:
        ya = _sb_fwd(p, name=t + "sb_fwd")
    else:
        ya, gathered = _sb_fwd(p, name=t + "sb_fwd", gather=gather[1])
        wt.update(_unpack_gathered(gather[0], gathered))
    w_sp = sm["w_spatial"][l]
    b_col = sm["b_spatial"][l][:, :, None]
    ln_g, ln_b = sm["sgu_ln_g"][l][None], sm["sgu_ln_b"][l][None]
    yb = _sgu_fwd(p, ln_g, ln_b, w_sp, b_col, name=t + "sgu_fwd")
    yc = _conv_fwd(p, sm["conv_w"][l], name=t + "conv_fwd")
    merged = _merge_fwd(ya, yb, yc, wt["w_branch", l], p, name=t + "merge_fwd")
    x1 = _mm(merged, wt["w_out", l], add=x, name=t + "out_proj")
    sv.update(h=h, p=p, ya=ya, yb=yb, yc=yc, merged=merged, x1=x1)

    h2 = _rms_fwd(x1, sm["norm_xa_g"][l][None], name=t + "rms_xa")
    mn = _rms_fwd(mem, sm["mem_norm_g"][l][None], name=t + "rms_mem")
    q = _mm(h2, wt["w_q_xa", l], out_dtype=BF16, name=t + "xa_q")
    k = _mm(mn, wt["w_k_xa", l], out_dtype=BF16, name=t + "xa_k")
    v = _mm(mn, wt["w_v_xa", l], out_dtype=BF16, name=t + "xa_v")
    o = _xa_fwd(q, k, v, name=t + "xa_fwd")
    x2 = _mm(o, wt["w_o_xa", l], add=x1, name=t + "xa_o")
    sv.update(h2=h2, mn=mn, q=q, k=k, v=v, o=o, x2=x2)

    h3 = _rms_fwd(x2, sm["norm_ffn_g"][l][None], name=t + "rms_ffn")
    a = _mm(h3, wt["w_gate_ffn", l], name=t + "ffn_gate", tn=1408)
    b = _mm(h3, wt["w_up_ffn", l], name=t + "ffn_up", tn=1408)
    hd = _swiglu_fwd(a, b, name=t + "swiglu_fwd")
    x3 = _mm(hd, wt["w_down_ffn", l], add=x2, name=t + "ffn_down")
    sv.update(h3=h3, a=a, b=b, hd=hd)
    return x3, sv


def _layer_bwd(l, dx3, mem, wt, sm, sv, scatter=None):
    t = f"l{l}_b_"
    gb, gs = {}, {}
    dhd = _mm(dx3, wt["w_down_ffn", l], tb=True, name=t + "ffn_down_dx", tn=1408)
    gb["w_down_ffn"] = _mm(sv["hd"], dx3, ta=True, out_dtype=BF16, name=t + "ffn_down_dw", tm=1408)
    da, db = _swiglu_bwd(sv["a"], sv["b"], dhd, name=t + "swiglu_bwd")
    dh3 = _mm(da, wt["w_gate_ffn", l], tb=True, name=t + "ffn_gate_dx")
    dh3 = _mm(db, wt["w_up_ffn", l], tb=True, add=dh3, name=t + "ffn_up_dx")
    gb["w_gate_ffn"] = _mm(sv["h3"], da, ta=True, out_dtype=BF16, name=t + "ffn_gate_dw", tn=1408)
    gb["w_up_ffn"] = _mm(sv["h3"], db, ta=True, out_dtype=BF16, name=t + "ffn_up_dw", tn=1408)
    dx2, dg = _rms_bwd(sv["x2"], sm["norm_ffn_g"][l][None], dh3, dx3, name=t + "rms_ffn")
    gs["norm_ffn_g"] = dg[0]
    do = _mm(dx2, wt["w_o_xa", l], tb=True, out_dtype=BF16, name=t + "xa_o_dx")
    gb["w_o_xa"] = _mm(sv["o"], dx2, ta=True, out_dtype=BF16, name=t + "xa_o_dw")
    dq, dk, dv = _xa_bwd(sv["q"], sv["k"], sv["v"], do, name=t + "xa_bwd")
    dh2 = _mm(dq, wt["w_q_xa", l], tb=True, name=t + "xa_q_dx")
    gb["w_q_xa"] = _mm(sv["h2"], dq, ta=True, out_dtype=BF16, name=t + "xa_q_dw")
    gb["w_k_xa"] = _mm(sv["mn"], dk, ta=True, out_dtype=BF16, name=t + "xa_k_dw")
    gb["w_v_xa"] = _mm(sv["mn"], dv, ta=True, out_dtype=BF16, name=t + "xa_v_dw")
    dmn = _mm(dk, wt["w_k_xa", l], tb=True, name=t + "xa_k_dx")
    dmn = _mm(dv, wt["w_v_xa", l], tb=True, add=dmn, name=t + "xa_v_dx")
    _, dg = _rms_bwd(mem, sm["mem_norm_g"][l][None], dmn, jnp.zeros_like(mem), name=t + "rms_mem")
    gs["mem_norm_g"] = dg[0]
    dx1, dg = _rms_bwd(sv["x1"], sm["norm_xa_g"][l][None], dh2, dx2, name=t + "rms_xa")
    gs["norm_xa_g"] = dg[0]
    dm = _mm(dx1, wt["w_out", l], tb=True, name=t + "out_proj_dx")
    gb["w_out"] = _mm(sv["merged"], dx1, ta=True, out_dtype=BF16, name=t + "out_proj_dw")
    p = sv["p"]
    dya, dyb, dyc, dgates, dbrd = _merge_bwd(dm, sv["ya"], sv["yb"], sv["yc"], wt["w_branch", l], p,
                                             name=t + "merge_bwd")
    gb["w_branch"] = jnp.stack([
        _mm(sv[y], dbrd[n], ta=True, out_dtype=BF16, name=t + f"branch{n}_dw")
        for n, y in enumerate(("ya", "yb", "yc"))])
    dcb, dcc, dcx, dcw = _conv_bwd(p, dyc, sm["conv_w"][l], name=t + "conv_bwd")
    gs["conv_w"] = dcw
    w_sp = sm["w_spatial"][l]
    dz, dlg, dlb, dwsp, dbsp = _sgu_bwd(p, dyb, sm["sgu_ln_g"][l][None], sm["sgu_ln_b"][l][None], w_sp,
                                        jnp.swapaxes(w_sp, 1, 2), sm["b_spatial"][l][:, :, None],
                                        name=t + "sgu_bwd")
    gs.update(sgu_ln_g=dlg[0], sgu_ln_b=dlb[0], w_spatial=dwsp, b_spatial=dbsp[:, :, 0])
    received = None
    if scatter is None:
        dq_a, dk_a, dv_a = _sb_bwd(p, dya, name=t + "sb_bwd")
    else:
        items, earlier = scatter
        ready = {**earlier, **{(n, l): g for n, g in gb.items()}}
        dq_a, dk_a, dv_a, received = _sb_bwd(p, dya, name=t + "sb_bwd", scatter=_pack_full(items, ready))
    dp = jnp.concatenate([dq_a, dk_a, dv_a, dz, dcb, dcc, dcx, dgates], axis=1)
    dh = _mm(dp, wt["w_in", l], tb=True, name=t + "in_proj_dx")
    gb["w_in"] = _mm(sv["h"], dp, ta=True, out_dtype=BF16, name=t + "in_proj_dw", tn=1024)
    dx, dg = _rms_bwd(sv["x0"], sm["norm_mix_g"][l][None], dh, dx1, name=t + "rms_mix")
    gs["norm_mix_g"] = dg[0]
    return dx, gb, gs, received


_WEIGHTS = ("norm_mix_g", "w_in", "sgu_ln_g", "sgu_ln_b", "w_spatial", "b_spatial", "conv_w", "w_branch", "w_out",
            "norm_xa_g", "mem_norm_g", "w_q_xa", "w_k_xa", "w_v_xa", "w_o_xa", "norm_ffn_g", "w_gate_ffn",
            "w_up_ffn", "w_down_ffn", "final_g")


def kernel(x, mem, norm_mix_g, w_in, sgu_ln_g, sgu_ln_b, w_spatial, b_spatial, conv_w, w_branch, w_out, norm_xa_g, mem_norm_g, w_q_xa, w_k_xa, w_v_xa, w_o_xa, norm_ffn_g, w_gate_ffn, w_up_ffn, w_down_ffn, final_g, loss_target, m_norm_mix_g, m_w_in, m_sgu_ln_g, m_sgu_ln_b, m_w_spatial, m_b_spatial, m_conv_w, m_w_branch, m_w_out, m_norm_xa_g, m_mem_norm_g, m_w_q_xa, m_w_k_xa, m_w_v_xa, m_w_o_xa, m_norm_ffn_g, m_w_gate_ffn, m_w_up_ffn, m_w_down_ffn, m_final_g, v_norm_mix_g, v_w_in, v_sgu_ln_g, v_sgu_ln_b, v_w_spatial, v_b_spatial, v_conv_w, v_w_branch, v_w_out, v_norm_xa_g, v_mem_norm_g, v_w_q_xa, v_w_k_xa, v_w_v_xa, v_w_o_xa, v_norm_ffn_g, v_w_gate_ffn, v_w_up_ffn, v_w_down_ffn, v_final_g):
    w = dict(norm_mix_g=norm_mix_g, w_in=w_in, sgu_ln_g=sgu_ln_g, sgu_ln_b=sgu_ln_b, w_spatial=w_spatial,
             b_spatial=b_spatial, conv_w=conv_w, w_branch=w_branch, w_out=w_out, norm_xa_g=norm_xa_g,
             mem_norm_g=mem_norm_g, w_q_xa=w_q_xa, w_k_xa=w_k_xa, w_v_xa=w_v_xa, w_o_xa=w_o_xa,
             norm_ffn_g=norm_ffn_g, w_gate_ffn=w_gate_ffn, w_up_ffn=w_up_ffn, w_down_ffn=w_down_ffn, final_g=final_g)
    m = dict(norm_mix_g=m_norm_mix_g, w_in=m_w_in, sgu_ln_g=m_sgu_ln_g, sgu_ln_b=m_sgu_ln_b, w_spatial=m_w_spatial,
             b_spatial=m_b_spatial, conv_w=m_conv_w, w_branch=m_w_branch, w_out=m_w_out, norm_xa_g=m_norm_xa_g,
             mem_norm_g=m_mem_norm_g, w_q_xa=m_w_q_xa, w_k_xa=m_w_k_xa, w_v_xa=m_w_v_xa, w_o_xa=m_w_o_xa,
             norm_ffn_g=m_norm_ffn_g, w_gate_ffn=m_w_gate_ffn, w_up_ffn=m_w_up_ffn, w_down_ffn=m_w_down_ffn,
             final_g=m_final_g)
    v = dict(norm_mix_g=v_norm_mix_g, w_in=v_w_in, sgu_ln_g=v_sgu_ln_g, sgu_ln_b=v_sgu_ln_b, w_spatial=v_w_spatial,
             b_spatial=v_b_spatial, conv_w=v_conv_w, w_branch=v_w_branch, w_out=v_w_out, norm_xa_g=v_norm_xa_g,
             mem_norm_g=v_mem_norm_g, w_q_xa=v_w_q_xa, w_k_xa=v_w_k_xa, w_v_xa=v_w_v_xa, w_o_xa=v_w_o_xa,
             norm_ffn_g=v_norm_ffn_g, w_gate_ffn=v_w_gate_ffn, w_up_ffn=v_w_up_ffn, w_down_ffn=v_w_down_ffn,
             final_g=v_final_g)

    names = [n for n, _, _ in BIG]
    shards = {(n, l): w[n][l].astype(BF16) for n in names for l in range(DEPTH)}
    first_items = [("w_in", 0)]
    later_items = [(n, l) for l in range(DEPTH) for n in names if (n, l) != ("w_in", 0)]
    wt = _unpack_gathered(first_items, _all_gather(_pack_shards(first_items, shards), name="gather_w_in0"))
    cw_pad = jnp.zeros((8, 128), F32).at[:DEPTH * 3, :BRANCH_W // N_DEV].set(conv_w.reshape(DEPTH * 3, -1))
    cw_all = _all_gather(cw_pad, name="gather_conv_w")[:, :DEPTH * 3, :BRANCH_W // N_DEV]
    conv_full = jnp.moveaxis(cw_all.reshape(N_DEV, DEPTH, 3, BRANCH_W // N_DEV), 0, 2).reshape(DEPTH, 3, BRANCH_W)
    sm = {n: w[n] for n, _ in SMALL}
    sm["conv_w"] = conv_full

    xs, ms = x[0], mem[0]
    x1, saved0 = _layer_fwd(0, xs, ms, wt, sm, gather=(later_items, _pack_shards(later_items, shards)))
    x2, saved1 = _layer_fwd(1, x1, ms, wt, sm)
    dcur, loss, dfinal = _final_loss(x2, sm["final_g"][None], loss_target[0], name="final_loss")
    loss = lax.psum(loss[0, 0], AXES)
    items_a = [(n, 1) for n in names if n != "w_in"]
    items_b = [("w_in", 1)] + [(n, 0) for n in names if n != "w_in"]
    items_c = [("w_in", 0)]
    dcur, gb1, gs1, recv_a = _layer_bwd(1, dcur, ms, wt, sm, saved1, scatter=(items_a, {}))
    dx, gb0, gs0, recv_b = _layer_bwd(0, dcur, ms, wt, sm, saved0, scatter=(items_b, {("w_in", 1): gb1["w_in"]}))

    shard_grads = _unpack_shard(items_a, _sum_devices(recv_a, name="rs_sum_a"))
    shard_grads.update(_unpack_shard(items_b, _sum_devices(recv_b, name="rs_sum_b")))
    g8 = _pack_full(items_c, {("w_in", 0): gb0["w_in"]})
    core = lax.axis_index("c").astype(jnp.int32).reshape(1)
    from_sibling = _rs_pair_exchange(g8, name="rs_pair_exchange")
    part = _pair_add(core, g8, from_sibling, name="rs_pair_add")
    by_chip = _rs_chip_exchange(part, name="rs_chip_exchange")
    shard_grads.update(_unpack_shard(items_c, _sum_chips(by_chip, name="rs_sum_chips")))
    grads = {n: jnp.stack([shard_grads[n, l] for l in range(DEPTH)]) for n in names}
    small = {n: jnp.stack([gs0[n], gs1[n]]) for n, _ in SMALL if n != "final_g"}
    small["final_g"] = dfinal[0]
    small_sum = _unpack_small(_all_reduce_small(_pack_small(small), name="all_reduce_small"))
    width = BRANCH_W // N_DEV
    dev = 4 * lax.axis_index("x") + 2 * lax.axis_index("y") + lax.axis_index("c")
    for n, _ in SMALL:
        grads[n] = small_sum[n]
    grads["conv_w"] = lax.dynamic_slice_in_dim(small_sum["conv_w"], dev * width, width, axis=2)

    delta, new_m, new_v = {}, {}, {}
    for n in _WEIGHTS:
        shp = w[n].shape
        two_d = (-1, shp[-1])
        d_, m_, v_ = _adamw(w[n].reshape(two_d), grads[n].reshape(two_d), m[n].reshape(two_d), v[n].reshape(two_d),
                            name="adamw_" + n)
        delta[n], new_m[n], new_v[n] = d_.reshape(shp), m_.reshape(shp), v_.reshape(shp)

    return (loss, dx[None], *[grads[n] for n in _WEIGHTS], *[delta[n] for n in _WEIGHTS],
            *[new_m[n] for n in _WEIGHTS], *[new_v[n] for n in _WEIGHTS])
```

```python
import functools

import jax
import jax.numpy as jnp
from jax import lax
from jax.experimental import pallas as pl
from jax.experimental.pallas import tpu as pltpu

F32 = jnp.float32
BF16 = jnp.bfloat16
MESH = pl.DeviceIdType.MESH

D_MODEL = 1024
BRANCH_W = 512
IN_COLS = 7168
FFN = 2816
N_DEV = 8
DEPTH = 2
SB_BLOCK = 128
SB_SPAN = 1024
SB_SCALE = 0.125
XA_HEAD = 256
XA_SCALE = 0.0625
SGU_LEN = 128
SGU_GROUPS = 4
RMS_EPS = 1e-6
LN_EPS = 1e-5
HALO = 8

ADAM_LR = 0.001
ADAM_B1 = 0.9
ADAM_B2 = 0.999
ADAM_EPS = 1e-08
ADAM_WD = 0.01
ADAM_STEP = 10

VMEM_LIMIT_BYTES = 52 * 1024 * 1024

AXES = ("x", "y", "c")


def _cparams(*sem):
    return pltpu.CompilerParams(dimension_semantics=sem, vmem_limit_bytes=VMEM_LIMIT_BYTES)


def _pick(n, target, align):
    t = (min(target, n) // align) * align
    while t >= align:
        if n % t == 0:
            return t
        t -= align
    return n


def _dot(a, b):
    return jnp.dot(a, b, preferred_element_type=F32)


def _dot_nt(a, b):
    return lax.dot_general(a, b, (((1,), (1,)), ((), ())), preferred_element_type=F32)


def _dot_tn(a, b):
    return lax.dot_general(a, b, (((0,), (0,)), ((), ())), preferred_element_type=F32)


def _sigmoid(x):
    return 1.0 / (1.0 + jnp.exp(-x))


def _mm(a, b, *, name, ta=False, tb=False, out_dtype=F32, add=None, rms=None, tm=1024, tn=1024, tk=2048):
    m, k = (a.shape[1], a.shape[0]) if ta else a.shape
    n = b.shape[0] if tb else b.shape[1]
    assert k == (b.shape[1] if tb else b.shape[0])
    tm = _pick(m, tm, 128)
    tn = n if rms is not None else _pick(n, tn, 128)
    tk = _pick(k, tk, 128)
    nk = k // tk
    ca = 0 if ta else 1
    cb = 1 if tb else 0
    assert add is None or rms is None

    def body(*refs):
        refs = list(refs)
        a_ref, b_ref = refs[:2]
        extra = refs[2:2 + (1 if add is not None else 3 if rms is not None else 0)]
        outs = refs[2 + len(extra):]
        o_ref = outs[0]
        kk = pl.program_id(2)
        first_row_tile = pl.program_id(0) == 0

        def product():
            return lax.dot_general(a_ref[...].astype(BF16), b_ref[...].astype(BF16),
                                   (((ca,), (cb,)), ((), ())), preferred_element_type=F32)

        def finish(r):
            if add is not None:
                r = r + extra[0][...]
            if rms is None:
                o_ref[...] = r.astype(out_dtype)
                return
            x_ref, g_ref, dres_ref = extra
            dg_ref = outs[1]

            @pl.when(first_row_tile)
            def _():
                dg_ref[...] = jnp.zeros_like(dg_ref)

            xv = x_ref[...]
            rs = lax.rsqrt(jnp.mean(xv * xv, axis=-1, keepdims=True) + RMS_EPS)
            xh = xv * rs
            dg_ref[...] += jnp.sum(r * xh, axis=0, keepdims=True)
            dxh = r * g_ref[...]
            o_ref[...] = dres_ref[...] + rs * (dxh - xh * jnp.mean(dxh * xh, axis=-1, keepdims=True))

        if nk == 1:
            finish(product())
        else:
            acc_ref = outs[-1]

            @pl.when(kk == 0)
            def _():
                acc_ref[...] = jnp.zeros_like(acc_ref)

            acc_ref[...] += product()

            @pl.when(kk == nk - 1)
            def _():
                finish(acc_ref[...])

    a_spec = pl.BlockSpec((tk, tm), lambda i, j, kk: (kk, i)) if ta else pl.BlockSpec((tm, tk), lambda i, j, kk: (i, kk))
    b_spec = pl.BlockSpec((tn, tk), lambda i, j, kk: (j, kk)) if tb else pl.BlockSpec((tk, tn), lambda i, j, kk: (kk, j))
    tile = pl.BlockSpec((tm, tn), lambda i, j, kk: (i, j))
    in_specs = [a_spec, b_spec]
    operands = [a, b]
    out_specs = [tile]
    out_shape = [jax.ShapeDtypeStruct((m, n), out_dtype)]
    if add is not None:
        in_specs.append(tile)
        operands.append(add)
    if rms is not None:
        vec = pl.BlockSpec((1, n), lambda i, j, kk: (0, 0))
        in_specs += [tile, vec, tile]
        operands += list(rms)
        out_specs.append(vec)
        out_shape = [jax.ShapeDtypeStruct((m, n), F32), jax.ShapeDtypeStruct((1, n), F32)]
    out = pl.pallas_call(
        body, name=name,
        grid=(m // tm, n // tn, nk),
        in_specs=in_specs, out_specs=out_specs, out_shape=out_shape,
        scratch_shapes=[pltpu.VMEM((tm, tn), F32)] if nk > 1 else [],
        compiler_params=_cparams("arbitrary" if rms is not None else "parallel", "parallel", "arbitrary"),
    )(*operands)
    return out[0] if rms is None else out


def _rms_fwd(x, g, *, name):
    r, d = x.shape
    tr = _pick(r, 512, 16)

    def body(x_ref, g_ref, o_ref):
        xv = x_ref[...]
        rs = lax.rsqrt(jnp.mean(xv * xv, axis=-1, keepdims=True) + RMS_EPS)
        o_ref[...] = (xv * rs * g_ref[...]).astype(BF16)

    return pl.pallas_call(
        body, name=name, grid=(r // tr,),
        in_specs=[pl.BlockSpec((tr, d), lambda i: (i, 0)), pl.BlockSpec((1, d), lambda i: (0, 0))],
        out_specs=pl.BlockSpec((tr, d), lambda i: (i, 0)),
        out_shape=jax.ShapeDtypeStruct((r, d), BF16),
        compiler_params=_cparams("parallel"),
    )(x, g)


def _rms_bwd(x, g, dh, dres, *, name):
    r, d = x.shape
    tr = _pick(r, 256, 8)

    def body(x_ref, g_ref, dh_ref, dres_ref, dx_ref, dg_ref):
        @pl.when(pl.program_id(0) == 0)
        def _():
            dg_ref[...] = jnp.zeros_like(dg_ref)

        xv = x_ref[...]
        dhv = dh_ref[...].astype(F32)
        rs = lax.rsqrt(jnp.mean(xv * xv, axis=-1, keepdims=True) + RMS_EPS)
        xh = xv * rs
        dg_ref[...] += jnp.sum(dhv * xh, axis=0, keepdims=True)
        dxh = dhv * g_ref[...]
        dx_ref[...] = dres_ref[...] + rs * (dxh - xh * jnp.mean(dxh * xh, axis=-1, keepdims=True))

    return pl.pallas_call(
        body, name=name, grid=(r // tr,),
        in_specs=[pl.BlockSpec((tr, d), lambda i: (i, 0)), pl.BlockSpec((1, d), lambda i: (0, 0)),
                  pl.BlockSpec((tr, d), lambda i: (i, 0)), pl.BlockSpec((tr, d), lambda i: (i, 0))],
        out_specs=[pl.BlockSpec((tr, d), lambda i: (i, 0)), pl.BlockSpec((1, d), lambda i: (0, 0))],
        out_shape=[jax.ShapeDtypeStruct((r, d), F32), jax.ShapeDtypeStruct((1, d), F32)],
        compiler_params=_cparams("arbitrary"),
    )(x, g, dh, dres)


def _final_loss(x, g, target, *, name):
    r, d = x.shape
    tr = _pick(r, 256, 8)

    def body(x_ref, g_ref, t_ref, dx_ref, loss_ref, dg_ref):
        @pl.when(pl.program_id(0) == 0)
        def _():
            dg_ref[...] = jnp.zeros_like(dg_ref)
            loss_ref[...] = jnp.zeros_like(loss_ref)

        xv = x_ref[...]
        gv = g_ref[...]
        rs = lax.rsqrt(jnp.mean(xv * xv, axis=-1, keepdims=True) + RMS_EPS)
        xh = xv * rs
        err = xh * gv - t_ref[...]
        row_loss = jnp.mean(err * err, axis=-1, keepdims=True)
        loss_ref[...] += 0.5 * jnp.sum(row_loss, axis=0, keepdims=True)
        dy = err * (1.0 / d)
        dg_ref[...] += jnp.sum(dy * xh, axis=0, keepdims=True)
        dxh = dy * gv
        dx_ref[...] = rs * (dxh - xh * jnp.mean(dxh * xh, axis=-1, keepdims=True))

    return pl.pallas_call(
        body, name=name, grid=(r // tr,),
        in_specs=[pl.BlockSpec((tr, d), lambda i: (i, 0)), pl.BlockSpec((1, d), lambda i: (0, 0)),
                  pl.BlockSpec((tr, d), lambda i: (i, 0))],
        out_specs=[pl.BlockSpec((tr, d), lambda i: (i, 0)), pl.BlockSpec((1, 128), lambda i: (0, 0)),
                   pl.BlockSpec((1, d), lambda i: (0, 0))],
        out_shape=[jax.ShapeDtypeStruct((r, d), F32), jax.ShapeDtypeStruct((1, 128), F32),
                   jax.ShapeDtypeStruct((1, d), F32)],
        compiler_params=_cparams("arbitrary"),
    )(x, g, target)


def _cumsum_operand(strict_after, terms=1):
    r = lax.broadcasted_iota(jnp.int32, (terms * SB_BLOCK, 2 * SB_BLOCK), 0) % SB_BLOCK
    c = lax.broadcasted_iota(jnp.int32, (terms * SB_BLOCK, 2 * SB_BLOCK), 1)
    tri = (r > c) if strict_after else (r < c)
    return jnp.where((c >= SB_BLOCK) | tri, 1.0, 0.0).astype(BF16)


def _sb_scores(qh, kw, run, valid, after_ones):
    nb = kw.shape[0] // SB_BLOCK
    z = _dot_nt(qh, kw)
    lsp = jnp.minimum(z, 0.0) - jnp.log(1.0 + jnp.exp(-jnp.abs(z)))
    l1m = lsp - z
    if valid is not None:
        l1m = jnp.where(valid, l1m, 0.0)
    hi = l1m.astype(BF16)
    lo = (l1m - hi.astype(F32)).astype(BF16)
    later = [None] * nb
    for b in reversed(range(nb)):
        cols = slice(b * SB_BLOCK, (b + 1) * SB_BLOCK)
        ct = _dot(jnp.concatenate([hi[:, cols], lo[:, cols]], axis=1), after_ones)
        later[b] = run + ct[:, :SB_BLOCK]
        run = run + ct[:, SB_BLOCK:]
    a = jnp.exp(lsp + jnp.concatenate(later, axis=1))
    if valid is not None:
        a = jnp.where(valid, a, 0.0)
    return lsp, a, run


def _sb_setup(q_ref, span):
    qi = pl.program_id(1)
    per = span // SB_BLOCK
    sd = qi // per
    lane = lax.broadcasted_iota(jnp.int32, (SB_BLOCK, SB_BLOCK), 1)
    col = lax.broadcasted_iota(jnp.int32, (SB_BLOCK, span), 1)
    row = lax.broadcasted_iota(jnp.int32, (SB_BLOCK, span), 0)
    valid = col < (qi - sd * per) * SB_BLOCK + row
    q = q_ref[...] * SB_SCALE
    qhs = (jnp.where(lane < 64, q, 0.0).astype(BF16), jnp.where(lane >= 64, q, 0.0).astype(BF16))
    return lane, sd, valid, qhs


def _sb_fwd(p, *, name, gather=None):
    s = p.shape[0]
    nq = s // SB_BLOCK
    kcol = BRANCH_W // SB_BLOCK
    span = min(SB_SPAN, s)

    def body(*refs):
        if gather is None:
            q_ref, k_ref, v_ref, o_ref = refs
        else:
            q_ref, k_ref, v_ref, x_ref, o_ref, g_ref, send_sems, recv_sems, local_sem = refs
            start, forward, finish = _gather_phases(x_ref, g_ref, send_sems, recv_sems, local_sem)
            step = pl.program_id(0) * nq + pl.program_id(1)
            pl.when(step == 0)(start)
        lane, sd, valid, qhs = _sb_setup(q_ref, span)
        after_ones = _cumsum_operand(True, terms=2)
        zero = jnp.zeros((SB_BLOCK, SB_BLOCK), F32)

        def span_step(sb, carry, mask):
            rows = pl.ds(pl.multiple_of(sb * span, span), span)
            kw = k_ref[rows, :].astype(BF16)
            vw = v_ref[rows, :].astype(BF16)
            out = []
            for h in range(2):
                run, acc = carry[h]
                _, a, run = _sb_scores(qhs[h], kw, run, mask, after_ones)
                out.append((run, acc + _dot(a.astype(BF16), vw)))
            return tuple(out)

        carry = span_step(sd, ((zero, zero), (zero, zero)), valid)
        carry = lax.fori_loop(0, sd, lambda t, c: span_step(sd - 1 - t, c, None), carry)
        o_ref[...] = jnp.where(lane < 64, carry[0][1], carry[1][1])
        if gather is not None:
            pl.when(step == (kcol - 1) * nq + (3 * nq) // 4)(forward)
            pl.when(step == kcol * nq - 1)(finish)

    in_specs = [pl.BlockSpec((SB_BLOCK, SB_BLOCK), lambda hp, qi: (qi, hp)),
                pl.BlockSpec((s, SB_BLOCK), lambda hp, qi: (0, kcol + hp)),
                pl.BlockSpec((s, SB_BLOCK), lambda hp, qi: (0, 2 * kcol + hp))]
    out_specs = [pl.BlockSpec((SB_BLOCK, SB_BLOCK), lambda hp, qi: (qi, hp))]
    out_shape = [jax.ShapeDtypeStruct((s, BRANCH_W), F32)]
    operands = [p, p, p]
    scratch = []
    if gather is not None:
        in_specs.append(pl.BlockSpec(memory_space=pl.ANY))
        out_specs.append(pl.BlockSpec(memory_space=pl.ANY))
        out_shape.append(jax.ShapeDtypeStruct((N_DEV,) + gather.shape, gather.dtype))
        operands.append(gather)
        scratch = _GATHER_SEMS
    out = pl.pallas_call(
        body, name=name, grid=(kcol, nq), in_specs=in_specs, out_specs=out_specs, out_shape=out_shape,
        scratch_shapes=scratch, compiler_params=_cparams("arbitrary", "arbitrary"),
    )(*operands)
    return out[0] if gather is None else out


def _sb_bwd(p, dya, *, name, scatter=None):
    s = p.shape[0]
    nq = s // SB_BLOCK
    kcol = BRANCH_W // SB_BLOCK
    span = min(SB_SPAN, s)
    per = span // SB_BLOCK

    def body(*refs):
        if scatter is None:
            q_ref, k_ref, v_ref, do_ref, dq_ref, dk_ref, dv_ref, a_s, b_s, dk_acc, dv_acc = refs
        else:
            (q_ref, k_ref, v_ref, do_ref, g_ref, dq_ref, dk_ref, dv_ref, r_ref,
             a_s, b_s, dk_acc, dv_acc, send_sems, recv_sems, local_sem) = refs
            start, finish = _scatter_phases(g_ref, r_ref, send_sems, recv_sems, local_sem)
            step = pl.program_id(0) * nq + pl.program_id(1)
            pl.when(step == 0)(start)
        qi = pl.program_id(1)

        @pl.when(qi == 0)
        def _():
            dk_acc[...] = jnp.zeros_like(dk_acc)
            dv_acc[...] = jnp.zeros_like(dv_acc)

        lane, sd, valid, qhs = _sb_setup(q_ref, span)
        after_ones = _cumsum_operand(True, terms=2)
        before_ones = _cumsum_operand(False)
        do = do_ref[...]
        dohs = (jnp.where(lane < 64, do, 0.0).astype(BF16), jnp.where(lane >= 64, do, 0.0).astype(BF16))
        zero = jnp.zeros((SB_BLOCK, SB_BLOCK), F32)

        def rebuild(sb, runs, mask):
            rows = pl.ds(pl.multiple_of(sb * span, span), span)
            kw = k_ref[rows, :].astype(BF16)
            out = []
            for h in range(2):
                lsp, a, run = _sb_scores(qhs[h], kw, runs[h], mask, after_ones)
                beta = jnp.exp(lsp)
                if mask is not None:
                    beta = jnp.where(mask, beta, 0.0)
                a_s[h, sb] = a
                b_s[h, sb] = beta
                out.append(run)
            return tuple(out)

        runs = rebuild(sd, (zero, zero), valid)
        lax.fori_loop(0, sd, lambda t, r: rebuild(sd - 1 - t, r, None), runs)

        def accumulate(sb, carry):
            rows = pl.ds(pl.multiple_of(sb * span, span), span)
            kw = k_ref[rows, :].astype(BF16)
            vw = v_ref[rows, :].astype(BF16)
            out = []
            dk_span = jnp.zeros((span, SB_BLOCK), F32)
            dv_span = jnp.zeros((span, SB_BLOCK), F32)
            for h in range(2):
                pg, dq = carry[h]
                a = a_s[h, sb]
                beta = b_s[h, sb]
                g = a * _dot_nt(dohs[h], vw)
                gb = g.astype(BF16)
                before = [None] * per
                for b in range(per):
                    cols = slice(b * SB_BLOCK, (b + 1) * SB_BLOCK)
                    gt = _dot(gb[:, cols], before_ones)
                    before[b] = pg + gt[:, :SB_BLOCK]
                    pg = pg + gt[:, SB_BLOCK:]
                dz = (g * (1.0 - beta) - beta * jnp.concatenate(before, axis=1)).astype(BF16)
                dk_span = dk_span + _dot_tn(dz, qhs[h])
                dv_span = dv_span + _dot_tn(a.astype(BF16), dohs[h])
                out.append((pg, dq + _dot(dz, kw)))
            dk_acc[rows, :] += dk_span
            dv_acc[rows, :] += dv_span
            return tuple(out)

        carry = lax.fori_loop(0, sd + 1, accumulate, ((zero, zero), (zero, zero)))
        dq_ref[...] = (jnp.where(lane < 64, carry[0][1], carry[1][1]) * SB_SCALE).astype(BF16)

        @pl.when(qi == nq - 1)
        def _():
            dk_ref[...] = dk_acc[...].astype(BF16)
            dv_ref[...] = dv_acc[...].astype(BF16)

        if scatter is not None:
            pl.when(step == kcol * nq - 1)(finish)

    blk = pl.BlockSpec((SB_BLOCK, SB_BLOCK), lambda hp, qi: (qi, hp))
    col = pl.BlockSpec((s, SB_BLOCK), lambda hp, qi: (0, hp))
    out = jax.ShapeDtypeStruct((s, BRANCH_W), BF16)
    in_specs = [blk,
                pl.BlockSpec((s, SB_BLOCK), lambda hp, qi: (0, kcol + hp)),
                pl.BlockSpec((s, SB_BLOCK), lambda hp, qi: (0, 2 * kcol + hp)),
                blk]
    out_specs = [blk, col, col]
    out_shape = [out, out, out]
    operands = [p, p, p, dya]
    scratch = [pltpu.VMEM((2, s // span, SB_BLOCK, span), F32), pltpu.VMEM((2, s // span, SB_BLOCK, span), F32),
               pltpu.VMEM((s, SB_BLOCK), F32), pltpu.VMEM((s, SB_BLOCK), F32)]
    if scatter is not None:
        in_specs.append(pl.BlockSpec(memory_space=pl.ANY))
        out_specs.append(pl.BlockSpec(memory_space=pl.ANY))
        out_shape.append(jax.ShapeDtypeStruct(scatter.shape, scatter.dtype))
        operands.append(scatter)
        scratch = scratch + _SCATTER_SEMS
    return pl.pallas_call(
        body, name=name, grid=(kcol, nq), in_specs=in_specs, out_specs=out_specs, out_shape=out_shape,
        scratch_shapes=scratch, compiler_params=_cparams("arbitrary", "arbitrary"),
    )(*operands)


_INV_SQRT2 = 0.7071067811865476
_INV_SQRT2PI = 0.3989422804014327


def _gelu(x):
    return 0.5 * x * (1.0 + lax.erf(x * _INV_SQRT2))


def _gelu_grad(x):
    return 0.5 * (1.0 + lax.erf(x * _INV_SQRT2)) + x * _INV_SQRT2PI * jnp.exp(-0.5 * x * x)


def _chunk_mask(transposed=False):
    r = lax.broadcasted_iota(jnp.int32, (SGU_LEN, SGU_LEN), 0)
    c = lax.broadcasted_iota(jnp.int32, (SGU_LEN, SGU_LEN), 1)
    return (c // 64) >= (r // 64) if transposed else (r // 64) >= (c // 64)


def _sgu_norm(v_raw, g, b):
    zv = _gelu(v_raw)
    xc = zv - jnp.mean(zv, axis=-1, keepdims=True)
    rs = lax.rsqrt(jnp.mean(xc * xc, axis=-1, keepdims=True) + LN_EPS)
    xh = xc * rs
    return xh, rs, xh * g + b


def _sgu_fwd(p, ln_g, ln_b, w, b_col, *, name):
    s = p.shape[0]
    tr = _pick(s, 512, SGU_LEN)

    def body(u_ref, v_ref, g_ref, b_ref, w_ref, bc_ref, o_ref):
        mask = _chunk_mask()
        zu = _gelu(u_ref[...])
        _, _, vn = _sgu_norm(v_ref[...], g_ref[...], b_ref[...])
        vnb = vn.astype(BF16)
        for gi in range(SGU_GROUPS):
            wg = jnp.where(mask, w_ref[gi], 0.0).astype(BF16)
            cs = slice(gi * SGU_LEN, (gi + 1) * SGU_LEN)
            for c in range(tr // SGU_LEN):
                rs_ = slice(c * SGU_LEN, (c + 1) * SGU_LEN)
                vm = _dot(wg, vnb[rs_, cs]) + bc_ref[gi]
                o_ref[rs_, cs] = zu[rs_, cs] * vm

    vec = pl.BlockSpec((1, BRANCH_W), lambda i: (0, 0))
    return pl.pallas_call(
        body, name=name, grid=(s // tr,),
        in_specs=[pl.BlockSpec((tr, BRANCH_W), lambda i: (i, 3)), pl.BlockSpec((tr, BRANCH_W), lambda i: (i, 4)),
                  vec, vec,
                  pl.BlockSpec((SGU_GROUPS, SGU_LEN, SGU_LEN), lambda i: (0, 0, 0)),
                  pl.BlockSpec((SGU_GROUPS, SGU_LEN, 1), lambda i: (0, 0, 0))],
        out_specs=pl.BlockSpec((tr, BRANCH_W), lambda i: (i, 0)),
        out_shape=jax.ShapeDtypeStruct((s, BRANCH_W), F32),
        compiler_params=_cparams("parallel"),
    )(p, p, ln_g, ln_b, w, b_col)


def _sgu_bwd(p, dyb, ln_g, ln_b, w, w_t, b_col, *, name):
    s = p.shape[0]
    tr = _pick(s, 256, SGU_LEN)

    def body(u_ref, v_ref, dy_ref, g_ref, b_ref, w_ref, wt_ref, bc_ref,
             dz_ref, dg_ref, db_ref, dw_ref, dbc_ref, dvn_s):
        @pl.when(pl.program_id(0) == 0)
        def _():
            dg_ref[...] = jnp.zeros_like(dg_ref)
            db_ref[...] = jnp.zeros_like(db_ref)
            dw_ref[...] = jnp.zeros_like(dw_ref)
            dbc_ref[...] = jnp.zeros_like(dbc_ref)

        mask = _chunk_mask()
        mask_t = _chunk_mask(transposed=True)
        u_raw = u_ref[...]
        v_raw = v_ref[...]
        dy = dy_ref[...]
        zu = _gelu(u_raw)
        xh, rs, vn = _sgu_norm(v_raw, g_ref[...], b_ref[...])
        vnb = vn.astype(BF16)
        dvm_all = dy * zu
        for gi in range(SGU_GROUPS):
            wg = jnp.where(mask, w_ref[gi], 0.0).astype(BF16)
            wgt = jnp.where(mask_t, wt_ref[gi], 0.0).astype(BF16)
            cs = slice(gi * SGU_LEN, (gi + 1) * SGU_LEN)
            dw_g = jnp.zeros((SGU_LEN, SGU_LEN), F32)
            db_g = jnp.zeros((SGU_LEN, 1), F32)
            for c in range(tr // SGU_LEN):
                rs_ = slice(c * SGU_LEN, (c + 1) * SGU_LEN)
                vm = _dot(wg, vnb[rs_, cs]) + bc_ref[gi]
                dz_ref[rs_, cs] = (dy[rs_, cs] * vm * _gelu_grad(u_raw[rs_, cs])).astype(BF16)
                dvm = dvm_all[rs_, cs]
                dvmb = dvm.astype(BF16)
                dw_g = dw_g + _dot_nt(dvmb, vnb[rs_, cs])
                db_g = db_g + jnp.sum(dvm, axis=1, keepdims=True)
                dvn_s[rs_, cs] = _dot(wgt, dvmb)
            dw_ref[gi] += jnp.where(mask, dw_g, 0.0)
            dbc_ref[gi] += db_g
        dvn = dvn_s[...]
        dg_ref[...] += jnp.sum(dvn * xh, axis=0, keepdims=True)
        db_ref[...] += jnp.sum(dvn, axis=0, keepdims=True)
        dxh = dvn * g_ref[...]
        dzv = rs * (dxh - jnp.mean(dxh, axis=-1, keepdims=True) - xh * jnp.mean(dxh * xh, axis=-1, keepdims=True))
        dz_ref[:, BRANCH_W:] = (dzv * _gelu_grad(v_raw)).astype(BF16)

    vec = pl.BlockSpec((1, BRANCH_W), lambda i: (0, 0))
    wspec = pl.BlockSpec((SGU_GROUPS, SGU_LEN, SGU_LEN), lambda i: (0, 0, 0))
    bspec = pl.BlockSpec((SGU_GROUPS, SGU_LEN, 1), lambda i: (0, 0, 0))
    return pl.pallas_call(
        body, name=name, grid=(s // tr,),
        in_specs=[pl.BlockSpec((tr, BRANCH_W), lambda i: (i, 3)), pl.BlockSpec((tr, BRANCH_W), lambda i: (i, 4)),
                  pl.BlockSpec((tr, BRANCH_W), lambda i: (i, 0)), vec, vec, wspec, wspec, bspec],
        out_specs=[pl.BlockSpec((tr, 2 * BRANCH_W), lambda i: (i, 0)), vec, vec, wspec, bspec],
        out_shape=[jax.ShapeDtypeStruct((s, 2 * BRANCH_W), BF16),
                   jax.ShapeDtypeStruct((1, BRANCH_W), F32), jax.ShapeDtypeStruct((1, BRANCH_W), F32),
                   jax.ShapeDtypeStruct((SGU_GROUPS, SGU_LEN, SGU_LEN), F32),
                   jax.ShapeDtypeStruct((SGU_GROUPS, SGU_LEN, 1), F32)],
        scratch_shapes=[pltpu.VMEM((tr, BRANCH_W), F32)],
        compiler_params=_cparams("arbitrary"),
    )(p, p, dyb, ln_g, ln_b, w, w_t, b_col)


def _shift_down(x, prev8, k):
    rolled = pltpu.roll(x, k, 0)
    r8 = lax.broadcasted_iota(jnp.int32, prev8.shape, 0)
    head = jnp.where(r8 < k, pltpu.roll(prev8, k, 0), rolled[:HALO])
    return jnp.concatenate([head, rolled[HALO:]], axis=0)


def _shift_up(x, next8, k):
    n = x.shape[0]
    rolled = pltpu.roll(x, n - k, 0)
    r8 = lax.broadcasted_iota(jnp.int32, next8.shape, 0)
    tail = jnp.where(r8 >= HALO - k, pltpu.roll(next8, HALO - k, 0), rolled[n - HALO:])
    return jnp.concatenate([rolled[:n - HALO], tail], axis=0)


def _conv_specs(s, tr):
    nb = tr // HALO
    last = s // HALO - 1
    tile = lambda cb: pl.BlockSpec((tr, 128), lambda j, i: (i, cb * 4 + j))
    above = lambda cb: pl.BlockSpec((HALO, 128), lambda j, i: (jnp.maximum(i * nb - 1, 0), cb * 4 + j))
    below = lambda cb: pl.BlockSpec((HALO, 128), lambda j, i: (jnp.minimum((i + 1) * nb, last), cb * 4 + j))
    return tile, above, below


def _conv_fwd(p, cw, *, name):
    s = p.shape[0]
    tr = _pick(s, 512, HALO)
    tile, above, _ = _conv_specs(s, tr)

    def body(cb_ref, cc_ref, cx_ref, ccp_ref, cxp_ref, w_ref, o_ref):
        first = pl.program_id(1) == 0
        y = cc_ref[...] * cx_ref[...]
        yp = jnp.where(first, 0.0, ccp_ref[...] * cxp_ref[...])
        conv = w_ref[2:3, :] * y + w_ref[1:2, :] * _shift_down(y, yp, 1) + w_ref[0:1, :] * _shift_down(y, yp, 2)
        o_ref[...] = cb_ref[...] * conv

    return pl.pallas_call(
        body, name=name, grid=(4, s // tr),
        in_specs=[tile(5), tile(6), tile(7), above(6), above(7), pl.BlockSpec((3, 128), lambda j, i: (0, j))],
        out_specs=pl.BlockSpec((tr, 128), lambda j, i: (i, j)),
        out_shape=jax.ShapeDtypeStruct((s, BRANCH_W), F32),
        compiler_params=_cparams("parallel", "parallel"),
    )(p, p, p, p, p, cw)


def _conv_bwd(p, dyc, cw, *, name):
    s = p.shape[0]
    tr = _pick(s, 512, HALO)
    nt = s // tr
    nb = tr // HALO
    last = s // HALO - 1
    tile, above, below = _conv_specs(s, tr)

    def body(cb_ref, cc_ref, cx_ref, ccp_ref, cxp_ref, cbn_ref, dy_ref, dyn_ref, w_ref,
             dcb_ref, dcc_ref, dcx_ref, dw_ref):
        i = pl.program_id(1)

        @pl.when(i == 0)
        def _():
            dw_ref[...] = jnp.zeros_like(dw_ref)

        cb = cb_ref[...]
        cc = cc_ref[...]
        cx = cx_ref[...]
        y = cc * cx
        yp = jnp.where(i == 0, 0.0, ccp_ref[...] * cxp_ref[...])
        y1 = _shift_down(y, yp, 1)
        y2 = _shift_down(y, yp, 2)
        w0, w1, w2 = w_ref[0:1, :], w_ref[1:2, :], w_ref[2:3, :]
        conv = w2 * y + w1 * y1 + w0 * y2
        dyc_v = dy_ref[...]
        dconv = dyc_v * cb
        dn = jnp.where(i == nt - 1, 0.0, dyn_ref[...] * cbn_ref[...])
        dyv = w2 * dconv + w1 * _shift_up(dconv, dn, 1) + w0 * _shift_up(dconv, dn, 2)
        dcb_ref[...] = (dyc_v * conv).astype(BF16)
        dcc_ref[...] = (dyv * cx).astype(BF16)
        dcx_ref[...] = (dyv * cc).astype(BF16)
        dw_ref[0:1, :] += jnp.sum(dconv * y2, axis=0, keepdims=True)
        dw_ref[1:2, :] += jnp.sum(dconv * y1, axis=0, keepdims=True)
        dw_ref[2:3, :] += jnp.sum(dconv * y, axis=0, keepdims=True)

    dy_tile = pl.BlockSpec((tr, 128), lambda j, i: (i, j))
    dy_below = pl.BlockSpec((HALO, 128), lambda j, i: (jnp.minimum((i + 1) * nb, last), j))
    out_tile = lambda cb: pl.BlockSpec((tr, 128), lambda j, i: (i, cb * 4 + j))
    w_spec = pl.BlockSpec((3, 128), lambda j, i: (0, j))
    dcb, dcc, dcx, dw = pl.pallas_call(
        body, name=name, grid=(4, nt),
        in_specs=[tile(5), tile(6), tile(7), above(6), above(7), below(5), dy_tile, dy_below, w_spec],
        out_specs=[dy_tile, dy_tile, dy_tile, w_spec],
        out_shape=[jax.ShapeDtypeStruct((s, BRANCH_W), BF16)] * 3 + [jax.ShapeDtypeStruct((3, BRANCH_W), F32)],
        compiler_params=_cparams("parallel", "arbitrary"),
    )(p, p, p, p, p, p, dyc, dyc, cw)
    return dcb, dcc, dcx, dw


def _merge_fwd(ya, yb, yc, wb, p, *, name):
    s = p.shape[0]
    tr = _pick(s, 256, 16)

    def body(ya_ref, yb_ref, yc_ref, wb_ref, g0_ref, g1_ref, g2_ref, o_ref):
        acc = jnp.zeros((tr, D_MODEL), F32)
        for n, (y_ref, g_ref) in enumerate(((ya_ref, g0_ref), (yb_ref, g1_ref), (yc_ref, g2_ref))):
            acc = acc + _sigmoid(g_ref[...]) * _dot(y_ref[...].astype(BF16), wb_ref[n])
        o_ref[...] = acc.astype(BF16)

    yspec = pl.BlockSpec((tr, BRANCH_W), lambda i: (i, 0))
    gate = lambda n: pl.BlockSpec((tr, D_MODEL), lambda i: (i, 4 + n))
    return pl.pallas_call(
        body, name=name, grid=(s // tr,),
        in_specs=[yspec, yspec, yspec, pl.BlockSpec((3, BRANCH_W, D_MODEL), lambda i: (0, 0, 0)),
                  gate(0), gate(1), gate(2)],
        out_specs=pl.BlockSpec((tr, D_MODEL), lambda i: (i, 0)),
        out_shape=jax.ShapeDtypeStruct((s, D_MODEL), BF16),
        compiler_params=_cparams("parallel"),
    )(ya, yb, yc, wb, p, p, p)


def _merge_bwd(dm, ya, yb, yc, wb, p, *, name):
    s = p.shape[0]
    tr = _pick(s, 256, 16)

    def body(dm_ref, ya_ref, yb_ref, yc_ref, wb_ref, g0_ref, g1_ref, g2_ref,
             dya_ref, dyb_ref, dyc_ref, dg_ref, dbrd_ref):
        dmv = dm_ref[...]
        ys = (ya_ref, yb_ref, yc_ref)
        gs = (g0_ref, g1_ref, g2_ref)
        dys = (dya_ref, dyb_ref, dyc_ref)
        for n in range(3):
            brd = _dot(ys[n][...].astype(BF16), wb_ref[n])
            sg = _sigmoid(gs[n][...])
            dbrd = (sg * dmv).astype(BF16)
            dbrd_ref[n] = dbrd
            dg_ref[:, n * D_MODEL:(n + 1) * D_MODEL] = (dmv * brd * sg * (1.0 - sg)).astype(BF16)
            dys[n][...] = _dot_nt(dbrd, wb_ref[n])

    yspec = pl.BlockSpec((tr, BRANCH_W), lambda i: (i, 0))
    gate = lambda n: pl.BlockSpec((tr, D_MODEL), lambda i: (i, 4 + n))
    row = pl.BlockSpec((tr, D_MODEL), lambda i: (i, 0))
    return pl.pallas_call(
        body, name=name, grid=(s // tr,),
        in_specs=[row, yspec, yspec, yspec, pl.BlockSpec((3, BRANCH_W, D_MODEL), lambda i: (0, 0, 0)),
                  gate(0), gate(1), gate(2)],
        out_specs=[yspec, yspec, yspec, pl.BlockSpec((tr, 3 * D_MODEL), lambda i: (i, 0)),
                   pl.BlockSpec((3, tr, D_MODEL), lambda i: (0, i, 0))],
        out_shape=[jax.ShapeDtypeStruct((s, BRANCH_W), F32)] * 3
                  + [jax.ShapeDtypeStruct((s, 3 * D_MODEL), BF16), jax.ShapeDtypeStruct((3, s, D_MODEL), BF16)],
        compiler_params=_cparams("parallel"),
    )(dm, ya, yb, yc, wb, p, p, p)


def _xa_probs(q, k):
    sc = _dot_nt(q, k) * XA_SCALE
    e = jnp.exp(sc - jnp.max(sc, axis=-1, keepdims=True))
    return e / jnp.sum(e, axis=-1, keepdims=True)


def _xa_fwd(q, k, v, *, name):
    s = q.shape[0]
    mt = k.shape[0]
    tr = _pick(s, 512, 16)

    def body(q_ref, k_ref, v_ref, o_ref):
        pr = _xa_probs(q_ref[...], k_ref[...])
        o_ref[...] = _dot(pr.astype(BF16), v_ref[...]).astype(BF16)

    qs = pl.BlockSpec((tr, XA_HEAD), lambda h, i: (i, h))
    ks = pl.BlockSpec((mt, XA_HEAD), lambda h, i: (0, h))
    return pl.pallas_call(
        body, name=name, grid=(D_MODEL // XA_HEAD, s // tr),
        in_specs=[qs, ks, ks], out_specs=qs,
        out_shape=jax.ShapeDtypeStruct((s, D_MODEL), BF16),
        compiler_params=_cparams("parallel", "parallel"),
    )(q, k, v)


def _xa_bwd(q, k, v, do, *, name):
    s = q.shape[0]
    mt = k.shape[0]
    tr = _pick(s, 512, 16)

    def body(q_ref, k_ref, v_ref, do_ref, dq_ref, dk_ref, dv_ref):
        @pl.when(pl.program_id(1) == 0)
        def _():
            dk_ref[...] = jnp.zeros_like(dk_ref)
            dv_ref[...] = jnp.zeros_like(dv_ref)

        qv = q_ref[...]
        kv = k_ref[...]
        dov = do_ref[...]
        pr = _xa_probs(qv, kv)
        dpr = _dot_nt(dov, v_ref[...])
        ds = (pr * (dpr - jnp.sum(dpr * pr, axis=-1, keepdims=True)) * XA_SCALE).astype(BF16)
        dq_ref[...] = _dot(ds, kv).astype(BF16)
        dk_ref[...] += _dot_tn(ds, qv)
        dv_ref[...] += _dot_tn(pr.astype(BF16), dov)

    qs = pl.BlockSpec((tr, XA_HEAD), lambda h, i: (i, h))
    ks = pl.BlockSpec((mt, XA_HEAD), lambda h, i: (0, h))
    return pl.pallas_call(
        body, name=name, grid=(D_MODEL // XA_HEAD, s // tr),
        in_specs=[qs, ks, ks, qs], out_specs=[qs, ks, ks],
        out_shape=[jax.ShapeDtypeStruct((s, D_MODEL), BF16), jax.ShapeDtypeStruct((mt, D_MODEL), F32),
                   jax.ShapeDtypeStruct((mt, D_MODEL), F32)],
        compiler_params=_cparams("parallel", "arbitrary"),
    )(q, k, v, do)


def _swiglu_fwd(ab, *, name):
    s, f = ab.shape[0], ab.shape[1] // 2
    tr = _pick(s, 256, 16)

    def body(a_ref, b_ref, o_ref):
        av = a_ref[...]
        o_ref[...] = (av * _sigmoid(av) * b_ref[...]).astype(BF16)

    half = lambda c: pl.BlockSpec((tr, f), lambda i: (i, c))
    return pl.pallas_call(
        body, name=name, grid=(s // tr,), in_specs=[half(0), half(1)], out_specs=half(0),
        out_shape=jax.ShapeDtypeStruct((s, f), BF16), compiler_params=_cparams("parallel"),
    )(ab, ab)


def _swiglu_bwd(ab, dh, *, name):
    s, f = ab.shape[0], ab.shape[1] // 2
    tr = _pick(s, 256, 16)

    def body(a_ref, b_ref, dh_ref, o_ref):
        av = a_ref[...]
        dhv = dh_ref[...]
        sg = _sigmoid(av)
        silu = av * sg
        o_ref[:, :f] = (dhv * b_ref[...] * (sg + silu * (1.0 - sg))).astype(BF16)
        o_ref[:, f:] = (dhv * silu).astype(BF16)

    half = lambda c: pl.BlockSpec((tr, f), lambda i: (i, c))
    return pl.pallas_call(
        body, name=name, grid=(s // tr,), in_specs=[half(0), half(1), half(0)],
        out_specs=pl.BlockSpec((tr, 2 * f), lambda i: (i, 0)),
        out_shape=jax.ShapeDtypeStruct((s, 2 * f), BF16), compiler_params=_cparams("parallel"),
    )(ab, ab, dh)


def _adamw(w, g, m, v, *, name):
    r, c = w.shape
    tr = _pick(r, 512, 8)

    def body(w_ref, g_ref, m_ref, v_ref, d_ref, mo_ref, vo_ref):
        gv = g_ref[...]
        mn = ADAM_B1 * m_ref[...] + (1.0 - ADAM_B1) * gv
        vn = ADAM_B2 * v_ref[...] + (1.0 - ADAM_B2) * (gv * gv)
        m_hat = mn / (1.0 - ADAM_B1 ** ADAM_STEP)
        v_hat = vn / (1.0 - ADAM_B2 ** ADAM_STEP)
        d_ref[...] = -ADAM_LR * (m_hat / (jnp.sqrt(v_hat) + ADAM_EPS) + ADAM_WD * w_ref[...])
        mo_ref[...] = mn
        vo_ref[...] = vn

    spec = pl.BlockSpec((tr, c), lambda i: (i, 0))
    shp = jax.ShapeDtypeStruct((r, c), F32)
    return pl.pallas_call(
        body, name=name, grid=(r // tr,), in_specs=[spec] * 4, out_specs=[spec] * 3,
        out_shape=[shp] * 3, compiler_params=_cparams("parallel"),
    )(w, g, m, v)


def _position():
    return lax.axis_index("x"), lax.axis_index("y"), lax.axis_index("c")


def _all_gather(x, *, name):
    t, c_ = x.shape

    def body(x_ref, out_ref, send_sems, recv_sems, local_sem):
        start, forward, finish = _gather_phases(x_ref, out_ref, send_sems, recv_sems, local_sem)
        start()
        forward()
        finish()

    return pl.pallas_call(
        body, name=name,
        out_shape=jax.ShapeDtypeStruct((N_DEV, t, c_), x.dtype),
        in_specs=[pl.BlockSpec(memory_space=pl.ANY)],
        out_specs=pl.BlockSpec(memory_space=pl.ANY),
        scratch_shapes=_GATHER_SEMS,
    )(x)


_GATHER_SEMS = [pltpu.SemaphoreType.DMA((7,)), pltpu.SemaphoreType.DMA((7,)), pltpu.SemaphoreType.DMA]


def _gather_phases(x_ref, out_ref, send_sems, recv_sems, local_sem):
    x_, y_, c = _position()
    me, sibling = (x_, y_, c), (x_, y_, 1 - c)
    chips = [(1 - x_, y_), (x_, 1 - y_), (1 - x_, 1 - y_)]

    def block(px, py, pc):
        return out_ref.at[4 * px + 2 * py + pc]

    def copy(k, blk, to, src=None):
        return pltpu.make_async_remote_copy(
            src_ref=block(*blk) if src is None else src, dst_ref=block(*blk),
            send_sem=send_sems.at[k], recv_sem=recv_sems.at[k], device_id=to, device_id_type=MESH)

    mine = pltpu.make_async_copy(x_ref, block(*me), local_sem)
    first = [copy(0, me, sibling, src=x_ref)]
    first += [copy(1 + j, me, (*chip, c), src=x_ref) for j, chip in enumerate(chips)]
    passed = [copy(4 + j, (*chip, c), sibling) for j, chip in enumerate(chips)]

    def start():
        mine.start()
        for cp in first:
            cp.start()

    def forward():
        for j, chip in enumerate(chips):
            copy(1 + j, (*chip, c), me).wait_recv()
            passed[j].start()

    def finish():
        copy(0, sibling, me).wait_recv()
        for j, chip in enumerate(chips):
            copy(4 + j, (*chip, 1 - c), me).wait_recv()
        for cp in first + passed:
            cp.wait_send()
        mine.wait()

    return start, forward, finish


_SCATTER_SEMS = [pltpu.SemaphoreType.DMA((7,)), pltpu.SemaphoreType.DMA((7,)), pltpu.SemaphoreType.DMA]


def _scatter_phases(g_ref, r_ref, send_sems, recv_sems, local_sem):
    x_, y_, c = _position()
    me = 4 * x_ + 2 * y_ + c
    local = pltpu.make_async_copy(g_ref.at[me], r_ref.at[me], local_sem)
    copies = []
    for k in range(1, N_DEV):
        to = (x_ ^ (k >> 2), y_ ^ ((k >> 1) & 1), c ^ (k & 1))
        copies.append(pltpu.make_async_remote_copy(
            src_ref=g_ref.at[me ^ k], dst_ref=r_ref.at[me], send_sem=send_sems.at[k - 1],
            recv_sem=recv_sems.at[k - 1], device_id=to, device_id_type=MESH))

    def start():
        local.start()
        for cp in copies:
            cp.start()

    def finish():
        for k in range(1, N_DEV):
            pltpu.make_async_remote_copy(
                src_ref=g_ref.at[me], dst_ref=r_ref.at[me ^ k], send_sem=send_sems.at[k - 1],
                recv_sem=recv_sems.at[k - 1], device_id=(x_, y_, c), device_id_type=MESH).wait_recv()
        for cp in copies:
            cp.wait_send()
        local.wait()

    return start, finish


def _sum_devices(r8, *, name):
    _, t, c_ = r8.shape
    tr = _pick(t, 256, 16)

    def body(r_ref, o_ref):
        acc = r_ref[0].astype(F32)
        for d in range(1, N_DEV):
            acc = acc + r_ref[d].astype(F32)
        o_ref[...] = acc

    return pl.pallas_call(
        body, name=name, grid=(t // tr,),
        in_specs=[pl.BlockSpec((N_DEV, tr, c_), lambda i: (0, i, 0))],
        out_specs=pl.BlockSpec((tr, c_), lambda i: (i, 0)),
        out_shape=jax.ShapeDtypeStruct((t, c_), F32),
        compiler_params=_cparams("parallel"),
    )(r8)


def _all_reduce_small(x, *, name):
    r, c_ = x.shape

    def body(x_ref, o_ref, buf, send_sems, recv_sems):
        x_, y_, c = _position()
        me = 4 * x_ + 2 * y_ + c
        buf[me] = x_ref[...]
        copies = []
        for k in range(1, N_DEV):
            to = (x_ ^ (k >> 2), y_ ^ ((k >> 1) & 1), c ^ (k & 1))
            copies.append(pltpu.make_async_remote_copy(
                src_ref=x_ref, dst_ref=buf.at[me], send_sem=send_sems.at[k - 1], recv_sem=recv_sems.at[k - 1],
                device_id=to, device_id_type=MESH))
        for cp in copies:
            cp.start()
        for k in range(1, N_DEV):
            src = me ^ k
            pltpu.make_async_remote_copy(
                src_ref=x_ref, dst_ref=buf.at[src], send_sem=send_sems.at[k - 1], recv_sem=recv_sems.at[k - 1],
                device_id=(x_, y_, c), device_id_type=MESH).wait_recv()
        for cp in copies:
            cp.wait_send()
        acc = buf[0]
        for d in range(1, N_DEV):
            acc = acc + buf[d]
        o_ref[...] = acc

    return pl.pallas_call(
        body, name=name,
        out_shape=jax.ShapeDtypeStruct((r, c_), F32),
        in_specs=[pl.BlockSpec(memory_space=pltpu.VMEM)],
        out_specs=pl.BlockSpec(memory_space=pltpu.VMEM),
        scratch_shapes=[pltpu.VMEM((N_DEV, r, c_), F32), pltpu.SemaphoreType.DMA((7,)), pltpu.SemaphoreType.DMA((7,))],
    )(x)


def _rs_pair_exchange(g8, *, name):
    _, t, c_ = g8.shape

    def body(g_ref, r_ref, send_sems, recv_sems):
        x_, y_, c = _position()
        copies = [pltpu.make_async_remote_copy(
            src_ref=g_ref.at[2 * ch + (1 - c)], dst_ref=r_ref.at[ch],
            send_sem=send_sems.at[ch], recv_sem=recv_sems.at[ch],
            device_id=(x_, y_, 1 - c), device_id_type=MESH) for ch in range(4)]
        for cp in copies:
            cp.start()
        for cp in copies:
            cp.wait()

    return pl.pallas_call(
        body, name=name,
        out_shape=jax.ShapeDtypeStruct((4, t, c_), g8.dtype),
        in_specs=[pl.BlockSpec(memory_space=pl.ANY)],
        out_specs=pl.BlockSpec(memory_space=pl.ANY),
        scratch_shapes=[pltpu.SemaphoreType.DMA((4,)), pltpu.SemaphoreType.DMA((4,))],
    )(g8)


def _pair_add(core, g8, recv, *, name):
    _, t, c_ = g8.shape
    tr = _pick(t, 512, 16)

    def body(core_ref, g_ref, r_ref, o_ref):
        o_ref[...] = (g_ref[...].astype(F32) + r_ref[...].astype(F32)).astype(o_ref.dtype)

    grid_spec = pltpu.PrefetchScalarGridSpec(
        num_scalar_prefetch=1, grid=(4, t // tr),
        in_specs=[pl.BlockSpec((None, tr, c_), lambda ch, i, core_ref: (2 * ch + core_ref[0], i, 0)),
                  pl.BlockSpec((None, tr, c_), lambda ch, i, core_ref: (ch, i, 0))],
        out_specs=pl.BlockSpec((None, tr, c_), lambda ch, i, core_ref: (ch, i, 0)))
    return pl.pallas_call(
        body, name=name, grid_spec=grid_spec,
        out_shape=jax.ShapeDtypeStruct((4, t, c_), g8.dtype),
        compiler_params=_cparams("parallel", "parallel"),
    )(core, g8, recv)


def _rs_chip_exchange(part, *, name):
    _, t, c_ = part.shape

    def body(p_ref, r_ref, send_sems, recv_sems, local_sem):
        x_, y_, c = _position()
        mine = 2 * x_ + y_
        local = pltpu.make_async_copy(p_ref.at[mine], r_ref.at[mine], local_sem)
        local.start()
        chips = [(1 - x_, y_), (x_, 1 - y_), (1 - x_, 1 - y_)]
        copies = [pltpu.make_async_remote_copy(
            src_ref=p_ref.at[2 * px + py], dst_ref=r_ref.at[mine],
            send_sem=send_sems.at[k], recv_sem=recv_sems.at[k],
            device_id=(px, py, c), device_id_type=MESH) for k, (px, py) in enumerate(chips)]
        for cp in copies:
            cp.start()
        for k, (px, py) in enumerate(chips):
            pltpu.make_async_remote_copy(
                src_ref=p_ref.at[mine], dst_ref=r_ref.at[2 * px + py],
                send_sem=send_sems.at[k], recv_sem=recv_sems.at[k],
                device_id=(x_, y_, c), device_id_type=MESH).wait_recv()
        for cp in copies:
            cp.wait_send()
        local.wait()

    return pl.pallas_call(
        body, name=name,
        out_shape=jax.ShapeDtypeStruct((4, t, c_), part.dtype),
        in_specs=[pl.BlockSpec(memory_space=pl.ANY)],
        out_specs=pl.BlockSpec(memory_space=pl.ANY),
        scratch_shapes=[pltpu.SemaphoreType.DMA((3,)), pltpu.SemaphoreType.DMA((3,)), pltpu.SemaphoreType.DMA],
    )(part)


def _sum_chips(r4, *, name):
    _, t, c_ = r4.shape
    tr = _pick(t, 512, 16)

    def body(r_ref, o_ref):
        acc = r_ref[0].astype(F32)
        for ch in range(1, 4):
            acc = acc + r_ref[ch].astype(F32)
        o_ref[...] = acc

    return pl.pallas_call(
        body, name=name, grid=(t // tr,),
        in_specs=[pl.BlockSpec((4, tr, c_), lambda i: (0, i, 0))],
        out_specs=pl.BlockSpec((tr, c_), lambda i: (i, 0)),
        out_shape=jax.ShapeDtypeStruct((t, c_), F32),
        compiler_params=_cparams("parallel"),
    )(r4)


BIG = (
    ("w_in", (D_MODEL, IN_COLS // N_DEV), 1),
    ("w_branch", (3, BRANCH_W, D_MODEL // N_DEV), 2),
    ("w_out", (D_MODEL // N_DEV, D_MODEL), 0),
    ("w_q_xa", (D_MODEL // N_DEV, D_MODEL), 0),
    ("w_k_xa", (D_MODEL // N_DEV, D_MODEL), 0),
    ("w_v_xa", (D_MODEL // N_DEV, D_MODEL), 0),
    ("w_o_xa", (D_MODEL // N_DEV, D_MODEL), 0),
    ("w_gate_ffn", (D_MODEL, FFN // N_DEV), 1),
    ("w_up_ffn", (D_MODEL, FFN // N_DEV), 1),
    ("w_down_ffn", (FFN // N_DEV, D_MODEL), 0),
)
_BIG_LAYOUT = {n: (shp, ax) for n, shp, ax in BIG}
PACK_COLS = 1024


def _size(shape):
    n = 1
    for d in shape:
        n *= d
    return n


def _pack_shards(items, shards):
    return jnp.concatenate([shards[it].reshape(-1, PACK_COLS) for it in items], axis=0)


def _unpack_gathered(items, g):
    out = {}
    r0 = 0
    for it in items:
        shp, ax = _BIG_LAYOUT[it[0]]
        rows = _size(shp) // PACK_COLS
        blk = g[:, r0:r0 + rows].reshape((N_DEV,) + shp)
        r0 += rows
        blk = jnp.moveaxis(blk, 0, ax)
        full = list(shp)
        full[ax] = shp[ax] * N_DEV
        out[it] = blk.reshape(full)
    return out


def _pack_full(items, full):
    parts = []
    for it in items:
        shp, ax = _BIG_LAYOUT[it[0]]
        t = full[it].reshape(shp[:ax] + (N_DEV, shp[ax]) + shp[ax + 1:])
        t = jnp.moveaxis(t, ax, 0)
        parts.append(t.reshape(N_DEV, -1, PACK_COLS))
    return jnp.concatenate(parts, axis=1)


def _unpack_shard(items, flat):
    out = {}
    r0 = 0
    for it in items:
        shp, _ = _BIG_LAYOUT[it[0]]
        rows = _size(shp) // PACK_COLS
        out[it] = flat[r0:r0 + rows].reshape(shp)
        r0 += rows
    return out


SMALL = (
    ("norm_mix_g", (DEPTH, D_MODEL)),
    ("sgu_ln_g", (DEPTH, BRANCH_W)),
    ("sgu_ln_b", (DEPTH, BRANCH_W)),
    ("w_spatial", (DEPTH, SGU_GROUPS, SGU_LEN, SGU_LEN)),
    ("b_spatial", (DEPTH, SGU_GROUPS, SGU_LEN)),
    ("conv_w", (DEPTH, 3, BRANCH_W)),
    ("norm_xa_g", (DEPTH, D_MODEL)),
    ("mem_norm_g", (DEPTH, D_MODEL)),
    ("norm_ffn_g", (DEPTH, D_MODEL)),
    ("final_g", (D_MODEL,)),
)


def _pack_small(grads):
    flat = jnp.concatenate([grads[n].reshape(-1) for n, _ in SMALL])
    rows = -(-flat.shape[0] // PACK_COLS)
    rows = -(-rows // 8) * 8
    flat = jnp.pad(flat, (0, rows * PACK_COLS - flat.shape[0]))
    return flat.reshape(rows, PACK_COLS)


def _unpack_small(buf):
    flat = buf.reshape(-1)
    out = {}
    o = 0
    for n, shp in SMALL:
        out[n] = flat[o:o + _size(shp)].reshape(shp)
        o += _size(shp)
    return out


def _layer_fwd(l, x, mem, wt, sm, gather=None):
    t = f"l{l}_"
    sv = {"x0": x}
    h = _rms_fwd(x, sm["norm_mix_g"][l][None], name=t + "rms_mix")
    p = _mm(h, wt["w_in", l], name=t + "in_proj", tm=2048)
    if gather is None:
        ya = _sb_fwd(p, name=t + "sb_fwd")
    else:
        ya, gathered = _sb_fwd(p, name=t + "sb_fwd", gather=gather[1])
        wt.update(_unpack_gathered(gather[0], gathered))
    w_sp = sm["w_spatial"][l]
    b_col = sm["b_spatial"][l][:, :, None]
    ln_g, ln_b = sm["sgu_ln_g"][l][None], sm["sgu_ln_b"][l][None]
    yb = _sgu_fwd(p, ln_g, ln_b, w_sp, b_col, name=t + "sgu_fwd")
    yc = _conv_fwd(p, sm["conv_w"][l], name=t + "conv_fwd")
    merged = _merge_fwd(ya, yb, yc, wt["w_branch", l], p, name=t + "merge_fwd")
    x1 = _mm(merged, wt["w_out", l], add=x, name=t + "out_proj")
    sv.update(h=h, p=p, ya=ya, yb=yb, yc=yc, merged=merged, x1=x1)

    h2 = _rms_fwd(x1, sm["norm_xa_g"][l][None], name=t + "rms_xa")
    mn = _rms_fwd(mem, sm["mem_norm_g"][l][None], name=t + "rms_mem")
    q = _mm(h2, wt["w_q_xa", l], out_dtype=BF16, name=t + "xa_q", tm=2048)
    k = _mm(mn, wt["w_k_xa", l], out_dtype=BF16, name=t + "xa_k")
    v = _mm(mn, wt["w_v_xa", l], out_dtype=BF16, name=t + "xa_v")
    o = _xa_fwd(q, k, v, name=t + "xa_fwd")
    x2 = _mm(o, wt["w_o_xa", l], add=x1, name=t + "xa_o")
    sv.update(h2=h2, mn=mn, q=q, k=k, v=v, o=o, x2=x2)

    h3 = _rms_fwd(x2, sm["norm_ffn_g"][l][None], name=t + "rms_ffn")
    w_gu = jnp.concatenate([wt["w_gate_ffn", l], wt["w_up_ffn", l]], axis=1)
    ab = _mm(h3, w_gu, name=t + "ffn_gate_up", tm=2048, tn=1408)
    hd = _swiglu_fwd(ab, name=t + "swiglu_fwd")
    x3 = _mm(hd, wt["w_down_ffn", l], add=x2, name=t + "ffn_down", tk=FFN)
    sv.update(h3=h3, ab=ab, hd=hd, w_gu=w_gu)
    return x3, sv


def _layer_bwd(l, dx3, mem, wt, sm, sv, scatter=None):
    t = f"l{l}_b_"
    gb, gs = {}, {}
    dhd = _mm(dx3, wt["w_down_ffn", l], tb=True, name=t + "ffn_down_dx", tn=1408)
    gb["w_down_ffn"] = _mm(sv["hd"], dx3, ta=True, out_dtype=BF16, name=t + "ffn_down_dw", tm=1408)
    dab = _swiglu_bwd(sv["ab"], dhd, name=t + "swiglu_bwd")
    dw_gu = _mm(sv["h3"], dab, ta=True, out_dtype=BF16, name=t + "ffn_gate_up_dw", tn=1408)
    gb["w_gate_ffn"], gb["w_up_ffn"] = dw_gu[:, :FFN], dw_gu[:, FFN:]
    dx2, dg = _mm(dab, sv["w_gu"], tb=True, rms=(sv["x2"], sm["norm_ffn_g"][l][None], dx3),
                  name=t + "ffn_gate_up_dx", tm=512, tk=1408)
    gs["norm_ffn_g"] = dg[0]
    do = _mm(dx2, wt["w_o_xa", l], tb=True, out_dtype=BF16, name=t + "xa_o_dx")
    gb["w_o_xa"] = _mm(sv["o"], dx2, ta=True, out_dtype=BF16, name=t + "xa_o_dw")
    dq, dk, dv = _xa_bwd(sv["q"], sv["k"], sv["v"], do, name=t + "xa_bwd")
    dx1, dg = _mm(dq, wt["w_q_xa", l], tb=True, rms=(sv["x1"], sm["norm_xa_g"][l][None], dx2),
                  name=t + "xa_q_dx", tm=512)
    gs["norm_xa_g"] = dg[0]
    gb["w_q_xa"] = _mm(sv["h2"], dq, ta=True, out_dtype=BF16, name=t + "xa_q_dw")
    gb["w_k_xa"] = _mm(sv["mn"], dk, ta=True, out_dtype=BF16, name=t + "xa_k_dw")
    gb["w_v_xa"] = _mm(sv["mn"], dv, ta=True, out_dtype=BF16, name=t + "xa_v_dw")
    dmn = _mm(dk, wt["w_k_xa", l], tb=True, name=t + "xa_k_dx")
    dmn = _mm(dv, wt["w_v_xa", l], tb=True, add=dmn, name=t + "xa_v_dx")
    _, dg = _rms_bwd(mem, sm["mem_norm_g"][l][None], dmn, jnp.zeros_like(mem), name=t + "rms_mem")
    gs["mem_norm_g"] = dg[0]
    dm = _mm(dx1, wt["w_out", l], tb=True, name=t + "out_proj_dx")
    gb["w_out"] = _mm(sv["merged"], dx1, ta=True, out_dtype=BF16, name=t + "out_proj_dw")
    p = sv["p"]
    dya, dyb, dyc, dgates, dbrd = _merge_bwd(dm, sv["ya"], sv["yb"], sv["yc"], wt["w_branch", l], p,
                                             name=t + "merge_bwd")
    gb["w_branch"] = jnp.stack([
        _mm(sv[y], dbrd[n], ta=True, out_dtype=BF16, name=t + f"branch{n}_dw")
        for n, y in enumerate(("ya", "yb", "yc"))])
    dcb, dcc, dcx, dcw = _conv_bwd(p, dyc, sm["conv_w"][l], name=t + "conv_bwd")
    gs["conv_w"] = dcw
    w_sp = sm["w_spatial"][l]
    dz, dlg, dlb, dwsp, dbsp = _sgu_bwd(p, dyb, sm["sgu_ln_g"][l][None], sm["sgu_ln_b"][l][None], w_sp,
                                        jnp.swapaxes(w_sp, 1, 2), sm["b_spatial"][l][:, :, None],
                                        name=t + "sgu_bwd")
    gs.update(sgu_ln_g=dlg[0], sgu_ln_b=dlb[0], w_spatial=dwsp, b_spatial=dbsp[:, :, 0])
    received = None
    if scatter is None:
        dq_a, dk_a, dv_a = _sb_bwd(p, dya, name=t + "sb_bwd")
    else:
        items, earlier = scatter
        ready = {**earlier, **{(n, l): g for n, g in gb.items()}}
        dq_a, dk_a, dv_a, received = _sb_bwd(p, dya, name=t + "sb_bwd", scatter=_pack_full(items, ready))
    dp = jnp.concatenate([dq_a, dk_a, dv_a, dz, dcb, dcc, dcx, dgates], axis=1)
    gb["w_in"] = _mm(sv["h"], dp, ta=True, out_dtype=BF16, name=t + "in_proj_dw")
    dx, dg = _mm(dp, wt["w_in", l], tb=True, rms=(sv["x0"], sm["norm_mix_g"][l][None], dx1),
                 name=t + "in_proj_dx", tm=512, tk=1792)
    gs["norm_mix_g"] = dg[0]
    return dx, gb, gs, received


_WEIGHTS = ("norm_mix_g", "w_in", "sgu_ln_g", "sgu_ln_b", "w_spatial", "b_spatial", "conv_w", "w_branch", "w_out",
            "norm_xa_g", "mem_norm_g", "w_q_xa", "w_k_xa", "w_v_xa", "w_o_xa", "norm_ffn_g", "w_gate_ffn",
            "w_up_ffn", "w_down_ffn", "final_g")


def kernel(x, mem, norm_mix_g, w_in, sgu_ln_g, sgu_ln_b, w_spatial, b_spatial, conv_w, w_branch, w_out, norm_xa_g, mem_norm_g, w_q_xa, w_k_xa, w_v_xa, w_o_xa, norm_ffn_g, w_gate_ffn, w_up_ffn, w_down_ffn, final_g, loss_target, m_norm_mix_g, m_w_in, m_sgu_ln_g, m_sgu_ln_b, m_w_spatial, m_b_spatial, m_conv_w, m_w_branch, m_w_out, m_norm_xa_g, m_mem_norm_g, m_w_q_xa, m_w_k_xa, m_w_v_xa, m_w_o_xa, m_norm_ffn_g, m_w_gate_ffn, m_w_up_ffn, m_w_down_ffn, m_final_g, v_norm_mix_g, v_w_in, v_sgu_ln_g, v_sgu_ln_b, v_w_spatial, v_b_spatial, v_conv_w, v_w_branch, v_w_out, v_norm_xa_g, v_mem_norm_g, v_w_q_xa, v_w_k_xa, v_w_v_xa, v_w_o_xa, v_norm_ffn_g, v_w_gate_ffn, v_w_up_ffn, v_w_down_ffn, v_final_g):
    w = dict(norm_mix_g=norm_mix_g, w_in=w_in, sgu_ln_g=sgu_ln_g, sgu_ln_b=sgu_ln_b, w_spatial=w_spatial,
             b_spatial=b_spatial, conv_w=conv_w, w_branch=w_branch, w_out=w_out, norm_xa_g=norm_xa_g,
             mem_norm_g=mem_norm_g, w_q_xa=w_q_xa, w_k_xa=w_k_xa, w_v_xa=w_v_xa, w_o_xa=w_o_xa,
             norm_ffn_g=norm_ffn_g, w_gate_ffn=w_gate_ffn, w_up_ffn=w_up_ffn, w_down_ffn=w_down_ffn, final_g=final_g)
    m = dict(norm_mix_g=m_norm_mix_g, w_in=m_w_in, sgu_ln_g=m_sgu_ln_g, sgu_ln_b=m_sgu_ln_b, w_spatial=m_w_spatial,
             b_spatial=m_b_spatial, conv_w=m_conv_w, w_branch=m_w_branch, w_out=m_w_out, norm_xa_g=m_norm_xa_g,
             mem_norm_g=m_mem_norm_g, w_q_xa=m_w_q_xa, w_k_xa=m_w_k_xa, w_v_xa=m_w_v_xa, w_o_xa=m_w_o_xa,
             norm_ffn_g=m_norm_ffn_g, w_gate_ffn=m_w_gate_ffn, w_up_ffn=m_w_up_ffn, w_down_ffn=m_w_down_ffn,
             final_g=m_final_g)
    v = dict(norm_mix_g=v_norm_mix_g, w_in=v_w_in, sgu_ln_g=v_sgu_ln_g, sgu_ln_b=v_sgu_ln_b, w_spatial=v_w_spatial,
             b_spatial=v_b_spatial, conv_w=v_conv_w, w_branch=v_w_branch, w_out=v_w_out, norm_xa_g=v_norm_xa_g,
             mem_norm_g=v_mem_norm_g, w_q_xa=v_w_q_xa, w_k_xa=v_w_k_xa, w_v_xa=v_w_v_xa, w_o_xa=v_w_o_xa,
             norm_ffn_g=v_norm_ffn_g, w_gate_ffn=v_w_gate_ffn, w_up_ffn=v_w_up_ffn, w_down_ffn=v_w_down_ffn,
             final_g=v_final_g)

    names = [n for n, _, _ in BIG]
    shards = {(n, l): w[n][l].astype(BF16) for n in names for l in range(DEPTH)}
    first_items = [("w_in", 0)]
    later_items = [(n, l) for l in range(DEPTH) for n in names if (n, l) != ("w_in", 0)]
    wt = _unpack_gathered(first_items, _all_gather(_pack_shards(first_items, shards), name="gather_w_in0"))
    cw_pad = jnp.zeros((8, 128), F32).at[:DEPTH * 3, :BRANCH_W // N_DEV].set(conv_w.reshape(DEPTH * 3, -1))
    cw_all = _all_gather(cw_pad, name="gather_conv_w")[:, :DEPTH * 3, :BRANCH_W // N_DEV]
    conv_full = jnp.moveaxis(cw_all.reshape(N_DEV, DEPTH, 3, BRANCH_W // N_DEV), 0, 2).reshape(DEPTH, 3, BRANCH_W)
    sm = {n: w[n] for n, _ in SMALL}
    sm["conv_w"] = conv_full

    xs, ms = x[0], mem[0]
    x1, saved0 = _layer_fwd(0, xs, ms, wt, sm, gather=(later_items, _pack_shards(later_items, shards)))
    x2, saved1 = _layer_fwd(1, x1, ms, wt, sm)
    dcur, loss, dfinal = _final_loss(x2, sm["final_g"][None], loss_target[0], name="final_loss")
    loss = lax.psum(loss[0, 0], AXES)
    items_a = [(n, 1) for n in names if n != "w_in"]
    items_b = [("w_in", 1)] + [(n, 0) for n in names if n != "w_in"]
    items_c = [("w_in", 0)]
    dcur, gb1, gs1, recv_a = _layer_bwd(1, dcur, ms, wt, sm, saved1, scatter=(items_a, {}))
    dx, gb0, gs0, recv_b = _layer_bwd(0, dcur, ms, wt, sm, saved0, scatter=(items_b, {("w_in", 1): gb1["w_in"]}))

    shard_grads = _unpack_shard(items_a, _sum_devices(recv_a, name="rs_sum_a"))
    shard_grads.update(_unpack_shard(items_b, _sum_devices(recv_b, name="rs_sum_b")))
    g8 = _pack_full(items_c, {("w_in", 0): gb0["w_in"]})
    core = lax.axis_index("c").astype(jnp.int32).reshape(1)
    from_sibling = _rs_pair_exchange(g8, name="rs_pair_exchange")
    part = _pair_add(core, g8, from_sibling, name="rs_pair_add")
    by_chip = _rs_chip_exchange(part, name="rs_chip_exchange")
    shard_grads.update(_unpack_shard(items_c, _sum_chips(by_chip, name="rs_sum_chips")))
    grads = {n: jnp.stack([shard_grads[n, l] for l in range(DEPTH)]) for n in names}
    small = {n: jnp.stack([gs0[n], gs1[n]]) for n, _ in SMALL if n != "final_g"}
    small["final_g"] = dfinal[0]
    small_sum = _unpack_small(_all_reduce_small(_pack_small(small), name="all_reduce_small"))
    width = BRANCH_W // N_DEV
    dev = 4 * lax.axis_index("x") + 2 * lax.axis_index("y") + lax.axis_index("c")
    for n, _ in SMALL:
        grads[n] = small_sum[n]
    grads["conv_w"] = lax.dynamic_slice_in_dim(small_sum["conv_w"], dev * width, width, axis=2)

    delta, new_m, new_v = {}, {}, {}
    for n in _WEIGHTS:
        shp = w[n].shape
        two_d = (-1, shp[-1])
        d_, m_, v_ = _adamw(w[n].reshape(two_d), grads[n].reshape(two_d), m[n].reshape(two_d), v[n].reshape(two_d),
                            name="adamw_" + n)
        delta[n], new_m[n], new_v[n] = d_.reshape(shp), m_.reshape(shp), v_.reshape(shp)

    return (loss, dx[None], *[grads[n] for n in _WEIGHTS], *[delta[n] for n in _WEIGHTS],
            *[new_m[n] for n in _WEIGHTS], *[new_v[n] for n in _WEIGHTS])
```

```python
import functools

import jax
import jax.numpy as jnp
from jax import lax
from jax.experimental import pallas as pl
from jax.experimental.pallas import tpu as pltpu

F32 = jnp.float32
BF16 = jnp.bfloat16
MESH = pl.DeviceIdType.MESH

D_MODEL = 1024
BRANCH_W = 512
IN_COLS = 7168
FFN = 2816
N_DEV = 8
DEPTH = 2
SB_BLOCK = 128
SB_SPAN = 1024
SB_Q_FWD = 512
SB_Q_BWD = 256
SB_SCALE = 0.125
XA_HEAD = 256
XA_SCALE = 0.0625
SGU_LEN = 128
SGU_GROUPS = 4
RMS_EPS = 1e-6
LN_EPS = 1e-5
HALO = 8

ADAM_LR = 0.001
ADAM_B1 = 0.9
ADAM_B2 = 0.999
ADAM_EPS = 1e-08
ADAM_WD = 0.01
ADAM_STEP = 10

VMEM_LIMIT_BYTES = 52 * 1024 * 1024

AXES = ("x", "y", "c")


def _cparams(*sem):
    return pltpu.CompilerParams(dimension_semantics=sem, vmem_limit_bytes=VMEM_LIMIT_BYTES)


def _pick(n, target, align):
    t = (min(target, n) // align) * align
    while t >= align:
        if n % t == 0:
            return t
        t -= align
    return n


def _dot(a, b):
    return jnp.dot(a, b, preferred_element_type=F32)


def _dot_nt(a, b):
    return lax.dot_general(a, b, (((1,), (1,)), ((), ())), preferred_element_type=F32)


def _dot_tn(a, b):
    return lax.dot_general(a, b, (((0,), (0,)), ((), ())), preferred_element_type=F32)


def _sigmoid(x):
    return 1.0 / (1.0 + jnp.exp(-x))


def _mm(a, b, *, name, ta=False, tb=False, out_dtype=F32, add=None, rms=None, tm=1024, tn=1024, tk=2048):
    m, k = (a.shape[1], a.shape[0]) if ta else a.shape
    n = b.shape[0] if tb else b.shape[1]
    assert k == (b.shape[1] if tb else b.shape[0])
    tm = _pick(m, tm, 128)
    tn = n if rms is not None else _pick(n, tn, 128)
    tk = _pick(k, tk, 128)
    nk = k // tk
    ca = 0 if ta else 1
    cb = 1 if tb else 0
    assert add is None or rms is None

    def body(*refs):
        refs = list(refs)
        a_ref, b_ref = refs[:2]
        extra = refs[2:2 + (1 if add is not None else 3 if rms is not None else 0)]
        outs = refs[2 + len(extra):]
        o_ref = outs[0]
        kk = pl.program_id(2)
        first_row_tile = pl.program_id(0) == 0

        def product():
            return lax.dot_general(a_ref[...].astype(BF16), b_ref[...].astype(BF16),
                                   (((ca,), (cb,)), ((), ())), preferred_element_type=F32)

        def finish(r):
            if add is not None:
                r = r + extra[0][...]
            if rms is None:
                o_ref[...] = r.astype(out_dtype)
                return
            x_ref, g_ref, dres_ref = extra
            dg_ref = outs[1]

            @pl.when(first_row_tile)
            def _():
                dg_ref[...] = jnp.zeros_like(dg_ref)

            xv = x_ref[...]
            rs = lax.rsqrt(jnp.mean(xv * xv, axis=-1, keepdims=True) + RMS_EPS)
            xh = xv * rs
            dg_ref[...] += jnp.sum(r * xh, axis=0, keepdims=True)
            dxh = r * g_ref[...]
            o_ref[...] = dres_ref[...] + rs * (dxh - xh * jnp.mean(dxh * xh, axis=-1, keepdims=True))

        if nk == 1:
            finish(product())
        else:
            acc_ref = outs[-1]

            @pl.when(kk == 0)
            def _():
                acc_ref[...] = jnp.zeros_like(acc_ref)

            acc_ref[...] += product()

            @pl.when(kk == nk - 1)
            def _():
                finish(acc_ref[...])

    a_spec = pl.BlockSpec((tk, tm), lambda i, j, kk: (kk, i)) if ta else pl.BlockSpec((tm, tk), lambda i, j, kk: (i, kk))
    b_spec = pl.BlockSpec((tn, tk), lambda i, j, kk: (j, kk)) if tb else pl.BlockSpec((tk, tn), lambda i, j, kk: (kk, j))
    tile = pl.BlockSpec((tm, tn), lambda i, j, kk: (i, j))
    in_specs = [a_spec, b_spec]
    operands = [a, b]
    out_specs = [tile]
    out_shape = [jax.ShapeDtypeStruct((m, n), out_dtype)]
    if add is not None:
        in_specs.append(tile)
        operands.append(add)
    if rms is not None:
        vec = pl.BlockSpec((1, n), lambda i, j, kk: (0, 0))
        in_specs += [tile, vec, tile]
        operands += list(rms)
        out_specs.append(vec)
        out_shape = [jax.ShapeDtypeStruct((m, n), F32), jax.ShapeDtypeStruct((1, n), F32)]
    out = pl.pallas_call(
        body, name=name,
        grid=(m // tm, n // tn, nk),
        in_specs=in_specs, out_specs=out_specs, out_shape=out_shape,
        scratch_shapes=[pltpu.VMEM((tm, tn), F32)] if nk > 1 else [],
        compiler_params=_cparams("arbitrary" if rms is not None else "parallel", "parallel", "arbitrary"),
    )(*operands)
    return out[0] if rms is None else out


def _rms_fwd(x, g, *, name):
    r, d = x.shape
    tr = _pick(r, 512, 16)

    def body(x_ref, g_ref, o_ref):
        xv = x_ref[...]
        rs = lax.rsqrt(jnp.mean(xv * xv, axis=-1, keepdims=True) + RMS_EPS)
        o_ref[...] = (xv * rs * g_ref[...]).astype(BF16)

    return pl.pallas_call(
        body, name=name, grid=(r // tr,),
        in_specs=[pl.BlockSpec((tr, d), lambda i: (i, 0)), pl.BlockSpec((1, d), lambda i: (0, 0))],
        out_specs=pl.BlockSpec((tr, d), lambda i: (i, 0)),
        out_shape=jax.ShapeDtypeStruct((r, d), BF16),
        compiler_params=_cparams("parallel"),
    )(x, g)


def _rms_bwd(x, g, dh, dres, *, name):
    r, d = x.shape
    tr = _pick(r, 256, 8)

    def body(x_ref, g_ref, dh_ref, dres_ref, dx_ref, dg_ref):
        @pl.when(pl.program_id(0) == 0)
        def _():
            dg_ref[...] = jnp.zeros_like(dg_ref)

        xv = x_ref[...]
        dhv = dh_ref[...].astype(F32)
        rs = lax.rsqrt(jnp.mean(xv * xv, axis=-1, keepdims=True) + RMS_EPS)
        xh = xv * rs
        dg_ref[...] += jnp.sum(dhv * xh, axis=0, keepdims=True)
        dxh = dhv * g_ref[...]
        dx_ref[...] = dres_ref[...] + rs * (dxh - xh * jnp.mean(dxh * xh, axis=-1, keepdims=True))

    return pl.pallas_call(
        body, name=name, grid=(r // tr,),
        in_specs=[pl.BlockSpec((tr, d), lambda i: (i, 0)), pl.BlockSpec((1, d), lambda i: (0, 0)),
                  pl.BlockSpec((tr, d), lambda i: (i, 0)), pl.BlockSpec((tr, d), lambda i: (i, 0))],
        out_specs=[pl.BlockSpec((tr, d), lambda i: (i, 0)), pl.BlockSpec((1, d), lambda i: (0, 0))],
        out_shape=[jax.ShapeDtypeStruct((r, d), F32), jax.ShapeDtypeStruct((1, d), F32)],
        compiler_params=_cparams("arbitrary"),
    )(x, g, dh, dres)


def _final_loss(x, g, target, *, name):
    r, d = x.shape
    tr = _pick(r, 256, 8)

    def body(x_ref, g_ref, t_ref, dx_ref, loss_ref, dg_ref):
        @pl.when(pl.program_id(0) == 0)
        def _():
            dg_ref[...] = jnp.zeros_like(dg_ref)
            loss_ref[...] = jnp.zeros_like(loss_ref)

        xv = x_ref[...]
        gv = g_ref[...]
        rs = lax.rsqrt(jnp.mean(xv * xv, axis=-1, keepdims=True) + RMS_EPS)
        xh = xv * rs
        err = xh * gv - t_ref[...]
        row_loss = jnp.mean(err * err, axis=-1, keepdims=True)
        loss_ref[...] += 0.5 * jnp.sum(row_loss, axis=0, keepdims=True)
        dy = err * (1.0 / d)
        dg_ref[...] += jnp.sum(dy * xh, axis=0, keepdims=True)
        dxh = dy * gv
        dx_ref[...] = rs * (dxh - xh * jnp.mean(dxh * xh, axis=-1, keepdims=True))

    return pl.pallas_call(
        body, name=name, grid=(r // tr,),
        in_specs=[pl.BlockSpec((tr, d), lambda i: (i, 0)), pl.BlockSpec((1, d), lambda i: (0, 0)),
                  pl.BlockSpec((tr, d), lambda i: (i, 0))],
        out_specs=[pl.BlockSpec((tr, d), lambda i: (i, 0)), pl.BlockSpec((1, 128), lambda i: (0, 0)),
                   pl.BlockSpec((1, d), lambda i: (0, 0))],
        out_shape=[jax.ShapeDtypeStruct((r, d), F32), jax.ShapeDtypeStruct((1, 128), F32),
                   jax.ShapeDtypeStruct((1, d), F32)],
        compiler_params=_cparams("arbitrary"),
    )(x, g, target)


def _cumsum_operand(strict_after, terms=1):
    r = lax.broadcasted_iota(jnp.int32, (terms * SB_BLOCK, 2 * SB_BLOCK), 0) % SB_BLOCK
    c = lax.broadcasted_iota(jnp.int32, (terms * SB_BLOCK, 2 * SB_BLOCK), 1)
    tri = (r > c) if strict_after else (r < c)
    return jnp.where((c >= SB_BLOCK) | tri, 1.0, 0.0).astype(BF16)


def _sb_scores(qh, kw, run, valid, after_ones):
    nb = kw.shape[0] // SB_BLOCK
    z = _dot_nt(qh, kw)
    lsp = jnp.minimum(z, 0.0) - jnp.log(1.0 + jnp.exp(-jnp.abs(z)))
    l1m = lsp - z
    if valid is not None:
        l1m = jnp.where(valid, l1m, 0.0)
    hi = l1m.astype(BF16)
    lo = (l1m - hi.astype(F32)).astype(BF16)
    later = [None] * nb
    for b in reversed(range(nb)):
        cols = slice(b * SB_BLOCK, (b + 1) * SB_BLOCK)
        ct = _dot(jnp.concatenate([hi[:, cols], lo[:, cols]], axis=1), after_ones)
        later[b] = run + ct[:, :SB_BLOCK]
        run = run + ct[:, SB_BLOCK:]
    a = jnp.exp(lsp + jnp.concatenate(later, axis=1))
    if valid is not None:
        a = jnp.where(valid, a, 0.0)
    return lsp, a, run


def _sb_setup(q_ref, span):
    qi = pl.program_id(1)
    rows = q_ref.shape[0]
    sd = (qi * rows + rows - 1) // span
    lane = lax.broadcasted_iota(jnp.int32, (rows, SB_BLOCK), 1)
    col = lax.broadcasted_iota(jnp.int32, (rows, span), 1)
    row = lax.broadcasted_iota(jnp.int32, (rows, span), 0)
    valid = col < (qi * rows - sd * span) + row
    q = q_ref[...] * SB_SCALE
    qhs = (jnp.where(lane < 64, q, 0.0).astype(BF16), jnp.where(lane >= 64, q, 0.0).astype(BF16))
    return lane, sd, valid, qhs


def _sb_fwd(p, *, name, gather=None):
    s = p.shape[0]
    qrows = min(SB_Q_FWD, s)
    nq = s // qrows
    kcol = BRANCH_W // SB_BLOCK
    span = min(SB_SPAN, s)

    def body(*refs):
        if gather is None:
            q_ref, k_ref, v_ref, o_ref = refs
        else:
            q_ref, k_ref, v_ref, x_ref, o_ref, g_ref, send_sems, recv_sems, local_sem = refs
            start, forward, finish = _gather_phases(x_ref, g_ref, send_sems, recv_sems, local_sem)
            step = pl.program_id(0) * nq + pl.program_id(1)
            pl.when(step == 0)(start)
        lane, sd, valid, qhs = _sb_setup(q_ref, span)
        after_ones = _cumsum_operand(True, terms=2)
        zero = jnp.zeros((qrows, SB_BLOCK), F32)

        def span_step(sb, carry, mask):
            rows = pl.ds(pl.multiple_of(sb * span, span), span)
            kw = k_ref[rows, :].astype(BF16)
            vw = v_ref[rows, :].astype(BF16)
            out = []
            for h in range(2):
                run, acc = carry[h]
                _, a, run = _sb_scores(qhs[h], kw, run, mask, after_ones)
                out.append((run, acc + _dot(a.astype(BF16), vw)))
            return tuple(out)

        carry = span_step(sd, ((zero, zero), (zero, zero)), valid)
        carry = lax.fori_loop(0, sd, lambda t, c: span_step(sd - 1 - t, c, None), carry)
        o_ref[...] = jnp.where(lane < 64, carry[0][1], carry[1][1])
        if gather is not None:
            pl.when(step == (kcol - 1) * nq + (3 * nq) // 4)(forward)
            pl.when(step == kcol * nq - 1)(finish)

    in_specs = [pl.BlockSpec((qrows, SB_BLOCK), lambda hp, qi: (qi, hp)),
                pl.BlockSpec((s, SB_BLOCK), lambda hp, qi: (0, kcol + hp)),
                pl.BlockSpec((s, SB_BLOCK), lambda hp, qi: (0, 2 * kcol + hp))]
    out_specs = [pl.BlockSpec((qrows, SB_BLOCK), lambda hp, qi: (qi, hp))]
    out_shape = [jax.ShapeDtypeStruct((s, BRANCH_W), F32)]
    operands = [p, p, p]
    scratch = []
    if gather is not None:
        in_specs.append(pl.BlockSpec(memory_space=pl.ANY))
        out_specs.append(pl.BlockSpec(memory_space=pl.ANY))
        out_shape.append(jax.ShapeDtypeStruct((N_DEV,) + gather.shape, gather.dtype))
        operands.append(gather)
        scratch = _GATHER_SEMS
    out = pl.pallas_call(
        body, name=name, grid=(kcol, nq), in_specs=in_specs, out_specs=out_specs, out_shape=out_shape,
        scratch_shapes=scratch, compiler_params=_cparams("arbitrary", "arbitrary"),
    )(*operands)
    return out[0] if gather is None else out


def _sb_bwd(p, dya, *, name, scatter=None):
    s = p.shape[0]
    qrows = min(SB_Q_BWD, s)
    nq = s // qrows
    kcol = BRANCH_W // SB_BLOCK
    span = min(SB_SPAN, s)
    per = span // SB_BLOCK

    def body(*refs):
        if scatter is None:
            q_ref, k_ref, v_ref, do_ref, dq_ref, dk_ref, dv_ref, a_s, b_s, dk_acc, dv_acc = refs
        else:
            (q_ref, k_ref, v_ref, do_ref, g_ref, dq_ref, dk_ref, dv_ref, r_ref,
             a_s, b_s, dk_acc, dv_acc, send_sems, recv_sems, local_sem) = refs
            start, finish = _scatter_phases(g_ref, r_ref, send_sems, recv_sems, local_sem)
            step = pl.program_id(0) * nq + pl.program_id(1)
            pl.when(step == 0)(start)
        qi = pl.program_id(1)

        @pl.when(qi == 0)
        def _():
            dk_acc[...] = jnp.zeros_like(dk_acc)
            dv_acc[...] = jnp.zeros_like(dv_acc)

        lane, sd, valid, qhs = _sb_setup(q_ref, span)
        after_ones = _cumsum_operand(True, terms=2)
        before_ones = _cumsum_operand(False)
        do = do_ref[...]
        dohs = (jnp.where(lane < 64, do, 0.0).astype(BF16), jnp.where(lane >= 64, do, 0.0).astype(BF16))
        zero = jnp.zeros((qrows, SB_BLOCK), F32)

        def rebuild(sb, runs, mask):
            rows = pl.ds(pl.multiple_of(sb * span, span), span)
            kw = k_ref[rows, :].astype(BF16)
            out = []
            for h in range(2):
                lsp, a, run = _sb_scores(qhs[h], kw, runs[h], mask, after_ones)
                beta = jnp.exp(lsp)
                if mask is not None:
                    beta = jnp.where(mask, beta, 0.0)
                a_s[h, sb] = a
                b_s[h, sb] = beta
                out.append(run)
            return tuple(out)

        runs = rebuild(sd, (zero, zero), valid)
        lax.fori_loop(0, sd, lambda t, r: rebuild(sd - 1 - t, r, None), runs)

        def accumulate(sb, carry):
            rows = pl.ds(pl.multiple_of(sb * span, span), span)
            kw = k_ref[rows, :].astype(BF16)
            vw = v_ref[rows, :].astype(BF16)
            out = []
            dk_span = jnp.zeros((span, SB_BLOCK), F32)
            dv_span = jnp.zeros((span, SB_BLOCK), F32)
            for h in range(2):
                pg, dq = carry[h]
                a = a_s[h, sb]
                beta = b_s[h, sb]
                g = a * _dot_nt(dohs[h], vw)
                gb = g.astype(BF16)
                before = [None] * per
                for b in range(per):
                    cols = slice(b * SB_BLOCK, (b + 1) * SB_BLOCK)
                    gt = _dot(gb[:, cols], before_ones)
                    before[b] = pg + gt[:, :SB_BLOCK]
                    pg = pg + gt[:, SB_BLOCK:]
                dz = (g * (1.0 - beta) - beta * jnp.concatenate(before, axis=1)).astype(BF16)
                dk_span = dk_span + _dot_tn(dz, qhs[h])
                dv_span = dv_span + _dot_tn(a.astype(BF16), dohs[h])
                out.append((pg, dq + _dot(dz, kw)))
            dk_acc[rows, :] += dk_span
            dv_acc[rows, :] += dv_span
            return tuple(out)

        carry = lax.fori_loop(0, sd + 1, accumulate, ((zero, zero), (zero, zero)))
        dq_ref[...] = (jnp.where(lane < 64, carry[0][1], carry[1][1]) * SB_SCALE).astype(BF16)

        @pl.when(qi == nq - 1)
        def _():
            dk_ref[...] = dk_acc[...].astype(BF16)
            dv_ref[...] = dv_acc[...].astype(BF16)

        if scatter is not None:
            pl.when(step == kcol * nq - 1)(finish)

    blk = pl.BlockSpec((qrows, SB_BLOCK), lambda hp, qi: (qi, hp))
    col = pl.BlockSpec((s, SB_BLOCK), lambda hp, qi: (0, hp))
    out = jax.ShapeDtypeStruct((s, BRANCH_W), BF16)
    in_specs = [blk,
                pl.BlockSpec((s, SB_BLOCK), lambda hp, qi: (0, kcol + hp)),
                pl.BlockSpec((s, SB_BLOCK), lambda hp, qi: (0, 2 * kcol + hp)),
                blk]
    out_specs = [blk, col, col]
    out_shape = [out, out, out]
    operands = [p, p, p, dya]
    scratch = [pltpu.VMEM((2, s // span, qrows, span), F32), pltpu.VMEM((2, s // span, qrows, span), F32),
               pltpu.VMEM((s, SB_BLOCK), F32), pltpu.VMEM((s, SB_BLOCK), F32)]
    if scatter is not None:
        in_specs.append(pl.BlockSpec(memory_space=pl.ANY))
        out_specs.append(pl.BlockSpec(memory_space=pl.ANY))
        out_shape.append(jax.ShapeDtypeStruct(scatter.shape, scatter.dtype))
        operands.append(scatter)
        scratch = scratch + _SCATTER_SEMS
    return pl.pallas_call(
        body, name=name, grid=(kcol, nq), in_specs=in_specs, out_specs=out_specs, out_shape=out_shape,
        scratch_shapes=scratch, compiler_params=_cparams("arbitrary", "arbitrary"),
    )(*operands)


_INV_SQRT2 = 0.7071067811865476
_INV_SQRT2PI = 0.3989422804014327


def _gelu(x):
    return 0.5 * x * (1.0 + lax.erf(x * _INV_SQRT2))


def _gelu_grad(x):
    return 0.5 * (1.0 + lax.erf(x * _INV_SQRT2)) + x * _INV_SQRT2PI * jnp.exp(-0.5 * x * x)


def _chunk_mask(transposed=False):
    r = lax.broadcasted_iota(jnp.int32, (SGU_LEN, SGU_LEN), 0)
    c = lax.broadcasted_iota(jnp.int32, (SGU_LEN, SGU_LEN), 1)
    return (c // 64) >= (r // 64) if transposed else (r // 64) >= (c // 64)


def _sgu_norm(v_raw, g, b):
    zv = _gelu(v_raw)
    xc = zv - jnp.mean(zv, axis=-1, keepdims=True)
    rs = lax.rsqrt(jnp.mean(xc * xc, axis=-1, keepdims=True) + LN_EPS)
    xh = xc * rs
    return xh, rs, xh * g + b


def _sgu_fwd(p, ln_g, ln_b, w, b_col, *, name):
    s = p.shape[0]
    tr = _pick(s, 512, SGU_LEN)

    def body(u_ref, v_ref, g_ref, b_ref, w_ref, bc_ref, o_ref):
        mask = _chunk_mask()
        zu = _gelu(u_ref[...])
        _, _, vn = _sgu_norm(v_ref[...], g_ref[...], b_ref[...])
        vnb = vn.astype(BF16)
        for gi in range(SGU_GROUPS):
            wg = jnp.where(mask, w_ref[gi], 0.0).astype(BF16)
            cs = slice(gi * SGU_LEN, (gi + 1) * SGU_LEN)
            for c in range(tr // SGU_LEN):
                rs_ = slice(c * SGU_LEN, (c + 1) * SGU_LEN)
                vm = _dot(wg, vnb[rs_, cs]) + bc_ref[gi]
                o_ref[rs_, cs] = zu[rs_, cs] * vm

    vec = pl.BlockSpec((1, BRANCH_W), lambda i: (0, 0))
    return pl.pallas_call(
        body, name=name, grid=(s // tr,),
        in_specs=[pl.BlockSpec((tr, BRANCH_W), lambda i: (i, 3)), pl.BlockSpec((tr, BRANCH_W), lambda i: (i, 4)),
                  vec, vec,
                  pl.BlockSpec((SGU_GROUPS, SGU_LEN, SGU_LEN), lambda i: (0, 0, 0)),
                  pl.BlockSpec((SGU_GROUPS, SGU_LEN, 1), lambda i: (0, 0, 0))],
        out_specs=pl.BlockSpec((tr, BRANCH_W), lambda i: (i, 0)),
        out_shape=jax.ShapeDtypeStruct((s, BRANCH_W), F32),
        compiler_params=_cparams("parallel"),
    )(p, p, ln_g, ln_b, w, b_col)


def _sgu_bwd(p, dyb, ln_g, ln_b, w, w_t, b_col, *, name):
    s = p.shape[0]
    tr = _pick(s, 256, SGU_LEN)

    def body(u_ref, v_ref, dy_ref, g_ref, b_ref, w_ref, wt_ref, bc_ref,
             dz_ref, dg_ref, db_ref, dw_ref, dbc_ref, dvn_s):
        @pl.when(pl.program_id(0) == 0)
        def _():
            dg_ref[...] = jnp.zeros_like(dg_ref)
            db_ref[...] = jnp.zeros_like(db_ref)
            dw_ref[...] = jnp.zeros_like(dw_ref)
            dbc_ref[...] = jnp.zeros_like(dbc_ref)

        mask = _chunk_mask()
        mask_t = _chunk_mask(transposed=True)
        u_raw = u_ref[...]
        v_raw = v_ref[...]
        dy = dy_ref[...]
        zu = _gelu(u_raw)
        xh, rs, vn = _sgu_norm(v_raw, g_ref[...], b_ref[...])
        vnb = vn.astype(BF16)
        dvm_all = dy * zu
        for gi in range(SGU_GROUPS):
            wg = jnp.where(mask, w_ref[gi], 0.0).astype(BF16)
            wgt = jnp.where(mask_t, wt_ref[gi], 0.0).astype(BF16)
            cs = slice(gi * SGU_LEN, (gi + 1) * SGU_LEN)
            dw_g = jnp.zeros((SGU_LEN, SGU_LEN), F32)
            db_g = jnp.zeros((SGU_LEN, 1), F32)
            for c in range(tr // SGU_LEN):
                rs_ = slice(c * SGU_LEN, (c + 1) * SGU_LEN)
                vm = _dot(wg, vnb[rs_, cs]) + bc_ref[gi]
                dz_ref[rs_, cs] = (dy[rs_, cs] * vm * _gelu_grad(u_raw[rs_, cs])).astype(BF16)
                dvm = dvm_all[rs_, cs]
                dvmb = dvm.astype(BF16)
                dw_g = dw_g + _dot_nt(dvmb, vnb[rs_, cs])
                db_g = db_g + jnp.sum(dvm, axis=1, keepdims=True)
                dvn_s[rs_, cs] = _dot(wgt, dvmb)
            dw_ref[gi] += jnp.where(mask, dw_g, 0.0)
            dbc_ref[gi] += db_g
        dvn = dvn_s[...]
        dg_ref[...] += jnp.sum(dvn * xh, axis=0, keepdims=True)
        db_ref[...] += jnp.sum(dvn, axis=0, keepdims=True)
        dxh = dvn * g_ref[...]
        dzv = rs * (dxh - jnp.mean(dxh, axis=-1, keepdims=True) - xh * jnp.mean(dxh * xh, axis=-1, keepdims=True))
        dz_ref[:, BRANCH_W:] = (dzv * _gelu_grad(v_raw)).astype(BF16)

    vec = pl.BlockSpec((1, BRANCH_W), lambda i: (0, 0))
    wspec = pl.BlockSpec((SGU_GROUPS, SGU_LEN, SGU_LEN), lambda i: (0, 0, 0))
    bspec = pl.BlockSpec((SGU_GROUPS, SGU_LEN, 1), lambda i: (0, 0, 0))
    return pl.pallas_call(
        body, name=name, grid=(s // tr,),
        in_specs=[pl.BlockSpec((tr, BRANCH_W), lambda i: (i, 3)), pl.BlockSpec((tr, BRANCH_W), lambda i: (i, 4)),
                  pl.BlockSpec((tr, BRANCH_W), lambda i: (i, 0)), vec, vec, wspec, wspec, bspec],
        out_specs=[pl.BlockSpec((tr, 2 * BRANCH_W), lambda i: (i, 0)), vec, vec, wspec, bspec],
        out_shape=[jax.ShapeDtypeStruct((s, 2 * BRANCH_W), BF16),
                   jax.ShapeDtypeStruct((1, BRANCH_W), F32), jax.ShapeDtypeStruct((1, BRANCH_W), F32),
                   jax.ShapeDtypeStruct((SGU_GROUPS, SGU_LEN, SGU_LEN), F32),
                   jax.ShapeDtypeStruct((SGU_GROUPS, SGU_LEN, 1), F32)],
        scratch_shapes=[pltpu.VMEM((tr, BRANCH_W), F32)],
        compiler_params=_cparams("arbitrary"),
    )(p, p, dyb, ln_g, ln_b, w, w_t, b_col)


def _shift_down(x, prev8, k):
    rolled = pltpu.roll(x, k, 0)
    r8 = lax.broadcasted_iota(jnp.int32, prev8.shape, 0)
    head = jnp.where(r8 < k, pltpu.roll(prev8, k, 0), rolled[:HALO])
    return jnp.concatenate([head, rolled[HALO:]], axis=0)


def _shift_up(x, next8, k):
    n = x.shape[0]
    rolled = pltpu.roll(x, n - k, 0)
    r8 = lax.broadcasted_iota(jnp.int32, next8.shape, 0)
    tail = jnp.where(r8 >= HALO - k, pltpu.roll(next8, HALO - k, 0), rolled[n - HALO:])
    return jnp.concatenate([rolled[:n - HALO], tail], axis=0)


def _conv_specs(s, tr):
    nb = tr // HALO
    last = s // HALO - 1
    tile = lambda cb: pl.BlockSpec((tr, 128), lambda j, i: (i, cb * 4 + j))
    above = lambda cb: pl.BlockSpec((HALO, 128), lambda j, i: (jnp.maximum(i * nb - 1, 0), cb * 4 + j))
    below = lambda cb: pl.BlockSpec((HALO, 128), lambda j, i: (jnp.minimum((i + 1) * nb, last), cb * 4 + j))
    return tile, above, below


def _conv_fwd(p, cw, *, name):
    s = p.shape[0]
    tr = _pick(s, 512, HALO)
    tile, above, _ = _conv_specs(s, tr)

    def body(cb_ref, cc_ref, cx_ref, ccp_ref, cxp_ref, w_ref, o_ref):
        first = pl.program_id(1) == 0
        y = cc_ref[...] * cx_ref[...]
        yp = jnp.where(first, 0.0, ccp_ref[...] * cxp_ref[...])
        conv = w_ref[2:3, :] * y + w_ref[1:2, :] * _shift_down(y, yp, 1) + w_ref[0:1, :] * _shift_down(y, yp, 2)
        o_ref[...] = cb_ref[...] * conv

    return pl.pallas_call(
        body, name=name, grid=(4, s // tr),
        in_specs=[tile(5), tile(6), tile(7), above(6), above(7), pl.BlockSpec((3, 128), lambda j, i: (0, j))],
        out_specs=pl.BlockSpec((tr, 128), lambda j, i: (i, j)),
        out_shape=jax.ShapeDtypeStruct((s, BRANCH_W), F32),
        compiler_params=_cparams("parallel", "parallel"),
    )(p, p, p, p, p, cw)


def _conv_bwd(p, dyc, cw, *, name):
    s = p.shape[0]
    tr = _pick(s, 512, HALO)
    nt = s // tr
    nb = tr // HALO
    last = s // HALO - 1
    tile, above, below = _conv_specs(s, tr)

    def body(cb_ref, cc_ref, cx_ref, ccp_ref, cxp_ref, cbn_ref, dy_ref, dyn_ref, w_ref,
             dcb_ref, dcc_ref, dcx_ref, dw_ref):
        i = pl.program_id(1)

        @pl.when(i == 0)
        def _():
            dw_ref[...] = jnp.zeros_like(dw_ref)

        cb = cb_ref[...]
        cc = cc_ref[...]
        cx = cx_ref[...]
        y = cc * cx
        yp = jnp.where(i == 0, 0.0, ccp_ref[...] * cxp_ref[...])
        y1 = _shift_down(y, yp, 1)
        y2 = _shift_down(y, yp, 2)
        w0, w1, w2 = w_ref[0:1, :], w_ref[1:2, :], w_ref[2:3, :]
        conv = w2 * y + w1 * y1 + w0 * y2
        dyc_v = dy_ref[...]
        dconv = dyc_v * cb
        dn = jnp.where(i == nt - 1, 0.0, dyn_ref[...] * cbn_ref[...])
        dyv = w2 * dconv + w1 * _shift_up(dconv, dn, 1) + w0 * _shift_up(dconv, dn, 2)
        dcb_ref[...] = (dyc_v * conv).astype(BF16)
        dcc_ref[...] = (dyv * cx).astype(BF16)
        dcx_ref[...] = (dyv * cc).astype(BF16)
        dw_ref[0:1, :] += jnp.sum(dconv * y2, axis=0, keepdims=True)
        dw_ref[1:2, :] += jnp.sum(dconv * y1, axis=0, keepdims=True)
        dw_ref[2:3, :] += jnp.sum(dconv * y, axis=0, keepdims=True)

    dy_tile = pl.BlockSpec((tr, 128), lambda j, i: (i, j))
    dy_below = pl.BlockSpec((HALO, 128), lambda j, i: (jnp.minimum((i + 1) * nb, last), j))
    out_tile = lambda cb: pl.BlockSpec((tr, 128), lambda j, i: (i, cb * 4 + j))
    w_spec = pl.BlockSpec((3, 128), lambda j, i: (0, j))
    dcb, dcc, dcx, dw = pl.pallas_call(
        body, name=name, grid=(4, nt),
        in_specs=[tile(5), tile(6), tile(7), above(6), above(7), below(5), dy_tile, dy_below, w_spec],
        out_specs=[dy_tile, dy_tile, dy_tile, w_spec],
        out_shape=[jax.ShapeDtypeStruct((s, BRANCH_W), BF16)] * 3 + [jax.ShapeDtypeStruct((3, BRANCH_W), F32)],
        compiler_params=_cparams("parallel", "arbitrary"),
    )(p, p, p, p, p, p, dyc, dyc, cw)
    return dcb, dcc, dcx, dw


def _merge_fwd(ya, yb, yc, wb, p, *, name):
    s = p.shape[0]
    tr = _pick(s, 256, 16)

    def body(ya_ref, yb_ref, yc_ref, wb_ref, g0_ref, g1_ref, g2_ref, o_ref):
        acc = jnp.zeros((tr, D_MODEL), F32)
        for n, (y_ref, g_ref) in enumerate(((ya_ref, g0_ref), (yb_ref, g1_ref), (yc_ref, g2_ref))):
            acc = acc + _sigmoid(g_ref[...]) * _dot(y_ref[...].astype(BF16), wb_ref[n])
        o_ref[...] = acc.astype(BF16)

    yspec = pl.BlockSpec((tr, BRANCH_W), lambda i: (i, 0))
    gate = lambda n: pl.BlockSpec((tr, D_MODEL), lambda i: (i, 4 + n))
    return pl.pallas_call(
        body, name=name, grid=(s // tr,),
        in_specs=[yspec, yspec, yspec, pl.BlockSpec((3, BRANCH_W, D_MODEL), lambda i: (0, 0, 0)),
                  gate(0), gate(1), gate(2)],
        out_specs=pl.BlockSpec((tr, D_MODEL), lambda i: (i, 0)),
        out_shape=jax.ShapeDtypeStruct((s, D_MODEL), BF16),
        compiler_params=_cparams("parallel"),
    )(ya, yb, yc, wb, p, p, p)


def _merge_bwd(dm, ya, yb, yc, wb, p, *, name):
    s = p.shape[0]
    tr = _pick(s, 256, 16)

    def body(dm_ref, ya_ref, yb_ref, yc_ref, wb_ref, g0_ref, g1_ref, g2_ref,
             dya_ref, dyb_ref, dyc_ref, dg_ref, dbrd_ref):
        dmv = dm_ref[...]
        ys = (ya_ref, yb_ref, yc_ref)
        gs = (g0_ref, g1_ref, g2_ref)
        dys = (dya_ref, dyb_ref, dyc_ref)
        for n in range(3):
            brd = _dot(ys[n][...].astype(BF16), wb_ref[n])
            sg = _sigmoid(gs[n][...])
            dbrd = (sg * dmv).astype(BF16)
            dbrd_ref[n] = dbrd
            dg_ref[:, n * D_MODEL:(n + 1) * D_MODEL] = (dmv * brd * sg * (1.0 - sg)).astype(BF16)
            dys[n][...] = _dot_nt(dbrd, wb_ref[n])

    yspec = pl.BlockSpec((tr, BRANCH_W), lambda i: (i, 0))
    gate = lambda n: pl.BlockSpec((tr, D_MODEL), lambda i: (i, 4 + n))
    row = pl.BlockSpec((tr, D_MODEL), lambda i: (i, 0))
    return pl.pallas_call(
        body, name=name, grid=(s // tr,),
        in_specs=[row, yspec, yspec, yspec, pl.BlockSpec((3, BRANCH_W, D_MODEL), lambda i: (0, 0, 0)),
                  gate(0), gate(1), gate(2)],
        out_specs=[yspec, yspec, yspec, pl.BlockSpec((tr, 3 * D_MODEL), lambda i: (i, 0)),
                   pl.BlockSpec((3, tr, D_MODEL), lambda i: (0, i, 0))],
        out_shape=[jax.ShapeDtypeStruct((s, BRANCH_W), F32)] * 3
                  + [jax.ShapeDtypeStruct((s, 3 * D_MODEL), BF16), jax.ShapeDtypeStruct((3, s, D_MODEL), BF16)],
        compiler_params=_cparams("parallel"),
    )(dm, ya, yb, yc, wb, p, p, p)


def _xa_probs(q, k):
    sc = _dot_nt(q, k) * XA_SCALE
    e = jnp.exp(sc - jnp.max(sc, axis=-1, keepdims=True))
    return e / jnp.sum(e, axis=-1, keepdims=True)


def _xa_fwd(q, k, v, *, name):
    s = q.shape[0]
    mt = k.shape[0]
    tr = _pick(s, 512, 16)

    def body(q_ref, k_ref, v_ref, o_ref):
        pr = _xa_probs(q_ref[...], k_ref[...])
        o_ref[...] = _dot(pr.astype(BF16), v_ref[...]).astype(BF16)

    qs = pl.BlockSpec((tr, XA_HEAD), lambda h, i: (i, h))
    ks = pl.BlockSpec((mt, XA_HEAD), lambda h, i: (0, h))
    return pl.pallas_call(
        body, name=name, grid=(D_MODEL // XA_HEAD, s // tr),
        in_specs=[qs, ks, ks], out_specs=qs,
        out_shape=jax.ShapeDtypeStruct((s, D_MODEL), BF16),
        compiler_params=_cparams("parallel", "parallel"),
    )(q, k, v)


def _xa_bwd(q, k, v, do, *, name):
    s = q.shape[0]
    mt = k.shape[0]
    tr = _pick(s, 512, 16)

    def body(q_ref, k_ref, v_ref, do_ref, dq_ref, dk_ref, dv_ref):
        @pl.when(pl.program_id(1) == 0)
        def _():
            dk_ref[...] = jnp.zeros_like(dk_ref)
            dv_ref[...] = jnp.zeros_like(dv_ref)

        qv = q_ref[...]
        kv = k_ref[...]
        dov = do_ref[...]
        pr = _xa_probs(qv, kv)
        dpr = _dot_nt(dov, v_ref[...])
        ds = (pr * (dpr - jnp.sum(dpr * pr, axis=-1, keepdims=True)) * XA_SCALE).astype(BF16)
        dq_ref[...] = _dot(ds, kv).astype(BF16)
        dk_ref[...] += _dot_tn(ds, qv)
        dv_ref[...] += _dot_tn(pr.astype(BF16), dov)

    qs = pl.BlockSpec((tr, XA_HEAD), lambda h, i: (i, h))
    ks = pl.BlockSpec((mt, XA_HEAD), lambda h, i: (0, h))
    return pl.pallas_call(
        body, name=name, grid=(D_MODEL // XA_HEAD, s // tr),
        in_specs=[qs, ks, ks, qs], out_specs=[qs, ks, ks],
        out_shape=[jax.ShapeDtypeStruct((s, D_MODEL), BF16), jax.ShapeDtypeStruct((mt, D_MODEL), F32),
                   jax.ShapeDtypeStruct((mt, D_MODEL), F32)],
        compiler_params=_cparams("parallel", "arbitrary"),
    )(q, k, v, do)


def _swiglu_fwd(ab, *, name):
    s, f = ab.shape[0], ab.shape[1] // 2
    tr = _pick(s, 256, 16)

    def body(a_ref, b_ref, o_ref):
        av = a_ref[...]
        o_ref[...] = (av * _sigmoid(av) * b_ref[...]).astype(BF16)

    half = lambda c: pl.BlockSpec((tr, f), lambda i: (i, c))
    return pl.pallas_call(
        body, name=name, grid=(s // tr,), in_specs=[half(0), half(1)], out_specs=half(0),
        out_shape=jax.ShapeDtypeStruct((s, f), BF16), compiler_params=_cparams("parallel"),
    )(ab, ab)


def _swiglu_bwd(ab, dh, *, name):
    s, f = ab.shape[0], ab.shape[1] // 2
    tr = _pick(s, 256, 16)

    def body(a_ref, b_ref, dh_ref, o_ref):
        av = a_ref[...]
        dhv = dh_ref[...]
        sg = _sigmoid(av)
        silu = av * sg
        o_ref[:, :f] = (dhv * b_ref[...] * (sg + silu * (1.0 - sg))).astype(BF16)
        o_ref[:, f:] = (dhv * silu).astype(BF16)

    half = lambda c: pl.BlockSpec((tr, f), lambda i: (i, c))
    return pl.pallas_call(
        body, name=name, grid=(s // tr,), in_specs=[half(0), half(1), half(0)],
        out_specs=pl.BlockSpec((tr, 2 * f), lambda i: (i, 0)),
        out_shape=jax.ShapeDtypeStruct((s, 2 * f), BF16), compiler_params=_cparams("parallel"),
    )(ab, ab, dh)


def _adamw(w, g, m, v, *, name):
    r, c = w.shape
    tr = _pick(r, 512, 8)

    def body(w_ref, g_ref, m_ref, v_ref, d_ref, mo_ref, vo_ref):
        gv = g_ref[...]
        mn = ADAM_B1 * m_ref[...] + (1.0 - ADAM_B1) * gv
        vn = ADAM_B2 * v_ref[...] + (1.0 - ADAM_B2) * (gv * gv)
        m_hat = mn / (1.0 - ADAM_B1 ** ADAM_STEP)
        v_hat = vn / (1.0 - ADAM_B2 ** ADAM_STEP)
        d_ref[...] = -ADAM_LR * (m_hat / (jnp.sqrt(v_hat) + ADAM_EPS) + ADAM_WD * w_ref[...])
        mo_ref[...] = mn
        vo_ref[...] = vn

    spec = pl.BlockSpec((tr, c), lambda i: (i, 0))
    shp = jax.ShapeDtypeStruct((r, c), F32)
    return pl.pallas_call(
        body, name=name, grid=(r // tr,), in_specs=[spec] * 4, out_specs=[spec] * 3,
        out_shape=[shp] * 3, compiler_params=_cparams("parallel"),
    )(w, g, m, v)


def _position():
    return lax.axis_index("x"), lax.axis_index("y"), lax.axis_index("c")


def _all_gather(x, *, name):
    t, c_ = x.shape

    def body(x_ref, out_ref, send_sems, recv_sems, local_sem):
        start, forward, finish = _gather_phases(x_ref, out_ref, send_sems, recv_sems, local_sem)
        start()
        forward()
        finish()

    return pl.pallas_call(
        body, name=name,
        out_shape=jax.ShapeDtypeStruct((N_DEV, t, c_), x.dtype),
        in_specs=[pl.BlockSpec(memory_space=pl.ANY)],
        out_specs=pl.BlockSpec(memory_space=pl.ANY),
        scratch_shapes=_GATHER_SEMS,
    )(x)


_GATHER_SEMS = [pltpu.SemaphoreType.DMA((7,)), pltpu.SemaphoreType.DMA((7,)), pltpu.SemaphoreType.DMA]


def _gather_phases(x_ref, out_ref, send_sems, recv_sems, local_sem):
    x_, y_, c = _position()
    me, sibling = (x_, y_, c), (x_, y_, 1 - c)
    chips = [(1 - x_, y_), (x_, 1 - y_), (1 - x_, 1 - y_)]

    def block(px, py, pc):
        return out_ref.at[4 * px + 2 * py + pc]

    def copy(k, blk, to, src=None):
        return pltpu.make_async_remote_copy(
            src_ref=block(*blk) if src is None else src, dst_ref=block(*blk),
            send_sem=send_sems.at[k], recv_sem=recv_sems.at[k], device_id=to, device_id_type=MESH)

    mine = pltpu.make_async_copy(x_ref, block(*me), local_sem)
    first = [copy(0, me, sibling, src=x_ref)]
    first += [copy(1 + j, me, (*chip, c), src=x_ref) for j, chip in enumerate(chips)]
    passed = [copy(4 + j, (*chip, c), sibling) for j, chip in enumerate(chips)]

    def start():
        mine.start()
        for cp in first:
            cp.start()

    def forward():
        for j, chip in enumerate(chips):
            copy(1 + j, (*chip, c), me).wait_recv()
            passed[j].start()

    def finish():
        copy(0, sibling, me).wait_recv()
        for j, chip in enumerate(chips):
            copy(4 + j, (*chip, 1 - c), me).wait_recv()
        for cp in first + passed:
            cp.wait_send()
        mine.wait()

    return start, forward, finish


_SCATTER_SEMS = [pltpu.SemaphoreType.DMA((7,)), pltpu.SemaphoreType.DMA((7,)), pltpu.SemaphoreType.DMA]


def _scatter_phases(g_ref, r_ref, send_sems, recv_sems, local_sem):
    x_, y_, c = _position()
    me = 4 * x_ + 2 * y_ + c
    local = pltpu.make_async_copy(g_ref.at[me], r_ref.at[me], local_sem)
    copies = []
    for k in range(1, N_DEV):
        to = (x_ ^ (k >> 2), y_ ^ ((k >> 1) & 1), c ^ (k & 1))
        copies.append(pltpu.make_async_remote_copy(
            src_ref=g_ref.at[me ^ k], dst_ref=r_ref.at[me], send_sem=send_sems.at[k - 1],
            recv_sem=recv_sems.at[k - 1], device_id=to, device_id_type=MESH))

    def start():
        local.start()
        for cp in copies:
            cp.start()

    def finish():
        for k in range(1, N_DEV):
            pltpu.make_async_remote_copy(
                src_ref=g_ref.at[me], dst_ref=r_ref.at[me ^ k], send_sem=send_sems.at[k - 1],
                recv_sem=recv_sems.at[k - 1], device_id=(x_, y_, c), device_id_type=MESH).wait_recv()
        for cp in copies:
            cp.wait_send()
        local.wait()

    return start, finish


def _sum_devices(r8, *, name):
    _, t, c_ = r8.shape
    tr = _pick(t, 256, 16)

    def body(r_ref, o_ref):
        acc = r_ref[0].astype(F32)
        for d in range(1, N_DEV):
            acc = acc + r_ref[d].astype(F32)
        o_ref[...] = acc

    return pl.pallas_call(
        body, name=name, grid=(t // tr,),
        in_specs=[pl.BlockSpec((N_DEV, tr, c_), lambda i: (0, i, 0))],
        out_specs=pl.BlockSpec((tr, c_), lambda i: (i, 0)),
        out_shape=jax.ShapeDtypeStruct((t, c_), F32),
        compiler_params=_cparams("parallel"),
    )(r8)


def _all_reduce_small(x, *, name):
    r, c_ = x.shape

    def body(x_ref, o_ref, buf, send_sems, recv_sems):
        x_, y_, c = _position()
        me = 4 * x_ + 2 * y_ + c
        buf[me] = x_ref[...]
        copies = []
        for k in range(1, N_DEV):
            to = (x_ ^ (k >> 2), y_ ^ ((k >> 1) & 1), c ^ (k & 1))
            copies.append(pltpu.make_async_remote_copy(
                src_ref=x_ref, dst_ref=buf.at[me], send_sem=send_sems.at[k - 1], recv_sem=recv_sems.at[k - 1],
                device_id=to, device_id_type=MESH))
        for cp in copies:
            cp.start()
        for k in range(1, N_DEV):
            src = me ^ k
            pltpu.make_async_remote_copy(
                src_ref=x_ref, dst_ref=buf.at[src], send_sem=send_sems.at[k - 1], recv_sem=recv_sems.at[k - 1],
                device_id=(x_, y_, c), device_id_type=MESH).wait_recv()
        for cp in copies:
            cp.wait_send()
        acc = buf[0]
        for d in range(1, N_DEV):
            acc = acc + buf[d]
        o_ref[...] = acc

    return pl.pallas_call(
        body, name=name,
        out_shape=jax.ShapeDtypeStruct((r, c_), F32),
        in_specs=[pl.BlockSpec(memory_space=pltpu.VMEM)],
        out_specs=pl.BlockSpec(memory_space=pltpu.VMEM),
        scratch_shapes=[pltpu.VMEM((N_DEV, r, c_), F32), pltpu.SemaphoreType.DMA((7,)), pltpu.SemaphoreType.DMA((7,))],
    )(x)


def _rs_pair_exchange(g8, *, name):
    _, t, c_ = g8.shape

    def body(g_ref, r_ref, send_sems, recv_sems):
        x_, y_, c = _position()
        copies = [pltpu.make_async_remote_copy(
            src_ref=g_ref.at[2 * ch + (1 - c)], dst_ref=r_ref.at[ch],
            send_sem=send_sems.at[ch], recv_sem=recv_sems.at[ch],
            device_id=(x_, y_, 1 - c), device_id_type=MESH) for ch in range(4)]
        for cp in copies:
            cp.start()
        for cp in copies:
            cp.wait()

    return pl.pallas_call(
        body, name=name,
        out_shape=jax.ShapeDtypeStruct((4, t, c_), g8.dtype),
        in_specs=[pl.BlockSpec(memory_space=pl.ANY)],
        out_specs=pl.BlockSpec(memory_space=pl.ANY),
        scratch_shapes=[pltpu.SemaphoreType.DMA((4,)), pltpu.SemaphoreType.DMA((4,))],
    )(g8)


def _pair_add(core, g8, recv, *, name):
    _, t, c_ = g8.shape
    tr = _pick(t, 512, 16)

    def body(core_ref, g_ref, r_ref, o_ref):
        o_ref[...] = (g_ref[...].astype(F32) + r_ref[...].astype(F32)).astype(o_ref.dtype)

    grid_spec = pltpu.PrefetchScalarGridSpec(
        num_scalar_prefetch=1, grid=(4, t // tr),
        in_specs=[pl.BlockSpec((None, tr, c_), lambda ch, i, core_ref: (2 * ch + core_ref[0], i, 0)),
                  pl.BlockSpec((None, tr, c_), lambda ch, i, core_ref: (ch, i, 0))],
        out_specs=pl.BlockSpec((None, tr, c_), lambda ch, i, core_ref: (ch, i, 0)))
    return pl.pallas_call(
        body, name=name, grid_spec=grid_spec,
        out_shape=jax.ShapeDtypeStruct((4, t, c_), g8.dtype),
        compiler_params=_cparams("parallel", "parallel"),
    )(core, g8, recv)


def _rs_chip_exchange(part, *, name):
    _, t, c_ = part.shape

    def body(p_ref, r_ref, send_sems, recv_sems, local_sem):
        x_, y_, c = _position()
        mine = 2 * x_ + y_
        local = pltpu.make_async_copy(p_ref.at[mine], r_ref.at[mine], local_sem)
        local.start()
        chips = [(1 - x_, y_), (x_, 1 - y_), (1 - x_, 1 - y_)]
        copies = [pltpu.make_async_remote_copy(
            src_ref=p_ref.at[2 * px + py], dst_ref=r_ref.at[mine],
            send_sem=send_sems.at[k], recv_sem=recv_sems.at[k],
            device_id=(px, py, c), device_id_type=MESH) for k, (px, py) in enumerate(chips)]
        for cp in copies:
            cp.start()
        for k, (px, py) in enumerate(chips):
            pltpu.make_async_remote_copy(
                src_ref=p_ref.at[mine], dst_ref=r_ref.at[2 * px + py],
                send_sem=send_sems.at[k], recv_sem=recv_sems.at[k],
                device_id=(x_, y_, c), device_id_type=MESH).wait_recv()
        for cp in copies:
            cp.wait_send()
        local.wait()

    return pl.pallas_call(
        body, name=name,
        out_shape=jax.ShapeDtypeStruct((4, t, c_), part.dtype),
        in_specs=[pl.BlockSpec(memory_space=pl.ANY)],
        out_specs=pl.BlockSpec(memory_space=pl.ANY),
        scratch_shapes=[pltpu.SemaphoreType.DMA((3,)), pltpu.SemaphoreType.DMA((3,)), pltpu.SemaphoreType.DMA],
    )(part)


def _sum_chips(r4, *, name):
    _, t, c_ = r4.shape
    tr = _pick(t, 512, 16)

    def body(r_ref, o_ref):
        acc = r_ref[0].astype(F32)
        for ch in range(1, 4):
            acc = acc + r_ref[ch].astype(F32)
        o_ref[...] = acc

    return pl.pallas_call(
        body, name=name, grid=(t // tr,),
        in_specs=[pl.BlockSpec((4, tr, c_), lambda i: (0, i, 0))],
        out_specs=pl.BlockSpec((tr, c_), lambda i: (i, 0)),
        out_shape=jax.ShapeDtypeStruct((t, c_), F32),
        compiler_params=_cparams("parallel"),
    )(r4)


BIG = (
    ("w_in", (IN_COLS // N_DEV, D_MODEL), 0),
    ("w_branch", (3, BRANCH_W, D_MODEL // N_DEV), 2),
    ("w_out", (D_MODEL // N_DEV, D_MODEL), 0),
    ("w_q_xa", (D_MODEL // N_DEV, D_MODEL), 0),
    ("w_k_xa", (D_MODEL // N_DEV, D_MODEL), 0),
    ("w_v_xa", (D_MODEL // N_DEV, D_MODEL), 0),
    ("w_o_xa", (D_MODEL // N_DEV, D_MODEL), 0),
    ("w_gate_ffn", (FFN // N_DEV, D_MODEL), 0),
    ("w_up_ffn", (FFN // N_DEV, D_MODEL), 0),
    ("w_down_ffn", (FFN // N_DEV, D_MODEL), 0),
)
TRANSPOSED = ("w_in", "w_gate_ffn", "w_up_ffn")
_BIG_LAYOUT = {n: (shp, ax) for n, shp, ax in BIG}
PACK_COLS = 1024


def _stored(name, shard):
    return shard.T if name in TRANSPOSED else shard


def _size(shape):
    n = 1
    for d in shape:
        n *= d
    return n


def _pack_shards(items, shards):
    return jnp.concatenate([shards[it].reshape(-1, PACK_COLS) for it in items], axis=0)


def _unpack_gathered(items, g):
    out = {}
    r0 = 0
    for it in items:
        shp, ax = _BIG_LAYOUT[it[0]]
        rows = _size(shp) // PACK_COLS
        blk = g[:, r0:r0 + rows].reshape((N_DEV,) + shp)
        r0 += rows
        blk = jnp.moveaxis(blk, 0, ax)
        full = list(shp)
        full[ax] = shp[ax] * N_DEV
        out[it] = blk.reshape(full)
    return out


def _pack_full(items, full):
    parts = []
    for it in items:
        shp, ax = _BIG_LAYOUT[it[0]]
        t = full[it].reshape(shp[:ax] + (N_DEV, shp[ax]) + shp[ax + 1:])
        t = jnp.moveaxis(t, ax, 0)
        parts.append(t.reshape(N_DEV, -1, PACK_COLS))
    return jnp.concatenate(parts, axis=1)


def _unpack_shard(items, flat):
    out = {}
    r0 = 0
    for it in items:
        shp, _ = _BIG_LAYOUT[it[0]]
        rows = _size(shp) // PACK_COLS
        out[it] = flat[r0:r0 + rows].reshape(shp)
        r0 += rows
    return out


SMALL = (
    ("norm_mix_g", (DEPTH, D_MODEL)),
    ("sgu_ln_g", (DEPTH, BRANCH_W)),
    ("sgu_ln_b", (DEPTH, BRANCH_W)),
    ("w_spatial", (DEPTH, SGU_GROUPS, SGU_LEN, SGU_LEN)),
    ("b_spatial", (DEPTH, SGU_GROUPS, SGU_LEN)),
    ("conv_w", (DEPTH, 3, BRANCH_W)),
    ("norm_xa_g", (DEPTH, D_MODEL)),
    ("mem_norm_g", (DEPTH, D_MODEL)),
    ("norm_ffn_g", (DEPTH, D_MODEL)),
    ("final_g", (D_MODEL,)),
)


def _pack_small(grads):
    flat = jnp.concatenate([grads[n].reshape(-1) for n, _ in SMALL])
    rows = -(-flat.shape[0] // PACK_COLS)
    rows = -(-rows // 8) * 8
    flat = jnp.pad(flat, (0, rows * PACK_COLS - flat.shape[0]))
    return flat.reshape(rows, PACK_COLS)


def _unpack_small(buf):
    flat = buf.reshape(-1)
    out = {}
    o = 0
    for n, shp in SMALL:
        out[n] = flat[o:o + _size(shp)].reshape(shp)
        o += _size(shp)
    return out


def _layer_fwd(l, x, mem, wt, sm, gather=None):
    t = f"l{l}_"
    sv = {"x0": x}
    h = _rms_fwd(x, sm["norm_mix_g"][l][None], name=t + "rms_mix")
    p = _mm(h, wt["w_in", l], tb=True, name=t + "in_proj", tm=2048)
    if gather is None:
        ya = _sb_fwd(p, name=t + "sb_fwd")
    else:
        ya, gathered = _sb_fwd(p, name=t + "sb_fwd", gather=gather[1])
        wt.update(_unpack_gathered(gather[0], gathered))
    w_sp = sm["w_spatial"][l]
    b_col = sm["b_spatial"][l][:, :, None]
    ln_g, ln_b = sm["sgu_ln_g"][l][None], sm["sgu_ln_b"][l][None]
    yb = _sgu_fwd(p, ln_g, ln_b, w_sp, b_col, name=t + "sgu_fwd")
    yc = _conv_fwd(p, sm["conv_w"][l], name=t + "conv_fwd")
    merged = _merge_fwd(ya, yb, yc, wt["w_branch", l], p, name=t + "merge_fwd")
    x1 = _mm(merged, wt["w_out", l], add=x, name=t + "out_proj")
    sv.update(h=h, p=p, ya=ya, yb=yb, yc=yc, merged=merged, x1=x1)

    h2 = _rms_fwd(x1, sm["norm_xa_g"][l][None], name=t + "rms_xa")
    mn = _rms_fwd(mem, sm["mem_norm_g"][l][None], name=t + "rms_mem")
    q = _mm(h2, wt["w_q_xa", l], out_dtype=BF16, name=t + "xa_q", tm=2048)
    k = _mm(mn, wt["w_k_xa", l], out_dtype=BF16, name=t + "xa_k")
    v = _mm(mn, wt["w_v_xa", l], out_dtype=BF16, name=t + "xa_v")
    o = _xa_fwd(q, k, v, name=t + "xa_fwd")
    x2 = _mm(o, wt["w_o_xa", l], add=x1, name=t + "xa_o")
    sv.update(h2=h2, mn=mn, q=q, k=k, v=v, o=o, x2=x2)

    h3 = _rms_fwd(x2, sm["norm_ffn_g"][l][None], name=t + "rms_ffn")
    w_gu = jnp.concatenate([wt["w_gate_ffn", l], wt["w_up_ffn", l]], axis=0)
    ab = _mm(h3, w_gu, tb=True, name=t + "ffn_gate_up", tm=2048, tn=1408)
    hd = _swiglu_fwd(ab, name=t + "swiglu_fwd")
    x3 = _mm(hd, wt["w_down_ffn", l], add=x2, name=t + "ffn_down", tk=FFN)
    sv.update(h3=h3, ab=ab, hd=hd, w_gu=w_gu)
    return x3, sv


def _layer_bwd(l, dx3, mem, wt, sm, sv, scatter=None):
    t = f"l{l}_b_"
    gb, gs = {}, {}
    dhd = _mm(dx3, wt["w_down_ffn", l], tb=True, name=t + "ffn_down_dx", tn=1408)
    gb["w_down_ffn"] = _mm(sv["hd"], dx3, ta=True, out_dtype=BF16, name=t + "ffn_down_dw", tm=1408)
    dab = _swiglu_bwd(sv["ab"], dhd, name=t + "swiglu_bwd")
    dw_gu = _mm(dab, sv["h3"], ta=True, out_dtype=BF16, name=t + "ffn_gate_up_dw", tm=1408)
    gb["w_gate_ffn"], gb["w_up_ffn"] = dw_gu[:FFN], dw_gu[FFN:]
    dx2, dg = _mm(dab, sv["w_gu"], rms=(sv["x2"], sm["norm_ffn_g"][l][None], dx3),
                  name=t + "ffn_gate_up_dx", tm=512, tk=1408)
    gs["norm_ffn_g"] = dg[0]
    do = _mm(dx2, wt["w_o_xa", l], tb=True, out_dtype=BF16, name=t + "xa_o_dx")
    gb["w_o_xa"] = _mm(sv["o"], dx2, ta=True, out_dtype=BF16, name=t + "xa_o_dw")
    dq, dk, dv = _xa_bwd(sv["q"], sv["k"], sv["v"], do, name=t + "xa_bwd")
    dx1, dg = _mm(dq, wt["w_q_xa", l], tb=True, rms=(sv["x1"], sm["norm_xa_g"][l][None], dx2),
                  name=t + "xa_q_dx", tm=512)
    gs["norm_xa_g"] = dg[0]
    gb["w_q_xa"] = _mm(sv["h2"], dq, ta=True, out_dtype=BF16, name=t + "xa_q_dw")
    gb["w_k_xa"] = _mm(sv["mn"], dk, ta=True, out_dtype=BF16, name=t + "xa_k_dw")
    gb["w_v_xa"] = _mm(sv["mn"], dv, ta=True, out_dtype=BF16, name=t + "xa_v_dw")
    dmn = _mm(dk, wt["w_k_xa", l], tb=True, name=t + "xa_k_dx")
    dmn = _mm(dv, wt["w_v_xa", l], tb=True, add=dmn, name=t + "xa_v_dx")
    _, dg = _rms_bwd(mem, sm["mem_norm_g"][l][None], dmn, jnp.zeros_like(mem), name=t + "rms_mem")
    gs["mem_norm_g"] = dg[0]
    dm = _mm(dx1, wt["w_out", l], tb=True, name=t + "out_proj_dx")
    gb["w_out"] = _mm(sv["merged"], dx1, ta=True, out_dtype=BF16, name=t + "out_proj_dw")
    p = sv["p"]
    dya, dyb, dyc, dgates, dbrd = _merge_bwd(dm, sv["ya"], sv["yb"], sv["yc"], wt["w_branch", l], p,
                                             name=t + "merge_bwd")
    gb["w_branch"] = jnp.stack([
        _mm(sv[y], dbrd[n], ta=True, out_dtype=BF16, name=t + f"branch{n}_dw")
        for n, y in enumerate(("ya", "yb", "yc"))])
    dcb, dcc, dcx, dcw = _conv_bwd(p, dyc, sm["conv_w"][l], name=t + "conv_bwd")
    gs["conv_w"] = dcw
    w_sp = sm["w_spatial"][l]
    dz, dlg, dlb, dwsp, dbsp = _sgu_bwd(p, dyb, sm["sgu_ln_g"][l][None], sm["sgu_ln_b"][l][None], w_sp,
                                        jnp.swapaxes(w_sp, 1, 2), sm["b_spatial"][l][:, :, None],
                                        name=t + "sgu_bwd")
    gs.update(sgu_ln_g=dlg[0], sgu_ln_b=dlb[0], w_spatial=dwsp, b_spatial=dbsp[:, :, 0])
    received = None
    if scatter is None:
        dq_a, dk_a, dv_a = _sb_bwd(p, dya, name=t + "sb_bwd")
    else:
        items, earlier = scatter
        ready = {**earlier, **{(n, l): g for n, g in gb.items()}}
        dq_a, dk_a, dv_a, received = _sb_bwd(p, dya, name=t + "sb_bwd", scatter=_pack_full(items, ready))
    dp = jnp.concatenate([dq_a, dk_a, dv_a, dz, dcb, dcc, dcx, dgates], axis=1)
    gb["w_in"] = _mm(dp, sv["h"], ta=True, out_dtype=BF16, name=t + "in_proj_dw")
    dx, dg = _mm(dp, wt["w_in", l], rms=(sv["x0"], sm["norm_mix_g"][l][None], dx1),
                 name=t + "in_proj_dx", tm=512, tk=1792)
    gs["norm_mix_g"] = dg[0]
    return dx, gb, gs, received


_WEIGHTS = ("norm_mix_g", "w_in", "sgu_ln_g", "sgu_ln_b", "w_spatial", "b_spatial", "conv_w", "w_branch", "w_out",
            "norm_xa_g", "mem_norm_g", "w_q_xa", "w_k_xa", "w_v_xa", "w_o_xa", "norm_ffn_g", "w_gate_ffn",
            "w_up_ffn", "w_down_ffn", "final_g")


def kernel(x, mem, norm_mix_g, w_in, sgu_ln_g, sgu_ln_b, w_spatial, b_spatial, conv_w, w_branch, w_out, norm_xa_g, mem_norm_g, w_q_xa, w_k_xa, w_v_xa, w_o_xa, norm_ffn_g, w_gate_ffn, w_up_ffn, w_down_ffn, final_g, loss_target, m_norm_mix_g, m_w_in, m_sgu_ln_g, m_sgu_ln_b, m_w_spatial, m_b_spatial, m_conv_w, m_w_branch, m_w_out, m_norm_xa_g, m_mem_norm_g, m_w_q_xa, m_w_k_xa, m_w_v_xa, m_w_o_xa, m_norm_ffn_g, m_w_gate_ffn, m_w_up_ffn, m_w_down_ffn, m_final_g, v_norm_mix_g, v_w_in, v_sgu_ln_g, v_sgu_ln_b, v_w_spatial, v_b_spatial, v_conv_w, v_w_branch, v_w_out, v_norm_xa_g, v_mem_norm_g, v_w_q_xa, v_w_k_xa, v_w_v_xa, v_w_o_xa, v_norm_ffn_g, v_w_gate_ffn, v_w_up_ffn, v_w_down_ffn, v_final_g):
    w = dict(norm_mix_g=norm_mix_g, w_in=w_in, sgu_ln_g=sgu_ln_g, sgu_ln_b=sgu_ln_b, w_spatial=w_spatial,
             b_spatial=b_spatial, conv_w=conv_w, w_branch=w_branch, w_out=w_out, norm_xa_g=norm_xa_g,
             mem_norm_g=mem_norm_g, w_q_xa=w_q_xa, w_k_xa=w_k_xa, w_v_xa=w_v_xa, w_o_xa=w_o_xa,
             norm_ffn_g=norm_ffn_g, w_gate_ffn=w_gate_ffn, w_up_ffn=w_up_ffn, w_down_ffn=w_down_ffn, final_g=final_g)
    m = dict(norm_mix_g=m_norm_mix_g, w_in=m_w_in, sgu_ln_g=m_sgu_ln_g, sgu_ln_b=m_sgu_ln_b, w_spatial=m_w_spatial,
             b_spatial=m_b_spatial, conv_w=m_conv_w, w_branch=m_w_branch, w_out=m_w_out, norm_xa_g=m_norm_xa_g,
             mem_norm_g=m_mem_norm_g, w_q_xa=m_w_q_xa, w_k_xa=m_w_k_xa, w_v_xa=m_w_v_xa, w_o_xa=m_w_o_xa,
             norm_ffn_g=m_norm_ffn_g, w_gate_ffn=m_w_gate_ffn, w_up_ffn=m_w_up_ffn, w_down_ffn=m_w_down_ffn,
             final_g=m_final_g)
    v = dict(norm_mix_g=v_norm_mix_g, w_in=v_w_in, sgu_ln_g=v_sgu_ln_g, sgu_ln_b=v_sgu_ln_b, w_spatial=v_w_spatial,
             b_spatial=v_b_spatial, conv_w=v_conv_w, w_branch=v_w_branch, w_out=v_w_out, norm_xa_g=v_norm_xa_g,
             mem_norm_g=v_mem_norm_g, w_q_xa=v_w_q_xa, w_k_xa=v_w_k_xa, w_v_xa=v_w_v_xa, w_o_xa=v_w_o_xa,
             norm_ffn_g=v_norm_ffn_g, w_gate_ffn=v_w_gate_ffn, w_up_ffn=v_w_up_ffn, w_down_ffn=v_w_down_ffn,
             final_g=v_final_g)

    names = [n for n, _, _ in BIG]
    shards = {(n, l): _stored(n, w[n][l].astype(BF16)) for n in names for l in range(DEPTH)}
    first_items = [("w_in", 0)]
    later_items = [(n, l) for l in range(DEPTH) for n in names if (n, l) != ("w_in", 0)]
    wt = _unpack_gathered(first_items, _all_gather(_pack_shards(first_items, shards), name="gather_w_in0"))
    cw_pad = jnp.zeros((8, 128), F32).at[:DEPTH * 3, :BRANCH_W // N_DEV].set(conv_w.reshape(DEPTH * 3, -1))
    cw_all = _all_gather(cw_pad, name="gather_conv_w")[:, :DEPTH * 3, :BRANCH_W // N_DEV]
    conv_full = jnp.moveaxis(cw_all.reshape(N_DEV, DEPTH, 3, BRANCH_W // N_DEV), 0, 2).reshape(DEPTH, 3, BRANCH_W)
    sm = {n: w[n] for n, _ in SMALL}
    sm["conv_w"] = conv_full

    xs, ms = x[0], mem[0]
    x1, saved0 = _layer_fwd(0, xs, ms, wt, sm, gather=(later_items, _pack_shards(later_items, shards)))
    x2, saved1 = _layer_fwd(1, x1, ms, wt, sm)
    dcur, loss, dfinal = _final_loss(x2, sm["final_g"][None], loss_target[0], name="final_loss")
    loss = lax.psum(loss[0, 0], AXES)
    items_a = [(n, 1) for n in names if n != "w_in"]
    items_b = [("w_in", 1)] + [(n, 0) for n in names if n != "w_in"]
    items_c = [("w_in", 0)]
    dcur, gb1, gs1, recv_a = _layer_bwd(1, dcur, ms, wt, sm, saved1, scatter=(items_a, {}))
    dx, gb0, gs0, recv_b = _layer_bwd(0, dcur, ms, wt, sm, saved0, scatter=(items_b, {("w_in", 1): gb1["w_in"]}))

    shard_grads = _unpack_shard(items_a, _sum_devices(recv_a, name="rs_sum_a"))
    shard_grads.update(_unpack_shard(items_b, _sum_devices(recv_b, name="rs_sum_b")))
    g8 = _pack_full(items_c, {("w_in", 0): gb0["w_in"]})
    core = lax.axis_index("c").astype(jnp.int32).reshape(1)
    from_sibling = _rs_pair_exchange(g8, name="rs_pair_exchange")
    part = _pair_add(core, g8, from_sibling, name="rs_pair_add")
    by_chip = _rs_chip_exchange(part, name="rs_chip_exchange")
    shard_grads.update(_unpack_shard(items_c, _sum_chips(by_chip, name="rs_sum_chips")))
    grads = {n: jnp.stack([_stored(n, shard_grads[n, l]) for l in range(DEPTH)]) for n in names}
    small = {n: jnp.stack([gs0[n], gs1[n]]) for n, _ in SMALL if n != "final_g"}
    small["final_g"] = dfinal[0]
    small_sum = _unpack_small(_all_reduce_small(_pack_small(small), name="all_reduce_small"))
    width = BRANCH_W // N_DEV
    dev = 4 * lax.axis_index("x") + 2 * lax.axis_index("y") + lax.axis_index("c")
    for n, _ in SMALL:
        grads[n] = small_sum[n]
    grads["conv_w"] = lax.dynamic_slice_in_dim(small_sum["conv_w"], dev * width, width, axis=2)

    delta, new_m, new_v = {}, {}, {}
    for n in _WEIGHTS:
        shp = w[n].shape
        two_d = (-1, shp[-1])
        d_, m_, v_ = _adamw(w[n].reshape(two_d), grads[n].reshape(two_d), m[n].reshape(two_d), v[n].reshape(two_d),
                            name="adamw_" + n)
        delta[n], new_m[n], new_v[n] = d_.reshape(shp), m_.reshape(shp), v_.reshape(shp)

    return (loss, dx[None], *[grads[n] for n in _WEIGHTS], *[delta[n] for n in _WEIGHTS],
            *[new_m[n] for n in _WEIGHTS], *[new_v[n] for n in _WEIGHTS])
```

```python
import functools

import jax
import jax.numpy as jnp
from jax import lax
from jax.experimental import pallas as pl
from jax.experimental.pallas import tpu as pltpu

F32 = jnp.float32
BF16 = jnp.bfloat16
MESH = pl.DeviceIdType.MESH

D_MODEL = 1024
BRANCH_W = 512
IN_COLS = 7168
FFN = 2816
N_DEV = 8
DEPTH = 2
SB_BLOCK = 128
SB_SPAN = 1024
SB_Q_FWD = 512
SB_Q_BWD = 256
SB_SCALE = 0.125
XA_HEAD = 256
XA_SCALE = 0.0625
SGU_LEN = 128
SGU_GROUPS = 4
RMS_EPS = 1e-6
LN_EPS = 1e-5
HALO = 8

ADAM_LR = 0.001
ADAM_B1 = 0.9
ADAM_B2 = 0.999
ADAM_EPS = 1e-08
ADAM_WD = 0.01
ADAM_STEP = 10

VMEM_LIMIT_BYTES = 52 * 1024 * 1024

AXES = ("x", "y", "c")


def _cparams(*sem):
    return pltpu.CompilerParams(dimension_semantics=sem, vmem_limit_bytes=VMEM_LIMIT_BYTES)


def _pick(n, target, align):
    t = (min(target, n) // align) * align
    while t >= align:
        if n % t == 0:
            return t
        t -= align
    return n


def _dot(a, b):
    return jnp.dot(a, b, preferred_element_type=F32)


def _dot_nt(a, b):
    return lax.dot_general(a, b, (((1,), (1,)), ((), ())), preferred_element_type=F32)


def _dot_tn(a, b):
    return lax.dot_general(a, b, (((0,), (0,)), ((), ())), preferred_element_type=F32)


def _sigmoid(x):
    return 1.0 / (1.0 + jnp.exp(-x))


def _mm(a, b, *, name, ta=False, tb=False, out_dtype=F32, add=None, rms=None, tm=1024, tn=1024, tk=2048):
    m, k = (a.shape[1], a.shape[0]) if ta else a.shape
    n = b.shape[0] if tb else b.shape[1]
    assert k == (b.shape[1] if tb else b.shape[0])
    tm = _pick(m, tm, 128)
    tn = n if rms is not None else _pick(n, tn, 128)
    tk = _pick(k, tk, 128)
    nk = k // tk
    ca = 0 if ta else 1
    cb = 1 if tb else 0
    assert add is None or rms is None

    def body(*refs):
        refs = list(refs)
        a_ref, b_ref = refs[:2]
        extra = refs[2:2 + (1 if add is not None else 3 if rms is not None else 0)]
        outs = refs[2 + len(extra):]
        o_ref = outs[0]
        kk = pl.program_id(2)
        first_row_tile = pl.program_id(0) == 0

        def product():
            return lax.dot_general(a_ref[...].astype(BF16), b_ref[...].astype(BF16),
                                   (((ca,), (cb,)), ((), ())), preferred_element_type=F32)

        def finish(r):
            if add is not None:
                r = r + extra[0][...]
            if rms is None:
                o_ref[...] = r.astype(out_dtype)
                return
            x_ref, g_ref, dres_ref = extra
            dg_ref = outs[1]

            @pl.when(first_row_tile)
            def _():
                dg_ref[...] = jnp.zeros_like(dg_ref)

            xv = x_ref[...]
            rs = lax.rsqrt(jnp.mean(xv * xv, axis=-1, keepdims=True) + RMS_EPS)
            xh = xv * rs
            dg_ref[...] += jnp.sum(r * xh, axis=0, keepdims=True)
            dxh = r * g_ref[...]
            o_ref[...] = dres_ref[...] + rs * (dxh - xh * jnp.mean(dxh * xh, axis=-1, keepdims=True))

        if nk == 1:
            finish(product())
        else:
            acc_ref = outs[-1]

            @pl.when(kk == 0)
            def _():
                acc_ref[...] = jnp.zeros_like(acc_ref)

            acc_ref[...] += product()

            @pl.when(kk == nk - 1)
            def _():
                finish(acc_ref[...])

    a_spec = pl.BlockSpec((tk, tm), lambda i, j, kk: (kk, i)) if ta else pl.BlockSpec((tm, tk), lambda i, j, kk: (i, kk))
    b_spec = pl.BlockSpec((tn, tk), lambda i, j, kk: (j, kk)) if tb else pl.BlockSpec((tk, tn), lambda i, j, kk: (kk, j))
    tile = pl.BlockSpec((tm, tn), lambda i, j, kk: (i, j))
    in_specs = [a_spec, b_spec]
    operands = [a, b]
    out_specs = [tile]
    out_shape = [jax.ShapeDtypeStruct((m, n), out_dtype)]
    if add is not None:
        in_specs.append(tile)
        operands.append(add)
    if rms is not None:
        vec = pl.BlockSpec((1, n), lambda i, j, kk: (0, 0))
        in_specs += [tile, vec, tile]
        operands += list(rms)
        out_specs.append(vec)
        out_shape = [jax.ShapeDtypeStruct((m, n), F32), jax.ShapeDtypeStruct((1, n), F32)]
    out = pl.pallas_call(
        body, name=name,
        grid=(m // tm, n // tn, nk),
        in_specs=in_specs, out_specs=out_specs, out_shape=out_shape,
        scratch_shapes=[pltpu.VMEM((tm, tn), F32)] if nk > 1 else [],
        compiler_params=_cparams("arbitrary" if rms is not None else "parallel", "parallel", "arbitrary"),
    )(*operands)
    return out[0] if rms is None else out


def _rms_fwd(x, g, *, name):
    r, d = x.shape
    tr = _pick(r, 512, 16)

    def body(x_ref, g_ref, o_ref):
        xv = x_ref[...]
        rs = lax.rsqrt(jnp.mean(xv * xv, axis=-1, keepdims=True) + RMS_EPS)
        o_ref[...] = (xv * rs * g_ref[...]).astype(BF16)

    return pl.pallas_call(
        body, name=name, grid=(r // tr,),
        in_specs=[pl.BlockSpec((tr, d), lambda i: (i, 0)), pl.BlockSpec((1, d), lambda i: (0, 0))],
        out_specs=pl.BlockSpec((tr, d), lambda i: (i, 0)),
        out_shape=jax.ShapeDtypeStruct((r, d), BF16),
        compiler_params=_cparams("parallel"),
    )(x, g)


def _rms_bwd(x, g, dh, dres, *, name):
    r, d = x.shape
    tr = _pick(r, 256, 8)

    def body(x_ref, g_ref, dh_ref, dres_ref, dx_ref, dg_ref):
        @pl.when(pl.program_id(0) == 0)
        def _():
            dg_ref[...] = jnp.zeros_like(dg_ref)

        xv = x_ref[...]
        dhv = dh_ref[...].astype(F32)
        rs = lax.rsqrt(jnp.mean(xv * xv, axis=-1, keepdims=True) + RMS_EPS)
        xh = xv * rs
        dg_ref[...] += jnp.sum(dhv * xh, axis=0, keepdims=True)
        dxh = dhv * g_ref[...]
        dx_ref[...] = dres_ref[...] + rs * (dxh - xh * jnp.mean(dxh * xh, axis=-1, keepdims=True))

    return pl.pallas_call(
        body, name=name, grid=(r // tr,),
        in_specs=[pl.BlockSpec((tr, d), lambda i: (i, 0)), pl.BlockSpec((1, d), lambda i: (0, 0)),
                  pl.BlockSpec((tr, d), lambda i: (i, 0)), pl.BlockSpec((tr, d), lambda i: (i, 0))],
        out_specs=[pl.BlockSpec((tr, d), lambda i: (i, 0)), pl.BlockSpec((1, d), lambda i: (0, 0))],
        out_shape=[jax.ShapeDtypeStruct((r, d), F32), jax.ShapeDtypeStruct((1, d), F32)],
        compiler_params=_cparams("arbitrary"),
    )(x, g, dh, dres)


def _final_loss(x, g, target, *, name):
    r, d = x.shape
    tr = _pick(r, 256, 8)

    def body(x_ref, g_ref, t_ref, dx_ref, loss_ref, dg_ref):
        @pl.when(pl.program_id(0) == 0)
        def _():
            dg_ref[...] = jnp.zeros_like(dg_ref)
            loss_ref[...] = jnp.zeros_like(loss_ref)

        xv = x_ref[...]
        gv = g_ref[...]
        rs = lax.rsqrt(jnp.mean(xv * xv, axis=-1, keepdims=True) + RMS_EPS)
        xh = xv * rs
        err = xh * gv - t_ref[...]
        row_loss = jnp.mean(err * err, axis=-1, keepdims=True)
        loss_ref[...] += 0.5 * jnp.sum(row_loss, axis=0, keepdims=True)
        dy = err * (1.0 / d)
        dg_ref[...] += jnp.sum(dy * xh, axis=0, keepdims=True)
        dxh = dy * gv
        dx_ref[...] = rs * (dxh - xh * jnp.mean(dxh * xh, axis=-1, keepdims=True))

    return pl.pallas_call(
        body, name=name, grid=(r // tr,),
        in_specs=[pl.BlockSpec((tr, d), lambda i: (i, 0)), pl.BlockSpec((1, d), lambda i: (0, 0)),
                  pl.BlockSpec((tr, d), lambda i: (i, 0))],
        out_specs=[pl.BlockSpec((tr, d), lambda i: (i, 0)), pl.BlockSpec((1, 128), lambda i: (0, 0)),
                   pl.BlockSpec((1, d), lambda i: (0, 0))],
        out_shape=[jax.ShapeDtypeStruct((r, d), F32), jax.ShapeDtypeStruct((1, 128), F32),
                   jax.ShapeDtypeStruct((1, d), F32)],
        compiler_params=_cparams("arbitrary"),
    )(x, g, target)


def _cumsum_operand(strict_after, terms=1):
    r = lax.broadcasted_iota(jnp.int32, (terms * SB_BLOCK, 2 * SB_BLOCK), 0) % SB_BLOCK
    c = lax.broadcasted_iota(jnp.int32, (terms * SB_BLOCK, 2 * SB_BLOCK), 1)
    tri = (r > c) if strict_after else (r < c)
    return jnp.where((c >= SB_BLOCK) | tri, 1.0, 0.0).astype(BF16)


def _sb_scores(qh, kw, run, valid, after_ones):
    nb = kw.shape[0] // SB_BLOCK
    z = _dot_nt(qh, kw)
    lsp = jnp.minimum(z, 0.0) - jnp.log(1.0 + jnp.exp(-jnp.abs(z)))
    l1m = lsp - z
    if valid is not None:
        l1m = jnp.where(valid, l1m, 0.0)
    l1b = l1m.astype(BF16)
    later = [None] * nb
    for b in reversed(range(nb)):
        cols = slice(b * SB_BLOCK, (b + 1) * SB_BLOCK)
        ct = _dot(l1b[:, cols], after_ones)
        later[b] = run + ct[:, :SB_BLOCK]
        run = run + ct[:, SB_BLOCK:]
    a = jnp.exp(lsp + jnp.concatenate(later, axis=1))
    if valid is not None:
        a = jnp.where(valid, a, 0.0)
    return lsp, a, run


def _sb_setup(q_ref, span):
    qi = pl.program_id(1)
    rows = q_ref.shape[0]
    sd = (qi * rows + rows - 1) // span
    lane = lax.broadcasted_iota(jnp.int32, (rows, SB_BLOCK), 1)
    col = lax.broadcasted_iota(jnp.int32, (rows, span), 1)
    row = lax.broadcasted_iota(jnp.int32, (rows, span), 0)
    valid = col < (qi * rows - sd * span) + row
    q = q_ref[...] * SB_SCALE
    qhs = (jnp.where(lane < 64, q, 0.0).astype(BF16), jnp.where(lane >= 64, q, 0.0).astype(BF16))
    return lane, sd, valid, qhs


def _sb_fwd(p, *, name, gather=None):
    s = p.shape[0]
    qrows = min(SB_Q_FWD, s)
    nq = s // qrows
    kcol = BRANCH_W // SB_BLOCK
    span = min(SB_SPAN, s)

    def body(*refs):
        if gather is None:
            q_ref, k_ref, v_ref, o_ref = refs
        else:
            q_ref, k_ref, v_ref, x_ref, o_ref, g_ref, send_sems, recv_sems, local_sem = refs
            start, forward, finish = _gather_phases(x_ref, g_ref, send_sems, recv_sems, local_sem)
            step = pl.program_id(0) * nq + pl.program_id(1)
            pl.when(step == 0)(start)
        lane, sd, valid, qhs = _sb_setup(q_ref, span)
        after_ones = _cumsum_operand(True)
        zero = jnp.zeros((qrows, SB_BLOCK), F32)

        def span_step(sb, carry, mask):
            rows = pl.ds(pl.multiple_of(sb * span, span), span)
            kw = k_ref[rows, :].astype(BF16)
            vw = v_ref[rows, :].astype(BF16)
            out = []
            for h in range(2):
                run, acc = carry[h]
                _, a, run = _sb_scores(qhs[h], kw, run, mask, after_ones)
                out.append((run, acc + _dot(a.astype(BF16), vw)))
            return tuple(out)

        carry = span_step(sd, ((zero, zero), (zero, zero)), valid)
        carry = lax.fori_loop(0, sd, lambda t, c: span_step(sd - 1 - t, c, None), carry)
        o_ref[...] = jnp.where(lane < 64, carry[0][1], carry[1][1])
        if gather is not None:
            pl.when(step == (kcol - 1) * nq + (3 * nq) // 4)(forward)
            pl.when(step == kcol * nq - 1)(finish)

    in_specs = [pl.BlockSpec((qrows, SB_BLOCK), lambda hp, qi: (qi, hp)),
                pl.BlockSpec((s, SB_BLOCK), lambda hp, qi: (0, kcol + hp)),
                pl.BlockSpec((s, SB_BLOCK), lambda hp, qi: (0, 2 * kcol + hp))]
    out_specs = [pl.BlockSpec((qrows, SB_BLOCK), lambda hp, qi: (qi, hp))]
    out_shape = [jax.ShapeDtypeStruct((s, BRANCH_W), F32)]
    operands = [p, p, p]
    scratch = []
    if gather is not None:
        in_specs.append(pl.BlockSpec(memory_space=pl.ANY))
        out_specs.append(pl.BlockSpec(memory_space=pl.ANY))
        out_shape.append(jax.ShapeDtypeStruct((N_DEV,) + gather.shape, gather.dtype))
        operands.append(gather)
        scratch = _GATHER_SEMS
    out = pl.pallas_call(
        body, name=name, grid=(kcol, nq), in_specs=in_specs, out_specs=out_specs, out_shape=out_shape,
        scratch_shapes=scratch, compiler_params=_cparams("arbitrary", "arbitrary"),
    )(*operands)
    return out[0] if gather is None else out


def _sb_bwd(p, dya, *, name, scatter=None):
    s = p.shape[0]
    qrows = min(SB_Q_BWD, s)
    nq = s // qrows
    kcol = BRANCH_W // SB_BLOCK
    span = min(SB_SPAN, s)
    per = span // SB_BLOCK

    def body(*refs):
        if scatter is None:
            q_ref, k_ref, v_ref, do_ref, dq_ref, dk_ref, dv_ref, a_s, b_s, dk_acc, dv_acc = refs
        else:
            (q_ref, k_ref, v_ref, do_ref, g_ref, dq_ref, dk_ref, dv_ref, r_ref,
             a_s, b_s, dk_acc, dv_acc, send_sems, recv_sems, local_sem) = refs
            start, finish = _scatter_phases(g_ref, r_ref, send_sems, recv_sems, local_sem)
            step = pl.program_id(0) * nq + pl.program_id(1)
            pl.when(step == 0)(start)
        qi = pl.program_id(1)

        @pl.when(qi == 0)
        def _():
            dk_acc[...] = jnp.zeros_like(dk_acc)
            dv_acc[...] = jnp.zeros_like(dv_acc)

        lane, sd, valid, qhs = _sb_setup(q_ref, span)
        after_ones = _cumsum_operand(True)
        before_ones = _cumsum_operand(False)
        do = do_ref[...]
        dohs = (jnp.where(lane < 64, do, 0.0).astype(BF16), jnp.where(lane >= 64, do, 0.0).astype(BF16))
        zero = jnp.zeros((qrows, SB_BLOCK), F32)

        def rebuild(sb, runs, mask):
            rows = pl.ds(pl.multiple_of(sb * span, span), span)
            kw = k_ref[rows, :].astype(BF16)
            out = []
            for h in range(2):
                lsp, a, run = _sb_scores(qhs[h], kw, runs[h], mask, after_ones)
                beta = jnp.exp(lsp)
                if mask is not None:
                    beta = jnp.where(mask, beta, 0.0)
                a_s[h, sb] = a
                b_s[h, sb] = beta
                out.append(run)
            return tuple(out)

        runs = rebuild(sd, (zero, zero), valid)
        lax.fori_loop(0, sd, lambda t, r: rebuild(sd - 1 - t, r, None), runs)

        def accumulate(sb, carry):
            rows = pl.ds(pl.multiple_of(sb * span, span), span)
            kw = k_ref[rows, :].astype(BF16)
            vw = v_ref[rows, :].astype(BF16)
            out = []
            dk_span = jnp.zeros((span, SB_BLOCK), F32)
            dv_span = jnp.zeros((span, SB_BLOCK), F32)
            for h in range(2):
                pg, dq = carry[h]
                a = a_s[h, sb]
                beta = b_s[h, sb]
                g = a * _dot_nt(dohs[h], vw)
                gb = g.astype(BF16)
                before = [None] * per
                for b in range(per):
                    cols = slice(b * SB_BLOCK, (b + 1) * SB_BLOCK)
                    gt = _dot(gb[:, cols], before_ones)
                    before[b] = pg + gt[:, :SB_BLOCK]
                    pg = pg + gt[:, SB_BLOCK:]
                dz = (g * (1.0 - beta) - beta * jnp.concatenate(before, axis=1)).astype(BF16)
                dk_span = dk_span + _dot_tn(dz, qhs[h])
                dv_span = dv_span + _dot_tn(a.astype(BF16), dohs[h])
                out.append((pg, dq + _dot(dz, kw)))
            dk_acc[rows, :] += dk_span
            dv_acc[rows, :] += dv_span
            return tuple(out)

        carry = lax.fori_loop(0, sd + 1, accumulate, ((zero, zero), (zero, zero)))
        dq_ref[...] = (jnp.where(lane < 64, carry[0][1], carry[1][1]) * SB_SCALE).astype(BF16)

        @pl.when(qi == nq - 1)
        def _():
            dk_ref[...] = dk_acc[...].astype(BF16)
            dv_ref[...] = dv_acc[...].astype(BF16)

        if scatter is not None:
            pl.when(step == kcol * nq - 1)(finish)

    blk = pl.BlockSpec((qrows, SB_BLOCK), lambda hp, qi: (qi, hp))
    col = pl.BlockSpec((s, SB_BLOCK), lambda hp, qi: (0, hp))
    out = jax.ShapeDtypeStruct((s, BRANCH_W), BF16)
    in_specs = [blk,
                pl.BlockSpec((s, SB_BLOCK), lambda hp, qi: (0, kcol + hp)),
                pl.BlockSpec((s, SB_BLOCK), lambda hp, qi: (0, 2 * kcol + hp)),
                blk]
    out_specs = [blk, col, col]
    out_shape = [out, out, out]
    operands = [p, p, p, dya]
    scratch = [pltpu.VMEM((2, s // span, qrows, span), F32), pltpu.VMEM((2, s // span, qrows, span), F32),
               pltpu.VMEM((s, SB_BLOCK), F32), pltpu.VMEM((s, SB_BLOCK), F32)]
    if scatter is not None:
        in_specs.append(pl.BlockSpec(memory_space=pl.ANY))
        out_specs.append(pl.BlockSpec(memory_space=pl.ANY))
        out_shape.append(jax.ShapeDtypeStruct(scatter.shape, scatter.dtype))
        operands.append(scatter)
        scratch = scratch + _SCATTER_SEMS
    return pl.pallas_call(
        body, name=name, grid=(kcol, nq), in_specs=in_specs, out_specs=out_specs, out_shape=out_shape,
        scratch_shapes=scratch, compiler_params=_cparams("arbitrary", "arbitrary"),
    )(*operands)


_INV_SQRT2 = 0.7071067811865476
_INV_SQRT2PI = 0.3989422804014327


def _gelu(x):
    return 0.5 * x * (1.0 + lax.erf(x * _INV_SQRT2))


def _gelu_grad(x):
    return 0.5 * (1.0 + lax.erf(x * _INV_SQRT2)) + x * _INV_SQRT2PI * jnp.exp(-0.5 * x * x)


def _chunk_mask(transposed=False):
    r = lax.broadcasted_iota(jnp.int32, (SGU_LEN, SGU_LEN), 0)
    c = lax.broadcasted_iota(jnp.int32, (SGU_LEN, SGU_LEN), 1)
    return (c // 64) >= (r // 64) if transposed else (r // 64) >= (c // 64)


def _sgu_norm(v_raw, g, b):
    zv = _gelu(v_raw)
    xc = zv - jnp.mean(zv, axis=-1, keepdims=True)
    rs = lax.rsqrt(jnp.mean(xc * xc, axis=-1, keepdims=True) + LN_EPS)
    xh = xc * rs
    return xh, rs, xh * g + b


def _sgu_fwd(p, ln_g, ln_b, w, b_col, *, name):
    s = p.shape[0]
    tr = _pick(s, 512, SGU_LEN)

    def body(u_ref, v_ref, g_ref, b_ref, w_ref, bc_ref, o_ref):
        mask = _chunk_mask()
        zu = _gelu(u_ref[...])
        _, _, vn = _sgu_norm(v_ref[...], g_ref[...], b_ref[...])
        vnb = vn.astype(BF16)
        for gi in range(SGU_GROUPS):
            wg = jnp.where(mask, w_ref[gi], 0.0).astype(BF16)
            cs = slice(gi * SGU_LEN, (gi + 1) * SGU_LEN)
            for c in range(tr // SGU_LEN):
                rs_ = slice(c * SGU_LEN, (c + 1) * SGU_LEN)
                vm = _dot(wg, vnb[rs_, cs]) + bc_ref[gi]
                o_ref[rs_, cs] = zu[rs_, cs] * vm

    vec = pl.BlockSpec((1, BRANCH_W), lambda i: (0, 0))
    return pl.pallas_call(
        body, name=name, grid=(s // tr,),
        in_specs=[pl.BlockSpec((tr, BRANCH_W), lambda i: (i, 3)), pl.BlockSpec((tr, BRANCH_W), lambda i: (i, 4)),
                  vec, vec,
                  pl.BlockSpec((SGU_GROUPS, SGU_LEN, SGU_LEN), lambda i: (0, 0, 0)),
                  pl.BlockSpec((SGU_GROUPS, SGU_LEN, 1), lambda i: (0, 0, 0))],
        out_specs=pl.BlockSpec((tr, BRANCH_W), lambda i: (i, 0)),
        out_shape=jax.ShapeDtypeStruct((s, BRANCH_W), F32),
        compiler_params=_cparams("parallel"),
    )(p, p, ln_g, ln_b, w, b_col)


def _sgu_bwd(p, dyb, ln_g, ln_b, w, w_t, b_col, *, name):
    s = p.shape[0]
    tr = _pick(s, 256, SGU_LEN)

    def body(u_ref, v_ref, dy_ref, g_ref, b_ref, w_ref, wt_ref, bc_ref,
             dz_ref, dg_ref, db_ref, dw_ref, dbc_ref, dvn_s):
        @pl.when(pl.program_id(0) == 0)
        def _():
            dg_ref[...] = jnp.zeros_like(dg_ref)
            db_ref[...] = jnp.zeros_like(db_ref)
            dw_ref[...] = jnp.zeros_like(dw_ref)
            dbc_ref[...] = jnp.zeros_like(dbc_ref)

        mask = _chunk_mask()
        mask_t = _chunk_mask(transposed=True)
        u_raw = u_ref[...]
        v_raw = v_ref[...]
        dy = dy_ref[...]
        zu = _gelu(u_raw)
        xh, rs, vn = _sgu_norm(v_raw, g_ref[...], b_ref[...])
        vnb = vn.astype(BF16)
        dvm_all = dy * zu
        for gi in range(SGU_GROUPS):
            wg = jnp.where(mask, w_ref[gi], 0.0).astype(BF16)
            wgt = jnp.where(mask_t, wt_ref[gi], 0.0).astype(BF16)
            cs = slice(gi * SGU_LEN, (gi + 1) * SGU_LEN)
            dw_g = jnp.zeros((SGU_LEN, SGU_LEN), F32)
            db_g = jnp.zeros((SGU_LEN, 1), F32)
            for c in range(tr // SGU_LEN):
                rs_ = slice(c * SGU_LEN, (c + 1) * SGU_LEN)
                vm = _dot(wg, vnb[rs_, cs]) + bc_ref[gi]
                dz_ref[rs_, cs] = (dy[rs_, cs] * vm * _gelu_grad(u_raw[rs_, cs])).astype(BF16)
                dvm = dvm_all[rs_, cs]
                dvmb = dvm.astype(BF16)
                dw_g = dw_g + _dot_nt(dvmb, vnb[rs_, cs])
                db_g = db_g + jnp.sum(dvm, axis=1, keepdims=True)
                dvn_s[rs_, cs] = _dot(wgt, dvmb)
            dw_ref[gi] += jnp.where(mask, dw_g, 0.0)
            dbc_ref[gi] += db_g
        dvn = dvn_s[...]
        dg_ref[...] += jnp.sum(dvn * xh, axis=0, keepdims=True)
        db_ref[...] += jnp.sum(dvn, axis=0, keepdims=True)
        dxh = dvn * g_ref[...]
        dzv = rs * (dxh - jnp.mean(dxh, axis=-1, keepdims=True) - xh * jnp.mean(dxh * xh, axis=-1, keepdims=True))
        dz_ref[:, BRANCH_W:] = (dzv * _gelu_grad(v_raw)).astype(BF16)

    vec = pl.BlockSpec((1, BRANCH_W), lambda i: (0, 0))
    wspec = pl.BlockSpec((SGU_GROUPS, SGU_LEN, SGU_LEN), lambda i: (0, 0, 0))
    bspec = pl.BlockSpec((SGU_GROUPS, SGU_LEN, 1), lambda i: (0, 0, 0))
    return pl.pallas_call(
        body, name=name, grid=(s // tr,),
        in_specs=[pl.BlockSpec((tr, BRANCH_W), lambda i: (i, 3)), pl.BlockSpec((tr, BRANCH_W), lambda i: (i, 4)),
                  pl.BlockSpec((tr, BRANCH_W), lambda i: (i, 0)), vec, vec, wspec, wspec, bspec],
        out_specs=[pl.BlockSpec((tr, 2 * BRANCH_W), lambda i: (i, 0)), vec, vec, wspec, bspec],
        out_shape=[jax.ShapeDtypeStruct((s, 2 * BRANCH_W), BF16),
                   jax.ShapeDtypeStruct((1, BRANCH_W), F32), jax.ShapeDtypeStruct((1, BRANCH_W), F32),
                   jax.ShapeDtypeStruct((SGU_GROUPS, SGU_LEN, SGU_LEN), F32),
                   jax.ShapeDtypeStruct((SGU_GROUPS, SGU_LEN, 1), F32)],
        scratch_shapes=[pltpu.VMEM((tr, BRANCH_W), F32)],
        compiler_params=_cparams("arbitrary"),
    )(p, p, dyb, ln_g, ln_b, w, w_t, b_col)


def _shift_down(x, prev8, k):
    rolled = pltpu.roll(x, k, 0)
    r8 = lax.broadcasted_iota(jnp.int32, prev8.shape, 0)
    head = jnp.where(r8 < k, pltpu.roll(prev8, k, 0), rolled[:HALO])
    return jnp.concatenate([head, rolled[HALO:]], axis=0)


def _shift_up(x, next8, k):
    n = x.shape[0]
    rolled = pltpu.roll(x, n - k, 0)
    r8 = lax.broadcasted_iota(jnp.int32, next8.shape, 0)
    tail = jnp.where(r8 >= HALO - k, pltpu.roll(next8, HALO - k, 0), rolled[n - HALO:])
    return jnp.concatenate([rolled[:n - HALO], tail], axis=0)


def _conv_specs(s, tr):
    nb = tr // HALO
    last = s // HALO - 1
    tile = lambda cb: pl.BlockSpec((tr, 128), lambda j, i: (i, cb * 4 + j))
    above = lambda cb: pl.BlockSpec((HALO, 128), lambda j, i: (jnp.maximum(i * nb - 1, 0), cb * 4 + j))
    below = lambda cb: pl.BlockSpec((HALO, 128), lambda j, i: (jnp.minimum((i + 1) * nb, last), cb * 4 + j))
    return tile, above, below


def _conv_fwd(p, cw, *, name):
    s = p.shape[0]
    tr = _pick(s, 512, HALO)
    tile, above, _ = _conv_specs(s, tr)

    def body(cb_ref, cc_ref, cx_ref, ccp_ref, cxp_ref, w_ref, o_ref):
        first = pl.program_id(1) == 0
        y = cc_ref[...] * cx_ref[...]
        yp = jnp.where(first, 0.0, ccp_ref[...] * cxp_ref[...])
        conv = w_ref[2:3, :] * y + w_ref[1:2, :] * _shift_down(y, yp, 1) + w_ref[0:1, :] * _shift_down(y, yp, 2)
        o_ref[...] = cb_ref[...] * conv

    return pl.pallas_call(
        body, name=name, grid=(4, s // tr),
        in_specs=[tile(5), tile(6), tile(7), above(6), above(7), pl.BlockSpec((3, 128), lambda j, i: (0, j))],
        out_specs=pl.BlockSpec((tr, 128), lambda j, i: (i, j)),
        out_shape=jax.ShapeDtypeStruct((s, BRANCH_W), F32),
        compiler_params=_cparams("parallel", "parallel"),
    )(p, p, p, p, p, cw)


def _conv_bwd(p, dyc, cw, *, name):
    s = p.shape[0]
    tr = _pick(s, 512, HALO)
    nt = s // tr
    nb = tr // HALO
    last = s // HALO - 1
    tile, above, below = _conv_specs(s, tr)

    def body(cb_ref, cc_ref, cx_ref, ccp_ref, cxp_ref, cbn_ref, dy_ref, dyn_ref, w_ref,
             dcb_ref, dcc_ref, dcx_ref, dw_ref):
        i = pl.program_id(1)

        @pl.when(i == 0)
        def _():
            dw_ref[...] = jnp.zeros_like(dw_ref)

        cb = cb_ref[...]
        cc = cc_ref[...]
        cx = cx_ref[...]
        y = cc * cx
        yp = jnp.where(i == 0, 0.0, ccp_ref[...] * cxp_ref[...])
        y1 = _shift_down(y, yp, 1)
        y2 = _shift_down(y, yp, 2)
        w0, w1, w2 = w_ref[0:1, :], w_ref[1:2, :], w_ref[2:3, :]
        conv = w2 * y + w1 * y1 + w0 * y2
        dyc_v = dy_ref[...]
        dconv = dyc_v * cb
        dn = jnp.where(i == nt - 1, 0.0, dyn_ref[...] * cbn_ref[...])
        dyv = w2 * dconv + w1 * _shift_up(dconv, dn, 1) + w0 * _shift_up(dconv, dn, 2)
        dcb_ref[...] = (dyc_v * conv).astype(BF16)
        dcc_ref[...] = (dyv * cx).astype(BF16)
        dcx_ref[...] = (dyv * cc).astype(BF16)
        dw_ref[0:1, :] += jnp.sum(dconv * y2, axis=0, keepdims=True)
        dw_ref[1:2, :] += jnp.sum(dconv * y1, axis=0, keepdims=True)
        dw_ref[2:3, :] += jnp.sum(dconv * y, axis=0, keepdims=True)

    dy_tile = pl.BlockSpec((tr, 128), lambda j, i: (i, j))
    dy_below = pl.BlockSpec((HALO, 128), lambda j, i: (jnp.minimum((i + 1) * nb, last), j))
    out_tile = lambda cb: pl.BlockSpec((tr, 128), lambda j, i: (i, cb * 4 + j))
    w_spec = pl.BlockSpec((3, 128), lambda j, i: (0, j))
    dcb, dcc, dcx, dw = pl.pallas_call(
        body, name=name, grid=(4, nt),
        in_specs=[tile(5), tile(6), tile(7), above(6), above(7), below(5), dy_tile, dy_below, w_spec],
        out_specs=[dy_tile, dy_tile, dy_tile, w_spec],
        out_shape=[jax.ShapeDtypeStruct((s, BRANCH_W), BF16)] * 3 + [jax.ShapeDtypeStruct((3, BRANCH_W), F32)],
        compiler_params=_cparams("parallel", "arbitrary"),
    )(p, p, p, p, p, p, dyc, dyc, cw)
    return dcb, dcc, dcx, dw


def _merge_fwd(ya, yb, yc, wb, p, *, name):
    s = p.shape[0]
    tr = _pick(s, 256, 16)

    def body(ya_ref, yb_ref, yc_ref, wb_ref, g0_ref, g1_ref, g2_ref, o_ref):
        acc = jnp.zeros((tr, D_MODEL), F32)
        for n, (y_ref, g_ref) in enumerate(((ya_ref, g0_ref), (yb_ref, g1_ref), (yc_ref, g2_ref))):
            acc = acc + _sigmoid(g_ref[...]) * _dot(y_ref[...].astype(BF16), wb_ref[n])
        o_ref[...] = acc.astype(BF16)

    yspec = pl.BlockSpec((tr, BRANCH_W), lambda i: (i, 0))
    gate = lambda n: pl.BlockSpec((tr, D_MODEL), lambda i: (i, 4 + n))
    return pl.pallas_call(
        body, name=name, grid=(s // tr,),
        in_specs=[yspec, yspec, yspec, pl.BlockSpec((3, BRANCH_W, D_MODEL), lambda i: (0, 0, 0)),
                  gate(0), gate(1), gate(2)],
        out_specs=pl.BlockSpec((tr, D_MODEL), lambda i: (i, 0)),
        out_shape=jax.ShapeDtypeStruct((s, D_MODEL), BF16),
        compiler_params=_cparams("parallel"),
    )(ya, yb, yc, wb, p, p, p)


def _merge_bwd(dm, ya, yb, yc, wb, p, *, name):
    s = p.shape[0]
    tr = _pick(s, 256, 16)

    def body(dm_ref, ya_ref, yb_ref, yc_ref, wb_ref, g0_ref, g1_ref, g2_ref,
             dya_ref, dyb_ref, dyc_ref, dg_ref, dbrd0_ref, dbrd1_ref, dbrd2_ref):
        dmv = dm_ref[...]
        ys = (ya_ref, yb_ref, yc_ref)
        gs = (g0_ref, g1_ref, g2_ref)
        dys = (dya_ref, dyb_ref, dyc_ref)
        dbrds = (dbrd0_ref, dbrd1_ref, dbrd2_ref)
        for n in range(3):
            brd = _dot(ys[n][...].astype(BF16), wb_ref[n])
            sg = _sigmoid(gs[n][...])
            dbrd = (sg * dmv).astype(BF16)
            dbrds[n][...] = dbrd
            dg_ref[:, n * D_MODEL:(n + 1) * D_MODEL] = (dmv * brd * sg * (1.0 - sg)).astype(BF16)
            dys[n][...] = _dot_nt(dbrd, wb_ref[n])

    yspec = pl.BlockSpec((tr, BRANCH_W), lambda i: (i, 0))
    gate = lambda n: pl.BlockSpec((tr, D_MODEL), lambda i: (i, 4 + n))
    row = pl.BlockSpec((tr, D_MODEL), lambda i: (i, 0))
    return pl.pallas_call(
        body, name=name, grid=(s // tr,),
        in_specs=[row, yspec, yspec, yspec, pl.BlockSpec((3, BRANCH_W, D_MODEL), lambda i: (0, 0, 0)),
                  gate(0), gate(1), gate(2)],
        out_specs=[yspec, yspec, yspec, pl.BlockSpec((tr, 3 * D_MODEL), lambda i: (i, 0)), row, row, row],
        out_shape=[jax.ShapeDtypeStruct((s, BRANCH_W), F32)] * 3
                  + [jax.ShapeDtypeStruct((s, 3 * D_MODEL), BF16)] + [jax.ShapeDtypeStruct((s, D_MODEL), BF16)] * 3,
        compiler_params=_cparams("parallel"),
    )(dm, ya, yb, yc, wb, p, p, p)


def _xa_probs(q, k):
    sc = _dot_nt(q, k) * XA_SCALE
    e = jnp.exp(sc - jnp.max(sc, axis=-1, keepdims=True))
    return e / jnp.sum(e, axis=-1, keepdims=True)


def _xa_fwd(q, k, v, *, name):
    s = q.shape[0]
    mt = k.shape[0]
    tr = _pick(s, 512, 16)

    def body(q_ref, k_ref, v_ref, o_ref):
        pr = _xa_probs(q_ref[...], k_ref[...])
        o_ref[...] = _dot(pr.astype(BF16), v_ref[...]).astype(BF16)

    qs = pl.BlockSpec((tr, XA_HEAD), lambda h, i: (i, h))
    ks = pl.BlockSpec((mt, XA_HEAD), lambda h, i: (0, h))
    return pl.pallas_call(
        body, name=name, grid=(D_MODEL // XA_HEAD, s // tr),
        in_specs=[qs, ks, ks], out_specs=qs,
        out_shape=jax.ShapeDtypeStruct((s, D_MODEL), BF16),
        compiler_params=_cparams("parallel", "parallel"),
    )(q, k, v)


def _xa_bwd(q, k, v, do, *, name):
    s = q.shape[0]
    mt = k.shape[0]
    tr = _pick(s, 512, 16)

    def body(q_ref, k_ref, v_ref, do_ref, dq_ref, dk_ref, dv_ref):
        @pl.when(pl.program_id(1) == 0)
        def _():
            dk_ref[...] = jnp.zeros_like(dk_ref)
            dv_ref[...] = jnp.zeros_like(dv_ref)

        qv = q_ref[...]
        kv = k_ref[...]
        dov = do_ref[...]
        pr = _xa_probs(qv, kv)
        dpr = _dot_nt(dov, v_ref[...])
        ds = (pr * (dpr - jnp.sum(dpr * pr, axis=-1, keepdims=True)) * XA_SCALE).astype(BF16)
        dq_ref[...] = _dot(ds, kv).astype(BF16)
        dk_ref[...] += _dot_tn(ds, qv)
        dv_ref[...] += _dot_tn(pr.astype(BF16), dov)

    qs = pl.BlockSpec((tr, XA_HEAD), lambda h, i: (i, h))
    ks = pl.BlockSpec((mt, XA_HEAD), lambda h, i: (0, h))
    return pl.pallas_call(
        body, name=name, grid=(D_MODEL // XA_HEAD, s // tr),
        in_specs=[qs, ks, ks, qs], out_specs=[qs, ks, ks],
        out_shape=[jax.ShapeDtypeStruct((s, D_MODEL), BF16), jax.ShapeDtypeStruct((mt, D_MODEL), F32),
                   jax.ShapeDtypeStruct((mt, D_MODEL), F32)],
        compiler_params=_cparams("parallel", "arbitrary"),
    )(q, k, v, do)


def _swiglu_fwd(ab, *, name):
    s, f = ab.shape[0], ab.shape[1] // 2
    tr = _pick(s, 256, 16)

    def body(a_ref, b_ref, o_ref):
        av = a_ref[...].astype(F32)
        o_ref[...] = (av * _sigmoid(av) * b_ref[...].astype(F32)).astype(BF16)

    half = lambda c: pl.BlockSpec((tr, f), lambda i: (i, c))
    return pl.pallas_call(
        body, name=name, grid=(s // tr,), in_specs=[half(0), half(1)], out_specs=half(0),
        out_shape=jax.ShapeDtypeStruct((s, f), BF16), compiler_params=_cparams("parallel"),
    )(ab, ab)


def _swiglu_bwd(ab, dh, *, name):
    s, f = ab.shape[0], ab.shape[1] // 2
    tr = _pick(s, 256, 16)

    def body(a_ref, b_ref, dh_ref, o_ref):
        av = a_ref[...].astype(F32)
        dhv = dh_ref[...]
        sg = _sigmoid(av)
        silu = av * sg
        o_ref[:, :f] = (dhv * b_ref[...].astype(F32) * (sg + silu * (1.0 - sg))).astype(BF16)
        o_ref[:, f:] = (dhv * silu).astype(BF16)

    half = lambda c: pl.BlockSpec((tr, f), lambda i: (i, c))
    return pl.pallas_call(
        body, name=name, grid=(s // tr,), in_specs=[half(0), half(1), half(0)],
        out_specs=pl.BlockSpec((tr, 2 * f), lambda i: (i, 0)),
        out_shape=jax.ShapeDtypeStruct((s, 2 * f), BF16), compiler_params=_cparams("parallel"),
    )(ab, ab, dh)


def _adamw(w, g, m, v, *, name):
    r, c = w.shape
    tr = _pick(r, 512, 8)

    def body(w_ref, g_ref, m_ref, v_ref, d_ref, mo_ref, vo_ref):
        gv = g_ref[...]
        mn = ADAM_B1 * m_ref[...] + (1.0 - ADAM_B1) * gv
        vn = ADAM_B2 * v_ref[...] + (1.0 - ADAM_B2) * (gv * gv)
        m_hat = mn / (1.0 - ADAM_B1 ** ADAM_STEP)
        v_hat = vn / (1.0 - ADAM_B2 ** ADAM_STEP)
        d_ref[...] = -ADAM_LR * (m_hat / (jnp.sqrt(v_hat) + ADAM_EPS) + ADAM_WD * w_ref[...])
        mo_ref[...] = mn
        vo_ref[...] = vn

    spec = pl.BlockSpec((tr, c), lambda i: (i, 0))
    shp = jax.ShapeDtypeStruct((r, c), F32)
    return pl.pallas_call(
        body, name=name, grid=(r // tr,), in_specs=[spec] * 4, out_specs=[spec] * 3,
        out_shape=[shp] * 3, compiler_params=_cparams("parallel"),
    )(w, g, m, v)


def _position():
    return lax.axis_index("x"), lax.axis_index("y"), lax.axis_index("c")


def _all_gather(x, *, name):
    t, c_ = x.shape

    def body(x_ref, out_ref, send_sems, recv_sems, local_sem):
        start, forward, finish = _gather_phases(x_ref, out_ref, send_sems, recv_sems, local_sem)
        start()
        forward()
        finish()

    return pl.pallas_call(
        body, name=name,
        out_shape=jax.ShapeDtypeStruct((N_DEV, t, c_), x.dtype),
        in_specs=[pl.BlockSpec(memory_space=pl.ANY)],
        out_specs=pl.BlockSpec(memory_space=pl.ANY),
        scratch_shapes=_GATHER_SEMS,
    )(x)


_GATHER_SEMS = [pltpu.SemaphoreType.DMA((7,)), pltpu.SemaphoreType.DMA((7,)), pltpu.SemaphoreType.DMA]


def _gather_phases(x_ref, out_ref, send_sems, recv_sems, local_sem):
    x_, y_, c = _position()
    me, sibling = (x_, y_, c), (x_, y_, 1 - c)
    chips = [(1 - x_, y_), (x_, 1 - y_), (1 - x_, 1 - y_)]

    def block(px, py, pc):
        return out_ref.at[4 * px + 2 * py + pc]

    def copy(k, blk, to, src=None):
        return pltpu.make_async_remote_copy(
            src_ref=block(*blk) if src is None else src, dst_ref=block(*blk),
            send_sem=send_sems.at[k], recv_sem=recv_sems.at[k], device_id=to, device_id_type=MESH)

    mine = pltpu.make_async_copy(x_ref, block(*me), local_sem)
    first = [copy(0, me, sibling, src=x_ref)]
    first += [copy(1 + j, me, (*chip, c), src=x_ref) for j, chip in enumerate(chips)]
    passed = [copy(4 + j, (*chip, c), sibling) for j, chip in enumerate(chips)]

    def start():
        mine.start()
        for cp in first:
            cp.start()

    def forward():
        for j, chip in enumerate(chips):
            copy(1 + j, (*chip, c), me).wait_recv()
            passed[j].start()

    def finish():
        copy(0, sibling, me).wait_recv()
        for j, chip in enumerate(chips):
            copy(4 + j, (*chip, 1 - c), me).wait_recv()
        for cp in first + passed:
            cp.wait_send()
        mine.wait()

    return start, forward, finish


_SCATTER_SEMS = [pltpu.SemaphoreType.DMA((7,)), pltpu.SemaphoreType.DMA((7,)), pltpu.SemaphoreType.DMA]


def _scatter_phases(g_ref, r_ref, send_sems, recv_sems, local_sem):
    x_, y_, c = _position()
    me = 4 * x_ + 2 * y_ + c
    local = pltpu.make_async_copy(g_ref.at[me], r_ref.at[me], local_sem)
    copies = []
    for k in range(1, N_DEV):
        to = (x_ ^ (k >> 2), y_ ^ ((k >> 1) & 1), c ^ (k & 1))
        copies.append(pltpu.make_async_remote_copy(
            src_ref=g_ref.at[me ^ k], dst_ref=r_ref.at[me], send_sem=send_sems.at[k - 1],
            recv_sem=recv_sems.at[k - 1], device_id=to, device_id_type=MESH))

    def start():
        local.start()
        for cp in copies:
            cp.start()

    def finish():
        for k in range(1, N_DEV):
            pltpu.make_async_remote_copy(
                src_ref=g_ref.at[me], dst_ref=r_ref.at[me ^ k], send_sem=send_sems.at[k - 1],
                recv_sem=recv_sems.at[k - 1], device_id=(x_, y_, c), device_id_type=MESH).wait_recv()
        for cp in copies:
            cp.wait_send()
        local.wait()

    return start, finish


def _sum_devices(r8, *, name):
    _, t, c_ = r8.shape
    tr = _pick(t, 256, 16)

    def body(r_ref, o_ref):
        acc = r_ref[0].astype(F32)
        for d in range(1, N_DEV):
            acc = acc + r_ref[d].astype(F32)
        o_ref[...] = acc

    return pl.pallas_call(
        body, name=name, grid=(t // tr,),
        in_specs=[pl.BlockSpec((N_DEV, tr, c_), lambda i: (0, i, 0))],
        out_specs=pl.BlockSpec((tr, c_), lambda i: (i, 0)),
        out_shape=jax.ShapeDtypeStruct((t, c_), F32),
        compiler_params=_cparams("parallel"),
    )(r8)


def _all_reduce_small(x, *, name):
    r, c_ = x.shape

    def body(x_ref, o_ref, buf, send_sems, recv_sems):
        x_, y_, c = _position()
        me = 4 * x_ + 2 * y_ + c
        buf[me] = x_ref[...]
        copies = []
        for k in range(1, N_DEV):
            to = (x_ ^ (k >> 2), y_ ^ ((k >> 1) & 1), c ^ (k & 1))
            copies.append(pltpu.make_async_remote_copy(
                src_ref=x_ref, dst_ref=buf.at[me], send_sem=send_sems.at[k - 1], recv_sem=recv_sems.at[k - 1],
                device_id=to, device_id_type=MESH))
        for cp in copies:
            cp.start()
        for k in range(1, N_DEV):
            src = me ^ k
            pltpu.make_async_remote_copy(
                src_ref=x_ref, dst_ref=buf.at[src], send_sem=send_sems.at[k - 1], recv_sem=recv_sems.at[k - 1],
                device_id=(x_, y_, c), device_id_type=MESH).wait_recv()
        for cp in copies:
            cp.wait_send()
        acc = buf[0]
        for d in range(1, N_DEV):
            acc = acc + buf[d]
        o_ref[...] = acc

    return pl.pallas_call(
        body, name=name,
        out_shape=jax.ShapeDtypeStruct((r, c_), F32),
        in_specs=[pl.BlockSpec(memory_space=pltpu.VMEM)],
        out_specs=pl.BlockSpec(memory_space=pltpu.VMEM),
        scratch_shapes=[pltpu.VMEM((N_DEV, r, c_), F32), pltpu.SemaphoreType.DMA((7,)), pltpu.SemaphoreType.DMA((7,))],
    )(x)


def _rs_pair_exchange(g8, *, name):
    _, t, c_ = g8.shape

    def body(g_ref, r_ref, send_sems, recv_sems):
        x_, y_, c = _position()
        copies = [pltpu.make_async_remote_copy(
            src_ref=g_ref.at[2 * ch + (1 - c)], dst_ref=r_ref.at[ch],
            send_sem=send_sems.at[ch], recv_sem=recv_sems.at[ch],
            device_id=(x_, y_, 1 - c), device_id_type=MESH) for ch in range(4)]
        for cp in copies:
            cp.start()
        for cp in copies:
            cp.wait()

    return pl.pallas_call(
        body, name=name,
        out_shape=jax.ShapeDtypeStruct((4, t, c_), g8.dtype),
        in_specs=[pl.BlockSpec(memory_space=pl.ANY)],
        out_specs=pl.BlockSpec(memory_space=pl.ANY),
        scratch_shapes=[pltpu.SemaphoreType.DMA((4,)), pltpu.SemaphoreType.DMA((4,))],
    )(g8)


def _pair_add(core, g8, recv, *, name):
    _, t, c_ = g8.shape
    tr = _pick(t, 512, 16)

    def body(core_ref, g_ref, r_ref, o_ref):
        o_ref[...] = (g_ref[...].astype(F32) + r_ref[...].astype(F32)).astype(o_ref.dtype)

    grid_spec = pltpu.PrefetchScalarGridSpec(
        num_scalar_prefetch=1, grid=(4, t // tr),
        in_specs=[pl.BlockSpec((None, tr, c_), lambda ch, i, core_ref: (2 * ch + core_ref[0], i, 0)),
                  pl.BlockSpec((None, tr, c_), lambda ch, i, core_ref: (ch, i, 0))],
        out_specs=pl.BlockSpec((None, tr, c_), lambda ch, i, core_ref: (ch, i, 0)))
    return pl.pallas_call(
        body, name=name, grid_spec=grid_spec,
        out_shape=jax.ShapeDtypeStruct((4, t, c_), g8.dtype),
        compiler_params=_cparams("parallel", "parallel"),
    )(core, g8, recv)


def _rs_chip_exchange(part, *, name):
    _, t, c_ = part.shape

    def body(p_ref, r_ref, send_sems, recv_sems, local_sem):
        x_, y_, c = _position()
        mine = 2 * x_ + y_
        local = pltpu.make_async_copy(p_ref.at[mine], r_ref.at[mine], local_sem)
        local.start()
        chips = [(1 - x_, y_), (x_, 1 - y_), (1 - x_, 1 - y_)]
        copies = [pltpu.make_async_remote_copy(
            src_ref=p_ref.at[2 * px + py], dst_ref=r_ref.at[mine],
            send_sem=send_sems.at[k], recv_sem=recv_sems.at[k],
            device_id=(px, py, c), device_id_type=MESH) for k, (px, py) in enumerate(chips)]
        for cp in copies:
            cp.start()
        for k, (px, py) in enumerate(chips):
            pltpu.make_async_remote_copy(
                src_ref=p_ref.at[mine], dst_ref=r_ref.at[2 * px + py],
                send_sem=send_sems.at[k], recv_sem=recv_sems.at[k],
                device_id=(x_, y_, c), device_id_type=MESH).wait_recv()
        for cp in copies:
            cp.wait_send()
        local.wait()

    return pl.pallas_call(
        body, name=name,
        out_shape=jax.ShapeDtypeStruct((4, t, c_), part.dtype),
        in_specs=[pl.BlockSpec(memory_space=pl.ANY)],
        out_specs=pl.BlockSpec(memory_space=pl.ANY),
        scratch_shapes=[pltpu.SemaphoreType.DMA((3,)), pltpu.SemaphoreType.DMA((3,)), pltpu.SemaphoreType.DMA],
    )(part)


def _sum_chips(r4, *, name):
    _, t, c_ = r4.shape
    tr = _pick(t, 512, 16)

    def body(r_ref, o_ref):
        acc = r_ref[0].astype(F32)
        for ch in range(1, 4):
            acc = acc + r_ref[ch].astype(F32)
        o_ref[...] = acc

    return pl.pallas_call(
        body, name=name, grid=(t // tr,),
        in_specs=[pl.BlockSpec((4, tr, c_), lambda i: (0, i, 0))],
        out_specs=pl.BlockSpec((tr, c_), lambda i: (i, 0)),
        out_shape=jax.ShapeDtypeStruct((t, c_), F32),
        compiler_params=_cparams("parallel"),
    )(r4)


BIG = (
    ("w_in", (IN_COLS // N_DEV, D_MODEL), 0),
    ("w_branch", (3, BRANCH_W, D_MODEL // N_DEV), 2),
    ("w_out", (D_MODEL // N_DEV, D_MODEL), 0),
    ("w_q_xa", (D_MODEL // N_DEV, D_MODEL), 0),
    ("w_k_xa", (D_MODEL // N_DEV, D_MODEL), 0),
    ("w_v_xa", (D_MODEL // N_DEV, D_MODEL), 0),
    ("w_o_xa", (D_MODEL // N_DEV, D_MODEL), 0),
    ("w_gate_ffn", (FFN // N_DEV, D_MODEL), 0),
    ("w_up_ffn", (FFN // N_DEV, D_MODEL), 0),
    ("w_down_ffn", (FFN // N_DEV, D_MODEL), 0),
)
TRANSPOSED = ("w_in", "w_gate_ffn", "w_up_ffn")
_BIG_LAYOUT = {n: (shp, ax) for n, shp, ax in BIG}
PACK_COLS = 1024


def _stored(name, shard):
    return shard.T if name in TRANSPOSED else shard


def _size(shape):
    n = 1
    for d in shape:
        n *= d
    return n


def _pack_shards(items, shards):
    return jnp.concatenate([shards[it].reshape(-1, PACK_COLS) for it in items], axis=0)


def _unpack_gathered(items, g):
    out = {}
    r0 = 0
    for it in items:
        shp, ax = _BIG_LAYOUT[it[0]]
        rows = _size(shp) // PACK_COLS
        blk = g[:, r0:r0 + rows].reshape((N_DEV,) + shp)
        r0 += rows
        blk = jnp.moveaxis(blk, 0, ax)
        full = list(shp)
        full[ax] = shp[ax] * N_DEV
        out[it] = blk.reshape(full)
    return out


def _pack_full(items, full):
    parts = []
    for it in items:
        shp, ax = _BIG_LAYOUT[it[0]]
        t = full[it].reshape(shp[:ax] + (N_DEV, shp[ax]) + shp[ax + 1:])
        t = jnp.moveaxis(t, ax, 0)
        parts.append(t.reshape(N_DEV, -1, PACK_COLS))
    return jnp.concatenate(parts, axis=1)


def _unpack_shard(items, flat):
    out = {}
    r0 = 0
    for it in items:
        shp, _ = _BIG_LAYOUT[it[0]]
        rows = _size(shp) // PACK_COLS
        out[it] = flat[r0:r0 + rows].reshape(shp)
        r0 += rows
    return out


SMALL = (
    ("norm_mix_g", (DEPTH, D_MODEL)),
    ("sgu_ln_g", (DEPTH, BRANCH_W)),
    ("sgu_ln_b", (DEPTH, BRANCH_W)),
    ("w_spatial", (DEPTH, SGU_GROUPS, SGU_LEN, SGU_LEN)),
    ("b_spatial", (DEPTH, SGU_GROUPS, SGU_LEN)),
    ("conv_w", (DEPTH, 3, BRANCH_W)),
    ("norm_xa_g", (DEPTH, D_MODEL)),
    ("mem_norm_g", (DEPTH, D_MODEL)),
    ("norm_ffn_g", (DEPTH, D_MODEL)),
    ("final_g", (D_MODEL,)),
)


def _pack_small(grads):
    flat = jnp.concatenate([grads[n].reshape(-1) for n, _ in SMALL])
    rows = -(-flat.shape[0] // PACK_COLS)
    rows = -(-rows // 8) * 8
    flat = jnp.pad(flat, (0, rows * PACK_COLS - flat.shape[0]))
    return flat.reshape(rows, PACK_COLS)


def _unpack_small(buf):
    flat = buf.reshape(-1)
    out = {}
    o = 0
    for n, shp in SMALL:
        out[n] = flat[o:o + _size(shp)].reshape(shp)
        o += _size(shp)
    return out


def _layer_fwd(l, x, mem, wt, sm, gather=None):
    t = f"l{l}_"
    sv = {"x0": x}
    h = _rms_fwd(x, sm["norm_mix_g"][l][None], name=t + "rms_mix")
    p = _mm(h, wt["w_in", l], tb=True, name=t + "in_proj", tm=2048)
    if gather is None:
        ya = _sb_fwd(p, name=t + "sb_fwd")
    else:
        ya, gathered = _sb_fwd(p, name=t + "sb_fwd", gather=gather[1])
        wt.update(_unpack_gathered(gather[0], gathered))
    w_sp = sm["w_spatial"][l]
    b_col = sm["b_spatial"][l][:, :, None]
    ln_g, ln_b = sm["sgu_ln_g"][l][None], sm["sgu_ln_b"][l][None]
    yb = _sgu_fwd(p, ln_g, ln_b, w_sp, b_col, name=t + "sgu_fwd")
    yc = _conv_fwd(p, sm["conv_w"][l], name=t + "conv_fwd")
    merged = _merge_fwd(ya, yb, yc, wt["w_branch", l], p, name=t + "merge_fwd")
    x1 = _mm(merged, wt["w_out", l], add=x, name=t + "out_proj")
    sv.update(h=h, p=p, ya=ya, yb=yb, yc=yc, merged=merged, x1=x1)

    h2 = _rms_fwd(x1, sm["norm_xa_g"][l][None], name=t + "rms_xa")
    mn = _rms_fwd(mem, sm["mem_norm_g"][l][None], name=t + "rms_mem")
    q = _mm(h2, wt["w_q_xa", l], out_dtype=BF16, name=t + "xa_q", tm=2048)
    k = _mm(mn, wt["w_k_xa", l], out_dtype=BF16, name=t + "xa_k")
    v = _mm(mn, wt["w_v_xa", l], out_dtype=BF16, name=t + "xa_v")
    o = _xa_fwd(q, k, v, name=t + "xa_fwd")
    x2 = _mm(o, wt["w_o_xa", l], add=x1, name=t + "xa_o")
    sv.update(h2=h2, mn=mn, q=q, k=k, v=v, o=o, x2=x2)

    h3 = _rms_fwd(x2, sm["norm_ffn_g"][l][None], name=t + "rms_ffn")
    w_gu = jnp.concatenate([wt["w_gate_ffn", l], wt["w_up_ffn", l]], axis=0)
    ab = _mm(h3, w_gu, tb=True, out_dtype=BF16, name=t + "ffn_gate_up", tm=2048, tn=1408)
    hd = _swiglu_fwd(ab, name=t + "swiglu_fwd")
    x3 = _mm(hd, wt["w_down_ffn", l], add=x2, name=t + "ffn_down", tk=FFN)
    sv.update(h3=h3, ab=ab, hd=hd, w_gu=w_gu)
    return x3, sv


def _layer_bwd(l, dx3, mem, wt, sm, sv, scatter=None):
    t = f"l{l}_b_"
    gb, gs = {}, {}
    dhd = _mm(dx3, wt["w_down_ffn", l], tb=True, name=t + "ffn_down_dx", tn=1408)
    gb["w_down_ffn"] = _mm(sv["hd"], dx3, ta=True, out_dtype=BF16, name=t + "ffn_down_dw", tm=1408)
    dab = _swiglu_bwd(sv["ab"], dhd, name=t + "swiglu_bwd")
    dw_gu = _mm(dab, sv["h3"], ta=True, out_dtype=BF16, name=t + "ffn_gate_up_dw", tm=1408)
    gb["w_gate_ffn"], gb["w_up_ffn"] = dw_gu[:FFN], dw_gu[FFN:]
    dx2, dg = _mm(dab, sv["w_gu"], rms=(sv["x2"], sm["norm_ffn_g"][l][None], dx3),
                  name=t + "ffn_gate_up_dx", tm=512, tk=1408)
    gs["norm_ffn_g"] = dg[0]
    do = _mm(dx2, wt["w_o_xa", l], tb=True, out_dtype=BF16, name=t + "xa_o_dx")
    gb["w_o_xa"] = _mm(sv["o"], dx2, ta=True, out_dtype=BF16, name=t + "xa_o_dw")
    dq, dk, dv = _xa_bwd(sv["q"], sv["k"], sv["v"], do, name=t + "xa_bwd")
    dx1, dg = _mm(dq, wt["w_q_xa", l], tb=True, rms=(sv["x1"], sm["norm_xa_g"][l][None], dx2),
                  name=t + "xa_q_dx", tm=512)
    gs["norm_xa_g"] = dg[0]
    gb["w_q_xa"] = _mm(sv["h2"], dq, ta=True, out_dtype=BF16, name=t + "xa_q_dw")
    gb["w_k_xa"] = _mm(sv["mn"], dk, ta=True, out_dtype=BF16, name=t + "xa_k_dw")
    gb["w_v_xa"] = _mm(sv["mn"], dv, ta=True, out_dtype=BF16, name=t + "xa_v_dw")
    dmn = _mm(dk, wt["w_k_xa", l], tb=True, name=t + "xa_k_dx")
    dmn = _mm(dv, wt["w_v_xa", l], tb=True, add=dmn, name=t + "xa_v_dx")
    _, dg = _rms_bwd(mem, sm["mem_norm_g"][l][None], dmn, jnp.zeros_like(mem), name=t + "rms_mem")
    gs["mem_norm_g"] = dg[0]
    dm = _mm(dx1, wt["w_out", l], tb=True, name=t + "out_proj_dx")
    gb["w_out"] = _mm(sv["merged"], dx1, ta=True, out_dtype=BF16, name=t + "out_proj_dw")
    p = sv["p"]
    dya, dyb, dyc, dgates, *dbrd = _merge_bwd(dm, sv["ya"], sv["yb"], sv["yc"], wt["w_branch", l], p,
                                              name=t + "merge_bwd")
    gb["w_branch"] = jnp.stack([
        _mm(sv[y], dbrd[n], ta=True, out_dtype=BF16, name=t + f"branch{n}_dw")
        for n, y in enumerate(("ya", "yb", "yc"))])
    dcb, dcc, dcx, dcw = _conv_bwd(p, dyc, sm["conv_w"][l], name=t + "conv_bwd")
    gs["conv_w"] = dcw
    w_sp = sm["w_spatial"][l]
    dz, dlg, dlb, dwsp, dbsp = _sgu_bwd(p, dyb, sm["sgu_ln_g"][l][None], sm["sgu_ln_b"][l][None], w_sp,
                                        jnp.swapaxes(w_sp, 1, 2), sm["b_spatial"][l][:, :, None],
                                        name=t + "sgu_bwd")
    gs.update(sgu_ln_g=dlg[0], sgu_ln_b=dlb[0], w_spatial=dwsp, b_spatial=dbsp[:, :, 0])
    received = None
    if scatter is None:
        dq_a, dk_a, dv_a = _sb_bwd(p, dya, name=t + "sb_bwd")
    else:
        items, earlier = scatter
        ready = {**earlier, **{(n, l): g for n, g in gb.items()}}
        dq_a, dk_a, dv_a, received = _sb_bwd(p, dya, name=t + "sb_bwd", scatter=_pack_full(items, ready))
    dp = jnp.concatenate([dq_a, dk_a, dv_a, dz, dcb, dcc, dcx, dgates], axis=1)
    gb["w_in"] = _mm(dp, sv["h"], ta=True, out_dtype=BF16, name=t + "in_proj_dw")
    dx, dg = _mm(dp, wt["w_in", l], rms=(sv["x0"], sm["norm_mix_g"][l][None], dx1),
                 name=t + "in_proj_dx", tm=512, tk=1792)
    gs["norm_mix_g"] = dg[0]
    return dx, gb, gs, received


_WEIGHTS = ("norm_mix_g", "w_in", "sgu_ln_g", "sgu_ln_b", "w_spatial", "b_spatial", "conv_w", "w_branch", "w_out",
            "norm_xa_g", "mem_norm_g", "w_q_xa", "w_k_xa", "w_v_xa", "w_o_xa", "norm_ffn_g", "w_gate_ffn",
            "w_up_ffn", "w_down_ffn", "final_g")


def kernel(x, mem, norm_mix_g, w_in, sgu_ln_g, sgu_ln_b, w_spatial, b_spatial, conv_w, w_branch, w_out, norm_xa_g, mem_norm_g, w_q_xa, w_k_xa, w_v_xa, w_o_xa, norm_ffn_g, w_gate_ffn, w_up_ffn, w_down_ffn, final_g, loss_target, m_norm_mix_g, m_w_in, m_sgu_ln_g, m_sgu_ln_b, m_w_spatial, m_b_spatial, m_conv_w, m_w_branch, m_w_out, m_norm_xa_g, m_mem_norm_g, m_w_q_xa, m_w_k_xa, m_w_v_xa, m_w_o_xa, m_norm_ffn_g, m_w_gate_ffn, m_w_up_ffn, m_w_down_ffn, m_final_g, v_norm_mix_g, v_w_in, v_sgu_ln_g, v_sgu_ln_b, v_w_spatial, v_b_spatial, v_conv_w, v_w_branch, v_w_out, v_norm_xa_g, v_mem_norm_g, v_w_q_xa, v_w_k_xa, v_w_v_xa, v_w_o_xa, v_norm_ffn_g, v_w_gate_ffn, v_w_up_ffn, v_w_down_ffn, v_final_g):
    w = dict(norm_mix_g=norm_mix_g, w_in=w_in, sgu_ln_g=sgu_ln_g, sgu_ln_b=sgu_ln_b, w_spatial=w_spatial,
             b_spatial=b_spatial, conv_w=conv_w, w_branch=w_branch, w_out=w_out, norm_xa_g=norm_xa_g,
             mem_norm_g=mem_norm_g, w_q_xa=w_q_xa, w_k_xa=w_k_xa, w_v_xa=w_v_xa, w_o_xa=w_o_xa,
             norm_ffn_g=norm_ffn_g, w_gate_ffn=w_gate_ffn, w_up_ffn=w_up_ffn, w_down_ffn=w_down_ffn, final_g=final_g)
    m = dict(norm_mix_g=m_norm_mix_g, w_in=m_w_in, sgu_ln_g=m_sgu_ln_g, sgu_ln_b=m_sgu_ln_b, w_spatial=m_w_spatial,
             b_spatial=m_b_spatial, conv_w=m_conv_w, w_branch=m_w_branch, w_out=m_w_out, norm_xa_g=m_norm_xa_g,
             mem_norm_g=m_mem_norm_g, w_q_xa=m_w_q_xa, w_k_xa=m_w_k_xa, w_v_xa=m_w_v_xa, w_o_xa=m_w_o_xa,
             norm_ffn_g=m_norm_ffn_g, w_gate_ffn=m_w_gate_ffn, w_up_ffn=m_w_up_ffn, w_down_ffn=m_w_down_ffn,
             final_g=m_final_g)
    v = dict(norm_mix_g=v_norm_mix_g, w_in=v_w_in, sgu_ln_g=v_sgu_ln_g, sgu_ln_b=v_sgu_ln_b, w_spatial=v_w_spatial,
             b_spatial=v_b_spatial, conv_w=v_conv_w, w_branch=v_w_branch, w_out=v_w_out, norm_xa_g=v_norm_xa_g,
             mem_norm_g=v_mem_norm_g, w_q_xa=v_w_q_xa, w_k_xa=v_w_k_xa, w_v_xa=v_w_v_xa, w_o_xa=v_w_o_xa,
             norm_ffn_g=v_norm_ffn_g, w_gate_ffn=v_w_gate_ffn, w_up_ffn=v_w_up_ffn, w_down_ffn=v_w_down_ffn,
             final_g=v_final_g)

    names = [n for n, _, _ in BIG]
    shards = {(n, l): _stored(n, w[n][l].astype(BF16)) for n in names for l in range(DEPTH)}
    first_items = [("w_in", 0)]
    later_items = [(n, l) for l in range(DEPTH) for n in names if (n, l) != ("w_in", 0)]
    wt = _unpack_gathered(first_items, _all_gather(_pack_shards(first_items, shards), name="gather_w_in0"))
    cw_pad = jnp.zeros((8, 128), F32).at[:DEPTH * 3, :BRANCH_W // N_DEV].set(conv_w.reshape(DEPTH * 3, -1))
    cw_all = _all_gather(cw_pad, name="gather_conv_w")[:, :DEPTH * 3, :BRANCH_W // N_DEV]
    conv_full = jnp.moveaxis(cw_all.reshape(N_DEV, DEPTH, 3, BRANCH_W // N_DEV), 0, 2).reshape(DEPTH, 3, BRANCH_W)
    sm = {n: w[n] for n, _ in SMALL}
    sm["conv_w"] = conv_full

    xs, ms = x[0], mem[0]
    x1, saved0 = _layer_fwd(0, xs, ms, wt, sm, gather=(later_items, _pack_shards(later_items, shards)))
    x2, saved1 = _layer_fwd(1, x1, ms, wt, sm)
    dcur, loss, dfinal = _final_loss(x2, sm["final_g"][None], loss_target[0], name="final_loss")
    loss = lax.psum(loss[0, 0], AXES)
    items_a = [(n, 1) for n in names if n != "w_in"]
    items_b = [("w_in", 1)] + [(n, 0) for n in names if n != "w_in"]
    items_c = [("w_in", 0)]
    dcur, gb1, gs1, recv_a = _layer_bwd(1, dcur, ms, wt, sm, saved1, scatter=(items_a, {}))
    dx, gb0, gs0, recv_b = _layer_bwd(0, dcur, ms, wt, sm, saved0, scatter=(items_b, {("w_in", 1): gb1["w_in"]}))

    shard_grads = _unpack_shard(items_a, _sum_devices(recv_a, name="rs_sum_a"))
    shard_grads.update(_unpack_shard(items_b, _sum_devices(recv_b, name="rs_sum_b")))
    g8 = _pack_full(items_c, {("w_in", 0): gb0["w_in"]})
    core = lax.axis_index("c").astype(jnp.int32).reshape(1)
    from_sibling = _rs_pair_exchange(g8, name="rs_pair_exchange")
    part = _pair_add(core, g8, from_sibling, name="rs_pair_add")
    by_chip = _rs_chip_exchange(part, name="rs_chip_exchange")
    shard_grads.update(_unpack_shard(items_c, _sum_chips(by_chip, name="rs_sum_chips")))
    grads = {n: jnp.stack([_stored(n, shard_grads[n, l]) for l in range(DEPTH)]) for n in names}
    small = {n: jnp.stack([gs0[n], gs1[n]]) for n, _ in SMALL if n != "final_g"}
    small["final_g"] = dfinal[0]
    small_sum = _unpack_small(_all_reduce_small(_pack_small(small), name="all_reduce_small"))
    width = BRANCH_W // N_DEV
    dev = 4 * lax.axis_index("x") + 2 * lax.axis_index("y") + lax.axis_index("c")
    for n, _ in SMALL:
        grads[n] = small_sum[n]
    grads["conv_w"] = lax.dynamic_slice_in_dim(small_sum["conv_w"], dev * width, width, axis=2)

    delta, new_m, new_v = {}, {}, {}
    for n in _WEIGHTS:
        shp = w[n].shape
        two_d = (-1, shp[-1])
        d_, m_, v_ = _adamw(w[n].reshape(two_d), grads[n].reshape(two_d), m[n].reshape(two_d), v[n].reshape(two_d),
                            name="adamw_" + n)
        delta[n], new_m[n], new_v[n] = d_.reshape(shp), m_.reshape(shp), v_.reshape(shp)

    return (loss, dx[None], *[grads[n] for n in _WEIGHTS], *[delta[n] for n in _WEIGHTS],
            *[new_m[n] for n in _WEIGHTS], *[new_v[n] for n in _WEIGHTS])
```

```python
import functools

import jax
import jax.numpy as jnp
from jax import lax
from jax.experimental import pallas as pl
from jax.experimental.pallas import tpu as pltpu

F32 = jnp.float32
BF16 = jnp.bfloat16
MESH = pl.DeviceIdType.MESH

D_MODEL = 1024
BRANCH_W = 512
IN_COLS = 7168
FFN = 2816
N_DEV = 8
DEPTH = 2
SB_BLOCK = 128
SB_SPAN = 1024
SB_Q_FWD = 512
SB_Q_BWD = 256
SB_SCALE = 0.125
XA_HEAD = 256
XA_SCALE = 0.0625
SGU_LEN = 128
SGU_GROUPS = 4
RMS_EPS = 1e-6
LN_EPS = 1e-5
HALO = 8

ADAM_LR = 0.001
ADAM_B1 = 0.9
ADAM_B2 = 0.999
ADAM_EPS = 1e-08
ADAM_WD = 0.01
ADAM_STEP = 10

VMEM_LIMIT_BYTES = 52 * 1024 * 1024

AXES = ("x", "y", "c")


def _cparams(*sem):
    return pltpu.CompilerParams(dimension_semantics=sem, vmem_limit_bytes=VMEM_LIMIT_BYTES)


def _pick(n, target, align):
    t = (min(target, n) // align) * align
    while t >= align:
        if n % t == 0:
            return t
        t -= align
    return n


def _dot(a, b):
    return jnp.dot(a, b, preferred_element_type=F32)


def _dot_nt(a, b):
    return lax.dot_general(a, b, (((1,), (1,)), ((), ())), preferred_element_type=F32)


def _dot_tn(a, b):
    return lax.dot_general(a, b, (((0,), (0,)), ((), ())), preferred_element_type=F32)


def _sigmoid(x):
    return 1.0 / (1.0 + jnp.exp(-x))


def _mm(a, b, *, name, ta=False, tb=False, out_dtype=F32, add=None, rms=None, tm=1024, tn=1024, tk=2048):
    m, k = (a.shape[1], a.shape[0]) if ta else a.shape
    n = b.shape[0] if tb else b.shape[1]
    assert k == (b.shape[1] if tb else b.shape[0])
    tm = _pick(m, tm, 128)
    tn = n if rms is not None else _pick(n, tn, 128)
    tk = _pick(k, tk, 128)
    nk = k // tk
    ca = 0 if ta else 1
    cb = 1 if tb else 0
    n_add = 0 if add is None else 1
    n_rms = 0 if rms is None else 3

    def body(*refs):
        refs = list(refs)
        a_ref, b_ref = refs[:2]
        extra = refs[2:2 + n_add + n_rms]
        outs = refs[2 + n_add + n_rms:]
        o_ref = outs[0]
        kk = pl.program_id(2)
        first_row_tile = pl.program_id(0) == 0

        def product():
            return lax.dot_general(a_ref[...].astype(BF16), b_ref[...].astype(BF16),
                                   (((ca,), (cb,)), ((), ())), preferred_element_type=F32)

        def finish(r):
            if add is not None:
                r = r + extra[0][...]
            if rms is None:
                o_ref[...] = r.astype(out_dtype)
                return
            x_ref, g_ref, dres_ref = extra[n_add:]
            dg_ref = outs[1]

            @pl.when(first_row_tile)
            def _():
                dg_ref[...] = jnp.zeros_like(dg_ref)

            xv = x_ref[...]
            rs = lax.rsqrt(jnp.mean(xv * xv, axis=-1, keepdims=True) + RMS_EPS)
            xh = xv * rs
            dg_ref[...] += jnp.sum(r * xh, axis=0, keepdims=True)
            dxh = r * g_ref[...]
            o_ref[...] = dres_ref[...] + rs * (dxh - xh * jnp.mean(dxh * xh, axis=-1, keepdims=True))

        if nk == 1:
            finish(product())
        else:
            acc_ref = outs[-1]

            @pl.when(kk == 0)
            def _():
                acc_ref[...] = jnp.zeros_like(acc_ref)

            acc_ref[...] += product()

            @pl.when(kk == nk - 1)
            def _():
                finish(acc_ref[...])

    a_spec = pl.BlockSpec((tk, tm), lambda i, j, kk: (kk, i)) if ta else pl.BlockSpec((tm, tk), lambda i, j, kk: (i, kk))
    b_spec = pl.BlockSpec((tn, tk), lambda i, j, kk: (j, kk)) if tb else pl.BlockSpec((tk, tn), lambda i, j, kk: (kk, j))
    tile = pl.BlockSpec((tm, tn), lambda i, j, kk: (i, j))
    in_specs = [a_spec, b_spec]
    operands = [a, b]
    out_specs = [tile]
    out_shape = [jax.ShapeDtypeStruct((m, n), out_dtype)]
    if add is not None:
        in_specs.append(tile)
        operands.append(add)
    if rms is not None:
        vec = pl.BlockSpec((1, n), lambda i, j, kk: (0, 0))
        in_specs += [tile, vec, tile]
        operands += list(rms)
        out_specs.append(vec)
        out_shape = [jax.ShapeDtypeStruct((m, n), F32), jax.ShapeDtypeStruct((1, n), F32)]
    out = pl.pallas_call(
        body, name=name,
        grid=(m // tm, n // tn, nk),
        in_specs=in_specs, out_specs=out_specs, out_shape=out_shape,
        scratch_shapes=[pltpu.VMEM((tm, tn), F32)] if nk > 1 else [],
        compiler_params=_cparams("arbitrary" if rms is not None else "parallel", "parallel", "arbitrary"),
    )(*operands)
    return out[0] if rms is None else out


def _rms_fwd(x, g, *, name):
    r, d = x.shape
    tr = _pick(r, 512, 16)

    def body(x_ref, g_ref, o_ref):
        xv = x_ref[...]
        rs = lax.rsqrt(jnp.mean(xv * xv, axis=-1, keepdims=True) + RMS_EPS)
        o_ref[...] = (xv * rs * g_ref[...]).astype(BF16)

    return pl.pallas_call(
        body, name=name, grid=(r // tr,),
        in_specs=[pl.BlockSpec((tr, d), lambda i: (i, 0)), pl.BlockSpec((1, d), lambda i: (0, 0))],
        out_specs=pl.BlockSpec((tr, d), lambda i: (i, 0)),
        out_shape=jax.ShapeDtypeStruct((r, d), BF16),
        compiler_params=_cparams("parallel"),
    )(x, g)


def _rms_bwd(x, g, dh, dres, *, name):
    r, d = x.shape
    tr = _pick(r, 256, 8)

    def body(x_ref, g_ref, dh_ref, dres_ref, dx_ref, dg_ref):
        @pl.when(pl.program_id(0) == 0)
        def _():
            dg_ref[...] = jnp.zeros_like(dg_ref)

        xv = x_ref[...]
        dhv = dh_ref[...].astype(F32)
        rs = lax.rsqrt(jnp.mean(xv * xv, axis=-1, keepdims=True) + RMS_EPS)
        xh = xv * rs
        dg_ref[...] += jnp.sum(dhv * xh, axis=0, keepdims=True)
        dxh = dhv * g_ref[...]
        dx_ref[...] = dres_ref[...] + rs * (dxh - xh * jnp.mean(dxh * xh, axis=-1, keepdims=True))

    return pl.pallas_call(
        body, name=name, grid=(r // tr,),
        in_specs=[pl.BlockSpec((tr, d), lambda i: (i, 0)), pl.BlockSpec((1, d), lambda i: (0, 0)),
                  pl.BlockSpec((tr, d), lambda i: (i, 0)), pl.BlockSpec((tr, d), lambda i: (i, 0))],
        out_specs=[pl.BlockSpec((tr, d), lambda i: (i, 0)), pl.BlockSpec((1, d), lambda i: (0, 0))],
        out_shape=[jax.ShapeDtypeStruct((r, d), F32), jax.ShapeDtypeStruct((1, d), F32)],
        compiler_params=_cparams("arbitrary"),
    )(x, g, dh, dres)


def _final_loss(x, g, target, *, name):
    r, d = x.shape
    tr = _pick(r, 256, 8)

    def body(x_ref, g_ref, t_ref, dx_ref, loss_ref, dg_ref):
        @pl.when(pl.program_id(0) == 0)
        def _():
            dg_ref[...] = jnp.zeros_like(dg_ref)
            loss_ref[...] = jnp.zeros_like(loss_ref)

        xv = x_ref[...]
        gv = g_ref[...]
        rs = lax.rsqrt(jnp.mean(xv * xv, axis=-1, keepdims=True) + RMS_EPS)
        xh = xv * rs
        err = xh * gv - t_ref[...]
        row_loss = jnp.mean(err * err, axis=-1, keepdims=True)
        loss_ref[...] += 0.5 * jnp.sum(row_loss, axis=0, keepdims=True)
        dy = err * (1.0 / d)
        dg_ref[...] += jnp.sum(dy * xh, axis=0, keepdims=True)
        dxh = dy * gv
        dx_ref[...] = rs * (dxh - xh * jnp.mean(dxh * xh, axis=-1, keepdims=True))

    return pl.pallas_call(
        body, name=name, grid=(r // tr,),
        in_specs=[pl.BlockSpec((tr, d), lambda i: (i, 0)), pl.BlockSpec((1, d), lambda i: (0, 0)),
                  pl.BlockSpec((tr, d), lambda i: (i, 0))],
        out_specs=[pl.BlockSpec((tr, d), lambda i: (i, 0)), pl.BlockSpec((1, 128), lambda i: (0, 0)),
                   pl.BlockSpec((1, d), lambda i: (0, 0))],
        out_shape=[jax.ShapeDtypeStruct((r, d), F32), jax.ShapeDtypeStruct((1, 128), F32),
                   jax.ShapeDtypeStruct((1, d), F32)],
        compiler_params=_cparams("arbitrary"),
    )(x, g, target)


def _cumsum_operand(strict_after, terms=1):
    r = lax.broadcasted_iota(jnp.int32, (terms * SB_BLOCK, 2 * SB_BLOCK), 0) % SB_BLOCK
    c = lax.broadcasted_iota(jnp.int32, (terms * SB_BLOCK, 2 * SB_BLOCK), 1)
    tri = (r > c) if strict_after else (r < c)
    return jnp.where((c >= SB_BLOCK) | tri, 1.0, 0.0).astype(BF16)


def _sb_scores(qh, kw, run, valid, after_ones):
    nb = kw.shape[0] // SB_BLOCK
    z = _dot_nt(qh, kw)
    lsp = jnp.minimum(z, 0.0) - jnp.log(1.0 + jnp.exp(-jnp.abs(z)))
    l1m = lsp - z
    if valid is not None:
        l1m = jnp.where(valid, l1m, 0.0)
    l1b = l1m.astype(BF16)
    later = [None] * nb
    for b in reversed(range(nb)):
        cols = slice(b * SB_BLOCK, (b + 1) * SB_BLOCK)
        ct = _dot(l1b[:, cols], after_ones)
        later[b] = run + ct[:, :SB_BLOCK]
        run = run + ct[:, SB_BLOCK:]
    a = jnp.exp(lsp + jnp.concatenate(later, axis=1))
    if valid is not None:
        a = jnp.where(valid, a, 0.0)
    return lsp, a, run


def _sb_setup(q_ref, span):
    qi = pl.program_id(1)
    rows = q_ref.shape[0]
    sd = (qi * rows + rows - 1) // span
    lane = lax.broadcasted_iota(jnp.int32, (rows, SB_BLOCK), 1)
    col = lax.broadcasted_iota(jnp.int32, (rows, span), 1)
    row = lax.broadcasted_iota(jnp.int32, (rows, span), 0)
    valid = col < (qi * rows - sd * span) + row
    q = q_ref[...] * SB_SCALE
    qhs = (jnp.where(lane < 64, q, 0.0).astype(BF16), jnp.where(lane >= 64, q, 0.0).astype(BF16))
    return lane, sd, valid, qhs


def _sb_fwd(p, *, name, gather=None):
    s = p.shape[0]
    qrows = min(SB_Q_FWD, s)
    nq = s // qrows
    kcol = BRANCH_W // SB_BLOCK
    span = min(SB_SPAN, s)

    def body(*refs):
        if gather is None:
            q_ref, k_ref, v_ref, o_ref = refs
        else:
            q_ref, k_ref, v_ref, x_ref, o_ref, g_ref, send_sems, recv_sems, local_sem = refs
            start, forward, finish = _gather_phases(x_ref, g_ref, send_sems, recv_sems, local_sem)
            step = pl.program_id(0) * nq + pl.program_id(1)
            pl.when(step == 0)(start)
        lane, sd, valid, qhs = _sb_setup(q_ref, span)
        after_ones = _cumsum_operand(True)
        zero = jnp.zeros((qrows, SB_BLOCK), F32)

        def span_step(sb, carry, mask):
            rows = pl.ds(pl.multiple_of(sb * span, span), span)
            kw = k_ref[rows, :].astype(BF16)
            vw = v_ref[rows, :].astype(BF16)
            out = []
            for h in range(2):
                run, acc = carry[h]
                _, a, run = _sb_scores(qhs[h], kw, run, mask, after_ones)
                out.append((run, acc + _dot(a.astype(BF16), vw)))
            return tuple(out)

        carry = span_step(sd, ((zero, zero), (zero, zero)), valid)
        carry = lax.fori_loop(0, sd, lambda t, c: span_step(sd - 1 - t, c, None), carry)
        o_ref[...] = jnp.where(lane < 64, carry[0][1], carry[1][1]).astype(BF16)
        if gather is not None:
            pl.when(step == (kcol - 1) * nq + (3 * nq) // 4)(forward)
            pl.when(step == kcol * nq - 1)(finish)

    in_specs = [pl.BlockSpec((qrows, SB_BLOCK), lambda hp, qi: (qi, hp)),
                pl.BlockSpec((s, SB_BLOCK), lambda hp, qi: (0, kcol + hp)),
                pl.BlockSpec((s, SB_BLOCK), lambda hp, qi: (0, 2 * kcol + hp))]
    out_specs = [pl.BlockSpec((qrows, SB_BLOCK), lambda hp, qi: (qi, hp))]
    out_shape = [jax.ShapeDtypeStruct((s, BRANCH_W), BF16)]
    operands = [p, p, p]
    scratch = []
    if gather is not None:
        in_specs.append(pl.BlockSpec(memory_space=pl.ANY))
        out_specs.append(pl.BlockSpec(memory_space=pl.ANY))
        out_shape.append(jax.ShapeDtypeStruct((N_DEV,) + gather.shape, gather.dtype))
        operands.append(gather)
        scratch = _GATHER_SEMS
    out = pl.pallas_call(
        body, name=name, grid=(kcol, nq), in_specs=in_specs, out_specs=out_specs, out_shape=out_shape,
        scratch_shapes=scratch, compiler_params=_cparams("arbitrary", "arbitrary"),
    )(*operands)
    return out[0] if gather is None else out


def _sb_bwd(p, dya, *, name, scatter=None):
    s = p.shape[0]
    qrows = min(SB_Q_BWD, s)
    nq = s // qrows
    kcol = BRANCH_W // SB_BLOCK
    span = min(SB_SPAN, s)
    per = span // SB_BLOCK

    def body(*refs):
        if scatter is None:
            q_ref, k_ref, v_ref, do_ref, dq_ref, dk_ref, dv_ref, a_s, b_s, dk_acc, dv_acc = refs
        else:
            (q_ref, k_ref, v_ref, do_ref, g_ref, dq_ref, dk_ref, dv_ref, r_ref,
             a_s, b_s, dk_acc, dv_acc, send_sems, recv_sems, local_sem) = refs
            start, finish = _scatter_phases(g_ref, r_ref, send_sems, recv_sems, local_sem)
            step = pl.program_id(0) * nq + pl.program_id(1)
            pl.when(step == 0)(start)
        qi = pl.program_id(1)

        @pl.when(qi == 0)
        def _():
            dk_acc[...] = jnp.zeros_like(dk_acc)
            dv_acc[...] = jnp.zeros_like(dv_acc)

        lane, sd, valid, qhs = _sb_setup(q_ref, span)
        after_ones = _cumsum_operand(True)
        before_ones = _cumsum_operand(False)
        do = do_ref[...]
        dohs = (jnp.where(lane < 64, do, 0.0).astype(BF16), jnp.where(lane >= 64, do, 0.0).astype(BF16))
        zero = jnp.zeros((qrows, SB_BLOCK), F32)

        def rebuild(sb, runs, mask):
            rows = pl.ds(pl.multiple_of(sb * span, span), span)
            kw = k_ref[rows, :].astype(BF16)
            out = []
            for h in range(2):
                lsp, a, run = _sb_scores(qhs[h], kw, runs[h], mask, after_ones)
                beta = jnp.exp(lsp)
                if mask is not None:
                    beta = jnp.where(mask, beta, 0.0)
                a_s[h, sb] = a
                b_s[h, sb] = beta
                out.append(run)
            return tuple(out)

        runs = rebuild(sd, (zero, zero), valid)
        lax.fori_loop(0, sd, lambda t, r: rebuild(sd - 1 - t, r, None), runs)

        def accumulate(sb, carry):
            rows = pl.ds(pl.multiple_of(sb * span, span), span)
            kw = k_ref[rows, :].astype(BF16)
            vw = v_ref[rows, :].astype(BF16)
            out = []
            dk_span = jnp.zeros((span, SB_BLOCK), F32)
            dv_span = jnp.zeros((span, SB_BLOCK), F32)
            for h in range(2):
                pg, dq = carry[h]
                a = a_s[h, sb]
                beta = b_s[h, sb]
                g = a * _dot_nt(dohs[h], vw)
                gb = g.astype(BF16)
                before = [None] * per
                for b in range(per):
                    cols = slice(b * SB_BLOCK, (b + 1) * SB_BLOCK)
                    gt = _dot(gb[:, cols], before_ones)
                    before[b] = pg + gt[:, :SB_BLOCK]
                    pg = pg + gt[:, SB_BLOCK:]
                dz = (g * (1.0 - beta) - beta * jnp.concatenate(before, axis=1)).astype(BF16)
                dk_span = dk_span + _dot_tn(dz, qhs[h])
                dv_span = dv_span + _dot_tn(a.astype(BF16), dohs[h])
                out.append((pg, dq + _dot(dz, kw)))
            dk_acc[rows, :] += dk_span
            dv_acc[rows, :] += dv_span
            return tuple(out)

        carry = lax.fori_loop(0, sd + 1, accumulate, ((zero, zero), (zero, zero)))
        dq_ref[...] = (jnp.where(lane < 64, carry[0][1], carry[1][1]) * SB_SCALE).astype(BF16)

        @pl.when(qi == nq - 1)
        def _():
            dk_ref[...] = dk_acc[...].astype(BF16)
            dv_ref[...] = dv_acc[...].astype(BF16)

        if scatter is not None:
            pl.when(step == kcol * nq - 1)(finish)

    blk = pl.BlockSpec((qrows, SB_BLOCK), lambda hp, qi: (qi, hp))
    col = pl.BlockSpec((s, SB_BLOCK), lambda hp, qi: (0, hp))
    out = jax.ShapeDtypeStruct((s, BRANCH_W), BF16)
    in_specs = [blk,
                pl.BlockSpec((s, SB_BLOCK), lambda hp, qi: (0, kcol + hp)),
                pl.BlockSpec((s, SB_BLOCK), lambda hp, qi: (0, 2 * kcol + hp)),
                blk]
    out_specs = [blk, col, col]
    out_shape = [out, out, out]
    operands = [p, p, p, dya]
    scratch = [pltpu.VMEM((2, s // span, qrows, span), F32), pltpu.VMEM((2, s // span, qrows, span), F32),
               pltpu.VMEM((s, SB_BLOCK), F32), pltpu.VMEM((s, SB_BLOCK), F32)]
    if scatter is not None:
        in_specs.append(pl.BlockSpec(memory_space=pl.ANY))
        out_specs.append(pl.BlockSpec(memory_space=pl.ANY))
        out_shape.append(jax.ShapeDtypeStruct(scatter.shape, scatter.dtype))
        operands.append(scatter)
        scratch = scratch + _SCATTER_SEMS
    return pl.pallas_call(
        body, name=name, grid=(kcol, nq), in_specs=in_specs, out_specs=out_specs, out_shape=out_shape,
        scratch_shapes=scratch, compiler_params=_cparams("arbitrary", "arbitrary"),
    )(*operands)


_INV_SQRT2 = 0.7071067811865476
_INV_SQRT2PI = 0.3989422804014327


def _gelu(x):
    return 0.5 * x * (1.0 + lax.erf(x * _INV_SQRT2))


def _gelu_grad(x):
    return 0.5 * (1.0 + lax.erf(x * _INV_SQRT2)) + x * _INV_SQRT2PI * jnp.exp(-0.5 * x * x)


def _chunk_mask(transposed=False):
    r = lax.broadcasted_iota(jnp.int32, (SGU_LEN, SGU_LEN), 0)
    c = lax.broadcasted_iota(jnp.int32, (SGU_LEN, SGU_LEN), 1)
    return (c // 64) >= (r // 64) if transposed else (r // 64) >= (c // 64)


def _sgu_norm(v_raw, g, b):
    zv = _gelu(v_raw)
    xc = zv - jnp.mean(zv, axis=-1, keepdims=True)
    rs = lax.rsqrt(jnp.mean(xc * xc, axis=-1, keepdims=True) + LN_EPS)
    xh = xc * rs
    return xh, rs, xh * g + b


def _sgu_fwd(p, ln_g, ln_b, w, b_col, *, name):
    s = p.shape[0]
    tr = _pick(s, 512, SGU_LEN)

    def body(u_ref, v_ref, g_ref, b_ref, w_ref, bc_ref, o_ref):
        mask = _chunk_mask()
        zu = _gelu(u_ref[...])
        _, _, vn = _sgu_norm(v_ref[...], g_ref[...], b_ref[...])
        vnb = vn.astype(BF16)
        for gi in range(SGU_GROUPS):
            wg = jnp.where(mask, w_ref[gi], 0.0).astype(BF16)
            cs = slice(gi * SGU_LEN, (gi + 1) * SGU_LEN)
            for c in range(tr // SGU_LEN):
                rs_ = slice(c * SGU_LEN, (c + 1) * SGU_LEN)
                vm = _dot(wg, vnb[rs_, cs]) + bc_ref[gi]
                o_ref[rs_, cs] = (zu[rs_, cs] * vm).astype(BF16)

    vec = pl.BlockSpec((1, BRANCH_W), lambda i: (0, 0))
    return pl.pallas_call(
        body, name=name, grid=(s // tr,),
        in_specs=[pl.BlockSpec((tr, BRANCH_W), lambda i: (i, 3)), pl.BlockSpec((tr, BRANCH_W), lambda i: (i, 4)),
                  vec, vec,
                  pl.BlockSpec((SGU_GROUPS, SGU_LEN, SGU_LEN), lambda i: (0, 0, 0)),
                  pl.BlockSpec((SGU_GROUPS, SGU_LEN, 1), lambda i: (0, 0, 0))],
        out_specs=pl.BlockSpec((tr, BRANCH_W), lambda i: (i, 0)),
        out_shape=jax.ShapeDtypeStruct((s, BRANCH_W), BF16),
        compiler_params=_cparams("parallel"),
    )(p, p, ln_g, ln_b, w, b_col)


def _sgu_bwd(p, dyb, ln_g, ln_b, w, w_t, b_col, *, name):
    s = p.shape[0]
    tr = _pick(s, 256, SGU_LEN)

    def body(u_ref, v_ref, dy_ref, g_ref, b_ref, w_ref, wt_ref, bc_ref,
             dz_ref, dg_ref, db_ref, dw_ref, dbc_ref, dvn_s):
        @pl.when(pl.program_id(0) == 0)
        def _():
            dg_ref[...] = jnp.zeros_like(dg_ref)
            db_ref[...] = jnp.zeros_like(db_ref)
            dw_ref[...] = jnp.zeros_like(dw_ref)
            dbc_ref[...] = jnp.zeros_like(dbc_ref)

        mask = _chunk_mask()
        mask_t = _chunk_mask(transposed=True)
        u_raw = u_ref[...]
        v_raw = v_ref[...]
        dy = dy_ref[...]
        zu = _gelu(u_raw)
        xh, rs, vn = _sgu_norm(v_raw, g_ref[...], b_ref[...])
        vnb = vn.astype(BF16)
        dvm_all = dy * zu
        for gi in range(SGU_GROUPS):
            wg = jnp.where(mask, w_ref[gi], 0.0).astype(BF16)
            wgt = jnp.where(mask_t, wt_ref[gi], 0.0).astype(BF16)
            cs = slice(gi * SGU_LEN, (gi + 1) * SGU_LEN)
            dw_g = jnp.zeros((SGU_LEN, SGU_LEN), F32)
            db_g = jnp.zeros((SGU_LEN, 1), F32)
            for c in range(tr // SGU_LEN):
                rs_ = slice(c * SGU_LEN, (c + 1) * SGU_LEN)
                vm = _dot(wg, vnb[rs_, cs]) + bc_ref[gi]
                dz_ref[rs_, cs] = (dy[rs_, cs] * vm * _gelu_grad(u_raw[rs_, cs])).astype(BF16)
                dvm = dvm_all[rs_, cs]
                dvmb = dvm.astype(BF16)
                dw_g = dw_g + _dot_nt(dvmb, vnb[rs_, cs])
                db_g = db_g + jnp.sum(dvm, axis=1, keepdims=True)
                dvn_s[rs_, cs] = _dot(wgt, dvmb)
            dw_ref[gi] += jnp.where(mask, dw_g, 0.0)
            dbc_ref[gi] += db_g
        dvn = dvn_s[...]
        dg_ref[...] += jnp.sum(dvn * xh, axis=0, keepdims=True)
        db_ref[...] += jnp.sum(dvn, axis=0, keepdims=True)
        dxh = dvn * g_ref[...]
        dzv = rs * (dxh - jnp.mean(dxh, axis=-1, keepdims=True) - xh * jnp.mean(dxh * xh, axis=-1, keepdims=True))
        dz_ref[:, BRANCH_W:] = (dzv * _gelu_grad(v_raw)).astype(BF16)

    vec = pl.BlockSpec((1, BRANCH_W), lambda i: (0, 0))
    wspec = pl.BlockSpec((SGU_GROUPS, SGU_LEN, SGU_LEN), lambda i: (0, 0, 0))
    bspec = pl.BlockSpec((SGU_GROUPS, SGU_LEN, 1), lambda i: (0, 0, 0))
    return pl.pallas_call(
        body, name=name, grid=(s // tr,),
        in_specs=[pl.BlockSpec((tr, BRANCH_W), lambda i: (i, 3)), pl.BlockSpec((tr, BRANCH_W), lambda i: (i, 4)),
                  pl.BlockSpec((tr, BRANCH_W), lambda i: (i, 0)), vec, vec, wspec, wspec, bspec],
        out_specs=[pl.BlockSpec((tr, 2 * BRANCH_W), lambda i: (i, 0)), vec, vec, wspec, bspec],
        out_shape=[jax.ShapeDtypeStruct((s, 2 * BRANCH_W), BF16),
                   jax.ShapeDtypeStruct((1, BRANCH_W), F32), jax.ShapeDtypeStruct((1, BRANCH_W), F32),
                   jax.ShapeDtypeStruct((SGU_GROUPS, SGU_LEN, SGU_LEN), F32),
                   jax.ShapeDtypeStruct((SGU_GROUPS, SGU_LEN, 1), F32)],
        scratch_shapes=[pltpu.VMEM((tr, BRANCH_W), F32)],
        compiler_params=_cparams("arbitrary"),
    )(p, p, dyb, ln_g, ln_b, w, w_t, b_col)


def _shift_down(x, prev8, k):
    rolled = pltpu.roll(x, k, 0)
    r8 = lax.broadcasted_iota(jnp.int32, prev8.shape, 0)
    head = jnp.where(r8 < k, pltpu.roll(prev8, k, 0), rolled[:HALO])
    return jnp.concatenate([head, rolled[HALO:]], axis=0)


def _shift_up(x, next8, k):
    n = x.shape[0]
    rolled = pltpu.roll(x, n - k, 0)
    r8 = lax.broadcasted_iota(jnp.int32, next8.shape, 0)
    tail = jnp.where(r8 >= HALO - k, pltpu.roll(next8, HALO - k, 0), rolled[n - HALO:])
    return jnp.concatenate([rolled[:n - HALO], tail], axis=0)


def _conv_specs(s, tr):
    nb = tr // HALO
    last = s // HALO - 1
    tile = lambda cb: pl.BlockSpec((tr, 128), lambda j, i: (i, cb * 4 + j))
    above = lambda cb: pl.BlockSpec((HALO, 128), lambda j, i: (jnp.maximum(i * nb - 1, 0), cb * 4 + j))
    below = lambda cb: pl.BlockSpec((HALO, 128), lambda j, i: (jnp.minimum((i + 1) * nb, last), cb * 4 + j))
    return tile, above, below


def _conv_fwd(p, cw, *, name):
    s = p.shape[0]
    tr = _pick(s, 512, HALO)
    tile, above, _ = _conv_specs(s, tr)

    def body(cb_ref, cc_ref, cx_ref, ccp_ref, cxp_ref, w_ref, o_ref):
        first = pl.program_id(1) == 0
        y = cc_ref[...] * cx_ref[...]
        yp = jnp.where(first, 0.0, ccp_ref[...] * cxp_ref[...])
        conv = w_ref[2:3, :] * y + w_ref[1:2, :] * _shift_down(y, yp, 1) + w_ref[0:1, :] * _shift_down(y, yp, 2)
        o_ref[...] = (cb_ref[...] * conv).astype(BF16)

    return pl.pallas_call(
        body, name=name, grid=(4, s // tr),
        in_specs=[tile(5), tile(6), tile(7), above(6), above(7), pl.BlockSpec((3, 128), lambda j, i: (0, j))],
        out_specs=pl.BlockSpec((tr, 128), lambda j, i: (i, j)),
        out_shape=jax.ShapeDtypeStruct((s, BRANCH_W), BF16),
        compiler_params=_cparams("parallel", "parallel"),
    )(p, p, p, p, p, cw)


def _conv_bwd(p, dyc, cw, *, name):
    s = p.shape[0]
    tr = _pick(s, 512, HALO)
    nt = s // tr
    nb = tr // HALO
    last = s // HALO - 1
    tile, above, below = _conv_specs(s, tr)

    def body(cb_ref, cc_ref, cx_ref, ccp_ref, cxp_ref, cbn_ref, dy_ref, dyn_ref, w_ref,
             dcb_ref, dcc_ref, dcx_ref, dw_ref):
        i = pl.program_id(1)

        @pl.when(i == 0)
        def _():
            dw_ref[...] = jnp.zeros_like(dw_ref)

        cb = cb_ref[...]
        cc = cc_ref[...]
        cx = cx_ref[...]
        y = cc * cx
        yp = jnp.where(i == 0, 0.0, ccp_ref[...] * cxp_ref[...])
        y1 = _shift_down(y, yp, 1)
        y2 = _shift_down(y, yp, 2)
        w0, w1, w2 = w_ref[0:1, :], w_ref[1:2, :], w_ref[2:3, :]
        conv = w2 * y + w1 * y1 + w0 * y2
        dyc_v = dy_ref[...]
        dconv = dyc_v * cb
        dn = jnp.where(i == nt - 1, 0.0, dyn_ref[...] * cbn_ref[...])
        dyv = w2 * dconv + w1 * _shift_up(dconv, dn, 1) + w0 * _shift_up(dconv, dn, 2)
        dcb_ref[...] = (dyc_v * conv).astype(BF16)
        dcc_ref[...] = (dyv * cx).astype(BF16)
        dcx_ref[...] = (dyv * cc).astype(BF16)
        dw_ref[0:1, :] += jnp.sum(dconv * y2, axis=0, keepdims=True)
        dw_ref[1:2, :] += jnp.sum(dconv * y1, axis=0, keepdims=True)
        dw_ref[2:3, :] += jnp.sum(dconv * y, axis=0, keepdims=True)

    dy_tile = pl.BlockSpec((tr, 128), lambda j, i: (i, j))
    dy_below = pl.BlockSpec((HALO, 128), lambda j, i: (jnp.minimum((i + 1) * nb, last), j))
    out_tile = lambda cb: pl.BlockSpec((tr, 128), lambda j, i: (i, cb * 4 + j))
    w_spec = pl.BlockSpec((3, 128), lambda j, i: (0, j))
    dcb, dcc, dcx, dw = pl.pallas_call(
        body, name=name, grid=(4, nt),
        in_specs=[tile(5), tile(6), tile(7), above(6), above(7), below(5), dy_tile, dy_below, w_spec],
        out_specs=[dy_tile, dy_tile, dy_tile, w_spec],
        out_shape=[jax.ShapeDtypeStruct((s, BRANCH_W), BF16)] * 3 + [jax.ShapeDtypeStruct((3, BRANCH_W), F32)],
        compiler_params=_cparams("parallel", "arbitrary"),
    )(p, p, p, p, p, p, dyc, dyc, cw)
    return dcb, dcc, dcx, dw


def _merge_fwd(ya, yb, yc, wb, p, *, name):
    s = p.shape[0]
    tr = _pick(s, 256, 16)

    def body(ya_ref, yb_ref, yc_ref, wb_ref, g0_ref, g1_ref, g2_ref, o_ref):
        acc = jnp.zeros((tr, D_MODEL), F32)
        for n, (y_ref, g_ref) in enumerate(((ya_ref, g0_ref), (yb_ref, g1_ref), (yc_ref, g2_ref))):
            acc = acc + _sigmoid(g_ref[...]) * _dot(y_ref[...].astype(BF16), wb_ref[n])
        o_ref[...] = acc.astype(BF16)

    yspec = pl.BlockSpec((tr, BRANCH_W), lambda i: (i, 0))
    gate = lambda n: pl.BlockSpec((tr, D_MODEL), lambda i: (i, 4 + n))
    return pl.pallas_call(
        body, name=name, grid=(s // tr,),
        in_specs=[yspec, yspec, yspec, pl.BlockSpec((3, BRANCH_W, D_MODEL), lambda i: (0, 0, 0)),
                  gate(0), gate(1), gate(2)],
        out_specs=pl.BlockSpec((tr, D_MODEL), lambda i: (i, 0)),
        out_shape=jax.ShapeDtypeStruct((s, D_MODEL), BF16),
        compiler_params=_cparams("parallel"),
    )(ya, yb, yc, wb, p, p, p)


def _merge_bwd(dm, ya, yb, yc, wb, p, *, name):
    s = p.shape[0]
    tr = _pick(s, 256, 16)

    def body(dm_ref, ya_ref, yb_ref, yc_ref, wb_ref, g0_ref, g1_ref, g2_ref,
             dya_ref, dyb_ref, dyc_ref, dg_ref, dbrd0_ref, dbrd1_ref, dbrd2_ref):
        dmv = dm_ref[...]
        ys = (ya_ref, yb_ref, yc_ref)
        gs = (g0_ref, g1_ref, g2_ref)
        dys = (dya_ref, dyb_ref, dyc_ref)
        dbrds = (dbrd0_ref, dbrd1_ref, dbrd2_ref)
        for n in range(3):
            brd = _dot(ys[n][...].astype(BF16), wb_ref[n])
            sg = _sigmoid(gs[n][...])
            dbrd = (sg * dmv).astype(BF16)
            dbrds[n][...] = dbrd
            dg_ref[:, n * D_MODEL:(n + 1) * D_MODEL] = (dmv * brd * sg * (1.0 - sg)).astype(BF16)
            dys[n][...] = _dot_nt(dbrd, wb_ref[n]).astype(dys[n].dtype)

    yspec = pl.BlockSpec((tr, BRANCH_W), lambda i: (i, 0))
    gate = lambda n: pl.BlockSpec((tr, D_MODEL), lambda i: (i, 4 + n))
    row = pl.BlockSpec((tr, D_MODEL), lambda i: (i, 0))
    return pl.pallas_call(
        body, name=name, grid=(s // tr,),
        in_specs=[row, yspec, yspec, yspec, pl.BlockSpec((3, BRANCH_W, D_MODEL), lambda i: (0, 0, 0)),
                  gate(0), gate(1), gate(2)],
        out_specs=[yspec, yspec, yspec, pl.BlockSpec((tr, 3 * D_MODEL), lambda i: (i, 0)), row, row, row],
        out_shape=[jax.ShapeDtypeStruct((s, BRANCH_W), BF16)] + [jax.ShapeDtypeStruct((s, BRANCH_W), F32)] * 2
                  + [jax.ShapeDtypeStruct((s, 3 * D_MODEL), BF16)] + [jax.ShapeDtypeStruct((s, D_MODEL), BF16)] * 3,
        compiler_params=_cparams("parallel"),
    )(dm, ya, yb, yc, wb, p, p, p)


def _xa_probs(q, k):
    sc = _dot_nt(q, k) * XA_SCALE
    e = jnp.exp(sc - jnp.max(sc, axis=-1, keepdims=True))
    return e / jnp.sum(e, axis=-1, keepdims=True)


def _xa_fwd(q, k, v, *, name):
    s = q.shape[0]
    mt = k.shape[0]
    tr = _pick(s, 512, 16)

    def body(q_ref, k_ref, v_ref, o_ref):
        pr = _xa_probs(q_ref[...], k_ref[...])
        o_ref[...] = _dot(pr.astype(BF16), v_ref[...]).astype(BF16)

    qs = pl.BlockSpec((tr, XA_HEAD), lambda h, i: (i, h))
    ks = pl.BlockSpec((mt, XA_HEAD), lambda h, i: (0, h))
    return pl.pallas_call(
        body, name=name, grid=(D_MODEL // XA_HEAD, s // tr),
        in_specs=[qs, ks, ks], out_specs=qs,
        out_shape=jax.ShapeDtypeStruct((s, D_MODEL), BF16),
        compiler_params=_cparams("parallel", "parallel"),
    )(q, k, v)


def _xa_bwd(q, k, v, do, *, name):
    s = q.shape[0]
    mt = k.shape[0]
    tr = _pick(s, 512, 16)

    def body(q_ref, k_ref, v_ref, do_ref, dq_ref, dk_ref, dv_ref):
        @pl.when(pl.program_id(1) == 0)
        def _():
            dk_ref[...] = jnp.zeros_like(dk_ref)
            dv_ref[...] = jnp.zeros_like(dv_ref)

        qv = q_ref[...]
        kv = k_ref[...]
        dov = do_ref[...]
        pr = _xa_probs(qv, kv)
        dpr = _dot_nt(dov, v_ref[...])
        ds = (pr * (dpr - jnp.sum(dpr * pr, axis=-1, keepdims=True)) * XA_SCALE).astype(BF16)
        dq_ref[...] = _dot(ds, kv).astype(BF16)
        dk_ref[...] += _dot_tn(ds, qv)
        dv_ref[...] += _dot_tn(pr.astype(BF16), dov)

    qs = pl.BlockSpec((tr, XA_HEAD), lambda h, i: (i, h))
    ks = pl.BlockSpec((mt, XA_HEAD), lambda h, i: (0, h))
    return pl.pallas_call(
        body, name=name, grid=(D_MODEL // XA_HEAD, s // tr),
        in_specs=[qs, ks, ks, qs], out_specs=[qs, ks, ks],
        out_shape=[jax.ShapeDtypeStruct((s, D_MODEL), BF16), jax.ShapeDtypeStruct((mt, D_MODEL), F32),
                   jax.ShapeDtypeStruct((mt, D_MODEL), F32)],
        compiler_params=_cparams("parallel", "arbitrary"),
    )(q, k, v, do)


def _ffn_in(h, wg, wu, *, name):
    s, d = h.shape
    f = wg.shape[0]
    tm = _pick(s, 1024, 128)
    tn = _pick(f, 1408, 128)

    def body(h_ref, wg_ref, wu_ref, a_ref, b_ref, o_ref):
        hv = h_ref[...]
        av = _dot_nt(hv, wg_ref[...])
        bv = _dot_nt(hv, wu_ref[...])
        a_ref[...] = av.astype(BF16)
        b_ref[...] = bv.astype(BF16)
        o_ref[...] = (av * _sigmoid(av) * bv).astype(BF16)

    wspec = pl.BlockSpec((tn, d), lambda i, j: (j, 0))
    tile = pl.BlockSpec((tm, tn), lambda i, j: (i, j))
    return pl.pallas_call(
        body, name=name, grid=(s // tm, f // tn),
        in_specs=[pl.BlockSpec((tm, d), lambda i, j: (i, 0)), wspec, wspec],
        out_specs=[tile, tile, tile], out_shape=[jax.ShapeDtypeStruct((s, f), BF16)] * 3,
        compiler_params=_cparams("parallel", "parallel"),
    )(h, wg, wu)


def _swiglu_bwd(a, b, dh, *, name):
    s, f = a.shape
    tr = _pick(s, 256, 16)

    def body(a_ref, b_ref, dh_ref, da_ref, db_ref):
        av = a_ref[...].astype(F32)
        dhv = dh_ref[...]
        sg = _sigmoid(av)
        silu = av * sg
        da_ref[...] = (dhv * b_ref[...].astype(F32) * (sg + silu * (1.0 - sg))).astype(BF16)
        db_ref[...] = (dhv * silu).astype(BF16)

    spec = pl.BlockSpec((tr, f), lambda i: (i, 0))
    return pl.pallas_call(
        body, name=name, grid=(s // tr,), in_specs=[spec, spec, spec], out_specs=[spec, spec],
        out_shape=[jax.ShapeDtypeStruct((s, f), BF16)] * 2, compiler_params=_cparams("parallel"),
    )(a, b, dh)


def _adamw(w, g, m, v, *, name):
    r, c = w.shape
    tr = _pick(r, 512, 8)

    def body(w_ref, g_ref, m_ref, v_ref, d_ref, mo_ref, vo_ref):
        gv = g_ref[...]
        mn = ADAM_B1 * m_ref[...] + (1.0 - ADAM_B1) * gv
        vn = ADAM_B2 * v_ref[...] + (1.0 - ADAM_B2) * (gv * gv)
        m_hat = mn / (1.0 - ADAM_B1 ** ADAM_STEP)
        v_hat = vn / (1.0 - ADAM_B2 ** ADAM_STEP)
        d_ref[...] = -ADAM_LR * (m_hat / (jnp.sqrt(v_hat) + ADAM_EPS) + ADAM_WD * w_ref[...])
        mo_ref[...] = mn
        vo_ref[...] = vn

    spec = pl.BlockSpec((tr, c), lambda i: (i, 0))
    shp = jax.ShapeDtypeStruct((r, c), F32)
    return pl.pallas_call(
        body, name=name, grid=(r // tr,), in_specs=[spec] * 4, out_specs=[spec] * 3,
        out_shape=[shp] * 3, compiler_params=_cparams("parallel"),
    )(w, g, m, v)


def _position():
    return lax.axis_index("x"), lax.axis_index("y"), lax.axis_index("c")


def _all_gather(x, *, name):
    t, c_ = x.shape

    def body(x_ref, out_ref, send_sems, recv_sems, local_sem):
        start, forward, finish = _gather_phases(x_ref, out_ref, send_sems, recv_sems, local_sem)
        start()
        forward()
        finish()

    return pl.pallas_call(
        body, name=name,
        out_shape=jax.ShapeDtypeStruct((N_DEV, t, c_), x.dtype),
        in_specs=[pl.BlockSpec(memory_space=pl.ANY)],
        out_specs=pl.BlockSpec(memory_space=pl.ANY),
        scratch_shapes=_GATHER_SEMS,
    )(x)


_GATHER_SEMS = [pltpu.SemaphoreType.DMA((7,)), pltpu.SemaphoreType.DMA((7,)), pltpu.SemaphoreType.DMA]


def _gather_phases(x_ref, out_ref, send_sems, recv_sems, local_sem):
    x_, y_, c = _position()
    me, sibling = (x_, y_, c), (x_, y_, 1 - c)
    chips = [(1 - x_, y_), (x_, 1 - y_), (1 - x_, 1 - y_)]

    def block(px, py, pc):
        return out_ref.at[4 * px + 2 * py + pc]

    def copy(k, blk, to, src=None):
        return pltpu.make_async_remote_copy(
            src_ref=block(*blk) if src is None else src, dst_ref=block(*blk),
            send_sem=send_sems.at[k], recv_sem=recv_sems.at[k], device_id=to, device_id_type=MESH)

    mine = pltpu.make_async_copy(x_ref, block(*me), local_sem)
    first = [copy(0, me, sibling, src=x_ref)]
    first += [copy(1 + j, me, (*chip, c), src=x_ref) for j, chip in enumerate(chips)]
    passed = [copy(4 + j, (*chip, c), sibling) for j, chip in enumerate(chips)]

    def start():
        mine.start()
        for cp in first:
            cp.start()

    def forward():
        for j, chip in enumerate(chips):
            copy(1 + j, (*chip, c), me).wait_recv()
            passed[j].start()

    def finish():
        copy(0, sibling, me).wait_recv()
        for j, chip in enumerate(chips):
            copy(4 + j, (*chip, 1 - c), me).wait_recv()
        for cp in first + passed:
            cp.wait_send()
        mine.wait()

    return start, forward, finish


_SCATTER_SEMS = [pltpu.SemaphoreType.DMA((7,)), pltpu.SemaphoreType.DMA((7,)), pltpu.SemaphoreType.DMA]


def _scatter_phases(g_ref, r_ref, send_sems, recv_sems, local_sem):
    x_, y_, c = _position()
    me = 4 * x_ + 2 * y_ + c
    local = pltpu.make_async_copy(g_ref.at[me], r_ref.at[me], local_sem)
    copies = []
    for k in range(1, N_DEV):
        to = (x_ ^ (k >> 2), y_ ^ ((k >> 1) & 1), c ^ (k & 1))
        copies.append(pltpu.make_async_remote_copy(
            src_ref=g_ref.at[me ^ k], dst_ref=r_ref.at[me], send_sem=send_sems.at[k - 1],
            recv_sem=recv_sems.at[k - 1], device_id=to, device_id_type=MESH))

    def start():
        local.start()
        for cp in copies:
            cp.start()

    def finish():
        for k in range(1, N_DEV):
            pltpu.make_async_remote_copy(
                src_ref=g_ref.at[me], dst_ref=r_ref.at[me ^ k], send_sem=send_sems.at[k - 1],
                recv_sem=recv_sems.at[k - 1], device_id=(x_, y_, c), device_id_type=MESH).wait_recv()
        for cp in copies:
            cp.wait_send()
        local.wait()

    return start, finish


def _sum_devices(r8, *, name):
    _, t, c_ = r8.shape
    tr = _pick(t, 256, 16)

    def body(r_ref, o_ref):
        acc = r_ref[0].astype(F32)
        for d in range(1, N_DEV):
            acc = acc + r_ref[d].astype(F32)
        o_ref[...] = acc

    return pl.pallas_call(
        body, name=name, grid=(t // tr,),
        in_specs=[pl.BlockSpec((N_DEV, tr, c_), lambda i: (0, i, 0))],
        out_specs=pl.BlockSpec((tr, c_), lambda i: (i, 0)),
        out_shape=jax.ShapeDtypeStruct((t, c_), F32),
        compiler_params=_cparams("parallel"),
    )(r8)


def _all_reduce_small(x, *, name):
    r, c_ = x.shape

    def body(x_ref, o_ref, buf, send_sems, recv_sems):
        x_, y_, c = _position()
        me = 4 * x_ + 2 * y_ + c
        buf[me] = x_ref[...]
        copies = []
        for k in range(1, N_DEV):
            to = (x_ ^ (k >> 2), y_ ^ ((k >> 1) & 1), c ^ (k & 1))
            copies.append(pltpu.make_async_remote_copy(
                src_ref=x_ref, dst_ref=buf.at[me], send_sem=send_sems.at[k - 1], recv_sem=recv_sems.at[k - 1],
                device_id=to, device_id_type=MESH))
        for cp in copies:
            cp.start()
        for k in range(1, N_DEV):
            src = me ^ k
            pltpu.make_async_remote_copy(
                src_ref=x_ref, dst_ref=buf.at[src], send_sem=send_sems.at[k - 1], recv_sem=recv_sems.at[k - 1],
                device_id=(x_, y_, c), device_id_type=MESH).wait_recv()
        for cp in copies:
            cp.wait_send()
        acc = buf[0]
        for d in range(1, N_DEV):
            acc = acc + buf[d]
        o_ref[...] = acc

    return pl.pallas_call(
        body, name=name,
        out_shape=jax.ShapeDtypeStruct((r, c_), F32),
        in_specs=[pl.BlockSpec(memory_space=pltpu.VMEM)],
        out_specs=pl.BlockSpec(memory_space=pltpu.VMEM),
        scratch_shapes=[pltpu.VMEM((N_DEV, r, c_), F32), pltpu.SemaphoreType.DMA((7,)), pltpu.SemaphoreType.DMA((7,))],
    )(x)


def _rs_pair_exchange(g8, *, name):
    _, t, c_ = g8.shape

    def body(g_ref, r_ref, send_sems, recv_sems):
        x_, y_, c = _position()
        copies = [pltpu.make_async_remote_copy(
            src_ref=g_ref.at[2 * ch + (1 - c)], dst_ref=r_ref.at[ch],
            send_sem=send_sems.at[ch], recv_sem=recv_sems.at[ch],
            device_id=(x_, y_, 1 - c), device_id_type=MESH) for ch in range(4)]
        for cp in copies:
            cp.start()
        for cp in copies:
            cp.wait()

    return pl.pallas_call(
        body, name=name,
        out_shape=jax.ShapeDtypeStruct((4, t, c_), g8.dtype),
        in_specs=[pl.BlockSpec(memory_space=pl.ANY)],
        out_specs=pl.BlockSpec(memory_space=pl.ANY),
        scratch_shapes=[pltpu.SemaphoreType.DMA((4,)), pltpu.SemaphoreType.DMA((4,))],
    )(g8)


def _pair_add(core, g8, recv, *, name):
    _, t, c_ = g8.shape
    tr = _pick(t, 512, 16)

    def body(core_ref, g_ref, r_ref, o_ref):
        o_ref[...] = (g_ref[...].astype(F32) + r_ref[...].astype(F32)).astype(o_ref.dtype)

    grid_spec = pltpu.PrefetchScalarGridSpec(
        num_scalar_prefetch=1, grid=(4, t // tr),
        in_specs=[pl.BlockSpec((None, tr, c_), lambda ch, i, core_ref: (2 * ch + core_ref[0], i, 0)),
                  pl.BlockSpec((None, tr, c_), lambda ch, i, core_ref: (ch, i, 0))],
        out_specs=pl.BlockSpec((None, tr, c_), lambda ch, i, core_ref: (ch, i, 0)))
    return pl.pallas_call(
        body, name=name, grid_spec=grid_spec,
        out_shape=jax.ShapeDtypeStruct((4, t, c_), g8.dtype),
        compiler_params=_cparams("parallel", "parallel"),
    )(core, g8, recv)


def _rs_chip_exchange(part, *, name):
    _, t, c_ = part.shape

    def body(p_ref, r_ref, send_sems, recv_sems, local_sem):
        x_, y_, c = _position()
        mine = 2 * x_ + y_
        local = pltpu.make_async_copy(p_ref.at[mine], r_ref.at[mine], local_sem)
        local.start()
        chips = [(1 - x_, y_), (x_, 1 - y_), (1 - x_, 1 - y_)]
        copies = [pltpu.make_async_remote_copy(
            src_ref=p_ref.at[2 * px + py], dst_ref=r_ref.at[mine],
            send_sem=send_sems.at[k], recv_sem=recv_sems.at[k],
            device_id=(px, py, c), device_id_type=MESH) for k, (px, py) in enumerate(chips)]
        for cp in copies:
            cp.start()
        for k, (px, py) in enumerate(chips):
            pltpu.make_async_remote_copy(
                src_ref=p_ref.at[mine], dst_ref=r_ref.at[2 * px + py],
                send_sem=send_sems.at[k], recv_sem=recv_sems.at[k],
                device_id=(x_, y_, c), device_id_type=MESH).wait_recv()
        for cp in copies:
            cp.wait_send()
        local.wait()

    return pl.pallas_call(
        body, name=name,
        out_shape=jax.ShapeDtypeStruct((4, t, c_), part.dtype),
        in_specs=[pl.BlockSpec(memory_space=pl.ANY)],
        out_specs=pl.BlockSpec(memory_space=pl.ANY),
        scratch_shapes=[pltpu.SemaphoreType.DMA((3,)), pltpu.SemaphoreType.DMA((3,)), pltpu.SemaphoreType.DMA],
    )(part)


def _sum_chips(r4, *, name):
    _, t, c_ = r4.shape
    tr = _pick(t, 512, 16)

    def body(r_ref, o_ref):
        acc = r_ref[0].astype(F32)
        for ch in range(1, 4):
            acc = acc + r_ref[ch].astype(F32)
        o_ref[...] = acc

    return pl.pallas_call(
        body, name=name, grid=(t // tr,),
        in_specs=[pl.BlockSpec((4, tr, c_), lambda i: (0, i, 0))],
        out_specs=pl.BlockSpec((tr, c_), lambda i: (i, 0)),
        out_shape=jax.ShapeDtypeStruct((t, c_), F32),
        compiler_params=_cparams("parallel"),
    )(r4)


BIG = (
    ("w_in", (IN_COLS // N_DEV, D_MODEL), 0),
    ("w_branch", (3, BRANCH_W, D_MODEL // N_DEV), 2),
    ("w_out", (D_MODEL // N_DEV, D_MODEL), 0),
    ("w_q_xa", (D_MODEL // N_DEV, D_MODEL), 0),
    ("w_k_xa", (D_MODEL // N_DEV, D_MODEL), 0),
    ("w_v_xa", (D_MODEL // N_DEV, D_MODEL), 0),
    ("w_o_xa", (D_MODEL // N_DEV, D_MODEL), 0),
    ("w_gate_ffn", (FFN // N_DEV, D_MODEL), 0),
    ("w_up_ffn", (FFN // N_DEV, D_MODEL), 0),
    ("w_down_ffn", (FFN // N_DEV, D_MODEL), 0),
)
TRANSPOSED = ("w_in", "w_gate_ffn", "w_up_ffn")
_BIG_LAYOUT = {n: (shp, ax) for n, shp, ax in BIG}
PACK_COLS = 1024


def _stored(name, shard):
    return shard.T if name in TRANSPOSED else shard


def _size(shape):
    n = 1
    for d in shape:
        n *= d
    return n


def _pack_shards(items, shards):
    return jnp.concatenate([shards[it].reshape(-1, PACK_COLS) for it in items], axis=0)


def _unpack_gathered(items, g):
    out = {}
    r0 = 0
    for it in items:
        shp, ax = _BIG_LAYOUT[it[0]]
        rows = _size(shp) // PACK_COLS
        blk = g[:, r0:r0 + rows].reshape((N_DEV,) + shp)
        r0 += rows
        blk = jnp.moveaxis(blk, 0, ax)
        full = list(shp)
        full[ax] = shp[ax] * N_DEV
        out[it] = blk.reshape(full)
    return out


def _pack_full(items, full):
    parts = []
    for it in items:
        shp, ax = _BIG_LAYOUT[it[0]]
        t = full[it].reshape(shp[:ax] + (N_DEV, shp[ax]) + shp[ax + 1:])
        t = jnp.moveaxis(t, ax, 0)
        parts.append(t.reshape(N_DEV, -1, PACK_COLS))
    return jnp.concatenate(parts, axis=1)


def _unpack_shard(items, flat):
    out = {}
    r0 = 0
    for it in items:
        shp, _ = _BIG_LAYOUT[it[0]]
        rows = _size(shp) // PACK_COLS
        out[it] = flat[r0:r0 + rows].reshape(shp)
        r0 += rows
    return out


SMALL = (
    ("norm_mix_g", (DEPTH, D_MODEL)),
    ("sgu_ln_g", (DEPTH, BRANCH_W)),
    ("sgu_ln_b", (DEPTH, BRANCH_W)),
    ("w_spatial", (DEPTH, SGU_GROUPS, SGU_LEN, SGU_LEN)),
    ("b_spatial", (DEPTH, SGU_GROUPS, SGU_LEN)),
    ("conv_w", (DEPTH, 3, BRANCH_W)),
    ("norm_xa_g", (DEPTH, D_MODEL)),
    ("mem_norm_g", (DEPTH, D_MODEL)),
    ("norm_ffn_g", (DEPTH, D_MODEL)),
    ("final_g", (D_MODEL,)),
)


def _pack_small(grads):
    flat = jnp.concatenate([grads[n].reshape(-1) for n, _ in SMALL])
    rows = -(-flat.shape[0] // PACK_COLS)
    rows = -(-rows // 8) * 8
    flat = jnp.pad(flat, (0, rows * PACK_COLS - flat.shape[0]))
    return flat.reshape(rows, PACK_COLS)


def _unpack_small(buf):
    flat = buf.reshape(-1)
    out = {}
    o = 0
    for n, shp in SMALL:
        out[n] = flat[o:o + _size(shp)].reshape(shp)
        o += _size(shp)
    return out


def _layer_fwd(l, x, mem, wt, sm, gather=None):
    t = f"l{l}_"
    sv = {"x0": x}
    h = _rms_fwd(x, sm["norm_mix_g"][l][None], name=t + "rms_mix")
    p = _mm(h, wt["w_in", l], tb=True, name=t + "in_proj", tm=2048)
    if gather is None:
        ya = _sb_fwd(p, name=t + "sb_fwd")
    else:
        ya, gathered = _sb_fwd(p, name=t + "sb_fwd", gather=gather[1])
        wt.update(_unpack_gathered(gather[0], gathered))
    w_sp = sm["w_spatial"][l]
    b_col = sm["b_spatial"][l][:, :, None]
    ln_g, ln_b = sm["sgu_ln_g"][l][None], sm["sgu_ln_b"][l][None]
    yb = _sgu_fwd(p, ln_g, ln_b, w_sp, b_col, name=t + "sgu_fwd")
    yc = _conv_fwd(p, sm["conv_w"][l], name=t + "conv_fwd")
    merged = _merge_fwd(ya, yb, yc, wt["w_branch", l], p, name=t + "merge_fwd")
    x1 = _mm(merged, wt["w_out", l], add=x, name=t + "out_proj")
    sv.update(h=h, p=p, ya=ya, yb=yb, yc=yc, merged=merged, x1=x1)

    h2 = _rms_fwd(x1, sm["norm_xa_g"][l][None], name=t + "rms_xa")
    mn = _rms_fwd(mem, sm["mem_norm_g"][l][None], name=t + "rms_mem")
    q = _mm(h2, wt["w_q_xa", l], out_dtype=BF16, name=t + "xa_q", tm=2048)
    k = _mm(mn, wt["w_k_xa", l], out_dtype=BF16, name=t + "xa_k")
    v = _mm(mn, wt["w_v_xa", l], out_dtype=BF16, name=t + "xa_v")
    o = _xa_fwd(q, k, v, name=t + "xa_fwd")
    x2 = _mm(o, wt["w_o_xa", l], add=x1, name=t + "xa_o")
    sv.update(h2=h2, mn=mn, q=q, k=k, v=v, o=o, x2=x2)

    h3 = _rms_fwd(x2, sm["norm_ffn_g"][l][None], name=t + "rms_ffn")
    a, b, hd = _ffn_in(h3, wt["w_gate_ffn", l], wt["w_up_ffn", l], name=t + "ffn_in")
    x3 = _mm(hd, wt["w_down_ffn", l], add=x2, name=t + "ffn_down", tk=FFN)
    sv.update(h3=h3, a=a, b=b, hd=hd)
    return x3, sv


def _layer_bwd(l, dx3, mem, wt, sm, sv, scatter=None):
    t = f"l{l}_b_"
    gb, gs = {}, {}
    dhd = _mm(dx3, wt["w_down_ffn", l], tb=True, name=t + "ffn_down_dx", tn=1408)
    gb["w_down_ffn"] = _mm(sv["hd"], dx3, ta=True, out_dtype=BF16, name=t + "ffn_down_dw", tm=1408)
    da, db = _swiglu_bwd(sv["a"], sv["b"], dhd, name=t + "swiglu_bwd")
    gb["w_gate_ffn"] = _mm(da, sv["h3"], ta=True, out_dtype=BF16, name=t + "ffn_gate_dw", tm=1408)
    gb["w_up_ffn"] = _mm(db, sv["h3"], ta=True, out_dtype=BF16, name=t + "ffn_up_dw", tm=1408)
    dh3 = _mm(da, wt["w_gate_ffn", l], name=t + "ffn_gate_dx", tk=1408)
    dx2, dg = _mm(db, wt["w_up_ffn", l], add=dh3, rms=(sv["x2"], sm["norm_ffn_g"][l][None], dx3),
                  name=t + "ffn_up_dx", tm=512, tk=1408)
    gs["norm_ffn_g"] = dg[0]
    do = _mm(dx2, wt["w_o_xa", l], tb=True, out_dtype=BF16, name=t + "xa_o_dx")
    gb["w_o_xa"] = _mm(sv["o"], dx2, ta=True, out_dtype=BF16, name=t + "xa_o_dw")
    dq, dk, dv = _xa_bwd(sv["q"], sv["k"], sv["v"], do, name=t + "xa_bwd")
    dx1, dg = _mm(dq, wt["w_q_xa", l], tb=True, rms=(sv["x1"], sm["norm_xa_g"][l][None], dx2),
                  name=t + "xa_q_dx", tm=512)
    gs["norm_xa_g"] = dg[0]
    gb["w_q_xa"] = _mm(sv["h2"], dq, ta=True, out_dtype=BF16, name=t + "xa_q_dw")
    gb["w_k_xa"] = _mm(sv["mn"], dk, ta=True, out_dtype=BF16, name=t + "xa_k_dw")
    gb["w_v_xa"] = _mm(sv["mn"], dv, ta=True, out_dtype=BF16, name=t + "xa_v_dw")
    dmn = _mm(dk, wt["w_k_xa", l], tb=True, name=t + "xa_k_dx")
    dmn = _mm(dv, wt["w_v_xa", l], tb=True, add=dmn, name=t + "xa_v_dx")
    _, dg = _rms_bwd(mem, sm["mem_norm_g"][l][None], dmn, jnp.zeros_like(mem), name=t + "rms_mem")
    gs["mem_norm_g"] = dg[0]
    dm = _mm(dx1, wt["w_out", l], tb=True, name=t + "out_proj_dx")
    gb["w_out"] = _mm(sv["merged"], dx1, ta=True, out_dtype=BF16, name=t + "out_proj_dw")
    p = sv["p"]
    dya, dyb, dyc, dgates, *dbrd = _merge_bwd(dm, sv["ya"], sv["yb"], sv["yc"], wt["w_branch", l], p,
                                              name=t + "merge_bwd")
    gb["w_branch"] = jnp.stack([
        _mm(sv[y], dbrd[n], ta=True, out_dtype=BF16, name=t + f"branch{n}_dw")
        for n, y in enumerate(("ya", "yb", "yc"))])
    dcb, dcc, dcx, dcw = _conv_bwd(p, dyc, sm["conv_w"][l], name=t + "conv_bwd")
    gs["conv_w"] = dcw
    w_sp = sm["w_spatial"][l]
    dz, dlg, dlb, dwsp, dbsp = _sgu_bwd(p, dyb, sm["sgu_ln_g"][l][None], sm["sgu_ln_b"][l][None], w_sp,
                                        jnp.swapaxes(w_sp, 1, 2), sm["b_spatial"][l][:, :, None],
                                        name=t + "sgu_bwd")
    gs.update(sgu_ln_g=dlg[0], sgu_ln_b=dlb[0], w_spatial=dwsp, b_spatial=dbsp[:, :, 0])
    received = None
    if scatter is None:
        dq_a, dk_a, dv_a = _sb_bwd(p, dya, name=t + "sb_bwd")
    else:
        items, earlier = scatter
        ready = {**earlier, **{(n, l): g for n, g in gb.items()}}
        dq_a, dk_a, dv_a, received = _sb_bwd(p, dya, name=t + "sb_bwd", scatter=_pack_full(items, ready))
    dp = jnp.concatenate([dq_a, dk_a, dv_a, dz, dcb, dcc, dcx, dgates], axis=1)
    gb["w_in"] = _mm(dp, sv["h"], ta=True, out_dtype=BF16, name=t + "in_proj_dw")
    dx, dg = _mm(dp, wt["w_in", l], rms=(sv["x0"], sm["norm_mix_g"][l][None], dx1),
                 name=t + "in_proj_dx", tm=512, tk=1792)
    gs["norm_mix_g"] = dg[0]
    return dx, gb, gs, received


_WEIGHTS = ("norm_mix_g", "w_in", "sgu_ln_g", "sgu_ln_b", "w_spatial", "b_spatial", "conv_w", "w_branch", "w_out",
            "norm_xa_g", "mem_norm_g", "w_q_xa", "w_k_xa", "w_v_xa", "w_o_xa", "norm_ffn_g", "w_gate_ffn",
            "w_up_ffn", "w_down_ffn", "final_g")


def kernel(x, mem, norm_mix_g, w_in, sgu_ln_g, sgu_ln_b, w_spatial, b_spatial, conv_w, w_branch, w_out, norm_xa_g, mem_norm_g, w_q_xa, w_k_xa, w_v_xa, w_o_xa, norm_ffn_g, w_gate_ffn, w_up_ffn, w_down_ffn, final_g, loss_target, m_norm_mix_g, m_w_in, m_sgu_ln_g, m_sgu_ln_b, m_w_spatial, m_b_spatial, m_conv_w, m_w_branch, m_w_out, m_norm_xa_g, m_mem_norm_g, m_w_q_xa, m_w_k_xa, m_w_v_xa, m_w_o_xa, m_norm_ffn_g, m_w_gate_ffn, m_w_up_ffn, m_w_down_ffn, m_final_g, v_norm_mix_g, v_w_in, v_sgu_ln_g, v_sgu_ln_b, v_w_spatial, v_b_spatial, v_conv_w, v_w_branch, v_w_out, v_norm_xa_g, v_mem_norm_g, v_w_q_xa, v_w_k_xa, v_w_v_xa, v_w_o_xa, v_norm_ffn_g, v_w_gate_ffn, v_w_up_ffn, v_w_down_ffn, v_final_g):
    w = dict(norm_mix_g=norm_mix_g, w_in=w_in, sgu_ln_g=sgu_ln_g, sgu_ln_b=sgu_ln_b, w_spatial=w_spatial,
             b_spatial=b_spatial, conv_w=conv_w, w_branch=w_branch, w_out=w_out, norm_xa_g=norm_xa_g,
             mem_norm_g=mem_norm_g, w_q_xa=w_q_xa, w_k_xa=w_k_xa, w_v_xa=w_v_xa, w_o_xa=w_o_xa,
             norm_ffn_g=norm_ffn_g, w_gate_ffn=w_gate_ffn, w_up_ffn=w_up_ffn, w_down_ffn=w_down_ffn, final_g=final_g)
    m = dict(norm_mix_g=m_norm_mix_g, w_in=m_w_in, sgu_ln_g=m_sgu_ln_g, sgu_ln_b=m_sgu_ln_b, w_spatial=m_w_spatial,
             b_spatial=m_b_spatial, conv_w=m_conv_w, w_branch=m_w_branch, w_out=m_w_out, norm_xa_g=m_norm_xa_g,
             mem_norm_g=m_mem_norm_g, w_q_xa=m_w_q_xa, w_k_xa=m_w_k_xa, w_v_xa=m_w_v_xa, w_o_xa=m_w_o_xa,
             norm_ffn_g=m_norm_ffn_g, w_gate_ffn=m_w_gate_ffn, w_up_ffn=m_w_up_ffn, w_down_ffn=m_w_down_ffn,
             final_g=m_final_g)
    v = dict(norm_mix_g=v_norm_mix_g, w_in=v_w_in, sgu_ln_g=v_sgu_ln_g, sgu_ln_b=v_sgu_ln_b, w_spatial=v_w_spatial,
             b_spatial=v_b_spatial, conv_w=v_conv_w, w_branch=v_w_branch, w_out=v_w_out, norm_xa_g=v_norm_xa_g,
             mem_norm_g=v_mem_norm_g, w_q_xa=v_w_q_xa, w_k_xa=v_w_k_xa, w_v_xa=v_w_v_xa, w_o_xa=v_w_o_xa,
             norm_ffn_g=v_norm_ffn_g, w_gate_ffn=v_w_gate_ffn, w_up_ffn=v_w_up_ffn, w_down_ffn=v_w_down_ffn,
             final_g=v_final_g)

    names = [n for n, _, _ in BIG]
    shards = {(n, l): _stored(n, w[n][l].astype(BF16)) for n in names for l in range(DEPTH)}
    first_items = [("w_in", 0)]
    later_items = [(n, l) for l in range(DEPTH) for n in names if (n, l) != ("w_in", 0)]
    wt = _unpack_gathered(first_items, _all_gather(_pack_shards(first_items, shards), name="gather_w_in0"))
    cw_pad = jnp.zeros((8, 128), F32).at[:DEPTH * 3, :BRANCH_W // N_DEV].set(conv_w.reshape(DEPTH * 3, -1))
    cw_all = _all_gather(cw_pad, name="gather_conv_w")[:, :DEPTH * 3, :BRANCH_W // N_DEV]
    conv_full = jnp.moveaxis(cw_all.reshape(N_DEV, DEPTH, 3, BRANCH_W // N_DEV), 0, 2).reshape(DEPTH, 3, BRANCH_W)
    sm = {n: w[n] for n, _ in SMALL}
    sm["conv_w"] = conv_full

    xs, ms = x[0], mem[0]
    x1, saved0 = _layer_fwd(0, xs, ms, wt, sm, gather=(later_items, _pack_shards(later_items, shards)))
    x2, saved1 = _layer_fwd(1, x1, ms, wt, sm)
    dcur, loss, dfinal = _final_loss(x2, sm["final_g"][None], loss_target[0], name="final_loss")
    loss = lax.psum(loss[0, 0], AXES)
    items_a = [(n, 1) for n in names if n != "w_in"]
    items_b = [("w_in", 1)] + [(n, 0) for n in names if n != "w_in"]
    items_c = [("w_in", 0)]
    dcur, gb1, gs1, recv_a = _layer_bwd(1, dcur, ms, wt, sm, saved1, scatter=(items_a, {}))
    dx, gb0, gs0, recv_b = _layer_bwd(0, dcur, ms, wt, sm, saved0, scatter=(items_b, {("w_in", 1): gb1["w_in"]}))

    shard_grads = _unpack_shard(items_a, _sum_devices(recv_a, name="rs_sum_a"))
    shard_grads.update(_unpack_shard(items_b, _sum_devices(recv_b, name="rs_sum_b")))
    g8 = _pack_full(items_c, {("w_in", 0): gb0["w_in"]})
    core = lax.axis_index("c").astype(jnp.int32).reshape(1)
    from_sibling = _rs_pair_exchange(g8, name="rs_pair_exchange")
    part = _pair_add(core, g8, from_sibling, name="rs_pair_add")
    by_chip = _rs_chip_exchange(part, name="rs_chip_exchange")
    shard_grads.update(_unpack_shard(items_c, _sum_chips(by_chip, name="rs_sum_chips")))
    grads = {n: jnp.stack([_stored(n, shard_grads[n, l]) for l in range(DEPTH)]) for n in names}
    small = {n: jnp.stack([gs0[n], gs1[n]]) for n, _ in SMALL if n != "final_g"}
    small["final_g"] = dfinal[0]
    small_sum = _unpack_small(_all_reduce_small(_pack_small(small), name="all_reduce_small"))
    width = BRANCH_W // N_DEV
    dev = 4 * lax.axis_index("x") + 2 * lax.axis_index("y") + lax.axis_index("c")
    for n, _ in SMALL:
        grads[n] = small_sum[n]
    grads["conv_w"] = lax.dynamic_slice_in_dim(small_sum["conv_w"], dev * width, width, axis=2)

    delta, new_m, new_v = {}, {}, {}
    for n in _WEIGHTS:
        shp = w[n].shape
        two_d = (-1, shp[-1])
        d_, m_, v_ = _adamw(w[n].reshape(two_d), grads[n].reshape(two_d), m[n].reshape(two_d), v[n].reshape(two_d),
                            name="adamw_" + n)
        delta[n], new_m[n], new_v[n] = d_.reshape(shp), m_.reshape(shp), v_.reshape(shp)

    return (loss, dx[None], *[grads[n] for n in _WEIGHTS], *[delta[n] for n in _WEIGHTS],
            *[new_m[n] for n in _WEIGHTS], *[new_v[n] for n in _WEIGHTS])
```

```python
import functools

import jax
import jax.numpy as jnp
from jax import lax
from jax.experimental import pallas as pl
from jax.experimental.pallas import tpu as pltpu

F32 = jnp.float32
BF16 = jnp.bfloat16
MESH = pl.DeviceIdType.MESH

D_MODEL = 1024
BRANCH_W = 512
IN_COLS = 7168
FFN = 2816
N_DEV = 8
DEPTH = 2
SB_BLOCK = 128
SB_SPAN = 1024
SB_Q_FWD = 512
SB_Q_BWD = 256
SB_SCALE = 0.125
XA_HEAD = 256
XA_SCALE = 0.0625
SGU_LEN = 128
SGU_GROUPS = 4
RMS_EPS = 1e-6
LN_EPS = 1e-5
HALO = 8

ADAM_LR = 0.001
ADAM_B1 = 0.9
ADAM_B2 = 0.999
ADAM_EPS = 1e-08
ADAM_WD = 0.01
ADAM_STEP = 10

VMEM_LIMIT_BYTES = 52 * 1024 * 1024

AXES = ("x", "y", "c")


def _cparams(*sem):
    return pltpu.CompilerParams(dimension_semantics=sem, vmem_limit_bytes=VMEM_LIMIT_BYTES)


def _pick(n, target, align):
    t = (min(target, n) // align) * align
    while t >= align:
        if n % t == 0:
            return t
        t -= align
    return n


def _dot(a, b):
    return jnp.dot(a, b, preferred_element_type=F32)


def _dot_nt(a, b):
    return lax.dot_general(a, b, (((1,), (1,)), ((), ())), preferred_element_type=F32)


def _dot_tn(a, b):
    return lax.dot_general(a, b, (((0,), (0,)), ((), ())), preferred_element_type=F32)


def _sigmoid(x):
    return 1.0 / (1.0 + jnp.exp(-x))


def _mm(a, b, *, name, ta=False, tb=False, out_dtype=F32, add=None, rms=None, tm=1024, tn=1024, tk=2048):
    m, k = (a.shape[1], a.shape[0]) if ta else a.shape
    n = b.shape[0] if tb else b.shape[1]
    assert k == (b.shape[1] if tb else b.shape[0])
    tm = _pick(m, tm, 128)
    tn = n if rms is not None else _pick(n, tn, 128)
    tk = _pick(k, tk, 128)
    nk = k // tk
    ca = 0 if ta else 1
    cb = 1 if tb else 0
    n_add = 0 if add is None else 1
    n_rms = 0 if rms is None else 3

    def body(*refs):
        refs = list(refs)
        a_ref, b_ref = refs[:2]
        extra = refs[2:2 + n_add + n_rms]
        outs = refs[2 + n_add + n_rms:]
        o_ref = outs[0]
        kk = pl.program_id(2)
        first_row_tile = pl.program_id(0) == 0

        def product():
            return lax.dot_general(a_ref[...].astype(BF16), b_ref[...].astype(BF16),
                                   (((ca,), (cb,)), ((), ())), preferred_element_type=F32)

        def finish(r):
            if add is not None:
                r = r + extra[0][...]
            if rms is None:
                o_ref[...] = r.astype(out_dtype)
                return
            x_ref, g_ref, dres_ref = extra[n_add:]
            dg_ref = outs[1]

            @pl.when(first_row_tile)
            def _():
                dg_ref[...] = jnp.zeros_like(dg_ref)

            xv = x_ref[...]
            rs = lax.rsqrt(jnp.mean(xv * xv, axis=-1, keepdims=True) + RMS_EPS)
            xh = xv * rs
            dg_ref[...] += jnp.sum(r * xh, axis=0, keepdims=True)
            dxh = r * g_ref[...]
            o_ref[...] = dres_ref[...] + rs * (dxh - xh * jnp.mean(dxh * xh, axis=-1, keepdims=True))

        if nk == 1:
            finish(product())
        else:
            acc_ref = outs[-1]

            @pl.when(kk == 0)
            def _():
                acc_ref[...] = jnp.zeros_like(acc_ref)

            acc_ref[...] += product()

            @pl.when(kk == nk - 1)
            def _():
                finish(acc_ref[...])

    a_spec = pl.BlockSpec((tk, tm), lambda i, j, kk: (kk, i)) if ta else pl.BlockSpec((tm, tk), lambda i, j, kk: (i, kk))
    b_spec = pl.BlockSpec((tn, tk), lambda i, j, kk: (j, kk)) if tb else pl.BlockSpec((tk, tn), lambda i, j, kk: (kk, j))
    tile = pl.BlockSpec((tm, tn), lambda i, j, kk: (i, j))
    in_specs = [a_spec, b_spec]
    operands = [a, b]
    out_specs = [tile]
    out_shape = [jax.ShapeDtypeStruct((m, n), out_dtype)]
    if add is not None:
        in_specs.append(tile)
        operands.append(add)
    if rms is not None:
        vec = pl.BlockSpec((1, n), lambda i, j, kk: (0, 0))
        in_specs += [tile, vec, tile]
        operands += list(rms)
        out_specs.append(vec)
        out_shape = [jax.ShapeDtypeStruct((m, n), F32), jax.ShapeDtypeStruct((1, n), F32)]
    out = pl.pallas_call(
        body, name=name,
        grid=(m // tm, n // tn, nk),
        in_specs=in_specs, out_specs=out_specs, out_shape=out_shape,
        scratch_shapes=[pltpu.VMEM((tm, tn), F32)] if nk > 1 else [],
        compiler_params=_cparams("arbitrary" if rms is not None else "parallel", "parallel", "arbitrary"),
    )(*operands)
    return out[0] if rms is None else out


def _rms_fwd(x, g, *, name):
    r, d = x.shape
    tr = _pick(r, 512, 16)

    def body(x_ref, g_ref, o_ref):
        xv = x_ref[...]
        rs = lax.rsqrt(jnp.mean(xv * xv, axis=-1, keepdims=True) + RMS_EPS)
        o_ref[...] = (xv * rs * g_ref[...]).astype(BF16)

    return pl.pallas_call(
        body, name=name, grid=(r // tr,),
        in_specs=[pl.BlockSpec((tr, d), lambda i: (i, 0)), pl.BlockSpec((1, d), lambda i: (0, 0))],
        out_specs=pl.BlockSpec((tr, d), lambda i: (i, 0)),
        out_shape=jax.ShapeDtypeStruct((r, d), BF16),
        compiler_params=_cparams("parallel"),
    )(x, g)


def _rms_bwd(x, g, dh, dres, *, name):
    r, d = x.shape
    tr = _pick(r, 256, 8)

    def body(x_ref, g_ref, dh_ref, dres_ref, dx_ref, dg_ref):
        @pl.when(pl.program_id(0) == 0)
        def _():
            dg_ref[...] = jnp.zeros_like(dg_ref)

        xv = x_ref[...]
        dhv = dh_ref[...].astype(F32)
        rs = lax.rsqrt(jnp.mean(xv * xv, axis=-1, keepdims=True) + RMS_EPS)
        xh = xv * rs
        dg_ref[...] += jnp.sum(dhv * xh, axis=0, keepdims=True)
        dxh = dhv * g_ref[...]
        dx_ref[...] = dres_ref[...] + rs * (dxh - xh * jnp.mean(dxh * xh, axis=-1, keepdims=True))

    return pl.pallas_call(
        body, name=name, grid=(r // tr,),
        in_specs=[pl.BlockSpec((tr, d), lambda i: (i, 0)), pl.BlockSpec((1, d), lambda i: (0, 0)),
                  pl.BlockSpec((tr, d), lambda i: (i, 0)), pl.BlockSpec((tr, d), lambda i: (i, 0))],
        out_specs=[pl.BlockSpec((tr, d), lambda i: (i, 0)), pl.BlockSpec((1, d), lambda i: (0, 0))],
        out_shape=[jax.ShapeDtypeStruct((r, d), F32), jax.ShapeDtypeStruct((1, d), F32)],
        compiler_params=_cparams("arbitrary"),
    )(x, g, dh, dres)


def _final_loss(x, g, target, *, name):
    r, d = x.shape
    tr = _pick(r, 256, 8)

    def body(x_ref, g_ref, t_ref, dx_ref, loss_ref, dg_ref):
        @pl.when(pl.program_id(0) == 0)
        def _():
            dg_ref[...] = jnp.zeros_like(dg_ref)
            loss_ref[...] = jnp.zeros_like(loss_ref)

        xv = x_ref[...]
        gv = g_ref[...]
        rs = lax.rsqrt(jnp.mean(xv * xv, axis=-1, keepdims=True) + RMS_EPS)
        xh = xv * rs
        err = xh * gv - t_ref[...]
        row_loss = jnp.mean(err * err, axis=-1, keepdims=True)
        loss_ref[...] += 0.5 * jnp.sum(row_loss, axis=0, keepdims=True)
        dy = err * (1.0 / d)
        dg_ref[...] += jnp.sum(dy * xh, axis=0, keepdims=True)
        dxh = dy * gv
        dx_ref[...] = rs * (dxh - xh * jnp.mean(dxh * xh, axis=-1, keepdims=True))

    return pl.pallas_call(
        body, name=name, grid=(r // tr,),
        in_specs=[pl.BlockSpec((tr, d), lambda i: (i, 0)), pl.BlockSpec((1, d), lambda i: (0, 0)),
                  pl.BlockSpec((tr, d), lambda i: (i, 0))],
        out_specs=[pl.BlockSpec((tr, d), lambda i: (i, 0)), pl.BlockSpec((1, 128), lambda i: (0, 0)),
                   pl.BlockSpec((1, d), lambda i: (0, 0))],
        out_shape=[jax.ShapeDtypeStruct((r, d), F32), jax.ShapeDtypeStruct((1, 128), F32),
                   jax.ShapeDtypeStruct((1, d), F32)],
        compiler_params=_cparams("arbitrary"),
    )(x, g, target)


def _cumsum_operand(strict_after, terms=1):
    r = lax.broadcasted_iota(jnp.int32, (terms * SB_BLOCK, 2 * SB_BLOCK), 0) % SB_BLOCK
    c = lax.broadcasted_iota(jnp.int32, (terms * SB_BLOCK, 2 * SB_BLOCK), 1)
    tri = (r > c) if strict_after else (r < c)
    return jnp.where((c >= SB_BLOCK) | tri, 1.0, 0.0).astype(BF16)


def _sb_scores(qh, kw, run, valid, after_ones):
    nb = kw.shape[0] // SB_BLOCK
    z = _dot_nt(qh, kw)
    lsp = jnp.minimum(z, 0.0) - jnp.log(1.0 + jnp.exp(-jnp.abs(z)))
    l1m = lsp - z
    if valid is not None:
        l1m = jnp.where(valid, l1m, 0.0)
    l1b = l1m.astype(BF16)
    later = [None] * nb
    for b in reversed(range(nb)):
        cols = slice(b * SB_BLOCK, (b + 1) * SB_BLOCK)
        ct = _dot(l1b[:, cols], after_ones)
        later[b] = run + ct[:, :SB_BLOCK]
        run = run + ct[:, SB_BLOCK:]
    a = jnp.exp(lsp + jnp.concatenate(later, axis=1))
    if valid is not None:
        a = jnp.where(valid, a, 0.0)
    return lsp, a, run


def _sb_setup(q_ref, span):
    qi = pl.program_id(1)
    rows = q_ref.shape[0]
    sd = (qi * rows + rows - 1) // span
    lane = lax.broadcasted_iota(jnp.int32, (rows, SB_BLOCK), 1)
    col = lax.broadcasted_iota(jnp.int32, (rows, span), 1)
    row = lax.broadcasted_iota(jnp.int32, (rows, span), 0)
    valid = col < (qi * rows - sd * span) + row
    q = q_ref[...] * SB_SCALE
    qhs = (jnp.where(lane < 64, q, 0.0).astype(BF16), jnp.where(lane >= 64, q, 0.0).astype(BF16))
    return lane, sd, valid, qhs


def _sb_fwd(p, *, name, gather=None):
    s = p.shape[0]
    qrows = min(SB_Q_FWD, s)
    nq = s // qrows
    kcol = BRANCH_W // SB_BLOCK
    span = min(SB_SPAN, s)

    def body(*refs):
        if gather is None:
            q_ref, k_ref, v_ref, o_ref = refs
        else:
            q_ref, k_ref, v_ref, x_ref, o_ref, g_ref, send_sems, recv_sems, local_sem = refs
            start, forward, finish = _gather_phases(x_ref, g_ref, send_sems, recv_sems, local_sem)
            step = pl.program_id(0) * nq + pl.program_id(1)
            pl.when(step == 0)(start)
        lane, sd, valid, qhs = _sb_setup(q_ref, span)
        after_ones = _cumsum_operand(True)
        zero = jnp.zeros((qrows, SB_BLOCK), F32)

        def span_step(sb, carry, mask):
            rows = pl.ds(pl.multiple_of(sb * span, span), span)
            kw = k_ref[rows, :].astype(BF16)
            vw = v_ref[rows, :].astype(BF16)
            out = []
            for h in range(2):
                run, acc = carry[h]
                _, a, run = _sb_scores(qhs[h], kw, run, mask, after_ones)
                out.append((run, acc + _dot(a.astype(BF16), vw)))
            return tuple(out)

        carry = span_step(sd, ((zero, zero), (zero, zero)), valid)
        carry = lax.fori_loop(0, sd, lambda t, c: span_step(sd - 1 - t, c, None), carry)
        o_ref[...] = jnp.where(lane < 64, carry[0][1], carry[1][1]).astype(BF16)
        if gather is not None:
            pl.when(step == (kcol - 1) * nq + (3 * nq) // 4)(forward)
            pl.when(step == kcol * nq - 1)(finish)

    in_specs = [pl.BlockSpec((qrows, SB_BLOCK), lambda hp, qi: (qi, hp)),
                pl.BlockSpec((s, SB_BLOCK), lambda hp, qi: (0, kcol + hp)),
                pl.BlockSpec((s, SB_BLOCK), lambda hp, qi: (0, 2 * kcol + hp))]
    out_specs = [pl.BlockSpec((qrows, SB_BLOCK), lambda hp, qi: (qi, hp))]
    out_shape = [jax.ShapeDtypeStruct((s, BRANCH_W), BF16)]
    operands = [p, p, p]
    scratch = []
    if gather is not None:
        in_specs.append(pl.BlockSpec(memory_space=pl.ANY))
        out_specs.append(pl.BlockSpec(memory_space=pl.ANY))
        out_shape.append(jax.ShapeDtypeStruct((N_DEV,) + gather.shape, gather.dtype))
        operands.append(gather)
        scratch = _GATHER_SEMS
    out = pl.pallas_call(
        body, name=name, grid=(kcol, nq), in_specs=in_specs, out_specs=out_specs, out_shape=out_shape,
        scratch_shapes=scratch, compiler_params=_cparams("arbitrary", "arbitrary"),
    )(*operands)
    return out[0] if gather is None else out


def _sb_bwd(p, dya, *, name, scatter=None):
    s = p.shape[0]
    qrows = min(SB_Q_BWD, s)
    nq = s // qrows
    kcol = BRANCH_W // SB_BLOCK
    span = min(SB_SPAN, s)
    per = span // SB_BLOCK

    def body(*refs):
        if scatter is None:
            q_ref, k_ref, v_ref, do_ref, dq_ref, dk_ref, dv_ref, a_s, b_s, dk_acc, dv_acc = refs
        else:
            (q_ref, k_ref, v_ref, do_ref, g_ref, dq_ref, dk_ref, dv_ref, r_ref,
             a_s, b_s, dk_acc, dv_acc, send_sems, recv_sems, local_sem) = refs
            start, finish = _scatter_phases(g_ref, r_ref, send_sems, recv_sems, local_sem)
            step = pl.program_id(0) * nq + pl.program_id(1)
            pl.when(step == 0)(start)
        qi = pl.program_id(1)

        @pl.when(qi == 0)
        def _():
            dk_acc[...] = jnp.zeros_like(dk_acc)
            dv_acc[...] = jnp.zeros_like(dv_acc)

        lane, sd, valid, qhs = _sb_setup(q_ref, span)
        after_ones = _cumsum_operand(True)
        before_ones = _cumsum_operand(False)
        do = do_ref[...]
        dohs = (jnp.where(lane < 64, do, 0.0).astype(BF16), jnp.where(lane >= 64, do, 0.0).astype(BF16))
        zero = jnp.zeros((qrows, SB_BLOCK), F32)

        def rebuild(sb, runs, mask):
            rows = pl.ds(pl.multiple_of(sb * span, span), span)
            kw = k_ref[rows, :].astype(BF16)
            out = []
            for h in range(2):
                lsp, a, run = _sb_scores(qhs[h], kw, runs[h], mask, after_ones)
                beta = jnp.exp(lsp)
                if mask is not None:
                    beta = jnp.where(mask, beta, 0.0)
                a_s[h, sb] = a
                b_s[h, sb] = beta
                out.append(run)
            return tuple(out)

        runs = rebuild(sd, (zero, zero), valid)
        lax.fori_loop(0, sd, lambda t, r: rebuild(sd - 1 - t, r, None), runs)

        def accumulate(sb, carry):
            rows = pl.ds(pl.multiple_of(sb * span, span), span)
            kw = k_ref[rows, :].astype(BF16)
            vw = v_ref[rows, :].astype(BF16)
            out = []
            dk_span = jnp.zeros((span, SB_BLOCK), F32)
            dv_span = jnp.zeros((span, SB_BLOCK), F32)
            for h in range(2):
                pg, dq = carry[h]
                a = a_s[h, sb]
                beta = b_s[h, sb]
                g = a * _dot_nt(dohs[h], vw)
                gb = g.astype(BF16)
                before = [None] * per
                for b in range(per):
                    cols = slice(b * SB_BLOCK, (b + 1) * SB_BLOCK)
                    gt = _dot(gb[:, cols], before_ones)
                    before[b] = pg + gt[:, :SB_BLOCK]
                    pg = pg + gt[:, SB_BLOCK:]
                dz = (g * (1.0 - beta) - beta * jnp.concatenate(before, axis=1)).astype(BF16)
                dk_span = dk_span + _dot_tn(dz, qhs[h])
                dv_span = dv_span + _dot_tn(a.astype(BF16), dohs[h])
                out.append((pg, dq + _dot(dz, kw)))
            dk_acc[rows, :] += dk_span
            dv_acc[rows, :] += dv_span
            return tuple(out)

        carry = lax.fori_loop(0, sd + 1, accumulate, ((zero, zero), (zero, zero)))
        dq_ref[...] = (jnp.where(lane < 64, carry[0][1], carry[1][1]) * SB_SCALE).astype(BF16)

        @pl.when(qi == nq - 1)
        def _():
            dk_ref[...] = dk_acc[...].astype(BF16)
            dv_ref[...] = dv_acc[...].astype(BF16)

        if scatter is not None:
            pl.when(step == kcol * nq - 1)(finish)

    blk = pl.BlockSpec((qrows, SB_BLOCK), lambda hp, qi: (qi, hp))
    col = pl.BlockSpec((s, SB_BLOCK), lambda hp, qi: (0, hp))
    out = jax.ShapeDtypeStruct((s, BRANCH_W), BF16)
    in_specs = [blk,
                pl.BlockSpec((s, SB_BLOCK), lambda hp, qi: (0, kcol + hp)),
                pl.BlockSpec((s, SB_BLOCK), lambda hp, qi: (0, 2 * kcol + hp)),
                blk]
    out_specs = [blk, col, col]
    out_shape = [out, out, out]
    operands = [p, p, p, dya]
    scratch = [pltpu.VMEM((2, s // span, qrows, span), F32), pltpu.VMEM((2, s // span, qrows, span), F32),
               pltpu.VMEM((s, SB_BLOCK), F32), pltpu.VMEM((s, SB_BLOCK), F32)]
    if scatter is not None:
        in_specs.append(pl.BlockSpec(memory_space=pl.ANY))
        out_specs.append(pl.BlockSpec(memory_space=pl.ANY))
        out_shape.append(jax.ShapeDtypeStruct(scatter.shape, scatter.dtype))
        operands.append(scatter)
        scratch = scratch + _SCATTER_SEMS
    return pl.pallas_call(
        body, name=name, grid=(kcol, nq), in_specs=in_specs, out_specs=out_specs, out_shape=out_shape,
        scratch_shapes=scratch, compiler_params=_cparams("arbitrary", "arbitrary"),
    )(*operands)


_INV_SQRT2 = 0.7071067811865476
_INV_SQRT2PI = 0.3989422804014327


def _gelu(x):
    return 0.5 * x * (1.0 + lax.erf(x * _INV_SQRT2))


def _gelu_grad(x):
    return 0.5 * (1.0 + lax.erf(x * _INV_SQRT2)) + x * _INV_SQRT2PI * jnp.exp(-0.5 * x * x)


def _chunk_mask(transposed=False):
    r = lax.broadcasted_iota(jnp.int32, (SGU_LEN, SGU_LEN), 0)
    c = lax.broadcasted_iota(jnp.int32, (SGU_LEN, SGU_LEN), 1)
    return (c // 64) >= (r // 64) if transposed else (r // 64) >= (c // 64)


def _sgu_norm(v_raw, g, b):
    zv = _gelu(v_raw)
    xc = zv - jnp.mean(zv, axis=-1, keepdims=True)
    rs = lax.rsqrt(jnp.mean(xc * xc, axis=-1, keepdims=True) + LN_EPS)
    xh = xc * rs
    return xh, rs, xh * g + b


def _sgu_fwd(p, ln_g, ln_b, w, b_col, *, name):
    s = p.shape[0]
    tr = _pick(s, 512, SGU_LEN)

    def body(u_ref, v_ref, g_ref, b_ref, w_ref, bc_ref, o_ref):
        mask = _chunk_mask()
        zu = _gelu(u_ref[...])
        _, _, vn = _sgu_norm(v_ref[...], g_ref[...], b_ref[...])
        vnb = vn.astype(BF16)
        for gi in range(SGU_GROUPS):
            wg = jnp.where(mask, w_ref[gi], 0.0).astype(BF16)
            cs = slice(gi * SGU_LEN, (gi + 1) * SGU_LEN)
            for c in range(tr // SGU_LEN):
                rs_ = slice(c * SGU_LEN, (c + 1) * SGU_LEN)
                vm = _dot(wg, vnb[rs_, cs]) + bc_ref[gi]
                o_ref[rs_, cs] = (zu[rs_, cs] * vm).astype(BF16)

    vec = pl.BlockSpec((1, BRANCH_W), lambda i: (0, 0))
    return pl.pallas_call(
        body, name=name, grid=(s // tr,),
        in_specs=[pl.BlockSpec((tr, BRANCH_W), lambda i: (i, 3)), pl.BlockSpec((tr, BRANCH_W), lambda i: (i, 4)),
                  vec, vec,
                  pl.BlockSpec((SGU_GROUPS, SGU_LEN, SGU_LEN), lambda i: (0, 0, 0)),
                  pl.BlockSpec((SGU_GROUPS, SGU_LEN, 1), lambda i: (0, 0, 0))],
        out_specs=pl.BlockSpec((tr, BRANCH_W), lambda i: (i, 0)),
        out_shape=jax.ShapeDtypeStruct((s, BRANCH_W), BF16),
        compiler_params=_cparams("parallel"),
    )(p, p, ln_g, ln_b, w, b_col)


def _sgu_bwd(p, dyb, ln_g, ln_b, w, w_t, b_col, *, name):
    s = p.shape[0]
    tr = _pick(s, 256, SGU_LEN)

    def body(u_ref, v_ref, dy_ref, g_ref, b_ref, w_ref, wt_ref, bc_ref,
             dz_ref, dg_ref, db_ref, dw_ref, dbc_ref, dvn_s):
        @pl.when(pl.program_id(0) == 0)
        def _():
            dg_ref[...] = jnp.zeros_like(dg_ref)
            db_ref[...] = jnp.zeros_like(db_ref)
            dw_ref[...] = jnp.zeros_like(dw_ref)
            dbc_ref[...] = jnp.zeros_like(dbc_ref)

        mask = _chunk_mask()
        mask_t = _chunk_mask(transposed=True)
        u_raw = u_ref[...]
        v_raw = v_ref[...]
        dy = dy_ref[...]
        zu = _gelu(u_raw)
        xh, rs, vn = _sgu_norm(v_raw, g_ref[...], b_ref[...])
        vnb = vn.astype(BF16)
        dvm_all = dy * zu
        for gi in range(SGU_GROUPS):
            wg = jnp.where(mask, w_ref[gi], 0.0).astype(BF16)
            wgt = jnp.where(mask_t, wt_ref[gi], 0.0).astype(BF16)
            cs = slice(gi * SGU_LEN, (gi + 1) * SGU_LEN)
            dw_g = jnp.zeros((SGU_LEN, SGU_LEN), F32)
            db_g = jnp.zeros((SGU_LEN, 1), F32)
            for c in range(tr // SGU_LEN):
                rs_ = slice(c * SGU_LEN, (c + 1) * SGU_LEN)
                vm = _dot(wg, vnb[rs_, cs]) + bc_ref[gi]
                dz_ref[rs_, cs] = (dy[rs_, cs] * vm * _gelu_grad(u_raw[rs_, cs])).astype(BF16)
                dvm = dvm_all[rs_, cs]
                dvmb = dvm.astype(BF16)
                dw_g = dw_g + _dot_nt(dvmb, vnb[rs_, cs])
                db_g = db_g + jnp.sum(dvm, axis=1, keepdims=True)
                dvn_s[rs_, cs] = _dot(wgt, dvmb)
            dw_ref[gi] += jnp.where(mask, dw_g, 0.0)
            dbc_ref[gi] += db_g
        dvn = dvn_s[...]
        dg_ref[...] += jnp.sum(dvn * xh, axis=0, keepdims=True)
        db_ref[...] += jnp.sum(dvn, axis=0, keepdims=True)
        dxh = dvn * g_ref[...]
        dzv = rs * (dxh - jnp.mean(dxh, axis=-1, keepdims=True) - xh * jnp.mean(dxh * xh, axis=-1, keepdims=True))
        dz_ref[:, BRANCH_W:] = (dzv * _gelu_grad(v_raw)).astype(BF16)

    vec = pl.BlockSpec((1, BRANCH_W), lambda i: (0, 0))
    wspec = pl.BlockSpec((SGU_GROUPS, SGU_LEN, SGU_LEN), lambda i: (0, 0, 0))
    bspec = pl.BlockSpec((SGU_GROUPS, SGU_LEN, 1), lambda i: (0, 0, 0))
    return pl.pallas_call(
        body, name=name, grid=(s // tr,),
        in_specs=[pl.BlockSpec((tr, BRANCH_W), lambda i: (i, 3)), pl.BlockSpec((tr, BRANCH_W), lambda i: (i, 4)),
                  pl.BlockSpec((tr, BRANCH_W), lambda i: (i, 0)), vec, vec, wspec, wspec, bspec],
        out_specs=[pl.BlockSpec((tr, 2 * BRANCH_W), lambda i: (i, 0)), vec, vec, wspec, bspec],
        out_shape=[jax.ShapeDtypeStruct((s, 2 * BRANCH_W), BF16),
                   jax.ShapeDtypeStruct((1, BRANCH_W), F32), jax.ShapeDtypeStruct((1, BRANCH_W), F32),
                   jax.ShapeDtypeStruct((SGU_GROUPS, SGU_LEN, SGU_LEN), F32),
                   jax.ShapeDtypeStruct((SGU_GROUPS, SGU_LEN, 1), F32)],
        scratch_shapes=[pltpu.VMEM((tr, BRANCH_W), F32)],
        compiler_params=_cparams("arbitrary"),
    )(p, p, dyb, ln_g, ln_b, w, w_t, b_col)


def _shift_down(x, prev8, k):
    rolled = pltpu.roll(x, k, 0)
    r8 = lax.broadcasted_iota(jnp.int32, prev8.shape, 0)
    head = jnp.where(r8 < k, pltpu.roll(prev8, k, 0), rolled[:HALO])
    return jnp.concatenate([head, rolled[HALO:]], axis=0)


def _shift_up(x, next8, k):
    n = x.shape[0]
    rolled = pltpu.roll(x, n - k, 0)
    r8 = lax.broadcasted_iota(jnp.int32, next8.shape, 0)
    tail = jnp.where(r8 >= HALO - k, pltpu.roll(next8, HALO - k, 0), rolled[n - HALO:])
    return jnp.concatenate([rolled[:n - HALO], tail], axis=0)


def _conv_specs(s, tr):
    nb = tr // HALO
    last = s // HALO - 1
    tile = lambda cb: pl.BlockSpec((tr, 128), lambda j, i: (i, cb * 4 + j))
    above = lambda cb: pl.BlockSpec((HALO, 128), lambda j, i: (jnp.maximum(i * nb - 1, 0), cb * 4 + j))
    below = lambda cb: pl.BlockSpec((HALO, 128), lambda j, i: (jnp.minimum((i + 1) * nb, last), cb * 4 + j))
    return tile, above, below


def _conv_fwd(p, cw, *, name):
    s = p.shape[0]
    tr = _pick(s, 512, HALO)
    tile, above, _ = _conv_specs(s, tr)

    def body(cb_ref, cc_ref, cx_ref, ccp_ref, cxp_ref, w_ref, o_ref):
        first = pl.program_id(1) == 0
        y = cc_ref[...] * cx_ref[...]
        yp = jnp.where(first, 0.0, ccp_ref[...] * cxp_ref[...])
        conv = w_ref[2:3, :] * y + w_ref[1:2, :] * _shift_down(y, yp, 1) + w_ref[0:1, :] * _shift_down(y, yp, 2)
        o_ref[...] = (cb_ref[...] * conv).astype(BF16)

    return pl.pallas_call(
        body, name=name, grid=(4, s // tr),
        in_specs=[tile(5), tile(6), tile(7), above(6), above(7), pl.BlockSpec((3, 128), lambda j, i: (0, j))],
        out_specs=pl.BlockSpec((tr, 128), lambda j, i: (i, j)),
        out_shape=jax.ShapeDtypeStruct((s, BRANCH_W), BF16),
        compiler_params=_cparams("parallel", "parallel"),
    )(p, p, p, p, p, cw)


def _conv_bwd(p, dyc, cw, *, name):
    s = p.shape[0]
    tr = _pick(s, 512, HALO)
    nt = s // tr
    nb = tr // HALO
    last = s // HALO - 1
    tile, above, below = _conv_specs(s, tr)

    def body(cb_ref, cc_ref, cx_ref, ccp_ref, cxp_ref, cbn_ref, dy_ref, dyn_ref, w_ref,
             dcb_ref, dcc_ref, dcx_ref, dw_ref):
        i = pl.program_id(1)

        @pl.when(i == 0)
        def _():
            dw_ref[...] = jnp.zeros_like(dw_ref)

        cb = cb_ref[...]
        cc = cc_ref[...]
        cx = cx_ref[...]
        y = cc * cx
        yp = jnp.where(i == 0, 0.0, ccp_ref[...] * cxp_ref[...])
        y1 = _shift_down(y, yp, 1)
        y2 = _shift_down(y, yp, 2)
        w0, w1, w2 = w_ref[0:1, :], w_ref[1:2, :], w_ref[2:3, :]
        conv = w2 * y + w1 * y1 + w0 * y2
        dyc_v = dy_ref[...]
        dconv = dyc_v * cb
        dn = jnp.where(i == nt - 1, 0.0, dyn_ref[...] * cbn_ref[...])
        dyv = w2 * dconv + w1 * _shift_up(dconv, dn, 1) + w0 * _shift_up(dconv, dn, 2)
        dcb_ref[...] = (dyc_v * conv).astype(BF16)
        dcc_ref[...] = (dyv * cx).astype(BF16)
        dcx_ref[...] = (dyv * cc).astype(BF16)
        dw_ref[0:1, :] += jnp.sum(dconv * y2, axis=0, keepdims=True)
        dw_ref[1:2, :] += jnp.sum(dconv * y1, axis=0, keepdims=True)
        dw_ref[2:3, :] += jnp.sum(dconv * y, axis=0, keepdims=True)

    dy_tile = pl.BlockSpec((tr, 128), lambda j, i: (i, j))
    dy_below = pl.BlockSpec((HALO, 128), lambda j, i: (jnp.minimum((i + 1) * nb, last), j))
    out_tile = lambda cb: pl.BlockSpec((tr, 128), lambda j, i: (i, cb * 4 + j))
    w_spec = pl.BlockSpec((3, 128), lambda j, i: (0, j))
    dcb, dcc, dcx, dw = pl.pallas_call(
        body, name=name, grid=(4, nt),
        in_specs=[tile(5), tile(6), tile(7), above(6), above(7), below(5), dy_tile, dy_below, w_spec],
        out_specs=[dy_tile, dy_tile, dy_tile, w_spec],
        out_shape=[jax.ShapeDtypeStruct((s, BRANCH_W), BF16)] * 3 + [jax.ShapeDtypeStruct((3, BRANCH_W), F32)],
        compiler_params=_cparams("parallel", "arbitrary"),
    )(p, p, p, p, p, p, dyc, dyc, cw)
    return dcb, dcc, dcx, dw


def _merge_fwd(ya, yb, yc, wb, p, *, name):
    s = p.shape[0]
    tr = _pick(s, 256, 16)

    def body(ya_ref, yb_ref, yc_ref, wb_ref, g0_ref, g1_ref, g2_ref, o_ref):
        acc = jnp.zeros((tr, D_MODEL), F32)
        for n, (y_ref, g_ref) in enumerate(((ya_ref, g0_ref), (yb_ref, g1_ref), (yc_ref, g2_ref))):
            acc = acc + _sigmoid(g_ref[...]) * _dot(y_ref[...].astype(BF16), wb_ref[n])
        o_ref[...] = acc.astype(BF16)

    yspec = pl.BlockSpec((tr, BRANCH_W), lambda i: (i, 0))
    gate = lambda n: pl.BlockSpec((tr, D_MODEL), lambda i: (i, 4 + n))
    return pl.pallas_call(
        body, name=name, grid=(s // tr,),
        in_specs=[yspec, yspec, yspec, pl.BlockSpec((3, BRANCH_W, D_MODEL), lambda i: (0, 0, 0)),
                  gate(0), gate(1), gate(2)],
        out_specs=pl.BlockSpec((tr, D_MODEL), lambda i: (i, 0)),
        out_shape=jax.ShapeDtypeStruct((s, D_MODEL), BF16),
        compiler_params=_cparams("parallel"),
    )(ya, yb, yc, wb, p, p, p)


def _merge_bwd(dm, ya, yb, yc, wb, p, *, name):
    s = p.shape[0]
    tr = _pick(s, 256, 16)

    def body(dm_ref, ya_ref, yb_ref, yc_ref, wb_ref, g0_ref, g1_ref, g2_ref,
             dya_ref, dyb_ref, dyc_ref, dg_ref, dbrd0_ref, dbrd1_ref, dbrd2_ref):
        dmv = dm_ref[...]
        ys = (ya_ref, yb_ref, yc_ref)
        gs = (g0_ref, g1_ref, g2_ref)
        dys = (dya_ref, dyb_ref, dyc_ref)
        dbrds = (dbrd0_ref, dbrd1_ref, dbrd2_ref)
        for n in range(3):
            brd = _dot(ys[n][...].astype(BF16), wb_ref[n])
            sg = _sigmoid(gs[n][...])
            dbrd = (sg * dmv).astype(BF16)
            dbrds[n][...] = dbrd
            dg_ref[:, n * D_MODEL:(n + 1) * D_MODEL] = (dmv * brd * sg * (1.0 - sg)).astype(BF16)
            dys[n][...] = _dot_nt(dbrd, wb_ref[n]).astype(dys[n].dtype)

    yspec = pl.BlockSpec((tr, BRANCH_W), lambda i: (i, 0))
    gate = lambda n: pl.BlockSpec((tr, D_MODEL), lambda i: (i, 4 + n))
    row = pl.BlockSpec((tr, D_MODEL), lambda i: (i, 0))
    return pl.pallas_call(
        body, name=name, grid=(s // tr,),
        in_specs=[row, yspec, yspec, yspec, pl.BlockSpec((3, BRANCH_W, D_MODEL), lambda i: (0, 0, 0)),
                  gate(0), gate(1), gate(2)],
        out_specs=[yspec, yspec, yspec, pl.BlockSpec((tr, 3 * D_MODEL), lambda i: (i, 0)), row, row, row],
        out_shape=[jax.ShapeDtypeStruct((s, BRANCH_W), BF16)] + [jax.ShapeDtypeStruct((s, BRANCH_W), F32)] * 2
                  + [jax.ShapeDtypeStruct((s, 3 * D_MODEL), BF16)] + [jax.ShapeDtypeStruct((s, D_MODEL), BF16)] * 3,
        compiler_params=_cparams("parallel"),
    )(dm, ya, yb, yc, wb, p, p, p)


def _xa_probs(q, k):
    sc = _dot_nt(q, k) * XA_SCALE
    e = jnp.exp(sc - jnp.max(sc, axis=-1, keepdims=True))
    return e / jnp.sum(e, axis=-1, keepdims=True)


def _xa_fwd(q, k, v, *, name):
    s = q.shape[0]
    mt = k.shape[0]
    tr = _pick(s, 512, 16)

    def body(q_ref, k_ref, v_ref, o_ref):
        pr = _xa_probs(q_ref[...], k_ref[...])
        o_ref[...] = _dot(pr.astype(BF16), v_ref[...]).astype(BF16)

    qs = pl.BlockSpec((tr, XA_HEAD), lambda h, i: (i, h))
    ks = pl.BlockSpec((mt, XA_HEAD), lambda h, i: (0, h))
    return pl.pallas_call(
        body, name=name, grid=(D_MODEL // XA_HEAD, s // tr),
        in_specs=[qs, ks, ks], out_specs=qs,
        out_shape=jax.ShapeDtypeStruct((s, D_MODEL), BF16),
        compiler_params=_cparams("parallel", "parallel"),
    )(q, k, v)


def _xa_bwd(q, k, v, do, *, name):
    s = q.shape[0]
    mt = k.shape[0]
    tr = _pick(s, 512, 16)

    def body(q_ref, k_ref, v_ref, do_ref, dq_ref, dk_ref, dv_ref):
        @pl.when(pl.program_id(1) == 0)
        def _():
            dk_ref[...] = jnp.zeros_like(dk_ref)
            dv_ref[...] = jnp.zeros_like(dv_ref)

        qv = q_ref[...]
        kv = k_ref[...]
        dov = do_ref[...]
        pr = _xa_probs(qv, kv)
        dpr = _dot_nt(dov, v_ref[...])
        ds = (pr * (dpr - jnp.sum(dpr * pr, axis=-1, keepdims=True)) * XA_SCALE).astype(BF16)
        dq_ref[...] = _dot(ds, kv).astype(BF16)
        dk_ref[...] += _dot_tn(ds, qv)
        dv_ref[...] += _dot_tn(pr.astype(BF16), dov)

    qs = pl.BlockSpec((tr, XA_HEAD), lambda h, i: (i, h))
    ks = pl.BlockSpec((mt, XA_HEAD), lambda h, i: (0, h))
    return pl.pallas_call(
        body, name=name, grid=(D_MODEL // XA_HEAD, s // tr),
        in_specs=[qs, ks, ks, qs], out_specs=[qs, ks, ks],
        out_shape=[jax.ShapeDtypeStruct((s, D_MODEL), BF16), jax.ShapeDtypeStruct((mt, D_MODEL), F32),
                   jax.ShapeDtypeStruct((mt, D_MODEL), F32)],
        compiler_params=_cparams("parallel", "arbitrary"),
    )(q, k, v, do)


def _ffn_in(h, wg, wu, *, name):
    s, d = h.shape
    f = wg.shape[0]
    tm = _pick(s, 1024, 128)
    tn = _pick(f, 1408, 128)

    def body(h_ref, wg_ref, wu_ref, a_ref, b_ref, o_ref):
        hv = h_ref[...]
        av = _dot_nt(hv, wg_ref[...])
        bv = _dot_nt(hv, wu_ref[...])
        a_ref[...] = av.astype(BF16)
        b_ref[...] = bv.astype(BF16)
        o_ref[...] = (av * _sigmoid(av) * bv).astype(BF16)

    wspec = pl.BlockSpec((tn, d), lambda i, j: (j, 0))
    tile = pl.BlockSpec((tm, tn), lambda i, j: (i, j))
    return pl.pallas_call(
        body, name=name, grid=(s // tm, f // tn),
        in_specs=[pl.BlockSpec((tm, d), lambda i, j: (i, 0)), wspec, wspec],
        out_specs=[tile, tile, tile], out_shape=[jax.ShapeDtypeStruct((s, f), BF16)] * 3,
        compiler_params=_cparams("parallel", "parallel"),
    )(h, wg, wu)


def _ffn_in_bwd(dx, wd, a, b, *, name):
    s, d = dx.shape
    f = wd.shape[0]
    tm = _pick(s, 1024, 128)
    tn = _pick(f, 1408, 128)

    def body(dx_ref, wd_ref, a_ref, b_ref, da_ref, db_ref):
        dhv = _dot_nt(dx_ref[...].astype(BF16), wd_ref[...])
        av = a_ref[...].astype(F32)
        sg = _sigmoid(av)
        silu = av * sg
        da_ref[...] = (dhv * b_ref[...].astype(F32) * (sg + silu * (1.0 - sg))).astype(BF16)
        db_ref[...] = (dhv * silu).astype(BF16)

    tile = pl.BlockSpec((tm, tn), lambda i, j: (i, j))
    return pl.pallas_call(
        body, name=name, grid=(s // tm, f // tn),
        in_specs=[pl.BlockSpec((tm, d), lambda i, j: (i, 0)), pl.BlockSpec((tn, d), lambda i, j: (j, 0)), tile, tile],
        out_specs=[tile, tile], out_shape=[jax.ShapeDtypeStruct((s, f), BF16)] * 2,
        compiler_params=_cparams("parallel", "parallel"),
    )(dx, wd, a, b)


def _adamw(w, g, m, v, *, name):
    r, c = w.shape
    tr = _pick(r, 512, 8)

    def body(w_ref, g_ref, m_ref, v_ref, d_ref, mo_ref, vo_ref):
        gv = g_ref[...]
        mn = ADAM_B1 * m_ref[...] + (1.0 - ADAM_B1) * gv
        vn = ADAM_B2 * v_ref[...] + (1.0 - ADAM_B2) * (gv * gv)
        m_hat = mn / (1.0 - ADAM_B1 ** ADAM_STEP)
        v_hat = vn / (1.0 - ADAM_B2 ** ADAM_STEP)
        d_ref[...] = -ADAM_LR * (m_hat / (jnp.sqrt(v_hat) + ADAM_EPS) + ADAM_WD * w_ref[...])
        mo_ref[...] = mn
        vo_ref[...] = vn

    spec = pl.BlockSpec((tr, c), lambda i: (i, 0))
    shp = jax.ShapeDtypeStruct((r, c), F32)
    return pl.pallas_call(
        body, name=name, grid=(r // tr,), in_specs=[spec] * 4, out_specs=[spec] * 3,
        out_shape=[shp] * 3, compiler_params=_cparams("parallel"),
    )(w, g, m, v)


def _position():
    return lax.axis_index("x"), lax.axis_index("y"), lax.axis_index("c")


def _all_gather(x, *, name):
    t, c_ = x.shape

    def body(x_ref, out_ref, send_sems, recv_sems, local_sem):
        start, forward, finish = _gather_phases(x_ref, out_ref, send_sems, recv_sems, local_sem)
        start()
        forward()
        finish()

    return pl.pallas_call(
        body, name=name,
        out_shape=jax.ShapeDtypeStruct((N_DEV, t, c_), x.dtype),
        in_specs=[pl.BlockSpec(memory_space=pl.ANY)],
        out_specs=pl.BlockSpec(memory_space=pl.ANY),
        scratch_shapes=_GATHER_SEMS,
    )(x)


_GATHER_SEMS = [pltpu.SemaphoreType.DMA((7,)), pltpu.SemaphoreType.DMA((7,)), pltpu.SemaphoreType.DMA]


def _gather_phases(x_ref, out_ref, send_sems, recv_sems, local_sem):
    x_, y_, c = _position()
    me, sibling = (x_, y_, c), (x_, y_, 1 - c)
    chips = [(1 - x_, y_), (x_, 1 - y_), (1 - x_, 1 - y_)]

    def block(px, py, pc):
        return out_ref.at[4 * px + 2 * py + pc]

    def copy(k, blk, to, src=None):
        return pltpu.make_async_remote_copy(
            src_ref=block(*blk) if src is None else src, dst_ref=block(*blk),
            send_sem=send_sems.at[k], recv_sem=recv_sems.at[k], device_id=to, device_id_type=MESH)

    mine = pltpu.make_async_copy(x_ref, block(*me), local_sem)
    first = [copy(0, me, sibling, src=x_ref)]
    first += [copy(1 + j, me, (*chip, c), src=x_ref) for j, chip in enumerate(chips)]
    passed = [copy(4 + j, (*chip, c), sibling) for j, chip in enumerate(chips)]

    def start():
        mine.start()
        for cp in first:
            cp.start()

    def forward():
        for j, chip in enumerate(chips):
            copy(1 + j, (*chip, c), me).wait_recv()
            passed[j].start()

    def finish():
        copy(0, sibling, me).wait_recv()
        for j, chip in enumerate(chips):
            copy(4 + j, (*chip, 1 - c), me).wait_recv()
        for cp in first + passed:
            cp.wait_send()
        mine.wait()

    return start, forward, finish


_SCATTER_SEMS = [pltpu.SemaphoreType.DMA((7,)), pltpu.SemaphoreType.DMA((7,)), pltpu.SemaphoreType.DMA]


def _scatter_phases(g_ref, r_ref, send_sems, recv_sems, local_sem):
    x_, y_, c = _position()
    me = 4 * x_ + 2 * y_ + c
    local = pltpu.make_async_copy(g_ref.at[me], r_ref.at[me], local_sem)
    copies = []
    for k in range(1, N_DEV):
        to = (x_ ^ (k >> 2), y_ ^ ((k >> 1) & 1), c ^ (k & 1))
        copies.append(pltpu.make_async_remote_copy(
            src_ref=g_ref.at[me ^ k], dst_ref=r_ref.at[me], send_sem=send_sems.at[k - 1],
            recv_sem=recv_sems.at[k - 1], device_id=to, device_id_type=MESH))

    def start():
        local.start()
        for cp in copies:
            cp.start()

    def finish():
        for k in range(1, N_DEV):
            pltpu.make_async_remote_copy(
                src_ref=g_ref.at[me], dst_ref=r_ref.at[me ^ k], send_sem=send_sems.at[k - 1],
                recv_sem=recv_sems.at[k - 1], device_id=(x_, y_, c), device_id_type=MESH).wait_recv()
        for cp in copies:
            cp.wait_send()
        local.wait()

    return start, finish


def _sum_devices(r8, *, name):
    _, t, c_ = r8.shape
    tr = _pick(t, 256, 16)

    def body(r_ref, o_ref):
        acc = r_ref[0].astype(F32)
        for d in range(1, N_DEV):
            acc = acc + r_ref[d].astype(F32)
        o_ref[...] = acc

    return pl.pallas_call(
        body, name=name, grid=(t // tr,),
        in_specs=[pl.BlockSpec((N_DEV, tr, c_), lambda i: (0, i, 0))],
        out_specs=pl.BlockSpec((tr, c_), lambda i: (i, 0)),
        out_shape=jax.ShapeDtypeStruct((t, c_), F32),
        compiler_params=_cparams("parallel"),
    )(r8)


def _all_reduce_small(x, *, name):
    r, c_ = x.shape

    def body(x_ref, o_ref, buf, send_sems, recv_sems):
        x_, y_, c = _position()
        me = 4 * x_ + 2 * y_ + c
        buf[me] = x_ref[...]
        copies = []
        for k in range(1, N_DEV):
            to = (x_ ^ (k >> 2), y_ ^ ((k >> 1) & 1), c ^ (k & 1))
            copies.append(pltpu.make_async_remote_copy(
                src_ref=x_ref, dst_ref=buf.at[me], send_sem=send_sems.at[k - 1], recv_sem=recv_sems.at[k - 1],
                device_id=to, device_id_type=MESH))
        for cp in copies:
            cp.start()
        for k in range(1, N_DEV):
            src = me ^ k
            pltpu.make_async_remote_copy(
                src_ref=x_ref, dst_ref=buf.at[src], send_sem=send_sems.at[k - 1], recv_sem=recv_sems.at[k - 1],
                device_id=(x_, y_, c), device_id_type=MESH).wait_recv()
        for cp in copies:
            cp.wait_send()
        acc = buf[0]
        for d in range(1, N_DEV):
            acc = acc + buf[d]
        o_ref[...] = acc

    return pl.pallas_call(
        body, name=name,
        out_shape=jax.ShapeDtypeStruct((r, c_), F32),
        in_specs=[pl.BlockSpec(memory_space=pltpu.VMEM)],
        out_specs=pl.BlockSpec(memory_space=pltpu.VMEM),
        scratch_shapes=[pltpu.VMEM((N_DEV, r, c_), F32), pltpu.SemaphoreType.DMA((7,)), pltpu.SemaphoreType.DMA((7,))],
    )(x)


def _rs_pair_exchange(g8, *, name):
    _, t, c_ = g8.shape

    def body(g_ref, r_ref, send_sems, recv_sems):
        x_, y_, c = _position()
        copies = [pltpu.make_async_remote_copy(
            src_ref=g_ref.at[2 * ch + (1 - c)], dst_ref=r_ref.at[ch],
            send_sem=send_sems.at[ch], recv_sem=recv_sems.at[ch],
            device_id=(x_, y_, 1 - c), device_id_type=MESH) for ch in range(4)]
        for cp in copies:
            cp.start()
        for cp in copies:
            cp.wait()

    return pl.pallas_call(
        body, name=name,
        out_shape=jax.ShapeDtypeStruct((4, t, c_), g8.dtype),
        in_specs=[pl.BlockSpec(memory_space=pl.ANY)],
        out_specs=pl.BlockSpec(memory_space=pl.ANY),
        scratch_shapes=[pltpu.SemaphoreType.DMA((4,)), pltpu.SemaphoreType.DMA((4,))],
    )(g8)


def _pair_add(core, g8, recv, *, name):
    _, t, c_ = g8.shape
    tr = _pick(t, 512, 16)

    def body(core_ref, g_ref, r_ref, o_ref):
        o_ref[...] = (g_ref[...].astype(F32) + r_ref[...].astype(F32)).astype(o_ref.dtype)

    grid_spec = pltpu.PrefetchScalarGridSpec(
        num_scalar_prefetch=1, grid=(4, t // tr),
        in_specs=[pl.BlockSpec((None, tr, c_), lambda ch, i, core_ref: (2 * ch + core_ref[0], i, 0)),
                  pl.BlockSpec((None, tr, c_), lambda ch, i, core_ref: (ch, i, 0))],
        out_specs=pl.BlockSpec((None, tr, c_), lambda ch, i, core_ref: (ch, i, 0)))
    return pl.pallas_call(
        body, name=name, grid_spec=grid_spec,
        out_shape=jax.ShapeDtypeStruct((4, t, c_), g8.dtype),
        compiler_params=_cparams("parallel", "parallel"),
    )(core, g8, recv)


def _rs_chip_exchange(part, *, name):
    _, t, c_ = part.shape

    def body(p_ref, r_ref, send_sems, recv_sems, local_sem):
        x_, y_, c = _position()
        mine = 2 * x_ + y_
        local = pltpu.make_async_copy(p_ref.at[mine], r_ref.at[mine], local_sem)
        local.start()
        chips = [(1 - x_, y_), (x_, 1 - y_), (1 - x_, 1 - y_)]
        copies = [pltpu.make_async_remote_copy(
            src_ref=p_ref.at[2 * px + py], dst_ref=r_ref.at[mine],
            send_sem=send_sems.at[k], recv_sem=recv_sems.at[k],
            device_id=(px, py, c), device_id_type=MESH) for k, (px, py) in enumerate(chips)]
        for cp in copies:
            cp.start()
        for k, (px, py) in enumerate(chips):
            pltpu.make_async_remote_copy(
                src_ref=p_ref.at[mine], dst_ref=r_ref.at[2 * px + py],
                send_sem=send_sems.at[k], recv_sem=recv_sems.at[k],
                device_id=(x_, y_, c), device_id_type=MESH).wait_recv()
        for cp in copies:
            cp.wait_send()
        local.wait()

    return pl.pallas_call(
        body, name=name,
        out_shape=jax.ShapeDtypeStruct((4, t, c_), part.dtype),
        in_specs=[pl.BlockSpec(memory_space=pl.ANY)],
        out_specs=pl.BlockSpec(memory_space=pl.ANY),
        scratch_shapes=[pltpu.SemaphoreType.DMA((3,)), pltpu.SemaphoreType.DMA((3,)), pltpu.SemaphoreType.DMA],
    )(part)


def _sum_chips(r4, *, name):
    _, t, c_ = r4.shape
    tr = _pick(t, 512, 16)

    def body(r_ref, o_ref):
        acc = r_ref[0].astype(F32)
        for ch in range(1, 4):
            acc = acc + r_ref[ch].astype(F32)
        o_ref[...] = acc

    return pl.pallas_call(
        body, name=name, grid=(t // tr,),
        in_specs=[pl.BlockSpec((4, tr, c_), lambda i: (0, i, 0))],
        out_specs=pl.BlockSpec((tr, c_), lambda i: (i, 0)),
        out_shape=jax.ShapeDtypeStruct((t, c_), F32),
        compiler_params=_cparams("parallel"),
    )(r4)


BIG = (
    ("w_in", (IN_COLS // N_DEV, D_MODEL), 0),
    ("w_branch", (3, BRANCH_W, D_MODEL // N_DEV), 2),
    ("w_out", (D_MODEL // N_DEV, D_MODEL), 0),
    ("w_q_xa", (D_MODEL // N_DEV, D_MODEL), 0),
    ("w_k_xa", (D_MODEL // N_DEV, D_MODEL), 0),
    ("w_v_xa", (D_MODEL // N_DEV, D_MODEL), 0),
    ("w_o_xa", (D_MODEL // N_DEV, D_MODEL), 0),
    ("w_gate_ffn", (FFN // N_DEV, D_MODEL), 0),
    ("w_up_ffn", (FFN // N_DEV, D_MODEL), 0),
    ("w_down_ffn", (FFN // N_DEV, D_MODEL), 0),
)
TRANSPOSED = ("w_in", "w_gate_ffn", "w_up_ffn")
_BIG_LAYOUT = {n: (shp, ax) for n, shp, ax in BIG}
PACK_COLS = 1024


def _stored(name, shard):
    return shard.T if name in TRANSPOSED else shard


def _size(shape):
    n = 1
    for d in shape:
        n *= d
    return n


def _pack_shards(items, shards):
    return jnp.concatenate([shards[it].reshape(-1, PACK_COLS) for it in items], axis=0)


def _unpack_gathered(items, g):
    out = {}
    r0 = 0
    for it in items:
        shp, ax = _BIG_LAYOUT[it[0]]
        rows = _size(shp) // PACK_COLS
        blk = g[:, r0:r0 + rows].reshape((N_DEV,) + shp)
        r0 += rows
        blk = jnp.moveaxis(blk, 0, ax)
        full = list(shp)
        full[ax] = shp[ax] * N_DEV
        out[it] = blk.reshape(full)
    return out


def _pack_full(items, full):
    parts = []
    for it in items:
        shp, ax = _BIG_LAYOUT[it[0]]
        t = full[it].reshape(shp[:ax] + (N_DEV, shp[ax]) + shp[ax + 1:])
        t = jnp.moveaxis(t, ax, 0)
        parts.append(t.reshape(N_DEV, -1, PACK_COLS))
    return jnp.concatenate(parts, axis=1)


def _unpack_shard(items, flat):
    out = {}
    r0 = 0
    for it in items:
        shp, _ = _BIG_LAYOUT[it[0]]
        rows = _size(shp) // PACK_COLS
        out[it] = flat[r0:r0 + rows].reshape(shp)
        r0 += rows
    return out


SMALL = (
    ("norm_mix_g", (DEPTH, D_MODEL)),
    ("sgu_ln_g", (DEPTH, BRANCH_W)),
    ("sgu_ln_b", (DEPTH, BRANCH_W)),
    ("w_spatial", (DEPTH, SGU_GROUPS, SGU_LEN, SGU_LEN)),
    ("b_spatial", (DEPTH, SGU_GROUPS, SGU_LEN)),
    ("conv_w", (DEPTH, 3, BRANCH_W)),
    ("norm_xa_g", (DEPTH, D_MODEL)),
    ("mem_norm_g", (DEPTH, D_MODEL)),
    ("norm_ffn_g", (DEPTH, D_MODEL)),
    ("final_g", (D_MODEL,)),
)


def _pack_small(grads):
    flat = jnp.concatenate([grads[n].reshape(-1) for n, _ in SMALL])
    rows = -(-flat.shape[0] // PACK_COLS)
    rows = -(-rows // 8) * 8
    flat = jnp.pad(flat, (0, rows * PACK_COLS - flat.shape[0]))
    return flat.reshape(rows, PACK_COLS)


def _unpack_small(buf):
    flat = buf.reshape(-1)
    out = {}
    o = 0
    for n, shp in SMALL:
        out[n] = flat[o:o + _size(shp)].reshape(shp)
        o += _size(shp)
    return out


def _layer_fwd(l, x, mem, wt, sm, gather=None):
    t = f"l{l}_"
    sv = {"x0": x}
    h = _rms_fwd(x, sm["norm_mix_g"][l][None], name=t + "rms_mix")
    p = _mm(h, wt["w_in", l], tb=True, name=t + "in_proj", tm=2048)
    if gather is None:
        ya = _sb_fwd(p, name=t + "sb_fwd")
    else:
        ya, gathered = _sb_fwd(p, name=t + "sb_fwd", gather=gather[1])
        wt.update(_unpack_gathered(gather[0], gathered))
    w_sp = sm["w_spatial"][l]
    b_col = sm["b_spatial"][l][:, :, None]
    ln_g, ln_b = sm["sgu_ln_g"][l][None], sm["sgu_ln_b"][l][None]
    yb = _sgu_fwd(p, ln_g, ln_b, w_sp, b_col, name=t + "sgu_fwd")
    yc = _conv_fwd(p, sm["conv_w"][l], name=t + "conv_fwd")
    merged = _merge_fwd(ya, yb, yc, wt["w_branch", l], p, name=t + "merge_fwd")
    x1 = _mm(merged, wt["w_out", l], add=x, name=t + "out_proj")
    sv.update(h=h, p=p, ya=ya, yb=yb, yc=yc, merged=merged, x1=x1)

    h2 = _rms_fwd(x1, sm["norm_xa_g"][l][None], name=t + "rms_xa")
    mn = _rms_fwd(mem, sm["mem_norm_g"][l][None], name=t + "rms_mem")
    q = _mm(h2, wt["w_q_xa", l], out_dtype=BF16, name=t + "xa_q", tm=2048)
    k = _mm(mn, wt["w_k_xa", l], out_dtype=BF16, name=t + "xa_k")
    v = _mm(mn, wt["w_v_xa", l], out_dtype=BF16, name=t + "xa_v")
    o = _xa_fwd(q, k, v, name=t + "xa_fwd")
    x2 = _mm(o, wt["w_o_xa", l], add=x1, name=t + "xa_o")
    sv.update(h2=h2, mn=mn, q=q, k=k, v=v, o=o, x2=x2)

    h3 = _rms_fwd(x2, sm["norm_ffn_g"][l][None], name=t + "rms_ffn")
    a, b, hd = _ffn_in(h3, wt["w_gate_ffn", l], wt["w_up_ffn", l], name=t + "ffn_in")
    x3 = _mm(hd, wt["w_down_ffn", l], add=x2, name=t + "ffn_down", tk=FFN)
    sv.update(h3=h3, a=a, b=b, hd=hd)
    return x3, sv


def _layer_bwd(l, dx3, mem, wt, sm, sv, scatter=None):
    t = f"l{l}_b_"
    gb, gs = {}, {}
    gb["w_down_ffn"] = _mm(sv["hd"], dx3, ta=True, out_dtype=BF16, name=t + "ffn_down_dw", tm=1408)
    da, db = _ffn_in_bwd(dx3, wt["w_down_ffn", l], sv["a"], sv["b"], name=t + "ffn_in_bwd")
    gb["w_gate_ffn"] = _mm(da, sv["h3"], ta=True, out_dtype=BF16, name=t + "ffn_gate_dw", tm=1408)
    gb["w_up_ffn"] = _mm(db, sv["h3"], ta=True, out_dtype=BF16, name=t + "ffn_up_dw", tm=1408)
    dh3 = _mm(da, wt["w_gate_ffn", l], name=t + "ffn_gate_dx", tk=1408)
    dx2, dg = _mm(db, wt["w_up_ffn", l], add=dh3, rms=(sv["x2"], sm["norm_ffn_g"][l][None], dx3),
                  name=t + "ffn_up_dx", tm=512, tk=1408)
    gs["norm_ffn_g"] = dg[0]
    do = _mm(dx2, wt["w_o_xa", l], tb=True, out_dtype=BF16, name=t + "xa_o_dx")
    gb["w_o_xa"] = _mm(sv["o"], dx2, ta=True, out_dtype=BF16, name=t + "xa_o_dw")
    dq, dk, dv = _xa_bwd(sv["q"], sv["k"], sv["v"], do, name=t + "xa_bwd")
    dx1, dg = _mm(dq, wt["w_q_xa", l], tb=True, rms=(sv["x1"], sm["norm_xa_g"][l][None], dx2),
                  name=t + "xa_q_dx", tm=1024)
    gs["norm_xa_g"] = dg[0]
    gb["w_q_xa"] = _mm(sv["h2"], dq, ta=True, out_dtype=BF16, name=t + "xa_q_dw")
    gb["w_k_xa"] = _mm(sv["mn"], dk, ta=True, out_dtype=BF16, name=t + "xa_k_dw")
    gb["w_v_xa"] = _mm(sv["mn"], dv, ta=True, out_dtype=BF16, name=t + "xa_v_dw")
    dmn = _mm(dk, wt["w_k_xa", l], tb=True, name=t + "xa_k_dx")
    dmn = _mm(dv, wt["w_v_xa", l], tb=True, add=dmn, name=t + "xa_v_dx")
    _, dg = _rms_bwd(mem, sm["mem_norm_g"][l][None], dmn, jnp.zeros_like(mem), name=t + "rms_mem")
    gs["mem_norm_g"] = dg[0]
    dm = _mm(dx1, wt["w_out", l], tb=True, name=t + "out_proj_dx")
    gb["w_out"] = _mm(sv["merged"], dx1, ta=True, out_dtype=BF16, name=t + "out_proj_dw")
    p = sv["p"]
    dya, dyb, dyc, dgates, *dbrd = _merge_bwd(dm, sv["ya"], sv["yb"], sv["yc"], wt["w_branch", l], p,
                                              name=t + "merge_bwd")
    gb["w_branch"] = jnp.stack([
        _mm(sv[y], dbrd[n], ta=True, out_dtype=BF16, name=t + f"branch{n}_dw")
        for n, y in enumerate(("ya", "yb", "yc"))])
    dcb, dcc, dcx, dcw = _conv_bwd(p, dyc, sm["conv_w"][l], name=t + "conv_bwd")
    gs["conv_w"] = dcw
    w_sp = sm["w_spatial"][l]
    dz, dlg, dlb, dwsp, dbsp = _sgu_bwd(p, dyb, sm["sgu_ln_g"][l][None], sm["sgu_ln_b"][l][None], w_sp,
                                        jnp.swapaxes(w_sp, 1, 2), sm["b_spatial"][l][:, :, None],
                                        name=t + "sgu_bwd")
    gs.update(sgu_ln_g=dlg[0], sgu_ln_b=dlb[0], w_spatial=dwsp, b_spatial=dbsp[:, :, 0])
    received = None
    if scatter is None:
        dq_a, dk_a, dv_a = _sb_bwd(p, dya, name=t + "sb_bwd")
    else:
        items, earlier = scatter
        ready = {**earlier, **{(n, l): g for n, g in gb.items()}}
        dq_a, dk_a, dv_a, received = _sb_bwd(p, dya, name=t + "sb_bwd", scatter=_pack_full(items, ready))
    dp = jnp.concatenate([dq_a, dk_a, dv_a, dz, dcb, dcc, dcx, dgates], axis=1)
    gb["w_in"] = _mm(dp, sv["h"], ta=True, out_dtype=BF16, name=t + "in_proj_dw")
    dx, dg = _mm(dp, wt["w_in", l], rms=(sv["x0"], sm["norm_mix_g"][l][None], dx1),
                 name=t + "in_proj_dx", tm=1024, tk=1792)
    gs["norm_mix_g"] = dg[0]
    return dx, gb, gs, received


_WEIGHTS = ("norm_mix_g", "w_in", "sgu_ln_g", "sgu_ln_b", "w_spatial", "b_spatial", "conv_w", "w_branch", "w_out",
            "norm_xa_g", "mem_norm_g", "w_q_xa", "w_k_xa", "w_v_xa", "w_o_xa", "norm_ffn_g", "w_gate_ffn",
            "w_up_ffn", "w_down_ffn", "final_g")


def kernel(x, mem, norm_mix_g, w_in, sgu_ln_g, sgu_ln_b, w_spatial, b_spatial, conv_w, w_branch, w_out, norm_xa_g, mem_norm_g, w_q_xa, w_k_xa, w_v_xa, w_o_xa, norm_ffn_g, w_gate_ffn, w_up_ffn, w_down_ffn, final_g, loss_target, m_norm_mix_g, m_w_in, m_sgu_ln_g, m_sgu_ln_b, m_w_spatial, m_b_spatial, m_conv_w, m_w_branch, m_w_out, m_norm_xa_g, m_mem_norm_g, m_w_q_xa, m_w_k_xa, m_w_v_xa, m_w_o_xa, m_norm_ffn_g, m_w_gate_ffn, m_w_up_ffn, m_w_down_ffn, m_final_g, v_norm_mix_g, v_w_in, v_sgu_ln_g, v_sgu_ln_b, v_w_spatial, v_b_spatial, v_conv_w, v_w_branch, v_w_out, v_norm_xa_g, v_mem_norm_g, v_w_q_xa, v_w_k_xa, v_w_v_xa, v_w_o_xa, v_norm_ffn_g, v_w_gate_ffn, v_w_up_ffn, v_w_down_ffn, v_final_g):
    w = dict(norm_mix_g=norm_mix_g, w_in=w_in, sgu_ln_g=sgu_ln_g, sgu_ln_b=sgu_ln_b, w_spatial=w_spatial,
             b_spatial=b_spatial, conv_w=conv_w, w_branch=w_branch, w_out=w_out, norm_xa_g=norm_xa_g,
             mem_norm_g=mem_norm_g, w_q_xa=w_q_xa, w_k_xa=w_k_xa, w_v_xa=w_v_xa, w_o_xa=w_o_xa,
             norm_ffn_g=norm_ffn_g, w_gate_ffn=w_gate_ffn, w_up_ffn=w_up_ffn, w_down_ffn=w_down_ffn, final_g=final_g)
    m = dict(norm_mix_g=m_norm_mix_g, w_in=m_w_in, sgu_ln_g=m_sgu_ln_g, sgu_ln_b=m_sgu_ln_b, w_spatial=m_w_spatial,
             b_spatial=m_b_spatial, conv_w=m_conv_w, w_branch=m_w_branch, w_out=m_w_out, norm_xa_g=m_norm_xa_g,
             mem_norm_g=m_mem_norm_g, w_q_xa=m_w_q_xa, w_k_xa=m_w_k_xa, w_v_xa=m_w_v_xa, w_o_xa=m_w_o_xa,
             norm_ffn_g=m_norm_ffn_g, w_gate_ffn=m_w_gate_ffn, w_up_ffn=m_w_up_ffn, w_down_ffn=m_w_down_ffn,
             final_g=m_final_g)
    v = dict(norm_mix_g=v_norm_mix_g, w_in=v_w_in, sgu_ln_g=v_sgu_ln_g, sgu_ln_b=v_sgu_ln_b, w_spatial=v_w_spatial,
             b_spatial=v_b_spatial, conv_w=v_conv_w, w_branch=v_w_branch, w_out=v_w_out, norm_xa_g=v_norm_xa_g,
             mem_norm_g=v_mem_norm_g, w_q_xa=v_w_q_xa, w_k_xa=v_w_k_xa, w_v_xa=v_w_v_xa, w_o_xa=v_w_o_xa,
             norm_ffn_g=v_norm_ffn_g, w_gate_ffn=v_w_gate_ffn, w_up_ffn=v_w_up_ffn, w_down_ffn=v_w_down_ffn,
             final_g=v_final_g)

    names = [n for n, _, _ in BIG]
    shards = {(n, l): _stored(n, w[n][l].astype(BF16)) for n in names for l in range(DEPTH)}
    first_items = [("w_in", 0)]
    later_items = [(n, l) for l in range(DEPTH) for n in names if (n, l) != ("w_in", 0)]
    wt = _unpack_gathered(first_items, _all_gather(_pack_shards(first_items, shards), name="gather_w_in0"))
    cw_pad = jnp.zeros((8, 128), F32).at[:DEPTH * 3, :BRANCH_W // N_DEV].set(conv_w.reshape(DEPTH * 3, -1))
    cw_all = _all_gather(cw_pad, name="gather_conv_w")[:, :DEPTH * 3, :BRANCH_W // N_DEV]
    conv_full = jnp.moveaxis(cw_all.reshape(N_DEV, DEPTH, 3, BRANCH_W // N_DEV), 0, 2).reshape(DEPTH, 3, BRANCH_W)
    sm = {n: w[n] for n, _ in SMALL}
    sm["conv_w"] = conv_full

    xs, ms = x[0], mem[0]
    x1, saved0 = _layer_fwd(0, xs, ms, wt, sm, gather=(later_items, _pack_shards(later_items, shards)))
    x2, saved1 = _layer_fwd(1, x1, ms, wt, sm)
    dcur, loss, dfinal = _final_loss(x2, sm["final_g"][None], loss_target[0], name="final_loss")
    loss = lax.psum(loss[0, 0], AXES)
    items_a = [(n, 1) for n in names if n != "w_in"]
    items_b = [("w_in", 1)] + [(n, 0) for n in names if n != "w_in"]
    items_c = [("w_in", 0)]
    dcur, gb1, gs1, recv_a = _layer_bwd(1, dcur, ms, wt, sm, saved1, scatter=(items_a, {}))
    dx, gb0, gs0, recv_b = _layer_bwd(0, dcur, ms, wt, sm, saved0, scatter=(items_b, {("w_in", 1): gb1["w_in"]}))

    shard_grads = _unpack_shard(items_a, _sum_devices(recv_a, name="rs_sum_a"))
    shard_grads.update(_unpack_shard(items_b, _sum_devices(recv_b, name="rs_sum_b")))
    g8 = _pack_full(items_c, {("w_in", 0): gb0["w_in"]})
    core = lax.axis_index("c").astype(jnp.int32).reshape(1)
    from_sibling = _rs_pair_exchange(g8, name="rs_pair_exchange")
    part = _pair_add(core, g8, from_sibling, name="rs_pair_add")
    by_chip = _rs_chip_exchange(part, name="rs_chip_exchange")
    shard_grads.update(_unpack_shard(items_c, _sum_chips(by_chip, name="rs_sum_chips")))
    grads = {n: jnp.stack([_stored(n, shard_grads[n, l]) for l in range(DEPTH)]) for n in names}
    small = {n: jnp.stack([gs0[n], gs1[n]]) for n, _ in SMALL if n != "final_g"}
    small["final_g"] = dfinal[0]
    small_sum = _unpack_small(_all_reduce_small(_pack_small(small), name="all_reduce_small"))
    width = BRANCH_W // N_DEV
    dev = 4 * lax.axis_index("x") + 2 * lax.axis_index("y") + lax.axis_index("c")
    for n, _ in SMALL:
        grads[n] = small_sum[n]
    grads["conv_w"] = lax.dynamic_slice_in_dim(small_sum["conv_w"], dev * width, width, axis=2)

    delta, new_m, new_v = {}, {}, {}
    for n in _WEIGHTS:
        shp = w[n].shape
        two_d = (-1, shp[-1])
        d_, m_, v_ = _adamw(w[n].reshape(two_d), grads[n].reshape(two_d), m[n].reshape(two_d), v[n].reshape(two_d),
                            name="adamw_" + n)
        delta[n], new_m[n], new_v[n] = d_.reshape(shp), m_.reshape(shp), v_.reshape(shp)

    return (loss, dx[None], *[grads[n] for n in _WEIGHTS], *[delta[n] for n in _WEIGHTS],
            *[new_m[n] for n in _WEIGHTS], *[new_v[n] for n in _WEIGHTS])
```

```python
import functools

import jax
import jax.numpy as jnp
from jax import lax
from jax.experimental import pallas as pl
from jax.experimental.pallas import tpu as pltpu

F32 = jnp.float32
BF16 = jnp.bfloat16
MESH = pl.DeviceIdType.MESH

D_MODEL = 1024
BRANCH_W = 512
IN_COLS = 7168
FFN = 2816
N_DEV = 8
DEPTH = 2
SB_BLOCK = 128
SB_SPAN = 1024
SB_Q_FWD = 512
SB_Q_BWD = 512
SB_SCALE = 0.125
XA_HEAD = 256
XA_SCALE = 0.0625
SGU_LEN = 128
SGU_GROUPS = 4
RMS_EPS = 1e-6
LN_EPS = 1e-5
HALO = 8

ADAM_LR = 0.001
ADAM_B1 = 0.9
ADAM_B2 = 0.999
ADAM_EPS = 1e-08
ADAM_WD = 0.01
ADAM_STEP = 10

VMEM_LIMIT_BYTES = 52 * 1024 * 1024

AXES = ("x", "y", "c")


def _cparams(*sem):
    return pltpu.CompilerParams(dimension_semantics=sem, vmem_limit_bytes=VMEM_LIMIT_BYTES)


def _pick(n, target, align):
    t = (min(target, n) // align) * align
    while t >= align:
        if n % t == 0:
            return t
        t -= align
    return n


def _dot(a, b):
    return jnp.dot(a, b, preferred_element_type=F32)


def _dot_nt(a, b):
    return lax.dot_general(a, b, (((1,), (1,)), ((), ())), preferred_element_type=F32)


def _dot_tn(a, b):
    return lax.dot_general(a, b, (((0,), (0,)), ((), ())), preferred_element_type=F32)


def _sigmoid(x):
    return 1.0 / (1.0 + jnp.exp(-x))


def _mm(a, b, *, name, ta=False, tb=False, out_dtype=F32, add=None, rms=None, tm=1024, tn=1024, tk=2048):
    m, k = (a.shape[1], a.shape[0]) if ta else a.shape
    n = b.shape[0] if tb else b.shape[1]
    assert k == (b.shape[1] if tb else b.shape[0])
    tm = _pick(m, tm, 128)
    tn = n if rms is not None else _pick(n, tn, 128)
    tk = _pick(k, tk, 128)
    nk = k // tk
    ca = 0 if ta else 1
    cb = 1 if tb else 0
    n_add = 0 if add is None else 1
    n_rms = 0 if rms is None else 3

    def body(*refs):
        refs = list(refs)
        a_ref, b_ref = refs[:2]
        extra = refs[2:2 + n_add + n_rms]
        outs = refs[2 + n_add + n_rms:]
        o_ref = outs[0]
        kk = pl.program_id(2)
        first_row_tile = pl.program_id(0) == 0

        def product():
            return lax.dot_general(a_ref[...].astype(BF16), b_ref[...].astype(BF16),
                                   (((ca,), (cb,)), ((), ())), preferred_element_type=F32)

        def finish(r):
            if add is not None:
                r = r + extra[0][...]
            if rms is None:
                o_ref[...] = r.astype(out_dtype)
                return
            x_ref, g_ref, dres_ref = extra[n_add:]
            dg_ref = outs[1]

            @pl.when(first_row_tile)
            def _():
                dg_ref[...] = jnp.zeros_like(dg_ref)

            xv = x_ref[...]
            rs = lax.rsqrt(jnp.mean(xv * xv, axis=-1, keepdims=True) + RMS_EPS)
            xh = xv * rs
            dg_ref[...] += jnp.sum(r * xh, axis=0, keepdims=True)
            dxh = r * g_ref[...]
            o_ref[...] = dres_ref[...] + rs * (dxh - xh * jnp.mean(dxh * xh, axis=-1, keepdims=True))

        if nk == 1:
            finish(product())
        else:
            acc_ref = outs[-1]

            @pl.when(kk == 0)
            def _():
                acc_ref[...] = jnp.zeros_like(acc_ref)

            acc_ref[...] += product()

            @pl.when(kk == nk - 1)
            def _():
                finish(acc_ref[...])

    a_spec = pl.BlockSpec((tk, tm), lambda i, j, kk: (kk, i)) if ta else pl.BlockSpec((tm, tk), lambda i, j, kk: (i, kk))
    b_spec = pl.BlockSpec((tn, tk), lambda i, j, kk: (j, kk)) if tb else pl.BlockSpec((tk, tn), lambda i, j, kk: (kk, j))
    tile = pl.BlockSpec((tm, tn), lambda i, j, kk: (i, j))
    in_specs = [a_spec, b_spec]
    operands = [a, b]
    out_specs = [tile]
    out_shape = [jax.ShapeDtypeStruct((m, n), out_dtype)]
    if add is not None:
        in_specs.append(tile)
        operands.append(add)
    if rms is not None:
        vec = pl.BlockSpec((1, n), lambda i, j, kk: (0, 0))
        in_specs += [tile, vec, tile]
        operands += list(rms)
        out_specs.append(vec)
        out_shape = [jax.ShapeDtypeStruct((m, n), F32), jax.ShapeDtypeStruct((1, n), F32)]
    out = pl.pallas_call(
        body, name=name,
        grid=(m // tm, n // tn, nk),
        in_specs=in_specs, out_specs=out_specs, out_shape=out_shape,
        scratch_shapes=[pltpu.VMEM((tm, tn), F32)] if nk > 1 else [],
        compiler_params=_cparams("arbitrary" if rms is not None else "parallel", "parallel", "arbitrary"),
    )(*operands)
    return out[0] if rms is None else out


def _rms_fwd(x, g, *, name):
    r, d = x.shape
    tr = _pick(r, 512, 16)

    def body(x_ref, g_ref, o_ref):
        xv = x_ref[...]
        rs = lax.rsqrt(jnp.mean(xv * xv, axis=-1, keepdims=True) + RMS_EPS)
        o_ref[...] = (xv * rs * g_ref[...]).astype(BF16)

    return pl.pallas_call(
        body, name=name, grid=(r // tr,),
        in_specs=[pl.BlockSpec((tr, d), lambda i: (i, 0)), pl.BlockSpec((1, d), lambda i: (0, 0))],
        out_specs=pl.BlockSpec((tr, d), lambda i: (i, 0)),
        out_shape=jax.ShapeDtypeStruct((r, d), BF16),
        compiler_params=_cparams("parallel"),
    )(x, g)


def _rms_bwd(x, g, dh, dres, *, name):
    r, d = x.shape
    tr = _pick(r, 256, 8)

    def body(x_ref, g_ref, dh_ref, dres_ref, dx_ref, dg_ref):
        @pl.when(pl.program_id(0) == 0)
        def _():
            dg_ref[...] = jnp.zeros_like(dg_ref)

        xv = x_ref[...]
        dhv = dh_ref[...].astype(F32)
        rs = lax.rsqrt(jnp.mean(xv * xv, axis=-1, keepdims=True) + RMS_EPS)
        xh = xv * rs
        dg_ref[...] += jnp.sum(dhv * xh, axis=0, keepdims=True)
        dxh = dhv * g_ref[...]
        dx_ref[...] = dres_ref[...] + rs * (dxh - xh * jnp.mean(dxh * xh, axis=-1, keepdims=True))

    return pl.pallas_call(
        body, name=name, grid=(r // tr,),
        in_specs=[pl.BlockSpec((tr, d), lambda i: (i, 0)), pl.BlockSpec((1, d), lambda i: (0, 0)),
                  pl.BlockSpec((tr, d), lambda i: (i, 0)), pl.BlockSpec((tr, d), lambda i: (i, 0))],
        out_specs=[pl.BlockSpec((tr, d), lambda i: (i, 0)), pl.BlockSpec((1, d), lambda i: (0, 0))],
        out_shape=[jax.ShapeDtypeStruct((r, d), F32), jax.ShapeDtypeStruct((1, d), F32)],
        compiler_params=_cparams("arbitrary"),
    )(x, g, dh, dres)


def _final_loss(x, g, target, *, name):
    r, d = x.shape
    tr = _pick(r, 256, 8)

    def body(x_ref, g_ref, t_ref, dx_ref, loss_ref, dg_ref):
        @pl.when(pl.program_id(0) == 0)
        def _():
            dg_ref[...] = jnp.zeros_like(dg_ref)
            loss_ref[...] = jnp.zeros_like(loss_ref)

        xv = x_ref[...]
        gv = g_ref[...]
        rs = lax.rsqrt(jnp.mean(xv * xv, axis=-1, keepdims=True) + RMS_EPS)
        xh = xv * rs
        err = xh * gv - t_ref[...]
        row_loss = jnp.mean(err * err, axis=-1, keepdims=True)
        loss_ref[...] += 0.5 * jnp.sum(row_loss, axis=0, keepdims=True)
        dy = err * (1.0 / d)
        dg_ref[...] += jnp.sum(dy * xh, axis=0, keepdims=True)
        dxh = dy * gv
        dx_ref[...] = rs * (dxh - xh * jnp.mean(dxh * xh, axis=-1, keepdims=True))

    return pl.pallas_call(
        body, name=name, grid=(r // tr,),
        in_specs=[pl.BlockSpec((tr, d), lambda i: (i, 0)), pl.BlockSpec((1, d), lambda i: (0, 0)),
                  pl.BlockSpec((tr, d), lambda i: (i, 0))],
        out_specs=[pl.BlockSpec((tr, d), lambda i: (i, 0)), pl.BlockSpec((1, 128), lambda i: (0, 0)),
                   pl.BlockSpec((1, d), lambda i: (0, 0))],
        out_shape=[jax.ShapeDtypeStruct((r, d), F32), jax.ShapeDtypeStruct((1, 128), F32),
                   jax.ShapeDtypeStruct((1, d), F32)],
        compiler_params=_cparams("arbitrary"),
    )(x, g, target)


def _cumsum_operand(strict_after, totals=True):
    width = (2 if totals else 1) * SB_BLOCK
    r = lax.broadcasted_iota(jnp.int32, (SB_BLOCK, width), 0)
    c = lax.broadcasted_iota(jnp.int32, (SB_BLOCK, width), 1)
    tri = (r > c) if strict_after else (r < c)
    return jnp.where((c >= SB_BLOCK) | tri, 1.0, 0.0).astype(BF16)


def _sb_scores(qh, kw, run, valid, after_ones):
    nb = kw.shape[0] // SB_BLOCK
    z = _dot_nt(qh, kw)
    lsp = jnp.minimum(z, 0.0) - jnp.log(1.0 + jnp.exp(-jnp.abs(z)))
    l1m = lsp - z
    if valid is not None:
        l1m = jnp.where(valid, l1m, 0.0)
    l1b = l1m.astype(BF16)
    later = [None] * nb
    seen = [None] * nb
    for b in reversed(range(nb)):
        cols = slice(b * SB_BLOCK, (b + 1) * SB_BLOCK)
        ct = _dot(l1b[:, cols], after_ones)
        seen[b] = run
        later[b] = run + ct[:, :SB_BLOCK]
        run = run + ct[:, SB_BLOCK:]
    a = jnp.exp(lsp + jnp.concatenate(later, axis=1))
    if valid is not None:
        a = jnp.where(valid, a, 0.0)
    return a, run, seen


def _sb_setup(q_ref, span):
    qi = pl.program_id(1)
    rows = q_ref.shape[0]
    sd = (qi * rows + rows - 1) // span
    lane = lax.broadcasted_iota(jnp.int32, (rows, SB_BLOCK), 1)
    col = lax.broadcasted_iota(jnp.int32, (rows, span), 1)
    row = lax.broadcasted_iota(jnp.int32, (rows, span), 0)
    valid = col < (qi * rows - sd * span) + row
    q = q_ref[...] * SB_SCALE
    qhs = (jnp.where(lane < 64, q, 0.0).astype(BF16), jnp.where(lane >= 64, q, 0.0).astype(BF16))
    return lane, sd, valid, qhs


def _sb_fwd(p, *, name, gather=None):
    s = p.shape[0]
    qrows = min(SB_Q_FWD, s)
    nq = s // qrows
    kcol = BRANCH_W // SB_BLOCK
    span = min(SB_SPAN, s)
    per = span // SB_BLOCK
    assert s // SB_BLOCK <= SB_BLOCK

    def body(*refs):
        if gather is None:
            q_ref, k_ref, v_ref, o_ref, r0_ref, r1_ref = refs
        else:
            q_ref, k_ref, v_ref, x_ref, o_ref, r0_ref, r1_ref, g_ref, send_sems, recv_sems, local_sem = refs
            start, forward, finish = _gather_phases(x_ref, g_ref, send_sems, recv_sems, local_sem)
            step = pl.program_id(0) * nq + pl.program_id(1)
            pl.when(step == 0)(start)
        lane, sd, valid, qhs = _sb_setup(q_ref, span)
        after_ones = _cumsum_operand(True)
        zero = jnp.zeros((qrows, SB_BLOCK), F32)
        lane_row = lax.broadcasted_iota(jnp.int32, (1, SB_BLOCK), 1)

        def span_step(sb, carry, mask):
            rows = pl.ds(pl.multiple_of(sb * span, span), span)
            kw = k_ref[rows, :].astype(BF16)
            vw = v_ref[rows, :].astype(BF16)
            out = []
            for h in range(2):
                run, acc, table = carry[h]
                a, run, seen = _sb_scores(qhs[h], kw, run, mask, after_ones)
                for b in range(per):
                    table = jnp.where(lane_row == sb * per + b, seen[b], table)
                out.append((run, acc + _dot(a.astype(BF16), vw), table))
            return tuple(out)

        carry = span_step(sd, ((zero, zero, zero), (zero, zero, zero)), valid)
        carry = lax.fori_loop(0, sd, lambda t, c: span_step(sd - 1 - t, c, None), carry)
        o_ref[...] = jnp.where(lane < 64, carry[0][1], carry[1][1]).astype(BF16)
        r0_ref[...] = carry[0][2]
        r1_ref[...] = carry[1][2]
        if gather is not None:
            pl.when(step == (kcol - 1) * nq + (3 * nq) // 4)(forward)
            pl.when(step == kcol * nq - 1)(finish)

    in_specs = [pl.BlockSpec((qrows, SB_BLOCK), lambda hp, qi: (qi, hp)),
                pl.BlockSpec((s, SB_BLOCK), lambda hp, qi: (0, kcol + hp)),
                pl.BlockSpec((s, SB_BLOCK), lambda hp, qi: (0, 2 * kcol + hp))]
    table = pl.BlockSpec((None, qrows, SB_BLOCK), lambda hp, qi: (hp, qi, 0))
    out_specs = [pl.BlockSpec((qrows, SB_BLOCK), lambda hp, qi: (qi, hp)), table, table]
    out_shape = [jax.ShapeDtypeStruct((s, BRANCH_W), BF16)] + [jax.ShapeDtypeStruct((kcol, s, SB_BLOCK), F32)] * 2
    operands = [p, p, p]
    scratch = []
    if gather is not None:
        in_specs.append(pl.BlockSpec(memory_space=pl.ANY))
        out_specs.append(pl.BlockSpec(memory_space=pl.ANY))
        out_shape.append(jax.ShapeDtypeStruct((N_DEV,) + gather.shape, gather.dtype))
        operands.append(gather)
        scratch = _GATHER_SEMS
    out = pl.pallas_call(
        body, name=name, grid=(kcol, nq), in_specs=in_specs, out_specs=out_specs, out_shape=out_shape,
        scratch_shapes=scratch, compiler_params=_cparams("arbitrary", "arbitrary"),
    )(*operands)
    return out


def _sb_bwd(p, dya, tables, *, name, scatter=None):
    s = p.shape[0]
    qrows = min(SB_Q_BWD, s)
    nq = s // qrows
    kcol = BRANCH_W // SB_BLOCK
    span = min(SB_SPAN, s)
    per = span // SB_BLOCK

    def body(*refs):
        if scatter is None:
            q_ref, k_ref, v_ref, do_ref, t0_ref, t1_ref, dq_ref, dk_ref, dv_ref, dk_acc, dv_acc = refs
        else:
            (q_ref, k_ref, v_ref, do_ref, t0_ref, t1_ref, g_ref, dq_ref, dk_ref, dv_ref, r_ref,
             dk_acc, dv_acc, send_sems, recv_sems, local_sem) = refs
            start, finish = _scatter_phases(g_ref, r_ref, send_sems, recv_sems, local_sem)
            step = pl.program_id(0) * nq + pl.program_id(1)
            pl.when(step == 0)(start)
        qi = pl.program_id(1)

        @pl.when(qi == 0)
        def _():
            dk_acc[...] = jnp.zeros_like(dk_acc)
            dv_acc[...] = jnp.zeros_like(dv_acc)

        lane, sd, valid, qhs = _sb_setup(q_ref, span)
        after = _cumsum_operand(True, totals=False)
        before_ones = _cumsum_operand(False)
        do = do_ref[...]
        dohs = (jnp.where(lane < 64, do, 0.0).astype(BF16), jnp.where(lane >= 64, do, 0.0).astype(BF16))
        tabs = (t0_ref[...], t1_ref[...])
        lane_row = lax.broadcasted_iota(jnp.int32, (1, SB_BLOCK), 1)
        zero = jnp.zeros((qrows, SB_BLOCK), F32)

        def span_step(sb, carry, mask):
            rows = pl.ds(pl.multiple_of(sb * span, span), span)
            kw = k_ref[rows, :].astype(BF16)
            vw = v_ref[rows, :].astype(BF16)
            out = []
            dk_span = jnp.zeros((span, SB_BLOCK), F32)
            dv_span = jnp.zeros((span, SB_BLOCK), F32)
            for h in range(2):
                pg, dq = carry[h]
                z = _dot_nt(qhs[h], kw)
                lsp = jnp.minimum(z, 0.0) - jnp.log(1.0 + jnp.exp(-jnp.abs(z)))
                l1m = lsp - z
                if mask is not None:
                    l1m = jnp.where(mask, l1m, 0.0)
                l1b = l1m.astype(BF16)
                later = [None] * per
                for b in range(per):
                    cols = slice(b * SB_BLOCK, (b + 1) * SB_BLOCK)
                    seen = jnp.sum(jnp.where(lane_row == sb * per + b, tabs[h], 0.0), axis=-1, keepdims=True)
                    later[b] = seen + _dot(l1b[:, cols], after)
                a = jnp.exp(lsp + jnp.concatenate(later, axis=1))
                beta = jnp.exp(lsp)
                if mask is not None:
                    a = jnp.where(mask, a, 0.0)
                    beta = jnp.where(mask, beta, 0.0)
                g = a * _dot_nt(dohs[h], vw)
                gb = g.astype(BF16)
                before = [None] * per
                for b in range(per):
                    cols = slice(b * SB_BLOCK, (b + 1) * SB_BLOCK)
                    gt = _dot(gb[:, cols], before_ones)
                    before[b] = pg + gt[:, :SB_BLOCK]
                    pg = pg + gt[:, SB_BLOCK:]
                dz = (g * (1.0 - beta) - beta * jnp.concatenate(before, axis=1)).astype(BF16)
                dk_span = dk_span + _dot_tn(dz, qhs[h])
                dv_span = dv_span + _dot_tn(a.astype(BF16), dohs[h])
                out.append((pg, dq + _dot(dz, kw)))
            dk_acc[rows, :] += dk_span
            dv_acc[rows, :] += dv_span
            return tuple(out)

        carry = lax.fori_loop(0, sd, lambda sb, c: span_step(sb, c, None), ((zero, zero), (zero, zero)))
        carry = span_step(sd, carry, valid)
        dq_ref[...] = (jnp.where(lane < 64, carry[0][1], carry[1][1]) * SB_SCALE).astype(BF16)

        @pl.when(qi == nq - 1)
        def _():
            dk_ref[...] = dk_acc[...].astype(BF16)
            dv_ref[...] = dv_acc[...].astype(BF16)

        if scatter is not None:
            pl.when(step == kcol * nq - 1)(finish)

    blk = pl.BlockSpec((qrows, SB_BLOCK), lambda hp, qi: (qi, hp))
    col = pl.BlockSpec((s, SB_BLOCK), lambda hp, qi: (0, hp))
    table = pl.BlockSpec((None, qrows, SB_BLOCK), lambda hp, qi: (hp, qi, 0))
    out = jax.ShapeDtypeStruct((s, BRANCH_W), BF16)
    in_specs = [blk,
                pl.BlockSpec((s, SB_BLOCK), lambda hp, qi: (0, kcol + hp)),
                pl.BlockSpec((s, SB_BLOCK), lambda hp, qi: (0, 2 * kcol + hp)),
                blk, table, table]
    out_specs = [blk, col, col]
    out_shape = [out, out, out]
    operands = [p, p, p, dya, tables[0], tables[1]]
    scratch = [pltpu.VMEM((s, SB_BLOCK), F32), pltpu.VMEM((s, SB_BLOCK), F32)]
    if scatter is not None:
        in_specs.append(pl.BlockSpec(memory_space=pl.ANY))
        out_specs.append(pl.BlockSpec(memory_space=pl.ANY))
        out_shape.append(jax.ShapeDtypeStruct(scatter.shape, scatter.dtype))
        operands.append(scatter)
        scratch = scratch + _SCATTER_SEMS
    return pl.pallas_call(
        body, name=name, grid=(kcol, nq), in_specs=in_specs, out_specs=out_specs, out_shape=out_shape,
        scratch_shapes=scratch, compiler_params=_cparams("arbitrary", "arbitrary"),
    )(*operands)


_INV_SQRT2 = 0.7071067811865476
_INV_SQRT2PI = 0.3989422804014327


def _gelu(x):
    return 0.5 * x * (1.0 + lax.erf(x * _INV_SQRT2))


def _gelu_grad(x):
    return 0.5 * (1.0 + lax.erf(x * _INV_SQRT2)) + x * _INV_SQRT2PI * jnp.exp(-0.5 * x * x)


def _chunk_mask(transposed=False):
    r = lax.broadcasted_iota(jnp.int32, (SGU_LEN, SGU_LEN), 0)
    c = lax.broadcasted_iota(jnp.int32, (SGU_LEN, SGU_LEN), 1)
    return (c // 64) >= (r // 64) if transposed else (r // 64) >= (c // 64)


def _sgu_norm(v_raw, g, b):
    zv = _gelu(v_raw)
    xc = zv - jnp.mean(zv, axis=-1, keepdims=True)
    rs = lax.rsqrt(jnp.mean(xc * xc, axis=-1, keepdims=True) + LN_EPS)
    xh = xc * rs
    return xh, rs, xh * g + b


def _sgu_fwd(p, ln_g, ln_b, w, b_col, *, name):
    s = p.shape[0]
    tr = _pick(s, 512, SGU_LEN)

    def body(u_ref, v_ref, g_ref, b_ref, w_ref, bc_ref, o_ref):
        mask = _chunk_mask()
        zu = _gelu(u_ref[...])
        _, _, vn = _sgu_norm(v_ref[...], g_ref[...], b_ref[...])
        vnb = vn.astype(BF16)
        for gi in range(SGU_GROUPS):
            wg = jnp.where(mask, w_ref[gi], 0.0).astype(BF16)
            cs = slice(gi * SGU_LEN, (gi + 1) * SGU_LEN)
            for c in range(tr // SGU_LEN):
                rs_ = slice(c * SGU_LEN, (c + 1) * SGU_LEN)
                vm = _dot(wg, vnb[rs_, cs]) + bc_ref[gi]
                o_ref[rs_, cs] = (zu[rs_, cs] * vm).astype(BF16)

    vec = pl.BlockSpec((1, BRANCH_W), lambda i: (0, 0))
    return pl.pallas_call(
        body, name=name, grid=(s // tr,),
        in_specs=[pl.BlockSpec((tr, BRANCH_W), lambda i: (i, 3)), pl.BlockSpec((tr, BRANCH_W), lambda i: (i, 4)),
                  vec, vec,
                  pl.BlockSpec((SGU_GROUPS, SGU_LEN, SGU_LEN), lambda i: (0, 0, 0)),
                  pl.BlockSpec((SGU_GROUPS, SGU_LEN, 1), lambda i: (0, 0, 0))],
        out_specs=pl.BlockSpec((tr, BRANCH_W), lambda i: (i, 0)),
        out_shape=jax.ShapeDtypeStruct((s, BRANCH_W), BF16),
        compiler_params=_cparams("parallel"),
    )(p, p, ln_g, ln_b, w, b_col)


def _sgu_bwd(p, dyb, ln_g, ln_b, w, w_t, b_col, *, name):
    s = p.shape[0]
    tr = _pick(s, 256, SGU_LEN)

    def body(u_ref, v_ref, dy_ref, g_ref, b_ref, w_ref, wt_ref, bc_ref,
             dz_ref, dg_ref, db_ref, dw_ref, dbc_ref, dvn_s):
        @pl.when(pl.program_id(0) == 0)
        def _():
            dg_ref[...] = jnp.zeros_like(dg_ref)
            db_ref[...] = jnp.zeros_like(db_ref)
            dw_ref[...] = jnp.zeros_like(dw_ref)
            dbc_ref[...] = jnp.zeros_like(dbc_ref)

        mask = _chunk_mask()
        mask_t = _chunk_mask(transposed=True)
        u_raw = u_ref[...]
        v_raw = v_ref[...]
        dy = dy_ref[...]
        zu = _gelu(u_raw)
        xh, rs, vn = _sgu_norm(v_raw, g_ref[...], b_ref[...])
        vnb = vn.astype(BF16)
        dvm_all = dy * zu
        for gi in range(SGU_GROUPS):
            wg = jnp.where(mask, w_ref[gi], 0.0).astype(BF16)
            wgt = jnp.where(mask_t, wt_ref[gi], 0.0).astype(BF16)
            cs = slice(gi * SGU_LEN, (gi + 1) * SGU_LEN)
            dw_g = jnp.zeros((SGU_LEN, SGU_LEN), F32)
            db_g = jnp.zeros((SGU_LEN, 1), F32)
            for c in range(tr // SGU_LEN):
                rs_ = slice(c * SGU_LEN, (c + 1) * SGU_LEN)
                vm = _dot(wg, vnb[rs_, cs]) + bc_ref[gi]
                dz_ref[rs_, cs] = (dy[rs_, cs] * vm * _gelu_grad(u_raw[rs_, cs])).astype(BF16)
                dvm = dvm_all[rs_, cs]
                dvmb = dvm.astype(BF16)
                dw_g = dw_g + _dot_nt(dvmb, vnb[rs_, cs])
                db_g = db_g + jnp.sum(dvm, axis=1, keepdims=True)
                dvn_s[rs_, cs] = _dot(wgt, dvmb)
            dw_ref[gi] += jnp.where(mask, dw_g, 0.0)
            dbc_ref[gi] += db_g
        dvn = dvn_s[...]
        dg_ref[...] += jnp.sum(dvn * xh, axis=0, keepdims=True)
        db_ref[...] += jnp.sum(dvn, axis=0, keepdims=True)
        dxh = dvn * g_ref[...]
        dzv = rs * (dxh - jnp.mean(dxh, axis=-1, keepdims=True) - xh * jnp.mean(dxh * xh, axis=-1, keepdims=True))
        dz_ref[:, BRANCH_W:] = (dzv * _gelu_grad(v_raw)).astype(BF16)

    vec = pl.BlockSpec((1, BRANCH_W), lambda i: (0, 0))
    wspec = pl.BlockSpec((SGU_GROUPS, SGU_LEN, SGU_LEN), lambda i: (0, 0, 0))
    bspec = pl.BlockSpec((SGU_GROUPS, SGU_LEN, 1), lambda i: (0, 0, 0))
    return pl.pallas_call(
        body, name=name, grid=(s // tr,),
        in_specs=[pl.BlockSpec((tr, BRANCH_W), lambda i: (i, 3)), pl.BlockSpec((tr, BRANCH_W), lambda i: (i, 4)),
                  pl.BlockSpec((tr, BRANCH_W), lambda i: (i, 0)), vec, vec, wspec, wspec, bspec],
        out_specs=[pl.BlockSpec((tr, 2 * BRANCH_W), lambda i: (i, 0)), vec, vec, wspec, bspec],
        out_shape=[jax.ShapeDtypeStruct((s, 2 * BRANCH_W), BF16),
                   jax.ShapeDtypeStruct((1, BRANCH_W), F32), jax.ShapeDtypeStruct((1, BRANCH_W), F32),
                   jax.ShapeDtypeStruct((SGU_GROUPS, SGU_LEN, SGU_LEN), F32),
                   jax.ShapeDtypeStruct((SGU_GROUPS, SGU_LEN, 1), F32)],
        scratch_shapes=[pltpu.VMEM((tr, BRANCH_W), F32)],
        compiler_params=_cparams("arbitrary"),
    )(p, p, dyb, ln_g, ln_b, w, w_t, b_col)


def _shift_down(x, prev8, k):
    rolled = pltpu.roll(x, k, 0)
    r8 = lax.broadcasted_iota(jnp.int32, prev8.shape, 0)
    head = jnp.where(r8 < k, pltpu.roll(prev8, k, 0), rolled[:HALO])
    return jnp.concatenate([head, rolled[HALO:]], axis=0)


def _shift_up(x, next8, k):
    n = x.shape[0]
    rolled = pltpu.roll(x, n - k, 0)
    r8 = lax.broadcasted_iota(jnp.int32, next8.shape, 0)
    tail = jnp.where(r8 >= HALO - k, pltpu.roll(next8, HALO - k, 0), rolled[n - HALO:])
    return jnp.concatenate([rolled[:n - HALO], tail], axis=0)


def _conv_specs(s, tr):
    nb = tr // HALO
    last = s // HALO - 1
    tile = lambda cb: pl.BlockSpec((tr, 128), lambda j, i: (i, cb * 4 + j))
    above = lambda cb: pl.BlockSpec((HALO, 128), lambda j, i: (jnp.maximum(i * nb - 1, 0), cb * 4 + j))
    below = lambda cb: pl.BlockSpec((HALO, 128), lambda j, i: (jnp.minimum((i + 1) * nb, last), cb * 4 + j))
    return tile, above, below


def _conv_fwd(p, cw, *, name):
    s = p.shape[0]
    tr = _pick(s, 512, HALO)
    tile, above, _ = _conv_specs(s, tr)

    def body(cb_ref, cc_ref, cx_ref, ccp_ref, cxp_ref, w_ref, o_ref):
        first = pl.program_id(1) == 0
        y = cc_ref[...] * cx_ref[...]
        yp = jnp.where(first, 0.0, ccp_ref[...] * cxp_ref[...])
        conv = w_ref[2:3, :] * y + w_ref[1:2, :] * _shift_down(y, yp, 1) + w_ref[0:1, :] * _shift_down(y, yp, 2)
        o_ref[...] = (cb_ref[...] * conv).astype(BF16)

    return pl.pallas_call(
        body, name=name, grid=(4, s // tr),
        in_specs=[tile(5), tile(6), tile(7), above(6), above(7), pl.BlockSpec((3, 128), lambda j, i: (0, j))],
        out_specs=pl.BlockSpec((tr, 128), lambda j, i: (i, j)),
        out_shape=jax.ShapeDtypeStruct((s, BRANCH_W), BF16),
        compiler_params=_cparams("parallel", "parallel"),
    )(p, p, p, p, p, cw)


def _conv_bwd(p, dyc, cw, *, name):
    s = p.shape[0]
    tr = _pick(s, 512, HALO)
    nt = s // tr
    nb = tr // HALO
    last = s // HALO - 1
    tile, above, below = _conv_specs(s, tr)

    def body(cb_ref, cc_ref, cx_ref, ccp_ref, cxp_ref, cbn_ref, dy_ref, dyn_ref, w_ref,
             dcb_ref, dcc_ref, dcx_ref, dw_ref):
        i = pl.program_id(1)

        @pl.when(i == 0)
        def _():
            dw_ref[...] = jnp.zeros_like(dw_ref)

        cb = cb_ref[...]
        cc = cc_ref[...]
        cx = cx_ref[...]
        y = cc * cx
        yp = jnp.where(i == 0, 0.0, ccp_ref[...] * cxp_ref[...])
        y1 = _shift_down(y, yp, 1)
        y2 = _shift_down(y, yp, 2)
        w0, w1, w2 = w_ref[0:1, :], w_ref[1:2, :], w_ref[2:3, :]
        conv = w2 * y + w1 * y1 + w0 * y2
        dyc_v = dy_ref[...]
        dconv = dyc_v * cb
        dn = jnp.where(i == nt - 1, 0.0, dyn_ref[...] * cbn_ref[...])
        dyv = w2 * dconv + w1 * _shift_up(dconv, dn, 1) + w0 * _shift_up(dconv, dn, 2)
        dcb_ref[...] = (dyc_v * conv).astype(BF16)
        dcc_ref[...] = (dyv * cx).astype(BF16)
        dcx_ref[...] = (dyv * cc).astype(BF16)
        dw_ref[0:1, :] += jnp.sum(dconv * y2, axis=0, keepdims=True)
        dw_ref[1:2, :] += jnp.sum(dconv * y1, axis=0, keepdims=True)
        dw_ref[2:3, :] += jnp.sum(dconv * y, axis=0, keepdims=True)

    dy_tile = pl.BlockSpec((tr, 128), lambda j, i: (i, j))
    dy_below = pl.BlockSpec((HALO, 128), lambda j, i: (jnp.minimum((i + 1) * nb, last), j))
    out_tile = lambda cb: pl.BlockSpec((tr, 128), lambda j, i: (i, cb * 4 + j))
    w_spec = pl.BlockSpec((3, 128), lambda j, i: (0, j))
    dcb, dcc, dcx, dw = pl.pallas_call(
        body, name=name, grid=(4, nt),
        in_specs=[tile(5), tile(6), tile(7), above(6), above(7), below(5), dy_tile, dy_below, w_spec],
        out_specs=[dy_tile, dy_tile, dy_tile, w_spec],
        out_shape=[jax.ShapeDtypeStruct((s, BRANCH_W), BF16)] * 3 + [jax.ShapeDtypeStruct((3, BRANCH_W), F32)],
        compiler_params=_cparams("parallel", "arbitrary"),
    )(p, p, p, p, p, p, dyc, dyc, cw)
    return dcb, dcc, dcx, dw


def _merge_fwd(ya, yb, yc, wb, p, *, name):
    s = p.shape[0]
    tr = _pick(s, 256, 16)

    def body(ya_ref, yb_ref, yc_ref, wb_ref, g0_ref, g1_ref, g2_ref, o_ref):
        acc = jnp.zeros((tr, D_MODEL), F32)
        for n, (y_ref, g_ref) in enumerate(((ya_ref, g0_ref), (yb_ref, g1_ref), (yc_ref, g2_ref))):
            acc = acc + _sigmoid(g_ref[...]) * _dot(y_ref[...].astype(BF16), wb_ref[n])
        o_ref[...] = acc.astype(BF16)

    yspec = pl.BlockSpec((tr, BRANCH_W), lambda i: (i, 0))
    gate = lambda n: pl.BlockSpec((tr, D_MODEL), lambda i: (i, 4 + n))
    return pl.pallas_call(
        body, name=name, grid=(s // tr,),
        in_specs=[yspec, yspec, yspec, pl.BlockSpec((3, BRANCH_W, D_MODEL), lambda i: (0, 0, 0)),
                  gate(0), gate(1), gate(2)],
        out_specs=pl.BlockSpec((tr, D_MODEL), lambda i: (i, 0)),
        out_shape=jax.ShapeDtypeStruct((s, D_MODEL), BF16),
        compiler_params=_cparams("parallel"),
    )(ya, yb, yc, wb, p, p, p)


def _merge_bwd(dm, ya, yb, yc, wb, p, *, name):
    s = p.shape[0]
    tr = _pick(s, 256, 16)

    def body(dm_ref, ya_ref, yb_ref, yc_ref, wb_ref, g0_ref, g1_ref, g2_ref,
             dya_ref, dyb_ref, dyc_ref, dg_ref, dbrd0_ref, dbrd1_ref, dbrd2_ref):
        dmv = dm_ref[...]
        ys = (ya_ref, yb_ref, yc_ref)
        gs = (g0_ref, g1_ref, g2_ref)
        dys = (dya_ref, dyb_ref, dyc_ref)
        dbrds = (dbrd0_ref, dbrd1_ref, dbrd2_ref)
        for n in range(3):
            brd = _dot(ys[n][...].astype(BF16), wb_ref[n])
            sg = _sigmoid(gs[n][...])
            dbrd = (sg * dmv).astype(BF16)
            dbrds[n][...] = dbrd
            dg_ref[:, n * D_MODEL:(n + 1) * D_MODEL] = (dmv * brd * sg * (1.0 - sg)).astype(BF16)
            dys[n][...] = _dot_nt(dbrd, wb_ref[n]).astype(dys[n].dtype)

    yspec = pl.BlockSpec((tr, BRANCH_W), lambda i: (i, 0))
    gate = lambda n: pl.BlockSpec((tr, D_MODEL), lambda i: (i, 4 + n))
    row = pl.BlockSpec((tr, D_MODEL), lambda i: (i, 0))
    return pl.pallas_call(
        body, name=name, grid=(s // tr,),
        in_specs=[row, yspec, yspec, yspec, pl.BlockSpec((3, BRANCH_W, D_MODEL), lambda i: (0, 0, 0)),
                  gate(0), gate(1), gate(2)],
        out_specs=[yspec, yspec, yspec, pl.BlockSpec((tr, 3 * D_MODEL), lambda i: (i, 0)), row, row, row],
        out_shape=[jax.ShapeDtypeStruct((s, BRANCH_W), BF16)] + [jax.ShapeDtypeStruct((s, BRANCH_W), F32)] * 2
                  + [jax.ShapeDtypeStruct((s, 3 * D_MODEL), BF16)] + [jax.ShapeDtypeStruct((s, D_MODEL), BF16)] * 3,
        compiler_params=_cparams("parallel"),
    )(dm, ya, yb, yc, wb, p, p, p)


def _xa_probs(q, k):
    sc = _dot_nt(q, k) * XA_SCALE
    e = jnp.exp(sc - jnp.max(sc, axis=-1, keepdims=True))
    return e / jnp.sum(e, axis=-1, keepdims=True)


def _xa_fwd(q, k, v, *, name):
    s = q.shape[0]
    mt = k.shape[0]
    tr = _pick(s, 512, 16)

    def body(q_ref, k_ref, v_ref, o_ref):
        pr = _xa_probs(q_ref[...], k_ref[...])
        o_ref[...] = _dot(pr.astype(BF16), v_ref[...]).astype(BF16)

    qs = pl.BlockSpec((tr, XA_HEAD), lambda h, i: (i, h))
    ks = pl.BlockSpec((mt, XA_HEAD), lambda h, i: (0, h))
    return pl.pallas_call(
        body, name=name, grid=(D_MODEL // XA_HEAD, s // tr),
        in_specs=[qs, ks, ks], out_specs=qs,
        out_shape=jax.ShapeDtypeStruct((s, D_MODEL), BF16),
        compiler_params=_cparams("parallel", "parallel"),
    )(q, k, v)


def _xa_bwd(q, k, v, do, *, name):
    s = q.shape[0]
    mt = k.shape[0]
    tr = _pick(s, 512, 16)

    def body(q_ref, k_ref, v_ref, do_ref, dq_ref, dk_ref, dv_ref):
        @pl.when(pl.program_id(1) == 0)
        def _():
            dk_ref[...] = jnp.zeros_like(dk_ref)
            dv_ref[...] = jnp.zeros_like(dv_ref)

        qv = q_ref[...]
        kv = k_ref[...]
        dov = do_ref[...]
        pr = _xa_probs(qv, kv)
        dpr = _dot_nt(dov, v_ref[...])
        ds = (pr * (dpr - jnp.sum(dpr * pr, axis=-1, keepdims=True)) * XA_SCALE).astype(BF16)
        dq_ref[...] = _dot(ds, kv).astype(BF16)
        dk_ref[...] += _dot_tn(ds, qv)
        dv_ref[...] += _dot_tn(pr.astype(BF16), dov)

    qs = pl.BlockSpec((tr, XA_HEAD), lambda h, i: (i, h))
    ks = pl.BlockSpec((mt, XA_HEAD), lambda h, i: (0, h))
    return pl.pallas_call(
        body, name=name, grid=(D_MODEL // XA_HEAD, s // tr),
        in_specs=[qs, ks, ks, qs], out_specs=[qs, ks, ks],
        out_shape=[jax.ShapeDtypeStruct((s, D_MODEL), BF16), jax.ShapeDtypeStruct((mt, D_MODEL), F32),
                   jax.ShapeDtypeStruct((mt, D_MODEL), F32)],
        compiler_params=_cparams("parallel", "arbitrary"),
    )(q, k, v, do)


def _ffn_in(h, wg, wu, *, name):
    s, d = h.shape
    f = wg.shape[0]
    tm = _pick(s, 1024, 128)
    tn = _pick(f, 1408, 128)

    def body(h_ref, wg_ref, wu_ref, a_ref, b_ref, o_ref):
        hv = h_ref[...]
        av = _dot_nt(hv, wg_ref[...])
        bv = _dot_nt(hv, wu_ref[...])
        a_ref[...] = av.astype(BF16)
        b_ref[...] = bv.astype(BF16)
        o_ref[...] = (av * _sigmoid(av) * bv).astype(BF16)

    wspec = pl.BlockSpec((tn, d), lambda i, j: (j, 0))
    tile = pl.BlockSpec((tm, tn), lambda i, j: (i, j))
    return pl.pallas_call(
        body, name=name, grid=(s // tm, f // tn),
        in_specs=[pl.BlockSpec((tm, d), lambda i, j: (i, 0)), wspec, wspec],
        out_specs=[tile, tile, tile], out_shape=[jax.ShapeDtypeStruct((s, f), BF16)] * 3,
        compiler_params=_cparams("parallel", "parallel"),
    )(h, wg, wu)


def _ffn_in_bwd(dx, wd, a, b, *, name):
    s, d = dx.shape
    f = wd.shape[0]
    tm = _pick(s, 1024, 128)
    tn = _pick(f, 1408, 128)

    def body(dx_ref, wd_ref, a_ref, b_ref, da_ref, db_ref):
        dhv = _dot_nt(dx_ref[...].astype(BF16), wd_ref[...])
        av = a_ref[...].astype(F32)
        sg = _sigmoid(av)
        silu = av * sg
        da_ref[...] = (dhv * b_ref[...].astype(F32) * (sg + silu * (1.0 - sg))).astype(BF16)
        db_ref[...] = (dhv * silu).astype(BF16)

    tile = pl.BlockSpec((tm, tn), lambda i, j: (i, j))
    return pl.pallas_call(
        body, name=name, grid=(s // tm, f // tn),
        in_specs=[pl.BlockSpec((tm, d), lambda i, j: (i, 0)), pl.BlockSpec((tn, d), lambda i, j: (j, 0)), tile, tile],
        out_specs=[tile, tile], out_shape=[jax.ShapeDtypeStruct((s, f), BF16)] * 2,
        compiler_params=_cparams("parallel", "parallel"),
    )(dx, wd, a, b)


def _adamw(w, g, m, v, *, name):
    r, c = w.shape
    tr = _pick(r, 512, 8)

    def body(w_ref, g_ref, m_ref, v_ref, d_ref, mo_ref, vo_ref):
        gv = g_ref[...]
        mn = ADAM_B1 * m_ref[...] + (1.0 - ADAM_B1) * gv
        vn = ADAM_B2 * v_ref[...] + (1.0 - ADAM_B2) * (gv * gv)
        m_hat = mn / (1.0 - ADAM_B1 ** ADAM_STEP)
        v_hat = vn / (1.0 - ADAM_B2 ** ADAM_STEP)
        d_ref[...] = -ADAM_LR * (m_hat / (jnp.sqrt(v_hat) + ADAM_EPS) + ADAM_WD * w_ref[...])
        mo_ref[...] = mn
        vo_ref[...] = vn

    spec = pl.BlockSpec((tr, c), lambda i: (i, 0))
    shp = jax.ShapeDtypeStruct((r, c), F32)
    return pl.pallas_call(
        body, name=name, grid=(r // tr,), in_specs=[spec] * 4, out_specs=[spec] * 3,
        out_shape=[shp] * 3, compiler_params=_cparams("parallel"),
    )(w, g, m, v)


def _position():
    return lax.axis_index("x"), lax.axis_index("y"), lax.axis_index("c")


def _all_gather(x, *, name):
    t, c_ = x.shape

    def body(x_ref, out_ref, send_sems, recv_sems, local_sem):
        start, forward, finish = _gather_phases(x_ref, out_ref, send_sems, recv_sems, local_sem)
        start()
        forward()
        finish()

    return pl.pallas_call(
        body, name=name,
        out_shape=jax.ShapeDtypeStruct((N_DEV, t, c_), x.dtype),
        in_specs=[pl.BlockSpec(memory_space=pl.ANY)],
        out_specs=pl.BlockSpec(memory_space=pl.ANY),
        scratch_shapes=_GATHER_SEMS,
    )(x)


_GATHER_SEMS = [pltpu.SemaphoreType.DMA((7,)), pltpu.SemaphoreType.DMA((7,)), pltpu.SemaphoreType.DMA]


def _gather_phases(x_ref, out_ref, send_sems, recv_sems, local_sem):
    x_, y_, c = _position()
    me, sibling = (x_, y_, c), (x_, y_, 1 - c)
    chips = [(1 - x_, y_), (x_, 1 - y_), (1 - x_, 1 - y_)]

    def block(px, py, pc):
        return out_ref.at[4 * px + 2 * py + pc]

    def copy(k, blk, to, src=None):
        return pltpu.make_async_remote_copy(
            src_ref=block(*blk) if src is None else src, dst_ref=block(*blk),
            send_sem=send_sems.at[k], recv_sem=recv_sems.at[k], device_id=to, device_id_type=MESH)

    mine = pltpu.make_async_copy(x_ref, block(*me), local_sem)
    first = [copy(0, me, sibling, src=x_ref)]
    first += [copy(1 + j, me, (*chip, c), src=x_ref) for j, chip in enumerate(chips)]
    passed = [copy(4 + j, (*chip, c), sibling) for j, chip in enumerate(chips)]

    def start():
        mine.start()
        for cp in first:
            cp.start()

    def forward():
        for j, chip in enumerate(chips):
            copy(1 + j, (*chip, c), me).wait_recv()
            passed[j].start()

    def finish():
        copy(0, sibling, me).wait_recv()
        for j, chip in enumerate(chips):
            copy(4 + j, (*chip, 1 - c), me).wait_recv()
        for cp in first + passed:
            cp.wait_send()
        mine.wait()

    return start, forward, finish


_SCATTER_SEMS = [pltpu.SemaphoreType.DMA((7,)), pltpu.SemaphoreType.DMA((7,)), pltpu.SemaphoreType.DMA]


def _scatter_phases(g_ref, r_ref, send_sems, recv_sems, local_sem):
    x_, y_, c = _position()
    me = 4 * x_ + 2 * y_ + c
    local = pltpu.make_async_copy(g_ref.at[me], r_ref.at[me], local_sem)
    copies = []
    for k in range(1, N_DEV):
        to = (x_ ^ (k >> 2), y_ ^ ((k >> 1) & 1), c ^ (k & 1))
        copies.append(pltpu.make_async_remote_copy(
            src_ref=g_ref.at[me ^ k], dst_ref=r_ref.at[me], send_sem=send_sems.at[k - 1],
            recv_sem=recv_sems.at[k - 1], device_id=to, device_id_type=MESH))

    def start():
        local.start()
        for cp in copies:
            cp.start()

    def finish():
        for k in range(1, N_DEV):
            pltpu.make_async_remote_copy(
                src_ref=g_ref.at[me], dst_ref=r_ref.at[me ^ k], send_sem=send_sems.at[k - 1],
                recv_sem=recv_sems.at[k - 1], device_id=(x_, y_, c), device_id_type=MESH).wait_recv()
        for cp in copies:
            cp.wait_send()
        local.wait()

    return start, finish


def _sum_devices(r8, *, name):
    _, t, c_ = r8.shape
    tr = _pick(t, 256, 16)

    def body(r_ref, o_ref):
        acc = r_ref[0].astype(F32)
        for d in range(1, N_DEV):
            acc = acc + r_ref[d].astype(F32)
        o_ref[...] = acc

    return pl.pallas_call(
        body, name=name, grid=(t // tr,),
        in_specs=[pl.BlockSpec((N_DEV, tr, c_), lambda i: (0, i, 0))],
        out_specs=pl.BlockSpec((tr, c_), lambda i: (i, 0)),
        out_shape=jax.ShapeDtypeStruct((t, c_), F32),
        compiler_params=_cparams("parallel"),
    )(r8)


def _all_reduce_small(x, *, name):
    r, c_ = x.shape

    def body(x_ref, o_ref, buf, send_sems, recv_sems):
        x_, y_, c = _position()
        me = 4 * x_ + 2 * y_ + c
        buf[me] = x_ref[...]
        copies = []
        for k in range(1, N_DEV):
            to = (x_ ^ (k >> 2), y_ ^ ((k >> 1) & 1), c ^ (k & 1))
            copies.append(pltpu.make_async_remote_copy(
                src_ref=x_ref, dst_ref=buf.at[me], send_sem=send_sems.at[k - 1], recv_sem=recv_sems.at[k - 1],
                device_id=to, device_id_type=MESH))
        for cp in copies:
            cp.start()
        for k in range(1, N_DEV):
            src = me ^ k
            pltpu.make_async_remote_copy(
                src_ref=x_ref, dst_ref=buf.at[src], send_sem=send_sems.at[k - 1], recv_sem=recv_sems.at[k - 1],
                device_id=(x_, y_, c), device_id_type=MESH).wait_recv()
        for cp in copies:
            cp.wait_send()
        acc = buf[0]
        for d in range(1, N_DEV):
            acc = acc + buf[d]
        o_ref[...] = acc

    return pl.pallas_call(
        body, name=name,
        out_shape=jax.ShapeDtypeStruct((r, c_), F32),
        in_specs=[pl.BlockSpec(memory_space=pltpu.VMEM)],
        out_specs=pl.BlockSpec(memory_space=pltpu.VMEM),
        scratch_shapes=[pltpu.VMEM((N_DEV, r, c_), F32), pltpu.SemaphoreType.DMA((7,)), pltpu.SemaphoreType.DMA((7,))],
    )(x)


def _rs_pair_exchange(g8, *, name):
    _, t, c_ = g8.shape

    def body(g_ref, r_ref, send_sems, recv_sems):
        x_, y_, c = _position()
        copies = [pltpu.make_async_remote_copy(
            src_ref=g_ref.at[2 * ch + (1 - c)], dst_ref=r_ref.at[ch],
            send_sem=send_sems.at[ch], recv_sem=recv_sems.at[ch],
            device_id=(x_, y_, 1 - c), device_id_type=MESH) for ch in range(4)]
        for cp in copies:
            cp.start()
        for cp in copies:
            cp.wait()

    return pl.pallas_call(
        body, name=name,
        out_shape=jax.ShapeDtypeStruct((4, t, c_), g8.dtype),
        in_specs=[pl.BlockSpec(memory_space=pl.ANY)],
        out_specs=pl.BlockSpec(memory_space=pl.ANY),
        scratch_shapes=[pltpu.SemaphoreType.DMA((4,)), pltpu.SemaphoreType.DMA((4,))],
    )(g8)


def _pair_add(core, g8, recv, *, name):
    _, t, c_ = g8.shape
    tr = _pick(t, 512, 16)

    def body(core_ref, g_ref, r_ref, o_ref):
        o_ref[...] = (g_ref[...].astype(F32) + r_ref[...].astype(F32)).astype(o_ref.dtype)

    grid_spec = pltpu.PrefetchScalarGridSpec(
        num_scalar_prefetch=1, grid=(4, t // tr),
        in_specs=[pl.BlockSpec((None, tr, c_), lambda ch, i, core_ref: (2 * ch + core_ref[0], i, 0)),
                  pl.BlockSpec((None, tr, c_), lambda ch, i, core_ref: (ch, i, 0))],
        out_specs=pl.BlockSpec((None, tr, c_), lambda ch, i, core_ref: (ch, i, 0)))
    return pl.pallas_call(
        body, name=name, grid_spec=grid_spec,
        out_shape=jax.ShapeDtypeStruct((4, t, c_), g8.dtype),
        compiler_params=_cparams("parallel", "parallel"),
    )(core, g8, recv)


def _rs_chip_exchange(part, *, name):
    _, t, c_ = part.shape

    def body(p_ref, r_ref, send_sems, recv_sems, local_sem):
        x_, y_, c = _position()
        mine = 2 * x_ + y_
        local = pltpu.make_async_copy(p_ref.at[mine], r_ref.at[mine], local_sem)
        local.start()
        chips = [(1 - x_, y_), (x_, 1 - y_), (1 - x_, 1 - y_)]
        copies = [pltpu.make_async_remote_copy(
            src_ref=p_ref.at[2 * px + py], dst_ref=r_ref.at[mine],
            send_sem=send_sems.at[k], recv_sem=recv_sems.at[k],
            device_id=(px, py, c), device_id_type=MESH) for k, (px, py) in enumerate(chips)]
        for cp in copies:
            cp.start()
        for k, (px, py) in enumerate(chips):
            pltpu.make_async_remote_copy(
                src_ref=p_ref.at[mine], dst_ref=r_ref.at[2 * px + py],
                send_sem=send_sems.at[k], recv_sem=recv_sems.at[k],
                device_id=(x_, y_, c), device_id_type=MESH).wait_recv()
        for cp in copies:
            cp.wait_send()
        local.wait()

    return pl.pallas_call(
        body, name=name,
        out_shape=jax.ShapeDtypeStruct((4, t, c_), part.dtype),
        in_specs=[pl.BlockSpec(memory_space=pl.ANY)],
        out_specs=pl.BlockSpec(memory_space=pl.ANY),
        scratch_shapes=[pltpu.SemaphoreType.DMA((3,)), pltpu.SemaphoreType.DMA((3,)), pltpu.SemaphoreType.DMA],
    )(part)


def _sum_chips(r4, *, name):
    _, t, c_ = r4.shape
    tr = _pick(t, 512, 16)

    def body(r_ref, o_ref):
        acc = r_ref[0].astype(F32)
        for ch in range(1, 4):
            acc = acc + r_ref[ch].astype(F32)
        o_ref[...] = acc

    return pl.pallas_call(
        body, name=name, grid=(t // tr,),
        in_specs=[pl.BlockSpec((4, tr, c_), lambda i: (0, i, 0))],
        out_specs=pl.BlockSpec((tr, c_), lambda i: (i, 0)),
        out_shape=jax.ShapeDtypeStruct((t, c_), F32),
        compiler_params=_cparams("parallel"),
    )(r4)


BIG = (
    ("w_in", (IN_COLS // N_DEV, D_MODEL), 0),
    ("w_branch", (3, BRANCH_W, D_MODEL // N_DEV), 2),
    ("w_out", (D_MODEL // N_DEV, D_MODEL), 0),
    ("w_q_xa", (D_MODEL // N_DEV, D_MODEL), 0),
    ("w_k_xa", (D_MODEL // N_DEV, D_MODEL), 0),
    ("w_v_xa", (D_MODEL // N_DEV, D_MODEL), 0),
    ("w_o_xa", (D_MODEL // N_DEV, D_MODEL), 0),
    ("w_gate_ffn", (FFN // N_DEV, D_MODEL), 0),
    ("w_up_ffn", (FFN // N_DEV, D_MODEL), 0),
    ("w_down_ffn", (FFN // N_DEV, D_MODEL), 0),
)
TRANSPOSED = ("w_in", "w_gate_ffn", "w_up_ffn")
_BIG_LAYOUT = {n: (shp, ax) for n, shp, ax in BIG}
PACK_COLS = 1024


def _stored(name, shard):
    return shard.T if name in TRANSPOSED else shard


def _size(shape):
    n = 1
    for d in shape:
        n *= d
    return n


def _pack_shards(items, shards):
    return jnp.concatenate([shards[it].reshape(-1, PACK_COLS) for it in items], axis=0)


def _unpack_gathered(items, g):
    out = {}
    r0 = 0
    for it in items:
        shp, ax = _BIG_LAYOUT[it[0]]
        rows = _size(shp) // PACK_COLS
        blk = g[:, r0:r0 + rows].reshape((N_DEV,) + shp)
        r0 += rows
        blk = jnp.moveaxis(blk, 0, ax)
        full = list(shp)
        full[ax] = shp[ax] * N_DEV
        out[it] = blk.reshape(full)
    return out


def _pack_full(items, full):
    parts = []
    for it in items:
        shp, ax = _BIG_LAYOUT[it[0]]
        t = full[it].reshape(shp[:ax] + (N_DEV, shp[ax]) + shp[ax + 1:])
        t = jnp.moveaxis(t, ax, 0)
        parts.append(t.reshape(N_DEV, -1, PACK_COLS))
    return jnp.concatenate(parts, axis=1)


def _unpack_shard(items, flat):
    out = {}
    r0 = 0
    for it in items:
        shp, _ = _BIG_LAYOUT[it[0]]
        rows = _size(shp) // PACK_COLS
        out[it] = flat[r0:r0 + rows].reshape(shp)
        r0 += rows
    return out


SMALL = (
    ("norm_mix_g", (DEPTH, D_MODEL)),
    ("sgu_ln_g", (DEPTH, BRANCH_W)),
    ("sgu_ln_b", (DEPTH, BRANCH_W)),
    ("w_spatial", (DEPTH, SGU_GROUPS, SGU_LEN, SGU_LEN)),
    ("b_spatial", (DEPTH, SGU_GROUPS, SGU_LEN)),
    ("conv_w", (DEPTH, 3, BRANCH_W)),
    ("norm_xa_g", (DEPTH, D_MODEL)),
    ("mem_norm_g", (DEPTH, D_MODEL)),
    ("norm_ffn_g", (DEPTH, D_MODEL)),
    ("final_g", (D_MODEL,)),
)


def _pack_small(grads):
    flat = jnp.concatenate([grads[n].reshape(-1) for n, _ in SMALL])
    rows = -(-flat.shape[0] // PACK_COLS)
    rows = -(-rows // 8) * 8
    flat = jnp.pad(flat, (0, rows * PACK_COLS - flat.shape[0]))
    return flat.reshape(rows, PACK_COLS)


def _unpack_small(buf):
    flat = buf.reshape(-1)
    out = {}
    o = 0
    for n, shp in SMALL:
        out[n] = flat[o:o + _size(shp)].reshape(shp)
        o += _size(shp)
    return out


def _layer_fwd(l, x, mem, wt, sm, gather=None):
    t = f"l{l}_"
    sv = {"x0": x}
    h = _rms_fwd(x, sm["norm_mix_g"][l][None], name=t + "rms_mix")
    p = _mm(h, wt["w_in", l], tb=True, name=t + "in_proj", tm=2048)
    if gather is None:
        ya, *tables = _sb_fwd(p, name=t + "sb_fwd")
    else:
        ya, *tables, gathered = _sb_fwd(p, name=t + "sb_fwd", gather=gather[1])
        wt.update(_unpack_gathered(gather[0], gathered))
    w_sp = sm["w_spatial"][l]
    b_col = sm["b_spatial"][l][:, :, None]
    ln_g, ln_b = sm["sgu_ln_g"][l][None], sm["sgu_ln_b"][l][None]
    yb = _sgu_fwd(p, ln_g, ln_b, w_sp, b_col, name=t + "sgu_fwd")
    yc = _conv_fwd(p, sm["conv_w"][l], name=t + "conv_fwd")
    merged = _merge_fwd(ya, yb, yc, wt["w_branch", l], p, name=t + "merge_fwd")
    x1 = _mm(merged, wt["w_out", l], add=x, name=t + "out_proj")
    sv.update(h=h, p=p, ya=ya, yb=yb, yc=yc, merged=merged, x1=x1, tables=tables)

    h2 = _rms_fwd(x1, sm["norm_xa_g"][l][None], name=t + "rms_xa")
    mn = _rms_fwd(mem, sm["mem_norm_g"][l][None], name=t + "rms_mem")
    q = _mm(h2, wt["w_q_xa", l], out_dtype=BF16, name=t + "xa_q", tm=2048)
    k = _mm(mn, wt["w_k_xa", l], out_dtype=BF16, name=t + "xa_k")
    v = _mm(mn, wt["w_v_xa", l], out_dtype=BF16, name=t + "xa_v")
    o = _xa_fwd(q, k, v, name=t + "xa_fwd")
    x2 = _mm(o, wt["w_o_xa", l], add=x1, name=t + "xa_o")
    sv.update(h2=h2, mn=mn, q=q, k=k, v=v, o=o, x2=x2)

    h3 = _rms_fwd(x2, sm["norm_ffn_g"][l][None], name=t + "rms_ffn")
    a, b, hd = _ffn_in(h3, wt["w_gate_ffn", l], wt["w_up_ffn", l], name=t + "ffn_in")
    x3 = _mm(hd, wt["w_down_ffn", l], add=x2, name=t + "ffn_down", tk=FFN)
    sv.update(h3=h3, a=a, b=b, hd=hd)
    return x3, sv


def _layer_bwd(l, dx3, mem, wt, sm, sv, scatter=None):
    t = f"l{l}_b_"
    gb, gs = {}, {}
    gb["w_down_ffn"] = _mm(sv["hd"], dx3, ta=True, out_dtype=BF16, name=t + "ffn_down_dw", tm=1408)
    da, db = _ffn_in_bwd(dx3, wt["w_down_ffn", l], sv["a"], sv["b"], name=t + "ffn_in_bwd")
    gb["w_gate_ffn"] = _mm(da, sv["h3"], ta=True, out_dtype=BF16, name=t + "ffn_gate_dw", tm=1408)
    gb["w_up_ffn"] = _mm(db, sv["h3"], ta=True, out_dtype=BF16, name=t + "ffn_up_dw", tm=1408)
    dh3 = _mm(da, wt["w_gate_ffn", l], name=t + "ffn_gate_dx", tk=1408)
    dx2, dg = _mm(db, wt["w_up_ffn", l], add=dh3, rms=(sv["x2"], sm["norm_ffn_g"][l][None], dx3),
                  name=t + "ffn_up_dx", tm=512, tk=1408)
    gs["norm_ffn_g"] = dg[0]
    do = _mm(dx2, wt["w_o_xa", l], tb=True, out_dtype=BF16, name=t + "xa_o_dx")
    gb["w_o_xa"] = _mm(sv["o"], dx2, ta=True, out_dtype=BF16, name=t + "xa_o_dw")
    dq, dk, dv = _xa_bwd(sv["q"], sv["k"], sv["v"], do, name=t + "xa_bwd")
    dx1, dg = _mm(dq, wt["w_q_xa", l], tb=True, rms=(sv["x1"], sm["norm_xa_g"][l][None], dx2),
                  name=t + "xa_q_dx", tm=1024)
    gs["norm_xa_g"] = dg[0]
    gb["w_q_xa"] = _mm(sv["h2"], dq, ta=True, out_dtype=BF16, name=t + "xa_q_dw")
    gb["w_k_xa"] = _mm(sv["mn"], dk, ta=True, out_dtype=BF16, name=t + "xa_k_dw")
    gb["w_v_xa"] = _mm(sv["mn"], dv, ta=True, out_dtype=BF16, name=t + "xa_v_dw")
    dmn = _mm(dk, wt["w_k_xa", l], tb=True, name=t + "xa_k_dx")
    dmn = _mm(dv, wt["w_v_xa", l], tb=True, add=dmn, name=t + "xa_v_dx")
    _, dg = _rms_bwd(mem, sm["mem_norm_g"][l][None], dmn, jnp.zeros_like(mem), name=t + "rms_mem")
    gs["mem_norm_g"] = dg[0]
    dm = _mm(dx1, wt["w_out", l], tb=True, name=t + "out_proj_dx")
    gb["w_out"] = _mm(sv["merged"], dx1, ta=True, out_dtype=BF16, name=t + "out_proj_dw")
    p = sv["p"]
    dya, dyb, dyc, dgates, *dbrd = _merge_bwd(dm, sv["ya"], sv["yb"], sv["yc"], wt["w_branch", l], p,
                                              name=t + "merge_bwd")
    gb["w_branch"] = jnp.stack([
        _mm(sv[y], dbrd[n], ta=True, out_dtype=BF16, name=t + f"branch{n}_dw")
        for n, y in enumerate(("ya", "yb", "yc"))])
    dcb, dcc, dcx, dcw = _conv_bwd(p, dyc, sm["conv_w"][l], name=t + "conv_bwd")
    gs["conv_w"] = dcw
    w_sp = sm["w_spatial"][l]
    dz, dlg, dlb, dwsp, dbsp = _sgu_bwd(p, dyb, sm["sgu_ln_g"][l][None], sm["sgu_ln_b"][l][None], w_sp,
                                        jnp.swapaxes(w_sp, 1, 2), sm["b_spatial"][l][:, :, None],
                                        name=t + "sgu_bwd")
    gs.update(sgu_ln_g=dlg[0], sgu_ln_b=dlb[0], w_spatial=dwsp, b_spatial=dbsp[:, :, 0])
    received = None
    if scatter is None:
        dq_a, dk_a, dv_a = _sb_bwd(p, dya, sv["tables"], name=t + "sb_bwd")
    else:
        items, earlier = scatter
        ready = {**earlier, **{(n, l): g for n, g in gb.items()}}
        dq_a, dk_a, dv_a, received = _sb_bwd(p, dya, sv["tables"], name=t + "sb_bwd",
                                             scatter=_pack_full(items, ready))
    dp = jnp.concatenate([dq_a, dk_a, dv_a, dz, dcb, dcc, dcx, dgates], axis=1)
    gb["w_in"] = _mm(dp, sv["h"], ta=True, out_dtype=BF16, name=t + "in_proj_dw")
    dx, dg = _mm(dp, wt["w_in", l], rms=(sv["x0"], sm["norm_mix_g"][l][None], dx1),
                 name=t + "in_proj_dx", tm=1024, tk=1792)
    gs["norm_mix_g"] = dg[0]
    return dx, gb, gs, received


_WEIGHTS = ("norm_mix_g", "w_in", "sgu_ln_g", "sgu_ln_b", "w_spatial", "b_spatial", "conv_w", "w_branch", "w_out",
            "norm_xa_g", "mem_norm_g", "w_q_xa", "w_k_xa", "w_v_xa", "w_o_xa", "norm_ffn_g", "w_gate_ffn",
            "w_up_ffn", "w_down_ffn", "final_g")


def kernel(x, mem, norm_mix_g, w_in, sgu_ln_g, sgu_ln_b, w_spatial, b_spatial, conv_w, w_branch, w_out, norm_xa_g, mem_norm_g, w_q_xa, w_k_xa, w_v_xa, w_o_xa, norm_ffn_g, w_gate_ffn, w_up_ffn, w_down_ffn, final_g, loss_target, m_norm_mix_g, m_w_in, m_sgu_ln_g, m_sgu_ln_b, m_w_spatial, m_b_spatial, m_conv_w, m_w_branch, m_w_out, m_norm_xa_g, m_mem_norm_g, m_w_q_xa, m_w_k_xa, m_w_v_xa, m_w_o_xa, m_norm_ffn_g, m_w_gate_ffn, m_w_up_ffn, m_w_down_ffn, m_final_g, v_norm_mix_g, v_w_in, v_sgu_ln_g, v_sgu_ln_b, v_w_spatial, v_b_spatial, v_conv_w, v_w_branch, v_w_out, v_norm_xa_g, v_mem_norm_g, v_w_q_xa, v_w_k_xa, v_w_v_xa, v_w_o_xa, v_norm_ffn_g, v_w_gate_ffn, v_w_up_ffn, v_w_down_ffn, v_final_g):
    w = dict(norm_mix_g=norm_mix_g, w_in=w_in, sgu_ln_g=sgu_ln_g, sgu_ln_b=sgu_ln_b, w_spatial=w_spatial,
             b_spatial=b_spatial, conv_w=conv_w, w_branch=w_branch, w_out=w_out, norm_xa_g=norm_xa_g,
             mem_norm_g=mem_norm_g, w_q_xa=w_q_xa, w_k_xa=w_k_xa, w_v_xa=w_v_xa, w_o_xa=w_o_xa,
             norm_ffn_g=norm_ffn_g, w_gate_ffn=w_gate_ffn, w_up_ffn=w_up_ffn, w_down_ffn=w_down_ffn, final_g=final_g)
    m = dict(norm_mix_g=m_norm_mix_g, w_in=m_w_in, sgu_ln_g=m_sgu_ln_g, sgu_ln_b=m_sgu_ln_b, w_spatial=m_w_spatial,
             b_spatial=m_b_spatial, conv_w=m_conv_w, w_branch=m_w_branch, w_out=m_w_out, norm_xa_g=m_norm_xa_g,
             mem_norm_g=m_mem_norm_g, w_q_xa=m_w_q_xa, w_k_xa=m_w_k_xa, w_v_xa=m_w_v_xa, w_o_xa=m_w_o_xa,
             norm_ffn_g=m_norm_ffn_g, w_gate_ffn=m_w_gate_ffn, w_up_ffn=m_w_up_ffn, w_down_ffn=m_w_down_ffn,
             final_g=m_final_g)
    v = dict(norm_mix_g=v_norm_mix_g, w_in=v_w_in, sgu_ln_g=v_sgu_ln_g, sgu_ln_b=v_sgu_ln_b, w_spatial=v_w_spatial,
             b_spatial=v_b_spatial, conv_w=v_conv_w, w_branch=v_w_branch, w_out=v_w_out, norm_xa_g=v_norm_xa_g,
             mem_norm_g=v_mem_norm_g, w_q_xa=v_w_q_xa, w_k_xa=v_w_k_xa, w_v_xa=v_w_v_xa, w_o_xa=v_w_o_xa,
             norm_ffn_g=v_norm_ffn_g, w_gate_ffn=v_w_gate_ffn, w_up_ffn=v_w_up_ffn, w_down_ffn=v_w_down_ffn,
             final_g=v_final_g)

    names = [n for n, _, _ in BIG]
    shards = {(n, l): _stored(n, w[n][l].astype(BF16)) for n in names for l in range(DEPTH)}
    first_items = [("w_in", 0)]
    later_items = [(n, l) for l in range(DEPTH) for n in names if (n, l) != ("w_in", 0)]
    wt = _unpack_gathered(first_items, _all_gather(_pack_shards(first_items, shards), name="gather_w_in0"))
    cw_pad = jnp.zeros((8, 128), F32).at[:DEPTH * 3, :BRANCH_W // N_DEV].set(conv_w.reshape(DEPTH * 3, -1))
    cw_all = _all_gather(cw_pad, name="gather_conv_w")[:, :DEPTH * 3, :BRANCH_W // N_DEV]
    conv_full = jnp.moveaxis(cw_all.reshape(N_DEV, DEPTH, 3, BRANCH_W // N_DEV), 0, 2).reshape(DEPTH, 3, BRANCH_W)
    sm = {n: w[n] for n, _ in SMALL}
    sm["conv_w"] = conv_full

    xs, ms = x[0], mem[0]
    x1, saved0 = _layer_fwd(0, xs, ms, wt, sm, gather=(later_items, _pack_shards(later_items, shards)))
    x2, saved1 = _layer_fwd(1, x1, ms, wt, sm)
    dcur, loss, dfinal = _final_loss(x2, sm["final_g"][None], loss_target[0], name="final_loss")
    loss = lax.psum(loss[0, 0], AXES)
    items_a = [(n, 1) for n in names if n != "w_in"]
    items_b = [("w_in", 1)] + [(n, 0) for n in names if n != "w_in"]
    items_c = [("w_in", 0)]
    dcur, gb1, gs1, recv_a = _layer_bwd(1, dcur, ms, wt, sm, saved1, scatter=(items_a, {}))
    dx, gb0, gs0, recv_b = _layer_bwd(0, dcur, ms, wt, sm, saved0, scatter=(items_b, {("w_in", 1): gb1["w_in"]}))

    shard_grads = _unpack_shard(items_a, _sum_devices(recv_a, name="rs_sum_a"))
    shard_grads.update(_unpack_shard(items_b, _sum_devices(recv_b, name="rs_sum_b")))
    g8 = _pack_full(items_c, {("w_in", 0): gb0["w_in"]})
    core = lax.axis_index("c").astype(jnp.int32).reshape(1)
    from_sibling = _rs_pair_exchange(g8, name="rs_pair_exchange")
    part = _pair_add(core, g8, from_sibling, name="rs_pair_add")
    by_chip = _rs_chip_exchange(part, name="rs_chip_exchange")
    shard_grads.update(_unpack_shard(items_c, _sum_chips(by_chip, name="rs_sum_chips")))
    grads = {n: jnp.stack([_stored(n, shard_grads[n, l]) for l in range(DEPTH)]) for n in names}
    small = {n: jnp.stack([gs0[n], gs1[n]]) for n, _ in SMALL if n != "final_g"}
    small["final_g"] = dfinal[0]
    small_sum = _unpack_small(_all_reduce_small(_pack_small(small), name="all_reduce_small"))
    width = BRANCH_W // N_DEV
    dev = 4 * lax.axis_index("x") + 2 * lax.axis_index("y") + lax.axis_index("c")
    for n, _ in SMALL:
        grads[n] = small_sum[n]
    grads["conv_w"] = lax.dynamic_slice_in_dim(small_sum["conv_w"], dev * width, width, axis=2)

    delta, new_m, new_v = {}, {}, {}
    for n in _WEIGHTS:
        shp = w[n].shape
        two_d = (-1, shp[-1])
        d_, m_, v_ = _adamw(w[n].reshape(two_d), grads[n].reshape(two_d), m[n].reshape(two_d), v[n].reshape(two_d),
                            name="adamw_" + n)
        delta[n], new_m[n], new_v[n] = d_.reshape(shp), m_.reshape(shp), v_.reshape(shp)

    return (loss, dx[None], *[grads[n] for n in _WEIGHTS], *[delta[n] for n in _WEIGHTS],
            *[new_m[n] for n in _WEIGHTS], *[new_v[n] for n in _WEIGHTS])
```

```python
import functools

import jax
import jax.numpy as jnp
from jax import lax
from jax.experimental import pallas as pl
from jax.experimental.pallas import tpu as pltpu

F32 = jnp.float32
BF16 = jnp.bfloat16
MESH = pl.DeviceIdType.MESH

D_MODEL = 1024
BRANCH_W = 512
IN_COLS = 7168
FFN = 2816
N_DEV = 8
DEPTH = 2
SB_BLOCK = 128
SB_SPAN = 1024
SB_Q_FWD = 512
SB_Q_BWD = 512
SB_SCALE = 0.125
XA_HEAD = 256
XA_SCALE = 0.0625
SGU_LEN = 128
SGU_GROUPS = 4
RMS_EPS = 1e-6
LN_EPS = 1e-5
HALO = 8

ADAM_LR = 0.001
ADAM_B1 = 0.9
ADAM_B2 = 0.999
ADAM_EPS = 1e-08
ADAM_WD = 0.01
ADAM_STEP = 10

VMEM_LIMIT_BYTES = 52 * 1024 * 1024

AXES = ("x", "y", "c")


def _cparams(*sem):
    return pltpu.CompilerParams(dimension_semantics=sem, vmem_limit_bytes=VMEM_LIMIT_BYTES)


def _pick(n, target, align):
    t = (min(target, n) // align) * align
    while t >= align:
        if n % t == 0:
            return t
        t -= align
    return n


def _dot(a, b):
    return jnp.dot(a, b, preferred_element_type=F32)


def _dot_nt(a, b):
    return lax.dot_general(a, b, (((1,), (1,)), ((), ())), preferred_element_type=F32)


def _dot_tn(a, b):
    return lax.dot_general(a, b, (((0,), (0,)), ((), ())), preferred_element_type=F32)


def _sigmoid(x):
    return 1.0 / (1.0 + jnp.exp(-x))


def _mm(a, b, *, name, ta=False, tb=False, out_dtype=F32, add=None, rms=None, tm=1024, tn=1024, tk=2048):
    m, k = (a.shape[1], a.shape[0]) if ta else a.shape
    n = b.shape[0] if tb else b.shape[1]
    assert k == (b.shape[1] if tb else b.shape[0])
    tm = _pick(m, tm, 128)
    tn = n if rms is not None else _pick(n, tn, 128)
    tk = _pick(k, tk, 128)
    nk = k // tk
    ca = 0 if ta else 1
    cb = 1 if tb else 0
    n_add = 0 if add is None else 1
    n_rms = 0 if rms is None else 3

    def body(*refs):
        refs = list(refs)
        a_ref, b_ref = refs[:2]
        extra = refs[2:2 + n_add + n_rms]
        outs = refs[2 + n_add + n_rms:]
        o_ref = outs[0]
        kk = pl.program_id(2)
        first_row_tile = pl.program_id(0) == 0

        def product():
            return lax.dot_general(a_ref[...].astype(BF16), b_ref[...].astype(BF16),
                                   (((ca,), (cb,)), ((), ())), preferred_element_type=F32)

        def finish(r):
            if add is not None:
                r = r + extra[0][...]
            if rms is None:
                o_ref[...] = r.astype(out_dtype)
                return
            x_ref, g_ref, dres_ref = extra[n_add:]
            dg_ref = outs[1]

            @pl.when(first_row_tile)
            def _():
                dg_ref[...] = jnp.zeros_like(dg_ref)

            xv = x_ref[...]
            rs = lax.rsqrt(jnp.mean(xv * xv, axis=-1, keepdims=True) + RMS_EPS)
            xh = xv * rs
            dg_ref[...] += jnp.sum(r * xh, axis=0, keepdims=True)
            dxh = r * g_ref[...]
            o_ref[...] = dres_ref[...] + rs * (dxh - xh * jnp.mean(dxh * xh, axis=-1, keepdims=True))

        if nk == 1:
            finish(product())
        else:
            acc_ref = outs[-1]

            @pl.when(kk == 0)
            def _():
                acc_ref[...] = jnp.zeros_like(acc_ref)

            acc_ref[...] += product()

            @pl.when(kk == nk - 1)
            def _():
                finish(acc_ref[...])

    a_spec = pl.BlockSpec((tk, tm), lambda i, j, kk: (kk, i)) if ta else pl.BlockSpec((tm, tk), lambda i, j, kk: (i, kk))
    b_spec = pl.BlockSpec((tn, tk), lambda i, j, kk: (j, kk)) if tb else pl.BlockSpec((tk, tn), lambda i, j, kk: (kk, j))
    tile = pl.BlockSpec((tm, tn), lambda i, j, kk: (i, j))
    in_specs = [a_spec, b_spec]
    operands = [a, b]
    out_specs = [tile]
    out_shape = [jax.ShapeDtypeStruct((m, n), out_dtype)]
    if add is not None:
        in_specs.append(tile)
        operands.append(add)
    if rms is not None:
        vec = pl.BlockSpec((1, n), lambda i, j, kk: (0, 0))
        in_specs += [tile, vec, tile]
        operands += list(rms)
        out_specs.append(vec)
        out_shape = [jax.ShapeDtypeStruct((m, n), F32), jax.ShapeDtypeStruct((1, n), F32)]
    out = pl.pallas_call(
        body, name=name,
        grid=(m // tm, n // tn, nk),
        in_specs=in_specs, out_specs=out_specs, out_shape=out_shape,
        scratch_shapes=[pltpu.VMEM((tm, tn), F32)] if nk > 1 else [],
        compiler_params=_cparams("arbitrary" if rms is not None else "parallel", "parallel", "arbitrary"),
    )(*operands)
    return out[0] if rms is None else out


def _rms_fwd(x, g, *, name):
    r, d = x.shape
    tr = _pick(r, 512, 16)

    def body(x_ref, g_ref, o_ref):
        xv = x_ref[...]
        rs = lax.rsqrt(jnp.mean(xv * xv, axis=-1, keepdims=True) + RMS_EPS)
        o_ref[...] = (xv * rs * g_ref[...]).astype(BF16)

    return pl.pallas_call(
        body, name=name, grid=(r // tr,),
        in_specs=[pl.BlockSpec((tr, d), lambda i: (i, 0)), pl.BlockSpec((1, d), lambda i: (0, 0))],
        out_specs=pl.BlockSpec((tr, d), lambda i: (i, 0)),
        out_shape=jax.ShapeDtypeStruct((r, d), BF16),
        compiler_params=_cparams("parallel"),
    )(x, g)


def _rms_bwd(x, g, dh, dres, *, name):
    r, d = x.shape
    tr = _pick(r, 256, 8)

    def body(x_ref, g_ref, dh_ref, dres_ref, dx_ref, dg_ref):
        @pl.when(pl.program_id(0) == 0)
        def _():
            dg_ref[...] = jnp.zeros_like(dg_ref)

        xv = x_ref[...]
        dhv = dh_ref[...].astype(F32)
        rs = lax.rsqrt(jnp.mean(xv * xv, axis=-1, keepdims=True) + RMS_EPS)
        xh = xv * rs
        dg_ref[...] += jnp.sum(dhv * xh, axis=0, keepdims=True)
        dxh = dhv * g_ref[...]
        dx_ref[...] = dres_ref[...] + rs * (dxh - xh * jnp.mean(dxh * xh, axis=-1, keepdims=True))

    return pl.pallas_call(
        body, name=name, grid=(r // tr,),
        in_specs=[pl.BlockSpec((tr, d), lambda i: (i, 0)), pl.BlockSpec((1, d), lambda i: (0, 0)),
                  pl.BlockSpec((tr, d), lambda i: (i, 0)), pl.BlockSpec((tr, d), lambda i: (i, 0))],
        out_specs=[pl.BlockSpec((tr, d), lambda i: (i, 0)), pl.BlockSpec((1, d), lambda i: (0, 0))],
        out_shape=[jax.ShapeDtypeStruct((r, d), F32), jax.ShapeDtypeStruct((1, d), F32)],
        compiler_params=_cparams("arbitrary"),
    )(x, g, dh, dres)


def _final_loss(x, g, target, *, name):
    r, d = x.shape
    tr = _pick(r, 256, 8)

    def body(x_ref, g_ref, t_ref, dx_ref, loss_ref, dg_ref):
        @pl.when(pl.program_id(0) == 0)
        def _():
            dg_ref[...] = jnp.zeros_like(dg_ref)
            loss_ref[...] = jnp.zeros_like(loss_ref)

        xv = x_ref[...]
        gv = g_ref[...]
        rs = lax.rsqrt(jnp.mean(xv * xv, axis=-1, keepdims=True) + RMS_EPS)
        xh = xv * rs
        err = xh * gv - t_ref[...]
        row_loss = jnp.mean(err * err, axis=-1, keepdims=True)
        loss_ref[...] += 0.5 * jnp.sum(row_loss, axis=0, keepdims=True)
        dy = err * (1.0 / d)
        dg_ref[...] += jnp.sum(dy * xh, axis=0, keepdims=True)
        dxh = dy * gv
        dx_ref[...] = rs * (dxh - xh * jnp.mean(dxh * xh, axis=-1, keepdims=True))

    return pl.pallas_call(
        body, name=name, grid=(r // tr,),
        in_specs=[pl.BlockSpec((tr, d), lambda i: (i, 0)), pl.BlockSpec((1, d), lambda i: (0, 0)),
                  pl.BlockSpec((tr, d), lambda i: (i, 0))],
        out_specs=[pl.BlockSpec((tr, d), lambda i: (i, 0)), pl.BlockSpec((1, 128), lambda i: (0, 0)),
                   pl.BlockSpec((1, d), lambda i: (0, 0))],
        out_shape=[jax.ShapeDtypeStruct((r, d), F32), jax.ShapeDtypeStruct((1, 128), F32),
                   jax.ShapeDtypeStruct((1, d), F32)],
        compiler_params=_cparams("arbitrary"),
    )(x, g, target)


def _cumsum_operand(strict_after, totals=True):
    width = (2 if totals else 1) * SB_BLOCK
    r = lax.broadcasted_iota(jnp.int32, (SB_BLOCK, width), 0)
    c = lax.broadcasted_iota(jnp.int32, (SB_BLOCK, width), 1)
    tri = (r > c) if strict_after else (r < c)
    return jnp.where((c >= SB_BLOCK) | tri, 1.0, 0.0).astype(BF16)


def _sb_scores(qh, kw, run, valid, after_ones):
    nb = kw.shape[0] // SB_BLOCK
    z = _dot_nt(qh, kw)
    lsp = jnp.minimum(z, 0.0) - jnp.log(1.0 + jnp.exp(-jnp.abs(z)))
    l1m = lsp - z
    if valid is not None:
        l1m = jnp.where(valid, l1m, 0.0)
    l1b = l1m.astype(BF16)
    later = [None] * nb
    seen = [None] * nb
    for b in reversed(range(nb)):
        cols = slice(b * SB_BLOCK, (b + 1) * SB_BLOCK)
        ct = _dot(l1b[:, cols], after_ones)
        seen[b] = run
        later[b] = run + ct[:, :SB_BLOCK]
        run = run + ct[:, SB_BLOCK:]
    a = jnp.exp(lsp + jnp.concatenate(later, axis=1))
    if valid is not None:
        a = jnp.where(valid, a, 0.0)
    return a, run, seen


def _sb_setup(q_ref, span):
    qi = pl.program_id(1)
    rows = q_ref.shape[0]
    sd = (qi * rows + rows - 1) // span
    lane = lax.broadcasted_iota(jnp.int32, (rows, SB_BLOCK), 1)
    col = lax.broadcasted_iota(jnp.int32, (rows, span), 1)
    row = lax.broadcasted_iota(jnp.int32, (rows, span), 0)
    valid = col < (qi * rows - sd * span) + row
    q = q_ref[...] * SB_SCALE
    qhs = (jnp.where(lane < 64, q, 0.0).astype(BF16), jnp.where(lane >= 64, q, 0.0).astype(BF16))
    return lane, sd, valid, qhs


def _sb_fwd(p, *, name, gather=None):
    s = p.shape[0]
    qrows = min(SB_Q_FWD, s)
    nq = s // qrows
    kcol = BRANCH_W // SB_BLOCK
    span = min(SB_SPAN, s)
    per = span // SB_BLOCK
    assert s // SB_BLOCK <= SB_BLOCK

    def body(*refs):
        if gather is None:
            q_ref, k_ref, v_ref, o_ref, r0_ref, r1_ref = refs
        else:
            q_ref, k_ref, v_ref, x_ref, o_ref, r0_ref, r1_ref, g_ref, send_sems, recv_sems, local_sem = refs
            start, forward, finish = _gather_phases(x_ref, g_ref, send_sems, recv_sems, local_sem)
            step = pl.program_id(0) * nq + pl.program_id(1)
            pl.when(step == 0)(start)
        lane, sd, valid, qhs = _sb_setup(q_ref, span)
        after_ones = _cumsum_operand(True)
        zero = jnp.zeros((qrows, SB_BLOCK), F32)
        lane_row = lax.broadcasted_iota(jnp.int32, (1, SB_BLOCK), 1)

        def span_step(sb, carry, mask):
            rows = pl.ds(pl.multiple_of(sb * span, span), span)
            kw = k_ref[rows, :].astype(BF16)
            vw = v_ref[rows, :].astype(BF16)
            out = []
            for h in range(2):
                run, acc, table = carry[h]
                a, run, seen = _sb_scores(qhs[h], kw, run, mask, after_ones)
                for b in range(per):
                    table = jnp.where(lane_row == sb * per + b, seen[b], table)
                out.append((run, acc + _dot(a.astype(BF16), vw), table))
            return tuple(out)

        carry = span_step(sd, ((zero, zero, zero), (zero, zero, zero)), valid)
        carry = lax.fori_loop(0, sd, lambda t, c: span_step(sd - 1 - t, c, None), carry)
        o_ref[...] = jnp.where(lane < 64, carry[0][1], carry[1][1]).astype(BF16)
        r0_ref[...] = carry[0][2]
        r1_ref[...] = carry[1][2]
        if gather is not None:
            pl.when(step == (kcol - 1) * nq + (3 * nq) // 4)(forward)
            pl.when(step == kcol * nq - 1)(finish)

    in_specs = [pl.BlockSpec((qrows, SB_BLOCK), lambda hp, qi: (qi, hp)),
                pl.BlockSpec((s, SB_BLOCK), lambda hp, qi: (0, kcol + hp)),
                pl.BlockSpec((s, SB_BLOCK), lambda hp, qi: (0, 2 * kcol + hp))]
    table = pl.BlockSpec((None, qrows, SB_BLOCK), lambda hp, qi: (hp, qi, 0))
    out_specs = [pl.BlockSpec((qrows, SB_BLOCK), lambda hp, qi: (qi, hp)), table, table]
    out_shape = [jax.ShapeDtypeStruct((s, BRANCH_W), BF16)] + [jax.ShapeDtypeStruct((kcol, s, SB_BLOCK), F32)] * 2
    operands = [p, p, p]
    scratch = []
    if gather is not None:
        in_specs.append(pl.BlockSpec(memory_space=pl.ANY))
        out_specs.append(pl.BlockSpec(memory_space=pl.ANY))
        out_shape.append(jax.ShapeDtypeStruct((N_DEV,) + gather.shape, gather.dtype))
        operands.append(gather)
        scratch = _GATHER_SEMS
    out = pl.pallas_call(
        body, name=name, grid=(kcol, nq), in_specs=in_specs, out_specs=out_specs, out_shape=out_shape,
        scratch_shapes=scratch, compiler_params=_cparams("arbitrary", "arbitrary"),
    )(*operands)
    return out


def _sb_bwd(p, dya, tables, *, name, scatter=None):
    s = p.shape[0]
    qrows = min(SB_Q_BWD, s)
    nq = s // qrows
    kcol = BRANCH_W // SB_BLOCK
    span = min(SB_SPAN, s)
    per = span // SB_BLOCK

    def body(*refs):
        if scatter is None:
            q_ref, k_ref, v_ref, do_ref, t0_ref, t1_ref, dq_ref, dk_ref, dv_ref, dk_acc, dv_acc = refs
        else:
            (q_ref, k_ref, v_ref, do_ref, t0_ref, t1_ref, g_ref, dq_ref, dk_ref, dv_ref, r_ref,
             dk_acc, dv_acc, send_sems, recv_sems, local_sem) = refs
            start, finish = _scatter_phases(g_ref, r_ref, send_sems, recv_sems, local_sem)
            step = pl.program_id(0) * nq + pl.program_id(1)
            pl.when(step == 0)(start)
        qi = pl.program_id(1)

        @pl.when(qi == 0)
        def _():
            dk_acc[...] = jnp.zeros_like(dk_acc)
            dv_acc[...] = jnp.zeros_like(dv_acc)

        lane, sd, valid, qhs = _sb_setup(q_ref, span)
        after = _cumsum_operand(True, totals=False)
        before_ones = _cumsum_operand(False)
        do = do_ref[...]
        dohs = (jnp.where(lane < 64, do, 0.0).astype(BF16), jnp.where(lane >= 64, do, 0.0).astype(BF16))
        tabs = (t0_ref[...], t1_ref[...])
        lane_row = lax.broadcasted_iota(jnp.int32, (1, SB_BLOCK), 1)
        zero = jnp.zeros((qrows, SB_BLOCK), F32)

        def span_step(sb, carry, mask):
            rows = pl.ds(pl.multiple_of(sb * span, span), span)
            kw = k_ref[rows, :].astype(BF16)
            vw = v_ref[rows, :].astype(BF16)
            out = []
            dk_span = jnp.zeros((span, SB_BLOCK), F32)
            dv_span = jnp.zeros((span, SB_BLOCK), F32)
            for h in range(2):
                pg, dq = carry[h]
                z = _dot_nt(qhs[h], kw)
                lsp = jnp.minimum(z, 0.0) - jnp.log(1.0 + jnp.exp(-jnp.abs(z)))
                l1m = lsp - z
                if mask is not None:
                    l1m = jnp.where(mask, l1m, 0.0)
                l1b = l1m.astype(BF16)
                later = [None] * per
                for b in range(per):
                    cols = slice(b * SB_BLOCK, (b + 1) * SB_BLOCK)
                    seen = jnp.sum(jnp.where(lane_row == sb * per + b, tabs[h], 0.0), axis=-1, keepdims=True)
                    later[b] = seen + _dot(l1b[:, cols], after)
                a = jnp.exp(lsp + jnp.concatenate(later, axis=1))
                beta = jnp.exp(lsp)
                if mask is not None:
                    a = jnp.where(mask, a, 0.0)
                    beta = jnp.where(mask, beta, 0.0)
                g = a * _dot_nt(dohs[h], vw)
                gb = g.astype(BF16)
                before = [None] * per
                for b in range(per):
                    cols = slice(b * SB_BLOCK, (b + 1) * SB_BLOCK)
                    gt = _dot(gb[:, cols], before_ones)
                    before[b] = pg + gt[:, :SB_BLOCK]
                    pg = pg + gt[:, SB_BLOCK:]
                dz = (g * (1.0 - beta) - beta * jnp.concatenate(before, axis=1)).astype(BF16)
                dk_span = dk_span + _dot_tn(dz, qhs[h])
                dv_span = dv_span + _dot_tn(a.astype(BF16), dohs[h])
                out.append((pg, dq + _dot(dz, kw)))
            dk_acc[rows, :] += dk_span
            dv_acc[rows, :] += dv_span
            return tuple(out)

        carry = lax.fori_loop(0, sd, lambda sb, c: span_step(sb, c, None), ((zero, zero), (zero, zero)))
        carry = span_step(sd, carry, valid)
        dq_ref[...] = (jnp.where(lane < 64, carry[0][1], carry[1][1]) * SB_SCALE).astype(BF16)

        @pl.when(qi == nq - 1)
        def _():
            dk_ref[...] = dk_acc[...].astype(BF16)
            dv_ref[...] = dv_acc[...].astype(BF16)

        if scatter is not None:
            pl.when(step == kcol * nq - 1)(finish)

    blk = pl.BlockSpec((qrows, SB_BLOCK), lambda hp, qi: (qi, hp))
    col = pl.BlockSpec((s, SB_BLOCK), lambda hp, qi: (0, hp))
    table = pl.BlockSpec((None, qrows, SB_BLOCK), lambda hp, qi: (hp, qi, 0))
    out = jax.ShapeDtypeStruct((s, BRANCH_W), BF16)
    in_specs = [blk,
                pl.BlockSpec((s, SB_BLOCK), lambda hp, qi: (0, kcol + hp)),
                pl.BlockSpec((s, SB_BLOCK), lambda hp, qi: (0, 2 * kcol + hp)),
                blk, table, table]
    out_specs = [blk, col, col]
    out_shape = [out, out, out]
    operands = [p, p, p, dya, tables[0], tables[1]]
    scratch = [pltpu.VMEM((s, SB_BLOCK), F32), pltpu.VMEM((s, SB_BLOCK), F32)]
    if scatter is not None:
        in_specs.append(pl.BlockSpec(memory_space=pl.ANY))
        out_specs.append(pl.BlockSpec(memory_space=pl.ANY))
        out_shape.append(jax.ShapeDtypeStruct(scatter.shape, scatter.dtype))
        operands.append(scatter)
        scratch = scratch + _SCATTER_SEMS
    return pl.pallas_call(
        body, name=name, grid=(kcol, nq), in_specs=in_specs, out_specs=out_specs, out_shape=out_shape,
        scratch_shapes=scratch, compiler_params=_cparams("arbitrary", "arbitrary"),
    )(*operands)


_INV_SQRT2 = 0.7071067811865476
_INV_SQRT2PI = 0.3989422804014327


def _gelu(x):
    return 0.5 * x * (1.0 + lax.erf(x * _INV_SQRT2))


def _gelu_grad(x):
    return 0.5 * (1.0 + lax.erf(x * _INV_SQRT2)) + x * _INV_SQRT2PI * jnp.exp(-0.5 * x * x)


def _chunk_mask(transposed=False):
    r = lax.broadcasted_iota(jnp.int32, (SGU_LEN, SGU_LEN), 0)
    c = lax.broadcasted_iota(jnp.int32, (SGU_LEN, SGU_LEN), 1)
    return (c // 64) >= (r // 64) if transposed else (r // 64) >= (c // 64)


def _sgu_norm(v_raw, g, b):
    zv = _gelu(v_raw)
    xc = zv - jnp.mean(zv, axis=-1, keepdims=True)
    rs = lax.rsqrt(jnp.mean(xc * xc, axis=-1, keepdims=True) + LN_EPS)
    xh = xc * rs
    return xh, rs, xh * g + b


def _sgu_fwd(p, ln_g, ln_b, w, b_col, *, name):
    s = p.shape[0]
    tr = _pick(s, 512, SGU_LEN)

    def body(u_ref, v_ref, g_ref, b_ref, w_ref, bc_ref, o_ref):
        mask = _chunk_mask()
        zu = _gelu(u_ref[...])
        _, _, vn = _sgu_norm(v_ref[...], g_ref[...], b_ref[...])
        vnb = vn.astype(BF16)
        for gi in range(SGU_GROUPS):
            wg = jnp.where(mask, w_ref[gi], 0.0).astype(BF16)
            cs = slice(gi * SGU_LEN, (gi + 1) * SGU_LEN)
            for c in range(tr // SGU_LEN):
                rs_ = slice(c * SGU_LEN, (c + 1) * SGU_LEN)
                vm = _dot(wg, vnb[rs_, cs]) + bc_ref[gi]
                o_ref[rs_, cs] = (zu[rs_, cs] * vm).astype(BF16)

    vec = pl.BlockSpec((1, BRANCH_W), lambda i: (0, 0))
    return pl.pallas_call(
        body, name=name, grid=(s // tr,),
        in_specs=[pl.BlockSpec((tr, BRANCH_W), lambda i: (i, 3)), pl.BlockSpec((tr, BRANCH_W), lambda i: (i, 4)),
                  vec, vec,
                  pl.BlockSpec((SGU_GROUPS, SGU_LEN, SGU_LEN), lambda i: (0, 0, 0)),
                  pl.BlockSpec((SGU_GROUPS, SGU_LEN, 1), lambda i: (0, 0, 0))],
        out_specs=pl.BlockSpec((tr, BRANCH_W), lambda i: (i, 0)),
        out_shape=jax.ShapeDtypeStruct((s, BRANCH_W), BF16),
        compiler_params=_cparams("parallel"),
    )(p, p, ln_g, ln_b, w, b_col)


def _sgu_bwd(p, dyb, ln_g, ln_b, w, w_t, b_col, *, name):
    s = p.shape[0]
    tr = _pick(s, 256, SGU_LEN)

    def body(u_ref, v_ref, dy_ref, g_ref, b_ref, w_ref, wt_ref, bc_ref,
             dz_ref, dg_ref, db_ref, dw_ref, dbc_ref, dvn_s):
        @pl.when(pl.program_id(0) == 0)
        def _():
            dg_ref[...] = jnp.zeros_like(dg_ref)
            db_ref[...] = jnp.zeros_like(db_ref)
            dw_ref[...] = jnp.zeros_like(dw_ref)
            dbc_ref[...] = jnp.zeros_like(dbc_ref)

        mask = _chunk_mask()
        mask_t = _chunk_mask(transposed=True)
        u_raw = u_ref[...]
        v_raw = v_ref[...]
        dy = dy_ref[...]
        zu = _gelu(u_raw)
        xh, rs, vn = _sgu_norm(v_raw, g_ref[...], b_ref[...])
        vnb = vn.astype(BF16)
        dvm_all = dy * zu
        for gi in range(SGU_GROUPS):
            wg = jnp.where(mask, w_ref[gi], 0.0).astype(BF16)
            wgt = jnp.where(mask_t, wt_ref[gi], 0.0).astype(BF16)
            cs = slice(gi * SGU_LEN, (gi + 1) * SGU_LEN)
            dw_g = jnp.zeros((SGU_LEN, SGU_LEN), F32)
            db_g = jnp.zeros((SGU_LEN, 1), F32)
            for c in range(tr // SGU_LEN):
                rs_ = slice(c * SGU_LEN, (c + 1) * SGU_LEN)
                vm = _dot(wg, vnb[rs_, cs]) + bc_ref[gi]
                dz_ref[rs_, cs] = (dy[rs_, cs] * vm * _gelu_grad(u_raw[rs_, cs])).astype(BF16)
                dvm = dvm_all[rs_, cs]
                dvmb = dvm.astype(BF16)
                dw_g = dw_g + _dot_nt(dvmb, vnb[rs_, cs])
                db_g = db_g + jnp.sum(dvm, axis=1, keepdims=True)
                dvn_s[rs_, cs] = _dot(wgt, dvmb)
            dw_ref[gi] += jnp.where(mask, dw_g, 0.0)
            dbc_ref[gi] += db_g
        dvn = dvn_s[...]
        dg_ref[...] += jnp.sum(dvn * xh, axis=0, keepdims=True)
        db_ref[...] += jnp.sum(dvn, axis=0, keepdims=True)
        dxh = dvn * g_ref[...]
        dzv = rs * (dxh - jnp.mean(dxh, axis=-1, keepdims=True) - xh * jnp.mean(dxh * xh, axis=-1, keepdims=True))
        dz_ref[:, BRANCH_W:] = (dzv * _gelu_grad(v_raw)).astype(BF16)

    vec = pl.BlockSpec((1, BRANCH_W), lambda i: (0, 0))
    wspec = pl.BlockSpec((SGU_GROUPS, SGU_LEN, SGU_LEN), lambda i: (0, 0, 0))
    bspec = pl.BlockSpec((SGU_GROUPS, SGU_LEN, 1), lambda i: (0, 0, 0))
    return pl.pallas_call(
        body, name=name, grid=(s // tr,),
        in_specs=[pl.BlockSpec((tr, BRANCH_W), lambda i: (i, 3)), pl.BlockSpec((tr, BRANCH_W), lambda i: (i, 4)),
                  pl.BlockSpec((tr, BRANCH_W), lambda i: (i, 0)), vec, vec, wspec, wspec, bspec],
        out_specs=[pl.BlockSpec((tr, 2 * BRANCH_W), lambda i: (i, 0)), vec, vec, wspec, bspec],
        out_shape=[jax.ShapeDtypeStruct((s, 2 * BRANCH_W), BF16),
                   jax.ShapeDtypeStruct((1, BRANCH_W), F32), jax.ShapeDtypeStruct((1, BRANCH_W), F32),
                   jax.ShapeDtypeStruct((SGU_GROUPS, SGU_LEN, SGU_LEN), F32),
                   jax.ShapeDtypeStruct((SGU_GROUPS, SGU_LEN, 1), F32)],
        scratch_shapes=[pltpu.VMEM((tr, BRANCH_W), F32)],
        compiler_params=_cparams("arbitrary"),
    )(p, p, dyb, ln_g, ln_b, w, w_t, b_col)


def _shift_down(x, prev8, k):
    rolled = pltpu.roll(x, k, 0)
    r8 = lax.broadcasted_iota(jnp.int32, prev8.shape, 0)
    head = jnp.where(r8 < k, pltpu.roll(prev8, k, 0), rolled[:HALO])
    return jnp.concatenate([head, rolled[HALO:]], axis=0)


def _shift_up(x, next8, k):
    n = x.shape[0]
    rolled = pltpu.roll(x, n - k, 0)
    r8 = lax.broadcasted_iota(jnp.int32, next8.shape, 0)
    tail = jnp.where(r8 >= HALO - k, pltpu.roll(next8, HALO - k, 0), rolled[n - HALO:])
    return jnp.concatenate([rolled[:n - HALO], tail], axis=0)


def _conv_specs(s, tr):
    nb = tr // HALO
    last = s // HALO - 1
    tile = lambda cb: pl.BlockSpec((tr, 128), lambda j, i: (i, cb * 4 + j))
    above = lambda cb: pl.BlockSpec((HALO, 128), lambda j, i: (jnp.maximum(i * nb - 1, 0), cb * 4 + j))
    below = lambda cb: pl.BlockSpec((HALO, 128), lambda j, i: (jnp.minimum((i + 1) * nb, last), cb * 4 + j))
    return tile, above, below


def _conv_fwd(p, cw, *, name):
    s = p.shape[0]
    tr = _pick(s, 512, HALO)
    tile, above, _ = _conv_specs(s, tr)

    def body(cb_ref, cc_ref, cx_ref, ccp_ref, cxp_ref, w_ref, o_ref):
        first = pl.program_id(1) == 0
        y = cc_ref[...] * cx_ref[...]
        yp = jnp.where(first, 0.0, ccp_ref[...] * cxp_ref[...])
        conv = w_ref[2:3, :] * y + w_ref[1:2, :] * _shift_down(y, yp, 1) + w_ref[0:1, :] * _shift_down(y, yp, 2)
        o_ref[...] = (cb_ref[...] * conv).astype(BF16)

    return pl.pallas_call(
        body, name=name, grid=(4, s // tr),
        in_specs=[tile(5), tile(6), tile(7), above(6), above(7), pl.BlockSpec((3, 128), lambda j, i: (0, j))],
        out_specs=pl.BlockSpec((tr, 128), lambda j, i: (i, j)),
        out_shape=jax.ShapeDtypeStruct((s, BRANCH_W), BF16),
        compiler_params=_cparams("parallel", "parallel"),
    )(p, p, p, p, p, cw)


def _conv_bwd(p, dyc, cw, *, name):
    s = p.shape[0]
    tr = _pick(s, 512, HALO)
    nt = s // tr
    nb = tr // HALO
    last = s // HALO - 1
    tile, above, below = _conv_specs(s, tr)

    def body(cb_ref, cc_ref, cx_ref, ccp_ref, cxp_ref, cbn_ref, dy_ref, dyn_ref, w_ref,
             dcb_ref, dcc_ref, dcx_ref, dw_ref):
        i = pl.program_id(1)

        @pl.when(i == 0)
        def _():
            dw_ref[...] = jnp.zeros_like(dw_ref)

        cb = cb_ref[...]
        cc = cc_ref[...]
        cx = cx_ref[...]
        y = cc * cx
        yp = jnp.where(i == 0, 0.0, ccp_ref[...] * cxp_ref[...])
        y1 = _shift_down(y, yp, 1)
        y2 = _shift_down(y, yp, 2)
        w0, w1, w2 = w_ref[0:1, :], w_ref[1:2, :], w_ref[2:3, :]
        conv = w2 * y + w1 * y1 + w0 * y2
        dyc_v = dy_ref[...]
        dconv = dyc_v * cb
        dn = jnp.where(i == nt - 1, 0.0, dyn_ref[...] * cbn_ref[...])
        dyv = w2 * dconv + w1 * _shift_up(dconv, dn, 1) + w0 * _shift_up(dconv, dn, 2)
        dcb_ref[...] = (dyc_v * conv).astype(BF16)
        dcc_ref[...] = (dyv * cx).astype(BF16)
        dcx_ref[...] = (dyv * cc).astype(BF16)
        dw_ref[0:1, :] += jnp.sum(dconv * y2, axis=0, keepdims=True)
        dw_ref[1:2, :] += jnp.sum(dconv * y1, axis=0, keepdims=True)
        dw_ref[2:3, :] += jnp.sum(dconv * y, axis=0, keepdims=True)

    dy_tile = pl.BlockSpec((tr, 128), lambda j, i: (i, j))
    dy_below = pl.BlockSpec((HALO, 128), lambda j, i: (jnp.minimum((i + 1) * nb, last), j))
    out_tile = lambda cb: pl.BlockSpec((tr, 128), lambda j, i: (i, cb * 4 + j))
    w_spec = pl.BlockSpec((3, 128), lambda j, i: (0, j))
    dcb, dcc, dcx, dw = pl.pallas_call(
        body, name=name, grid=(4, nt),
        in_specs=[tile(5), tile(6), tile(7), above(6), above(7), below(5), dy_tile, dy_below, w_spec],
        out_specs=[dy_tile, dy_tile, dy_tile, w_spec],
        out_shape=[jax.ShapeDtypeStruct((s, BRANCH_W), BF16)] * 3 + [jax.ShapeDtypeStruct((3, BRANCH_W), F32)],
        compiler_params=_cparams("parallel", "arbitrary"),
    )(p, p, p, p, p, p, dyc, dyc, cw)
    return dcb, dcc, dcx, dw


def _merge_fwd(ya, yb, yc, wb, p, *, name):
    s = p.shape[0]
    tr = _pick(s, 256, 16)

    def body(ya_ref, yb_ref, yc_ref, wb_ref, g0_ref, g1_ref, g2_ref, o_ref):
        acc = jnp.zeros((tr, D_MODEL), F32)
        for n, (y_ref, g_ref) in enumerate(((ya_ref, g0_ref), (yb_ref, g1_ref), (yc_ref, g2_ref))):
            acc = acc + _sigmoid(g_ref[...]) * _dot(y_ref[...].astype(BF16), wb_ref[n])
        o_ref[...] = acc.astype(BF16)

    yspec = pl.BlockSpec((tr, BRANCH_W), lambda i: (i, 0))
    gate = lambda n: pl.BlockSpec((tr, D_MODEL), lambda i: (i, 4 + n))
    return pl.pallas_call(
        body, name=name, grid=(s // tr,),
        in_specs=[yspec, yspec, yspec, pl.BlockSpec((3, BRANCH_W, D_MODEL), lambda i: (0, 0, 0)),
                  gate(0), gate(1), gate(2)],
        out_specs=pl.BlockSpec((tr, D_MODEL), lambda i: (i, 0)),
        out_shape=jax.ShapeDtypeStruct((s, D_MODEL), BF16),
        compiler_params=_cparams("parallel"),
    )(ya, yb, yc, wb, p, p, p)


def _merge_bwd(dm, ya, yb, yc, wb, p, *, name):
    s = p.shape[0]
    tr = _pick(s, 256, 16)

    def body(dm_ref, ya_ref, yb_ref, yc_ref, wb_ref, g0_ref, g1_ref, g2_ref,
             dya_ref, dyb_ref, dyc_ref, dg_ref, dbrd0_ref, dbrd1_ref, dbrd2_ref):
        dmv = dm_ref[...]
        ys = (ya_ref, yb_ref, yc_ref)
        gs = (g0_ref, g1_ref, g2_ref)
        dys = (dya_ref, dyb_ref, dyc_ref)
        dbrds = (dbrd0_ref, dbrd1_ref, dbrd2_ref)
        for n in range(3):
            brd = _dot(ys[n][...].astype(BF16), wb_ref[n])
            sg = _sigmoid(gs[n][...])
            dbrd = (sg * dmv).astype(BF16)
            dbrds[n][...] = dbrd
            dg_ref[:, n * D_MODEL:(n + 1) * D_MODEL] = (dmv * brd * sg * (1.0 - sg)).astype(BF16)
            dys[n][...] = _dot_nt(dbrd, wb_ref[n]).astype(dys[n].dtype)

    yspec = pl.BlockSpec((tr, BRANCH_W), lambda i: (i, 0))
    gate = lambda n: pl.BlockSpec((tr, D_MODEL), lambda i: (i, 4 + n))
    row = pl.BlockSpec((tr, D_MODEL), lambda i: (i, 0))
    return pl.pallas_call(
        body, name=name, grid=(s // tr,),
        in_specs=[row, yspec, yspec, yspec, pl.BlockSpec((3, BRANCH_W, D_MODEL), lambda i: (0, 0, 0)),
                  gate(0), gate(1), gate(2)],
        out_specs=[yspec, yspec, yspec, pl.BlockSpec((tr, 3 * D_MODEL), lambda i: (i, 0)), row, row, row],
        out_shape=[jax.ShapeDtypeStruct((s, BRANCH_W), BF16)] + [jax.ShapeDtypeStruct((s, BRANCH_W), F32)] * 2
                  + [jax.ShapeDtypeStruct((s, 3 * D_MODEL), BF16)] + [jax.ShapeDtypeStruct((s, D_MODEL), BF16)] * 3,
        compiler_params=_cparams("parallel"),
    )(dm, ya, yb, yc, wb, p, p, p)


def _xa_probs(q, k):
    sc = _dot_nt(q, k) * XA_SCALE
    e = jnp.exp(sc - jnp.max(sc, axis=-1, keepdims=True))
    return e / jnp.sum(e, axis=-1, keepdims=True)


def _xa_fwd(q, k, v, *, name):
    s = q.shape[0]
    mt = k.shape[0]
    tr = _pick(s, 512, 16)

    def body(q_ref, k_ref, v_ref, o_ref):
        pr = _xa_probs(q_ref[...], k_ref[...])
        o_ref[...] = _dot(pr.astype(BF16), v_ref[...]).astype(BF16)

    qs = pl.BlockSpec((tr, XA_HEAD), lambda h, i: (i, h))
    ks = pl.BlockSpec((mt, XA_HEAD), lambda h, i: (0, h))
    return pl.pallas_call(
        body, name=name, grid=(D_MODEL // XA_HEAD, s // tr),
        in_specs=[qs, ks, ks], out_specs=qs,
        out_shape=jax.ShapeDtypeStruct((s, D_MODEL), BF16),
        compiler_params=_cparams("parallel", "parallel"),
    )(q, k, v)


def _xa_bwd(q, k, v, do, *, name):
    s = q.shape[0]
    mt = k.shape[0]
    tr = _pick(s, 512, 16)

    def body(q_ref, k_ref, v_ref, do_ref, dq_ref, dk_ref, dv_ref):
        @pl.when(pl.program_id(1) == 0)
        def _():
            dk_ref[...] = jnp.zeros_like(dk_ref)
            dv_ref[...] = jnp.zeros_like(dv_ref)

        qv = q_ref[...]
        kv = k_ref[...]
        dov = do_ref[...]
        pr = _xa_probs(qv, kv)
        dpr = _dot_nt(dov, v_ref[...])
        ds = (pr * (dpr - jnp.sum(dpr * pr, axis=-1, keepdims=True)) * XA_SCALE).astype(BF16)
        dq_ref[...] = _dot(ds, kv).astype(BF16)
        dk_ref[...] += _dot_tn(ds, qv)
        dv_ref[...] += _dot_tn(pr.astype(BF16), dov)

    qs = pl.BlockSpec((tr, XA_HEAD), lambda h, i: (i, h))
    ks = pl.BlockSpec((mt, XA_HEAD), lambda h, i: (0, h))
    return pl.pallas_call(
        body, name=name, grid=(D_MODEL // XA_HEAD, s // tr),
        in_specs=[qs, ks, ks, qs], out_specs=[qs, ks, ks],
        out_shape=[jax.ShapeDtypeStruct((s, D_MODEL), BF16), jax.ShapeDtypeStruct((mt, D_MODEL), F32),
                   jax.ShapeDtypeStruct((mt, D_MODEL), F32)],
        compiler_params=_cparams("parallel", "arbitrary"),
    )(q, k, v, do)


def _ffn_in(h, wg, wu, *, name):
    s, d = h.shape
    f = wg.shape[0]
    tm = _pick(s, 1024, 128)
    tn = _pick(f, 1408, 128)

    def body(h_ref, wg_ref, wu_ref, a_ref, b_ref, o_ref):
        hv = h_ref[...]
        av = _dot_nt(hv, wg_ref[...])
        bv = _dot_nt(hv, wu_ref[...])
        a_ref[...] = av.astype(BF16)
        b_ref[...] = bv.astype(BF16)
        o_ref[...] = (av * _sigmoid(av) * bv).astype(BF16)

    wspec = pl.BlockSpec((tn, d), lambda i, j: (j, 0))
    tile = pl.BlockSpec((tm, tn), lambda i, j: (i, j))
    return pl.pallas_call(
        body, name=name, grid=(s // tm, f // tn),
        in_specs=[pl.BlockSpec((tm, d), lambda i, j: (i, 0)), wspec, wspec],
        out_specs=[tile, tile, tile], out_shape=[jax.ShapeDtypeStruct((s, f), BF16)] * 3,
        compiler_params=_cparams("parallel", "parallel"),
    )(h, wg, wu)


def _ffn_in_bwd(dx, wd, a, b, *, name):
    s, d = dx.shape
    f = wd.shape[0]
    tm = _pick(s, 1024, 128)
    tn = _pick(f, 1408, 128)

    def body(dx_ref, wd_ref, a_ref, b_ref, da_ref, db_ref):
        dhv = _dot_nt(dx_ref[...].astype(BF16), wd_ref[...])
        av = a_ref[...].astype(F32)
        sg = _sigmoid(av)
        silu = av * sg
        da_ref[...] = (dhv * b_ref[...].astype(F32) * (sg + silu * (1.0 - sg))).astype(BF16)
        db_ref[...] = (dhv * silu).astype(BF16)

    tile = pl.BlockSpec((tm, tn), lambda i, j: (i, j))
    return pl.pallas_call(
        body, name=name, grid=(s // tm, f // tn),
        in_specs=[pl.BlockSpec((tm, d), lambda i, j: (i, 0)), pl.BlockSpec((tn, d), lambda i, j: (j, 0)), tile, tile],
        out_specs=[tile, tile], out_shape=[jax.ShapeDtypeStruct((s, f), BF16)] * 2,
        compiler_params=_cparams("parallel", "parallel"),
    )(dx, wd, a, b)


def _adamw(w, g, m, v, *, name):
    r, c = w.shape
    tr = _pick(r, 512, 8)

    def body(w_ref, g_ref, m_ref, v_ref, d_ref, mo_ref, vo_ref):
        gv = g_ref[...]
        mn = ADAM_B1 * m_ref[...] + (1.0 - ADAM_B1) * gv
        vn = ADAM_B2 * v_ref[...] + (1.0 - ADAM_B2) * (gv * gv)
        m_hat = mn / (1.0 - ADAM_B1 ** ADAM_STEP)
        v_hat = vn / (1.0 - ADAM_B2 ** ADAM_STEP)
        d_ref[...] = -ADAM_LR * (m_hat / (jnp.sqrt(v_hat) + ADAM_EPS) + ADAM_WD * w_ref[...])
        mo_ref[...] = mn
        vo_ref[...] = vn

    spec = pl.BlockSpec((tr, c), lambda i: (i, 0))
    shp = jax.ShapeDtypeStruct((r, c), F32)
    return pl.pallas_call(
        body, name=name, grid=(r // tr,), in_specs=[spec] * 4, out_specs=[spec] * 3,
        out_shape=[shp] * 3, compiler_params=_cparams("parallel"),
    )(w, g, m, v)


def _position():
    return lax.axis_index("x"), lax.axis_index("y"), lax.axis_index("c")


def _all_gather(x, *, name):
    t, c_ = x.shape

    def body(x_ref, out_ref, send_sems, recv_sems, local_sem):
        start, forward, finish = _gather_phases(x_ref, out_ref, send_sems, recv_sems, local_sem)
        start()
        forward()
        finish()

    return pl.pallas_call(
        body, name=name,
        out_shape=jax.ShapeDtypeStruct((N_DEV, t, c_), x.dtype),
        in_specs=[pl.BlockSpec(memory_space=pl.ANY)],
        out_specs=pl.BlockSpec(memory_space=pl.ANY),
        scratch_shapes=_GATHER_SEMS,
    )(x)


_GATHER_SEMS = [pltpu.SemaphoreType.DMA((7,)), pltpu.SemaphoreType.DMA((7,)), pltpu.SemaphoreType.DMA]


def _gather_phases(x_ref, out_ref, send_sems, recv_sems, local_sem):
    x_, y_, c = _position()
    me, sibling = (x_, y_, c), (x_, y_, 1 - c)
    chips = [(1 - x_, y_), (x_, 1 - y_), (1 - x_, 1 - y_)]

    def block(px, py, pc):
        return out_ref.at[4 * px + 2 * py + pc]

    def copy(k, blk, to, src=None):
        return pltpu.make_async_remote_copy(
            src_ref=block(*blk) if src is None else src, dst_ref=block(*blk),
            send_sem=send_sems.at[k], recv_sem=recv_sems.at[k], device_id=to, device_id_type=MESH)

    mine = pltpu.make_async_copy(x_ref, block(*me), local_sem)
    first = [copy(0, me, sibling, src=x_ref)]
    first += [copy(1 + j, me, (*chip, c), src=x_ref) for j, chip in enumerate(chips)]
    passed = [copy(4 + j, (*chip, c), sibling) for j, chip in enumerate(chips)]

    def start():
        mine.start()
        for cp in first:
            cp.start()

    def forward():
        for j, chip in enumerate(chips):
            copy(1 + j, (*chip, c), me).wait_recv()
            passed[j].start()

    def finish():
        copy(0, sibling, me).wait_recv()
        for j, chip in enumerate(chips):
            copy(4 + j, (*chip, 1 - c), me).wait_recv()
        for cp in first + passed:
            cp.wait_send()
        mine.wait()

    return start, forward, finish


_SCATTER_SEMS = [pltpu.SemaphoreType.DMA((7,)), pltpu.SemaphoreType.DMA((7,)), pltpu.SemaphoreType.DMA]


def _scatter_phases(g_ref, r_ref, send_sems, recv_sems, local_sem):
    x_, y_, c = _position()
    me = 4 * x_ + 2 * y_ + c
    local = pltpu.make_async_copy(g_ref.at[me], r_ref.at[me], local_sem)
    copies = []
    for k in range(1, N_DEV):
        to = (x_ ^ (k >> 2), y_ ^ ((k >> 1) & 1), c ^ (k & 1))
        copies.append(pltpu.make_async_remote_copy(
            src_ref=g_ref.at[me ^ k], dst_ref=r_ref.at[me], send_sem=send_sems.at[k - 1],
            recv_sem=recv_sems.at[k - 1], device_id=to, device_id_type=MESH))

    def start():
        local.start()
        for cp in copies:
            cp.start()

    def finish():
        for k in range(1, N_DEV):
            pltpu.make_async_remote_copy(
                src_ref=g_ref.at[me], dst_ref=r_ref.at[me ^ k], send_sem=send_sems.at[k - 1],
                recv_sem=recv_sems.at[k - 1], device_id=(x_, y_, c), device_id_type=MESH).wait_recv()
        for cp in copies:
            cp.wait_send()
        local.wait()

    return start, finish


def _sum_devices(r8, *, name):
    _, t, c_ = r8.shape
    tr = _pick(t, 512, 16)

    def body(r_ref, o_ref):
        acc = r_ref[0].astype(F32)
        for d in range(1, N_DEV):
            acc = acc + r_ref[d].astype(F32)
        o_ref[...] = acc

    return pl.pallas_call(
        body, name=name, grid=(t // tr,),
        in_specs=[pl.BlockSpec((N_DEV, tr, c_), lambda i: (0, i, 0))],
        out_specs=pl.BlockSpec((tr, c_), lambda i: (i, 0)),
        out_shape=jax.ShapeDtypeStruct((t, c_), F32),
        compiler_params=_cparams("parallel"),
    )(r8)


def _all_reduce_small(x, *, name):
    r, c_ = x.shape

    def body(x_ref, o_ref, buf, send_sems, recv_sems):
        x_, y_, c = _position()
        me = 4 * x_ + 2 * y_ + c
        buf[me] = x_ref[...]
        copies = []
        for k in range(1, N_DEV):
            to = (x_ ^ (k >> 2), y_ ^ ((k >> 1) & 1), c ^ (k & 1))
            copies.append(pltpu.make_async_remote_copy(
                src_ref=x_ref, dst_ref=buf.at[me], send_sem=send_sems.at[k - 1], recv_sem=recv_sems.at[k - 1],
                device_id=to, device_id_type=MESH))
        for cp in copies:
            cp.start()
        for k in range(1, N_DEV):
            src = me ^ k
            pltpu.make_async_remote_copy(
                src_ref=x_ref, dst_ref=buf.at[src], send_sem=send_sems.at[k - 1], recv_sem=recv_sems.at[k - 1],
                device_id=(x_, y_, c), device_id_type=MESH).wait_recv()
        for cp in copies:
            cp.wait_send()
        acc = buf[0]
        for d in range(1, N_DEV):
            acc = acc + buf[d]
        o_ref[...] = acc

    return pl.pallas_call(
        body, name=name,
        out_shape=jax.ShapeDtypeStruct((r, c_), F32),
        in_specs=[pl.BlockSpec(memory_space=pltpu.VMEM)],
        out_specs=pl.BlockSpec(memory_space=pltpu.VMEM),
        scratch_shapes=[pltpu.VMEM((N_DEV, r, c_), F32), pltpu.SemaphoreType.DMA((7,)), pltpu.SemaphoreType.DMA((7,))],
    )(x)


def _rs_pair_exchange(g8, *, name):
    _, t, c_ = g8.shape

    def body(g_ref, r_ref, send_sems, recv_sems):
        x_, y_, c = _position()
        copies = [pltpu.make_async_remote_copy(
            src_ref=g_ref.at[2 * ch + (1 - c)], dst_ref=r_ref.at[ch],
            send_sem=send_sems.at[ch], recv_sem=recv_sems.at[ch],
            device_id=(x_, y_, 1 - c), device_id_type=MESH) for ch in range(4)]
        for cp in copies:
            cp.start()
        for cp in copies:
            cp.wait()

    return pl.pallas_call(
        body, name=name,
        out_shape=jax.ShapeDtypeStruct((4, t, c_), g8.dtype),
        in_specs=[pl.BlockSpec(memory_space=pl.ANY)],
        out_specs=pl.BlockSpec(memory_space=pl.ANY),
        scratch_shapes=[pltpu.SemaphoreType.DMA((4,)), pltpu.SemaphoreType.DMA((4,))],
    )(g8)


def _pair_add(core, g8, recv, *, name):
    _, t, c_ = g8.shape
    tr = _pick(t, 512, 16)

    def body(core_ref, g_ref, r_ref, o_ref):
        o_ref[...] = (g_ref[...].astype(F32) + r_ref[...].astype(F32)).astype(o_ref.dtype)

    grid_spec = pltpu.PrefetchScalarGridSpec(
        num_scalar_prefetch=1, grid=(4, t // tr),
        in_specs=[pl.BlockSpec((None, tr, c_), lambda ch, i, core_ref: (2 * ch + core_ref[0], i, 0)),
                  pl.BlockSpec((None, tr, c_), lambda ch, i, core_ref: (ch, i, 0))],
        out_specs=pl.BlockSpec((None, tr, c_), lambda ch, i, core_ref: (ch, i, 0)))
    return pl.pallas_call(
        body, name=name, grid_spec=grid_spec,
        out_shape=jax.ShapeDtypeStruct((4, t, c_), g8.dtype),
        compiler_params=_cparams("parallel", "parallel"),
    )(core, g8, recv)


def _rs_chip_exchange(part, *, name):
    _, t, c_ = part.shape

    def body(p_ref, r_ref, send_sems, recv_sems, local_sem):
        x_, y_, c = _position()
        mine = 2 * x_ + y_
        local = pltpu.make_async_copy(p_ref.at[mine], r_ref.at[mine], local_sem)
        local.start()
        chips = [(1 - x_, y_), (x_, 1 - y_), (1 - x_, 1 - y_)]
        copies = [pltpu.make_async_remote_copy(
            src_ref=p_ref.at[2 * px + py], dst_ref=r_ref.at[mine],
            send_sem=send_sems.at[k], recv_sem=recv_sems.at[k],
            device_id=(px, py, c), device_id_type=MESH) for k, (px, py) in enumerate(chips)]
        for cp in copies:
            cp.start()
        for k, (px, py) in enumerate(chips):
            pltpu.make_async_remote_copy(
                src_ref=p_ref.at[mine], dst_ref=r_ref.at[2 * px + py],
                send_sem=send_sems.at[k], recv_sem=recv_sems.at[k],
                device_id=(x_, y_, c), device_id_type=MESH).wait_recv()
        for cp in copies:
            cp.wait_send()
        local.wait()

    return pl.pallas_call(
        body, name=name,
        out_shape=jax.ShapeDtypeStruct((4, t, c_), part.dtype),
        in_specs=[pl.BlockSpec(memory_space=pl.ANY)],
        out_specs=pl.BlockSpec(memory_space=pl.ANY),
        scratch_shapes=[pltpu.SemaphoreType.DMA((3,)), pltpu.SemaphoreType.DMA((3,)), pltpu.SemaphoreType.DMA],
    )(part)


def _sum_chips(r4, *, name):
    _, t, c_ = r4.shape
    tr = _pick(t, 512, 16)

    def body(r_ref, o_ref):
        acc = r_ref[0].astype(F32)
        for ch in range(1, 4):
            acc = acc + r_ref[ch].astype(F32)
        o_ref[...] = acc

    return pl.pallas_call(
        body, name=name, grid=(t // tr,),
        in_specs=[pl.BlockSpec((4, tr, c_), lambda i: (0, i, 0))],
        out_specs=pl.BlockSpec((tr, c_), lambda i: (i, 0)),
        out_shape=jax.ShapeDtypeStruct((t, c_), F32),
        compiler_params=_cparams("parallel"),
    )(r4)


BIG = (
    ("w_in", (IN_COLS // N_DEV, D_MODEL), 0),
    ("w_branch", (3, BRANCH_W, D_MODEL // N_DEV), 2),
    ("w_out", (D_MODEL // N_DEV, D_MODEL), 0),
    ("w_q_xa", (D_MODEL // N_DEV, D_MODEL), 0),
    ("w_k_xa", (D_MODEL // N_DEV, D_MODEL), 0),
    ("w_v_xa", (D_MODEL // N_DEV, D_MODEL), 0),
    ("w_o_xa", (D_MODEL // N_DEV, D_MODEL), 0),
    ("w_gate_ffn", (FFN // N_DEV, D_MODEL), 0),
    ("w_up_ffn", (FFN // N_DEV, D_MODEL), 0),
    ("w_down_ffn", (FFN // N_DEV, D_MODEL), 0),
)
TRANSPOSED = ("w_in", "w_gate_ffn", "w_up_ffn")
_BIG_LAYOUT = {n: (shp, ax) for n, shp, ax in BIG}
PACK_COLS = 1024


def _stored(name, shard):
    return shard.T if name in TRANSPOSED else shard


def _size(shape):
    n = 1
    for d in shape:
        n *= d
    return n


def _pack_shards(items, shards):
    return jnp.concatenate([shards[it].reshape(-1, PACK_COLS) for it in items], axis=0)


def _unpack_gathered(items, g):
    out = {}
    r0 = 0
    for it in items:
        shp, ax = _BIG_LAYOUT[it[0]]
        rows = _size(shp) // PACK_COLS
        blk = g[:, r0:r0 + rows].reshape((N_DEV,) + shp)
        r0 += rows
        blk = jnp.moveaxis(blk, 0, ax)
        full = list(shp)
        full[ax] = shp[ax] * N_DEV
        out[it] = blk.reshape(full)
    return out


def _pack_full(items, full):
    parts = []
    for it in items:
        shp, ax = _BIG_LAYOUT[it[0]]
        t = full[it].reshape(shp[:ax] + (N_DEV, shp[ax]) + shp[ax + 1:])
        t = jnp.moveaxis(t, ax, 0)
        parts.append(t.reshape(N_DEV, -1, PACK_COLS))
    rows = sum(part.shape[1] for part in parts)
    if rows % 128:
        parts.append(jnp.zeros((N_DEV, 128 - rows % 128, PACK_COLS), parts[0].dtype))
    return jnp.concatenate(parts, axis=1)


def _unpack_shard(items, flat):
    out = {}
    r0 = 0
    for it in items:
        shp, _ = _BIG_LAYOUT[it[0]]
        rows = _size(shp) // PACK_COLS
        out[it] = flat[r0:r0 + rows].reshape(shp)
        r0 += rows
    return out


SMALL = (
    ("norm_mix_g", (DEPTH, D_MODEL)),
    ("sgu_ln_g", (DEPTH, BRANCH_W)),
    ("sgu_ln_b", (DEPTH, BRANCH_W)),
    ("w_spatial", (DEPTH, SGU_GROUPS, SGU_LEN, SGU_LEN)),
    ("b_spatial", (DEPTH, SGU_GROUPS, SGU_LEN)),
    ("conv_w", (DEPTH, 3, BRANCH_W)),
    ("norm_xa_g", (DEPTH, D_MODEL)),
    ("mem_norm_g", (DEPTH, D_MODEL)),
    ("norm_ffn_g", (DEPTH, D_MODEL)),
    ("final_g", (D_MODEL,)),
)


def _pack_small(grads):
    flat = jnp.concatenate([grads[n].reshape(-1) for n, _ in SMALL])
    rows = -(-flat.shape[0] // PACK_COLS)
    rows = -(-rows // 8) * 8
    flat = jnp.pad(flat, (0, rows * PACK_COLS - flat.shape[0]))
    return flat.reshape(rows, PACK_COLS)


def _unpack_small(buf):
    flat = buf.reshape(-1)
    out = {}
    o = 0
    for n, shp in SMALL:
        out[n] = flat[o:o + _size(shp)].reshape(shp)
        o += _size(shp)
    return out


def _layer_fwd(l, x, mem, wt, sm, gather=None):
    t = f"l{l}_"
    sv = {"x0": x}
    h = _rms_fwd(x, sm["norm_mix_g"][l][None], name=t + "rms_mix")
    p = _mm(h, wt["w_in", l], tb=True, name=t + "in_proj", tm=2048)
    if gather is None:
        ya, *tables = _sb_fwd(p, name=t + "sb_fwd")
    else:
        ya, *tables, gathered = _sb_fwd(p, name=t + "sb_fwd", gather=gather[1])
        wt.update(_unpack_gathered(gather[0], gathered))
    w_sp = sm["w_spatial"][l]
    b_col = sm["b_spatial"][l][:, :, None]
    ln_g, ln_b = sm["sgu_ln_g"][l][None], sm["sgu_ln_b"][l][None]
    yb = _sgu_fwd(p, ln_g, ln_b, w_sp, b_col, name=t + "sgu_fwd")
    yc = _conv_fwd(p, sm["conv_w"][l], name=t + "conv_fwd")
    merged = _merge_fwd(ya, yb, yc, wt["w_branch", l], p, name=t + "merge_fwd")
    x1 = _mm(merged, wt["w_out", l], add=x, name=t + "out_proj")
    sv.update(h=h, p=p, ya=ya, yb=yb, yc=yc, merged=merged, x1=x1, tables=tables)

    h2 = _rms_fwd(x1, sm["norm_xa_g"][l][None], name=t + "rms_xa")
    mn = _rms_fwd(mem, sm["mem_norm_g"][l][None], name=t + "rms_mem")
    q = _mm(h2, wt["w_q_xa", l], out_dtype=BF16, name=t + "xa_q", tm=2048)
    k = _mm(mn, wt["w_k_xa", l], out_dtype=BF16, name=t + "xa_k")
    v = _mm(mn, wt["w_v_xa", l], out_dtype=BF16, name=t + "xa_v")
    o = _xa_fwd(q, k, v, name=t + "xa_fwd")
    x2 = _mm(o, wt["w_o_xa", l], add=x1, name=t + "xa_o")
    sv.update(h2=h2, mn=mn, q=q, k=k, v=v, o=o, x2=x2)

    h3 = _rms_fwd(x2, sm["norm_ffn_g"][l][None], name=t + "rms_ffn")
    a, b, hd = _ffn_in(h3, wt["w_gate_ffn", l], wt["w_up_ffn", l], name=t + "ffn_in")
    x3 = _mm(hd, wt["w_down_ffn", l], add=x2, name=t + "ffn_down", tk=FFN)
    sv.update(h3=h3, a=a, b=b, hd=hd)
    return x3, sv


def _layer_bwd(l, dx3, mem, wt, sm, sv, scatter=None):
    t = f"l{l}_b_"
    gb, gs = {}, {}
    gb["w_down_ffn"] = _mm(sv["hd"], dx3, ta=True, out_dtype=BF16, name=t + "ffn_down_dw", tm=1408)
    da, db = _ffn_in_bwd(dx3, wt["w_down_ffn", l], sv["a"], sv["b"], name=t + "ffn_in_bwd")
    gb["w_gate_ffn"] = _mm(da, sv["h3"], ta=True, out_dtype=BF16, name=t + "ffn_gate_dw", tm=1408)
    gb["w_up_ffn"] = _mm(db, sv["h3"], ta=True, out_dtype=BF16, name=t + "ffn_up_dw", tm=1408)
    dh3 = _mm(da, wt["w_gate_ffn", l], name=t + "ffn_gate_dx", tk=1408)
    dx2, dg = _mm(db, wt["w_up_ffn", l], add=dh3, rms=(sv["x2"], sm["norm_ffn_g"][l][None], dx3),
                  name=t + "ffn_up_dx", tm=512, tk=1408)
    gs["norm_ffn_g"] = dg[0]
    do = _mm(dx2, wt["w_o_xa", l], tb=True, out_dtype=BF16, name=t + "xa_o_dx")
    gb["w_o_xa"] = _mm(sv["o"], dx2, ta=True, out_dtype=BF16, name=t + "xa_o_dw")
    dq, dk, dv = _xa_bwd(sv["q"], sv["k"], sv["v"], do, name=t + "xa_bwd")
    dx1, dg = _mm(dq, wt["w_q_xa", l], tb=True, rms=(sv["x1"], sm["norm_xa_g"][l][None], dx2),
                  name=t + "xa_q_dx", tm=1024)
    gs["norm_xa_g"] = dg[0]
    gb["w_q_xa"] = _mm(sv["h2"], dq, ta=True, out_dtype=BF16, name=t + "xa_q_dw")
    gb["w_k_xa"] = _mm(sv["mn"], dk, ta=True, out_dtype=BF16, name=t + "xa_k_dw")
    gb["w_v_xa"] = _mm(sv["mn"], dv, ta=True, out_dtype=BF16, name=t + "xa_v_dw")
    dmn = _mm(dk, wt["w_k_xa", l], tb=True, name=t + "xa_k_dx")
    dmn = _mm(dv, wt["w_v_xa", l], tb=True, add=dmn, name=t + "xa_v_dx")
    _, dg = _rms_bwd(mem, sm["mem_norm_g"][l][None], dmn, jnp.zeros_like(mem), name=t + "rms_mem")
    gs["mem_norm_g"] = dg[0]
    dm = _mm(dx1, wt["w_out", l], tb=True, name=t + "out_proj_dx")
    gb["w_out"] = _mm(sv["merged"], dx1, ta=True, out_dtype=BF16, name=t + "out_proj_dw")
    p = sv["p"]
    dya, dyb, dyc, dgates, *dbrd = _merge_bwd(dm, sv["ya"], sv["yb"], sv["yc"], wt["w_branch", l], p,
                                              name=t + "merge_bwd")
    gb["w_branch"] = jnp.stack([
        _mm(sv[y], dbrd[n], ta=True, out_dtype=BF16, name=t + f"branch{n}_dw")
        for n, y in enumerate(("ya", "yb", "yc"))])
    dcb, dcc, dcx, dcw = _conv_bwd(p, dyc, sm["conv_w"][l], name=t + "conv_bwd")
    gs["conv_w"] = dcw
    w_sp = sm["w_spatial"][l]
    dz, dlg, dlb, dwsp, dbsp = _sgu_bwd(p, dyb, sm["sgu_ln_g"][l][None], sm["sgu_ln_b"][l][None], w_sp,
                                        jnp.swapaxes(w_sp, 1, 2), sm["b_spatial"][l][:, :, None],
                                        name=t + "sgu_bwd")
    gs.update(sgu_ln_g=dlg[0], sgu_ln_b=dlb[0], w_spatial=dwsp, b_spatial=dbsp[:, :, 0])
    received = None
    if scatter is None:
        dq_a, dk_a, dv_a = _sb_bwd(p, dya, sv["tables"], name=t + "sb_bwd")
    else:
        items, earlier = scatter
        ready = {**earlier, **{(n, l): g for n, g in gb.items()}}
        dq_a, dk_a, dv_a, received = _sb_bwd(p, dya, sv["tables"], name=t + "sb_bwd",
                                             scatter=_pack_full(items, ready))
    dp = jnp.concatenate([dq_a, dk_a, dv_a, dz, dcb, dcc, dcx, dgates], axis=1)
    gb["w_in"] = _mm(dp, sv["h"], ta=True, out_dtype=BF16, name=t + "in_proj_dw")
    dx, dg = _mm(dp, wt["w_in", l], rms=(sv["x0"], sm["norm_mix_g"][l][None], dx1),
                 name=t + "in_proj_dx", tm=1024, tk=1792)
    gs["norm_mix_g"] = dg[0]
    return dx, gb, gs, received


_WEIGHTS = ("norm_mix_g", "w_in", "sgu_ln_g", "sgu_ln_b", "w_spatial", "b_spatial", "conv_w", "w_branch", "w_out",
            "norm_xa_g", "mem_norm_g", "w_q_xa", "w_k_xa", "w_v_xa", "w_o_xa", "norm_ffn_g", "w_gate_ffn",
            "w_up_ffn", "w_down_ffn", "final_g")


def kernel(x, mem, norm_mix_g, w_in, sgu_ln_g, sgu_ln_b, w_spatial, b_spatial, conv_w, w_branch, w_out, norm_xa_g, mem_norm_g, w_q_xa, w_k_xa, w_v_xa, w_o_xa, norm_ffn_g, w_gate_ffn, w_up_ffn, w_down_ffn, final_g, loss_target, m_norm_mix_g, m_w_in, m_sgu_ln_g, m_sgu_ln_b, m_w_spatial, m_b_spatial, m_conv_w, m_w_branch, m_w_out, m_norm_xa_g, m_mem_norm_g, m_w_q_xa, m_w_k_xa, m_w_v_xa, m_w_o_xa, m_norm_ffn_g, m_w_gate_ffn, m_w_up_ffn, m_w_down_ffn, m_final_g, v_norm_mix_g, v_w_in, v_sgu_ln_g, v_sgu_ln_b, v_w_spatial, v_b_spatial, v_conv_w, v_w_branch, v_w_out, v_norm_xa_g, v_mem_norm_g, v_w_q_xa, v_w_k_xa, v_w_v_xa, v_w_o_xa, v_norm_ffn_g, v_w_gate_ffn, v_w_up_ffn, v_w_down_ffn, v_final_g):
    w = dict(norm_mix_g=norm_mix_g, w_in=w_in, sgu_ln_g=sgu_ln_g, sgu_ln_b=sgu_ln_b, w_spatial=w_spatial,
             b_spatial=b_spatial, conv_w=conv_w, w_branch=w_branch, w_out=w_out, norm_xa_g=norm_xa_g,
             mem_norm_g=mem_norm_g, w_q_xa=w_q_xa, w_k_xa=w_k_xa, w_v_xa=w_v_xa, w_o_xa=w_o_xa,
             norm_ffn_g=norm_ffn_g, w_gate_ffn=w_gate_ffn, w_up_ffn=w_up_ffn, w_down_ffn=w_down_ffn, final_g=final_g)
    m = dict(norm_mix_g=m_norm_mix_g, w_in=m_w_in, sgu_ln_g=m_sgu_ln_g, sgu_ln_b=m_sgu_ln_b, w_spatial=m_w_spatial,
             b_spatial=m_b_spatial, conv_w=m_conv_w, w_branch=m_w_branch, w_out=m_w_out, norm_xa_g=m_norm_xa_g,
             mem_norm_g=m_mem_norm_g, w_q_xa=m_w_q_xa, w_k_xa=m_w_k_xa, w_v_xa=m_w_v_xa, w_o_xa=m_w_o_xa,
             norm_ffn_g=m_norm_ffn_g, w_gate_ffn=m_w_gate_ffn, w_up_ffn=m_w_up_ffn, w_down_ffn=m_w_down_ffn,
             final_g=m_final_g)
    v = dict(norm_mix_g=v_norm_mix_g, w_in=v_w_in, sgu_ln_g=v_sgu_ln_g, sgu_ln_b=v_sgu_ln_b, w_spatial=v_w_spatial,
             b_spatial=v_b_spatial, conv_w=v_conv_w, w_branch=v_w_branch, w_out=v_w_out, norm_xa_g=v_norm_xa_g,
             mem_norm_g=v_mem_norm_g, w_q_xa=v_w_q_xa, w_k_xa=v_w_k_xa, w_v_xa=v_w_v_xa, w_o_xa=v_w_o_xa,
             norm_ffn_g=v_norm_ffn_g, w_gate_ffn=v_w_gate_ffn, w_up_ffn=v_w_up_ffn, w_down_ffn=v_w_down_ffn,
             final_g=v_final_g)

    names = [n for n, _, _ in BIG]
    shards = {(n, l): _stored(n, w[n][l].astype(BF16)) for n in names for l in range(DEPTH)}
    first_items = [("w_in", 0)]
    mid_items = [(n, 0) for n in names if n != "w_in"] + [("w_in", 1)]
    last_items = [(n, 1) for n in names if n != "w_in"]
    wt = _unpack_gathered(first_items, _all_gather(_pack_shards(first_items, shards), name="gather_w_in0"))
    cw_pad = jnp.zeros((8, 128), F32).at[:DEPTH * 3, :BRANCH_W // N_DEV].set(conv_w.reshape(DEPTH * 3, -1))
    cw_all = _all_gather(cw_pad, name="gather_conv_w")[:, :DEPTH * 3, :BRANCH_W // N_DEV]
    conv_full = jnp.moveaxis(cw_all.reshape(N_DEV, DEPTH, 3, BRANCH_W // N_DEV), 0, 2).reshape(DEPTH, 3, BRANCH_W)
    sm = {n: w[n] for n, _ in SMALL}
    sm["conv_w"] = conv_full

    xs, ms = x[0], mem[0]
    x1, saved0 = _layer_fwd(0, xs, ms, wt, sm, gather=(mid_items, _pack_shards(mid_items, shards)))
    x2, saved1 = _layer_fwd(1, x1, ms, wt, sm, gather=(last_items, _pack_shards(last_items, shards)))
    dcur, loss, dfinal = _final_loss(x2, sm["final_g"][None], loss_target[0], name="final_loss")
    loss = lax.psum(loss[0, 0], AXES)
    items_a = [(n, 1) for n in names if n != "w_in"]
    items_b = [("w_in", 1)] + [(n, 0) for n in names if n != "w_in"]
    items_c = [("w_in", 0)]
    dcur, gb1, gs1, recv_a = _layer_bwd(1, dcur, ms, wt, sm, saved1, scatter=(items_a, {}))
    dx, gb0, gs0, recv_b = _layer_bwd(0, dcur, ms, wt, sm, saved0, scatter=(items_b, {("w_in", 1): gb1["w_in"]}))

    shard_grads = _unpack_shard(items_a, _sum_devices(recv_a, name="rs_sum_a"))
    shard_grads.update(_unpack_shard(items_b, _sum_devices(recv_b, name="rs_sum_b")))
    g8 = _pack_full(items_c, {("w_in", 0): gb0["w_in"]})
    core = lax.axis_index("c").astype(jnp.int32).reshape(1)
    from_sibling = _rs_pair_exchange(g8, name="rs_pair_exchange")
    part = _pair_add(core, g8, from_sibling, name="rs_pair_add")
    by_chip = _rs_chip_exchange(part, name="rs_chip_exchange")
    shard_grads.update(_unpack_shard(items_c, _sum_chips(by_chip, name="rs_sum_chips")))
    grads = {n: jnp.stack([_stored(n, shard_grads[n, l]) for l in range(DEPTH)]) for n in names}
    small = {n: jnp.stack([gs0[n], gs1[n]]) for n, _ in SMALL if n != "final_g"}
    small["final_g"] = dfinal[0]
    small_sum = _unpack_small(_all_reduce_small(_pack_small(small), name="all_reduce_small"))
    width = BRANCH_W // N_DEV
    dev = 4 * lax.axis_index("x") + 2 * lax.axis_index("y") + lax.axis_index("c")
    for n, _ in SMALL:
        grads[n] = small_sum[n]
    grads["conv_w"] = lax.dynamic_slice_in_dim(small_sum["conv_w"], dev * width, width, axis=2)

    delta, new_m, new_v = {}, {}, {}
    for n in _WEIGHTS:
        shp = w[n].shape
        two_d = (-1, shp[-1])
        d_, m_, v_ = _adamw(w[n].reshape(two_d), grads[n].reshape(two_d), m[n].reshape(two_d), v[n].reshape(two_d),
                            name="adamw_" + n)
        delta[n], new_m[n], new_v[n] = d_.reshape(shp), m_.reshape(shp), v_.reshape(shp)

    return (loss, dx[None], *[grads[n] for n in _WEIGHTS], *[delta[n] for n in _WEIGHTS],
            *[new_m[n] for n in _WEIGHTS], *[new_v[n] for n in _WEIGHTS])
```

```python
import functools

import jax
import jax.numpy as jnp
from jax import lax
from jax.experimental import pallas as pl
from jax.experimental.pallas import tpu as pltpu

F32 = jnp.float32
BF16 = jnp.bfloat16
MESH = pl.DeviceIdType.MESH

D_MODEL = 1024
BRANCH_W = 512
IN_COLS = 7168
FFN = 2816
N_DEV = 8
DEPTH = 2
SB_BLOCK = 128
SB_SPAN = 1024
SB_Q_FWD = 512
SB_Q_BWD = 512
SB_SCALE = 0.125
XA_HEAD = 256
XA_SCALE = 0.0625
SGU_LEN = 128
SGU_GROUPS = 4
RMS_EPS = 1e-6
LN_EPS = 1e-5
HALO = 8

ADAM_LR = 0.001
ADAM_B1 = 0.9
ADAM_B2 = 0.999
ADAM_EPS = 1e-08
ADAM_WD = 0.01
ADAM_STEP = 10

VMEM_LIMIT_BYTES = 52 * 1024 * 1024

AXES = ("x", "y", "c")


def _cparams(*sem):
    return pltpu.CompilerParams(dimension_semantics=sem, vmem_limit_bytes=VMEM_LIMIT_BYTES)


def _pick(n, target, align):
    t = (min(target, n) // align) * align
    while t >= align:
        if n % t == 0:
            return t
        t -= align
    return n


def _dot(a, b):
    return jnp.dot(a, b, preferred_element_type=F32)


def _dot_nt(a, b):
    return lax.dot_general(a, b, (((1,), (1,)), ((), ())), preferred_element_type=F32)


def _dot_tn(a, b):
    return lax.dot_general(a, b, (((0,), (0,)), ((), ())), preferred_element_type=F32)


def _sigmoid(x):
    return 1.0 / (1.0 + jnp.exp(-x))


def _mm(a, b, *, name, ta=False, tb=False, out_dtype=F32, add=None, rms=None, tm=1024, tn=1024, tk=2048):
    m, k = (a.shape[1], a.shape[0]) if ta else a.shape
    n = b.shape[0] if tb else b.shape[1]
    assert k == (b.shape[1] if tb else b.shape[0])
    tm = _pick(m, tm, 128)
    tn = n if rms is not None else _pick(n, tn, 128)
    tk = _pick(k, tk, 128)
    nk = k // tk
    ca = 0 if ta else 1
    cb = 1 if tb else 0
    n_add = 0 if add is None else 1
    n_rms = 0 if rms is None else 3

    def body(*refs):
        refs = list(refs)
        a_ref, b_ref = refs[:2]
        extra = refs[2:2 + n_add + n_rms]
        outs = refs[2 + n_add + n_rms:]
        o_ref = outs[0]
        kk = pl.program_id(2)
        first_row_tile = pl.program_id(0) == 0

        def product():
            return lax.dot_general(a_ref[...].astype(BF16), b_ref[...].astype(BF16),
                                   (((ca,), (cb,)), ((), ())), preferred_element_type=F32)

        def finish(r):
            if add is not None:
                r = r + extra[0][...]
            if rms is None:
                o_ref[...] = r.astype(out_dtype)
                return
            x_ref, g_ref, dres_ref = extra[n_add:]
            dg_ref = outs[1]

            @pl.when(first_row_tile)
            def _():
                dg_ref[...] = jnp.zeros_like(dg_ref)

            xv = x_ref[...]
            rs = lax.rsqrt(jnp.mean(xv * xv, axis=-1, keepdims=True) + RMS_EPS)
            xh = xv * rs
            dg_ref[...] += jnp.sum(r * xh, axis=0, keepdims=True)
            dxh = r * g_ref[...]
            o_ref[...] = dres_ref[...] + rs * (dxh - xh * jnp.mean(dxh * xh, axis=-1, keepdims=True))

        if nk == 1:
            finish(product())
        else:
            acc_ref = outs[-1]

            @pl.when(kk == 0)
            def _():
                acc_ref[...] = jnp.zeros_like(acc_ref)

            acc_ref[...] += product()

            @pl.when(kk == nk - 1)
            def _():
                finish(acc_ref[...])

    a_spec = pl.BlockSpec((tk, tm), lambda i, j, kk: (kk, i)) if ta else pl.BlockSpec((tm, tk), lambda i, j, kk: (i, kk))
    b_spec = pl.BlockSpec((tn, tk), lambda i, j, kk: (j, kk)) if tb else pl.BlockSpec((tk, tn), lambda i, j, kk: (kk, j))
    tile = pl.BlockSpec((tm, tn), lambda i, j, kk: (i, j))
    in_specs = [a_spec, b_spec]
    operands = [a, b]
    out_specs = [tile]
    out_shape = [jax.ShapeDtypeStruct((m, n), out_dtype)]
    if add is not None:
        in_specs.append(tile)
        operands.append(add)
    if rms is not None:
        vec = pl.BlockSpec((1, n), lambda i, j, kk: (0, 0))
        in_specs += [tile, vec, tile]
        operands += list(rms)
        out_specs.append(vec)
        out_shape = [jax.ShapeDtypeStruct((m, n), F32), jax.ShapeDtypeStruct((1, n), F32)]
    out = pl.pallas_call(
        body, name=name,
        grid=(m // tm, n // tn, nk),
        in_specs=in_specs, out_specs=out_specs, out_shape=out_shape,
        scratch_shapes=[pltpu.VMEM((tm, tn), F32)] if nk > 1 else [],
        compiler_params=_cparams("arbitrary" if rms is not None else "parallel", "parallel", "arbitrary"),
    )(*operands)
    return out[0] if rms is None else out


def _rms_fwd(x, g, *, name):
    r, d = x.shape
    tr = _pick(r, 512, 16)

    def body(x_ref, g_ref, o_ref):
        xv = x_ref[...]
        rs = lax.rsqrt(jnp.mean(xv * xv, axis=-1, keepdims=True) + RMS_EPS)
        o_ref[...] = (xv * rs * g_ref[...]).astype(BF16)

    return pl.pallas_call(
        body, name=name, grid=(r // tr,),
        in_specs=[pl.BlockSpec((tr, d), lambda i: (i, 0)), pl.BlockSpec((1, d), lambda i: (0, 0))],
        out_specs=pl.BlockSpec((tr, d), lambda i: (i, 0)),
        out_shape=jax.ShapeDtypeStruct((r, d), BF16),
        compiler_params=_cparams("parallel"),
    )(x, g)


def _rms_bwd(x, g, dh, dres, *, name):
    r, d = x.shape
    tr = _pick(r, 256, 8)

    def body(x_ref, g_ref, dh_ref, dres_ref, dx_ref, dg_ref):
        @pl.when(pl.program_id(0) == 0)
        def _():
            dg_ref[...] = jnp.zeros_like(dg_ref)

        xv = x_ref[...]
        dhv = dh_ref[...].astype(F32)
        rs = lax.rsqrt(jnp.mean(xv * xv, axis=-1, keepdims=True) + RMS_EPS)
        xh = xv * rs
        dg_ref[...] += jnp.sum(dhv * xh, axis=0, keepdims=True)
        dxh = dhv * g_ref[...]
        dx_ref[...] = dres_ref[...] + rs * (dxh - xh * jnp.mean(dxh * xh, axis=-1, keepdims=True))

    return pl.pallas_call(
        body, name=name, grid=(r // tr,),
        in_specs=[pl.BlockSpec((tr, d), lambda i: (i, 0)), pl.BlockSpec((1, d), lambda i: (0, 0)),
                  pl.BlockSpec((tr, d), lambda i: (i, 0)), pl.BlockSpec((tr, d), lambda i: (i, 0))],
        out_specs=[pl.BlockSpec((tr, d), lambda i: (i, 0)), pl.BlockSpec((1, d), lambda i: (0, 0))],
        out_shape=[jax.ShapeDtypeStruct((r, d), F32), jax.ShapeDtypeStruct((1, d), F32)],
        compiler_params=_cparams("arbitrary"),
    )(x, g, dh, dres)


def _final_loss(x, g, target, *, name):
    r, d = x.shape
    tr = _pick(r, 256, 8)

    def body(x_ref, g_ref, t_ref, dx_ref, loss_ref, dg_ref):
        @pl.when(pl.program_id(0) == 0)
        def _():
            dg_ref[...] = jnp.zeros_like(dg_ref)
            loss_ref[...] = jnp.zeros_like(loss_ref)

        xv = x_ref[...]
        gv = g_ref[...]
        rs = lax.rsqrt(jnp.mean(xv * xv, axis=-1, keepdims=True) + RMS_EPS)
        xh = xv * rs
        err = xh * gv - t_ref[...]
        row_loss = jnp.mean(err * err, axis=-1, keepdims=True)
        loss_ref[...] += 0.5 * jnp.sum(row_loss, axis=0, keepdims=True)
        dy = err * (1.0 / d)
        dg_ref[...] += jnp.sum(dy * xh, axis=0, keepdims=True)
        dxh = dy * gv
        dx_ref[...] = rs * (dxh - xh * jnp.mean(dxh * xh, axis=-1, keepdims=True))

    return pl.pallas_call(
        body, name=name, grid=(r // tr,),
        in_specs=[pl.BlockSpec((tr, d), lambda i: (i, 0)), pl.BlockSpec((1, d), lambda i: (0, 0)),
                  pl.BlockSpec((tr, d), lambda i: (i, 0))],
        out_specs=[pl.BlockSpec((tr, d), lambda i: (i, 0)), pl.BlockSpec((1, 128), lambda i: (0, 0)),
                   pl.BlockSpec((1, d), lambda i: (0, 0))],
        out_shape=[jax.ShapeDtypeStruct((r, d), F32), jax.ShapeDtypeStruct((1, 128), F32),
                   jax.ShapeDtypeStruct((1, d), F32)],
        compiler_params=_cparams("arbitrary"),
    )(x, g, target)


def _cumsum_operand(strict_after, totals=True):
    width = (2 if totals else 1) * SB_BLOCK
    r = lax.broadcasted_iota(jnp.int32, (SB_BLOCK, width), 0)
    c = lax.broadcasted_iota(jnp.int32, (SB_BLOCK, width), 1)
    tri = (r > c) if strict_after else (r < c)
    return jnp.where((c >= SB_BLOCK) | tri, 1.0, 0.0).astype(BF16)


def _sb_scores(qh, kw, run, valid, after_ones):
    nb = kw.shape[0] // SB_BLOCK
    z = _dot_nt(qh, kw)
    lsp = jnp.minimum(z, 0.0) - jnp.log(1.0 + jnp.exp(-jnp.abs(z)))
    l1m = lsp - z
    if valid is not None:
        l1m = jnp.where(valid, l1m, 0.0)
    l1b = l1m.astype(BF16)
    later = [None] * nb
    seen = [None] * nb
    for b in reversed(range(nb)):
        cols = slice(b * SB_BLOCK, (b + 1) * SB_BLOCK)
        ct = _dot(l1b[:, cols], after_ones)
        seen[b] = run
        later[b] = run + ct[:, :SB_BLOCK]
        run = run + ct[:, SB_BLOCK:]
    a = jnp.exp(lsp + jnp.concatenate(later, axis=1))
    if valid is not None:
        a = jnp.where(valid, a, 0.0)
    return a, run, seen


def _sb_setup(q_ref, span):
    qi = pl.program_id(1)
    rows = q_ref.shape[0]
    sd = (qi * rows + rows - 1) // span
    lane = lax.broadcasted_iota(jnp.int32, (rows, SB_BLOCK), 1)
    col = lax.broadcasted_iota(jnp.int32, (rows, span), 1)
    row = lax.broadcasted_iota(jnp.int32, (rows, span), 0)
    valid = col < (qi * rows - sd * span) + row
    q = q_ref[...] * SB_SCALE
    qhs = (jnp.where(lane < 64, q, 0.0).astype(BF16), jnp.where(lane >= 64, q, 0.0).astype(BF16))
    return lane, sd, valid, qhs


def _sb_fwd(p, *, name, gather=None):
    s = p.shape[0]
    qrows = min(SB_Q_FWD, s)
    nq = s // qrows
    kcol = BRANCH_W // SB_BLOCK
    span = min(SB_SPAN, s)
    per = span // SB_BLOCK
    assert s // SB_BLOCK <= SB_BLOCK

    def body(*refs):
        if gather is None:
            q_ref, k_ref, v_ref, o_ref, r0_ref, r1_ref = refs
        else:
            q_ref, k_ref, v_ref, x_ref, o_ref, r0_ref, r1_ref, g_ref, send_sems, recv_sems, local_sem = refs
            start, forward, finish = _gather_phases(x_ref, g_ref, send_sems, recv_sems, local_sem)
            step = pl.program_id(0) * nq + pl.program_id(1)
            pl.when(step == 0)(start)
        lane, sd, valid, qhs = _sb_setup(q_ref, span)
        after_ones = _cumsum_operand(True)
        zero = jnp.zeros((qrows, SB_BLOCK), F32)
        lane_row = lax.broadcasted_iota(jnp.int32, (1, SB_BLOCK), 1)

        def span_step(sb, carry, mask):
            rows = pl.ds(pl.multiple_of(sb * span, span), span)
            kw = k_ref[rows, :].astype(BF16)
            vw = v_ref[rows, :].astype(BF16)
            out = []
            for h in range(2):
                run, acc, table = carry[h]
                a, run, seen = _sb_scores(qhs[h], kw, run, mask, after_ones)
                for b in range(per):
                    table = jnp.where(lane_row == sb * per + b, seen[b], table)
                out.append((run, acc + _dot(a.astype(BF16), vw), table))
            return tuple(out)

        carry = span_step(sd, ((zero, zero, zero), (zero, zero, zero)), valid)
        carry = lax.fori_loop(0, sd, lambda t, c: span_step(sd - 1 - t, c, None), carry)
        o_ref[...] = jnp.where(lane < 64, carry[0][1], carry[1][1]).astype(BF16)
        r0_ref[...] = carry[0][2]
        r1_ref[...] = carry[1][2]
        if gather is not None:
            pl.when(step == (kcol - 1) * nq + (3 * nq) // 4)(forward)
            pl.when(step == kcol * nq - 1)(finish)

    in_specs = [pl.BlockSpec((qrows, SB_BLOCK), lambda hp, qi: (qi, hp)),
                pl.BlockSpec((s, SB_BLOCK), lambda hp, qi: (0, kcol + hp)),
                pl.BlockSpec((s, SB_BLOCK), lambda hp, qi: (0, 2 * kcol + hp))]
    table = pl.BlockSpec((None, qrows, SB_BLOCK), lambda hp, qi: (hp, qi, 0))
    out_specs = [pl.BlockSpec((qrows, SB_BLOCK), lambda hp, qi: (qi, hp)), table, table]
    out_shape = [jax.ShapeDtypeStruct((s, BRANCH_W), BF16)] + [jax.ShapeDtypeStruct((kcol, s, SB_BLOCK), F32)] * 2
    operands = [p, p, p]
    scratch = []
    if gather is not None:
        in_specs.append(pl.BlockSpec(memory_space=pl.ANY))
        out_specs.append(pl.BlockSpec(memory_space=pl.ANY))
        out_shape.append(jax.ShapeDtypeStruct((N_DEV,) + gather.shape, gather.dtype))
        operands.append(gather)
        scratch = _GATHER_SEMS
    out = pl.pallas_call(
        body, name=name, grid=(kcol, nq), in_specs=in_specs, out_specs=out_specs, out_shape=out_shape,
        scratch_shapes=scratch, compiler_params=_cparams("arbitrary", "arbitrary"),
    )(*operands)
    return out


def _sb_bwd(p, dya, tables, *, name, scatter=None):
    s = p.shape[0]
    qrows = min(SB_Q_BWD, s)
    nq = s // qrows
    kcol = BRANCH_W // SB_BLOCK
    span = min(SB_SPAN, s)
    per = span // SB_BLOCK

    def body(*refs):
        if scatter is None:
            q_ref, k_ref, v_ref, do_ref, t0_ref, t1_ref, dq_ref, dk_ref, dv_ref, dk_acc, dv_acc = refs
        else:
            (q_ref, k_ref, v_ref, do_ref, t0_ref, t1_ref, g_ref, dq_ref, dk_ref, dv_ref, r_ref,
             dk_acc, dv_acc, send_sems, recv_sems, local_sem) = refs
            start, finish = _scatter_phases(g_ref, r_ref, send_sems, recv_sems, local_sem)
            step = pl.program_id(0) * nq + pl.program_id(1)
            pl.when(step == 0)(start)
        qi = pl.program_id(1)

        @pl.when(qi == 0)
        def _():
            dk_acc[...] = jnp.zeros_like(dk_acc)
            dv_acc[...] = jnp.zeros_like(dv_acc)

        lane, sd, valid, qhs = _sb_setup(q_ref, span)
        after = _cumsum_operand(True, totals=False)
        before_ones = _cumsum_operand(False)
        do = do_ref[...]
        dohs = (jnp.where(lane < 64, do, 0.0).astype(BF16), jnp.where(lane >= 64, do, 0.0).astype(BF16))
        tabs = (t0_ref[...], t1_ref[...])
        lane_row = lax.broadcasted_iota(jnp.int32, (1, SB_BLOCK), 1)
        zero = jnp.zeros((qrows, SB_BLOCK), F32)

        def span_step(sb, carry, mask):
            rows = pl.ds(pl.multiple_of(sb * span, span), span)
            kw = k_ref[rows, :].astype(BF16)
            vw = v_ref[rows, :].astype(BF16)
            out = []
            dk_span = jnp.zeros((span, SB_BLOCK), F32)
            dv_span = jnp.zeros((span, SB_BLOCK), F32)
            for h in range(2):
                pg, dq = carry[h]
                z = _dot_nt(qhs[h], kw)
                lsp = jnp.minimum(z, 0.0) - jnp.log(1.0 + jnp.exp(-jnp.abs(z)))
                l1m = lsp - z
                if mask is not None:
                    l1m = jnp.where(mask, l1m, 0.0)
                l1b = l1m.astype(BF16)
                later = [None] * per
                for b in range(per):
                    cols = slice(b * SB_BLOCK, (b + 1) * SB_BLOCK)
                    seen = jnp.sum(jnp.where(lane_row == sb * per + b, tabs[h], 0.0), axis=-1, keepdims=True)
                    later[b] = seen + _dot(l1b[:, cols], after)
                a = jnp.exp(lsp + jnp.concatenate(later, axis=1))
                beta = jnp.exp(lsp)
                if mask is not None:
                    a = jnp.where(mask, a, 0.0)
                    beta = jnp.where(mask, beta, 0.0)
                g = a * _dot_nt(dohs[h], vw)
                gb = g.astype(BF16)
                before = [None] * per
                for b in range(per):
                    cols = slice(b * SB_BLOCK, (b + 1) * SB_BLOCK)
                    gt = _dot(gb[:, cols], before_ones)
                    before[b] = pg + gt[:, :SB_BLOCK]
                    pg = pg + gt[:, SB_BLOCK:]
                dz = (g * (1.0 - beta) - beta * jnp.concatenate(before, axis=1)).astype(BF16)
                dk_span = dk_span + _dot_tn(dz, qhs[h])
                dv_span = dv_span + _dot_tn(a.astype(BF16), dohs[h])
                out.append((pg, dq + _dot(dz, kw)))
            dk_acc[rows, :] += dk_span
            dv_acc[rows, :] += dv_span
            return tuple(out)

        carry = lax.fori_loop(0, sd, lambda sb, c: span_step(sb, c, None), ((zero, zero), (zero, zero)))
        carry = span_step(sd, carry, valid)
        dq_ref[...] = (jnp.where(lane < 64, carry[0][1], carry[1][1]) * SB_SCALE).astype(BF16)

        @pl.when(qi == nq - 1)
        def _():
            dk_ref[...] = dk_acc[...].astype(BF16)
            dv_ref[...] = dv_acc[...].astype(BF16)

        if scatter is not None:
            pl.when(step == kcol * nq - 1)(finish)

    blk = pl.BlockSpec((qrows, SB_BLOCK), lambda hp, qi: (qi, hp))
    col = pl.BlockSpec((s, SB_BLOCK), lambda hp, qi: (0, hp))
    table = pl.BlockSpec((None, qrows, SB_BLOCK), lambda hp, qi: (hp, qi, 0))
    out = jax.ShapeDtypeStruct((s, BRANCH_W), BF16)
    in_specs = [blk,
                pl.BlockSpec((s, SB_BLOCK), lambda hp, qi: (0, kcol + hp)),
                pl.BlockSpec((s, SB_BLOCK), lambda hp, qi: (0, 2 * kcol + hp)),
                blk, table, table]
    out_specs = [blk, col, col]
    out_shape = [out, out, out]
    operands = [p, p, p, dya, tables[0], tables[1]]
    scratch = [pltpu.VMEM((s, SB_BLOCK), F32), pltpu.VMEM((s, SB_BLOCK), F32)]
    if scatter is not None:
        in_specs.append(pl.BlockSpec(memory_space=pl.ANY))
        out_specs.append(pl.BlockSpec(memory_space=pl.ANY))
        out_shape.append(jax.ShapeDtypeStruct(scatter.shape, scatter.dtype))
        operands.append(scatter)
        scratch = scratch + _SCATTER_SEMS
    return pl.pallas_call(
        body, name=name, grid=(kcol, nq), in_specs=in_specs, out_specs=out_specs, out_shape=out_shape,
        scratch_shapes=scratch, compiler_params=_cparams("arbitrary", "arbitrary"),
    )(*operands)


_INV_SQRT2 = 0.7071067811865476
_INV_SQRT2PI = 0.3989422804014327


def _gelu(x):
    return 0.5 * x * (1.0 + lax.erf(x * _INV_SQRT2))


def _gelu_grad(x):
    return 0.5 * (1.0 + lax.erf(x * _INV_SQRT2)) + x * _INV_SQRT2PI * jnp.exp(-0.5 * x * x)


def _chunk_mask(transposed=False):
    r = lax.broadcasted_iota(jnp.int32, (SGU_LEN, SGU_LEN), 0)
    c = lax.broadcasted_iota(jnp.int32, (SGU_LEN, SGU_LEN), 1)
    return (c // 64) >= (r // 64) if transposed else (r // 64) >= (c // 64)


def _sgu_norm(v_raw, g, b):
    zv = _gelu(v_raw)
    xc = zv - jnp.mean(zv, axis=-1, keepdims=True)
    rs = lax.rsqrt(jnp.mean(xc * xc, axis=-1, keepdims=True) + LN_EPS)
    xh = xc * rs
    return xh, rs, xh * g + b


def _sgu_fwd(p, ln_g, ln_b, w, b_col, *, name):
    s = p.shape[0]
    tr = _pick(s, 512, SGU_LEN)

    def body(u_ref, v_ref, g_ref, b_ref, w_ref, bc_ref, o_ref):
        mask = _chunk_mask()
        zu = _gelu(u_ref[...])
        _, _, vn = _sgu_norm(v_ref[...], g_ref[...], b_ref[...])
        vnb = vn.astype(BF16)
        for gi in range(SGU_GROUPS):
            wg = jnp.where(mask, w_ref[gi], 0.0).astype(BF16)
            cs = slice(gi * SGU_LEN, (gi + 1) * SGU_LEN)
            for c in range(tr // SGU_LEN):
                rs_ = slice(c * SGU_LEN, (c + 1) * SGU_LEN)
                vm = _dot(wg, vnb[rs_, cs]) + bc_ref[gi]
                o_ref[rs_, cs] = (zu[rs_, cs] * vm).astype(BF16)

    vec = pl.BlockSpec((1, BRANCH_W), lambda i: (0, 0))
    return pl.pallas_call(
        body, name=name, grid=(s // tr,),
        in_specs=[pl.BlockSpec((tr, BRANCH_W), lambda i: (i, 3)), pl.BlockSpec((tr, BRANCH_W), lambda i: (i, 4)),
                  vec, vec,
                  pl.BlockSpec((SGU_GROUPS, SGU_LEN, SGU_LEN), lambda i: (0, 0, 0)),
                  pl.BlockSpec((SGU_GROUPS, SGU_LEN, 1), lambda i: (0, 0, 0))],
        out_specs=pl.BlockSpec((tr, BRANCH_W), lambda i: (i, 0)),
        out_shape=jax.ShapeDtypeStruct((s, BRANCH_W), BF16),
        compiler_params=_cparams("parallel"),
    )(p, p, ln_g, ln_b, w, b_col)


def _sgu_bwd(p, dyb, ln_g, ln_b, w, w_t, b_col, *, name):
    s = p.shape[0]
    tr = _pick(s, 256, SGU_LEN)

    def body(u_ref, v_ref, dy_ref, g_ref, b_ref, w_ref, wt_ref, bc_ref,
             dz_ref, dg_ref, db_ref, dw_ref, dbc_ref, dvn_s):
        @pl.when(pl.program_id(0) == 0)
        def _():
            dg_ref[...] = jnp.zeros_like(dg_ref)
            db_ref[...] = jnp.zeros_like(db_ref)
            dw_ref[...] = jnp.zeros_like(dw_ref)
            dbc_ref[...] = jnp.zeros_like(dbc_ref)

        mask = _chunk_mask()
        mask_t = _chunk_mask(transposed=True)
        u_raw = u_ref[...]
        v_raw = v_ref[...]
        dy = dy_ref[...]
        zu = _gelu(u_raw)
        xh, rs, vn = _sgu_norm(v_raw, g_ref[...], b_ref[...])
        vnb = vn.astype(BF16)
        dvm_all = dy * zu
        for gi in range(SGU_GROUPS):
            wg = jnp.where(mask, w_ref[gi], 0.0).astype(BF16)
            wgt = jnp.where(mask_t, wt_ref[gi], 0.0).astype(BF16)
            cs = slice(gi * SGU_LEN, (gi + 1) * SGU_LEN)
            dw_g = jnp.zeros((SGU_LEN, SGU_LEN), F32)
            db_g = jnp.zeros((SGU_LEN, 1), F32)
            for c in range(tr // SGU_LEN):
                rs_ = slice(c * SGU_LEN, (c + 1) * SGU_LEN)
                vm = _dot(wg, vnb[rs_, cs]) + bc_ref[gi]
                dz_ref[rs_, cs] = (dy[rs_, cs] * vm * _gelu_grad(u_raw[rs_, cs])).astype(BF16)
                dvm = dvm_all[rs_, cs]
                dvmb = dvm.astype(BF16)
                dw_g = dw_g + _dot_nt(dvmb, vnb[rs_, cs])
                db_g = db_g + jnp.sum(dvm, axis=1, keepdims=True)
                dvn_s[rs_, cs] = _dot(wgt, dvmb)
            dw_ref[gi] += jnp.where(mask, dw_g, 0.0)
            dbc_ref[gi] += db_g
        dvn = dvn_s[...]
        dg_ref[...] += jnp.sum(dvn * xh, axis=0, keepdims=True)
        db_ref[...] += jnp.sum(dvn, axis=0, keepdims=True)
        dxh = dvn * g_ref[...]
        dzv = rs * (dxh - jnp.mean(dxh, axis=-1, keepdims=True) - xh * jnp.mean(dxh * xh, axis=-1, keepdims=True))
        dz_ref[:, BRANCH_W:] = (dzv * _gelu_grad(v_raw)).astype(BF16)

    vec = pl.BlockSpec((1, BRANCH_W), lambda i: (0, 0))
    wspec = pl.BlockSpec((SGU_GROUPS, SGU_LEN, SGU_LEN), lambda i: (0, 0, 0))
    bspec = pl.BlockSpec((SGU_GROUPS, SGU_LEN, 1), lambda i: (0, 0, 0))
    return pl.pallas_call(
        body, name=name, grid=(s // tr,),
        in_specs=[pl.BlockSpec((tr, BRANCH_W), lambda i: (i, 3)), pl.BlockSpec((tr, BRANCH_W), lambda i: (i, 4)),
                  pl.BlockSpec((tr, BRANCH_W), lambda i: (i, 0)), vec, vec, wspec, wspec, bspec],
        out_specs=[pl.BlockSpec((tr, 2 * BRANCH_W), lambda i: (i, 0)), vec, vec, wspec, bspec],
        out_shape=[jax.ShapeDtypeStruct((s, 2 * BRANCH_W), BF16),
                   jax.ShapeDtypeStruct((1, BRANCH_W), F32), jax.ShapeDtypeStruct((1, BRANCH_W), F32),
                   jax.ShapeDtypeStruct((SGU_GROUPS, SGU_LEN, SGU_LEN), F32),
                   jax.ShapeDtypeStruct((SGU_GROUPS, SGU_LEN, 1), F32)],
        scratch_shapes=[pltpu.VMEM((tr, BRANCH_W), F32)],
        compiler_params=_cparams("arbitrary"),
    )(p, p, dyb, ln_g, ln_b, w, w_t, b_col)


def _shift_down(x, prev8, k):
    rolled = pltpu.roll(x, k, 0)
    r8 = lax.broadcasted_iota(jnp.int32, prev8.shape, 0)
    head = jnp.where(r8 < k, pltpu.roll(prev8, k, 0), rolled[:HALO])
    return jnp.concatenate([head, rolled[HALO:]], axis=0)


def _shift_up(x, next8, k):
    n = x.shape[0]
    rolled = pltpu.roll(x, n - k, 0)
    r8 = lax.broadcasted_iota(jnp.int32, next8.shape, 0)
    tail = jnp.where(r8 >= HALO - k, pltpu.roll(next8, HALO - k, 0), rolled[n - HALO:])
    return jnp.concatenate([rolled[:n - HALO], tail], axis=0)


def _conv_specs(s, tr):
    nb = tr // HALO
    last = s // HALO - 1
    tile = lambda cb: pl.BlockSpec((tr, 128), lambda j, i: (i, cb * 4 + j))
    above = lambda cb: pl.BlockSpec((HALO, 128), lambda j, i: (jnp.maximum(i * nb - 1, 0), cb * 4 + j))
    below = lambda cb: pl.BlockSpec((HALO, 128), lambda j, i: (jnp.minimum((i + 1) * nb, last), cb * 4 + j))
    return tile, above, below


def _conv_fwd(p, cw, *, name):
    s = p.shape[0]
    tr = _pick(s, 1024, HALO)
    tile, above, _ = _conv_specs(s, tr)

    def body(cb_ref, cc_ref, cx_ref, ccp_ref, cxp_ref, w_ref, o_ref):
        first = pl.program_id(1) == 0
        y = cc_ref[...] * cx_ref[...]
        yp = jnp.where(first, 0.0, ccp_ref[...] * cxp_ref[...])
        conv = w_ref[2:3, :] * y + w_ref[1:2, :] * _shift_down(y, yp, 1) + w_ref[0:1, :] * _shift_down(y, yp, 2)
        o_ref[...] = (cb_ref[...] * conv).astype(BF16)

    return pl.pallas_call(
        body, name=name, grid=(4, s // tr),
        in_specs=[tile(5), tile(6), tile(7), above(6), above(7), pl.BlockSpec((3, 128), lambda j, i: (0, j))],
        out_specs=pl.BlockSpec((tr, 128), lambda j, i: (i, j)),
        out_shape=jax.ShapeDtypeStruct((s, BRANCH_W), BF16),
        compiler_params=_cparams("parallel", "parallel"),
    )(p, p, p, p, p, cw)


def _conv_bwd(p, dyc, cw, *, name):
    s = p.shape[0]
    tr = _pick(s, 1024, HALO)
    nt = s // tr
    nb = tr // HALO
    last = s // HALO - 1
    tile, above, below = _conv_specs(s, tr)

    def body(cb_ref, cc_ref, cx_ref, ccp_ref, cxp_ref, cbn_ref, dy_ref, dyn_ref, w_ref,
             dcb_ref, dcc_ref, dcx_ref, dw_ref):
        i = pl.program_id(1)

        @pl.when(i == 0)
        def _():
            dw_ref[...] = jnp.zeros_like(dw_ref)

        cb = cb_ref[...]
        cc = cc_ref[...]
        cx = cx_ref[...]
        y = cc * cx
        yp = jnp.where(i == 0, 0.0, ccp_ref[...] * cxp_ref[...])
        y1 = _shift_down(y, yp, 1)
        y2 = _shift_down(y, yp, 2)
        w0, w1, w2 = w_ref[0:1, :], w_ref[1:2, :], w_ref[2:3, :]
        conv = w2 * y + w1 * y1 + w0 * y2
        dyc_v = dy_ref[...]
        dconv = dyc_v * cb
        dn = jnp.where(i == nt - 1, 0.0, dyn_ref[...] * cbn_ref[...])
        dyv = w2 * dconv + w1 * _shift_up(dconv, dn, 1) + w0 * _shift_up(dconv, dn, 2)
        dcb_ref[...] = (dyc_v * conv).astype(BF16)
        dcc_ref[...] = (dyv * cx).astype(BF16)
        dcx_ref[...] = (dyv * cc).astype(BF16)
        dw_ref[0:1, :] += jnp.sum(dconv * y2, axis=0, keepdims=True)
        dw_ref[1:2, :] += jnp.sum(dconv * y1, axis=0, keepdims=True)
        dw_ref[2:3, :] += jnp.sum(dconv * y, axis=0, keepdims=True)

    dy_tile = pl.BlockSpec((tr, 128), lambda j, i: (i, j))
    dy_below = pl.BlockSpec((HALO, 128), lambda j, i: (jnp.minimum((i + 1) * nb, last), j))
    out_tile = lambda cb: pl.BlockSpec((tr, 128), lambda j, i: (i, cb * 4 + j))
    w_spec = pl.BlockSpec((3, 128), lambda j, i: (0, j))
    dcb, dcc, dcx, dw = pl.pallas_call(
        body, name=name, grid=(4, nt),
        in_specs=[tile(5), tile(6), tile(7), above(6), above(7), below(5), dy_tile, dy_below, w_spec],
        out_specs=[dy_tile, dy_tile, dy_tile, w_spec],
        out_shape=[jax.ShapeDtypeStruct((s, BRANCH_W), BF16)] * 3 + [jax.ShapeDtypeStruct((3, BRANCH_W), F32)],
        compiler_params=_cparams("parallel", "arbitrary"),
    )(p, p, p, p, p, p, dyc, dyc, cw)
    return dcb, dcc, dcx, dw


def _merge_fwd(ya, yb, yc, wb, p, *, name):
    s = p.shape[0]
    tr = _pick(s, 256, 16)

    def body(ya_ref, yb_ref, yc_ref, wb_ref, g0_ref, g1_ref, g2_ref, o_ref):
        acc = jnp.zeros((tr, D_MODEL), F32)
        for n, (y_ref, g_ref) in enumerate(((ya_ref, g0_ref), (yb_ref, g1_ref), (yc_ref, g2_ref))):
            acc = acc + _sigmoid(g_ref[...]) * _dot(y_ref[...].astype(BF16), wb_ref[n])
        o_ref[...] = acc.astype(BF16)

    yspec = pl.BlockSpec((tr, BRANCH_W), lambda i: (i, 0))
    gate = lambda n: pl.BlockSpec((tr, D_MODEL), lambda i: (i, 4 + n))
    return pl.pallas_call(
        body, name=name, grid=(s // tr,),
        in_specs=[yspec, yspec, yspec, pl.BlockSpec((3, BRANCH_W, D_MODEL), lambda i: (0, 0, 0)),
                  gate(0), gate(1), gate(2)],
        out_specs=pl.BlockSpec((tr, D_MODEL), lambda i: (i, 0)),
        out_shape=jax.ShapeDtypeStruct((s, D_MODEL), BF16),
        compiler_params=_cparams("parallel"),
    )(ya, yb, yc, wb, p, p, p)


def _merge_bwd(dm, ya, yb, yc, wb, p, *, name):
    s = p.shape[0]
    tr = _pick(s, 256, 16)

    def body(dm_ref, ya_ref, yb_ref, yc_ref, wb_ref, g0_ref, g1_ref, g2_ref,
             dya_ref, dyb_ref, dyc_ref, dg_ref, dbrd0_ref, dbrd1_ref, dbrd2_ref):
        dmv = dm_ref[...]
        ys = (ya_ref, yb_ref, yc_ref)
        gs = (g0_ref, g1_ref, g2_ref)
        dys = (dya_ref, dyb_ref, dyc_ref)
        dbrds = (dbrd0_ref, dbrd1_ref, dbrd2_ref)
        for n in range(3):
            brd = _dot(ys[n][...].astype(BF16), wb_ref[n])
            sg = _sigmoid(gs[n][...])
            dbrd = (sg * dmv).astype(BF16)
            dbrds[n][...] = dbrd
            dg_ref[:, n * D_MODEL:(n + 1) * D_MODEL] = (dmv * brd * sg * (1.0 - sg)).astype(BF16)
            dys[n][...] = _dot_nt(dbrd, wb_ref[n]).astype(dys[n].dtype)

    yspec = pl.BlockSpec((tr, BRANCH_W), lambda i: (i, 0))
    gate = lambda n: pl.BlockSpec((tr, D_MODEL), lambda i: (i, 4 + n))
    row = pl.BlockSpec((tr, D_MODEL), lambda i: (i, 0))
    return pl.pallas_call(
        body, name=name, grid=(s // tr,),
        in_specs=[row, yspec, yspec, yspec, pl.BlockSpec((3, BRANCH_W, D_MODEL), lambda i: (0, 0, 0)),
                  gate(0), gate(1), gate(2)],
        out_specs=[yspec, yspec, yspec, pl.BlockSpec((tr, 3 * D_MODEL), lambda i: (i, 0)), row, row, row],
        out_shape=[jax.ShapeDtypeStruct((s, BRANCH_W), BF16)] + [jax.ShapeDtypeStruct((s, BRANCH_W), F32)] * 2
                  + [jax.ShapeDtypeStruct((s, 3 * D_MODEL), BF16)] + [jax.ShapeDtypeStruct((s, D_MODEL), BF16)] * 3,
        compiler_params=_cparams("parallel"),
    )(dm, ya, yb, yc, wb, p, p, p)


def _xa_probs(q, k):
    sc = _dot_nt(q, k) * XA_SCALE
    e = jnp.exp(sc - jnp.max(sc, axis=-1, keepdims=True))
    return e / jnp.sum(e, axis=-1, keepdims=True)


def _xa_fwd(q, k, v, *, name):
    s = q.shape[0]
    mt = k.shape[0]
    tr = _pick(s, 2048, 16)

    def body(q_ref, k_ref, v_ref, o_ref):
        pr = _xa_probs(q_ref[...], k_ref[...])
        o_ref[...] = _dot(pr.astype(BF16), v_ref[...]).astype(BF16)

    qs = pl.BlockSpec((tr, XA_HEAD), lambda h, i: (i, h))
    ks = pl.BlockSpec((mt, XA_HEAD), lambda h, i: (0, h))
    return pl.pallas_call(
        body, name=name, grid=(D_MODEL // XA_HEAD, s // tr),
        in_specs=[qs, ks, ks], out_specs=qs,
        out_shape=jax.ShapeDtypeStruct((s, D_MODEL), BF16),
        compiler_params=_cparams("parallel", "parallel"),
    )(q, k, v)


def _xa_bwd(q, k, v, do, *, name):
    s = q.shape[0]
    mt = k.shape[0]
    tr = _pick(s, 2048, 16)

    def body(q_ref, k_ref, v_ref, do_ref, dq_ref, dk_ref, dv_ref):
        @pl.when(pl.program_id(1) == 0)
        def _():
            dk_ref[...] = jnp.zeros_like(dk_ref)
            dv_ref[...] = jnp.zeros_like(dv_ref)

        qv = q_ref[...]
        kv = k_ref[...]
        dov = do_ref[...]
        pr = _xa_probs(qv, kv)
        dpr = _dot_nt(dov, v_ref[...])
        ds = (pr * (dpr - jnp.sum(dpr * pr, axis=-1, keepdims=True)) * XA_SCALE).astype(BF16)
        dq_ref[...] = _dot(ds, kv).astype(BF16)
        dk_ref[...] += _dot_tn(ds, qv)
        dv_ref[...] += _dot_tn(pr.astype(BF16), dov)

    qs = pl.BlockSpec((tr, XA_HEAD), lambda h, i: (i, h))
    ks = pl.BlockSpec((mt, XA_HEAD), lambda h, i: (0, h))
    return pl.pallas_call(
        body, name=name, grid=(D_MODEL // XA_HEAD, s // tr),
        in_specs=[qs, ks, ks, qs], out_specs=[qs, ks, ks],
        out_shape=[jax.ShapeDtypeStruct((s, D_MODEL), BF16), jax.ShapeDtypeStruct((mt, D_MODEL), F32),
                   jax.ShapeDtypeStruct((mt, D_MODEL), F32)],
        compiler_params=_cparams("parallel", "arbitrary"),
    )(q, k, v, do)


def _ffn_in(h, wg, wu, *, name):
    s, d = h.shape
    f = wg.shape[0]
    tm = _pick(s, 1024, 128)
    tn = _pick(f, 1408, 128)

    def body(h_ref, wg_ref, wu_ref, a_ref, b_ref, o_ref):
        hv = h_ref[...]
        av = _dot_nt(hv, wg_ref[...])
        bv = _dot_nt(hv, wu_ref[...])
        a_ref[...] = av.astype(BF16)
        b_ref[...] = bv.astype(BF16)
        o_ref[...] = (av * _sigmoid(av) * bv).astype(BF16)

    wspec = pl.BlockSpec((tn, d), lambda i, j: (j, 0))
    tile = pl.BlockSpec((tm, tn), lambda i, j: (i, j))
    return pl.pallas_call(
        body, name=name, grid=(s // tm, f // tn),
        in_specs=[pl.BlockSpec((tm, d), lambda i, j: (i, 0)), wspec, wspec],
        out_specs=[tile, tile, tile], out_shape=[jax.ShapeDtypeStruct((s, f), BF16)] * 3,
        compiler_params=_cparams("parallel", "parallel"),
    )(h, wg, wu)


def _ffn_in_bwd(dx, wd, a, b, *, name):
    s, d = dx.shape
    f = wd.shape[0]
    tm = _pick(s, 1024, 128)
    tn = _pick(f, 1408, 128)

    def body(dx_ref, wd_ref, a_ref, b_ref, da_ref, db_ref):
        dhv = _dot_nt(dx_ref[...].astype(BF16), wd_ref[...])
        av = a_ref[...].astype(F32)
        sg = _sigmoid(av)
        silu = av * sg
        da_ref[...] = (dhv * b_ref[...].astype(F32) * (sg + silu * (1.0 - sg))).astype(BF16)
        db_ref[...] = (dhv * silu).astype(BF16)

    tile = pl.BlockSpec((tm, tn), lambda i, j: (i, j))
    return pl.pallas_call(
        body, name=name, grid=(s // tm, f // tn),
        in_specs=[pl.BlockSpec((tm, d), lambda i, j: (i, 0)), pl.BlockSpec((tn, d), lambda i, j: (j, 0)), tile, tile],
        out_specs=[tile, tile], out_shape=[jax.ShapeDtypeStruct((s, f), BF16)] * 2,
        compiler_params=_cparams("parallel", "parallel"),
    )(dx, wd, a, b)


def _adamw(w, g, m, v, *, name):
    r, c = w.shape
    tr = _pick(r, 512, 8)

    def body(w_ref, g_ref, m_ref, v_ref, d_ref, mo_ref, vo_ref):
        gv = g_ref[...]
        mn = ADAM_B1 * m_ref[...] + (1.0 - ADAM_B1) * gv
        vn = ADAM_B2 * v_ref[...] + (1.0 - ADAM_B2) * (gv * gv)
        m_hat = mn / (1.0 - ADAM_B1 ** ADAM_STEP)
        v_hat = vn / (1.0 - ADAM_B2 ** ADAM_STEP)
        d_ref[...] = -ADAM_LR * (m_hat / (jnp.sqrt(v_hat) + ADAM_EPS) + ADAM_WD * w_ref[...])
        mo_ref[...] = mn
        vo_ref[...] = vn

    spec = pl.BlockSpec((tr, c), lambda i: (i, 0))
    shp = jax.ShapeDtypeStruct((r, c), F32)
    return pl.pallas_call(
        body, name=name, grid=(r // tr,), in_specs=[spec] * 4, out_specs=[spec] * 3,
        out_shape=[shp] * 3, compiler_params=_cparams("parallel"),
    )(w, g, m, v)


def _position():
    return lax.axis_index("x"), lax.axis_index("y"), lax.axis_index("c")


def _all_gather(x, *, name):
    t, c_ = x.shape

    def body(x_ref, out_ref, send_sems, recv_sems, local_sem):
        start, forward, finish = _gather_phases(x_ref, out_ref, send_sems, recv_sems, local_sem)
        start()
        forward()
        finish()

    return pl.pallas_call(
        body, name=name,
        out_shape=jax.ShapeDtypeStruct((N_DEV, t, c_), x.dtype),
        in_specs=[pl.BlockSpec(memory_space=pl.ANY)],
        out_specs=pl.BlockSpec(memory_space=pl.ANY),
        scratch_shapes=_GATHER_SEMS,
    )(x)


_GATHER_SEMS = [pltpu.SemaphoreType.DMA((7,)), pltpu.SemaphoreType.DMA((7,)), pltpu.SemaphoreType.DMA]


def _gather_phases(x_ref, out_ref, send_sems, recv_sems, local_sem):
    x_, y_, c = _position()
    me, sibling = (x_, y_, c), (x_, y_, 1 - c)
    chips = [(1 - x_, y_), (x_, 1 - y_), (1 - x_, 1 - y_)]

    def block(px, py, pc):
        return out_ref.at[4 * px + 2 * py + pc]

    def copy(k, blk, to, src=None):
        return pltpu.make_async_remote_copy(
            src_ref=block(*blk) if src is None else src, dst_ref=block(*blk),
            send_sem=send_sems.at[k], recv_sem=recv_sems.at[k], device_id=to, device_id_type=MESH)

    mine = pltpu.make_async_copy(x_ref, block(*me), local_sem)
    first = [copy(0, me, sibling, src=x_ref)]
    first += [copy(1 + j, me, (*chip, c), src=x_ref) for j, chip in enumerate(chips)]
    passed = [copy(4 + j, (*chip, c), sibling) for j, chip in enumerate(chips)]

    def start():
        mine.start()
        for cp in first:
            cp.start()

    def forward():
        for j, chip in enumerate(chips):
            copy(1 + j, (*chip, c), me).wait_recv()
            passed[j].start()

    def finish():
        copy(0, sibling, me).wait_recv()
        for j, chip in enumerate(chips):
            copy(4 + j, (*chip, 1 - c), me).wait_recv()
        for cp in first + passed:
            cp.wait_send()
        mine.wait()

    return start, forward, finish


_SCATTER_SEMS = [pltpu.SemaphoreType.DMA((7,)), pltpu.SemaphoreType.DMA((7,)), pltpu.SemaphoreType.DMA]


def _scatter_phases(g_ref, r_ref, send_sems, recv_sems, local_sem):
    x_, y_, c = _position()
    me = 4 * x_ + 2 * y_ + c
    local = pltpu.make_async_copy(g_ref.at[me], r_ref.at[me], local_sem)
    copies = []
    for k in range(1, N_DEV):
        to = (x_ ^ (k >> 2), y_ ^ ((k >> 1) & 1), c ^ (k & 1))
        copies.append(pltpu.make_async_remote_copy(
            src_ref=g_ref.at[me ^ k], dst_ref=r_ref.at[me], send_sem=send_sems.at[k - 1],
            recv_sem=recv_sems.at[k - 1], device_id=to, device_id_type=MESH))

    def start():
        local.start()
        for cp in copies:
            cp.start()

    def finish():
        for k in range(1, N_DEV):
            pltpu.make_async_remote_copy(
                src_ref=g_ref.at[me], dst_ref=r_ref.at[me ^ k], send_sem=send_sems.at[k - 1],
                recv_sem=recv_sems.at[k - 1], device_id=(x_, y_, c), device_id_type=MESH).wait_recv()
        for cp in copies:
            cp.wait_send()
        local.wait()

    return start, finish


def _sum_devices(r8, *, name):
    _, t, c_ = r8.shape
    tr = _pick(t, 512, 16)

    def body(r_ref, o_ref):
        acc = r_ref[0].astype(F32)
        for d in range(1, N_DEV):
            acc = acc + r_ref[d].astype(F32)
        o_ref[...] = acc

    return pl.pallas_call(
        body, name=name, grid=(t // tr,),
        in_specs=[pl.BlockSpec((N_DEV, tr, c_), lambda i: (0, i, 0))],
        out_specs=pl.BlockSpec((tr, c_), lambda i: (i, 0)),
        out_shape=jax.ShapeDtypeStruct((t, c_), F32),
        compiler_params=_cparams("parallel"),
    )(r8)


def _all_reduce_small(x, *, name):
    r, c_ = x.shape

    def body(x_ref, o_ref, buf, send_sems, recv_sems):
        x_, y_, c = _position()
        me = 4 * x_ + 2 * y_ + c
        buf[me] = x_ref[...]
        copies = []
        for k in range(1, N_DEV):
            to = (x_ ^ (k >> 2), y_ ^ ((k >> 1) & 1), c ^ (k & 1))
            copies.append(pltpu.make_async_remote_copy(
                src_ref=x_ref, dst_ref=buf.at[me], send_sem=send_sems.at[k - 1], recv_sem=recv_sems.at[k - 1],
                device_id=to, device_id_type=MESH))
        for cp in copies:
            cp.start()
        for k in range(1, N_DEV):
            src = me ^ k
            pltpu.make_async_remote_copy(
                src_ref=x_ref, dst_ref=buf.at[src], send_sem=send_sems.at[k - 1], recv_sem=recv_sems.at[k - 1],
                device_id=(x_, y_, c), device_id_type=MESH).wait_recv()
        for cp in copies:
            cp.wait_send()
        acc = buf[0]
        for d in range(1, N_DEV):
            acc = acc + buf[d]
        o_ref[...] = acc

    return pl.pallas_call(
        body, name=name,
        out_shape=jax.ShapeDtypeStruct((r, c_), F32),
        in_specs=[pl.BlockSpec(memory_space=pltpu.VMEM)],
        out_specs=pl.BlockSpec(memory_space=pltpu.VMEM),
        scratch_shapes=[pltpu.VMEM((N_DEV, r, c_), F32), pltpu.SemaphoreType.DMA((7,)), pltpu.SemaphoreType.DMA((7,))],
    )(x)


def _rs_pair_exchange(g8, *, name):
    _, t, c_ = g8.shape

    def body(g_ref, r_ref, send_sems, recv_sems):
        x_, y_, c = _position()
        copies = [pltpu.make_async_remote_copy(
            src_ref=g_ref.at[2 * ch + (1 - c)], dst_ref=r_ref.at[ch],
            send_sem=send_sems.at[ch], recv_sem=recv_sems.at[ch],
            device_id=(x_, y_, 1 - c), device_id_type=MESH) for ch in range(4)]
        for cp in copies:
            cp.start()
        for cp in copies:
            cp.wait()

    return pl.pallas_call(
        body, name=name,
        out_shape=jax.ShapeDtypeStruct((4, t, c_), g8.dtype),
        in_specs=[pl.BlockSpec(memory_space=pl.ANY)],
        out_specs=pl.BlockSpec(memory_space=pl.ANY),
        scratch_shapes=[pltpu.SemaphoreType.DMA((4,)), pltpu.SemaphoreType.DMA((4,))],
    )(g8)


def _pair_add(core, g8, recv, *, name):
    _, t, c_ = g8.shape
    tr = _pick(t, 512, 16)

    def body(core_ref, g_ref, r_ref, o_ref):
        o_ref[...] = (g_ref[...].astype(F32) + r_ref[...].astype(F32)).astype(o_ref.dtype)

    grid_spec = pltpu.PrefetchScalarGridSpec(
        num_scalar_prefetch=1, grid=(4, t // tr),
        in_specs=[pl.BlockSpec((None, tr, c_), lambda ch, i, core_ref: (2 * ch + core_ref[0], i, 0)),
                  pl.BlockSpec((None, tr, c_), lambda ch, i, core_ref: (ch, i, 0))],
        out_specs=pl.BlockSpec((None, tr, c_), lambda ch, i, core_ref: (ch, i, 0)))
    return pl.pallas_call(
        body, name=name, grid_spec=grid_spec,
        out_shape=jax.ShapeDtypeStruct((4, t, c_), g8.dtype),
        compiler_params=_cparams("parallel", "parallel"),
    )(core, g8, recv)


def _rs_chip_exchange(part, *, name):
    _, t, c_ = part.shape

    def body(p_ref, r_ref, send_sems, recv_sems, local_sem):
        x_, y_, c = _position()
        mine = 2 * x_ + y_
        local = pltpu.make_async_copy(p_ref.at[mine], r_ref.at[mine], local_sem)
        local.start()
        chips = [(1 - x_, y_), (x_, 1 - y_), (1 - x_, 1 - y_)]
        copies = [pltpu.make_async_remote_copy(
            src_ref=p_ref.at[2 * px + py], dst_ref=r_ref.at[mine],
            send_sem=send_sems.at[k], recv_sem=recv_sems.at[k],
            device_id=(px, py, c), device_id_type=MESH) for k, (px, py) in enumerate(chips)]
        for cp in copies:
            cp.start()
        for k, (px, py) in enumerate(chips):
            pltpu.make_async_remote_copy(
                src_ref=p_ref.at[mine], dst_ref=r_ref.at[2 * px + py],
                send_sem=send_sems.at[k], recv_sem=recv_sems.at[k],
                device_id=(x_, y_, c), device_id_type=MESH).wait_recv()
        for cp in copies:
            cp.wait_send()
        local.wait()

    return pl.pallas_call(
        body, name=name,
        out_shape=jax.ShapeDtypeStruct((4, t, c_), part.dtype),
        in_specs=[pl.BlockSpec(memory_space=pl.ANY)],
        out_specs=pl.BlockSpec(memory_space=pl.ANY),
        scratch_shapes=[pltpu.SemaphoreType.DMA((3,)), pltpu.SemaphoreType.DMA((3,)), pltpu.SemaphoreType.DMA],
    )(part)


def _sum_chips(r4, *, name):
    _, t, c_ = r4.shape
    tr = _pick(t, 512, 16)

    def body(r_ref, o_ref):
        acc = r_ref[0].astype(F32)
        for ch in range(1, 4):
            acc = acc + r_ref[ch].astype(F32)
        o_ref[...] = acc

    return pl.pallas_call(
        body, name=name, grid=(t // tr,),
        in_specs=[pl.BlockSpec((4, tr, c_), lambda i: (0, i, 0))],
        out_specs=pl.BlockSpec((tr, c_), lambda i: (i, 0)),
        out_shape=jax.ShapeDtypeStruct((t, c_), F32),
        compiler_params=_cparams("parallel"),
    )(r4)


BIG = (
    ("w_in", (IN_COLS // N_DEV, D_MODEL), 0),
    ("w_branch", (3, BRANCH_W, D_MODEL // N_DEV), 2),
    ("w_out", (D_MODEL // N_DEV, D_MODEL), 0),
    ("w_q_xa", (D_MODEL // N_DEV, D_MODEL), 0),
    ("w_k_xa", (D_MODEL // N_DEV, D_MODEL), 0),
    ("w_v_xa", (D_MODEL // N_DEV, D_MODEL), 0),
    ("w_o_xa", (D_MODEL // N_DEV, D_MODEL), 0),
    ("w_gate_ffn", (FFN // N_DEV, D_MODEL), 0),
    ("w_up_ffn", (FFN // N_DEV, D_MODEL), 0),
    ("w_down_ffn", (FFN // N_DEV, D_MODEL), 0),
)
TRANSPOSED = ("w_in", "w_gate_ffn", "w_up_ffn")
_BIG_LAYOUT = {n: (shp, ax) for n, shp, ax in BIG}
PACK_COLS = 1024


def _stored(name, shard):
    return shard.T if name in TRANSPOSED else shard


def _size(shape):
    n = 1
    for d in shape:
        n *= d
    return n


def _pack_shards(items, shards):
    return jnp.concatenate([shards[it].reshape(-1, PACK_COLS) for it in items], axis=0)


def _unpack_gathered(items, g):
    out = {}
    r0 = 0
    for it in items:
        shp, ax = _BIG_LAYOUT[it[0]]
        rows = _size(shp) // PACK_COLS
        blk = g[:, r0:r0 + rows].reshape((N_DEV,) + shp)
        r0 += rows
        blk = jnp.moveaxis(blk, 0, ax)
        full = list(shp)
        full[ax] = shp[ax] * N_DEV
        out[it] = blk.reshape(full)
    return out


def _pack_full(items, full):
    parts = []
    for it in items:
        shp, ax = _BIG_LAYOUT[it[0]]
        t = full[it].reshape(shp[:ax] + (N_DEV, shp[ax]) + shp[ax + 1:])
        t = jnp.moveaxis(t, ax, 0)
        parts.append(t.reshape(N_DEV, -1, PACK_COLS))
    rows = sum(part.shape[1] for part in parts)
    if rows % 128:
        parts.append(jnp.zeros((N_DEV, 128 - rows % 128, PACK_COLS), parts[0].dtype))
    return jnp.concatenate(parts, axis=1)


def _unpack_shard(items, flat):
    out = {}
    r0 = 0
    for it in items:
        shp, _ = _BIG_LAYOUT[it[0]]
        rows = _size(shp) // PACK_COLS
        out[it] = flat[r0:r0 + rows].reshape(shp)
        r0 += rows
    return out


SMALL = (
    ("norm_mix_g", (DEPTH, D_MODEL)),
    ("sgu_ln_g", (DEPTH, BRANCH_W)),
    ("sgu_ln_b", (DEPTH, BRANCH_W)),
    ("w_spatial", (DEPTH, SGU_GROUPS, SGU_LEN, SGU_LEN)),
    ("b_spatial", (DEPTH, SGU_GROUPS, SGU_LEN)),
    ("conv_w", (DEPTH, 3, BRANCH_W)),
    ("norm_xa_g", (DEPTH, D_MODEL)),
    ("mem_norm_g", (DEPTH, D_MODEL)),
    ("norm_ffn_g", (DEPTH, D_MODEL)),
    ("final_g", (D_MODEL,)),
)


def _pack_small(grads):
    flat = jnp.concatenate([grads[n].reshape(-1) for n, _ in SMALL])
    rows = -(-flat.shape[0] // PACK_COLS)
    rows = -(-rows // 8) * 8
    flat = jnp.pad(flat, (0, rows * PACK_COLS - flat.shape[0]))
    return flat.reshape(rows, PACK_COLS)


def _unpack_small(buf):
    flat = buf.reshape(-1)
    out = {}
    o = 0
    for n, shp in SMALL:
        out[n] = flat[o:o + _size(shp)].reshape(shp)
        o += _size(shp)
    return out


def _layer_fwd(l, x, mem, wt, sm, gather=None):
    t = f"l{l}_"
    sv = {"x0": x}
    h = _rms_fwd(x, sm["norm_mix_g"][l][None], name=t + "rms_mix")
    p = _mm(h, wt["w_in", l], tb=True, name=t + "in_proj", tm=2048)
    if gather is None:
        ya, *tables = _sb_fwd(p, name=t + "sb_fwd")
    else:
        ya, *tables, gathered = _sb_fwd(p, name=t + "sb_fwd", gather=gather[1])
        wt.update(_unpack_gathered(gather[0], gathered))
    w_sp = sm["w_spatial"][l]
    b_col = sm["b_spatial"][l][:, :, None]
    ln_g, ln_b = sm["sgu_ln_g"][l][None], sm["sgu_ln_b"][l][None]
    yb = _sgu_fwd(p, ln_g, ln_b, w_sp, b_col, name=t + "sgu_fwd")
    yc = _conv_fwd(p, sm["conv_w"][l], name=t + "conv_fwd")
    merged = _merge_fwd(ya, yb, yc, wt["w_branch", l], p, name=t + "merge_fwd")
    x1 = _mm(merged, wt["w_out", l], add=x, name=t + "out_proj")
    sv.update(h=h, p=p, ya=ya, yb=yb, yc=yc, merged=merged, x1=x1, tables=tables)

    h2 = _rms_fwd(x1, sm["norm_xa_g"][l][None], name=t + "rms_xa")
    mn = _rms_fwd(mem, sm["mem_norm_g"][l][None], name=t + "rms_mem")
    q = _mm(h2, wt["w_q_xa", l], out_dtype=BF16, name=t + "xa_q", tm=2048)
    k = _mm(mn, wt["w_k_xa", l], out_dtype=BF16, name=t + "xa_k")
    v = _mm(mn, wt["w_v_xa", l], out_dtype=BF16, name=t + "xa_v")
    o = _xa_fwd(q, k, v, name=t + "xa_fwd")
    x2 = _mm(o, wt["w_o_xa", l], add=x1, name=t + "xa_o")
    sv.update(h2=h2, mn=mn, q=q, k=k, v=v, o=o, x2=x2)

    h3 = _rms_fwd(x2, sm["norm_ffn_g"][l][None], name=t + "rms_ffn")
    a, b, hd = _ffn_in(h3, wt["w_gate_ffn", l], wt["w_up_ffn", l], name=t + "ffn_in")
    x3 = _mm(hd, wt["w_down_ffn", l], add=x2, name=t + "ffn_down", tk=FFN)
    sv.update(h3=h3, a=a, b=b, hd=hd)
    return x3, sv


def _layer_bwd(l, dx3, mem, wt, sm, sv, scatter=None):
    t = f"l{l}_b_"
    gb, gs = {}, {}
    gb["w_down_ffn"] = _mm(sv["hd"], dx3, ta=True, out_dtype=BF16, name=t + "ffn_down_dw", tm=1408)
    da, db = _ffn_in_bwd(dx3, wt["w_down_ffn", l], sv["a"], sv["b"], name=t + "ffn_in_bwd")
    gb["w_gate_ffn"] = _mm(da, sv["h3"], ta=True, out_dtype=BF16, name=t + "ffn_gate_dw", tm=1408)
    gb["w_up_ffn"] = _mm(db, sv["h3"], ta=True, out_dtype=BF16, name=t + "ffn_up_dw", tm=1408)
    dh3 = _mm(da, wt["w_gate_ffn", l], name=t + "ffn_gate_dx", tk=1408)
    dx2, dg = _mm(db, wt["w_up_ffn", l], add=dh3, rms=(sv["x2"], sm["norm_ffn_g"][l][None], dx3),
                  name=t + "ffn_up_dx", tm=512, tk=1408)
    gs["norm_ffn_g"] = dg[0]
    do = _mm(dx2, wt["w_o_xa", l], tb=True, out_dtype=BF16, name=t + "xa_o_dx")
    gb["w_o_xa"] = _mm(sv["o"], dx2, ta=True, out_dtype=BF16, name=t + "xa_o_dw")
    dq, dk, dv = _xa_bwd(sv["q"], sv["k"], sv["v"], do, name=t + "xa_bwd")
    dx1, dg = _mm(dq, wt["w_q_xa", l], tb=True, rms=(sv["x1"], sm["norm_xa_g"][l][None], dx2),
                  name=t + "xa_q_dx", tm=1024)
    gs["norm_xa_g"] = dg[0]
    gb["w_q_xa"] = _mm(sv["h2"], dq, ta=True, out_dtype=BF16, name=t + "xa_q_dw")
    gb["w_k_xa"] = _mm(sv["mn"], dk, ta=True, out_dtype=BF16, name=t + "xa_k_dw")
    gb["w_v_xa"] = _mm(sv["mn"], dv, ta=True, out_dtype=BF16, name=t + "xa_v_dw")
    dmn = _mm(dk, wt["w_k_xa", l], tb=True, name=t + "xa_k_dx")
    dmn = _mm(dv, wt["w_v_xa", l], tb=True, add=dmn, name=t + "xa_v_dx")
    _, dg = _rms_bwd(mem, sm["mem_norm_g"][l][None], dmn, jnp.zeros_like(mem), name=t + "rms_mem")
    gs["mem_norm_g"] = dg[0]
    dm = _mm(dx1, wt["w_out", l], tb=True, name=t + "out_proj_dx")
    gb["w_out"] = _mm(sv["merged"], dx1, ta=True, out_dtype=BF16, name=t + "out_proj_dw")
    p = sv["p"]
    dya, dyb, dyc, dgates, *dbrd = _merge_bwd(dm, sv["ya"], sv["yb"], sv["yc"], wt["w_branch", l], p,
                                              name=t + "merge_bwd")
    gb["w_branch"] = jnp.stack([
        _mm(sv[y], dbrd[n], ta=True, out_dtype=BF16, name=t + f"branch{n}_dw")
        for n, y in enumerate(("ya", "yb", "yc"))])
    dcb, dcc, dcx, dcw = _conv_bwd(p, dyc, sm["conv_w"][l], name=t + "conv_bwd")
    gs["conv_w"] = dcw
    w_sp = sm["w_spatial"][l]
    dz, dlg, dlb, dwsp, dbsp = _sgu_bwd(p, dyb, sm["sgu_ln_g"][l][None], sm["sgu_ln_b"][l][None], w_sp,
                                        jnp.swapaxes(w_sp, 1, 2), sm["b_spatial"][l][:, :, None],
                                        name=t + "sgu_bwd")
    gs.update(sgu_ln_g=dlg[0], sgu_ln_b=dlb[0], w_spatial=dwsp, b_spatial=dbsp[:, :, 0])
    received = None
    if scatter is None:
        dq_a, dk_a, dv_a = _sb_bwd(p, dya, sv["tables"], name=t + "sb_bwd")
    else:
        items, earlier = scatter
        ready = {**earlier, **{(n, l): g for n, g in gb.items()}}
        dq_a, dk_a, dv_a, received = _sb_bwd(p, dya, sv["tables"], name=t + "sb_bwd",
                                             scatter=_pack_full(items, ready))
    dp = jnp.concatenate([dq_a, dk_a, dv_a, dz, dcb, dcc, dcx, dgates], axis=1)
    gb["w_in"] = _mm(dp, sv["h"], ta=True, out_dtype=BF16, name=t + "in_proj_dw")
    dx, dg = _mm(dp, wt["w_in", l], rms=(sv["x0"], sm["norm_mix_g"][l][None], dx1),
                 name=t + "in_proj_dx", tm=1024, tk=1792)
    gs["norm_mix_g"] = dg[0]
    return dx, gb, gs, received


_WEIGHTS = ("norm_mix_g", "w_in", "sgu_ln_g", "sgu_ln_b", "w_spatial", "b_spatial", "conv_w", "w_branch", "w_out",
            "norm_xa_g", "mem_norm_g", "w_q_xa", "w_k_xa", "w_v_xa", "w_o_xa", "norm_ffn_g", "w_gate_ffn",
            "w_up_ffn", "w_down_ffn", "final_g")


def kernel(x, mem, norm_mix_g, w_in, sgu_ln_g, sgu_ln_b, w_spatial, b_spatial, conv_w, w_branch, w_out, norm_xa_g, mem_norm_g, w_q_xa, w_k_xa, w_v_xa, w_o_xa, norm_ffn_g, w_gate_ffn, w_up_ffn, w_down_ffn, final_g, loss_target, m_norm_mix_g, m_w_in, m_sgu_ln_g, m_sgu_ln_b, m_w_spatial, m_b_spatial, m_conv_w, m_w_branch, m_w_out, m_norm_xa_g, m_mem_norm_g, m_w_q_xa, m_w_k_xa, m_w_v_xa, m_w_o_xa, m_norm_ffn_g, m_w_gate_ffn, m_w_up_ffn, m_w_down_ffn, m_final_g, v_norm_mix_g, v_w_in, v_sgu_ln_g, v_sgu_ln_b, v_w_spatial, v_b_spatial, v_conv_w, v_w_branch, v_w_out, v_norm_xa_g, v_mem_norm_g, v_w_q_xa, v_w_k_xa, v_w_v_xa, v_w_o_xa, v_norm_ffn_g, v_w_gate_ffn, v_w_up_ffn, v_w_down_ffn, v_final_g):
    w = dict(norm_mix_g=norm_mix_g, w_in=w_in, sgu_ln_g=sgu_ln_g, sgu_ln_b=sgu_ln_b, w_spatial=w_spatial,
             b_spatial=b_spatial, conv_w=conv_w, w_branch=w_branch, w_out=w_out, norm_xa_g=norm_xa_g,
             mem_norm_g=mem_norm_g, w_q_xa=w_q_xa, w_k_xa=w_k_xa, w_v_xa=w_v_xa, w_o_xa=w_o_xa,
             norm_ffn_g=norm_ffn_g, w_gate_ffn=w_gate_ffn, w_up_ffn=w_up_ffn, w_down_ffn=w_down_ffn, final_g=final_g)
    m = dict(norm_mix_g=m_norm_mix_g, w_in=m_w_in, sgu_ln_g=m_sgu_ln_g, sgu_ln_b=m_sgu_ln_b, w_spatial=m_w_spatial,
             b_spatial=m_b_spatial, conv_w=m_conv_w, w_branch=m_w_branch, w_out=m_w_out, norm_xa_g=m_norm_xa_g,
             mem_norm_g=m_mem_norm_g, w_q_xa=m_w_q_xa, w_k_xa=m_w_k_xa, w_v_xa=m_w_v_xa, w_o_xa=m_w_o_xa,
             norm_ffn_g=m_norm_ffn_g, w_gate_ffn=m_w_gate_ffn, w_up_ffn=m_w_up_ffn, w_down_ffn=m_w_down_ffn,
             final_g=m_final_g)
    v = dict(norm_mix_g=v_norm_mix_g, w_in=v_w_in, sgu_ln_g=v_sgu_ln_g, sgu_ln_b=v_sgu_ln_b, w_spatial=v_w_spatial,
             b_spatial=v_b_spatial, conv_w=v_conv_w, w_branch=v_w_branch, w_out=v_w_out, norm_xa_g=v_norm_xa_g,
             mem_norm_g=v_mem_norm_g, w_q_xa=v_w_q_xa, w_k_xa=v_w_k_xa, w_v_xa=v_w_v_xa, w_o_xa=v_w_o_xa,
             norm_ffn_g=v_norm_ffn_g, w_gate_ffn=v_w_gate_ffn, w_up_ffn=v_w_up_ffn, w_down_ffn=v_w_down_ffn,
             final_g=v_final_g)

    names = [n for n, _, _ in BIG]
    shards = {(n, l): _stored(n, w[n][l].astype(BF16)) for n in names for l in range(DEPTH)}
    first_items = [("w_in", 0)]
    mid_items = [(n, 0) for n in names if n != "w_in"] + [("w_in", 1)]
    last_items = [(n, 1) for n in names if n != "w_in"]
    wt = _unpack_gathered(first_items, _all_gather(_pack_shards(first_items, shards), name="gather_w_in0"))
    cw_pad = jnp.zeros((8, 128), F32).at[:DEPTH * 3, :BRANCH_W // N_DEV].set(conv_w.reshape(DEPTH * 3, -1))
    cw_all = _all_gather(cw_pad, name="gather_conv_w")[:, :DEPTH * 3, :BRANCH_W // N_DEV]
    conv_full = jnp.moveaxis(cw_all.reshape(N_DEV, DEPTH, 3, BRANCH_W // N_DEV), 0, 2).reshape(DEPTH, 3, BRANCH_W)
    sm = {n: w[n] for n, _ in SMALL}
    sm["conv_w"] = conv_full

    xs, ms = x[0], mem[0]
    x1, saved0 = _layer_fwd(0, xs, ms, wt, sm, gather=(mid_items, _pack_shards(mid_items, shards)))
    x2, saved1 = _layer_fwd(1, x1, ms, wt, sm, gather=(last_items, _pack_shards(last_items, shards)))
    dcur, loss, dfinal = _final_loss(x2, sm["final_g"][None], loss_target[0], name="final_loss")
    loss = lax.psum(loss[0, 0], AXES)
    items_a = [(n, 1) for n in names if n != "w_in"]
    items_b = [("w_in", 1)] + [(n, 0) for n in names if n != "w_in"]
    items_c = [("w_in", 0)]
    dcur, gb1, gs1, recv_a = _layer_bwd(1, dcur, ms, wt, sm, saved1, scatter=(items_a, {}))
    dx, gb0, gs0, recv_b = _layer_bwd(0, dcur, ms, wt, sm, saved0, scatter=(items_b, {("w_in", 1): gb1["w_in"]}))

    shard_grads = _unpack_shard(items_a, _sum_devices(recv_a, name="rs_sum_a"))
    shard_grads.update(_unpack_shard(items_b, _sum_devices(recv_b, name="rs_sum_b")))
    g8 = _pack_full(items_c, {("w_in", 0): gb0["w_in"]})
    core = lax.axis_index("c").astype(jnp.int32).reshape(1)
    from_sibling = _rs_pair_exchange(g8, name="rs_pair_exchange")
    part = _pair_add(core, g8, from_sibling, name="rs_pair_add")
    by_chip = _rs_chip_exchange(part, name="rs_chip_exchange")
    shard_grads.update(_unpack_shard(items_c, _sum_chips(by_chip, name="rs_sum_chips")))
    grads = {n: jnp.stack([_stored(n, shard_grads[n, l]) for l in range(DEPTH)]) for n in names}
    small = {n: jnp.stack([gs0[n], gs1[n]]) for n, _ in SMALL if n != "final_g"}
    small["final_g"] = dfinal[0]
    small_sum = _unpack_small(_all_reduce_small(_pack_small(small), name="all_reduce_small"))
    width = BRANCH_W // N_DEV
    dev = 4 * lax.axis_index("x") + 2 * lax.axis_index("y") + lax.axis_index("c")
    for n, _ in SMALL:
        grads[n] = small_sum[n]
    grads["conv_w"] = lax.dynamic_slice_in_dim(small_sum["conv_w"], dev * width, width, axis=2)

    delta, new_m, new_v = {}, {}, {}
    for n in _WEIGHTS:
        shp = w[n].shape
        two_d = (-1, shp[-1])
        d_, m_, v_ = _adamw(w[n].reshape(two_d), grads[n].reshape(two_d), m[n].reshape(two_d), v[n].reshape(two_d),
                            name="adamw_" + n)
        delta[n], new_m[n], new_v[n] = d_.reshape(shp), m_.reshape(shp), v_.reshape(shp)

    return (loss, dx[None], *[grads[n] for n in _WEIGHTS], *[delta[n] for n in _WEIGHTS],
            *[new_m[n] for n in _WEIGHTS], *[new_v[n] for n in _WEIGHTS])
```

```python
import functools

import jax
import jax.numpy as jnp
from jax import lax
from jax.experimental import pallas as pl
from jax.experimental.pallas import tpu as pltpu

F32 = jnp.float32
BF16 = jnp.bfloat16
MESH = pl.DeviceIdType.MESH

D_MODEL = 1024
BRANCH_W = 512
IN_COLS = 7168
FFN = 2816
N_DEV = 8
DEPTH = 2
SB_BLOCK = 128
SB_SPAN = 1024
SB_Q_FWD = 512
SB_Q_BWD = 512
SB_SCALE = 0.125
XA_HEAD = 256
XA_SCALE = 0.0625
SGU_LEN = 128
SGU_GROUPS = 4
RMS_EPS = 1e-6
LN_EPS = 1e-5
HALO = 8

ADAM_LR = 0.001
ADAM_B1 = 0.9
ADAM_B2 = 0.999
ADAM_EPS = 1e-08
ADAM_WD = 0.01
ADAM_STEP = 10

VMEM_LIMIT_BYTES = 52 * 1024 * 1024

AXES = ("x", "y", "c")


def _cparams(*sem):
    return pltpu.CompilerParams(dimension_semantics=sem, vmem_limit_bytes=VMEM_LIMIT_BYTES)


def _pick(n, target, align):
    t = (min(target, n) // align) * align
    while t >= align:
        if n % t == 0:
            return t
        t -= align
    return n


def _dot(a, b):
    return jnp.dot(a, b, preferred_element_type=F32)


def _dot_nt(a, b):
    return lax.dot_general(a, b, (((1,), (1,)), ((), ())), preferred_element_type=F32)


def _dot_tn(a, b):
    return lax.dot_general(a, b, (((0,), (0,)), ((), ())), preferred_element_type=F32)


def _sigmoid(x):
    return 1.0 / (1.0 + jnp.exp(-x))


def _mm(a, b, *, name, ta=False, tb=False, out_dtype=F32, add=None, rms=None, tm=1024, tn=1024, tk=2048):
    m, k = (a.shape[1], a.shape[0]) if ta else a.shape
    n = b.shape[0] if tb else b.shape[1]
    assert k == (b.shape[1] if tb else b.shape[0])
    tm = _pick(m, tm, 128)
    tn = n if rms is not None else _pick(n, tn, 128)
    tk = _pick(k, tk, 128)
    nk = k // tk
    ca = 0 if ta else 1
    cb = 1 if tb else 0
    n_add = 0 if add is None else 1
    n_rms = 0 if rms is None else 3

    def body(*refs):
        refs = list(refs)
        a_ref, b_ref = refs[:2]
        extra = refs[2:2 + n_add + n_rms]
        outs = refs[2 + n_add + n_rms:]
        o_ref = outs[0]
        kk = pl.program_id(2)
        first_row_tile = pl.program_id(0) == 0

        def product():
            return lax.dot_general(a_ref[...].astype(BF16), b_ref[...].astype(BF16),
                                   (((ca,), (cb,)), ((), ())), preferred_element_type=F32)

        def finish(r):
            if add is not None:
                r = r + extra[0][...]
            if rms is None:
                o_ref[...] = r.astype(out_dtype)
                return
            x_ref, g_ref, dres_ref = extra[n_add:]
            dg_ref = outs[1]

            @pl.when(first_row_tile)
            def _():
                dg_ref[...] = jnp.zeros_like(dg_ref)

            xv = x_ref[...]
            rs = lax.rsqrt(jnp.mean(xv * xv, axis=-1, keepdims=True) + RMS_EPS)
            xh = xv * rs
            dg_ref[...] += jnp.sum(r * xh, axis=0, keepdims=True)
            dxh = r * g_ref[...]
            o_ref[...] = dres_ref[...] + rs * (dxh - xh * jnp.mean(dxh * xh, axis=-1, keepdims=True))

        if nk == 1:
            finish(product())
        else:
            acc_ref = outs[-1]

            @pl.when(kk == 0)
            def _():
                acc_ref[...] = jnp.zeros_like(acc_ref)

            acc_ref[...] += product()

            @pl.when(kk == nk - 1)
            def _():
                finish(acc_ref[...])

    a_spec = pl.BlockSpec((tk, tm), lambda i, j, kk: (kk, i)) if ta else pl.BlockSpec((tm, tk), lambda i, j, kk: (i, kk))
    b_spec = pl.BlockSpec((tn, tk), lambda i, j, kk: (j, kk)) if tb else pl.BlockSpec((tk, tn), lambda i, j, kk: (kk, j))
    tile = pl.BlockSpec((tm, tn), lambda i, j, kk: (i, j))
    in_specs = [a_spec, b_spec]
    operands = [a, b]
    out_specs = [tile]
    out_shape = [jax.ShapeDtypeStruct((m, n), out_dtype)]
    if add is not None:
        in_specs.append(tile)
        operands.append(add)
    if rms is not None:
        vec = pl.BlockSpec((1, n), lambda i, j, kk: (0, 0))
        in_specs += [tile, vec, tile]
        operands += list(rms)
        out_specs.append(vec)
        out_shape = [jax.ShapeDtypeStruct((m, n), F32), jax.ShapeDtypeStruct((1, n), F32)]
    out = pl.pallas_call(
        body, name=name,
        grid=(m // tm, n // tn, nk),
        in_specs=in_specs, out_specs=out_specs, out_shape=out_shape,
        scratch_shapes=[pltpu.VMEM((tm, tn), F32)] if nk > 1 else [],
        compiler_params=_cparams("arbitrary" if rms is not None else "parallel", "parallel", "arbitrary"),
    )(*operands)
    return out[0] if rms is None else out


def _rms_fwd(x, g, *, name):
    r, d = x.shape
    tr = _pick(r, 512, 16)

    def body(x_ref, g_ref, o_ref):
        xv = x_ref[...]
        rs = lax.rsqrt(jnp.mean(xv * xv, axis=-1, keepdims=True) + RMS_EPS)
        o_ref[...] = (xv * rs * g_ref[...]).astype(BF16)

    return pl.pallas_call(
        body, name=name, grid=(r // tr,),
        in_specs=[pl.BlockSpec((tr, d), lambda i: (i, 0)), pl.BlockSpec((1, d), lambda i: (0, 0))],
        out_specs=pl.BlockSpec((tr, d), lambda i: (i, 0)),
        out_shape=jax.ShapeDtypeStruct((r, d), BF16),
        compiler_params=_cparams("parallel"),
    )(x, g)


def _rms_bwd(x, g, dh, dres, *, name):
    r, d = x.shape
    tr = _pick(r, 256, 8)

    def body(x_ref, g_ref, dh_ref, dres_ref, dx_ref, dg_ref):
        @pl.when(pl.program_id(0) == 0)
        def _():
            dg_ref[...] = jnp.zeros_like(dg_ref)

        xv = x_ref[...]
        dhv = dh_ref[...].astype(F32)
        rs = lax.rsqrt(jnp.mean(xv * xv, axis=-1, keepdims=True) + RMS_EPS)
        xh = xv * rs
        dg_ref[...] += jnp.sum(dhv * xh, axis=0, keepdims=True)
        dxh = dhv * g_ref[...]
        dx_ref[...] = dres_ref[...] + rs * (dxh - xh * jnp.mean(dxh * xh, axis=-1, keepdims=True))

    return pl.pallas_call(
        body, name=name, grid=(r // tr,),
        in_specs=[pl.BlockSpec((tr, d), lambda i: (i, 0)), pl.BlockSpec((1, d), lambda i: (0, 0)),
                  pl.BlockSpec((tr, d), lambda i: (i, 0)), pl.BlockSpec((tr, d), lambda i: (i, 0))],
        out_specs=[pl.BlockSpec((tr, d), lambda i: (i, 0)), pl.BlockSpec((1, d), lambda i: (0, 0))],
        out_shape=[jax.ShapeDtypeStruct((r, d), F32), jax.ShapeDtypeStruct((1, d), F32)],
        compiler_params=_cparams("arbitrary"),
    )(x, g, dh, dres)


def _final_loss(x, g, target, *, name):
    r, d = x.shape
    tr = _pick(r, 512, 8)

    def body(x_ref, g_ref, t_ref, dx_ref, loss_ref, dg_ref):
        @pl.when(pl.program_id(0) == 0)
        def _():
            dg_ref[...] = jnp.zeros_like(dg_ref)
            loss_ref[...] = jnp.zeros_like(loss_ref)

        xv = x_ref[...]
        gv = g_ref[...]
        rs = lax.rsqrt(jnp.mean(xv * xv, axis=-1, keepdims=True) + RMS_EPS)
        xh = xv * rs
        err = xh * gv - t_ref[...]
        row_loss = jnp.mean(err * err, axis=-1, keepdims=True)
        loss_ref[...] += 0.5 * jnp.sum(row_loss, axis=0, keepdims=True)
        dy = err * (1.0 / d)
        dg_ref[...] += jnp.sum(dy * xh, axis=0, keepdims=True)
        dxh = dy * gv
        dx_ref[...] = rs * (dxh - xh * jnp.mean(dxh * xh, axis=-1, keepdims=True))

    return pl.pallas_call(
        body, name=name, grid=(r // tr,),
        in_specs=[pl.BlockSpec((tr, d), lambda i: (i, 0)), pl.BlockSpec((1, d), lambda i: (0, 0)),
                  pl.BlockSpec((tr, d), lambda i: (i, 0))],
        out_specs=[pl.BlockSpec((tr, d), lambda i: (i, 0)), pl.BlockSpec((1, 128), lambda i: (0, 0)),
                   pl.BlockSpec((1, d), lambda i: (0, 0))],
        out_shape=[jax.ShapeDtypeStruct((r, d), F32), jax.ShapeDtypeStruct((1, 128), F32),
                   jax.ShapeDtypeStruct((1, d), F32)],
        compiler_params=_cparams("arbitrary"),
    )(x, g, target)


def _cumsum_operand(strict_after, totals=True):
    width = (2 if totals else 1) * SB_BLOCK
    r = lax.broadcasted_iota(jnp.int32, (SB_BLOCK, width), 0)
    c = lax.broadcasted_iota(jnp.int32, (SB_BLOCK, width), 1)
    tri = (r > c) if strict_after else (r < c)
    return jnp.where((c >= SB_BLOCK) | tri, 1.0, 0.0).astype(BF16)


def _sb_scores(qh, kw, run, valid, after_ones):
    nb = kw.shape[0] // SB_BLOCK
    z = _dot_nt(qh, kw)
    lsp = jnp.minimum(z, 0.0) - jnp.log(1.0 + jnp.exp(-jnp.abs(z)))
    l1m = lsp - z
    if valid is not None:
        l1m = jnp.where(valid, l1m, 0.0)
    l1b = l1m.astype(BF16)
    later = [None] * nb
    seen = [None] * nb
    for b in reversed(range(nb)):
        cols = slice(b * SB_BLOCK, (b + 1) * SB_BLOCK)
        ct = _dot(l1b[:, cols], after_ones)
        seen[b] = run
        later[b] = run + ct[:, :SB_BLOCK]
        run = run + ct[:, SB_BLOCK:]
    a = jnp.exp(lsp + jnp.concatenate(later, axis=1))
    if valid is not None:
        a = jnp.where(valid, a, 0.0)
    return a, run, seen


def _sb_setup(q_ref, span):
    qi = pl.program_id(1)
    rows = q_ref.shape[0]
    sd = (qi * rows + rows - 1) // span
    lane = lax.broadcasted_iota(jnp.int32, (rows, SB_BLOCK), 1)
    col = lax.broadcasted_iota(jnp.int32, (rows, span), 1)
    row = lax.broadcasted_iota(jnp.int32, (rows, span), 0)
    valid = col < (qi * rows - sd * span) + row
    q = q_ref[...] * SB_SCALE
    qhs = (jnp.where(lane < 64, q, 0.0).astype(BF16), jnp.where(lane >= 64, q, 0.0).astype(BF16))
    return lane, sd, valid, qhs


def _sb_fwd(p, *, name, gather=None):
    s = p.shape[0]
    qrows = min(SB_Q_FWD, s)
    nq = s // qrows
    kcol = BRANCH_W // SB_BLOCK
    span = min(SB_SPAN, s)
    per = span // SB_BLOCK
    assert s // SB_BLOCK <= SB_BLOCK

    def body(*refs):
        if gather is None:
            q_ref, k_ref, v_ref, o_ref, r0_ref, r1_ref = refs
        else:
            q_ref, k_ref, v_ref, x_ref, o_ref, r0_ref, r1_ref, g_ref, send_sems, recv_sems, local_sem = refs
            start, forward, finish = _gather_phases(x_ref, g_ref, send_sems, recv_sems, local_sem)
            step = pl.program_id(0) * nq + pl.program_id(1)
            pl.when(step == 0)(start)
        lane, sd, valid, qhs = _sb_setup(q_ref, span)
        after_ones = _cumsum_operand(True)
        zero = jnp.zeros((qrows, SB_BLOCK), F32)
        lane_row = lax.broadcasted_iota(jnp.int32, (1, SB_BLOCK), 1)

        def span_step(sb, carry, mask):
            rows = pl.ds(pl.multiple_of(sb * span, span), span)
            kw = k_ref[rows, :].astype(BF16)
            vw = v_ref[rows, :].astype(BF16)
            out = []
            for h in range(2):
                run, acc, table = carry[h]
                a, run, seen = _sb_scores(qhs[h], kw, run, mask, after_ones)
                for b in range(per):
                    table = jnp.where(lane_row == sb * per + b, seen[b], table)
                out.append((run, acc + _dot(a.astype(BF16), vw), table))
            return tuple(out)

        carry = span_step(sd, ((zero, zero, zero), (zero, zero, zero)), valid)
        carry = lax.fori_loop(0, sd, lambda t, c: span_step(sd - 1 - t, c, None), carry)
        o_ref[...] = jnp.where(lane < 64, carry[0][1], carry[1][1]).astype(BF16)
        r0_ref[...] = carry[0][2]
        r1_ref[...] = carry[1][2]
        if gather is not None:
            pl.when(step == (kcol - 1) * nq + (3 * nq) // 4)(forward)
            pl.when(step == kcol * nq - 1)(finish)

    in_specs = [pl.BlockSpec((qrows, SB_BLOCK), lambda hp, qi: (qi, hp)),
                pl.BlockSpec((s, SB_BLOCK), lambda hp, qi: (0, kcol + hp)),
                pl.BlockSpec((s, SB_BLOCK), lambda hp, qi: (0, 2 * kcol + hp))]
    table = pl.BlockSpec((None, qrows, SB_BLOCK), lambda hp, qi: (hp, qi, 0))
    out_specs = [pl.BlockSpec((qrows, SB_BLOCK), lambda hp, qi: (qi, hp)), table, table]
    out_shape = [jax.ShapeDtypeStruct((s, BRANCH_W), BF16)] + [jax.ShapeDtypeStruct((kcol, s, SB_BLOCK), F32)] * 2
    operands = [p, p, p]
    scratch = []
    if gather is not None:
        in_specs.append(pl.BlockSpec(memory_space=pl.ANY))
        out_specs.append(pl.BlockSpec(memory_space=pl.ANY))
        out_shape.append(jax.ShapeDtypeStruct((N_DEV,) + gather.shape, gather.dtype))
        operands.append(gather)
        scratch = _GATHER_SEMS
    out = pl.pallas_call(
        body, name=name, grid=(kcol, nq), in_specs=in_specs, out_specs=out_specs, out_shape=out_shape,
        scratch_shapes=scratch, compiler_params=_cparams("arbitrary", "arbitrary"),
    )(*operands)
    return out


def _sb_bwd(p, dya, tables, *, name, scatter=None):
    s = p.shape[0]
    qrows = min(SB_Q_BWD, s)
    nq = s // qrows
    kcol = BRANCH_W // SB_BLOCK
    span = min(SB_SPAN, s)
    per = span // SB_BLOCK

    def body(*refs):
        if scatter is None:
            q_ref, k_ref, v_ref, do_ref, t0_ref, t1_ref, dq_ref, dk_ref, dv_ref, dk_acc, dv_acc = refs
        else:
            (q_ref, k_ref, v_ref, do_ref, t0_ref, t1_ref, g_ref, dq_ref, dk_ref, dv_ref, r_ref,
             dk_acc, dv_acc, send_sems, recv_sems, local_sem) = refs
            start, finish = _scatter_phases(g_ref, r_ref, send_sems, recv_sems, local_sem)
            step = pl.program_id(0) * nq + pl.program_id(1)
            pl.when(step == 0)(start)
        qi = pl.program_id(1)

        @pl.when(qi == 0)
        def _():
            dk_acc[...] = jnp.zeros_like(dk_acc)
            dv_acc[...] = jnp.zeros_like(dv_acc)

        lane, sd, valid, qhs = _sb_setup(q_ref, span)
        after = _cumsum_operand(True, totals=False)
        before_ones = _cumsum_operand(False)
        do = do_ref[...]
        dohs = (jnp.where(lane < 64, do, 0.0).astype(BF16), jnp.where(lane >= 64, do, 0.0).astype(BF16))
        tabs = (t0_ref[...], t1_ref[...])
        lane_row = lax.broadcasted_iota(jnp.int32, (1, SB_BLOCK), 1)
        zero = jnp.zeros((qrows, SB_BLOCK), F32)

        def span_step(sb, carry, mask):
            rows = pl.ds(pl.multiple_of(sb * span, span), span)
            kw = k_ref[rows, :].astype(BF16)
            vw = v_ref[rows, :].astype(BF16)
            out = []
            dk_span = jnp.zeros((span, SB_BLOCK), F32)
            dv_span = jnp.zeros((span, SB_BLOCK), F32)
            for h in range(2):
                pg, dq = carry[h]
                z = _dot_nt(qhs[h], kw)
                lsp = jnp.minimum(z, 0.0) - jnp.log(1.0 + jnp.exp(-jnp.abs(z)))
                l1m = lsp - z
                if mask is not None:
                    l1m = jnp.where(mask, l1m, 0.0)
                l1b = l1m.astype(BF16)
                later = [None] * per
                for b in range(per):
                    cols = slice(b * SB_BLOCK, (b + 1) * SB_BLOCK)
                    seen = jnp.sum(jnp.where(lane_row == sb * per + b, tabs[h], 0.0), axis=-1, keepdims=True)
                    later[b] = seen + _dot(l1b[:, cols], after)
                a = jnp.exp(lsp + jnp.concatenate(later, axis=1))
                beta = jnp.exp(lsp)
                if mask is not None:
                    a = jnp.where(mask, a, 0.0)
                    beta = jnp.where(mask, beta, 0.0)
                g = a * _dot_nt(dohs[h], vw)
                gb = g.astype(BF16)
                before = [None] * per
                for b in range(per):
                    cols = slice(b * SB_BLOCK, (b + 1) * SB_BLOCK)
                    gt = _dot(gb[:, cols], before_ones)
                    before[b] = pg + gt[:, :SB_BLOCK]
                    pg = pg + gt[:, SB_BLOCK:]
                dz = (g * (1.0 - beta) - beta * jnp.concatenate(before, axis=1)).astype(BF16)
                dk_span = dk_span + _dot_tn(dz, qhs[h])
                dv_span = dv_span + _dot_tn(a.astype(BF16), dohs[h])
                out.append((pg, dq + _dot(dz, kw)))
            dk_acc[rows, :] += dk_span
            dv_acc[rows, :] += dv_span
            return tuple(out)

        carry = lax.fori_loop(0, sd, lambda sb, c: span_step(sb, c, None), ((zero, zero), (zero, zero)))
        carry = span_step(sd, carry, valid)
        dq_ref[...] = (jnp.where(lane < 64, carry[0][1], carry[1][1]) * SB_SCALE).astype(BF16)

        @pl.when(qi == nq - 1)
        def _():
            dk_ref[...] = dk_acc[...].astype(BF16)
            dv_ref[...] = dv_acc[...].astype(BF16)

        if scatter is not None:
            pl.when(step == kcol * nq - 1)(finish)

    blk = pl.BlockSpec((qrows, SB_BLOCK), lambda hp, qi: (qi, hp))
    col = pl.BlockSpec((s, SB_BLOCK), lambda hp, qi: (0, hp))
    table = pl.BlockSpec((None, qrows, SB_BLOCK), lambda hp, qi: (hp, qi, 0))
    out = jax.ShapeDtypeStruct((s, BRANCH_W), BF16)
    in_specs = [blk,
                pl.BlockSpec((s, SB_BLOCK), lambda hp, qi: (0, kcol + hp)),
                pl.BlockSpec((s, SB_BLOCK), lambda hp, qi: (0, 2 * kcol + hp)),
                blk, table, table]
    out_specs = [blk, col, col]
    out_shape = [out, out, out]
    operands = [p, p, p, dya, tables[0], tables[1]]
    scratch = [pltpu.VMEM((s, SB_BLOCK), F32), pltpu.VMEM((s, SB_BLOCK), F32)]
    if scatter is not None:
        in_specs.append(pl.BlockSpec(memory_space=pl.ANY))
        out_specs.append(pl.BlockSpec(memory_space=pl.ANY))
        out_shape.append(jax.ShapeDtypeStruct(scatter.shape, scatter.dtype))
        operands.append(scatter)
        scratch = scratch + _SCATTER_SEMS
    return pl.pallas_call(
        body, name=name, grid=(kcol, nq), in_specs=in_specs, out_specs=out_specs, out_shape=out_shape,
        scratch_shapes=scratch, compiler_params=_cparams("arbitrary", "arbitrary"),
    )(*operands)


_INV_SQRT2 = 0.7071067811865476
_INV_SQRT2PI = 0.3989422804014327


def _gelu(x):
    return 0.5 * x * (1.0 + lax.erf(x * _INV_SQRT2))


def _gelu_grad(x):
    return 0.5 * (1.0 + lax.erf(x * _INV_SQRT2)) + x * _INV_SQRT2PI * jnp.exp(-0.5 * x * x)


def _chunk_mask(transposed=False):
    r = lax.broadcasted_iota(jnp.int32, (SGU_LEN, SGU_LEN), 0)
    c = lax.broadcasted_iota(jnp.int32, (SGU_LEN, SGU_LEN), 1)
    return (c // 64) >= (r // 64) if transposed else (r // 64) >= (c // 64)


def _sgu_norm(v_raw, g, b):
    zv = _gelu(v_raw)
    xc = zv - jnp.mean(zv, axis=-1, keepdims=True)
    rs = lax.rsqrt(jnp.mean(xc * xc, axis=-1, keepdims=True) + LN_EPS)
    xh = xc * rs
    return xh, rs, xh * g + b


def _sgu_fwd(p, ln_g, ln_b, w, b_col, *, name):
    s = p.shape[0]
    tr = _pick(s, 512, SGU_LEN)

    def body(u_ref, v_ref, g_ref, b_ref, w_ref, bc_ref, o_ref):
        mask = _chunk_mask()
        zu = _gelu(u_ref[...])
        _, _, vn = _sgu_norm(v_ref[...], g_ref[...], b_ref[...])
        vnb = vn.astype(BF16)
        for gi in range(SGU_GROUPS):
            wg = jnp.where(mask, w_ref[gi], 0.0).astype(BF16)
            cs = slice(gi * SGU_LEN, (gi + 1) * SGU_LEN)
            for c in range(tr // SGU_LEN):
                rs_ = slice(c * SGU_LEN, (c + 1) * SGU_LEN)
                vm = _dot(wg, vnb[rs_, cs]) + bc_ref[gi]
                o_ref[rs_, cs] = (zu[rs_, cs] * vm).astype(BF16)

    vec = pl.BlockSpec((1, BRANCH_W), lambda i: (0, 0))
    return pl.pallas_call(
        body, name=name, grid=(s // tr,),
        in_specs=[pl.BlockSpec((tr, BRANCH_W), lambda i: (i, 3)), pl.BlockSpec((tr, BRANCH_W), lambda i: (i, 4)),
                  vec, vec,
                  pl.BlockSpec((SGU_GROUPS, SGU_LEN, SGU_LEN), lambda i: (0, 0, 0)),
                  pl.BlockSpec((SGU_GROUPS, SGU_LEN, 1), lambda i: (0, 0, 0))],
        out_specs=pl.BlockSpec((tr, BRANCH_W), lambda i: (i, 0)),
        out_shape=jax.ShapeDtypeStruct((s, BRANCH_W), BF16),
        compiler_params=_cparams("parallel"),
    )(p, p, ln_g, ln_b, w, b_col)


def _sgu_bwd(p, dyb, ln_g, ln_b, w, w_t, b_col, *, name):
    s = p.shape[0]
    tr = _pick(s, 512, SGU_LEN)

    def body(u_ref, v_ref, dy_ref, g_ref, b_ref, w_ref, wt_ref, bc_ref,
             dz_ref, dg_ref, db_ref, dw_ref, dbc_ref, dvn_s):
        @pl.when(pl.program_id(0) == 0)
        def _():
            dg_ref[...] = jnp.zeros_like(dg_ref)
            db_ref[...] = jnp.zeros_like(db_ref)
            dw_ref[...] = jnp.zeros_like(dw_ref)
            dbc_ref[...] = jnp.zeros_like(dbc_ref)

        mask = _chunk_mask()
        mask_t = _chunk_mask(transposed=True)
        u_raw = u_ref[...]
        v_raw = v_ref[...]
        dy = dy_ref[...]
        zu = _gelu(u_raw)
        xh, rs, vn = _sgu_norm(v_raw, g_ref[...], b_ref[...])
        vnb = vn.astype(BF16)
        dvm_all = dy * zu
        for gi in range(SGU_GROUPS):
            wg = jnp.where(mask, w_ref[gi], 0.0).astype(BF16)
            wgt = jnp.where(mask_t, wt_ref[gi], 0.0).astype(BF16)
            cs = slice(gi * SGU_LEN, (gi + 1) * SGU_LEN)
            dw_g = jnp.zeros((SGU_LEN, SGU_LEN), F32)
            db_g = jnp.zeros((SGU_LEN, 1), F32)
            for c in range(tr // SGU_LEN):
                rs_ = slice(c * SGU_LEN, (c + 1) * SGU_LEN)
                vm = _dot(wg, vnb[rs_, cs]) + bc_ref[gi]
                dz_ref[rs_, cs] = (dy[rs_, cs] * vm * _gelu_grad(u_raw[rs_, cs])).astype(BF16)
                dvm = dvm_all[rs_, cs]
                dvmb = dvm.astype(BF16)
                dw_g = dw_g + _dot_nt(dvmb, vnb[rs_, cs])
                db_g = db_g + jnp.sum(dvm, axis=1, keepdims=True)
                dvn_s[rs_, cs] = _dot(wgt, dvmb)
            dw_ref[gi] += jnp.where(mask, dw_g, 0.0)
            dbc_ref[gi] += db_g
        dvn = dvn_s[...]
        dg_ref[...] += jnp.sum(dvn * xh, axis=0, keepdims=True)
        db_ref[...] += jnp.sum(dvn, axis=0, keepdims=True)
        dxh = dvn * g_ref[...]
        dzv = rs * (dxh - jnp.mean(dxh, axis=-1, keepdims=True) - xh * jnp.mean(dxh * xh, axis=-1, keepdims=True))
        dz_ref[:, BRANCH_W:] = (dzv * _gelu_grad(v_raw)).astype(BF16)

    vec = pl.BlockSpec((1, BRANCH_W), lambda i: (0, 0))
    wspec = pl.BlockSpec((SGU_GROUPS, SGU_LEN, SGU_LEN), lambda i: (0, 0, 0))
    bspec = pl.BlockSpec((SGU_GROUPS, SGU_LEN, 1), lambda i: (0, 0, 0))
    return pl.pallas_call(
        body, name=name, grid=(s // tr,),
        in_specs=[pl.BlockSpec((tr, BRANCH_W), lambda i: (i, 3)), pl.BlockSpec((tr, BRANCH_W), lambda i: (i, 4)),
                  pl.BlockSpec((tr, BRANCH_W), lambda i: (i, 0)), vec, vec, wspec, wspec, bspec],
        out_specs=[pl.BlockSpec((tr, 2 * BRANCH_W), lambda i: (i, 0)), vec, vec, wspec, bspec],
        out_shape=[jax.ShapeDtypeStruct((s, 2 * BRANCH_W), BF16),
                   jax.ShapeDtypeStruct((1, BRANCH_W), F32), jax.ShapeDtypeStruct((1, BRANCH_W), F32),
                   jax.ShapeDtypeStruct((SGU_GROUPS, SGU_LEN, SGU_LEN), F32),
                   jax.ShapeDtypeStruct((SGU_GROUPS, SGU_LEN, 1), F32)],
        scratch_shapes=[pltpu.VMEM((tr, BRANCH_W), F32)],
        compiler_params=_cparams("arbitrary"),
    )(p, p, dyb, ln_g, ln_b, w, w_t, b_col)


def _shift_down(x, prev8, k):
    rolled = pltpu.roll(x, k, 0)
    r8 = lax.broadcasted_iota(jnp.int32, prev8.shape, 0)
    head = jnp.where(r8 < k, pltpu.roll(prev8, k, 0), rolled[:HALO])
    return jnp.concatenate([head, rolled[HALO:]], axis=0)


def _shift_up(x, next8, k):
    n = x.shape[0]
    rolled = pltpu.roll(x, n - k, 0)
    r8 = lax.broadcasted_iota(jnp.int32, next8.shape, 0)
    tail = jnp.where(r8 >= HALO - k, pltpu.roll(next8, HALO - k, 0), rolled[n - HALO:])
    return jnp.concatenate([rolled[:n - HALO], tail], axis=0)


def _conv_specs(s, tr):
    nb = tr // HALO
    last = s // HALO - 1
    tile = lambda cb: pl.BlockSpec((tr, 128), lambda j, i: (i, cb * 4 + j))
    above = lambda cb: pl.BlockSpec((HALO, 128), lambda j, i: (jnp.maximum(i * nb - 1, 0), cb * 4 + j))
    below = lambda cb: pl.BlockSpec((HALO, 128), lambda j, i: (jnp.minimum((i + 1) * nb, last), cb * 4 + j))
    return tile, above, below


def _conv_fwd(p, cw, *, name):
    s = p.shape[0]
    tr = _pick(s, 1024, HALO)
    tile, above, _ = _conv_specs(s, tr)

    def body(cb_ref, cc_ref, cx_ref, ccp_ref, cxp_ref, w_ref, o_ref):
        first = pl.program_id(1) == 0
        y = cc_ref[...] * cx_ref[...]
        yp = jnp.where(first, 0.0, ccp_ref[...] * cxp_ref[...])
        conv = w_ref[2:3, :] * y + w_ref[1:2, :] * _shift_down(y, yp, 1) + w_ref[0:1, :] * _shift_down(y, yp, 2)
        o_ref[...] = (cb_ref[...] * conv).astype(BF16)

    return pl.pallas_call(
        body, name=name, grid=(4, s // tr),
        in_specs=[tile(5), tile(6), tile(7), above(6), above(7), pl.BlockSpec((3, 128), lambda j, i: (0, j))],
        out_specs=pl.BlockSpec((tr, 128), lambda j, i: (i, j)),
        out_shape=jax.ShapeDtypeStruct((s, BRANCH_W), BF16),
        compiler_params=_cparams("parallel", "parallel"),
    )(p, p, p, p, p, cw)


def _conv_bwd(p, dyc, cw, *, name):
    s = p.shape[0]
    tr = _pick(s, 1024, HALO)
    nt = s // tr
    nb = tr // HALO
    last = s // HALO - 1
    tile, above, below = _conv_specs(s, tr)

    def body(cb_ref, cc_ref, cx_ref, ccp_ref, cxp_ref, cbn_ref, dy_ref, dyn_ref, w_ref,
             dcb_ref, dcc_ref, dcx_ref, dw_ref):
        i = pl.program_id(1)

        @pl.when(i == 0)
        def _():
            dw_ref[...] = jnp.zeros_like(dw_ref)

        cb = cb_ref[...]
        cc = cc_ref[...]
        cx = cx_ref[...]
        y = cc * cx
        yp = jnp.where(i == 0, 0.0, ccp_ref[...] * cxp_ref[...])
        y1 = _shift_down(y, yp, 1)
        y2 = _shift_down(y, yp, 2)
        w0, w1, w2 = w_ref[0:1, :], w_ref[1:2, :], w_ref[2:3, :]
        conv = w2 * y + w1 * y1 + w0 * y2
        dyc_v = dy_ref[...]
        dconv = dyc_v * cb
        dn = jnp.where(i == nt - 1, 0.0, dyn_ref[...] * cbn_ref[...])
        dyv = w2 * dconv + w1 * _shift_up(dconv, dn, 1) + w0 * _shift_up(dconv, dn, 2)
        dcb_ref[...] = (dyc_v * conv).astype(BF16)
        dcc_ref[...] = (dyv * cx).astype(BF16)
        dcx_ref[...] = (dyv * cc).astype(BF16)
        dw_ref[0:1, :] += jnp.sum(dconv * y2, axis=0, keepdims=True)
        dw_ref[1:2, :] += jnp.sum(dconv * y1, axis=0, keepdims=True)
        dw_ref[2:3, :] += jnp.sum(dconv * y, axis=0, keepdims=True)

    dy_tile = pl.BlockSpec((tr, 128), lambda j, i: (i, j))
    dy_below = pl.BlockSpec((HALO, 128), lambda j, i: (jnp.minimum((i + 1) * nb, last), j))
    out_tile = lambda cb: pl.BlockSpec((tr, 128), lambda j, i: (i, cb * 4 + j))
    w_spec = pl.BlockSpec((3, 128), lambda j, i: (0, j))
    dcb, dcc, dcx, dw = pl.pallas_call(
        body, name=name, grid=(4, nt),
        in_specs=[tile(5), tile(6), tile(7), above(6), above(7), below(5), dy_tile, dy_below, w_spec],
        out_specs=[dy_tile, dy_tile, dy_tile, w_spec],
        out_shape=[jax.ShapeDtypeStruct((s, BRANCH_W), BF16)] * 3 + [jax.ShapeDtypeStruct((3, BRANCH_W), F32)],
        compiler_params=_cparams("parallel", "arbitrary"),
    )(p, p, p, p, p, p, dyc, dyc, cw)
    return dcb, dcc, dcx, dw


def _merge_fwd(ya, yb, yc, wb, p, *, name):
    s = p.shape[0]
    tr = _pick(s, 512, 16)

    def body(ya_ref, yb_ref, yc_ref, wb_ref, g0_ref, g1_ref, g2_ref, o_ref):
        acc = jnp.zeros((tr, D_MODEL), F32)
        for n, (y_ref, g_ref) in enumerate(((ya_ref, g0_ref), (yb_ref, g1_ref), (yc_ref, g2_ref))):
            acc = acc + _sigmoid(g_ref[...]) * _dot(y_ref[...].astype(BF16), wb_ref[n])
        o_ref[...] = acc.astype(BF16)

    yspec = pl.BlockSpec((tr, BRANCH_W), lambda i: (i, 0))
    gate = lambda n: pl.BlockSpec((tr, D_MODEL), lambda i: (i, 4 + n))
    return pl.pallas_call(
        body, name=name, grid=(s // tr,),
        in_specs=[yspec, yspec, yspec, pl.BlockSpec((3, BRANCH_W, D_MODEL), lambda i: (0, 0, 0)),
                  gate(0), gate(1), gate(2)],
        out_specs=pl.BlockSpec((tr, D_MODEL), lambda i: (i, 0)),
        out_shape=jax.ShapeDtypeStruct((s, D_MODEL), BF16),
        compiler_params=_cparams("parallel"),
    )(ya, yb, yc, wb, p, p, p)


def _merge_bwd(dm, ya, yb, yc, wb, p, *, name):
    s = p.shape[0]
    tr = _pick(s, 256, 16)

    def body(dm_ref, ya_ref, yb_ref, yc_ref, wb_ref, g0_ref, g1_ref, g2_ref,
             dya_ref, dyb_ref, dyc_ref, dg_ref, dbrd0_ref, dbrd1_ref, dbrd2_ref):
        dmv = dm_ref[...]
        ys = (ya_ref, yb_ref, yc_ref)
        gs = (g0_ref, g1_ref, g2_ref)
        dys = (dya_ref, dyb_ref, dyc_ref)
        dbrds = (dbrd0_ref, dbrd1_ref, dbrd2_ref)
        for n in range(3):
            brd = _dot(ys[n][...].astype(BF16), wb_ref[n])
            sg = _sigmoid(gs[n][...])
            dbrd = (sg * dmv).astype(BF16)
            dbrds[n][...] = dbrd
            dg_ref[:, n * D_MODEL:(n + 1) * D_MODEL] = (dmv * brd * sg * (1.0 - sg)).astype(BF16)
            dys[n][...] = _dot_nt(dbrd, wb_ref[n]).astype(dys[n].dtype)

    yspec = pl.BlockSpec((tr, BRANCH_W), lambda i: (i, 0))
    gate = lambda n: pl.BlockSpec((tr, D_MODEL), lambda i: (i, 4 + n))
    row = pl.BlockSpec((tr, D_MODEL), lambda i: (i, 0))
    return pl.pallas_call(
        body, name=name, grid=(s // tr,),
        in_specs=[row, yspec, yspec, yspec, pl.BlockSpec((3, BRANCH_W, D_MODEL), lambda i: (0, 0, 0)),
                  gate(0), gate(1), gate(2)],
        out_specs=[yspec, yspec, yspec, pl.BlockSpec((tr, 3 * D_MODEL), lambda i: (i, 0)), row, row, row],
        out_shape=[jax.ShapeDtypeStruct((s, BRANCH_W), BF16)] + [jax.ShapeDtypeStruct((s, BRANCH_W), F32)] * 2
                  + [jax.ShapeDtypeStruct((s, 3 * D_MODEL), BF16)] + [jax.ShapeDtypeStruct((s, D_MODEL), BF16)] * 3,
        compiler_params=_cparams("parallel"),
    )(dm, ya, yb, yc, wb, p, p, p)


def _xa_probs(q, k):
    sc = _dot_nt(q, k) * XA_SCALE
    e = jnp.exp(sc - jnp.max(sc, axis=-1, keepdims=True))
    return e / jnp.sum(e, axis=-1, keepdims=True)


def _xa_fwd(q, k, v, *, name):
    s = q.shape[0]
    mt = k.shape[0]
    tr = _pick(s, 2048, 16)

    def body(q_ref, k_ref, v_ref, o_ref):
        pr = _xa_probs(q_ref[...], k_ref[...])
        o_ref[...] = _dot(pr.astype(BF16), v_ref[...]).astype(BF16)

    qs = pl.BlockSpec((tr, XA_HEAD), lambda h, i: (i, h))
    ks = pl.BlockSpec((mt, XA_HEAD), lambda h, i: (0, h))
    return pl.pallas_call(
        body, name=name, grid=(D_MODEL // XA_HEAD, s // tr),
        in_specs=[qs, ks, ks], out_specs=qs,
        out_shape=jax.ShapeDtypeStruct((s, D_MODEL), BF16),
        compiler_params=_cparams("parallel", "parallel"),
    )(q, k, v)


def _xa_bwd(q, k, v, do, *, name):
    s = q.shape[0]
    mt = k.shape[0]
    tr = _pick(s, 2048, 16)

    def body(q_ref, k_ref, v_ref, do_ref, dq_ref, dk_ref, dv_ref):
        @pl.when(pl.program_id(1) == 0)
        def _():
            dk_ref[...] = jnp.zeros_like(dk_ref)
            dv_ref[...] = jnp.zeros_like(dv_ref)

        qv = q_ref[...]
        kv = k_ref[...]
        dov = do_ref[...]
        pr = _xa_probs(qv, kv)
        dpr = _dot_nt(dov, v_ref[...])
        ds = (pr * (dpr - jnp.sum(dpr * pr, axis=-1, keepdims=True)) * XA_SCALE).astype(BF16)
        dq_ref[...] = _dot(ds, kv).astype(BF16)
        dk_ref[...] += _dot_tn(ds, qv)
        dv_ref[...] += _dot_tn(pr.astype(BF16), dov)

    qs = pl.BlockSpec((tr, XA_HEAD), lambda h, i: (i, h))
    ks = pl.BlockSpec((mt, XA_HEAD), lambda h, i: (0, h))
    return pl.pallas_call(
        body, name=name, grid=(D_MODEL // XA_HEAD, s // tr),
        in_specs=[qs, ks, ks, qs], out_specs=[qs, ks, ks],
        out_shape=[jax.ShapeDtypeStruct((s, D_MODEL), BF16), jax.ShapeDtypeStruct((mt, D_MODEL), F32),
                   jax.ShapeDtypeStruct((mt, D_MODEL), F32)],
        compiler_params=_cparams("parallel", "arbitrary"),
    )(q, k, v, do)


def _ffn_in(h, wg, wu, *, name):
    s, d = h.shape
    f = wg.shape[0]
    tm = _pick(s, 1024, 128)
    tn = _pick(f, 1408, 128)

    def body(h_ref, wg_ref, wu_ref, a_ref, b_ref, o_ref):
        hv = h_ref[...]
        av = _dot_nt(hv, wg_ref[...])
        bv = _dot_nt(hv, wu_ref[...])
        a_ref[...] = av.astype(BF16)
        b_ref[...] = bv.astype(BF16)
        o_ref[...] = (av * _sigmoid(av) * bv).astype(BF16)

    wspec = pl.BlockSpec((tn, d), lambda i, j: (j, 0))
    tile = pl.BlockSpec((tm, tn), lambda i, j: (i, j))
    return pl.pallas_call(
        body, name=name, grid=(s // tm, f // tn),
        in_specs=[pl.BlockSpec((tm, d), lambda i, j: (i, 0)), wspec, wspec],
        out_specs=[tile, tile, tile], out_shape=[jax.ShapeDtypeStruct((s, f), BF16)] * 3,
        compiler_params=_cparams("parallel", "parallel"),
    )(h, wg, wu)


def _ffn_in_bwd(dx, wd, a, b, *, name):
    s, d = dx.shape
    f = wd.shape[0]
    tm = _pick(s, 1024, 128)
    tn = _pick(f, 1408, 128)

    def body(dx_ref, wd_ref, a_ref, b_ref, da_ref, db_ref):
        dhv = _dot_nt(dx_ref[...].astype(BF16), wd_ref[...])
        av = a_ref[...].astype(F32)
        sg = _sigmoid(av)
        silu = av * sg
        da_ref[...] = (dhv * b_ref[...].astype(F32) * (sg + silu * (1.0 - sg))).astype(BF16)
        db_ref[...] = (dhv * silu).astype(BF16)

    tile = pl.BlockSpec((tm, tn), lambda i, j: (i, j))
    return pl.pallas_call(
        body, name=name, grid=(s // tm, f // tn),
        in_specs=[pl.BlockSpec((tm, d), lambda i, j: (i, 0)), pl.BlockSpec((tn, d), lambda i, j: (j, 0)), tile, tile],
        out_specs=[tile, tile], out_shape=[jax.ShapeDtypeStruct((s, f), BF16)] * 2,
        compiler_params=_cparams("parallel", "parallel"),
    )(dx, wd, a, b)


def _adamw(w, g, m, v, *, name):
    r, c = w.shape
    tr = _pick(r, 512, 8)

    def body(w_ref, g_ref, m_ref, v_ref, d_ref, mo_ref, vo_ref):
        gv = g_ref[...]
        mn = ADAM_B1 * m_ref[...] + (1.0 - ADAM_B1) * gv
        vn = ADAM_B2 * v_ref[...] + (1.0 - ADAM_B2) * (gv * gv)
        m_hat = mn / (1.0 - ADAM_B1 ** ADAM_STEP)
        v_hat = vn / (1.0 - ADAM_B2 ** ADAM_STEP)
        d_ref[...] = -ADAM_LR * (m_hat / (jnp.sqrt(v_hat) + ADAM_EPS) + ADAM_WD * w_ref[...])
        mo_ref[...] = mn
        vo_ref[...] = vn

    spec = pl.BlockSpec((tr, c), lambda i: (i, 0))
    shp = jax.ShapeDtypeStruct((r, c), F32)
    return pl.pallas_call(
        body, name=name, grid=(r // tr,), in_specs=[spec] * 4, out_specs=[spec] * 3,
        out_shape=[shp] * 3, compiler_params=_cparams("parallel"),
    )(w, g, m, v)


def _position():
    return lax.axis_index("x"), lax.axis_index("y"), lax.axis_index("c")


def _all_gather(x, *, name):
    t, c_ = x.shape

    def body(x_ref, out_ref, send_sems, recv_sems, local_sem):
        start, forward, finish = _gather_phases(x_ref, out_ref, send_sems, recv_sems, local_sem)
        start()
        forward()
        finish()

    return pl.pallas_call(
        body, name=name,
        out_shape=jax.ShapeDtypeStruct((N_DEV, t, c_), x.dtype),
        in_specs=[pl.BlockSpec(memory_space=pl.ANY)],
        out_specs=pl.BlockSpec(memory_space=pl.ANY),
        scratch_shapes=_GATHER_SEMS,
    )(x)


_GATHER_SEMS = [pltpu.SemaphoreType.DMA((7,)), pltpu.SemaphoreType.DMA((7,)), pltpu.SemaphoreType.DMA]


def _gather_phases(x_ref, out_ref, send_sems, recv_sems, local_sem):
    x_, y_, c = _position()
    me, sibling = (x_, y_, c), (x_, y_, 1 - c)
    chips = [(1 - x_, y_), (x_, 1 - y_), (1 - x_, 1 - y_)]

    def block(px, py, pc):
        return out_ref.at[4 * px + 2 * py + pc]

    def copy(k, blk, to, src=None):
        return pltpu.make_async_remote_copy(
            src_ref=block(*blk) if src is None else src, dst_ref=block(*blk),
            send_sem=send_sems.at[k], recv_sem=recv_sems.at[k], device_id=to, device_id_type=MESH)

    mine = pltpu.make_async_copy(x_ref, block(*me), local_sem)
    first = [copy(0, me, sibling, src=x_ref)]
    first += [copy(1 + j, me, (*chip, c), src=x_ref) for j, chip in enumerate(chips)]
    passed = [copy(4 + j, (*chip, c), sibling) for j, chip in enumerate(chips)]

    def start():
        mine.start()
        for cp in first:
            cp.start()

    def forward():
        for j, chip in enumerate(chips):
            copy(1 + j, (*chip, c), me).wait_recv()
            passed[j].start()

    def finish():
        copy(0, sibling, me).wait_recv()
        for j, chip in enumerate(chips):
            copy(4 + j, (*chip, 1 - c), me).wait_recv()
        for cp in first + passed:
            cp.wait_send()
        mine.wait()

    return start, forward, finish


_SCATTER_SEMS = [pltpu.SemaphoreType.DMA((7,)), pltpu.SemaphoreType.DMA((7,)), pltpu.SemaphoreType.DMA]


def _scatter_phases(g_ref, r_ref, send_sems, recv_sems, local_sem):
    x_, y_, c = _position()
    me = 4 * x_ + 2 * y_ + c
    local = pltpu.make_async_copy(g_ref.at[me], r_ref.at[me], local_sem)
    copies = []
    for k in range(1, N_DEV):
        to = (x_ ^ (k >> 2), y_ ^ ((k >> 1) & 1), c ^ (k & 1))
        copies.append(pltpu.make_async_remote_copy(
            src_ref=g_ref.at[me ^ k], dst_ref=r_ref.at[me], send_sem=send_sems.at[k - 1],
            recv_sem=recv_sems.at[k - 1], device_id=to, device_id_type=MESH))

    def start():
        local.start()
        for cp in copies:
            cp.start()

    def finish():
        for k in range(1, N_DEV):
            pltpu.make_async_remote_copy(
                src_ref=g_ref.at[me], dst_ref=r_ref.at[me ^ k], send_sem=send_sems.at[k - 1],
                recv_sem=recv_sems.at[k - 1], device_id=(x_, y_, c), device_id_type=MESH).wait_recv()
        for cp in copies:
            cp.wait_send()
        local.wait()

    return start, finish


def _sum_devices(r8, *, name):
    _, t, c_ = r8.shape
    tr = _pick(t, 512, 16)

    def body(r_ref, o_ref):
        acc = r_ref[0].astype(F32)
        for d in range(1, N_DEV):
            acc = acc + r_ref[d].astype(F32)
        o_ref[...] = acc

    return pl.pallas_call(
        body, name=name, grid=(t // tr,),
        in_specs=[pl.BlockSpec((N_DEV, tr, c_), lambda i: (0, i, 0))],
        out_specs=pl.BlockSpec((tr, c_), lambda i: (i, 0)),
        out_shape=jax.ShapeDtypeStruct((t, c_), F32),
        compiler_params=_cparams("parallel"),
    )(r8)


def _all_reduce_small(x, *, name):
    r, c_ = x.shape

    def body(x_ref, o_ref, buf, send_sems, recv_sems):
        x_, y_, c = _position()
        me = 4 * x_ + 2 * y_ + c
        buf[me] = x_ref[...]
        copies = []
        for k in range(1, N_DEV):
            to = (x_ ^ (k >> 2), y_ ^ ((k >> 1) & 1), c ^ (k & 1))
            copies.append(pltpu.make_async_remote_copy(
                src_ref=x_ref, dst_ref=buf.at[me], send_sem=send_sems.at[k - 1], recv_sem=recv_sems.at[k - 1],
                device_id=to, device_id_type=MESH))
        for cp in copies:
            cp.start()
        for k in range(1, N_DEV):
            src = me ^ k
            pltpu.make_async_remote_copy(
                src_ref=x_ref, dst_ref=buf.at[src], send_sem=send_sems.at[k - 1], recv_sem=recv_sems.at[k - 1],
                device_id=(x_, y_, c), device_id_type=MESH).wait_recv()
        for cp in copies:
            cp.wait_send()
        acc = buf[0]
        for d in range(1, N_DEV):
            acc = acc + buf[d]
        o_ref[...] = acc

    return pl.pallas_call(
        body, name=name,
        out_shape=jax.ShapeDtypeStruct((r, c_), F32),
        in_specs=[pl.BlockSpec(memory_space=pltpu.VMEM)],
        out_specs=pl.BlockSpec(memory_space=pltpu.VMEM),
        scratch_shapes=[pltpu.VMEM((N_DEV, r, c_), F32), pltpu.SemaphoreType.DMA((7,)), pltpu.SemaphoreType.DMA((7,))],
    )(x)


def _rs_pair_exchange(g8, *, name):
    _, t, c_ = g8.shape

    def body(g_ref, r_ref, send_sems, recv_sems):
        x_, y_, c = _position()
        copies = [pltpu.make_async_remote_copy(
            src_ref=g_ref.at[2 * ch + (1 - c)], dst_ref=r_ref.at[ch],
            send_sem=send_sems.at[ch], recv_sem=recv_sems.at[ch],
            device_id=(x_, y_, 1 - c), device_id_type=MESH) for ch in range(4)]
        for cp in copies:
            cp.start()
        for cp in copies:
            cp.wait()

    return pl.pallas_call(
        body, name=name,
        out_shape=jax.ShapeDtypeStruct((4, t, c_), g8.dtype),
        in_specs=[pl.BlockSpec(memory_space=pl.ANY)],
        out_specs=pl.BlockSpec(memory_space=pl.ANY),
        scratch_shapes=[pltpu.SemaphoreType.DMA((4,)), pltpu.SemaphoreType.DMA((4,))],
    )(g8)


def _pair_add(core, g8, recv, *, name):
    _, t, c_ = g8.shape
    tr = _pick(t, 512, 16)

    def body(core_ref, g_ref, r_ref, o_ref):
        o_ref[...] = (g_ref[...].astype(F32) + r_ref[...].astype(F32)).astype(o_ref.dtype)

    grid_spec = pltpu.PrefetchScalarGridSpec(
        num_scalar_prefetch=1, grid=(4, t // tr),
        in_specs=[pl.BlockSpec((None, tr, c_), lambda ch, i, core_ref: (2 * ch + core_ref[0], i, 0)),
                  pl.BlockSpec((None, tr, c_), lambda ch, i, core_ref: (ch, i, 0))],
        out_specs=pl.BlockSpec((None, tr, c_), lambda ch, i, core_ref: (ch, i, 0)))
    return pl.pallas_call(
        body, name=name, grid_spec=grid_spec,
        out_shape=jax.ShapeDtypeStruct((4, t, c_), g8.dtype),
        compiler_params=_cparams("parallel", "parallel"),
    )(core, g8, recv)


def _rs_chip_exchange(part, *, name):
    _, t, c_ = part.shape

    def body(p_ref, r_ref, send_sems, recv_sems, local_sem):
        x_, y_, c = _position()
        mine = 2 * x_ + y_
        local = pltpu.make_async_copy(p_ref.at[mine], r_ref.at[mine], local_sem)
        local.start()
        chips = [(1 - x_, y_), (x_, 1 - y_), (1 - x_, 1 - y_)]
        copies = [pltpu.make_async_remote_copy(
            src_ref=p_ref.at[2 * px + py], dst_ref=r_ref.at[mine],
            send_sem=send_sems.at[k], recv_sem=recv_sems.at[k],
            device_id=(px, py, c), device_id_type=MESH) for k, (px, py) in enumerate(chips)]
        for cp in copies:
            cp.start()
        for k, (px, py) in enumerate(chips):
            pltpu.make_async_remote_copy(
                src_ref=p_ref.at[mine], dst_ref=r_ref.at[2 * px + py],
                send_sem=send_sems.at[k], recv_sem=recv_sems.at[k],
                device_id=(x_, y_, c), device_id_type=MESH).wait_recv()
        for cp in copies:
            cp.wait_send()
        local.wait()

    return pl.pallas_call(
        body, name=name,
        out_shape=jax.ShapeDtypeStruct((4, t, c_), part.dtype),
        in_specs=[pl.BlockSpec(memory_space=pl.ANY)],
        out_specs=pl.BlockSpec(memory_space=pl.ANY),
        scratch_shapes=[pltpu.SemaphoreType.DMA((3,)), pltpu.SemaphoreType.DMA((3,)), pltpu.SemaphoreType.DMA],
    )(part)


def _sum_chips(r4, *, name):
    _, t, c_ = r4.shape
    tr = _pick(t, 512, 16)

    def body(r_ref, o_ref):
        acc = r_ref[0].astype(F32)
        for ch in range(1, 4):
            acc = acc + r_ref[ch].astype(F32)
        o_ref[...] = acc

    return pl.pallas_call(
        body, name=name, grid=(t // tr,),
        in_specs=[pl.BlockSpec((4, tr, c_), lambda i: (0, i, 0))],
        out_specs=pl.BlockSpec((tr, c_), lambda i: (i, 0)),
        out_shape=jax.ShapeDtypeStruct((t, c_), F32),
        compiler_params=_cparams("parallel"),
    )(r4)


BIG = (
    ("w_in", (IN_COLS // N_DEV, D_MODEL), 0),
    ("w_branch", (3, BRANCH_W, D_MODEL // N_DEV), 2),
    ("w_out", (D_MODEL // N_DEV, D_MODEL), 0),
    ("w_q_xa", (D_MODEL // N_DEV, D_MODEL), 0),
    ("w_k_xa", (D_MODEL // N_DEV, D_MODEL), 0),
    ("w_v_xa", (D_MODEL // N_DEV, D_MODEL), 0),
    ("w_o_xa", (D_MODEL // N_DEV, D_MODEL), 0),
    ("w_gate_ffn", (FFN // N_DEV, D_MODEL), 0),
    ("w_up_ffn", (FFN // N_DEV, D_MODEL), 0),
    ("w_down_ffn", (FFN // N_DEV, D_MODEL), 0),
)
TRANSPOSED = ("w_in", "w_gate_ffn", "w_up_ffn")
_BIG_LAYOUT = {n: (shp, ax) for n, shp, ax in BIG}
PACK_COLS = 1024


def _stored(name, shard):
    return shard.T if name in TRANSPOSED else shard


def _size(shape):
    n = 1
    for d in shape:
        n *= d
    return n


def _pack_shards(items, shards):
    return jnp.concatenate([shards[it].reshape(-1, PACK_COLS) for it in items], axis=0)


def _unpack_gathered(items, g):
    out = {}
    r0 = 0
    for it in items:
        shp, ax = _BIG_LAYOUT[it[0]]
        rows = _size(shp) // PACK_COLS
        blk = g[:, r0:r0 + rows].reshape((N_DEV,) + shp)
        r0 += rows
        blk = jnp.moveaxis(blk, 0, ax)
        full = list(shp)
        full[ax] = shp[ax] * N_DEV
        out[it] = blk.reshape(full)
    return out


def _pack_full(items, full):
    parts = []
    for it in items:
        shp, ax = _BIG_LAYOUT[it[0]]
        t = full[it].reshape(shp[:ax] + (N_DEV, shp[ax]) + shp[ax + 1:])
        t = jnp.moveaxis(t, ax, 0)
        parts.append(t.reshape(N_DEV, -1, PACK_COLS))
    rows = sum(part.shape[1] for part in parts)
    if rows % 128:
        parts.append(jnp.zeros((N_DEV, 128 - rows % 128, PACK_COLS), parts[0].dtype))
    return jnp.concatenate(parts, axis=1)


def _unpack_shard(items, flat):
    out = {}
    r0 = 0
    for it in items:
        shp, _ = _BIG_LAYOUT[it[0]]
        rows = _size(shp) // PACK_COLS
        out[it] = flat[r0:r0 + rows].reshape(shp)
        r0 += rows
    return out


SMALL = (
    ("norm_mix_g", (DEPTH, D_MODEL)),
    ("sgu_ln_g", (DEPTH, BRANCH_W)),
    ("sgu_ln_b", (DEPTH, BRANCH_W)),
    ("w_spatial", (DEPTH, SGU_GROUPS, SGU_LEN, SGU_LEN)),
    ("b_spatial", (DEPTH, SGU_GROUPS, SGU_LEN)),
    ("conv_w", (DEPTH, 3, BRANCH_W)),
    ("norm_xa_g", (DEPTH, D_MODEL)),
    ("mem_norm_g", (DEPTH, D_MODEL)),
    ("norm_ffn_g", (DEPTH, D_MODEL)),
    ("final_g", (D_MODEL,)),
)


def _pack_small(grads):
    flat = jnp.concatenate([grads[n].reshape(-1) for n, _ in SMALL])
    rows = -(-flat.shape[0] // PACK_COLS)
    rows = -(-rows // 8) * 8
    flat = jnp.pad(flat, (0, rows * PACK_COLS - flat.shape[0]))
    return flat.reshape(rows, PACK_COLS)


def _unpack_small(buf):
    flat = buf.reshape(-1)
    out = {}
    o = 0
    for n, shp in SMALL:
        out[n] = flat[o:o + _size(shp)].reshape(shp)
        o += _size(shp)
    return out


def _layer_fwd(l, x, mem, wt, sm, gather=None):
    t = f"l{l}_"
    sv = {"x0": x}
    h = _rms_fwd(x, sm["norm_mix_g"][l][None], name=t + "rms_mix")
    p = _mm(h, wt["w_in", l], tb=True, name=t + "in_proj", tm=2048)
    if gather is None:
        ya, *tables = _sb_fwd(p, name=t + "sb_fwd")
    else:
        ya, *tables, gathered = _sb_fwd(p, name=t + "sb_fwd", gather=gather[1])
        wt.update(_unpack_gathered(gather[0], gathered))
    w_sp = sm["w_spatial"][l]
    b_col = sm["b_spatial"][l][:, :, None]
    ln_g, ln_b = sm["sgu_ln_g"][l][None], sm["sgu_ln_b"][l][None]
    yb = _sgu_fwd(p, ln_g, ln_b, w_sp, b_col, name=t + "sgu_fwd")
    yc = _conv_fwd(p, sm["conv_w"][l], name=t + "conv_fwd")
    merged = _merge_fwd(ya, yb, yc, wt["w_branch", l], p, name=t + "merge_fwd")
    x1 = _mm(merged, wt["w_out", l], add=x, name=t + "out_proj")
    sv.update(h=h, p=p, ya=ya, yb=yb, yc=yc, merged=merged, x1=x1, tables=tables)

    h2 = _rms_fwd(x1, sm["norm_xa_g"][l][None], name=t + "rms_xa")
    mn = _rms_fwd(mem, sm["mem_norm_g"][l][None], name=t + "rms_mem")
    q = _mm(h2, wt["w_q_xa", l], out_dtype=BF16, name=t + "xa_q", tm=512)
    k = _mm(mn, wt["w_k_xa", l], out_dtype=BF16, name=t + "xa_k")
    v = _mm(mn, wt["w_v_xa", l], out_dtype=BF16, name=t + "xa_v")
    o = _xa_fwd(q, k, v, name=t + "xa_fwd")
    x2 = _mm(o, wt["w_o_xa", l], add=x1, name=t + "xa_o")
    sv.update(h2=h2, mn=mn, q=q, k=k, v=v, o=o, x2=x2)

    h3 = _rms_fwd(x2, sm["norm_ffn_g"][l][None], name=t + "rms_ffn")
    a, b, hd = _ffn_in(h3, wt["w_gate_ffn", l], wt["w_up_ffn", l], name=t + "ffn_in")
    x3 = _mm(hd, wt["w_down_ffn", l], add=x2, name=t + "ffn_down", tk=FFN)
    sv.update(h3=h3, a=a, b=b, hd=hd)
    return x3, sv


def _layer_bwd(l, dx3, mem, wt, sm, sv, scatter=None):
    t = f"l{l}_b_"
    gb, gs = {}, {}
    gb["w_down_ffn"] = _mm(sv["hd"], dx3, ta=True, out_dtype=BF16, name=t + "ffn_down_dw", tm=1408, tk=1024)
    da, db = _ffn_in_bwd(dx3, wt["w_down_ffn", l], sv["a"], sv["b"], name=t + "ffn_in_bwd")
    gb["w_gate_ffn"] = _mm(da, sv["h3"], ta=True, out_dtype=BF16, name=t + "ffn_gate_dw", tm=1408, tk=1024)
    gb["w_up_ffn"] = _mm(db, sv["h3"], ta=True, out_dtype=BF16, name=t + "ffn_up_dw", tm=1408, tk=1024)
    dh3 = _mm(da, wt["w_gate_ffn", l], name=t + "ffn_gate_dx", tk=1408)
    dx2, dg = _mm(db, wt["w_up_ffn", l], add=dh3, rms=(sv["x2"], sm["norm_ffn_g"][l][None], dx3),
                  name=t + "ffn_up_dx", tm=512, tk=1408)
    gs["norm_ffn_g"] = dg[0]
    do = _mm(dx2, wt["w_o_xa", l], tb=True, out_dtype=BF16, name=t + "xa_o_dx", tm=512)
    gb["w_o_xa"] = _mm(sv["o"], dx2, ta=True, out_dtype=BF16, name=t + "xa_o_dw", tk=512)
    dq, dk, dv = _xa_bwd(sv["q"], sv["k"], sv["v"], do, name=t + "xa_bwd")
    dx1, dg = _mm(dq, wt["w_q_xa", l], tb=True, rms=(sv["x1"], sm["norm_xa_g"][l][None], dx2),
                  name=t + "xa_q_dx", tm=1024)
    gs["norm_xa_g"] = dg[0]
    gb["w_q_xa"] = _mm(sv["h2"], dq, ta=True, out_dtype=BF16, name=t + "xa_q_dw", tk=512)
    gb["w_k_xa"] = _mm(sv["mn"], dk, ta=True, out_dtype=BF16, name=t + "xa_k_dw")
    gb["w_v_xa"] = _mm(sv["mn"], dv, ta=True, out_dtype=BF16, name=t + "xa_v_dw")
    dmn = _mm(dk, wt["w_k_xa", l], tb=True, name=t + "xa_k_dx")
    dmn = _mm(dv, wt["w_v_xa", l], tb=True, add=dmn, name=t + "xa_v_dx")
    _, dg = _rms_bwd(mem, sm["mem_norm_g"][l][None], dmn, jnp.zeros_like(mem), name=t + "rms_mem")
    gs["mem_norm_g"] = dg[0]
    dm = _mm(dx1, wt["w_out", l], tb=True, name=t + "out_proj_dx", tm=512)
    gb["w_out"] = _mm(sv["merged"], dx1, ta=True, out_dtype=BF16, name=t + "out_proj_dw", tk=512)
    p = sv["p"]
    dya, dyb, dyc, dgates, *dbrd = _merge_bwd(dm, sv["ya"], sv["yb"], sv["yc"], wt["w_branch", l], p,
                                              name=t + "merge_bwd")
    gb["w_branch"] = jnp.stack([
        _mm(sv[y], dbrd[n], ta=True, out_dtype=BF16, name=t + f"branch{n}_dw", tk=512)
        for n, y in enumerate(("ya", "yb", "yc"))])
    dcb, dcc, dcx, dcw = _conv_bwd(p, dyc, sm["conv_w"][l], name=t + "conv_bwd")
    gs["conv_w"] = dcw
    w_sp = sm["w_spatial"][l]
    dz, dlg, dlb, dwsp, dbsp = _sgu_bwd(p, dyb, sm["sgu_ln_g"][l][None], sm["sgu_ln_b"][l][None], w_sp,
                                        jnp.swapaxes(w_sp, 1, 2), sm["b_spatial"][l][:, :, None],
                                        name=t + "sgu_bwd")
    gs.update(sgu_ln_g=dlg[0], sgu_ln_b=dlb[0], w_spatial=dwsp, b_spatial=dbsp[:, :, 0])
    received = None
    if scatter is None:
        dq_a, dk_a, dv_a = _sb_bwd(p, dya, sv["tables"], name=t + "sb_bwd")
    else:
        items, earlier = scatter
        ready = {**earlier, **{(n, l): g for n, g in gb.items()}}
        dq_a, dk_a, dv_a, received = _sb_bwd(p, dya, sv["tables"], name=t + "sb_bwd",
                                             scatter=_pack_full(items, ready))
    dp = jnp.concatenate([dq_a, dk_a, dv_a, dz, dcb, dcc, dcx, dgates], axis=1)
    gb["w_in"] = _mm(dp, sv["h"], ta=True, out_dtype=BF16, name=t + "in_proj_dw")
    dx, dg = _mm(dp, wt["w_in", l], rms=(sv["x0"], sm["norm_mix_g"][l][None], dx1),
                 name=t + "in_proj_dx", tm=1024, tk=1792)
    gs["norm_mix_g"] = dg[0]
    return dx, gb, gs, received


_WEIGHTS = ("norm_mix_g", "w_in", "sgu_ln_g", "sgu_ln_b", "w_spatial", "b_spatial", "conv_w", "w_branch", "w_out",
            "norm_xa_g", "mem_norm_g", "w_q_xa", "w_k_xa", "w_v_xa", "w_o_xa", "norm_ffn_g", "w_gate_ffn",
            "w_up_ffn", "w_down_ffn", "final_g")


def kernel(x, mem, norm_mix_g, w_in, sgu_ln_g, sgu_ln_b, w_spatial, b_spatial, conv_w, w_branch, w_out, norm_xa_g, mem_norm_g, w_q_xa, w_k_xa, w_v_xa, w_o_xa, norm_ffn_g, w_gate_ffn, w_up_ffn, w_down_ffn, final_g, loss_target, m_norm_mix_g, m_w_in, m_sgu_ln_g, m_sgu_ln_b, m_w_spatial, m_b_spatial, m_conv_w, m_w_branch, m_w_out, m_norm_xa_g, m_mem_norm_g, m_w_q_xa, m_w_k_xa, m_w_v_xa, m_w_o_xa, m_norm_ffn_g, m_w_gate_ffn, m_w_up_ffn, m_w_down_ffn, m_final_g, v_norm_mix_g, v_w_in, v_sgu_ln_g, v_sgu_ln_b, v_w_spatial, v_b_spatial, v_conv_w, v_w_branch, v_w_out, v_norm_xa_g, v_mem_norm_g, v_w_q_xa, v_w_k_xa, v_w_v_xa, v_w_o_xa, v_norm_ffn_g, v_w_gate_ffn, v_w_up_ffn, v_w_down_ffn, v_final_g):
    w = dict(norm_mix_g=norm_mix_g, w_in=w_in, sgu_ln_g=sgu_ln_g, sgu_ln_b=sgu_ln_b, w_spatial=w_spatial,
             b_spatial=b_spatial, conv_w=conv_w, w_branch=w_branch, w_out=w_out, norm_xa_g=norm_xa_g,
             mem_norm_g=mem_norm_g, w_q_xa=w_q_xa, w_k_xa=w_k_xa, w_v_xa=w_v_xa, w_o_xa=w_o_xa,
             norm_ffn_g=norm_ffn_g, w_gate_ffn=w_gate_ffn, w_up_ffn=w_up_ffn, w_down_ffn=w_down_ffn, final_g=final_g)
    m = dict(norm_mix_g=m_norm_mix_g, w_in=m_w_in, sgu_ln_g=m_sgu_ln_g, sgu_ln_b=m_sgu_ln_b, w_spatial=m_w_spatial,
             b_spatial=m_b_spatial, conv_w=m_conv_w, w_branch=m_w_branch, w_out=m_w_out, norm_xa_g=m_norm_xa_g,
             mem_norm_g=m_mem_norm_g, w_q_xa=m_w_q_xa, w_k_xa=m_w_k_xa, w_v_xa=m_w_v_xa, w_o_xa=m_w_o_xa,
             norm_ffn_g=m_norm_ffn_g, w_gate_ffn=m_w_gate_ffn, w_up_ffn=m_w_up_ffn, w_down_ffn=m_w_down_ffn,
             final_g=m_final_g)
    v = dict(norm_mix_g=v_norm_mix_g, w_in=v_w_in, sgu_ln_g=v_sgu_ln_g, sgu_ln_b=v_sgu_ln_b, w_spatial=v_w_spatial,
             b_spatial=v_b_spatial, conv_w=v_conv_w, w_branch=v_w_branch, w_out=v_w_out, norm_xa_g=v_norm_xa_g,
             mem_norm_g=v_mem_norm_g, w_q_xa=v_w_q_xa, w_k_xa=v_w_k_xa, w_v_xa=v_w_v_xa, w_o_xa=v_w_o_xa,
             norm_ffn_g=v_norm_ffn_g, w_gate_ffn=v_w_gate_ffn, w_up_ffn=v_w_up_ffn, w_down_ffn=v_w_down_ffn,
             final_g=v_final_g)

    names = [n for n, _, _ in BIG]
    shards = {(n, l): _stored(n, w[n][l].astype(BF16)) for n in names for l in range(DEPTH)}
    first_items = [("w_in", 0)]
    mid_items = [(n, 0) for n in names if n != "w_in"] + [("w_in", 1)]
    last_items = [(n, 1) for n in names if n != "w_in"]
    wt = _unpack_gathered(first_items, _all_gather(_pack_shards(first_items, shards), name="gather_w_in0"))
    cw_pad = jnp.zeros((8, 128), F32).at[:DEPTH * 3, :BRANCH_W // N_DEV].set(conv_w.reshape(DEPTH * 3, -1))
    cw_all = _all_gather(cw_pad, name="gather_conv_w")[:, :DEPTH * 3, :BRANCH_W // N_DEV]
    conv_full = jnp.moveaxis(cw_all.reshape(N_DEV, DEPTH, 3, BRANCH_W // N_DEV), 0, 2).reshape(DEPTH, 3, BRANCH_W)
    sm = {n: w[n] for n, _ in SMALL}
    sm["conv_w"] = conv_full

    xs, ms = x[0], mem[0]
    x1, saved0 = _layer_fwd(0, xs, ms, wt, sm, gather=(mid_items, _pack_shards(mid_items, shards)))
    x2, saved1 = _layer_fwd(1, x1, ms, wt, sm, gather=(last_items, _pack_shards(last_items, shards)))
    dcur, loss, dfinal = _final_loss(x2, sm["final_g"][None], loss_target[0], name="final_loss")
    loss = lax.psum(loss[0, 0], AXES)
    items_a = [(n, 1) for n in names if n != "w_in"]
    items_b = [("w_in", 1)] + [(n, 0) for n in names if n != "w_in"]
    items_c = [("w_in", 0)]
    dcur, gb1, gs1, recv_a = _layer_bwd(1, dcur, ms, wt, sm, saved1, scatter=(items_a, {}))
    dx, gb0, gs0, recv_b = _layer_bwd(0, dcur, ms, wt, sm, saved0, scatter=(items_b, {("w_in", 1): gb1["w_in"]}))

    shard_grads = _unpack_shard(items_a, _sum_devices(recv_a, name="rs_sum_a"))
    shard_grads.update(_unpack_shard(items_b, _sum_devices(recv_b, name="rs_sum_b")))
    g8 = _pack_full(items_c, {("w_in", 0): gb0["w_in"]})
    core = lax.axis_index("c").astype(jnp.int32).reshape(1)
    from_sibling = _rs_pair_exchange(g8, name="rs_pair_exchange")
    part = _pair_add(core, g8, from_sibling, name="rs_pair_add")
    by_chip = _rs_chip_exchange(part, name="rs_chip_exchange")
    shard_grads.update(_unpack_shard(items_c, _sum_chips(by_chip, name="rs_sum_chips")))
    grads = {n: jnp.stack([_stored(n, shard_grads[n, l]) for l in range(DEPTH)]) for n in names}
    small = {n: jnp.stack([gs0[n], gs1[n]]) for n, _ in SMALL if n != "final_g"}
    small["final_g"] = dfinal[0]
    small_sum = _unpack_small(_all_reduce_small(_pack_small(small), name="all_reduce_small"))
    width = BRANCH_W // N_DEV
    dev = 4 * lax.axis_index("x") + 2 * lax.axis_index("y") + lax.axis_index("c")
    for n, _ in SMALL:
        grads[n] = small_sum[n]
    grads["conv_w"] = lax.dynamic_slice_in_dim(small_sum["conv_w"], dev * width, width, axis=2)

    delta, new_m, new_v = {}, {}, {}
    for n in _WEIGHTS:
        shp = w[n].shape
        two_d = (-1, shp[-1])
        d_, m_, v_ = _adamw(w[n].reshape(two_d), grads[n].reshape(two_d), m[n].reshape(two_d), v[n].reshape(two_d),
                            name="adamw_" + n)
        delta[n], new_m[n], new_v[n] = d_.reshape(shp), m_.reshape(shp), v_.reshape(shp)

    return (loss, dx[None], *[grads[n] for n in _WEIGHTS], *[delta[n] for n in _WEIGHTS],
            *[new_m[n] for n in _WEIGHTS], *[new_v[n] for n in _WEIGHTS])
```

```python
import jax
import jax.numpy as jnp
from jax import lax
from jax.experimental import pallas as pl
from jax.experimental.pallas import tpu as pltpu

F32 = jnp.float32
BF16 = jnp.bfloat16
MESH = pl.DeviceIdType.MESH

D_MODEL = 1024
BRANCH_W = 512
IN_COLS = 7168
FFN = 2816
N_DEV = 8
DEPTH = 2
SB_BLOCK = 128
SB_SPAN = 1024
SB_Q_FWD = 512
SB_Q_BWD = 512
SB_SCALE = 0.125
XA_HEAD = 256
XA_SCALE = 0.0625
SGU_LEN = 128
SGU_GROUPS = 4
RMS_EPS = 1e-6
LN_EPS = 1e-5
HALO = 8

ADAM_LR = 0.001
ADAM_B1 = 0.9
ADAM_B2 = 0.999
ADAM_EPS = 1e-08
ADAM_WD = 0.01
ADAM_STEP = 10

VMEM_LIMIT_BYTES = 52 * 1024 * 1024

AXES = ("x", "y", "c")


def _cparams(*sem):
    return pltpu.CompilerParams(dimension_semantics=sem, vmem_limit_bytes=VMEM_LIMIT_BYTES)


def _pick(n, target, align):
    t = (min(target, n) // align) * align
    while t >= align:
        if n % t == 0:
            return t
        t -= align
    return n


def _dot(a, b):
    return jnp.dot(a, b, preferred_element_type=F32)


def _dot_nt(a, b):
    return lax.dot_general(a, b, (((1,), (1,)), ((), ())), preferred_element_type=F32)


def _dot_tn(a, b):
    return lax.dot_general(a, b, (((0,), (0,)), ((), ())), preferred_element_type=F32)


def _sigmoid(x):
    return 1.0 / (1.0 + jnp.exp(-x))


def _mm(a, b, *, name, ta=False, tb=False, out_dtype=F32, add=None, rms=None, tm=1024, tn=1024, tk=2048):
    m, k = (a.shape[1], a.shape[0]) if ta else a.shape
    n = b.shape[0] if tb else b.shape[1]
    assert k == (b.shape[1] if tb else b.shape[0])
    tm = _pick(m, tm, 128)
    tn = n if rms is not None else _pick(n, tn, 128)
    tk = _pick(k, tk, 128)
    nk = k // tk
    ca = 0 if ta else 1
    cb = 1 if tb else 0
    n_add = 0 if add is None else 1
    n_rms = 0 if rms is None else 3

    def body(*refs):
        refs = list(refs)
        a_ref, b_ref = refs[:2]
        extra = refs[2:2 + n_add + n_rms]
        outs = refs[2 + n_add + n_rms:]
        o_ref = outs[0]
        kk = pl.program_id(2)
        first_row_tile = pl.program_id(0) == 0

        def product():
            return lax.dot_general(a_ref[...].astype(BF16), b_ref[...].astype(BF16),
                                   (((ca,), (cb,)), ((), ())), preferred_element_type=F32)

        def finish(r):
            if add is not None:
                r = r + extra[0][...]
            if rms is None:
                o_ref[...] = r.astype(out_dtype)
                return
            x_ref, g_ref, dres_ref = extra[n_add:]
            dg_ref = outs[1]

            @pl.when(first_row_tile)
            def _():
                dg_ref[...] = jnp.zeros_like(dg_ref)

            xv = x_ref[...]
            rs = lax.rsqrt(jnp.mean(xv * xv, axis=-1, keepdims=True) + RMS_EPS)
            xh = xv * rs
            dg_ref[...] += jnp.sum(r * xh, axis=0, keepdims=True)
            dxh = r * g_ref[...]
            o_ref[...] = dres_ref[...] + rs * (dxh - xh * jnp.mean(dxh * xh, axis=-1, keepdims=True))

        if nk == 1:
            finish(product())
        else:
            acc_ref = outs[-1]

            @pl.when(kk == 0)
            def _():
                acc_ref[...] = jnp.zeros_like(acc_ref)

            acc_ref[...] += product()

            @pl.when(kk == nk - 1)
            def _():
                finish(acc_ref[...])

    a_spec = pl.BlockSpec((tk, tm), lambda i, j, kk: (kk, i)) if ta else pl.BlockSpec((tm, tk), lambda i, j, kk: (i, kk))
    b_spec = pl.BlockSpec((tn, tk), lambda i, j, kk: (j, kk)) if tb else pl.BlockSpec((tk, tn), lambda i, j, kk: (kk, j))
    tile = pl.BlockSpec((tm, tn), lambda i, j, kk: (i, j))
    in_specs = [a_spec, b_spec]
    operands = [a, b]
    out_specs = [tile]
    out_shape = [jax.ShapeDtypeStruct((m, n), out_dtype)]
    if add is not None:
        in_specs.append(tile)
        operands.append(add)
    if rms is not None:
        vec = pl.BlockSpec((1, n), lambda i, j, kk: (0, 0))
        in_specs += [tile, vec, tile]
        operands += list(rms)
        out_specs.append(vec)
        out_shape = [jax.ShapeDtypeStruct((m, n), F32), jax.ShapeDtypeStruct((1, n), F32)]
    out = pl.pallas_call(
        body, name=name,
        grid=(m // tm, n // tn, nk),
        in_specs=in_specs, out_specs=out_specs, out_shape=out_shape,
        scratch_shapes=[pltpu.VMEM((tm, tn), F32)] if nk > 1 else [],
        compiler_params=_cparams("arbitrary" if rms is not None else "parallel", "parallel", "arbitrary"),
    )(*operands)
    return out[0] if rms is None else out


def _rms_fwd(x, g, *, name):
    r, d = x.shape
    tr = _pick(r, 512, 16)

    def body(x_ref, g_ref, o_ref):
        xv = x_ref[...]
        rs = lax.rsqrt(jnp.mean(xv * xv, axis=-1, keepdims=True) + RMS_EPS)
        o_ref[...] = (xv * rs * g_ref[...]).astype(BF16)

    return pl.pallas_call(
        body, name=name, grid=(r // tr,),
        in_specs=[pl.BlockSpec((tr, d), lambda i: (i, 0)), pl.BlockSpec((1, d), lambda i: (0, 0))],
        out_specs=pl.BlockSpec((tr, d), lambda i: (i, 0)),
        out_shape=jax.ShapeDtypeStruct((r, d), BF16),
        compiler_params=_cparams("parallel"),
    )(x, g)


def _rms_bwd(x, g, dh, dres, *, name):
    r, d = x.shape
    tr = _pick(r, 256, 8)

    def body(x_ref, g_ref, dh_ref, dres_ref, dx_ref, dg_ref):
        @pl.when(pl.program_id(0) == 0)
        def _():
            dg_ref[...] = jnp.zeros_like(dg_ref)

        xv = x_ref[...]
        dhv = dh_ref[...].astype(F32)
        rs = lax.rsqrt(jnp.mean(xv * xv, axis=-1, keepdims=True) + RMS_EPS)
        xh = xv * rs
        dg_ref[...] += jnp.sum(dhv * xh, axis=0, keepdims=True)
        dxh = dhv * g_ref[...]
        dx_ref[...] = dres_ref[...] + rs * (dxh - xh * jnp.mean(dxh * xh, axis=-1, keepdims=True))

    return pl.pallas_call(
        body, name=name, grid=(r // tr,),
        in_specs=[pl.BlockSpec((tr, d), lambda i: (i, 0)), pl.BlockSpec((1, d), lambda i: (0, 0)),
                  pl.BlockSpec((tr, d), lambda i: (i, 0)), pl.BlockSpec((tr, d), lambda i: (i, 0))],
        out_specs=[pl.BlockSpec((tr, d), lambda i: (i, 0)), pl.BlockSpec((1, d), lambda i: (0, 0))],
        out_shape=[jax.ShapeDtypeStruct((r, d), F32), jax.ShapeDtypeStruct((1, d), F32)],
        compiler_params=_cparams("arbitrary"),
    )(x, g, dh, dres)


def _final_loss(x, g, target, *, name):
    r, d = x.shape
    tr = _pick(r, 512, 8)

    def body(x_ref, g_ref, t_ref, dx_ref, loss_ref, dg_ref):
        @pl.when(pl.program_id(0) == 0)
        def _():
            dg_ref[...] = jnp.zeros_like(dg_ref)
            loss_ref[...] = jnp.zeros_like(loss_ref)

        xv = x_ref[...]
        gv = g_ref[...]
        rs = lax.rsqrt(jnp.mean(xv * xv, axis=-1, keepdims=True) + RMS_EPS)
        xh = xv * rs
        err = xh * gv - t_ref[...]
        row_loss = jnp.mean(err * err, axis=-1, keepdims=True)
        loss_ref[...] += 0.5 * jnp.sum(row_loss, axis=0, keepdims=True)
        dy = err * (1.0 / d)
        dg_ref[...] += jnp.sum(dy * xh, axis=0, keepdims=True)
        dxh = dy * gv
        dx_ref[...] = rs * (dxh - xh * jnp.mean(dxh * xh, axis=-1, keepdims=True))

    return pl.pallas_call(
        body, name=name, grid=(r // tr,),
        in_specs=[pl.BlockSpec((tr, d), lambda i: (i, 0)), pl.BlockSpec((1, d), lambda i: (0, 0)),
                  pl.BlockSpec((tr, d), lambda i: (i, 0))],
        out_specs=[pl.BlockSpec((tr, d), lambda i: (i, 0)), pl.BlockSpec((1, 128), lambda i: (0, 0)),
                   pl.BlockSpec((1, d), lambda i: (0, 0))],
        out_shape=[jax.ShapeDtypeStruct((r, d), F32), jax.ShapeDtypeStruct((1, 128), F32),
                   jax.ShapeDtypeStruct((1, d), F32)],
        compiler_params=_cparams("arbitrary"),
    )(x, g, target)


def _cumsum_operand(strict_after, totals=True):
    width = (2 if totals else 1) * SB_BLOCK
    r = lax.broadcasted_iota(jnp.int32, (SB_BLOCK, width), 0)
    c = lax.broadcasted_iota(jnp.int32, (SB_BLOCK, width), 1)
    tri = (r > c) if strict_after else (r < c)
    return jnp.where((c >= SB_BLOCK) | tri, 1.0, 0.0).astype(BF16)


def _sb_scores(qh, kw, run, valid, after_ones):
    nb = kw.shape[0] // SB_BLOCK
    z = _dot_nt(qh, kw)
    lsp = jnp.minimum(z, 0.0) - jnp.log(1.0 + jnp.exp(-jnp.abs(z)))
    l1m = lsp - z
    if valid is not None:
        l1m = jnp.where(valid, l1m, 0.0)
    l1b = l1m.astype(BF16)
    later = [None] * nb
    seen = [None] * nb
    for b in reversed(range(nb)):
        cols = slice(b * SB_BLOCK, (b + 1) * SB_BLOCK)
        ct = _dot(l1b[:, cols], after_ones)
        seen[b] = run
        later[b] = run + ct[:, :SB_BLOCK]
        run = run + ct[:, SB_BLOCK:]
    a = jnp.exp(lsp + jnp.concatenate(later, axis=1))
    if valid is not None:
        a = jnp.where(valid, a, 0.0)
    return a, run, seen


def _sb_setup(q_ref, span):
    qi = pl.program_id(1)
    rows = q_ref.shape[0]
    sd = (qi * rows + rows - 1) // span
    lane = lax.broadcasted_iota(jnp.int32, (rows, SB_BLOCK), 1)
    col = lax.broadcasted_iota(jnp.int32, (rows, span), 1)
    row = lax.broadcasted_iota(jnp.int32, (rows, span), 0)
    valid = col < (qi * rows - sd * span) + row
    q = q_ref[...] * SB_SCALE
    qhs = (jnp.where(lane < 64, q, 0.0).astype(BF16), jnp.where(lane >= 64, q, 0.0).astype(BF16))
    return lane, sd, valid, qhs


def _sb_fwd(p, *, name, gather=None):
    s = p.shape[0]
    qrows = min(SB_Q_FWD, s)
    nq = s // qrows
    kcol = BRANCH_W // SB_BLOCK
    span = min(SB_SPAN, s)
    per = span // SB_BLOCK
    assert s // SB_BLOCK <= SB_BLOCK

    def body(*refs):
        if gather is None:
            q_ref, k_ref, v_ref, o_ref, r0_ref, r1_ref = refs
        else:
            q_ref, k_ref, v_ref, x_ref, o_ref, r0_ref, r1_ref, g_ref, send_sems, recv_sems, local_sem = refs
            start, forward, finish = _gather_phases(x_ref, g_ref, send_sems, recv_sems, local_sem)
            step = pl.program_id(0) * nq + pl.program_id(1)
            pl.when(step == 0)(start)
        lane, sd, valid, qhs = _sb_setup(q_ref, span)
        after_ones = _cumsum_operand(True)
        zero = jnp.zeros((qrows, SB_BLOCK), F32)
        lane_row = lax.broadcasted_iota(jnp.int32, (1, SB_BLOCK), 1)

        def span_step(sb, carry, mask):
            rows = pl.ds(pl.multiple_of(sb * span, span), span)
            kw = k_ref[rows, :].astype(BF16)
            vw = v_ref[rows, :].astype(BF16)
            out = []
            for h in range(2):
                run, acc, table = carry[h]
                a, run, seen = _sb_scores(qhs[h], kw, run, mask, after_ones)
                for b in range(per):
                    table = jnp.where(lane_row == sb * per + b, seen[b], table)
                out.append((run, acc + _dot(a.astype(BF16), vw), table))
            return tuple(out)

        carry = span_step(sd, ((zero, zero, zero), (zero, zero, zero)), valid)
        carry = lax.fori_loop(0, sd, lambda t, c: span_step(sd - 1 - t, c, None), carry)
        o_ref[...] = jnp.where(lane < 64, carry[0][1], carry[1][1]).astype(BF16)
        r0_ref[...] = carry[0][2]
        r1_ref[...] = carry[1][2]
        if gather is not None:
            pl.when(step == (kcol - 1) * nq + (3 * nq) // 4)(forward)
            pl.when(step == kcol * nq - 1)(finish)

    in_specs = [pl.BlockSpec((qrows, SB_BLOCK), lambda hp, qi: (qi, hp)),
                pl.BlockSpec((s, SB_BLOCK), lambda hp, qi: (0, kcol + hp)),
                pl.BlockSpec((s, SB_BLOCK), lambda hp, qi: (0, 2 * kcol + hp))]
    table = pl.BlockSpec((None, qrows, SB_BLOCK), lambda hp, qi: (hp, qi, 0))
    out_specs = [pl.BlockSpec((qrows, SB_BLOCK), lambda hp, qi: (qi, hp)), table, table]
    out_shape = [jax.ShapeDtypeStruct((s, BRANCH_W), BF16)] + [jax.ShapeDtypeStruct((kcol, s, SB_BLOCK), F32)] * 2
    operands = [p, p, p]
    scratch = []
    if gather is not None:
        in_specs.append(pl.BlockSpec(memory_space=pl.ANY))
        out_specs.append(pl.BlockSpec(memory_space=pl.ANY))
        out_shape.append(jax.ShapeDtypeStruct((N_DEV,) + gather.shape, gather.dtype))
        operands.append(gather)
        scratch = _GATHER_SEMS
    out = pl.pallas_call(
        body, name=name, grid=(kcol, nq), in_specs=in_specs, out_specs=out_specs, out_shape=out_shape,
        scratch_shapes=scratch, compiler_params=_cparams("arbitrary", "arbitrary"),
    )(*operands)
    return out


def _sb_bwd(p, dya, tables, *, name, scatter=None):
    s = p.shape[0]
    qrows = min(SB_Q_BWD, s)
    nq = s // qrows
    kcol = BRANCH_W // SB_BLOCK
    span = min(SB_SPAN, s)
    per = span // SB_BLOCK

    def body(*refs):
        if scatter is None:
            q_ref, k_ref, v_ref, do_ref, t0_ref, t1_ref, dq_ref, dk_ref, dv_ref, dk_acc, dv_acc = refs
        else:
            (q_ref, k_ref, v_ref, do_ref, t0_ref, t1_ref, g_ref, dq_ref, dk_ref, dv_ref, r_ref,
             dk_acc, dv_acc, send_sems, recv_sems, local_sem) = refs
            start, finish = _scatter_phases(g_ref, r_ref, send_sems, recv_sems, local_sem)
            step = pl.program_id(0) * nq + pl.program_id(1)
            pl.when(step == 0)(start)
        qi = pl.program_id(1)

        @pl.when(qi == 0)
        def _():
            dk_acc[...] = jnp.zeros_like(dk_acc)
            dv_acc[...] = jnp.zeros_like(dv_acc)

        lane, sd, valid, qhs = _sb_setup(q_ref, span)
        after = _cumsum_operand(True, totals=False)
        before_ones = _cumsum_operand(False)
        do = do_ref[...]
        dohs = (jnp.where(lane < 64, do, 0.0).astype(BF16), jnp.where(lane >= 64, do, 0.0).astype(BF16))
        tabs = (t0_ref[...], t1_ref[...])
        lane_row = lax.broadcasted_iota(jnp.int32, (1, SB_BLOCK), 1)
        zero = jnp.zeros((qrows, SB_BLOCK), F32)

        def span_step(sb, carry, mask):
            rows = pl.ds(pl.multiple_of(sb * span, span), span)
            kw = k_ref[rows, :].astype(BF16)
            vw = v_ref[rows, :].astype(BF16)
            out = []
            dk_span = jnp.zeros((span, SB_BLOCK), F32)
            dv_span = jnp.zeros((span, SB_BLOCK), F32)
            for h in range(2):
                pg, dq = carry[h]
                z = _dot_nt(qhs[h], kw)
                lsp = jnp.minimum(z, 0.0) - jnp.log(1.0 + jnp.exp(-jnp.abs(z)))
                l1m = lsp - z
                if mask is not None:
                    l1m = jnp.where(mask, l1m, 0.0)
                l1b = l1m.astype(BF16)
                later = [None] * per
                for b in range(per):
                    cols = slice(b * SB_BLOCK, (b + 1) * SB_BLOCK)
                    seen = jnp.sum(jnp.where(lane_row == sb * per + b, tabs[h], 0.0), axis=-1, keepdims=True)
                    later[b] = seen + _dot(l1b[:, cols], after)
                a = jnp.exp(lsp + jnp.concatenate(later, axis=1))
                beta = jnp.exp(lsp)
                if mask is not None:
                    a = jnp.where(mask, a, 0.0)
                    beta = jnp.where(mask, beta, 0.0)
                g = a * _dot_nt(dohs[h], vw)
                gb = g.astype(BF16)
                before = [None] * per
                for b in range(per):
                    cols = slice(b * SB_BLOCK, (b + 1) * SB_BLOCK)
                    gt = _dot(gb[:, cols], before_ones)
                    before[b] = pg + gt[:, :SB_BLOCK]
                    pg = pg + gt[:, SB_BLOCK:]
                dz = (g * (1.0 - beta) - beta * jnp.concatenate(before, axis=1)).astype(BF16)
                dk_span = dk_span + _dot_tn(dz, qhs[h])
                dv_span = dv_span + _dot_tn(a.astype(BF16), dohs[h])
                out.append((pg, dq + _dot(dz, kw)))
            dk_acc[rows, :] += dk_span
            dv_acc[rows, :] += dv_span
            return tuple(out)

        carry = lax.fori_loop(0, sd, lambda sb, c: span_step(sb, c, None), ((zero, zero), (zero, zero)))
        carry = span_step(sd, carry, valid)
        dq_ref[...] = (jnp.where(lane < 64, carry[0][1], carry[1][1]) * SB_SCALE).astype(BF16)

        @pl.when(qi == nq - 1)
        def _():
            dk_ref[...] = dk_acc[...].astype(BF16)
            dv_ref[...] = dv_acc[...].astype(BF16)

        if scatter is not None:
            pl.when(step == kcol * nq - 1)(finish)

    blk = pl.BlockSpec((qrows, SB_BLOCK), lambda hp, qi: (qi, hp))
    col = pl.BlockSpec((s, SB_BLOCK), lambda hp, qi: (0, hp))
    table = pl.BlockSpec((None, qrows, SB_BLOCK), lambda hp, qi: (hp, qi, 0))
    out = jax.ShapeDtypeStruct((s, BRANCH_W), BF16)
    in_specs = [blk,
                pl.BlockSpec((s, SB_BLOCK), lambda hp, qi: (0, kcol + hp)),
                pl.BlockSpec((s, SB_BLOCK), lambda hp, qi: (0, 2 * kcol + hp)),
                blk, table, table]
    out_specs = [blk, col, col]
    out_shape = [out, out, out]
    operands = [p, p, p, dya, tables[0], tables[1]]
    scratch = [pltpu.VMEM((s, SB_BLOCK), F32), pltpu.VMEM((s, SB_BLOCK), F32)]
    if scatter is not None:
        in_specs.append(pl.BlockSpec(memory_space=pl.ANY))
        out_specs.append(pl.BlockSpec(memory_space=pl.ANY))
        out_shape.append(jax.ShapeDtypeStruct(scatter.shape, scatter.dtype))
        operands.append(scatter)
        scratch = scratch + _SCATTER_SEMS
    return pl.pallas_call(
        body, name=name, grid=(kcol, nq), in_specs=in_specs, out_specs=out_specs, out_shape=out_shape,
        scratch_shapes=scratch, compiler_params=_cparams("arbitrary", "arbitrary"),
    )(*operands)


_INV_SQRT2 = 0.7071067811865476
_INV_SQRT2PI = 0.3989422804014327


def _gelu(x):
    return 0.5 * x * (1.0 + lax.erf(x * _INV_SQRT2))


def _gelu_grad(x):
    return 0.5 * (1.0 + lax.erf(x * _INV_SQRT2)) + x * _INV_SQRT2PI * jnp.exp(-0.5 * x * x)


def _chunk_mask(transposed=False):
    r = lax.broadcasted_iota(jnp.int32, (SGU_LEN, SGU_LEN), 0)
    c = lax.broadcasted_iota(jnp.int32, (SGU_LEN, SGU_LEN), 1)
    return (c // 64) >= (r // 64) if transposed else (r // 64) >= (c // 64)


def _sgu_norm(v_raw, g, b):
    zv = _gelu(v_raw)
    xc = zv - jnp.mean(zv, axis=-1, keepdims=True)
    rs = lax.rsqrt(jnp.mean(xc * xc, axis=-1, keepdims=True) + LN_EPS)
    xh = xc * rs
    return xh, rs, xh * g + b


def _sgu_fwd(p, ln_g, ln_b, w, b_col, *, name):
    s = p.shape[0]
    tr = _pick(s, 512, SGU_LEN)

    def body(u_ref, v_ref, g_ref, b_ref, w_ref, bc_ref, o_ref):
        mask = _chunk_mask()
        zu = _gelu(u_ref[...])
        _, _, vn = _sgu_norm(v_ref[...], g_ref[...], b_ref[...])
        vnb = vn.astype(BF16)
        for gi in range(SGU_GROUPS):
            wg = jnp.where(mask, w_ref[gi], 0.0).astype(BF16)
            cs = slice(gi * SGU_LEN, (gi + 1) * SGU_LEN)
            for c in range(tr // SGU_LEN):
                rs_ = slice(c * SGU_LEN, (c + 1) * SGU_LEN)
                vm = _dot(wg, vnb[rs_, cs]) + bc_ref[gi]
                o_ref[rs_, cs] = (zu[rs_, cs] * vm).astype(BF16)

    vec = pl.BlockSpec((1, BRANCH_W), lambda i: (0, 0))
    return pl.pallas_call(
        body, name=name, grid=(s // tr,),
        in_specs=[pl.BlockSpec((tr, BRANCH_W), lambda i: (i, 3)), pl.BlockSpec((tr, BRANCH_W), lambda i: (i, 4)),
                  vec, vec,
                  pl.BlockSpec((SGU_GROUPS, SGU_LEN, SGU_LEN), lambda i: (0, 0, 0)),
                  pl.BlockSpec((SGU_GROUPS, SGU_LEN, 1), lambda i: (0, 0, 0))],
        out_specs=pl.BlockSpec((tr, BRANCH_W), lambda i: (i, 0)),
        out_shape=jax.ShapeDtypeStruct((s, BRANCH_W), BF16),
        compiler_params=_cparams("parallel"),
    )(p, p, ln_g, ln_b, w, b_col)


def _sgu_bwd(p, dyb, ln_g, ln_b, w, w_t, b_col, *, name):
    s = p.shape[0]
    tr = _pick(s, 512, SGU_LEN)

    def body(u_ref, v_ref, dy_ref, g_ref, b_ref, w_ref, wt_ref, bc_ref,
             dz_ref, dg_ref, db_ref, dw_ref, dbc_ref, dvn_s):
        @pl.when(pl.program_id(0) == 0)
        def _():
            dg_ref[...] = jnp.zeros_like(dg_ref)
            db_ref[...] = jnp.zeros_like(db_ref)
            dw_ref[...] = jnp.zeros_like(dw_ref)
            dbc_ref[...] = jnp.zeros_like(dbc_ref)

        mask = _chunk_mask()
        mask_t = _chunk_mask(transposed=True)
        u_raw = u_ref[...]
        v_raw = v_ref[...]
        dy = dy_ref[...]
        zu = _gelu(u_raw)
        xh, rs, vn = _sgu_norm(v_raw, g_ref[...], b_ref[...])
        vnb = vn.astype(BF16)
        dvm_all = dy * zu
        for gi in range(SGU_GROUPS):
            wg = jnp.where(mask, w_ref[gi], 0.0).astype(BF16)
            wgt = jnp.where(mask_t, wt_ref[gi], 0.0).astype(BF16)
            cs = slice(gi * SGU_LEN, (gi + 1) * SGU_LEN)
            dw_g = jnp.zeros((SGU_LEN, SGU_LEN), F32)
            db_g = jnp.zeros((SGU_LEN, 1), F32)
            for c in range(tr // SGU_LEN):
                rs_ = slice(c * SGU_LEN, (c + 1) * SGU_LEN)
                vm = _dot(wg, vnb[rs_, cs]) + bc_ref[gi]
                dz_ref[rs_, cs] = (dy[rs_, cs] * vm * _gelu_grad(u_raw[rs_, cs])).astype(BF16)
                dvm = dvm_all[rs_, cs]
                dvmb = dvm.astype(BF16)
                dw_g = dw_g + _dot_nt(dvmb, vnb[rs_, cs])
                db_g = db_g + jnp.sum(dvm, axis=1, keepdims=True)
                dvn_s[rs_, cs] = _dot(wgt, dvmb)
            dw_ref[gi] += jnp.where(mask, dw_g, 0.0)
            dbc_ref[gi] += db_g
        dvn = dvn_s[...]
        dg_ref[...] += jnp.sum(dvn * xh, axis=0, keepdims=True)
        db_ref[...] += jnp.sum(dvn, axis=0, keepdims=True)
        dxh = dvn * g_ref[...]
        dzv = rs * (dxh - jnp.mean(dxh, axis=-1, keepdims=True) - xh * jnp.mean(dxh * xh, axis=-1, keepdims=True))
        dz_ref[:, BRANCH_W:] = (dzv * _gelu_grad(v_raw)).astype(BF16)

    vec = pl.BlockSpec((1, BRANCH_W), lambda i: (0, 0))
    wspec = pl.BlockSpec((SGU_GROUPS, SGU_LEN, SGU_LEN), lambda i: (0, 0, 0))
    bspec = pl.BlockSpec((SGU_GROUPS, SGU_LEN, 1), lambda i: (0, 0, 0))
    return pl.pallas_call(
        body, name=name, grid=(s // tr,),
        in_specs=[pl.BlockSpec((tr, BRANCH_W), lambda i: (i, 3)), pl.BlockSpec((tr, BRANCH_W), lambda i: (i, 4)),
                  pl.BlockSpec((tr, BRANCH_W), lambda i: (i, 0)), vec, vec, wspec, wspec, bspec],
        out_specs=[pl.BlockSpec((tr, 2 * BRANCH_W), lambda i: (i, 0)), vec, vec, wspec, bspec],
        out_shape=[jax.ShapeDtypeStruct((s, 2 * BRANCH_W), BF16),
                   jax.ShapeDtypeStruct((1, BRANCH_W), F32), jax.ShapeDtypeStruct((1, BRANCH_W), F32),
                   jax.ShapeDtypeStruct((SGU_GROUPS, SGU_LEN, SGU_LEN), F32),
                   jax.ShapeDtypeStruct((SGU_GROUPS, SGU_LEN, 1), F32)],
        scratch_shapes=[pltpu.VMEM((tr, BRANCH_W), F32)],
        compiler_params=_cparams("arbitrary"),
    )(p, p, dyb, ln_g, ln_b, w, w_t, b_col)


def _shift_down(x, prev8, k):
    rolled = pltpu.roll(x, k, 0)
    r8 = lax.broadcasted_iota(jnp.int32, prev8.shape, 0)
    head = jnp.where(r8 < k, pltpu.roll(prev8, k, 0), rolled[:HALO])
    return jnp.concatenate([head, rolled[HALO:]], axis=0)


def _shift_up(x, next8, k):
    n = x.shape[0]
    rolled = pltpu.roll(x, n - k, 0)
    r8 = lax.broadcasted_iota(jnp.int32, next8.shape, 0)
    tail = jnp.where(r8 >= HALO - k, pltpu.roll(next8, HALO - k, 0), rolled[n - HALO:])
    return jnp.concatenate([rolled[:n - HALO], tail], axis=0)


def _conv_specs(s, tr):
    nb = tr // HALO
    last = s // HALO - 1
    tile = lambda cb: pl.BlockSpec((tr, 128), lambda j, i: (i, cb * 4 + j))
    above = lambda cb: pl.BlockSpec((HALO, 128), lambda j, i: (jnp.maximum(i * nb - 1, 0), cb * 4 + j))
    below = lambda cb: pl.BlockSpec((HALO, 128), lambda j, i: (jnp.minimum((i + 1) * nb, last), cb * 4 + j))
    return tile, above, below


def _conv_fwd(p, cw, *, name):
    s = p.shape[0]
    tr = _pick(s, 1024, HALO)
    tile, above, _ = _conv_specs(s, tr)

    def body(cb_ref, cc_ref, cx_ref, ccp_ref, cxp_ref, w_ref, o_ref):
        first = pl.program_id(1) == 0
        y = cc_ref[...] * cx_ref[...]
        yp = jnp.where(first, 0.0, ccp_ref[...] * cxp_ref[...])
        conv = w_ref[2:3, :] * y + w_ref[1:2, :] * _shift_down(y, yp, 1) + w_ref[0:1, :] * _shift_down(y, yp, 2)
        o_ref[...] = (cb_ref[...] * conv).astype(BF16)

    return pl.pallas_call(
        body, name=name, grid=(4, s // tr),
        in_specs=[tile(5), tile(6), tile(7), above(6), above(7), pl.BlockSpec((3, 128), lambda j, i: (0, j))],
        out_specs=pl.BlockSpec((tr, 128), lambda j, i: (i, j)),
        out_shape=jax.ShapeDtypeStruct((s, BRANCH_W), BF16),
        compiler_params=_cparams("parallel", "parallel"),
    )(p, p, p, p, p, cw)


def _conv_bwd(p, dyc, cw, *, name):
    s = p.shape[0]
    tr = _pick(s, 1024, HALO)
    nt = s // tr
    nb = tr // HALO
    last = s // HALO - 1
    tile, above, below = _conv_specs(s, tr)

    def body(cb_ref, cc_ref, cx_ref, ccp_ref, cxp_ref, cbn_ref, dy_ref, dyn_ref, w_ref,
             dcb_ref, dcc_ref, dcx_ref, dw_ref):
        i = pl.program_id(1)

        @pl.when(i == 0)
        def _():
            dw_ref[...] = jnp.zeros_like(dw_ref)

        cb = cb_ref[...]
        cc = cc_ref[...]
        cx = cx_ref[...]
        y = cc * cx
        yp = jnp.where(i == 0, 0.0, ccp_ref[...] * cxp_ref[...])
        y1 = _shift_down(y, yp, 1)
        y2 = _shift_down(y, yp, 2)
        w0, w1, w2 = w_ref[0:1, :], w_ref[1:2, :], w_ref[2:3, :]
        conv = w2 * y + w1 * y1 + w0 * y2
        dyc_v = dy_ref[...]
        dconv = dyc_v * cb
        dn = jnp.where(i == nt - 1, 0.0, dyn_ref[...] * cbn_ref[...])
        dyv = w2 * dconv + w1 * _shift_up(dconv, dn, 1) + w0 * _shift_up(dconv, dn, 2)
        dcb_ref[...] = (dyc_v * conv).astype(BF16)
        dcc_ref[...] = (dyv * cx).astype(BF16)
        dcx_ref[...] = (dyv * cc).astype(BF16)
        dw_ref[0:1, :] += jnp.sum(dconv * y2, axis=0, keepdims=True)
        dw_ref[1:2, :] += jnp.sum(dconv * y1, axis=0, keepdims=True)
        dw_ref[2:3, :] += jnp.sum(dconv * y, axis=0, keepdims=True)

    dy_tile = pl.BlockSpec((tr, 128), lambda j, i: (i, j))
    dy_below = pl.BlockSpec((HALO, 128), lambda j, i: (jnp.minimum((i + 1) * nb, last), j))
    out_tile = lambda cb: pl.BlockSpec((tr, 128), lambda j, i: (i, cb * 4 + j))
    w_spec = pl.BlockSpec((3, 128), lambda j, i: (0, j))
    dcb, dcc, dcx, dw = pl.pallas_call(
        body, name=name, grid=(4, nt),
        in_specs=[tile(5), tile(6), tile(7), above(6), above(7), below(5), dy_tile, dy_below, w_spec],
        out_specs=[dy_tile, dy_tile, dy_tile, w_spec],
        out_shape=[jax.ShapeDtypeStruct((s, BRANCH_W), BF16)] * 3 + [jax.ShapeDtypeStruct((3, BRANCH_W), F32)],
        compiler_params=_cparams("parallel", "arbitrary"),
    )(p, p, p, p, p, p, dyc, dyc, cw)
    return dcb, dcc, dcx, dw


def _merge_fwd(ya, yb, yc, wb, p, *, name):
    s = p.shape[0]
    tr = _pick(s, 512, 16)

    def body(ya_ref, yb_ref, yc_ref, wb_ref, g0_ref, g1_ref, g2_ref, o_ref):
        acc = jnp.zeros((tr, D_MODEL), F32)
        for n, (y_ref, g_ref) in enumerate(((ya_ref, g0_ref), (yb_ref, g1_ref), (yc_ref, g2_ref))):
            acc = acc + _sigmoid(g_ref[...]) * _dot(y_ref[...].astype(BF16), wb_ref[n])
        o_ref[...] = acc.astype(BF16)

    yspec = pl.BlockSpec((tr, BRANCH_W), lambda i: (i, 0))
    gate = lambda n: pl.BlockSpec((tr, D_MODEL), lambda i: (i, 4 + n))
    return pl.pallas_call(
        body, name=name, grid=(s // tr,),
        in_specs=[yspec, yspec, yspec, pl.BlockSpec((3, BRANCH_W, D_MODEL), lambda i: (0, 0, 0)),
                  gate(0), gate(1), gate(2)],
        out_specs=pl.BlockSpec((tr, D_MODEL), lambda i: (i, 0)),
        out_shape=jax.ShapeDtypeStruct((s, D_MODEL), BF16),
        compiler_params=_cparams("parallel"),
    )(ya, yb, yc, wb, p, p, p)


def _merge_bwd(dm, ya, yb, yc, wb, p, *, name):
    s = p.shape[0]
    tr = _pick(s, 256, 16)

    def body(dm_ref, ya_ref, yb_ref, yc_ref, wb_ref, g0_ref, g1_ref, g2_ref,
             dya_ref, dyb_ref, dyc_ref, dg_ref, dbrd0_ref, dbrd1_ref, dbrd2_ref):
        dmv = dm_ref[...]
        ys = (ya_ref, yb_ref, yc_ref)
        gs = (g0_ref, g1_ref, g2_ref)
        dys = (dya_ref, dyb_ref, dyc_ref)
        dbrds = (dbrd0_ref, dbrd1_ref, dbrd2_ref)
        for n in range(3):
            brd = _dot(ys[n][...].astype(BF16), wb_ref[n])
            sg = _sigmoid(gs[n][...])
            dbrd = (sg * dmv).astype(BF16)
            dbrds[n][...] = dbrd
            dg_ref[:, n * D_MODEL:(n + 1) * D_MODEL] = (dmv * brd * sg * (1.0 - sg)).astype(BF16)
            dys[n][...] = _dot_nt(dbrd, wb_ref[n]).astype(dys[n].dtype)

    yspec = pl.BlockSpec((tr, BRANCH_W), lambda i: (i, 0))
    gate = lambda n: pl.BlockSpec((tr, D_MODEL), lambda i: (i, 4 + n))
    row = pl.BlockSpec((tr, D_MODEL), lambda i: (i, 0))
    return pl.pallas_call(
        body, name=name, grid=(s // tr,),
        in_specs=[row, yspec, yspec, yspec, pl.BlockSpec((3, BRANCH_W, D_MODEL), lambda i: (0, 0, 0)),
                  gate(0), gate(1), gate(2)],
        out_specs=[yspec, yspec, yspec, pl.BlockSpec((tr, 3 * D_MODEL), lambda i: (i, 0)), row, row, row],
        out_shape=[jax.ShapeDtypeStruct((s, BRANCH_W), BF16)] + [jax.ShapeDtypeStruct((s, BRANCH_W), F32)] * 2
                  + [jax.ShapeDtypeStruct((s, 3 * D_MODEL), BF16)] + [jax.ShapeDtypeStruct((s, D_MODEL), BF16)] * 3,
        compiler_params=_cparams("parallel"),
    )(dm, ya, yb, yc, wb, p, p, p)


def _xa_probs(q, k):
    sc = _dot_nt(q, k) * XA_SCALE
    e = jnp.exp(sc - jnp.max(sc, axis=-1, keepdims=True))
    return e / jnp.sum(e, axis=-1, keepdims=True)


def _xa_fwd(q, k, v, *, name):
    s = q.shape[0]
    mt = k.shape[0]
    tr = _pick(s, 2048, 16)

    def body(q_ref, k_ref, v_ref, o_ref):
        pr = _xa_probs(q_ref[...], k_ref[...])
        o_ref[...] = _dot(pr.astype(BF16), v_ref[...]).astype(BF16)

    qs = pl.BlockSpec((tr, XA_HEAD), lambda h, i: (i, h))
    ks = pl.BlockSpec((mt, XA_HEAD), lambda h, i: (0, h))
    return pl.pallas_call(
        body, name=name, grid=(D_MODEL // XA_HEAD, s // tr),
        in_specs=[qs, ks, ks], out_specs=qs,
        out_shape=jax.ShapeDtypeStruct((s, D_MODEL), BF16),
        compiler_params=_cparams("parallel", "parallel"),
    )(q, k, v)


def _xa_bwd(q, k, v, do, *, name):
    s = q.shape[0]
    mt = k.shape[0]
    tr = _pick(s, 2048, 16)

    def body(q_ref, k_ref, v_ref, do_ref, dq_ref, dk_ref, dv_ref):
        @pl.when(pl.program_id(1) == 0)
        def _():
            dk_ref[...] = jnp.zeros_like(dk_ref)
            dv_ref[...] = jnp.zeros_like(dv_ref)

        qv = q_ref[...]
        kv = k_ref[...]
        dov = do_ref[...]
        pr = _xa_probs(qv, kv)
        dpr = _dot_nt(dov, v_ref[...])
        ds = (pr * (dpr - jnp.sum(dpr * pr, axis=-1, keepdims=True)) * XA_SCALE).astype(BF16)
        dq_ref[...] = _dot(ds, kv).astype(BF16)
        dk_ref[...] += _dot_tn(ds, qv)
        dv_ref[...] += _dot_tn(pr.astype(BF16), dov)

    qs = pl.BlockSpec((tr, XA_HEAD), lambda h, i: (i, h))
    ks = pl.BlockSpec((mt, XA_HEAD), lambda h, i: (0, h))
    return pl.pallas_call(
        body, name=name, grid=(D_MODEL // XA_HEAD, s // tr),
        in_specs=[qs, ks, ks, qs], out_specs=[qs, ks, ks],
        out_shape=[jax.ShapeDtypeStruct((s, D_MODEL), BF16), jax.ShapeDtypeStruct((mt, D_MODEL), F32),
                   jax.ShapeDtypeStruct((mt, D_MODEL), F32)],
        compiler_params=_cparams("parallel", "arbitrary"),
    )(q, k, v, do)


def _ffn_in(h, wg, wu, *, name):
    s, d = h.shape
    f = wg.shape[0]
    tm = _pick(s, 1024, 128)
    tn = _pick(f, 1408, 128)

    def body(h_ref, wg_ref, wu_ref, a_ref, b_ref, o_ref):
        hv = h_ref[...]
        av = _dot_nt(hv, wg_ref[...])
        bv = _dot_nt(hv, wu_ref[...])
        a_ref[...] = av.astype(BF16)
        b_ref[...] = bv.astype(BF16)
        o_ref[...] = (av * _sigmoid(av) * bv).astype(BF16)

    wspec = pl.BlockSpec((tn, d), lambda i, j: (j, 0))
    tile = pl.BlockSpec((tm, tn), lambda i, j: (i, j))
    return pl.pallas_call(
        body, name=name, grid=(s // tm, f // tn),
        in_specs=[pl.BlockSpec((tm, d), lambda i, j: (i, 0)), wspec, wspec],
        out_specs=[tile, tile, tile], out_shape=[jax.ShapeDtypeStruct((s, f), BF16)] * 3,
        compiler_params=_cparams("parallel", "parallel"),
    )(h, wg, wu)


def _ffn_in_bwd(dx, wd, a, b, *, name):
    s, d = dx.shape
    f = wd.shape[0]
    tm = _pick(s, 1024, 128)
    tn = _pick(f, 1408, 128)

    def body(dx_ref, wd_ref, a_ref, b_ref, da_ref, db_ref):
        dhv = _dot_nt(dx_ref[...].astype(BF16), wd_ref[...])
        av = a_ref[...].astype(F32)
        sg = _sigmoid(av)
        silu = av * sg
        da_ref[...] = (dhv * b_ref[...].astype(F32) * (sg + silu * (1.0 - sg))).astype(BF16)
        db_ref[...] = (dhv * silu).astype(BF16)

    tile = pl.BlockSpec((tm, tn), lambda i, j: (i, j))
    return pl.pallas_call(
        body, name=name, grid=(s // tm, f // tn),
        in_specs=[pl.BlockSpec((tm, d), lambda i, j: (i, 0)), pl.BlockSpec((tn, d), lambda i, j: (j, 0)), tile, tile],
        out_specs=[tile, tile], out_shape=[jax.ShapeDtypeStruct((s, f), BF16)] * 2,
        compiler_params=_cparams("parallel", "parallel"),
    )(dx, wd, a, b)


def _adamw(w, g, m, v, *, name):
    r, c = w.shape
    tr = _pick(r, 512, 8)

    def body(w_ref, g_ref, m_ref, v_ref, d_ref, mo_ref, vo_ref):
        gv = g_ref[...]
        mn = ADAM_B1 * m_ref[...] + (1.0 - ADAM_B1) * gv
        vn = ADAM_B2 * v_ref[...] + (1.0 - ADAM_B2) * (gv * gv)
        m_hat = mn / (1.0 - ADAM_B1 ** ADAM_STEP)
        v_hat = vn / (1.0 - ADAM_B2 ** ADAM_STEP)
        d_ref[...] = -ADAM_LR * (m_hat / (jnp.sqrt(v_hat) + ADAM_EPS) + ADAM_WD * w_ref[...])
        mo_ref[...] = mn
        vo_ref[...] = vn

    spec = pl.BlockSpec((tr, c), lambda i: (i, 0))
    shp = jax.ShapeDtypeStruct((r, c), F32)
    return pl.pallas_call(
        body, name=name, grid=(r // tr,), in_specs=[spec] * 4, out_specs=[spec] * 3,
        out_shape=[shp] * 3, compiler_params=_cparams("parallel"),
    )(w, g, m, v)


def _position():
    return lax.axis_index("x"), lax.axis_index("y"), lax.axis_index("c")


def _all_gather(x, *, name):
    t, c_ = x.shape

    def body(x_ref, out_ref, send_sems, recv_sems, local_sem):
        start, forward, finish = _gather_phases(x_ref, out_ref, send_sems, recv_sems, local_sem)
        start()
        forward()
        finish()

    return pl.pallas_call(
        body, name=name,
        out_shape=jax.ShapeDtypeStruct((N_DEV, t, c_), x.dtype),
        in_specs=[pl.BlockSpec(memory_space=pl.ANY)],
        out_specs=pl.BlockSpec(memory_space=pl.ANY),
        scratch_shapes=_GATHER_SEMS,
    )(x)


_GATHER_SEMS = [pltpu.SemaphoreType.DMA((7,)), pltpu.SemaphoreType.DMA((7,)), pltpu.SemaphoreType.DMA]


def _gather_phases(x_ref, out_ref, send_sems, recv_sems, local_sem):
    x_, y_, c = _position()
    me, sibling = (x_, y_, c), (x_, y_, 1 - c)
    chips = [(1 - x_, y_), (x_, 1 - y_), (1 - x_, 1 - y_)]

    def block(px, py, pc):
        return out_ref.at[4 * px + 2 * py + pc]

    def copy(k, blk, to, src=None):
        return pltpu.make_async_remote_copy(
            src_ref=block(*blk) if src is None else src, dst_ref=block(*blk),
            send_sem=send_sems.at[k], recv_sem=recv_sems.at[k], device_id=to, device_id_type=MESH)

    mine = pltpu.make_async_copy(x_ref, block(*me), local_sem)
    first = [copy(0, me, sibling, src=x_ref)]
    first += [copy(1 + j, me, (*chip, c), src=x_ref) for j, chip in enumerate(chips)]
    passed = [copy(4 + j, (*chip, c), sibling) for j, chip in enumerate(chips)]

    def start():
        mine.start()
        for cp in first:
            cp.start()

    def forward():
        for j, chip in enumerate(chips):
            copy(1 + j, (*chip, c), me).wait_recv()
            passed[j].start()

    def finish():
        copy(0, sibling, me).wait_recv()
        for j, chip in enumerate(chips):
            copy(4 + j, (*chip, 1 - c), me).wait_recv()
        for cp in first + passed:
            cp.wait_send()
        mine.wait()

    return start, forward, finish


_SCATTER_SEMS = [pltpu.SemaphoreType.DMA((7,)), pltpu.SemaphoreType.DMA((7,)), pltpu.SemaphoreType.DMA]


def _scatter_phases(g_ref, r_ref, send_sems, recv_sems, local_sem):
    x_, y_, c = _position()
    me = 4 * x_ + 2 * y_ + c
    local = pltpu.make_async_copy(g_ref.at[me], r_ref.at[me], local_sem)
    copies = []
    for k in range(1, N_DEV):
        to = (x_ ^ (k >> 2), y_ ^ ((k >> 1) & 1), c ^ (k & 1))
        copies.append(pltpu.make_async_remote_copy(
            src_ref=g_ref.at[me ^ k], dst_ref=r_ref.at[me], send_sem=send_sems.at[k - 1],
            recv_sem=recv_sems.at[k - 1], device_id=to, device_id_type=MESH))

    def start():
        local.start()
        for cp in copies:
            cp.start()

    def finish():
        for k in range(1, N_DEV):
            pltpu.make_async_remote_copy(
                src_ref=g_ref.at[me], dst_ref=r_ref.at[me ^ k], send_sem=send_sems.at[k - 1],
                recv_sem=recv_sems.at[k - 1], device_id=(x_, y_, c), device_id_type=MESH).wait_recv()
        for cp in copies:
            cp.wait_send()
        local.wait()

    return start, finish


def _sum_devices(r8, *, name):
    _, t, c_ = r8.shape
    tr = _pick(t, 512, 16)

    def body(r_ref, o_ref):
        acc = r_ref[0].astype(F32)
        for d in range(1, N_DEV):
            acc = acc + r_ref[d].astype(F32)
        o_ref[...] = acc

    return pl.pallas_call(
        body, name=name, grid=(t // tr,),
        in_specs=[pl.BlockSpec((N_DEV, tr, c_), lambda i: (0, i, 0))],
        out_specs=pl.BlockSpec((tr, c_), lambda i: (i, 0)),
        out_shape=jax.ShapeDtypeStruct((t, c_), F32),
        compiler_params=_cparams("parallel"),
    )(r8)


def _all_reduce_small(x, *, name):
    r, c_ = x.shape

    def body(x_ref, o_ref, buf, send_sems, recv_sems):
        x_, y_, c = _position()
        me = 4 * x_ + 2 * y_ + c
        buf[me] = x_ref[...]
        copies = []
        for k in range(1, N_DEV):
            to = (x_ ^ (k >> 2), y_ ^ ((k >> 1) & 1), c ^ (k & 1))
            copies.append(pltpu.make_async_remote_copy(
                src_ref=x_ref, dst_ref=buf.at[me], send_sem=send_sems.at[k - 1], recv_sem=recv_sems.at[k - 1],
                device_id=to, device_id_type=MESH))
        for cp in copies:
            cp.start()
        for k in range(1, N_DEV):
            src = me ^ k
            pltpu.make_async_remote_copy(
                src_ref=x_ref, dst_ref=buf.at[src], send_sem=send_sems.at[k - 1], recv_sem=recv_sems.at[k - 1],
                device_id=(x_, y_, c), device_id_type=MESH).wait_recv()
        for cp in copies:
            cp.wait_send()
        acc = buf[0]
        for d in range(1, N_DEV):
            acc = acc + buf[d]
        o_ref[...] = acc

    return pl.pallas_call(
        body, name=name,
        out_shape=jax.ShapeDtypeStruct((r, c_), F32),
        in_specs=[pl.BlockSpec(memory_space=pltpu.VMEM)],
        out_specs=pl.BlockSpec(memory_space=pltpu.VMEM),
        scratch_shapes=[pltpu.VMEM((N_DEV, r, c_), F32), pltpu.SemaphoreType.DMA((7,)), pltpu.SemaphoreType.DMA((7,))],
    )(x)


def _rs_pair_exchange(g8, *, name):
    _, t, c_ = g8.shape

    def body(g_ref, r_ref, send_sems, recv_sems):
        x_, y_, c = _position()
        copies = [pltpu.make_async_remote_copy(
            src_ref=g_ref.at[2 * ch + (1 - c)], dst_ref=r_ref.at[ch],
            send_sem=send_sems.at[ch], recv_sem=recv_sems.at[ch],
            device_id=(x_, y_, 1 - c), device_id_type=MESH) for ch in range(4)]
        for cp in copies:
            cp.start()
        for cp in copies:
            cp.wait()

    return pl.pallas_call(
        body, name=name,
        out_shape=jax.ShapeDtypeStruct((4, t, c_), g8.dtype),
        in_specs=[pl.BlockSpec(memory_space=pl.ANY)],
        out_specs=pl.BlockSpec(memory_space=pl.ANY),
        scratch_shapes=[pltpu.SemaphoreType.DMA((4,)), pltpu.SemaphoreType.DMA((4,))],
    )(g8)


def _pair_add(core, g8, recv, *, name):
    _, t, c_ = g8.shape
    tr = _pick(t, 512, 16)

    def body(core_ref, g_ref, r_ref, o_ref):
        o_ref[...] = (g_ref[...].astype(F32) + r_ref[...].astype(F32)).astype(o_ref.dtype)

    grid_spec = pltpu.PrefetchScalarGridSpec(
        num_scalar_prefetch=1, grid=(4, t // tr),
        in_specs=[pl.BlockSpec((None, tr, c_), lambda ch, i, core_ref: (2 * ch + core_ref[0], i, 0)),
                  pl.BlockSpec((None, tr, c_), lambda ch, i, core_ref: (ch, i, 0))],
        out_specs=pl.BlockSpec((None, tr, c_), lambda ch, i, core_ref: (ch, i, 0)))
    return pl.pallas_call(
        body, name=name, grid_spec=grid_spec,
        out_shape=jax.ShapeDtypeStruct((4, t, c_), g8.dtype),
        compiler_params=_cparams("parallel", "parallel"),
    )(core, g8, recv)


def _rs_chip_exchange(part, *, name):
    _, t, c_ = part.shape

    def body(p_ref, r_ref, send_sems, recv_sems, local_sem):
        x_, y_, c = _position()
        mine = 2 * x_ + y_
        local = pltpu.make_async_copy(p_ref.at[mine], r_ref.at[mine], local_sem)
        local.start()
        chips = [(1 - x_, y_), (x_, 1 - y_), (1 - x_, 1 - y_)]
        copies = [pltpu.make_async_remote_copy(
            src_ref=p_ref.at[2 * px + py], dst_ref=r_ref.at[mine],
            send_sem=send_sems.at[k], recv_sem=recv_sems.at[k],
            device_id=(px, py, c), device_id_type=MESH) for k, (px, py) in enumerate(chips)]
        for cp in copies:
            cp.start()
        for k, (px, py) in enumerate(chips):
            pltpu.make_async_remote_copy(
                src_ref=p_ref.at[mine], dst_ref=r_ref.at[2 * px + py],
                send_sem=send_sems.at[k], recv_sem=recv_sems.at[k],
                device_id=(x_, y_, c), device_id_type=MESH).wait_recv()
        for cp in copies:
            cp.wait_send()
        local.wait()

    return pl.pallas_call(
        body, name=name,
        out_shape=jax.ShapeDtypeStruct((4, t, c_), part.dtype),
        in_specs=[pl.BlockSpec(memory_space=pl.ANY)],
        out_specs=pl.BlockSpec(memory_space=pl.ANY),
        scratch_shapes=[pltpu.SemaphoreType.DMA((3,)), pltpu.SemaphoreType.DMA((3,)), pltpu.SemaphoreType.DMA],
    )(part)


def _sum_chips(r4, *, name):
    _, t, c_ = r4.shape
    tr = _pick(t, 512, 16)

    def body(r_ref, o_ref):
        acc = r_ref[0].astype(F32)
        for ch in range(1, 4):
            acc = acc + r_ref[ch].astype(F32)
        o_ref[...] = acc

    return pl.pallas_call(
        body, name=name, grid=(t // tr,),
        in_specs=[pl.BlockSpec((4, tr, c_), lambda i: (0, i, 0))],
        out_specs=pl.BlockSpec((tr, c_), lambda i: (i, 0)),
        out_shape=jax.ShapeDtypeStruct((t, c_), F32),
        compiler_params=_cparams("parallel"),
    )(r4)


BIG = (
    ("w_in", (IN_COLS // N_DEV, D_MODEL), 0),
    ("w_branch", (3, BRANCH_W, D_MODEL // N_DEV), 2),
    ("w_out", (D_MODEL // N_DEV, D_MODEL), 0),
    ("w_q_xa", (D_MODEL // N_DEV, D_MODEL), 0),
    ("w_k_xa", (D_MODEL // N_DEV, D_MODEL), 0),
    ("w_v_xa", (D_MODEL // N_DEV, D_MODEL), 0),
    ("w_o_xa", (D_MODEL // N_DEV, D_MODEL), 0),
    ("w_gate_ffn", (FFN // N_DEV, D_MODEL), 0),
    ("w_up_ffn", (FFN // N_DEV, D_MODEL), 0),
    ("w_down_ffn", (FFN // N_DEV, D_MODEL), 0),
)
TRANSPOSED = ("w_in", "w_gate_ffn", "w_up_ffn")
_BIG_LAYOUT = {n: (shp, ax) for n, shp, ax in BIG}
PACK_COLS = 1024


def _stored(name, shard):
    return shard.T if name in TRANSPOSED else shard


def _size(shape):
    n = 1
    for d in shape:
        n *= d
    return n


def _pack_shards(items, shards):
    return jnp.concatenate([shards[it].reshape(-1, PACK_COLS) for it in items], axis=0)


def _unpack_gathered(items, g):
    out = {}
    r0 = 0
    for it in items:
        shp, ax = _BIG_LAYOUT[it[0]]
        rows = _size(shp) // PACK_COLS
        blk = g[:, r0:r0 + rows].reshape((N_DEV,) + shp)
        r0 += rows
        blk = jnp.moveaxis(blk, 0, ax)
        full = list(shp)
        full[ax] = shp[ax] * N_DEV
        out[it] = blk.reshape(full)
    return out


def _pack_full(items, full):
    parts = []
    for it in items:
        shp, ax = _BIG_LAYOUT[it[0]]
        t = full[it].reshape(shp[:ax] + (N_DEV, shp[ax]) + shp[ax + 1:])
        t = jnp.moveaxis(t, ax, 0)
        parts.append(t.reshape(N_DEV, -1, PACK_COLS))
    rows = sum(part.shape[1] for part in parts)
    if rows % 128:
        parts.append(jnp.zeros((N_DEV, 128 - rows % 128, PACK_COLS), parts[0].dtype))
    return jnp.concatenate(parts, axis=1)


def _unpack_shard(items, flat):
    out = {}
    r0 = 0
    for it in items:
        shp, _ = _BIG_LAYOUT[it[0]]
        rows = _size(shp) // PACK_COLS
        out[it] = flat[r0:r0 + rows].reshape(shp)
        r0 += rows
    return out


SMALL = (
    ("norm_mix_g", (DEPTH, D_MODEL)),
    ("sgu_ln_g", (DEPTH, BRANCH_W)),
    ("sgu_ln_b", (DEPTH, BRANCH_W)),
    ("w_spatial", (DEPTH, SGU_GROUPS, SGU_LEN, SGU_LEN)),
    ("b_spatial", (DEPTH, SGU_GROUPS, SGU_LEN)),
    ("conv_w", (DEPTH, 3, BRANCH_W)),
    ("norm_xa_g", (DEPTH, D_MODEL)),
    ("mem_norm_g", (DEPTH, D_MODEL)),
    ("norm_ffn_g", (DEPTH, D_MODEL)),
    ("final_g", (D_MODEL,)),
)


def _pack_small(grads):
    flat = jnp.concatenate([grads[n].reshape(-1) for n, _ in SMALL])
    rows = -(-flat.shape[0] // PACK_COLS)
    rows = -(-rows // 8) * 8
    flat = jnp.pad(flat, (0, rows * PACK_COLS - flat.shape[0]))
    return flat.reshape(rows, PACK_COLS)


def _unpack_small(buf):
    flat = buf.reshape(-1)
    out = {}
    o = 0
    for n, shp in SMALL:
        out[n] = flat[o:o + _size(shp)].reshape(shp)
        o += _size(shp)
    return out


def _layer_fwd(l, x, mem, wt, sm, gather=None):
    t = f"l{l}_"
    sv = {"x0": x}
    h = _rms_fwd(x, sm["norm_mix_g"][l][None], name=t + "rms_mix")
    p = _mm(h, wt["w_in", l], tb=True, name=t + "in_proj", tm=2048)
    if gather is None:
        ya, *tables = _sb_fwd(p, name=t + "sb_fwd")
    else:
        ya, *tables, gathered = _sb_fwd(p, name=t + "sb_fwd", gather=gather[1])
        wt.update(_unpack_gathered(gather[0], gathered))
    w_sp = sm["w_spatial"][l]
    b_col = sm["b_spatial"][l][:, :, None]
    ln_g, ln_b = sm["sgu_ln_g"][l][None], sm["sgu_ln_b"][l][None]
    yb = _sgu_fwd(p, ln_g, ln_b, w_sp, b_col, name=t + "sgu_fwd")
    yc = _conv_fwd(p, sm["conv_w"][l], name=t + "conv_fwd")
    merged = _merge_fwd(ya, yb, yc, wt["w_branch", l], p, name=t + "merge_fwd")
    x1 = _mm(merged, wt["w_out", l], add=x, name=t + "out_proj")
    sv.update(h=h, p=p, ya=ya, yb=yb, yc=yc, merged=merged, x1=x1, tables=tables)

    h2 = _rms_fwd(x1, sm["norm_xa_g"][l][None], name=t + "rms_xa")
    mn = _rms_fwd(mem, sm["mem_norm_g"][l][None], name=t + "rms_mem")
    q = _mm(h2, wt["w_q_xa", l], out_dtype=BF16, name=t + "xa_q", tm=2048)
    k = _mm(mn, wt["w_k_xa", l], out_dtype=BF16, name=t + "xa_k")
    v = _mm(mn, wt["w_v_xa", l], out_dtype=BF16, name=t + "xa_v")
    o = _xa_fwd(q, k, v, name=t + "xa_fwd")
    x2 = _mm(o, wt["w_o_xa", l], add=x1, name=t + "xa_o")
    sv.update(h2=h2, mn=mn, q=q, k=k, v=v, o=o, x2=x2)

    h3 = _rms_fwd(x2, sm["norm_ffn_g"][l][None], name=t + "rms_ffn")
    a, b, hd = _ffn_in(h3, wt["w_gate_ffn", l], wt["w_up_ffn", l], name=t + "ffn_in")
    x3 = _mm(hd, wt["w_down_ffn", l], add=x2, name=t + "ffn_down", tk=FFN)
    sv.update(h3=h3, a=a, b=b, hd=hd)
    return x3, sv


def _layer_bwd(l, dx3, mem, wt, sm, sv, scatter=None):
    t = f"l{l}_b_"
    gb, gs = {}, {}
    gb["w_down_ffn"] = _mm(sv["hd"], dx3, ta=True, out_dtype=BF16, name=t + "ffn_down_dw", tm=1408, tk=1024)
    da, db = _ffn_in_bwd(dx3, wt["w_down_ffn", l], sv["a"], sv["b"], name=t + "ffn_in_bwd")
    gb["w_gate_ffn"] = _mm(da, sv["h3"], ta=True, out_dtype=BF16, name=t + "ffn_gate_dw", tm=1408, tk=1024)
    gb["w_up_ffn"] = _mm(db, sv["h3"], ta=True, out_dtype=BF16, name=t + "ffn_up_dw", tm=1408, tk=1024)
    dh3 = _mm(da, wt["w_gate_ffn", l], name=t + "ffn_gate_dx", tk=1408)
    dx2, dg = _mm(db, wt["w_up_ffn", l], add=dh3, rms=(sv["x2"], sm["norm_ffn_g"][l][None], dx3),
                  name=t + "ffn_up_dx", tm=512, tk=1408)
    gs["norm_ffn_g"] = dg[0]
    do = _mm(dx2, wt["w_o_xa", l], tb=True, out_dtype=BF16, name=t + "xa_o_dx")
    gb["w_o_xa"] = _mm(sv["o"], dx2, ta=True, out_dtype=BF16, name=t + "xa_o_dw")
    dq, dk, dv = _xa_bwd(sv["q"], sv["k"], sv["v"], do, name=t + "xa_bwd")
    dx1, dg = _mm(dq, wt["w_q_xa", l], tb=True, rms=(sv["x1"], sm["norm_xa_g"][l][None], dx2),
                  name=t + "xa_q_dx", tm=1024)
    gs["norm_xa_g"] = dg[0]
    gb["w_q_xa"] = _mm(sv["h2"], dq, ta=True, out_dtype=BF16, name=t + "xa_q_dw")
    gb["w_k_xa"] = _mm(sv["mn"], dk, ta=True, out_dtype=BF16, name=t + "xa_k_dw")
    gb["w_v_xa"] = _mm(sv["mn"], dv, ta=True, out_dtype=BF16, name=t + "xa_v_dw")
    dmn = _mm(dk, wt["w_k_xa", l], tb=True, name=t + "xa_k_dx")
    dmn = _mm(dv, wt["w_v_xa", l], tb=True, add=dmn, name=t + "xa_v_dx")
    _, dg = _rms_bwd(mem, sm["mem_norm_g"][l][None], dmn, jnp.zeros_like(mem), name=t + "rms_mem")
    gs["mem_norm_g"] = dg[0]
    dm = _mm(dx1, wt["w_out", l], tb=True, name=t + "out_proj_dx")
    gb["w_out"] = _mm(sv["merged"], dx1, ta=True, out_dtype=BF16, name=t + "out_proj_dw")
    p = sv["p"]
    dya, dyb, dyc, dgates, *dbrd = _merge_bwd(dm, sv["ya"], sv["yb"], sv["yc"], wt["w_branch", l], p,
                                              name=t + "merge_bwd")
    gb["w_branch"] = jnp.stack([
        _mm(sv[y], dbrd[n], ta=True, out_dtype=BF16, name=t + f"branch{n}_dw")
        for n, y in enumerate(("ya", "yb", "yc"))])
    dcb, dcc, dcx, dcw = _conv_bwd(p, dyc, sm["conv_w"][l], name=t + "conv_bwd")
    gs["conv_w"] = dcw
    w_sp = sm["w_spatial"][l]
    dz, dlg, dlb, dwsp, dbsp = _sgu_bwd(p, dyb, sm["sgu_ln_g"][l][None], sm["sgu_ln_b"][l][None], w_sp,
                                        jnp.swapaxes(w_sp, 1, 2), sm["b_spatial"][l][:, :, None],
                                        name=t + "sgu_bwd")
    gs.update(sgu_ln_g=dlg[0], sgu_ln_b=dlb[0], w_spatial=dwsp, b_spatial=dbsp[:, :, 0])
    received = None
    if scatter is None:
        dq_a, dk_a, dv_a = _sb_bwd(p, dya, sv["tables"], name=t + "sb_bwd")
    else:
        items, earlier = scatter
        ready = {**earlier, **{(n, l): g for n, g in gb.items()}}
        dq_a, dk_a, dv_a, received = _sb_bwd(p, dya, sv["tables"], name=t + "sb_bwd",
                                             scatter=_pack_full(items, ready))
    dp = jnp.concatenate([dq_a, dk_a, dv_a, dz, dcb, dcc, dcx, dgates], axis=1)
    gb["w_in"] = _mm(dp, sv["h"], ta=True, out_dtype=BF16, name=t + "in_proj_dw")
    dx, dg = _mm(dp, wt["w_in", l], rms=(sv["x0"], sm["norm_mix_g"][l][None], dx1),
                 name=t + "in_proj_dx", tm=1024, tk=1792)
    gs["norm_mix_g"] = dg[0]
    return dx, gb, gs, received


_WEIGHTS = ("norm_mix_g", "w_in", "sgu_ln_g", "sgu_ln_b", "w_spatial", "b_spatial", "conv_w", "w_branch", "w_out",
            "norm_xa_g", "mem_norm_g", "w_q_xa", "w_k_xa", "w_v_xa", "w_o_xa", "norm_ffn_g", "w_gate_ffn",
            "w_up_ffn", "w_down_ffn", "final_g")


def kernel(x, mem, norm_mix_g, w_in, sgu_ln_g, sgu_ln_b, w_spatial, b_spatial, conv_w, w_branch, w_out, norm_xa_g, mem_norm_g, w_q_xa, w_k_xa, w_v_xa, w_o_xa, norm_ffn_g, w_gate_ffn, w_up_ffn, w_down_ffn, final_g, loss_target, m_norm_mix_g, m_w_in, m_sgu_ln_g, m_sgu_ln_b, m_w_spatial, m_b_spatial, m_conv_w, m_w_branch, m_w_out, m_norm_xa_g, m_mem_norm_g, m_w_q_xa, m_w_k_xa, m_w_v_xa, m_w_o_xa, m_norm_ffn_g, m_w_gate_ffn, m_w_up_ffn, m_w_down_ffn, m_final_g, v_norm_mix_g, v_w_in, v_sgu_ln_g, v_sgu_ln_b, v_w_spatial, v_b_spatial, v_conv_w, v_w_branch, v_w_out, v_norm_xa_g, v_mem_norm_g, v_w_q_xa, v_w_k_xa, v_w_v_xa, v_w_o_xa, v_norm_ffn_g, v_w_gate_ffn, v_w_up_ffn, v_w_down_ffn, v_final_g):
    w = dict(norm_mix_g=norm_mix_g, w_in=w_in, sgu_ln_g=sgu_ln_g, sgu_ln_b=sgu_ln_b, w_spatial=w_spatial,
             b_spatial=b_spatial, conv_w=conv_w, w_branch=w_branch, w_out=w_out, norm_xa_g=norm_xa_g,
             mem_norm_g=mem_norm_g, w_q_xa=w_q_xa, w_k_xa=w_k_xa, w_v_xa=w_v_xa, w_o_xa=w_o_xa,
             norm_ffn_g=norm_ffn_g, w_gate_ffn=w_gate_ffn, w_up_ffn=w_up_ffn, w_down_ffn=w_down_ffn, final_g=final_g)
    m = dict(norm_mix_g=m_norm_mix_g, w_in=m_w_in, sgu_ln_g=m_sgu_ln_g, sgu_ln_b=m_sgu_ln_b, w_spatial=m_w_spatial,
             b_spatial=m_b_spatial, conv_w=m_conv_w, w_branch=m_w_branch, w_out=m_w_out, norm_xa_g=m_norm_xa_g,
             mem_norm_g=m_mem_norm_g, w_q_xa=m_w_q_xa, w_k_xa=m_w_k_xa, w_v_xa=m_w_v_xa, w_o_xa=m_w_o_xa,
             norm_ffn_g=m_norm_ffn_g, w_gate_ffn=m_w_gate_ffn, w_up_ffn=m_w_up_ffn, w_down_ffn=m_w_down_ffn,
             final_g=m_final_g)
    v = dict(norm_mix_g=v_norm_mix_g, w_in=v_w_in, sgu_ln_g=v_sgu_ln_g, sgu_ln_b=v_sgu_ln_b, w_spatial=v_w_spatial,
             b_spatial=v_b_spatial, conv_w=v_conv_w, w_branch=v_w_branch, w_out=v_w_out, norm_xa_g=v_norm_xa_g,
             mem_norm_g=v_mem_norm_g, w_q_xa=v_w_q_xa, w_k_xa=v_w_k_xa, w_v_xa=v_w_v_xa, w_o_xa=v_w_o_xa,
             norm_ffn_g=v_norm_ffn_g, w_gate_ffn=v_w_gate_ffn, w_up_ffn=v_w_up_ffn, w_down_ffn=v_w_down_ffn,
             final_g=v_final_g)

    names = [n for n, _, _ in BIG]
    shards = {(n, l): _stored(n, w[n][l].astype(BF16)) for n in names for l in range(DEPTH)}
    first_items = [("w_in", 0)]
    mid_items = [(n, 0) for n in names if n != "w_in"] + [("w_in", 1)]
    last_items = [(n, 1) for n in names if n != "w_in"]
    wt = _unpack_gathered(first_items, _all_gather(_pack_shards(first_items, shards), name="gather_w_in0"))
    cw_pad = jnp.zeros((8, 128), F32).at[:DEPTH * 3, :BRANCH_W // N_DEV].set(conv_w.reshape(DEPTH * 3, -1))
    cw_all = _all_gather(cw_pad, name="gather_conv_w")[:, :DEPTH * 3, :BRANCH_W // N_DEV]
    conv_full = jnp.moveaxis(cw_all.reshape(N_DEV, DEPTH, 3, BRANCH_W // N_DEV), 0, 2).reshape(DEPTH, 3, BRANCH_W)
    sm = {n: w[n] for n, _ in SMALL}
    sm["conv_w"] = conv_full

    xs, ms = x[0], mem[0]
    x1, saved0 = _layer_fwd(0, xs, ms, wt, sm, gather=(mid_items, _pack_shards(mid_items, shards)))
    x2, saved1 = _layer_fwd(1, x1, ms, wt, sm, gather=(last_items, _pack_shards(last_items, shards)))
    dcur, loss, dfinal = _final_loss(x2, sm["final_g"][None], loss_target[0], name="final_loss")
    loss = lax.psum(loss[0, 0], AXES)
    items_a = [(n, 1) for n in names if n != "w_in"]
    items_b = [("w_in", 1)] + [(n, 0) for n in names if n != "w_in"]
    items_c = [("w_in", 0)]
    dcur, gb1, gs1, recv_a = _layer_bwd(1, dcur, ms, wt, sm, saved1, scatter=(items_a, {}))
    dx, gb0, gs0, recv_b = _layer_bwd(0, dcur, ms, wt, sm, saved0, scatter=(items_b, {("w_in", 1): gb1["w_in"]}))

    shard_grads = _unpack_shard(items_a, _sum_devices(recv_a, name="rs_sum_a"))
    shard_grads.update(_unpack_shard(items_b, _sum_devices(recv_b, name="rs_sum_b")))
    g8 = _pack_full(items_c, {("w_in", 0): gb0["w_in"]})
    core = lax.axis_index("c").astype(jnp.int32).reshape(1)
    from_sibling = _rs_pair_exchange(g8, name="rs_pair_exchange")
    part = _pair_add(core, g8, from_sibling, name="rs_pair_add")
    by_chip = _rs_chip_exchange(part, name="rs_chip_exchange")
    shard_grads.update(_unpack_shard(items_c, _sum_chips(by_chip, name="rs_sum_chips")))
    grads = {n: jnp.stack([_stored(n, shard_grads[n, l]) for l in range(DEPTH)]) for n in names}
    small = {n: jnp.stack([gs0[n], gs1[n]]) for n, _ in SMALL if n != "final_g"}
    small["final_g"] = dfinal[0]
    small_sum = _unpack_small(_all_reduce_small(_pack_small(small), name="all_reduce_small"))
    width = BRANCH_W // N_DEV
    dev = 4 * lax.axis_index("x") + 2 * lax.axis_index("y") + lax.axis_index("c")
    for n, _ in SMALL:
        grads[n] = small_sum[n]
    grads["conv_w"] = lax.dynamic_slice_in_dim(small_sum["conv_w"], dev * width, width, axis=2)

    delta, new_m, new_v = {}, {}, {}
    for n in _WEIGHTS:
        shp = w[n].shape
        two_d = (-1, shp[-1])
        d_, m_, v_ = _adamw(w[n].reshape(two_d), grads[n].reshape(two_d), m[n].reshape(two_d), v[n].reshape(two_d),
                            name="adamw_" + n)
        delta[n], new_m[n], new_v[n] = d_.reshape(shp), m_.reshape(shp), v_.reshape(shp)

    return (loss, dx[None], *[grads[n] for n in _WEIGHTS], *[delta[n] for n in _WEIGHTS],
            *[new_m[n] for n in _WEIGHTS], *[new_v[n] for n in _WEIGHTS])
```

```python
import jax
import jax.numpy as jnp
from jax import lax
from jax.experimental import pallas as pl
from jax.experimental.pallas import tpu as pltpu

F32 = jnp.float32
BF16 = jnp.bfloat16
MESH = pl.DeviceIdType.MESH

D_MODEL = 1024
BRANCH_W = 512
IN_COLS = 7168
FFN = 2816
N_DEV = 8
DEPTH = 2
SB_BLOCK = 128
SB_SPAN = 1024
SB_Q_FWD = 512
SB_Q_BWD = 512
SB_SCALE = 0.125
XA_HEAD = 256
XA_SCALE = 0.0625
SGU_LEN = 128
SGU_GROUPS = 4
RMS_EPS = 1e-6
LN_EPS = 1e-5
HALO = 8

ADAM_LR = 0.001
ADAM_B1 = 0.9
ADAM_B2 = 0.999
ADAM_EPS = 1e-08
ADAM_WD = 0.01
ADAM_STEP = 10

VMEM_LIMIT_BYTES = 52 * 1024 * 1024

AXES = ("x", "y", "c")


def _cparams(*sem):
    return pltpu.CompilerParams(dimension_semantics=sem, vmem_limit_bytes=VMEM_LIMIT_BYTES)


def _pick(n, target, align):
    t = (min(target, n) // align) * align
    while t >= align:
        if n % t == 0:
            return t
        t -= align
    return n


def _dot(a, b):
    return jnp.dot(a, b, preferred_element_type=F32)


def _dot_nt(a, b):
    return lax.dot_general(a, b, (((1,), (1,)), ((), ())), preferred_element_type=F32)


def _dot_tn(a, b):
    return lax.dot_general(a, b, (((0,), (0,)), ((), ())), preferred_element_type=F32)


def _sigmoid(x):
    return 1.0 / (1.0 + jnp.exp(-x))


def _mm(a, b, *, name, ta=False, tb=False, out_dtype=F32, add=None, rms=None, scatter=None,
        tm=1024, tn=1024, tk=2048):
    m, k = (a.shape[1], a.shape[0]) if ta else a.shape
    n = b.shape[0] if tb else b.shape[1]
    assert k == (b.shape[1] if tb else b.shape[0])
    tm = _pick(m, tm, 128)
    tn = n if rms is not None else _pick(n, tn, 128)
    tk = _pick(k, tk, 128)
    nk = k // tk
    grid = (m // tm, n // tn, nk)
    ca = 0 if ta else 1
    cb = 1 if tb else 0
    n_add = 0 if add is None else 1
    n_rms = 0 if rms is None else 3
    n_sc = 0 if scatter is None else 1
    n_in = 2 + n_add + n_rms + n_sc
    n_out = 1 + (1 if rms is not None else 0) + n_sc

    def body(*refs):
        refs = list(refs)
        a_ref, b_ref = refs[:2]
        extra = refs[2:2 + n_add + n_rms]
        outs = refs[n_in:n_in + n_out]
        scratch = refs[n_in + n_out:]
        o_ref = outs[0]
        kk = pl.program_id(2)
        first_row_tile = pl.program_id(0) == 0
        if scatter is not None:
            start, finish_scatter = _scatter_phases(refs[n_in - 1], outs[-1], *scratch[-3:])
            step = (pl.program_id(0) * grid[1] + pl.program_id(1)) * grid[2] + kk
            pl.when(step == 0)(start)

        def product():
            return lax.dot_general(a_ref[...].astype(BF16), b_ref[...].astype(BF16),
                                   (((ca,), (cb,)), ((), ())), preferred_element_type=F32)

        def finish(r):
            if add is not None:
                r = r + extra[0][...]
            if rms is None:
                o_ref[...] = r.astype(out_dtype)
                return
            x_ref, g_ref, dres_ref = extra[n_add:]
            dg_ref = outs[1]

            @pl.when(first_row_tile)
            def _():
                dg_ref[...] = jnp.zeros_like(dg_ref)

            xv = x_ref[...]
            rs = lax.rsqrt(jnp.mean(xv * xv, axis=-1, keepdims=True) + RMS_EPS)
            xh = xv * rs
            dg_ref[...] += jnp.sum(r * xh, axis=0, keepdims=True)
            dxh = r * g_ref[...]
            o_ref[...] = dres_ref[...] + rs * (dxh - xh * jnp.mean(dxh * xh, axis=-1, keepdims=True))

        if nk == 1:
            finish(product())
        else:
            acc_ref = scratch[0]

            @pl.when(kk == 0)
            def _():
                acc_ref[...] = jnp.zeros_like(acc_ref)

            acc_ref[...] += product()

            @pl.when(kk == nk - 1)
            def _():
                finish(acc_ref[...])

        if scatter is not None:
            pl.when(step == grid[0] * grid[1] * grid[2] - 1)(finish_scatter)

    a_spec = pl.BlockSpec((tk, tm), lambda i, j, kk: (kk, i)) if ta else pl.BlockSpec((tm, tk), lambda i, j, kk: (i, kk))
    b_spec = pl.BlockSpec((tn, tk), lambda i, j, kk: (j, kk)) if tb else pl.BlockSpec((tk, tn), lambda i, j, kk: (kk, j))
    tile = pl.BlockSpec((tm, tn), lambda i, j, kk: (i, j))
    in_specs = [a_spec, b_spec]
    operands = [a, b]
    out_specs = [tile]
    out_shape = [jax.ShapeDtypeStruct((m, n), out_dtype)]
    if add is not None:
        in_specs.append(tile)
        operands.append(add)
    if rms is not None:
        vec = pl.BlockSpec((1, n), lambda i, j, kk: (0, 0))
        in_specs += [tile, vec, tile]
        operands += list(rms)
        out_specs.append(vec)
        out_shape = [jax.ShapeDtypeStruct((m, n), F32), jax.ShapeDtypeStruct((1, n), F32)]
    scratch_shapes = [pltpu.VMEM((tm, tn), F32)] if nk > 1 else []
    semantics = ("arbitrary" if rms is not None else "parallel", "parallel", "arbitrary")
    if scatter is not None:
        in_specs.append(pl.BlockSpec(memory_space=pl.ANY))
        operands.append(scatter)
        out_specs.append(pl.BlockSpec(memory_space=pl.ANY))
        out_shape.append(jax.ShapeDtypeStruct(scatter.shape, scatter.dtype))
        scratch_shapes = scratch_shapes + _SCATTER_SEMS
        semantics = ("arbitrary", "arbitrary", "arbitrary")
    out = pl.pallas_call(
        body, name=name, grid=grid,
        in_specs=in_specs, out_specs=out_specs, out_shape=out_shape,
        scratch_shapes=scratch_shapes, compiler_params=_cparams(*semantics),
    )(*operands)
    return out[0] if len(out) == 1 else out


def _rms_fwd(x, g, *, name):
    r, d = x.shape
    tr = _pick(r, 512, 16)

    def body(x_ref, g_ref, o_ref):
        xv = x_ref[...]
        rs = lax.rsqrt(jnp.mean(xv * xv, axis=-1, keepdims=True) + RMS_EPS)
        o_ref[...] = (xv * rs * g_ref[...]).astype(BF16)

    return pl.pallas_call(
        body, name=name, grid=(r // tr,),
        in_specs=[pl.BlockSpec((tr, d), lambda i: (i, 0)), pl.BlockSpec((1, d), lambda i: (0, 0))],
        out_specs=pl.BlockSpec((tr, d), lambda i: (i, 0)),
        out_shape=jax.ShapeDtypeStruct((r, d), BF16),
        compiler_params=_cparams("parallel"),
    )(x, g)


def _rms_bwd(x, g, dh, dres, *, name):
    r, d = x.shape
    tr = _pick(r, 256, 8)

    def body(x_ref, g_ref, dh_ref, dres_ref, dx_ref, dg_ref):
        @pl.when(pl.program_id(0) == 0)
        def _():
            dg_ref[...] = jnp.zeros_like(dg_ref)

        xv = x_ref[...]
        dhv = dh_ref[...].astype(F32)
        rs = lax.rsqrt(jnp.mean(xv * xv, axis=-1, keepdims=True) + RMS_EPS)
        xh = xv * rs
        dg_ref[...] += jnp.sum(dhv * xh, axis=0, keepdims=True)
        dxh = dhv * g_ref[...]
        dx_ref[...] = dres_ref[...] + rs * (dxh - xh * jnp.mean(dxh * xh, axis=-1, keepdims=True))

    return pl.pallas_call(
        body, name=name, grid=(r // tr,),
        in_specs=[pl.BlockSpec((tr, d), lambda i: (i, 0)), pl.BlockSpec((1, d), lambda i: (0, 0)),
                  pl.BlockSpec((tr, d), lambda i: (i, 0)), pl.BlockSpec((tr, d), lambda i: (i, 0))],
        out_specs=[pl.BlockSpec((tr, d), lambda i: (i, 0)), pl.BlockSpec((1, d), lambda i: (0, 0))],
        out_shape=[jax.ShapeDtypeStruct((r, d), F32), jax.ShapeDtypeStruct((1, d), F32)],
        compiler_params=_cparams("arbitrary"),
    )(x, g, dh, dres)


def _final_loss(x, g, target, *, name):
    r, d = x.shape
    tr = _pick(r, 512, 8)

    def body(x_ref, g_ref, t_ref, dx_ref, loss_ref, dg_ref):
        @pl.when(pl.program_id(0) == 0)
        def _():
            dg_ref[...] = jnp.zeros_like(dg_ref)
            loss_ref[...] = jnp.zeros_like(loss_ref)

        xv = x_ref[...]
        gv = g_ref[...]
        rs = lax.rsqrt(jnp.mean(xv * xv, axis=-1, keepdims=True) + RMS_EPS)
        xh = xv * rs
        err = xh * gv - t_ref[...]
        row_loss = jnp.mean(err * err, axis=-1, keepdims=True)
        loss_ref[...] += 0.5 * jnp.sum(row_loss, axis=0, keepdims=True)
        dy = err * (1.0 / d)
        dg_ref[...] += jnp.sum(dy * xh, axis=0, keepdims=True)
        dxh = dy * gv
        dx_ref[...] = rs * (dxh - xh * jnp.mean(dxh * xh, axis=-1, keepdims=True))

    return pl.pallas_call(
        body, name=name, grid=(r // tr,),
        in_specs=[pl.BlockSpec((tr, d), lambda i: (i, 0)), pl.BlockSpec((1, d), lambda i: (0, 0)),
                  pl.BlockSpec((tr, d), lambda i: (i, 0))],
        out_specs=[pl.BlockSpec((tr, d), lambda i: (i, 0)), pl.BlockSpec((1, 128), lambda i: (0, 0)),
                   pl.BlockSpec((1, d), lambda i: (0, 0))],
        out_shape=[jax.ShapeDtypeStruct((r, d), F32), jax.ShapeDtypeStruct((1, 128), F32),
                   jax.ShapeDtypeStruct((1, d), F32)],
        compiler_params=_cparams("arbitrary"),
    )(x, g, target)


def _cumsum_operand(strict_after, totals=True):
    width = (2 if totals else 1) * SB_BLOCK
    r = lax.broadcasted_iota(jnp.int32, (SB_BLOCK, width), 0)
    c = lax.broadcasted_iota(jnp.int32, (SB_BLOCK, width), 1)
    tri = (r > c) if strict_after else (r < c)
    return jnp.where((c >= SB_BLOCK) | tri, 1.0, 0.0).astype(BF16)


def _sb_scores(qh, kw, run, valid, after_ones):
    nb = kw.shape[0] // SB_BLOCK
    z = _dot_nt(qh, kw)
    lsp = jnp.minimum(z, 0.0) - jnp.log(1.0 + jnp.exp(-jnp.abs(z)))
    l1m = lsp - z
    if valid is not None:
        l1m = jnp.where(valid, l1m, 0.0)
    l1b = l1m.astype(BF16)
    later = [None] * nb
    seen = [None] * nb
    for b in reversed(range(nb)):
        cols = slice(b * SB_BLOCK, (b + 1) * SB_BLOCK)
        ct = _dot(l1b[:, cols], after_ones)
        seen[b] = run
        later[b] = run + ct[:, :SB_BLOCK]
        run = run + ct[:, SB_BLOCK:]
    a = jnp.exp(lsp + jnp.concatenate(later, axis=1))
    if valid is not None:
        a = jnp.where(valid, a, 0.0)
    return a, run, seen


def _sb_setup(q_ref, span):
    qi = pl.program_id(1)
    rows = q_ref.shape[0]
    sd = (qi * rows + rows - 1) // span
    lane = lax.broadcasted_iota(jnp.int32, (rows, SB_BLOCK), 1)
    col = lax.broadcasted_iota(jnp.int32, (rows, span), 1)
    row = lax.broadcasted_iota(jnp.int32, (rows, span), 0)
    valid = col < (qi * rows - sd * span) + row
    q = q_ref[...] * SB_SCALE
    qhs = (jnp.where(lane < 64, q, 0.0).astype(BF16), jnp.where(lane >= 64, q, 0.0).astype(BF16))
    return lane, sd, valid, qhs


def _sb_fwd(p, *, name, gather=None):
    s = p.shape[0]
    qrows = min(SB_Q_FWD, s)
    nq = s // qrows
    kcol = BRANCH_W // SB_BLOCK
    span = min(SB_SPAN, s)
    per = span // SB_BLOCK
    assert s // SB_BLOCK <= SB_BLOCK

    def body(*refs):
        if gather is None:
            q_ref, k_ref, v_ref, o_ref, r0_ref, r1_ref = refs
        else:
            q_ref, k_ref, v_ref, x_ref, o_ref, r0_ref, r1_ref, g_ref, send_sems, recv_sems, local_sem = refs
            start, forward, finish = _gather_phases(x_ref, g_ref, send_sems, recv_sems, local_sem)
            step = pl.program_id(0) * nq + pl.program_id(1)
            pl.when(step == 0)(start)
        lane, sd, valid, qhs = _sb_setup(q_ref, span)
        after_ones = _cumsum_operand(True)
        zero = jnp.zeros((qrows, SB_BLOCK), F32)
        lane_row = lax.broadcasted_iota(jnp.int32, (1, SB_BLOCK), 1)

        def span_step(sb, carry, mask):
            rows = pl.ds(pl.multiple_of(sb * span, span), span)
            kw = k_ref[rows, :].astype(BF16)
            vw = v_ref[rows, :].astype(BF16)
            out = []
            for h in range(2):
                run, acc, table = carry[h]
                a, run, seen = _sb_scores(qhs[h], kw, run, mask, after_ones)
                for b in range(per):
                    table = jnp.where(lane_row == sb * per + b, seen[b], table)
                out.append((run, acc + _dot(a.astype(BF16), vw), table))
            return tuple(out)

        carry = span_step(sd, ((zero, zero, zero), (zero, zero, zero)), valid)
        carry = lax.fori_loop(0, sd, lambda t, c: span_step(sd - 1 - t, c, None), carry)
        o_ref[...] = jnp.where(lane < 64, carry[0][1], carry[1][1]).astype(BF16)
        r0_ref[...] = carry[0][2]
        r1_ref[...] = carry[1][2]
        if gather is not None:
            pl.when(step == (kcol - 1) * nq + (3 * nq) // 4)(forward)
            pl.when(step == kcol * nq - 1)(finish)

    in_specs = [pl.BlockSpec((qrows, SB_BLOCK), lambda hp, qi: (qi, hp)),
                pl.BlockSpec((s, SB_BLOCK), lambda hp, qi: (0, kcol + hp)),
                pl.BlockSpec((s, SB_BLOCK), lambda hp, qi: (0, 2 * kcol + hp))]
    table = pl.BlockSpec((None, qrows, SB_BLOCK), lambda hp, qi: (hp, qi, 0))
    out_specs = [pl.BlockSpec((qrows, SB_BLOCK), lambda hp, qi: (qi, hp)), table, table]
    out_shape = [jax.ShapeDtypeStruct((s, BRANCH_W), BF16)] + [jax.ShapeDtypeStruct((kcol, s, SB_BLOCK), F32)] * 2
    operands = [p, p, p]
    scratch = []
    if gather is not None:
        in_specs.append(pl.BlockSpec(memory_space=pl.ANY))
        out_specs.append(pl.BlockSpec(memory_space=pl.ANY))
        out_shape.append(jax.ShapeDtypeStruct((N_DEV,) + gather.shape, gather.dtype))
        operands.append(gather)
        scratch = _GATHER_SEMS
    out = pl.pallas_call(
        body, name=name, grid=(kcol, nq), in_specs=in_specs, out_specs=out_specs, out_shape=out_shape,
        scratch_shapes=scratch, compiler_params=_cparams("arbitrary", "arbitrary"),
    )(*operands)
    return out


def _sb_bwd(p, dya, tables, *, name, scatter=None):
    s = p.shape[0]
    qrows = min(SB_Q_BWD, s)
    nq = s // qrows
    kcol = BRANCH_W // SB_BLOCK
    span = min(SB_SPAN, s)
    per = span // SB_BLOCK

    def body(*refs):
        if scatter is None:
            q_ref, k_ref, v_ref, do_ref, t0_ref, t1_ref, dq_ref, dk_ref, dv_ref, dk_acc, dv_acc = refs
        else:
            (q_ref, k_ref, v_ref, do_ref, t0_ref, t1_ref, g_ref, dq_ref, dk_ref, dv_ref, r_ref,
             dk_acc, dv_acc, send_sems, recv_sems, local_sem) = refs
            start, finish = _scatter_phases(g_ref, r_ref, send_sems, recv_sems, local_sem)
            step = pl.program_id(0) * nq + pl.program_id(1)
            pl.when(step == 0)(start)
        qi = pl.program_id(1)

        @pl.when(qi == 0)
        def _():
            dk_acc[...] = jnp.zeros_like(dk_acc)
            dv_acc[...] = jnp.zeros_like(dv_acc)

        lane, sd, valid, qhs = _sb_setup(q_ref, span)
        after = _cumsum_operand(True, totals=False)
        before_ones = _cumsum_operand(False)
        do = do_ref[...]
        dohs = (jnp.where(lane < 64, do, 0.0).astype(BF16), jnp.where(lane >= 64, do, 0.0).astype(BF16))
        tabs = (t0_ref[...], t1_ref[...])
        lane_row = lax.broadcasted_iota(jnp.int32, (1, SB_BLOCK), 1)
        zero = jnp.zeros((qrows, SB_BLOCK), F32)

        def span_step(sb, carry, mask):
            rows = pl.ds(pl.multiple_of(sb * span, span), span)
            kw = k_ref[rows, :].astype(BF16)
            vw = v_ref[rows, :].astype(BF16)
            out = []
            dk_span = jnp.zeros((span, SB_BLOCK), F32)
            dv_span = jnp.zeros((span, SB_BLOCK), F32)
            for h in range(2):
                pg, dq = carry[h]
                z = _dot_nt(qhs[h], kw)
                lsp = jnp.minimum(z, 0.0) - jnp.log(1.0 + jnp.exp(-jnp.abs(z)))
                l1m = lsp - z
                if mask is not None:
                    l1m = jnp.where(mask, l1m, 0.0)
                l1b = l1m.astype(BF16)
                later = [None] * per
                for b in range(per):
                    cols = slice(b * SB_BLOCK, (b + 1) * SB_BLOCK)
                    seen = jnp.sum(jnp.where(lane_row == sb * per + b, tabs[h], 0.0), axis=-1, keepdims=True)
                    later[b] = seen + _dot(l1b[:, cols], after)
                a = jnp.exp(lsp + jnp.concatenate(later, axis=1))
                beta = jnp.exp(lsp)
                if mask is not None:
                    a = jnp.where(mask, a, 0.0)
                    beta = jnp.where(mask, beta, 0.0)
                g = a * _dot_nt(dohs[h], vw)
                gb = g.astype(BF16)
                before = [None] * per
                for b in range(per):
                    cols = slice(b * SB_BLOCK, (b + 1) * SB_BLOCK)
                    gt = _dot(gb[:, cols], before_ones)
                    before[b] = pg + gt[:, :SB_BLOCK]
                    pg = pg + gt[:, SB_BLOCK:]
                dz = (g * (1.0 - beta) - beta * jnp.concatenate(before, axis=1)).astype(BF16)
                dk_span = dk_span + _dot_tn(dz, qhs[h])
                dv_span = dv_span + _dot_tn(a.astype(BF16), dohs[h])
                out.append((pg, dq + _dot(dz, kw)))
            dk_acc[rows, :] += dk_span
            dv_acc[rows, :] += dv_span
            return tuple(out)

        carry = lax.fori_loop(0, sd, lambda sb, c: span_step(sb, c, None), ((zero, zero), (zero, zero)))
        carry = span_step(sd, carry, valid)
        dq_ref[...] = (jnp.where(lane < 64, carry[0][1], carry[1][1]) * SB_SCALE).astype(BF16)

        @pl.when(qi == nq - 1)
        def _():
            dk_ref[...] = dk_acc[...].astype(BF16)
            dv_ref[...] = dv_acc[...].astype(BF16)

        if scatter is not None:
            pl.when(step == kcol * nq - 1)(finish)

    blk = pl.BlockSpec((qrows, SB_BLOCK), lambda hp, qi: (qi, hp))
    col = pl.BlockSpec((s, SB_BLOCK), lambda hp, qi: (0, hp))
    table = pl.BlockSpec((None, qrows, SB_BLOCK), lambda hp, qi: (hp, qi, 0))
    out = jax.ShapeDtypeStruct((s, BRANCH_W), BF16)
    in_specs = [blk,
                pl.BlockSpec((s, SB_BLOCK), lambda hp, qi: (0, kcol + hp)),
                pl.BlockSpec((s, SB_BLOCK), lambda hp, qi: (0, 2 * kcol + hp)),
                blk, table, table]
    out_specs = [blk, col, col]
    out_shape = [out, out, out]
    operands = [p, p, p, dya, tables[0], tables[1]]
    scratch = [pltpu.VMEM((s, SB_BLOCK), F32), pltpu.VMEM((s, SB_BLOCK), F32)]
    if scatter is not None:
        in_specs.append(pl.BlockSpec(memory_space=pl.ANY))
        out_specs.append(pl.BlockSpec(memory_space=pl.ANY))
        out_shape.append(jax.ShapeDtypeStruct(scatter.shape, scatter.dtype))
        operands.append(scatter)
        scratch = scratch + _SCATTER_SEMS
    return pl.pallas_call(
        body, name=name, grid=(kcol, nq), in_specs=in_specs, out_specs=out_specs, out_shape=out_shape,
        scratch_shapes=scratch, compiler_params=_cparams("arbitrary", "arbitrary"),
    )(*operands)


_INV_SQRT2 = 0.7071067811865476
_INV_SQRT2PI = 0.3989422804014327


def _gelu(x):
    return 0.5 * x * (1.0 + lax.erf(x * _INV_SQRT2))


def _gelu_grad(x):
    return 0.5 * (1.0 + lax.erf(x * _INV_SQRT2)) + x * _INV_SQRT2PI * jnp.exp(-0.5 * x * x)


def _chunk_mask(transposed=False):
    r = lax.broadcasted_iota(jnp.int32, (SGU_LEN, SGU_LEN), 0)
    c = lax.broadcasted_iota(jnp.int32, (SGU_LEN, SGU_LEN), 1)
    return (c // 64) >= (r // 64) if transposed else (r // 64) >= (c // 64)


def _sgu_norm(v_raw, g, b):
    zv = _gelu(v_raw)
    xc = zv - jnp.mean(zv, axis=-1, keepdims=True)
    rs = lax.rsqrt(jnp.mean(xc * xc, axis=-1, keepdims=True) + LN_EPS)
    xh = xc * rs
    return xh, rs, xh * g + b


def _sgu_fwd(p, ln_g, ln_b, w, b_col, *, name):
    s = p.shape[0]
    tr = _pick(s, 512, SGU_LEN)

    def body(u_ref, v_ref, g_ref, b_ref, w_ref, bc_ref, o_ref):
        mask = _chunk_mask()
        zu = _gelu(u_ref[...])
        _, _, vn = _sgu_norm(v_ref[...], g_ref[...], b_ref[...])
        vnb = vn.astype(BF16)
        for gi in range(SGU_GROUPS):
            wg = jnp.where(mask, w_ref[gi], 0.0).astype(BF16)
            cs = slice(gi * SGU_LEN, (gi + 1) * SGU_LEN)
            for c in range(tr // SGU_LEN):
                rs_ = slice(c * SGU_LEN, (c + 1) * SGU_LEN)
                vm = _dot(wg, vnb[rs_, cs]) + bc_ref[gi]
                o_ref[rs_, cs] = (zu[rs_, cs] * vm).astype(BF16)

    vec = pl.BlockSpec((1, BRANCH_W), lambda i: (0, 0))
    return pl.pallas_call(
        body, name=name, grid=(s // tr,),
        in_specs=[pl.BlockSpec((tr, BRANCH_W), lambda i: (i, 3)), pl.BlockSpec((tr, BRANCH_W), lambda i: (i, 4)),
                  vec, vec,
                  pl.BlockSpec((SGU_GROUPS, SGU_LEN, SGU_LEN), lambda i: (0, 0, 0)),
                  pl.BlockSpec((SGU_GROUPS, SGU_LEN, 1), lambda i: (0, 0, 0))],
        out_specs=pl.BlockSpec((tr, BRANCH_W), lambda i: (i, 0)),
        out_shape=jax.ShapeDtypeStruct((s, BRANCH_W), BF16),
        compiler_params=_cparams("parallel"),
    )(p, p, ln_g, ln_b, w, b_col)


def _sgu_bwd(p, dyb, ln_g, ln_b, w, w_t, b_col, *, name):
    s = p.shape[0]
    tr = _pick(s, 512, SGU_LEN)

    def body(u_ref, v_ref, dy_ref, g_ref, b_ref, w_ref, wt_ref, bc_ref,
             dz_ref, dg_ref, db_ref, dw_ref, dbc_ref, dvn_s):
        @pl.when(pl.program_id(0) == 0)
        def _():
            dg_ref[...] = jnp.zeros_like(dg_ref)
            db_ref[...] = jnp.zeros_like(db_ref)
            dw_ref[...] = jnp.zeros_like(dw_ref)
            dbc_ref[...] = jnp.zeros_like(dbc_ref)

        mask = _chunk_mask()
        mask_t = _chunk_mask(transposed=True)
        u_raw = u_ref[...]
        v_raw = v_ref[...]
        dy = dy_ref[...]
        zu = _gelu(u_raw)
        xh, rs, vn = _sgu_norm(v_raw, g_ref[...], b_ref[...])
        vnb = vn.astype(BF16)
        dvm_all = dy * zu
        for gi in range(SGU_GROUPS):
            wg = jnp.where(mask, w_ref[gi], 0.0).astype(BF16)
            wgt = jnp.where(mask_t, wt_ref[gi], 0.0).astype(BF16)
            cs = slice(gi * SGU_LEN, (gi + 1) * SGU_LEN)
            dw_g = jnp.zeros((SGU_LEN, SGU_LEN), F32)
            db_g = jnp.zeros((SGU_LEN, 1), F32)
            for c in range(tr // SGU_LEN):
                rs_ = slice(c * SGU_LEN, (c + 1) * SGU_LEN)
                vm = _dot(wg, vnb[rs_, cs]) + bc_ref[gi]
                dz_ref[rs_, cs] = (dy[rs_, cs] * vm * _gelu_grad(u_raw[rs_, cs])).astype(BF16)
                dvm = dvm_all[rs_, cs]
                dvmb = dvm.astype(BF16)
                dw_g = dw_g + _dot_nt(dvmb, vnb[rs_, cs])
                db_g = db_g + jnp.sum(dvm, axis=1, keepdims=True)
                dvn_s[rs_, cs] = _dot(wgt, dvmb)
            dw_ref[gi] += jnp.where(mask, dw_g, 0.0)
            dbc_ref[gi] += db_g
        dvn = dvn_s[...]
        dg_ref[...] += jnp.sum(dvn * xh, axis=0, keepdims=True)
        db_ref[...] += jnp.sum(dvn, axis=0, keepdims=True)
        dxh = dvn * g_ref[...]
        dzv = rs * (dxh - jnp.mean(dxh, axis=-1, keepdims=True) - xh * jnp.mean(dxh * xh, axis=-1, keepdims=True))
        dz_ref[:, BRANCH_W:] = (dzv * _gelu_grad(v_raw)).astype(BF16)

    vec = pl.BlockSpec((1, BRANCH_W), lambda i: (0, 0))
    wspec = pl.BlockSpec((SGU_GROUPS, SGU_LEN, SGU_LEN), lambda i: (0, 0, 0))
    bspec = pl.BlockSpec((SGU_GROUPS, SGU_LEN, 1), lambda i: (0, 0, 0))
    return pl.pallas_call(
        body, name=name, grid=(s // tr,),
        in_specs=[pl.BlockSpec((tr, BRANCH_W), lambda i: (i, 3)), pl.BlockSpec((tr, BRANCH_W), lambda i: (i, 4)),
                  pl.BlockSpec((tr, BRANCH_W), lambda i: (i, 0)), vec, vec, wspec, wspec, bspec],
        out_specs=[pl.BlockSpec((tr, 2 * BRANCH_W), lambda i: (i, 0)), vec, vec, wspec, bspec],
        out_shape=[jax.ShapeDtypeStruct((s, 2 * BRANCH_W), BF16),
                   jax.ShapeDtypeStruct((1, BRANCH_W), F32), jax.ShapeDtypeStruct((1, BRANCH_W), F32),
                   jax.ShapeDtypeStruct((SGU_GROUPS, SGU_LEN, SGU_LEN), F32),
                   jax.ShapeDtypeStruct((SGU_GROUPS, SGU_LEN, 1), F32)],
        scratch_shapes=[pltpu.VMEM((tr, BRANCH_W), F32)],
        compiler_params=_cparams("arbitrary"),
    )(p, p, dyb, ln_g, ln_b, w, w_t, b_col)


def _shift_down(x, prev8, k):
    rolled = pltpu.roll(x, k, 0)
    r8 = lax.broadcasted_iota(jnp.int32, prev8.shape, 0)
    head = jnp.where(r8 < k, pltpu.roll(prev8, k, 0), rolled[:HALO])
    return jnp.concatenate([head, rolled[HALO:]], axis=0)


def _shift_up(x, next8, k):
    n = x.shape[0]
    rolled = pltpu.roll(x, n - k, 0)
    r8 = lax.broadcasted_iota(jnp.int32, next8.shape, 0)
    tail = jnp.where(r8 >= HALO - k, pltpu.roll(next8, HALO - k, 0), rolled[n - HALO:])
    return jnp.concatenate([rolled[:n - HALO], tail], axis=0)


def _conv_specs(s, tr):
    nb = tr // HALO
    last = s // HALO - 1
    tile = lambda cb: pl.BlockSpec((tr, 128), lambda j, i: (i, cb * 4 + j))
    above = lambda cb: pl.BlockSpec((HALO, 128), lambda j, i: (jnp.maximum(i * nb - 1, 0), cb * 4 + j))
    below = lambda cb: pl.BlockSpec((HALO, 128), lambda j, i: (jnp.minimum((i + 1) * nb, last), cb * 4 + j))
    return tile, above, below


def _conv_fwd(p, cw, *, name):
    s = p.shape[0]
    tr = _pick(s, 1024, HALO)
    tile, above, _ = _conv_specs(s, tr)

    def body(cb_ref, cc_ref, cx_ref, ccp_ref, cxp_ref, w_ref, o_ref):
        first = pl.program_id(1) == 0
        y = cc_ref[...] * cx_ref[...]
        yp = jnp.where(first, 0.0, ccp_ref[...] * cxp_ref[...])
        conv = w_ref[2:3, :] * y + w_ref[1:2, :] * _shift_down(y, yp, 1) + w_ref[0:1, :] * _shift_down(y, yp, 2)
        o_ref[...] = (cb_ref[...] * conv).astype(BF16)

    return pl.pallas_call(
        body, name=name, grid=(4, s // tr),
        in_specs=[tile(5), tile(6), tile(7), above(6), above(7), pl.BlockSpec((3, 128), lambda j, i: (0, j))],
        out_specs=pl.BlockSpec((tr, 128), lambda j, i: (i, j)),
        out_shape=jax.ShapeDtypeStruct((s, BRANCH_W), BF16),
        compiler_params=_cparams("parallel", "parallel"),
    )(p, p, p, p, p, cw)


def _conv_bwd(p, dyc, cw, *, name):
    s = p.shape[0]
    tr = _pick(s, 1024, HALO)
    nt = s // tr
    nb = tr // HALO
    last = s // HALO - 1
    tile, above, below = _conv_specs(s, tr)

    def body(cb_ref, cc_ref, cx_ref, ccp_ref, cxp_ref, cbn_ref, dy_ref, dyn_ref, w_ref,
             dcb_ref, dcc_ref, dcx_ref, dw_ref):
        i = pl.program_id(1)

        @pl.when(i == 0)
        def _():
            dw_ref[...] = jnp.zeros_like(dw_ref)

        cb = cb_ref[...]
        cc = cc_ref[...]
        cx = cx_ref[...]
        y = cc * cx
        yp = jnp.where(i == 0, 0.0, ccp_ref[...] * cxp_ref[...])
        y1 = _shift_down(y, yp, 1)
        y2 = _shift_down(y, yp, 2)
        w0, w1, w2 = w_ref[0:1, :], w_ref[1:2, :], w_ref[2:3, :]
        conv = w2 * y + w1 * y1 + w0 * y2
        dyc_v = dy_ref[...]
        dconv = dyc_v * cb
        dn = jnp.where(i == nt - 1, 0.0, dyn_ref[...] * cbn_ref[...])
        dyv = w2 * dconv + w1 * _shift_up(dconv, dn, 1) + w0 * _shift_up(dconv, dn, 2)
        dcb_ref[...] = (dyc_v * conv).astype(BF16)
        dcc_ref[...] = (dyv * cx).astype(BF16)
        dcx_ref[...] = (dyv * cc).astype(BF16)
        dw_ref[0:1, :] += jnp.sum(dconv * y2, axis=0, keepdims=True)
        dw_ref[1:2, :] += jnp.sum(dconv * y1, axis=0, keepdims=True)
        dw_ref[2:3, :] += jnp.sum(dconv * y, axis=0, keepdims=True)

    dy_tile = pl.BlockSpec((tr, 128), lambda j, i: (i, j))
    dy_below = pl.BlockSpec((HALO, 128), lambda j, i: (jnp.minimum((i + 1) * nb, last), j))
    out_tile = lambda cb: pl.BlockSpec((tr, 128), lambda j, i: (i, cb * 4 + j))
    w_spec = pl.BlockSpec((3, 128), lambda j, i: (0, j))
    dcb, dcc, dcx, dw = pl.pallas_call(
        body, name=name, grid=(4, nt),
        in_specs=[tile(5), tile(6), tile(7), above(6), above(7), below(5), dy_tile, dy_below, w_spec],
        out_specs=[dy_tile, dy_tile, dy_tile, w_spec],
        out_shape=[jax.ShapeDtypeStruct((s, BRANCH_W), BF16)] * 3 + [jax.ShapeDtypeStruct((3, BRANCH_W), F32)],
        compiler_params=_cparams("parallel", "arbitrary"),
    )(p, p, p, p, p, p, dyc, dyc, cw)
    return dcb, dcc, dcx, dw


def _merge_fwd(ya, yb, yc, wb, p, *, name):
    s = p.shape[0]
    tr = _pick(s, 512, 16)

    def body(ya_ref, yb_ref, yc_ref, wb_ref, g0_ref, g1_ref, g2_ref, o_ref):
        acc = jnp.zeros((tr, D_MODEL), F32)
        for n, (y_ref, g_ref) in enumerate(((ya_ref, g0_ref), (yb_ref, g1_ref), (yc_ref, g2_ref))):
            acc = acc + _sigmoid(g_ref[...]) * _dot(y_ref[...].astype(BF16), wb_ref[n])
        o_ref[...] = acc.astype(BF16)

    yspec = pl.BlockSpec((tr, BRANCH_W), lambda i: (i, 0))
    gate = lambda n: pl.BlockSpec((tr, D_MODEL), lambda i: (i, 4 + n))
    return pl.pallas_call(
        body, name=name, grid=(s // tr,),
        in_specs=[yspec, yspec, yspec, pl.BlockSpec((3, BRANCH_W, D_MODEL), lambda i: (0, 0, 0)),
                  gate(0), gate(1), gate(2)],
        out_specs=pl.BlockSpec((tr, D_MODEL), lambda i: (i, 0)),
        out_shape=jax.ShapeDtypeStruct((s, D_MODEL), BF16),
        compiler_params=_cparams("parallel"),
    )(ya, yb, yc, wb, p, p, p)


def _merge_bwd(dm, ya, yb, yc, wb, p, *, name):
    s = p.shape[0]
    tr = _pick(s, 256, 16)

    def body(dm_ref, ya_ref, yb_ref, yc_ref, wb_ref, g0_ref, g1_ref, g2_ref,
             dya_ref, dyb_ref, dyc_ref, dg_ref, dbrd0_ref, dbrd1_ref, dbrd2_ref):
        dmv = dm_ref[...]
        ys = (ya_ref, yb_ref, yc_ref)
        gs = (g0_ref, g1_ref, g2_ref)
        dys = (dya_ref, dyb_ref, dyc_ref)
        dbrds = (dbrd0_ref, dbrd1_ref, dbrd2_ref)
        for n in range(3):
            brd = _dot(ys[n][...].astype(BF16), wb_ref[n])
            sg = _sigmoid(gs[n][...])
            dbrd = (sg * dmv).astype(BF16)
            dbrds[n][...] = dbrd
            dg_ref[:, n * D_MODEL:(n + 1) * D_MODEL] = (dmv * brd * sg * (1.0 - sg)).astype(BF16)
            dys[n][...] = _dot_nt(dbrd, wb_ref[n]).astype(dys[n].dtype)

    yspec = pl.BlockSpec((tr, BRANCH_W), lambda i: (i, 0))
    gate = lambda n: pl.BlockSpec((tr, D_MODEL), lambda i: (i, 4 + n))
    row = pl.BlockSpec((tr, D_MODEL), lambda i: (i, 0))
    return pl.pallas_call(
        body, name=name, grid=(s // tr,),
        in_specs=[row, yspec, yspec, yspec, pl.BlockSpec((3, BRANCH_W, D_MODEL), lambda i: (0, 0, 0)),
                  gate(0), gate(1), gate(2)],
        out_specs=[yspec, yspec, yspec, pl.BlockSpec((tr, 3 * D_MODEL), lambda i: (i, 0)), row, row, row],
        out_shape=[jax.ShapeDtypeStruct((s, BRANCH_W), BF16)] + [jax.ShapeDtypeStruct((s, BRANCH_W), F32)] * 2
                  + [jax.ShapeDtypeStruct((s, 3 * D_MODEL), BF16)] + [jax.ShapeDtypeStruct((s, D_MODEL), BF16)] * 3,
        compiler_params=_cparams("parallel"),
    )(dm, ya, yb, yc, wb, p, p, p)


def _xa_probs(q, k):
    sc = _dot_nt(q, k) * XA_SCALE
    e = jnp.exp(sc - jnp.max(sc, axis=-1, keepdims=True))
    return e / jnp.sum(e, axis=-1, keepdims=True)


def _xa_fwd(q, k, v, *, name):
    s = q.shape[0]
    mt = k.shape[0]
    tr = _pick(s, 2048, 16)

    def body(q_ref, k_ref, v_ref, o_ref):
        pr = _xa_probs(q_ref[...], k_ref[...])
        o_ref[...] = _dot(pr.astype(BF16), v_ref[...]).astype(BF16)

    qs = pl.BlockSpec((tr, XA_HEAD), lambda h, i: (i, h))
    ks = pl.BlockSpec((mt, XA_HEAD), lambda h, i: (0, h))
    return pl.pallas_call(
        body, name=name, grid=(D_MODEL // XA_HEAD, s // tr),
        in_specs=[qs, ks, ks], out_specs=qs,
        out_shape=jax.ShapeDtypeStruct((s, D_MODEL), BF16),
        compiler_params=_cparams("parallel", "parallel"),
    )(q, k, v)


def _xa_bwd(q, k, v, do, *, name):
    s = q.shape[0]
    mt = k.shape[0]
    tr = _pick(s, 2048, 16)

    def body(q_ref, k_ref, v_ref, do_ref, dq_ref, dk_ref, dv_ref):
        @pl.when(pl.program_id(1) == 0)
        def _():
            dk_ref[...] = jnp.zeros_like(dk_ref)
            dv_ref[...] = jnp.zeros_like(dv_ref)

        qv = q_ref[...]
        kv = k_ref[...]
        dov = do_ref[...]
        pr = _xa_probs(qv, kv)
        dpr = _dot_nt(dov, v_ref[...])
        ds = (pr * (dpr - jnp.sum(dpr * pr, axis=-1, keepdims=True)) * XA_SCALE).astype(BF16)
        dq_ref[...] = _dot(ds, kv).astype(BF16)
        dk_ref[...] += _dot_tn(ds, qv)
        dv_ref[...] += _dot_tn(pr.astype(BF16), dov)

    qs = pl.BlockSpec((tr, XA_HEAD), lambda h, i: (i, h))
    ks = pl.BlockSpec((mt, XA_HEAD), lambda h, i: (0, h))
    return pl.pallas_call(
        body, name=name, grid=(D_MODEL // XA_HEAD, s // tr),
        in_specs=[qs, ks, ks, qs], out_specs=[qs, ks, ks],
        out_shape=[jax.ShapeDtypeStruct((s, D_MODEL), BF16), jax.ShapeDtypeStruct((mt, D_MODEL), F32),
                   jax.ShapeDtypeStruct((mt, D_MODEL), F32)],
        compiler_params=_cparams("parallel", "arbitrary"),
    )(q, k, v, do)


def _ffn_in(h, wg, wu, *, name):
    s, d = h.shape
    f = wg.shape[0]
    tm = _pick(s, 1024, 128)
    tn = _pick(f, 1408, 128)

    def body(h_ref, wg_ref, wu_ref, a_ref, b_ref, o_ref):
        hv = h_ref[...]
        av = _dot_nt(hv, wg_ref[...])
        bv = _dot_nt(hv, wu_ref[...])
        a_ref[...] = av.astype(BF16)
        b_ref[...] = bv.astype(BF16)
        o_ref[...] = (av * _sigmoid(av) * bv).astype(BF16)

    wspec = pl.BlockSpec((tn, d), lambda i, j: (j, 0))
    tile = pl.BlockSpec((tm, tn), lambda i, j: (i, j))
    return pl.pallas_call(
        body, name=name, grid=(s // tm, f // tn),
        in_specs=[pl.BlockSpec((tm, d), lambda i, j: (i, 0)), wspec, wspec],
        out_specs=[tile, tile, tile], out_shape=[jax.ShapeDtypeStruct((s, f), BF16)] * 3,
        compiler_params=_cparams("parallel", "parallel"),
    )(h, wg, wu)


def _ffn_in_bwd(dx, wd, a, b, *, name):
    s, d = dx.shape
    f = wd.shape[0]
    tm = _pick(s, 1024, 128)
    tn = _pick(f, 1408, 128)

    def body(dx_ref, wd_ref, a_ref, b_ref, da_ref, db_ref):
        dhv = _dot_nt(dx_ref[...].astype(BF16), wd_ref[...])
        av = a_ref[...].astype(F32)
        sg = _sigmoid(av)
        silu = av * sg
        da_ref[...] = (dhv * b_ref[...].astype(F32) * (sg + silu * (1.0 - sg))).astype(BF16)
        db_ref[...] = (dhv * silu).astype(BF16)

    tile = pl.BlockSpec((tm, tn), lambda i, j: (i, j))
    return pl.pallas_call(
        body, name=name, grid=(s // tm, f // tn),
        in_specs=[pl.BlockSpec((tm, d), lambda i, j: (i, 0)), pl.BlockSpec((tn, d), lambda i, j: (j, 0)), tile, tile],
        out_specs=[tile, tile], out_shape=[jax.ShapeDtypeStruct((s, f), BF16)] * 2,
        compiler_params=_cparams("parallel", "parallel"),
    )(dx, wd, a, b)


def _adamw(w, g, m, v, *, name):
    r, c = w.shape
    tr = _pick(r, 512, 8)

    def body(w_ref, g_ref, m_ref, v_ref, d_ref, mo_ref, vo_ref):
        gv = g_ref[...]
        mn = ADAM_B1 * m_ref[...] + (1.0 - ADAM_B1) * gv
        vn = ADAM_B2 * v_ref[...] + (1.0 - ADAM_B2) * (gv * gv)
        m_hat = mn / (1.0 - ADAM_B1 ** ADAM_STEP)
        v_hat = vn / (1.0 - ADAM_B2 ** ADAM_STEP)
        d_ref[...] = -ADAM_LR * (m_hat / (jnp.sqrt(v_hat) + ADAM_EPS) + ADAM_WD * w_ref[...])
        mo_ref[...] = mn
        vo_ref[...] = vn

    spec = pl.BlockSpec((tr, c), lambda i: (i, 0))
    shp = jax.ShapeDtypeStruct((r, c), F32)
    return pl.pallas_call(
        body, name=name, grid=(r // tr,), in_specs=[spec] * 4, out_specs=[spec] * 3,
        out_shape=[shp] * 3, compiler_params=_cparams("parallel"),
    )(w, g, m, v)


def _position():
    return lax.axis_index("x"), lax.axis_index("y"), lax.axis_index("c")


def _all_gather(x, *, name):
    t, c_ = x.shape

    def body(x_ref, out_ref, send_sems, recv_sems, local_sem):
        start, forward, finish = _gather_phases(x_ref, out_ref, send_sems, recv_sems, local_sem)
        start()
        forward()
        finish()

    return pl.pallas_call(
        body, name=name,
        out_shape=jax.ShapeDtypeStruct((N_DEV, t, c_), x.dtype),
        in_specs=[pl.BlockSpec(memory_space=pl.ANY)],
        out_specs=pl.BlockSpec(memory_space=pl.ANY),
        scratch_shapes=_GATHER_SEMS,
    )(x)


_GATHER_SEMS = [pltpu.SemaphoreType.DMA((7,)), pltpu.SemaphoreType.DMA((7,)), pltpu.SemaphoreType.DMA]


def _gather_phases(x_ref, out_ref, send_sems, recv_sems, local_sem):
    x_, y_, c = _position()
    me, sibling = (x_, y_, c), (x_, y_, 1 - c)
    chips = [(1 - x_, y_), (x_, 1 - y_), (1 - x_, 1 - y_)]

    def block(px, py, pc):
        return out_ref.at[4 * px + 2 * py + pc]

    def copy(k, blk, to, src=None):
        return pltpu.make_async_remote_copy(
            src_ref=block(*blk) if src is None else src, dst_ref=block(*blk),
            send_sem=send_sems.at[k], recv_sem=recv_sems.at[k], device_id=to, device_id_type=MESH)

    mine = pltpu.make_async_copy(x_ref, block(*me), local_sem)
    first = [copy(0, me, sibling, src=x_ref)]
    first += [copy(1 + j, me, (*chip, c), src=x_ref) for j, chip in enumerate(chips)]
    passed = [copy(4 + j, (*chip, c), sibling) for j, chip in enumerate(chips)]

    def start():
        mine.start()
        for cp in first:
            cp.start()

    def forward():
        for j, chip in enumerate(chips):
            copy(1 + j, (*chip, c), me).wait_recv()
            passed[j].start()

    def finish():
        copy(0, sibling, me).wait_recv()
        for j, chip in enumerate(chips):
            copy(4 + j, (*chip, 1 - c), me).wait_recv()
        for cp in first + passed:
            cp.wait_send()
        mine.wait()

    return start, forward, finish


_SCATTER_SEMS = [pltpu.SemaphoreType.DMA((7,)), pltpu.SemaphoreType.DMA((7,)), pltpu.SemaphoreType.DMA]


def _scatter_phases(g_ref, r_ref, send_sems, recv_sems, local_sem):
    x_, y_, c = _position()
    me = 4 * x_ + 2 * y_ + c
    local = pltpu.make_async_copy(g_ref.at[me], r_ref.at[me], local_sem)
    copies = []
    for k in range(1, N_DEV):
        to = (x_ ^ (k >> 2), y_ ^ ((k >> 1) & 1), c ^ (k & 1))
        copies.append(pltpu.make_async_remote_copy(
            src_ref=g_ref.at[me ^ k], dst_ref=r_ref.at[me], send_sem=send_sems.at[k - 1],
            recv_sem=recv_sems.at[k - 1], device_id=to, device_id_type=MESH))

    def start():
        local.start()
        for cp in copies:
            cp.start()

    def finish():
        for k in range(1, N_DEV):
            pltpu.make_async_remote_copy(
                src_ref=g_ref.at[me], dst_ref=r_ref.at[me ^ k], send_sem=send_sems.at[k - 1],
                recv_sem=recv_sems.at[k - 1], device_id=(x_, y_, c), device_id_type=MESH).wait_recv()
        for cp in copies:
            cp.wait_send()
        local.wait()

    return start, finish


def _sum_devices(r8, *, name):
    _, t, c_ = r8.shape
    tr = _pick(t, 512, 16)

    def body(r_ref, o_ref):
        acc = r_ref[0].astype(F32)
        for d in range(1, N_DEV):
            acc = acc + r_ref[d].astype(F32)
        o_ref[...] = acc

    return pl.pallas_call(
        body, name=name, grid=(t // tr,),
        in_specs=[pl.BlockSpec((N_DEV, tr, c_), lambda i: (0, i, 0))],
        out_specs=pl.BlockSpec((tr, c_), lambda i: (i, 0)),
        out_shape=jax.ShapeDtypeStruct((t, c_), F32),
        compiler_params=_cparams("parallel"),
    )(r8)


def _all_reduce_small(x, *, name):
    r, c_ = x.shape

    def body(x_ref, o_ref, buf, send_sems, recv_sems):
        x_, y_, c = _position()
        me = 4 * x_ + 2 * y_ + c
        buf[me] = x_ref[...]
        copies = []
        for k in range(1, N_DEV):
            to = (x_ ^ (k >> 2), y_ ^ ((k >> 1) & 1), c ^ (k & 1))
            copies.append(pltpu.make_async_remote_copy(
                src_ref=x_ref, dst_ref=buf.at[me], send_sem=send_sems.at[k - 1], recv_sem=recv_sems.at[k - 1],
                device_id=to, device_id_type=MESH))
        for cp in copies:
            cp.start()
        for k in range(1, N_DEV):
            src = me ^ k
            pltpu.make_async_remote_copy(
                src_ref=x_ref, dst_ref=buf.at[src], send_sem=send_sems.at[k - 1], recv_sem=recv_sems.at[k - 1],
                device_id=(x_, y_, c), device_id_type=MESH).wait_recv()
        for cp in copies:
            cp.wait_send()
        acc = buf[0]
        for d in range(1, N_DEV):
            acc = acc + buf[d]
        o_ref[...] = acc

    return pl.pallas_call(
        body, name=name,
        out_shape=jax.ShapeDtypeStruct((r, c_), F32),
        in_specs=[pl.BlockSpec(memory_space=pltpu.VMEM)],
        out_specs=pl.BlockSpec(memory_space=pltpu.VMEM),
        scratch_shapes=[pltpu.VMEM((N_DEV, r, c_), F32), pltpu.SemaphoreType.DMA((7,)), pltpu.SemaphoreType.DMA((7,))],
    )(x)


def _rs_pair_exchange(g8, *, name):
    _, t, c_ = g8.shape

    def body(g_ref, r_ref, send_sems, recv_sems):
        x_, y_, c = _position()
        copies = [pltpu.make_async_remote_copy(
            src_ref=g_ref.at[2 * ch + (1 - c)], dst_ref=r_ref.at[ch],
            send_sem=send_sems.at[ch], recv_sem=recv_sems.at[ch],
            device_id=(x_, y_, 1 - c), device_id_type=MESH) for ch in range(4)]
        for cp in copies:
            cp.start()
        for cp in copies:
            cp.wait()

    return pl.pallas_call(
        body, name=name,
        out_shape=jax.ShapeDtypeStruct((4, t, c_), g8.dtype),
        in_specs=[pl.BlockSpec(memory_space=pl.ANY)],
        out_specs=pl.BlockSpec(memory_space=pl.ANY),
        scratch_shapes=[pltpu.SemaphoreType.DMA((4,)), pltpu.SemaphoreType.DMA((4,))],
    )(g8)


def _pair_add(core, g8, recv, *, name):
    _, t, c_ = g8.shape
    tr = _pick(t, 512, 16)

    def body(core_ref, g_ref, r_ref, o_ref):
        o_ref[...] = (g_ref[...].astype(F32) + r_ref[...].astype(F32)).astype(o_ref.dtype)

    grid_spec = pltpu.PrefetchScalarGridSpec(
        num_scalar_prefetch=1, grid=(4, t // tr),
        in_specs=[pl.BlockSpec((None, tr, c_), lambda ch, i, core_ref: (2 * ch + core_ref[0], i, 0)),
                  pl.BlockSpec((None, tr, c_), lambda ch, i, core_ref: (ch, i, 0))],
        out_specs=pl.BlockSpec((None, tr, c_), lambda ch, i, core_ref: (ch, i, 0)))
    return pl.pallas_call(
        body, name=name, grid_spec=grid_spec,
        out_shape=jax.ShapeDtypeStruct((4, t, c_), g8.dtype),
        compiler_params=_cparams("parallel", "parallel"),
    )(core, g8, recv)


def _rs_chip_exchange(part, *, name):
    _, t, c_ = part.shape

    def body(p_ref, r_ref, send_sems, recv_sems, local_sem):
        x_, y_, c = _position()
        mine = 2 * x_ + y_
        local = pltpu.make_async_copy(p_ref.at[mine], r_ref.at[mine], local_sem)
        local.start()
        chips = [(1 - x_, y_), (x_, 1 - y_), (1 - x_, 1 - y_)]
        copies = [pltpu.make_async_remote_copy(
            src_ref=p_ref.at[2 * px + py], dst_ref=r_ref.at[mine],
            send_sem=send_sems.at[k], recv_sem=recv_sems.at[k],
            device_id=(px, py, c), device_id_type=MESH) for k, (px, py) in enumerate(chips)]
        for cp in copies:
            cp.start()
        for k, (px, py) in enumerate(chips):
            pltpu.make_async_remote_copy(
                src_ref=p_ref.at[mine], dst_ref=r_ref.at[2 * px + py],
                send_sem=send_sems.at[k], recv_sem=recv_sems.at[k],
                device_id=(x_, y_, c), device_id_type=MESH).wait_recv()
        for cp in copies:
            cp.wait_send()
        local.wait()

    return pl.pallas_call(
        body, name=name,
        out_shape=jax.ShapeDtypeStruct((4, t, c_), part.dtype),
        in_specs=[pl.BlockSpec(memory_space=pl.ANY)],
        out_specs=pl.BlockSpec(memory_space=pl.ANY),
        scratch_shapes=[pltpu.SemaphoreType.DMA((3,)), pltpu.SemaphoreType.DMA((3,)), pltpu.SemaphoreType.DMA],
    )(part)


def _sum_chips(r4, *, name):
    _, t, c_ = r4.shape
    tr = _pick(t, 512, 16)

    def body(r_ref, o_ref):
        acc = r_ref[0].astype(F32)
        for ch in range(1, 4):
            acc = acc + r_ref[ch].astype(F32)
        o_ref[...] = acc

    return pl.pallas_call(
        body, name=name, grid=(t // tr,),
        in_specs=[pl.BlockSpec((4, tr, c_), lambda i: (0, i, 0))],
        out_specs=pl.BlockSpec((tr, c_), lambda i: (i, 0)),
        out_shape=jax.ShapeDtypeStruct((t, c_), F32),
        compiler_params=_cparams("parallel"),
    )(r4)


BIG = (
    ("w_in", (IN_COLS // N_DEV, D_MODEL), 0),
    ("w_branch", (3, BRANCH_W, D_MODEL // N_DEV), 2),
    ("w_out", (D_MODEL // N_DEV, D_MODEL), 0),
    ("w_q_xa", (D_MODEL // N_DEV, D_MODEL), 0),
    ("w_k_xa", (D_MODEL // N_DEV, D_MODEL), 0),
    ("w_v_xa", (D_MODEL // N_DEV, D_MODEL), 0),
    ("w_o_xa", (D_MODEL // N_DEV, D_MODEL), 0),
    ("w_gate_ffn", (FFN // N_DEV, D_MODEL), 0),
    ("w_up_ffn", (FFN // N_DEV, D_MODEL), 0),
    ("w_down_ffn", (FFN // N_DEV, D_MODEL), 0),
)
TRANSPOSED = ("w_in", "w_gate_ffn", "w_up_ffn")
_BIG_LAYOUT = {n: (shp, ax) for n, shp, ax in BIG}
PACK_COLS = 1024


def _stored(name, shard):
    return shard.T if name in TRANSPOSED else shard


def _size(shape):
    n = 1
    for d in shape:
        n *= d
    return n


def _pack_shards(items, shards):
    return jnp.concatenate([shards[it].reshape(-1, PACK_COLS) for it in items], axis=0)


def _unpack_gathered(items, g):
    out = {}
    r0 = 0
    for it in items:
        shp, ax = _BIG_LAYOUT[it[0]]
        rows = _size(shp) // PACK_COLS
        blk = g[:, r0:r0 + rows].reshape((N_DEV,) + shp)
        r0 += rows
        blk = jnp.moveaxis(blk, 0, ax)
        full = list(shp)
        full[ax] = shp[ax] * N_DEV
        out[it] = blk.reshape(full)
    return out


def _pack_full(items, full):
    parts = []
    for it in items:
        shp, ax = _BIG_LAYOUT[it[0]]
        t = full[it].reshape(shp[:ax] + (N_DEV, shp[ax]) + shp[ax + 1:])
        t = jnp.moveaxis(t, ax, 0)
        parts.append(t.reshape(N_DEV, -1, PACK_COLS))
    rows = sum(part.shape[1] for part in parts)
    if rows % 128:
        parts.append(jnp.zeros((N_DEV, 128 - rows % 128, PACK_COLS), parts[0].dtype))
    return jnp.concatenate(parts, axis=1)


def _unpack_shard(items, flat):
    out = {}
    r0 = 0
    for it in items:
        shp, _ = _BIG_LAYOUT[it[0]]
        rows = _size(shp) // PACK_COLS
        out[it] = flat[r0:r0 + rows].reshape(shp)
        r0 += rows
    return out


SMALL = (
    ("norm_mix_g", (DEPTH, D_MODEL)),
    ("sgu_ln_g", (DEPTH, BRANCH_W)),
    ("sgu_ln_b", (DEPTH, BRANCH_W)),
    ("w_spatial", (DEPTH, SGU_GROUPS, SGU_LEN, SGU_LEN)),
    ("b_spatial", (DEPTH, SGU_GROUPS, SGU_LEN)),
    ("conv_w", (DEPTH, 3, BRANCH_W)),
    ("norm_xa_g", (DEPTH, D_MODEL)),
    ("mem_norm_g", (DEPTH, D_MODEL)),
    ("norm_ffn_g", (DEPTH, D_MODEL)),
    ("final_g", (D_MODEL,)),
)


def _pack_small(grads):
    flat = jnp.concatenate([grads[n].reshape(-1) for n, _ in SMALL])
    rows = -(-flat.shape[0] // PACK_COLS)
    rows = -(-rows // 8) * 8
    flat = jnp.pad(flat, (0, rows * PACK_COLS - flat.shape[0]))
    return flat.reshape(rows, PACK_COLS)


def _unpack_small(buf):
    flat = buf.reshape(-1)
    out = {}
    o = 0
    for n, shp in SMALL:
        out[n] = flat[o:o + _size(shp)].reshape(shp)
        o += _size(shp)
    return out


def _layer_fwd(l, x, mem, wt, sm, gather=None):
    t = f"l{l}_"
    sv = {"x0": x}
    h = _rms_fwd(x, sm["norm_mix_g"][l][None], name=t + "rms_mix")
    p = _mm(h, wt["w_in", l], tb=True, name=t + "in_proj", tm=2048)
    if gather is None:
        ya, *tables = _sb_fwd(p, name=t + "sb_fwd")
    else:
        ya, *tables, gathered = _sb_fwd(p, name=t + "sb_fwd", gather=gather[1])
        wt.update(_unpack_gathered(gather[0], gathered))
    w_sp = sm["w_spatial"][l]
    b_col = sm["b_spatial"][l][:, :, None]
    ln_g, ln_b = sm["sgu_ln_g"][l][None], sm["sgu_ln_b"][l][None]
    yb = _sgu_fwd(p, ln_g, ln_b, w_sp, b_col, name=t + "sgu_fwd")
    yc = _conv_fwd(p, sm["conv_w"][l], name=t + "conv_fwd")
    merged = _merge_fwd(ya, yb, yc, wt["w_branch", l], p, name=t + "merge_fwd")
    x1 = _mm(merged, wt["w_out", l], add=x, name=t + "out_proj")
    sv.update(h=h, p=p, ya=ya, yb=yb, yc=yc, merged=merged, x1=x1, tables=tables)

    h2 = _rms_fwd(x1, sm["norm_xa_g"][l][None], name=t + "rms_xa")
    mn = _rms_fwd(mem, sm["mem_norm_g"][l][None], name=t + "rms_mem")
    q = _mm(h2, wt["w_q_xa", l], out_dtype=BF16, name=t + "xa_q", tm=2048)
    k = _mm(mn, wt["w_k_xa", l], out_dtype=BF16, name=t + "xa_k")
    v = _mm(mn, wt["w_v_xa", l], out_dtype=BF16, name=t + "xa_v")
    o = _xa_fwd(q, k, v, name=t + "xa_fwd")
    x2 = _mm(o, wt["w_o_xa", l], add=x1, name=t + "xa_o")
    sv.update(h2=h2, mn=mn, q=q, k=k, v=v, o=o, x2=x2)

    h3 = _rms_fwd(x2, sm["norm_ffn_g"][l][None], name=t + "rms_ffn")
    a, b, hd = _ffn_in(h3, wt["w_gate_ffn", l], wt["w_up_ffn", l], name=t + "ffn_in")
    x3 = _mm(hd, wt["w_down_ffn", l], add=x2, name=t + "ffn_down", tk=FFN)
    sv.update(h3=h3, a=a, b=b, hd=hd)
    return x3, sv


def _layer_bwd(l, dx3, mem, wt, sm, sv, scatter=None, scatter_w_in=False):
    t = f"l{l}_b_"
    gb, gs = {}, {}
    gb["w_down_ffn"] = _mm(sv["hd"], dx3, ta=True, out_dtype=BF16, name=t + "ffn_down_dw", tm=1408, tk=1024)
    da, db = _ffn_in_bwd(dx3, wt["w_down_ffn", l], sv["a"], sv["b"], name=t + "ffn_in_bwd")
    gb["w_gate_ffn"] = _mm(da, sv["h3"], ta=True, out_dtype=BF16, name=t + "ffn_gate_dw", tm=1408, tk=1024)
    gb["w_up_ffn"] = _mm(db, sv["h3"], ta=True, out_dtype=BF16, name=t + "ffn_up_dw", tm=1408, tk=1024)
    dh3 = _mm(da, wt["w_gate_ffn", l], name=t + "ffn_gate_dx", tk=1408)
    dx2, dg = _mm(db, wt["w_up_ffn", l], add=dh3, rms=(sv["x2"], sm["norm_ffn_g"][l][None], dx3),
                  name=t + "ffn_up_dx", tm=512, tk=1408)
    gs["norm_ffn_g"] = dg[0]
    do = _mm(dx2, wt["w_o_xa", l], tb=True, out_dtype=BF16, name=t + "xa_o_dx")
    gb["w_o_xa"] = _mm(sv["o"], dx2, ta=True, out_dtype=BF16, name=t + "xa_o_dw")
    dq, dk, dv = _xa_bwd(sv["q"], sv["k"], sv["v"], do, name=t + "xa_bwd")
    dx1, dg = _mm(dq, wt["w_q_xa", l], tb=True, rms=(sv["x1"], sm["norm_xa_g"][l][None], dx2),
                  name=t + "xa_q_dx", tm=1024)
    gs["norm_xa_g"] = dg[0]
    gb["w_q_xa"] = _mm(sv["h2"], dq, ta=True, out_dtype=BF16, name=t + "xa_q_dw")
    gb["w_k_xa"] = _mm(sv["mn"], dk, ta=True, out_dtype=BF16, name=t + "xa_k_dw")
    gb["w_v_xa"] = _mm(sv["mn"], dv, ta=True, out_dtype=BF16, name=t + "xa_v_dw")
    dmn = _mm(dk, wt["w_k_xa", l], tb=True, name=t + "xa_k_dx")
    dmn = _mm(dv, wt["w_v_xa", l], tb=True, add=dmn, name=t + "xa_v_dx")
    _, dg = _rms_bwd(mem, sm["mem_norm_g"][l][None], dmn, jnp.zeros_like(mem), name=t + "rms_mem")
    gs["mem_norm_g"] = dg[0]
    dm = _mm(dx1, wt["w_out", l], tb=True, name=t + "out_proj_dx")
    gb["w_out"] = _mm(sv["merged"], dx1, ta=True, out_dtype=BF16, name=t + "out_proj_dw")
    p = sv["p"]
    dya, dyb, dyc, dgates, *dbrd = _merge_bwd(dm, sv["ya"], sv["yb"], sv["yc"], wt["w_branch", l], p,
                                              name=t + "merge_bwd")
    gb["w_branch"] = jnp.stack([
        _mm(sv[y], dbrd[n], ta=True, out_dtype=BF16, name=t + f"branch{n}_dw")
        for n, y in enumerate(("ya", "yb", "yc"))])
    dcb, dcc, dcx, dcw = _conv_bwd(p, dyc, sm["conv_w"][l], name=t + "conv_bwd")
    gs["conv_w"] = dcw
    w_sp = sm["w_spatial"][l]
    dz, dlg, dlb, dwsp, dbsp = _sgu_bwd(p, dyb, sm["sgu_ln_g"][l][None], sm["sgu_ln_b"][l][None], w_sp,
                                        jnp.swapaxes(w_sp, 1, 2), sm["b_spatial"][l][:, :, None],
                                        name=t + "sgu_bwd")
    gs.update(sgu_ln_g=dlg[0], sgu_ln_b=dlb[0], w_spatial=dwsp, b_spatial=dbsp[:, :, 0])
    received = None
    if scatter is None:
        dq_a, dk_a, dv_a = _sb_bwd(p, dya, sv["tables"], name=t + "sb_bwd")
    else:
        items, earlier = scatter
        ready = {**earlier, **{(n, l): g for n, g in gb.items()}}
        dq_a, dk_a, dv_a, received = _sb_bwd(p, dya, sv["tables"], name=t + "sb_bwd",
                                             scatter=_pack_full(items, ready))
    dp = jnp.concatenate([dq_a, dk_a, dv_a, dz, dcb, dcc, dcx, dgates], axis=1)
    gb["w_in"] = _mm(dp, sv["h"], ta=True, out_dtype=BF16, name=t + "in_proj_dw")
    tail = None if not scatter_w_in else _pack_full([("w_in", l)], {("w_in", l): gb["w_in"]})
    dx, dg, *received_w_in = _mm(dp, wt["w_in", l], rms=(sv["x0"], sm["norm_mix_g"][l][None], dx1),
                                 scatter=tail, name=t + "in_proj_dx", tm=1024, tk=1792)
    gs["norm_mix_g"] = dg[0]
    return dx, gb, gs, received, (received_w_in[0] if scatter_w_in else None)


_WEIGHTS = ("norm_mix_g", "w_in", "sgu_ln_g", "sgu_ln_b", "w_spatial", "b_spatial", "conv_w", "w_branch", "w_out",
            "norm_xa_g", "mem_norm_g", "w_q_xa", "w_k_xa", "w_v_xa", "w_o_xa", "norm_ffn_g", "w_gate_ffn",
            "w_up_ffn", "w_down_ffn", "final_g")


def kernel(x, mem, norm_mix_g, w_in, sgu_ln_g, sgu_ln_b, w_spatial, b_spatial, conv_w, w_branch, w_out, norm_xa_g, mem_norm_g, w_q_xa, w_k_xa, w_v_xa, w_o_xa, norm_ffn_g, w_gate_ffn, w_up_ffn, w_down_ffn, final_g, loss_target, m_norm_mix_g, m_w_in, m_sgu_ln_g, m_sgu_ln_b, m_w_spatial, m_b_spatial, m_conv_w, m_w_branch, m_w_out, m_norm_xa_g, m_mem_norm_g, m_w_q_xa, m_w_k_xa, m_w_v_xa, m_w_o_xa, m_norm_ffn_g, m_w_gate_ffn, m_w_up_ffn, m_w_down_ffn, m_final_g, v_norm_mix_g, v_w_in, v_sgu_ln_g, v_sgu_ln_b, v_w_spatial, v_b_spatial, v_conv_w, v_w_branch, v_w_out, v_norm_xa_g, v_mem_norm_g, v_w_q_xa, v_w_k_xa, v_w_v_xa, v_w_o_xa, v_norm_ffn_g, v_w_gate_ffn, v_w_up_ffn, v_w_down_ffn, v_final_g):
    w = dict(norm_mix_g=norm_mix_g, w_in=w_in, sgu_ln_g=sgu_ln_g, sgu_ln_b=sgu_ln_b, w_spatial=w_spatial,
             b_spatial=b_spatial, conv_w=conv_w, w_branch=w_branch, w_out=w_out, norm_xa_g=norm_xa_g,
             mem_norm_g=mem_norm_g, w_q_xa=w_q_xa, w_k_xa=w_k_xa, w_v_xa=w_v_xa, w_o_xa=w_o_xa,
             norm_ffn_g=norm_ffn_g, w_gate_ffn=w_gate_ffn, w_up_ffn=w_up_ffn, w_down_ffn=w_down_ffn, final_g=final_g)
    m = dict(norm_mix_g=m_norm_mix_g, w_in=m_w_in, sgu_ln_g=m_sgu_ln_g, sgu_ln_b=m_sgu_ln_b, w_spatial=m_w_spatial,
             b_spatial=m_b_spatial, conv_w=m_conv_w, w_branch=m_w_branch, w_out=m_w_out, norm_xa_g=m_norm_xa_g,
             mem_norm_g=m_mem_norm_g, w_q_xa=m_w_q_xa, w_k_xa=m_w_k_xa, w_v_xa=m_w_v_xa, w_o_xa=m_w_o_xa,
             norm_ffn_g=m_norm_ffn_g, w_gate_ffn=m_w_gate_ffn, w_up_ffn=m_w_up_ffn, w_down_ffn=m_w_down_ffn,
             final_g=m_final_g)
    v = dict(norm_mix_g=v_norm_mix_g, w_in=v_w_in, sgu_ln_g=v_sgu_ln_g, sgu_ln_b=v_sgu_ln_b, w_spatial=v_w_spatial,
             b_spatial=v_b_spatial, conv_w=v_conv_w, w_branch=v_w_branch, w_out=v_w_out, norm_xa_g=v_norm_xa_g,
             mem_norm_g=v_mem_norm_g, w_q_xa=v_w_q_xa, w_k_xa=v_w_k_xa, w_v_xa=v_w_v_xa, w_o_xa=v_w_o_xa,
             norm_ffn_g=v_norm_ffn_g, w_gate_ffn=v_w_gate_ffn, w_up_ffn=v_w_up_ffn, w_down_ffn=v_w_down_ffn,
             final_g=v_final_g)

    names = [n for n, _, _ in BIG]
    shards = {(n, l): _stored(n, w[n][l].astype(BF16)) for n in names for l in range(DEPTH)}
    first_items = [("w_in", 0)]
    mid_items = [(n, 0) for n in names if n != "w_in"] + [("w_in", 1)]
    last_items = [(n, 1) for n in names if n != "w_in"]
    wt = _unpack_gathered(first_items, _all_gather(_pack_shards(first_items, shards), name="gather_w_in0"))
    cw_pad = jnp.zeros((8, 128), F32).at[:DEPTH * 3, :BRANCH_W // N_DEV].set(conv_w.reshape(DEPTH * 3, -1))
    cw_all = _all_gather(cw_pad, name="gather_conv_w")[:, :DEPTH * 3, :BRANCH_W // N_DEV]
    conv_full = jnp.moveaxis(cw_all.reshape(N_DEV, DEPTH, 3, BRANCH_W // N_DEV), 0, 2).reshape(DEPTH, 3, BRANCH_W)
    sm = {n: w[n] for n, _ in SMALL}
    sm["conv_w"] = conv_full

    xs, ms = x[0], mem[0]
    x1, saved0 = _layer_fwd(0, xs, ms, wt, sm, gather=(mid_items, _pack_shards(mid_items, shards)))
    x2, saved1 = _layer_fwd(1, x1, ms, wt, sm, gather=(last_items, _pack_shards(last_items, shards)))
    dcur, loss, dfinal = _final_loss(x2, sm["final_g"][None], loss_target[0], name="final_loss")
    loss = lax.psum(loss[0, 0], AXES)
    items_a = [(n, 1) for n in names if n != "w_in"]
    items_b = [("w_in", 1)] + [(n, 0) for n in names if n != "w_in"]
    items_c = [("w_in", 0)]
    dcur, gb1, gs1, recv_a, _ = _layer_bwd(1, dcur, ms, wt, sm, saved1, scatter=(items_a, {}))
    dx, gb0, gs0, recv_b, recv_c = _layer_bwd(0, dcur, ms, wt, sm, saved0,
                                              scatter=(items_b, {("w_in", 1): gb1["w_in"]}), scatter_w_in=True)

    shard_grads = _unpack_shard(items_a, _sum_devices(recv_a, name="rs_sum_a"))
    shard_grads.update(_unpack_shard(items_b, _sum_devices(recv_b, name="rs_sum_b")))
    shard_grads.update(_unpack_shard(items_c, _sum_devices(recv_c, name="rs_sum_c")))
    grads = {n: jnp.stack([_stored(n, shard_grads[n, l]) for l in range(DEPTH)]) for n in names}
    small = {n: jnp.stack([gs0[n], gs1[n]]) for n, _ in SMALL if n != "final_g"}
    small["final_g"] = dfinal[0]
    small_sum = _unpack_small(_all_reduce_small(_pack_small(small), name="all_reduce_small"))
    width = BRANCH_W // N_DEV
    dev = 4 * lax.axis_index("x") + 2 * lax.axis_index("y") + lax.axis_index("c")
    for n, _ in SMALL:
        grads[n] = small_sum[n]
    grads["conv_w"] = lax.dynamic_slice_in_dim(small_sum["conv_w"], dev * width, width, axis=2)

    delta, new_m, new_v = {}, {}, {}
    for n in _WEIGHTS:
        shp = w[n].shape
        two_d = (-1, shp[-1])
        d_, m_, v_ = _adamw(w[n].reshape(two_d), grads[n].reshape(two_d), m[n].reshape(two_d), v[n].reshape(two_d),
                            name="adamw_" + n)
        delta[n], new_m[n], new_v[n] = d_.reshape(shp), m_.reshape(shp), v_.reshape(shp)

    return (loss, dx[None], *[grads[n] for n in _WEIGHTS], *[delta[n] for n in _WEIGHTS],
            *[new_m[n] for n in _WEIGHTS], *[new_v[n] for n in _WEIGHTS])
```

```python
import jax
import jax.numpy as jnp
from jax import lax
from jax.experimental import pallas as pl
from jax.experimental.pallas import tpu as pltpu

F32 = jnp.float32
BF16 = jnp.bfloat16
MESH = pl.DeviceIdType.MESH

D_MODEL = 1024
BRANCH_W = 512
IN_COLS = 7168
FFN = 2816
N_DEV = 8
DEPTH = 2
SB_BLOCK = 128
SB_SPAN = 1024
SB_Q_FWD = 512
SB_Q_BWD = 512
SB_SCALE = 0.125
XA_HEAD = 256
XA_SCALE = 0.0625
SGU_LEN = 128
SGU_GROUPS = 4
RMS_EPS = 1e-6
LN_EPS = 1e-5
HALO = 8

ADAM_LR = 0.001
ADAM_B1 = 0.9
ADAM_B2 = 0.999
ADAM_EPS = 1e-08
ADAM_WD = 0.01
ADAM_STEP = 10

VMEM_LIMIT_BYTES = 52 * 1024 * 1024

AXES = ("x", "y", "c")


def _cparams(*sem):
    return pltpu.CompilerParams(dimension_semantics=sem, vmem_limit_bytes=VMEM_LIMIT_BYTES)


def _pick(n, target, align):
    t = (min(target, n) // align) * align
    while t >= align:
        if n % t == 0:
            return t
        t -= align
    return n


def _dot(a, b):
    return jnp.dot(a, b, preferred_element_type=F32)


def _dot_nt(a, b):
    return lax.dot_general(a, b, (((1,), (1,)), ((), ())), preferred_element_type=F32)


def _dot_tn(a, b):
    return lax.dot_general(a, b, (((0,), (0,)), ((), ())), preferred_element_type=F32)


def _sigmoid(x):
    return 1.0 / (1.0 + jnp.exp(-x))


def _mm(a, b, *, name, ta=False, tb=False, out_dtype=F32, add=None, rms=None, scatter=None,
        tm=1024, tn=1024, tk=2048):
    m, k = (a.shape[1], a.shape[0]) if ta else a.shape
    n = b.shape[0] if tb else b.shape[1]
    assert k == (b.shape[1] if tb else b.shape[0])
    tm = _pick(m, tm, 128)
    tn = n if rms is not None else _pick(n, tn, 128)
    tk = _pick(k, tk, 128)
    nk = k // tk
    grid = (m // tm, n // tn, nk)
    ca = 0 if ta else 1
    cb = 1 if tb else 0
    n_add = 0 if add is None else 1
    n_rms = 0 if rms is None else 3
    n_sc = 0 if scatter is None else 1
    n_in = 2 + n_add + n_rms + n_sc
    n_out = 1 + (1 if rms is not None else 0) + n_sc

    def body(*refs):
        refs = list(refs)
        a_ref, b_ref = refs[:2]
        extra = refs[2:2 + n_add + n_rms]
        outs = refs[n_in:n_in + n_out]
        scratch = refs[n_in + n_out:]
        o_ref = outs[0]
        kk = pl.program_id(2)
        first_row_tile = pl.program_id(0) == 0
        if scatter is not None:
            start, finish_scatter = _scatter_phases(refs[n_in - 1], outs[-1], *scratch[-3:])
            step = (pl.program_id(0) * grid[1] + pl.program_id(1)) * grid[2] + kk
            pl.when(step == 0)(start)

        def product():
            return lax.dot_general(a_ref[...].astype(BF16), b_ref[...].astype(BF16),
                                   (((ca,), (cb,)), ((), ())), preferred_element_type=F32)

        def finish(r):
            if add is not None:
                r = r + extra[0][...]
            if rms is None:
                o_ref[...] = r.astype(out_dtype)
                return
            x_ref, g_ref, dres_ref = extra[n_add:]
            dg_ref = outs[1]

            @pl.when(first_row_tile)
            def _():
                dg_ref[...] = jnp.zeros_like(dg_ref)

            xv = x_ref[...]
            rs = lax.rsqrt(jnp.mean(xv * xv, axis=-1, keepdims=True) + RMS_EPS)
            xh = xv * rs
            dg_ref[...] += jnp.sum(r * xh, axis=0, keepdims=True)
            dxh = r * g_ref[...]
            o_ref[...] = dres_ref[...] + rs * (dxh - xh * jnp.mean(dxh * xh, axis=-1, keepdims=True))

        if nk == 1:
            finish(product())
        else:
            acc_ref = scratch[0]

            @pl.when(kk == 0)
            def _():
                acc_ref[...] = jnp.zeros_like(acc_ref)

            acc_ref[...] += product()

            @pl.when(kk == nk - 1)
            def _():
                finish(acc_ref[...])

        if scatter is not None:
            pl.when(step == grid[0] * grid[1] * grid[2] - 1)(finish_scatter)

    a_spec = pl.BlockSpec((tk, tm), lambda i, j, kk: (kk, i)) if ta else pl.BlockSpec((tm, tk), lambda i, j, kk: (i, kk))
    b_spec = pl.BlockSpec((tn, tk), lambda i, j, kk: (j, kk)) if tb else pl.BlockSpec((tk, tn), lambda i, j, kk: (kk, j))
    tile = pl.BlockSpec((tm, tn), lambda i, j, kk: (i, j))
    in_specs = [a_spec, b_spec]
    operands = [a, b]
    out_specs = [tile]
    out_shape = [jax.ShapeDtypeStruct((m, n), out_dtype)]
    if add is not None:
        in_specs.append(tile)
        operands.append(add)
    if rms is not None:
        vec = pl.BlockSpec((1, n), lambda i, j, kk: (0, 0))
        in_specs += [tile, vec, tile]
        operands += list(rms)
        out_specs.append(vec)
        out_shape = [jax.ShapeDtypeStruct((m, n), F32), jax.ShapeDtypeStruct((1, n), F32)]
    scratch_shapes = [pltpu.VMEM((tm, tn), F32)] if nk > 1 else []
    semantics = ("arbitrary" if rms is not None else "parallel", "parallel", "arbitrary")
    if scatter is not None:
        in_specs.append(pl.BlockSpec(memory_space=pl.ANY))
        operands.append(scatter)
        out_specs.append(pl.BlockSpec(memory_space=pl.ANY))
        out_shape.append(jax.ShapeDtypeStruct(scatter.shape, scatter.dtype))
        scratch_shapes = scratch_shapes + _SCATTER_SEMS
        semantics = ("arbitrary", "arbitrary", "arbitrary")
    out = pl.pallas_call(
        body, name=name, grid=grid,
        in_specs=in_specs, out_specs=out_specs, out_shape=out_shape,
        scratch_shapes=scratch_shapes, compiler_params=_cparams(*semantics),
    )(*operands)
    return out[0] if len(out) == 1 else out


def _rms_fwd(x, g, *, name):
    r, d = x.shape
    tr = _pick(r, 512, 16)

    def body(x_ref, g_ref, o_ref):
        xv = x_ref[...]
        rs = lax.rsqrt(jnp.mean(xv * xv, axis=-1, keepdims=True) + RMS_EPS)
        o_ref[...] = (xv * rs * g_ref[...]).astype(BF16)

    return pl.pallas_call(
        body, name=name, grid=(r // tr,),
        in_specs=[pl.BlockSpec((tr, d), lambda i: (i, 0)), pl.BlockSpec((1, d), lambda i: (0, 0))],
        out_specs=pl.BlockSpec((tr, d), lambda i: (i, 0)),
        out_shape=jax.ShapeDtypeStruct((r, d), BF16),
        compiler_params=_cparams("parallel"),
    )(x, g)


def _rms_bwd(x, g, dh, dres, *, name):
    r, d = x.shape
    tr = _pick(r, 256, 8)

    def body(x_ref, g_ref, dh_ref, dres_ref, dx_ref, dg_ref):
        @pl.when(pl.program_id(0) == 0)
        def _():
            dg_ref[...] = jnp.zeros_like(dg_ref)

        xv = x_ref[...]
        dhv = dh_ref[...].astype(F32)
        rs = lax.rsqrt(jnp.mean(xv * xv, axis=-1, keepdims=True) + RMS_EPS)
        xh = xv * rs
        dg_ref[...] += jnp.sum(dhv * xh, axis=0, keepdims=True)
        dxh = dhv * g_ref[...]
        dx_ref[...] = dres_ref[...] + rs * (dxh - xh * jnp.mean(dxh * xh, axis=-1, keepdims=True))

    return pl.pallas_call(
        body, name=name, grid=(r // tr,),
        in_specs=[pl.BlockSpec((tr, d), lambda i: (i, 0)), pl.BlockSpec((1, d), lambda i: (0, 0)),
                  pl.BlockSpec((tr, d), lambda i: (i, 0)), pl.BlockSpec((tr, d), lambda i: (i, 0))],
        out_specs=[pl.BlockSpec((tr, d), lambda i: (i, 0)), pl.BlockSpec((1, d), lambda i: (0, 0))],
        out_shape=[jax.ShapeDtypeStruct((r, d), F32), jax.ShapeDtypeStruct((1, d), F32)],
        compiler_params=_cparams("arbitrary"),
    )(x, g, dh, dres)


def _final_loss(x, g, target, *, name):
    r, d = x.shape
    tr = _pick(r, 512, 8)

    def body(x_ref, g_ref, t_ref, dx_ref, loss_ref, dg_ref):
        @pl.when(pl.program_id(0) == 0)
        def _():
            dg_ref[...] = jnp.zeros_like(dg_ref)
            loss_ref[...] = jnp.zeros_like(loss_ref)

        xv = x_ref[...]
        gv = g_ref[...]
        rs = lax.rsqrt(jnp.mean(xv * xv, axis=-1, keepdims=True) + RMS_EPS)
        xh = xv * rs
        err = xh * gv - t_ref[...]
        row_loss = jnp.mean(err * err, axis=-1, keepdims=True)
        loss_ref[...] += 0.5 * jnp.sum(row_loss, axis=0, keepdims=True)
        dy = err * (1.0 / d)
        dg_ref[...] += jnp.sum(dy * xh, axis=0, keepdims=True)
        dxh = dy * gv
        dx_ref[...] = rs * (dxh - xh * jnp.mean(dxh * xh, axis=-1, keepdims=True))

    return pl.pallas_call(
        body, name=name, grid=(r // tr,),
        in_specs=[pl.BlockSpec((tr, d), lambda i: (i, 0)), pl.BlockSpec((1, d), lambda i: (0, 0)),
                  pl.BlockSpec((tr, d), lambda i: (i, 0))],
        out_specs=[pl.BlockSpec((tr, d), lambda i: (i, 0)), pl.BlockSpec((1, 128), lambda i: (0, 0)),
                   pl.BlockSpec((1, d), lambda i: (0, 0))],
        out_shape=[jax.ShapeDtypeStruct((r, d), F32), jax.ShapeDtypeStruct((1, 128), F32),
                   jax.ShapeDtypeStruct((1, d), F32)],
        compiler_params=_cparams("arbitrary"),
    )(x, g, target)


def _cumsum_operand(strict_after, totals=True):
    width = (2 if totals else 1) * SB_BLOCK
    r = lax.broadcasted_iota(jnp.int32, (SB_BLOCK, width), 0)
    c = lax.broadcasted_iota(jnp.int32, (SB_BLOCK, width), 1)
    tri = (r > c) if strict_after else (r < c)
    return jnp.where((c >= SB_BLOCK) | tri, 1.0, 0.0).astype(BF16)


def _sb_scores(qh, kw, run, valid, after_ones):
    nb = kw.shape[0] // SB_BLOCK
    z = _dot_nt(qh, kw)
    lsp = jnp.minimum(z, 0.0) - jnp.log(1.0 + jnp.exp(-jnp.abs(z)))
    l1m = lsp - z
    if valid is not None:
        l1m = jnp.where(valid, l1m, 0.0)
    l1b = l1m.astype(BF16)
    later = [None] * nb
    seen = [None] * nb
    for b in reversed(range(nb)):
        cols = slice(b * SB_BLOCK, (b + 1) * SB_BLOCK)
        ct = _dot(l1b[:, cols], after_ones)
        seen[b] = run
        later[b] = run + ct[:, :SB_BLOCK]
        run = run + ct[:, SB_BLOCK:]
    a = jnp.exp(lsp + jnp.concatenate(later, axis=1))
    if valid is not None:
        a = jnp.where(valid, a, 0.0)
    return a, run, seen


def _sb_setup(q_ref, span):
    qi = pl.program_id(1)
    rows = q_ref.shape[0]
    sd = (qi * rows + rows - 1) // span
    lane = lax.broadcasted_iota(jnp.int32, (rows, SB_BLOCK), 1)
    col = lax.broadcasted_iota(jnp.int32, (rows, span), 1)
    row = lax.broadcasted_iota(jnp.int32, (rows, span), 0)
    valid = col < (qi * rows - sd * span) + row
    q = q_ref[...] * SB_SCALE
    qhs = (jnp.where(lane < 64, q, 0.0).astype(BF16), jnp.where(lane >= 64, q, 0.0).astype(BF16))
    return lane, sd, valid, qhs


def _sb_fwd(p, *, name, gather=None):
    s = p.shape[0]
    qrows = min(SB_Q_FWD, s)
    nq = s // qrows
    kcol = BRANCH_W // SB_BLOCK
    span = min(SB_SPAN, s)
    per = span // SB_BLOCK
    assert s // SB_BLOCK <= SB_BLOCK

    def body(*refs):
        if gather is None:
            q_ref, k_ref, v_ref, o_ref, r0_ref, r1_ref = refs
        else:
            q_ref, k_ref, v_ref, x_ref, o_ref, r0_ref, r1_ref, g_ref, send_sems, recv_sems, local_sem = refs
            start, forward, finish = _gather_phases(x_ref, g_ref, send_sems, recv_sems, local_sem)
            step = pl.program_id(0) * nq + pl.program_id(1)
            pl.when(step == 0)(start)
        lane, sd, valid, qhs = _sb_setup(q_ref, span)
        after_ones = _cumsum_operand(True)
        zero = jnp.zeros((qrows, SB_BLOCK), F32)
        lane_row = lax.broadcasted_iota(jnp.int32, (1, SB_BLOCK), 1)

        def span_step(sb, carry, mask):
            rows = pl.ds(pl.multiple_of(sb * span, span), span)
            kw = k_ref[rows, :].astype(BF16)
            vw = v_ref[rows, :].astype(BF16)
            out = []
            for h in range(2):
                run, acc, table = carry[h]
                a, run, seen = _sb_scores(qhs[h], kw, run, mask, after_ones)
                for b in range(per):
                    table = jnp.where(lane_row == sb * per + b, seen[b], table)
                out.append((run, acc + _dot(a.astype(BF16), vw), table))
            return tuple(out)

        carry = span_step(sd, ((zero, zero, zero), (zero, zero, zero)), valid)
        carry = lax.fori_loop(0, sd, lambda t, c: span_step(sd - 1 - t, c, None), carry)
        o_ref[...] = jnp.where(lane < 64, carry[0][1], carry[1][1]).astype(BF16)
        r0_ref[...] = carry[0][2]
        r1_ref[...] = carry[1][2]
        if gather is not None:
            pl.when(step == (kcol - 1) * nq + (3 * nq) // 4)(forward)
            pl.when(step == kcol * nq - 1)(finish)

    in_specs = [pl.BlockSpec((qrows, SB_BLOCK), lambda hp, qi: (qi, hp)),
                pl.BlockSpec((s, SB_BLOCK), lambda hp, qi: (0, kcol + hp)),
                pl.BlockSpec((s, SB_BLOCK), lambda hp, qi: (0, 2 * kcol + hp))]
    table = pl.BlockSpec((None, qrows, SB_BLOCK), lambda hp, qi: (hp, qi, 0))
    out_specs = [pl.BlockSpec((qrows, SB_BLOCK), lambda hp, qi: (qi, hp)), table, table]
    out_shape = [jax.ShapeDtypeStruct((s, BRANCH_W), BF16)] + [jax.ShapeDtypeStruct((kcol, s, SB_BLOCK), F32)] * 2
    operands = [p, p, p]
    scratch = []
    if gather is not None:
        in_specs.append(pl.BlockSpec(memory_space=pl.ANY))
        out_specs.append(pl.BlockSpec(memory_space=pl.ANY))
        out_shape.append(jax.ShapeDtypeStruct((N_DEV,) + gather.shape, gather.dtype))
        operands.append(gather)
        scratch = _GATHER_SEMS
    out = pl.pallas_call(
        body, name=name, grid=(kcol, nq), in_specs=in_specs, out_specs=out_specs, out_shape=out_shape,
        scratch_shapes=scratch, compiler_params=_cparams("arbitrary", "arbitrary"),
    )(*operands)
    return out


def _sb_bwd(p, dya, tables, *, name, scatter=None):
    s = p.shape[0]
    qrows = min(SB_Q_BWD, s)
    nq = s // qrows
    kcol = BRANCH_W // SB_BLOCK
    span = min(SB_SPAN, s)
    per = span // SB_BLOCK

    def body(*refs):
        if scatter is None:
            q_ref, k_ref, v_ref, do_ref, t0_ref, t1_ref, dq_ref, dk_ref, dv_ref, dk_acc, dv_acc = refs
        else:
            (q_ref, k_ref, v_ref, do_ref, t0_ref, t1_ref, g_ref, dq_ref, dk_ref, dv_ref, r_ref,
             dk_acc, dv_acc, send_sems, recv_sems, local_sem) = refs
            start, finish = _scatter_phases(g_ref, r_ref, send_sems, recv_sems, local_sem)
            step = pl.program_id(0) * nq + pl.program_id(1)
            pl.when(step == 0)(start)
        qi = pl.program_id(1)

        @pl.when(qi == 0)
        def _():
            dk_acc[...] = jnp.zeros_like(dk_acc)
            dv_acc[...] = jnp.zeros_like(dv_acc)

        lane, sd, valid, qhs = _sb_setup(q_ref, span)
        after = _cumsum_operand(True, totals=False)
        before_ones = _cumsum_operand(False)
        do = do_ref[...]
        dohs = (jnp.where(lane < 64, do, 0.0).astype(BF16), jnp.where(lane >= 64, do, 0.0).astype(BF16))
        tabs = (t0_ref[...], t1_ref[...])
        lane_row = lax.broadcasted_iota(jnp.int32, (1, SB_BLOCK), 1)
        zero = jnp.zeros((qrows, SB_BLOCK), F32)

        def span_step(sb, carry, mask):
            rows = pl.ds(pl.multiple_of(sb * span, span), span)
            kw = k_ref[rows, :].astype(BF16)
            vw = v_ref[rows, :].astype(BF16)
            out = []
            dk_span = jnp.zeros((span, SB_BLOCK), F32)
            dv_span = jnp.zeros((span, SB_BLOCK), F32)
            for h in range(2):
                pg, dq = carry[h]
                z = _dot_nt(qhs[h], kw)
                lsp = jnp.minimum(z, 0.0) - jnp.log(1.0 + jnp.exp(-jnp.abs(z)))
                l1m = lsp - z
                if mask is not None:
                    l1m = jnp.where(mask, l1m, 0.0)
                l1b = l1m.astype(BF16)
                later = [None] * per
                for b in range(per):
                    cols = slice(b * SB_BLOCK, (b + 1) * SB_BLOCK)
                    seen = jnp.sum(jnp.where(lane_row == sb * per + b, tabs[h], 0.0), axis=-1, keepdims=True)
                    later[b] = seen + _dot(l1b[:, cols], after)
                a = jnp.exp(lsp + jnp.concatenate(later, axis=1))
                beta = jnp.exp(lsp)
                if mask is not None:
                    a = jnp.where(mask, a, 0.0)
                    beta = jnp.where(mask, beta, 0.0)
                g = a * _dot_nt(dohs[h], vw)
                gb = g.astype(BF16)
                before = [None] * per
                for b in range(per):
                    cols = slice(b * SB_BLOCK, (b + 1) * SB_BLOCK)
                    gt = _dot(gb[:, cols], before_ones)
                    before[b] = pg + gt[:, :SB_BLOCK]
                    pg = pg + gt[:, SB_BLOCK:]
                dz = (g - beta * (g + jnp.concatenate(before, axis=1))).astype(BF16)
                dk_span = dk_span + _dot_tn(dz, qhs[h])
                dv_span = dv_span + _dot_tn(a.astype(BF16), dohs[h])
                out.append((pg, dq + _dot(dz, kw)))
            dk_acc[rows, :] += dk_span
            dv_acc[rows, :] += dv_span
            return tuple(out)

        carry = lax.fori_loop(0, sd, lambda sb, c: span_step(sb, c, None), ((zero, zero), (zero, zero)))
        carry = span_step(sd, carry, valid)
        dq_ref[...] = (jnp.where(lane < 64, carry[0][1], carry[1][1]) * SB_SCALE).astype(BF16)

        @pl.when(qi == nq - 1)
        def _():
            dk_ref[...] = dk_acc[...].astype(BF16)
            dv_ref[...] = dv_acc[...].astype(BF16)

        if scatter is not None:
            pl.when(step == kcol * nq - 1)(finish)

    blk = pl.BlockSpec((qrows, SB_BLOCK), lambda hp, qi: (qi, hp))
    col = pl.BlockSpec((s, SB_BLOCK), lambda hp, qi: (0, hp))
    table = pl.BlockSpec((None, qrows, SB_BLOCK), lambda hp, qi: (hp, qi, 0))
    out = jax.ShapeDtypeStruct((s, BRANCH_W), BF16)
    in_specs = [blk,
                pl.BlockSpec((s, SB_BLOCK), lambda hp, qi: (0, kcol + hp)),
                pl.BlockSpec((s, SB_BLOCK), lambda hp, qi: (0, 2 * kcol + hp)),
                blk, table, table]
    out_specs = [blk, col, col]
    out_shape = [out, out, out]
    operands = [p, p, p, dya, tables[0], tables[1]]
    scratch = [pltpu.VMEM((s, SB_BLOCK), F32), pltpu.VMEM((s, SB_BLOCK), F32)]
    if scatter is not None:
        in_specs.append(pl.BlockSpec(memory_space=pl.ANY))
        out_specs.append(pl.BlockSpec(memory_space=pl.ANY))
        out_shape.append(jax.ShapeDtypeStruct(scatter.shape, scatter.dtype))
        operands.append(scatter)
        scratch = scratch + _SCATTER_SEMS
    return pl.pallas_call(
        body, name=name, grid=(kcol, nq), in_specs=in_specs, out_specs=out_specs, out_shape=out_shape,
        scratch_shapes=scratch, compiler_params=_cparams("arbitrary", "arbitrary"),
    )(*operands)


_INV_SQRT2 = 0.7071067811865476
_INV_SQRT2PI = 0.3989422804014327


def _gelu(x):
    return 0.5 * x * (1.0 + lax.erf(x * _INV_SQRT2))


def _gelu_grad(x):
    return 0.5 * (1.0 + lax.erf(x * _INV_SQRT2)) + x * _INV_SQRT2PI * jnp.exp(-0.5 * x * x)


def _chunk_mask(transposed=False):
    r = lax.broadcasted_iota(jnp.int32, (SGU_LEN, SGU_LEN), 0)
    c = lax.broadcasted_iota(jnp.int32, (SGU_LEN, SGU_LEN), 1)
    return (c // 64) >= (r // 64) if transposed else (r // 64) >= (c // 64)


def _sgu_norm(v_raw, g, b):
    zv = _gelu(v_raw)
    xc = zv - jnp.mean(zv, axis=-1, keepdims=True)
    rs = lax.rsqrt(jnp.mean(xc * xc, axis=-1, keepdims=True) + LN_EPS)
    xh = xc * rs
    return xh, rs, xh * g + b


def _sgu_fwd(p, ln_g, ln_b, w, b_col, *, name):
    s = p.shape[0]
    tr = _pick(s, 512, SGU_LEN)

    def body(u_ref, v_ref, g_ref, b_ref, w_ref, bc_ref, o_ref):
        mask = _chunk_mask()
        zu = _gelu(u_ref[...])
        _, _, vn = _sgu_norm(v_ref[...], g_ref[...], b_ref[...])
        vnb = vn.astype(BF16)
        for gi in range(SGU_GROUPS):
            wg = jnp.where(mask, w_ref[gi], 0.0).astype(BF16)
            cs = slice(gi * SGU_LEN, (gi + 1) * SGU_LEN)
            for c in range(tr // SGU_LEN):
                rs_ = slice(c * SGU_LEN, (c + 1) * SGU_LEN)
                vm = _dot(wg, vnb[rs_, cs]) + bc_ref[gi]
                o_ref[rs_, cs] = (zu[rs_, cs] * vm).astype(BF16)

    vec = pl.BlockSpec((1, BRANCH_W), lambda i: (0, 0))
    return pl.pallas_call(
        body, name=name, grid=(s // tr,),
        in_specs=[pl.BlockSpec((tr, BRANCH_W), lambda i: (i, 3)), pl.BlockSpec((tr, BRANCH_W), lambda i: (i, 4)),
                  vec, vec,
                  pl.BlockSpec((SGU_GROUPS, SGU_LEN, SGU_LEN), lambda i: (0, 0, 0)),
                  pl.BlockSpec((SGU_GROUPS, SGU_LEN, 1), lambda i: (0, 0, 0))],
        out_specs=pl.BlockSpec((tr, BRANCH_W), lambda i: (i, 0)),
        out_shape=jax.ShapeDtypeStruct((s, BRANCH_W), BF16),
        compiler_params=_cparams("parallel"),
    )(p, p, ln_g, ln_b, w, b_col)


def _sgu_bwd(p, dyb, ln_g, ln_b, w, w_t, b_col, *, name):
    s = p.shape[0]
    tr = _pick(s, 512, SGU_LEN)

    def body(u_ref, v_ref, dy_ref, g_ref, b_ref, w_ref, wt_ref, bc_ref,
             dz_ref, dg_ref, db_ref, dw_ref, dbc_ref, dvn_s):
        @pl.when(pl.program_id(0) == 0)
        def _():
            dg_ref[...] = jnp.zeros_like(dg_ref)
            db_ref[...] = jnp.zeros_like(db_ref)
            dw_ref[...] = jnp.zeros_like(dw_ref)
            dbc_ref[...] = jnp.zeros_like(dbc_ref)

        mask = _chunk_mask()
        mask_t = _chunk_mask(transposed=True)
        u_raw = u_ref[...]
        v_raw = v_ref[...]
        dy = dy_ref[...]
        zu = _gelu(u_raw)
        xh, rs, vn = _sgu_norm(v_raw, g_ref[...], b_ref[...])
        vnb = vn.astype(BF16)
        dvm_all = dy * zu
        for gi in range(SGU_GROUPS):
            wg = jnp.where(mask, w_ref[gi], 0.0).astype(BF16)
            wgt = jnp.where(mask_t, wt_ref[gi], 0.0).astype(BF16)
            cs = slice(gi * SGU_LEN, (gi + 1) * SGU_LEN)
            dw_g = jnp.zeros((SGU_LEN, SGU_LEN), F32)
            db_g = jnp.zeros((SGU_LEN, 1), F32)
            for c in range(tr // SGU_LEN):
                rs_ = slice(c * SGU_LEN, (c + 1) * SGU_LEN)
                vm = _dot(wg, vnb[rs_, cs]) + bc_ref[gi]
                dz_ref[rs_, cs] = (dy[rs_, cs] * vm * _gelu_grad(u_raw[rs_, cs])).astype(BF16)
                dvm = dvm_all[rs_, cs]
                dvmb = dvm.astype(BF16)
                dw_g = dw_g + _dot_nt(dvmb, vnb[rs_, cs])
                db_g = db_g + jnp.sum(dvm, axis=1, keepdims=True)
                dvn_s[rs_, cs] = _dot(wgt, dvmb)
            dw_ref[gi] += jnp.where(mask, dw_g, 0.0)
            dbc_ref[gi] += db_g
        dvn = dvn_s[...]
        dg_ref[...] += jnp.sum(dvn * xh, axis=0, keepdims=True)
        db_ref[...] += jnp.sum(dvn, axis=0, keepdims=True)
        dxh = dvn * g_ref[...]
        dzv = rs * (dxh - jnp.mean(dxh, axis=-1, keepdims=True) - xh * jnp.mean(dxh * xh, axis=-1, keepdims=True))
        dz_ref[:, BRANCH_W:] = (dzv * _gelu_grad(v_raw)).astype(BF16)

    vec = pl.BlockSpec((1, BRANCH_W), lambda i: (0, 0))
    wspec = pl.BlockSpec((SGU_GROUPS, SGU_LEN, SGU_LEN), lambda i: (0, 0, 0))
    bspec = pl.BlockSpec((SGU_GROUPS, SGU_LEN, 1), lambda i: (0, 0, 0))
    return pl.pallas_call(
        body, name=name, grid=(s // tr,),
        in_specs=[pl.BlockSpec((tr, BRANCH_W), lambda i: (i, 3)), pl.BlockSpec((tr, BRANCH_W), lambda i: (i, 4)),
                  pl.BlockSpec((tr, BRANCH_W), lambda i: (i, 0)), vec, vec, wspec, wspec, bspec],
        out_specs=[pl.BlockSpec((tr, 2 * BRANCH_W), lambda i: (i, 0)), vec, vec, wspec, bspec],
        out_shape=[jax.ShapeDtypeStruct((s, 2 * BRANCH_W), BF16),
                   jax.ShapeDtypeStruct((1, BRANCH_W), F32), jax.ShapeDtypeStruct((1, BRANCH_W), F32),
                   jax.ShapeDtypeStruct((SGU_GROUPS, SGU_LEN, SGU_LEN), F32),
                   jax.ShapeDtypeStruct((SGU_GROUPS, SGU_LEN, 1), F32)],
        scratch_shapes=[pltpu.VMEM((tr, BRANCH_W), F32)],
        compiler_params=_cparams("arbitrary"),
    )(p, p, dyb, ln_g, ln_b, w, w_t, b_col)


def _shift_down(x, prev8, k):
    rolled = pltpu.roll(x, k, 0)
    r8 = lax.broadcasted_iota(jnp.int32, prev8.shape, 0)
    head = jnp.where(r8 < k, pltpu.roll(prev8, k, 0), rolled[:HALO])
    return jnp.concatenate([head, rolled[HALO:]], axis=0)


def _shift_up(x, next8, k):
    n = x.shape[0]
    rolled = pltpu.roll(x, n - k, 0)
    r8 = lax.broadcasted_iota(jnp.int32, next8.shape, 0)
    tail = jnp.where(r8 >= HALO - k, pltpu.roll(next8, HALO - k, 0), rolled[n - HALO:])
    return jnp.concatenate([rolled[:n - HALO], tail], axis=0)


def _conv_specs(s, tr):
    nb = tr // HALO
    last = s // HALO - 1
    tile = lambda cb: pl.BlockSpec((tr, 128), lambda j, i: (i, cb * 4 + j))
    above = lambda cb: pl.BlockSpec((HALO, 128), lambda j, i: (jnp.maximum(i * nb - 1, 0), cb * 4 + j))
    below = lambda cb: pl.BlockSpec((HALO, 128), lambda j, i: (jnp.minimum((i + 1) * nb, last), cb * 4 + j))
    return tile, above, below


def _conv_fwd(p, cw, *, name):
    s = p.shape[0]
    tr = _pick(s, 1024, HALO)
    tile, above, _ = _conv_specs(s, tr)

    def body(cb_ref, cc_ref, cx_ref, ccp_ref, cxp_ref, w_ref, o_ref):
        first = pl.program_id(1) == 0
        y = cc_ref[...] * cx_ref[...]
        yp = jnp.where(first, 0.0, ccp_ref[...] * cxp_ref[...])
        conv = w_ref[2:3, :] * y + w_ref[1:2, :] * _shift_down(y, yp, 1) + w_ref[0:1, :] * _shift_down(y, yp, 2)
        o_ref[...] = (cb_ref[...] * conv).astype(BF16)

    return pl.pallas_call(
        body, name=name, grid=(4, s // tr),
        in_specs=[tile(5), tile(6), tile(7), above(6), above(7), pl.BlockSpec((3, 128), lambda j, i: (0, j))],
        out_specs=pl.BlockSpec((tr, 128), lambda j, i: (i, j)),
        out_shape=jax.ShapeDtypeStruct((s, BRANCH_W), BF16),
        compiler_params=_cparams("parallel", "parallel"),
    )(p, p, p, p, p, cw)


def _conv_bwd(p, dyc, cw, *, name):
    s = p.shape[0]
    tr = _pick(s, 1024, HALO)
    nt = s // tr
    nb = tr // HALO
    last = s // HALO - 1
    tile, above, below = _conv_specs(s, tr)

    def body(cb_ref, cc_ref, cx_ref, ccp_ref, cxp_ref, cbn_ref, dy_ref, dyn_ref, w_ref,
             dcb_ref, dcc_ref, dcx_ref, dw_ref):
        i = pl.program_id(1)

        @pl.when(i == 0)
        def _():
            dw_ref[...] = jnp.zeros_like(dw_ref)

        cb = cb_ref[...]
        cc = cc_ref[...]
        cx = cx_ref[...]
        y = cc * cx
        yp = jnp.where(i == 0, 0.0, ccp_ref[...] * cxp_ref[...])
        y1 = _shift_down(y, yp, 1)
        y2 = _shift_down(y, yp, 2)
        w0, w1, w2 = w_ref[0:1, :], w_ref[1:2, :], w_ref[2:3, :]
        conv = w2 * y + w1 * y1 + w0 * y2
        dyc_v = dy_ref[...]
        dconv = dyc_v * cb
        dn = jnp.where(i == nt - 1, 0.0, dyn_ref[...] * cbn_ref[...])
        dyv = w2 * dconv + w1 * _shift_up(dconv, dn, 1) + w0 * _shift_up(dconv, dn, 2)
        dcb_ref[...] = (dyc_v * conv).astype(BF16)
        dcc_ref[...] = (dyv * cx).astype(BF16)
        dcx_ref[...] = (dyv * cc).astype(BF16)
        dw_ref[0:1, :] += jnp.sum(dconv * y2, axis=0, keepdims=True)
        dw_ref[1:2, :] += jnp.sum(dconv * y1, axis=0, keepdims=True)
        dw_ref[2:3, :] += jnp.sum(dconv * y, axis=0, keepdims=True)

    dy_tile = pl.BlockSpec((tr, 128), lambda j, i: (i, j))
    dy_below = pl.BlockSpec((HALO, 128), lambda j, i: (jnp.minimum((i + 1) * nb, last), j))
    out_tile = lambda cb: pl.BlockSpec((tr, 128), lambda j, i: (i, cb * 4 + j))
    w_spec = pl.BlockSpec((3, 128), lambda j, i: (0, j))
    dcb, dcc, dcx, dw = pl.pallas_call(
        body, name=name, grid=(4, nt),
        in_specs=[tile(5), tile(6), tile(7), above(6), above(7), below(5), dy_tile, dy_below, w_spec],
        out_specs=[dy_tile, dy_tile, dy_tile, w_spec],
        out_shape=[jax.ShapeDtypeStruct((s, BRANCH_W), BF16)] * 3 + [jax.ShapeDtypeStruct((3, BRANCH_W), F32)],
        compiler_params=_cparams("parallel", "arbitrary"),
    )(p, p, p, p, p, p, dyc, dyc, cw)
    return dcb, dcc, dcx, dw


def _merge_fwd(ya, yb, yc, wb, p, *, name):
    s = p.shape[0]
    tr = _pick(s, 512, 16)

    def body(ya_ref, yb_ref, yc_ref, wb_ref, g0_ref, g1_ref, g2_ref, o_ref):
        acc = jnp.zeros((tr, D_MODEL), F32)
        for n, (y_ref, g_ref) in enumerate(((ya_ref, g0_ref), (yb_ref, g1_ref), (yc_ref, g2_ref))):
            acc = acc + _sigmoid(g_ref[...]) * _dot(y_ref[...].astype(BF16), wb_ref[n])
        o_ref[...] = acc.astype(BF16)

    yspec = pl.BlockSpec((tr, BRANCH_W), lambda i: (i, 0))
    gate = lambda n: pl.BlockSpec((tr, D_MODEL), lambda i: (i, 4 + n))
    return pl.pallas_call(
        body, name=name, grid=(s // tr,),
        in_specs=[yspec, yspec, yspec, pl.BlockSpec((3, BRANCH_W, D_MODEL), lambda i: (0, 0, 0)),
                  gate(0), gate(1), gate(2)],
        out_specs=pl.BlockSpec((tr, D_MODEL), lambda i: (i, 0)),
        out_shape=jax.ShapeDtypeStruct((s, D_MODEL), BF16),
        compiler_params=_cparams("parallel"),
    )(ya, yb, yc, wb, p, p, p)


def _merge_bwd(dm, ya, yb, yc, wb, p, *, name):
    s = p.shape[0]
    tr = _pick(s, 256, 16)

    def body(dm_ref, ya_ref, yb_ref, yc_ref, wb_ref, g0_ref, g1_ref, g2_ref,
             dya_ref, dyb_ref, dyc_ref, dg_ref, dbrd0_ref, dbrd1_ref, dbrd2_ref):
        dmv = dm_ref[...]
        ys = (ya_ref, yb_ref, yc_ref)
        gs = (g0_ref, g1_ref, g2_ref)
        dys = (dya_ref, dyb_ref, dyc_ref)
        dbrds = (dbrd0_ref, dbrd1_ref, dbrd2_ref)
        for n in range(3):
            brd = _dot(ys[n][...].astype(BF16), wb_ref[n])
            sg = _sigmoid(gs[n][...])
            dbrd = (sg * dmv).astype(BF16)
            dbrds[n][...] = dbrd
            dg_ref[:, n * D_MODEL:(n + 1) * D_MODEL] = (dmv * brd * sg * (1.0 - sg)).astype(BF16)
            dys[n][...] = _dot_nt(dbrd, wb_ref[n]).astype(dys[n].dtype)

    yspec = pl.BlockSpec((tr, BRANCH_W), lambda i: (i, 0))
    gate = lambda n: pl.BlockSpec((tr, D_MODEL), lambda i: (i, 4 + n))
    row = pl.BlockSpec((tr, D_MODEL), lambda i: (i, 0))
    return pl.pallas_call(
        body, name=name, grid=(s // tr,),
        in_specs=[row, yspec, yspec, yspec, pl.BlockSpec((3, BRANCH_W, D_MODEL), lambda i: (0, 0, 0)),
                  gate(0), gate(1), gate(2)],
        out_specs=[yspec, yspec, yspec, pl.BlockSpec((tr, 3 * D_MODEL), lambda i: (i, 0)), row, row, row],
        out_shape=[jax.ShapeDtypeStruct((s, BRANCH_W), BF16)] + [jax.ShapeDtypeStruct((s, BRANCH_W), F32)] * 2
                  + [jax.ShapeDtypeStruct((s, 3 * D_MODEL), BF16)] + [jax.ShapeDtypeStruct((s, D_MODEL), BF16)] * 3,
        compiler_params=_cparams("parallel"),
    )(dm, ya, yb, yc, wb, p, p, p)


def _xa_probs(q, k):
    sc = _dot_nt(q, k) * XA_SCALE
    e = jnp.exp(sc - jnp.max(sc, axis=-1, keepdims=True))
    return e / jnp.sum(e, axis=-1, keepdims=True)


def _xa_fwd(q, k, v, *, name):
    s = q.shape[0]
    mt = k.shape[0]
    tr = _pick(s, 2048, 16)

    def body(q_ref, k_ref, v_ref, o_ref):
        pr = _xa_probs(q_ref[...], k_ref[...])
        o_ref[...] = _dot(pr.astype(BF16), v_ref[...]).astype(BF16)

    qs = pl.BlockSpec((tr, XA_HEAD), lambda h, i: (i, h))
    ks = pl.BlockSpec((mt, XA_HEAD), lambda h, i: (0, h))
    return pl.pallas_call(
        body, name=name, grid=(D_MODEL // XA_HEAD, s // tr),
        in_specs=[qs, ks, ks], out_specs=qs,
        out_shape=jax.ShapeDtypeStruct((s, D_MODEL), BF16),
        compiler_params=_cparams("parallel", "parallel"),
    )(q, k, v)


def _xa_bwd(q, k, v, do, *, name):
    s = q.shape[0]
    mt = k.shape[0]
    tr = _pick(s, 2048, 16)

    def body(q_ref, k_ref, v_ref, do_ref, dq_ref, dk_ref, dv_ref):
        @pl.when(pl.program_id(1) == 0)
        def _():
            dk_ref[...] = jnp.zeros_like(dk_ref)
            dv_ref[...] = jnp.zeros_like(dv_ref)

        qv = q_ref[...]
        kv = k_ref[...]
        dov = do_ref[...]
        pr = _xa_probs(qv, kv)
        dpr = _dot_nt(dov, v_ref[...])
        ds = (pr * (dpr - jnp.sum(dpr * pr, axis=-1, keepdims=True)) * XA_SCALE).astype(BF16)
        dq_ref[...] = _dot(ds, kv).astype(BF16)
        dk_ref[...] += _dot_tn(ds, qv)
        dv_ref[...] += _dot_tn(pr.astype(BF16), dov)

    qs = pl.BlockSpec((tr, XA_HEAD), lambda h, i: (i, h))
    ks = pl.BlockSpec((mt, XA_HEAD), lambda h, i: (0, h))
    return pl.pallas_call(
        body, name=name, grid=(D_MODEL // XA_HEAD, s // tr),
        in_specs=[qs, ks, ks, qs], out_specs=[qs, ks, ks],
        out_shape=[jax.ShapeDtypeStruct((s, D_MODEL), BF16), jax.ShapeDtypeStruct((mt, D_MODEL), F32),
                   jax.ShapeDtypeStruct((mt, D_MODEL), F32)],
        compiler_params=_cparams("parallel", "arbitrary"),
    )(q, k, v, do)


def _ffn_in(h, wg, wu, *, name):
    s, d = h.shape
    f = wg.shape[0]
    tm = _pick(s, 1024, 128)
    tn = _pick(f, 1408, 128)

    def body(h_ref, wg_ref, wu_ref, a_ref, b_ref, o_ref):
        hv = h_ref[...]
        av = _dot_nt(hv, wg_ref[...])
        bv = _dot_nt(hv, wu_ref[...])
        a_ref[...] = av.astype(BF16)
        b_ref[...] = bv.astype(BF16)
        o_ref[...] = (av * _sigmoid(av) * bv).astype(BF16)

    wspec = pl.BlockSpec((tn, d), lambda i, j: (j, 0))
    tile = pl.BlockSpec((tm, tn), lambda i, j: (i, j))
    return pl.pallas_call(
        body, name=name, grid=(s // tm, f // tn),
        in_specs=[pl.BlockSpec((tm, d), lambda i, j: (i, 0)), wspec, wspec],
        out_specs=[tile, tile, tile], out_shape=[jax.ShapeDtypeStruct((s, f), BF16)] * 3,
        compiler_params=_cparams("parallel", "parallel"),
    )(h, wg, wu)


def _ffn_in_bwd(dx, wd, a, b, *, name):
    s, d = dx.shape
    f = wd.shape[0]
    tm = _pick(s, 1024, 128)
    tn = _pick(f, 1408, 128)

    def body(dx_ref, wd_ref, a_ref, b_ref, da_ref, db_ref):
        dhv = _dot_nt(dx_ref[...].astype(BF16), wd_ref[...])
        av = a_ref[...].astype(F32)
        sg = _sigmoid(av)
        silu = av * sg
        da_ref[...] = (dhv * b_ref[...].astype(F32) * (sg + silu * (1.0 - sg))).astype(BF16)
        db_ref[...] = (dhv * silu).astype(BF16)

    tile = pl.BlockSpec((tm, tn), lambda i, j: (i, j))
    return pl.pallas_call(
        body, name=name, grid=(s // tm, f // tn),
        in_specs=[pl.BlockSpec((tm, d), lambda i, j: (i, 0)), pl.BlockSpec((tn, d), lambda i, j: (j, 0)), tile, tile],
        out_specs=[tile, tile], out_shape=[jax.ShapeDtypeStruct((s, f), BF16)] * 2,
        compiler_params=_cparams("parallel", "parallel"),
    )(dx, wd, a, b)


def _adamw(w, g, m, v, *, name):
    r, c = w.shape
    tr = _pick(r, 512, 8)

    def body(w_ref, g_ref, m_ref, v_ref, d_ref, mo_ref, vo_ref):
        gv = g_ref[...]
        mn = ADAM_B1 * m_ref[...] + (1.0 - ADAM_B1) * gv
        vn = ADAM_B2 * v_ref[...] + (1.0 - ADAM_B2) * (gv * gv)
        m_hat = mn / (1.0 - ADAM_B1 ** ADAM_STEP)
        v_hat = vn / (1.0 - ADAM_B2 ** ADAM_STEP)
        d_ref[...] = -ADAM_LR * (m_hat / (jnp.sqrt(v_hat) + ADAM_EPS) + ADAM_WD * w_ref[...])
        mo_ref[...] = mn
        vo_ref[...] = vn

    spec = pl.BlockSpec((tr, c), lambda i: (i, 0))
    shp = jax.ShapeDtypeStruct((r, c), F32)
    return pl.pallas_call(
        body, name=name, grid=(r // tr,), in_specs=[spec] * 4, out_specs=[spec] * 3,
        out_shape=[shp] * 3, compiler_params=_cparams("parallel"),
    )(w, g, m, v)


def _position():
    return lax.axis_index("x"), lax.axis_index("y"), lax.axis_index("c")


def _all_gather(x, *, name):
    t, c_ = x.shape

    def body(x_ref, out_ref, send_sems, recv_sems, local_sem):
        start, forward, finish = _gather_phases(x_ref, out_ref, send_sems, recv_sems, local_sem)
        start()
        forward()
        finish()

    return pl.pallas_call(
        body, name=name,
        out_shape=jax.ShapeDtypeStruct((N_DEV, t, c_), x.dtype),
        in_specs=[pl.BlockSpec(memory_space=pl.ANY)],
        out_specs=pl.BlockSpec(memory_space=pl.ANY),
        scratch_shapes=_GATHER_SEMS,
    )(x)


_GATHER_SEMS = [pltpu.SemaphoreType.DMA((7,)), pltpu.SemaphoreType.DMA((7,)), pltpu.SemaphoreType.DMA]


def _gather_phases(x_ref, out_ref, send_sems, recv_sems, local_sem):
    x_, y_, c = _position()
    me, sibling = (x_, y_, c), (x_, y_, 1 - c)
    chips = [(1 - x_, y_), (x_, 1 - y_), (1 - x_, 1 - y_)]

    def block(px, py, pc):
        return out_ref.at[4 * px + 2 * py + pc]

    def copy(k, blk, to, src=None):
        return pltpu.make_async_remote_copy(
            src_ref=block(*blk) if src is None else src, dst_ref=block(*blk),
            send_sem=send_sems.at[k], recv_sem=recv_sems.at[k], device_id=to, device_id_type=MESH)

    mine = pltpu.make_async_copy(x_ref, block(*me), local_sem)
    first = [copy(0, me, sibling, src=x_ref)]
    first += [copy(1 + j, me, (*chip, c), src=x_ref) for j, chip in enumerate(chips)]
    passed = [copy(4 + j, (*chip, c), sibling) for j, chip in enumerate(chips)]

    def start():
        mine.start()
        for cp in first:
            cp.start()

    def forward():
        for j, chip in enumerate(chips):
            copy(1 + j, (*chip, c), me).wait_recv()
            passed[j].start()

    def finish():
        copy(0, sibling, me).wait_recv()
        for j, chip in enumerate(chips):
            copy(4 + j, (*chip, 1 - c), me).wait_recv()
        for cp in first + passed:
            cp.wait_send()
        mine.wait()

    return start, forward, finish


_SCATTER_SEMS = [pltpu.SemaphoreType.DMA((7,)), pltpu.SemaphoreType.DMA((7,)), pltpu.SemaphoreType.DMA]


def _scatter_phases(g_ref, r_ref, send_sems, recv_sems, local_sem):
    x_, y_, c = _position()
    me = 4 * x_ + 2 * y_ + c
    local = pltpu.make_async_copy(g_ref.at[me], r_ref.at[me], local_sem)
    copies = []
    for k in range(1, N_DEV):
        to = (x_ ^ (k >> 2), y_ ^ ((k >> 1) & 1), c ^ (k & 1))
        copies.append(pltpu.make_async_remote_copy(
            src_ref=g_ref.at[me ^ k], dst_ref=r_ref.at[me], send_sem=send_sems.at[k - 1],
            recv_sem=recv_sems.at[k - 1], device_id=to, device_id_type=MESH))

    def start():
        local.start()
        for cp in copies:
            cp.start()

    def finish():
        for k in range(1, N_DEV):
            pltpu.make_async_remote_copy(
                src_ref=g_ref.at[me], dst_ref=r_ref.at[me ^ k], send_sem=send_sems.at[k - 1],
                recv_sem=recv_sems.at[k - 1], device_id=(x_, y_, c), device_id_type=MESH).wait_recv()
        for cp in copies:
            cp.wait_send()
        local.wait()

    return start, finish


def _sum_devices(r8, *, name):
    _, t, c_ = r8.shape
    tr = _pick(t, 512, 16)

    def body(r_ref, o_ref):
        acc = r_ref[0].astype(F32)
        for d in range(1, N_DEV):
            acc = acc + r_ref[d].astype(F32)
        o_ref[...] = acc

    return pl.pallas_call(
        body, name=name, grid=(t // tr,),
        in_specs=[pl.BlockSpec((N_DEV, tr, c_), lambda i: (0, i, 0))],
        out_specs=pl.BlockSpec((tr, c_), lambda i: (i, 0)),
        out_shape=jax.ShapeDtypeStruct((t, c_), F32),
        compiler_params=_cparams("parallel"),
    )(r8)


def _all_reduce_small(x, *, name):
    r, c_ = x.shape

    def body(x_ref, o_ref, buf, send_sems, recv_sems):
        x_, y_, c = _position()
        me = 4 * x_ + 2 * y_ + c
        buf[me] = x_ref[...]
        copies = []
        for k in range(1, N_DEV):
            to = (x_ ^ (k >> 2), y_ ^ ((k >> 1) & 1), c ^ (k & 1))
            copies.append(pltpu.make_async_remote_copy(
                src_ref=x_ref, dst_ref=buf.at[me], send_sem=send_sems.at[k - 1], recv_sem=recv_sems.at[k - 1],
                device_id=to, device_id_type=MESH))
        for cp in copies:
            cp.start()
        for k in range(1, N_DEV):
            src = me ^ k
            pltpu.make_async_remote_copy(
                src_ref=x_ref, dst_ref=buf.at[src], send_sem=send_sems.at[k - 1], recv_sem=recv_sems.at[k - 1],
                device_id=(x_, y_, c), device_id_type=MESH).wait_recv()
        for cp in copies:
            cp.wait_send()
        acc = buf[0]
        for d in range(1, N_DEV):
            acc = acc + buf[d]
        o_ref[...] = acc

    return pl.pallas_call(
        body, name=name,
        out_shape=jax.ShapeDtypeStruct((r, c_), F32),
        in_specs=[pl.BlockSpec(memory_space=pltpu.VMEM)],
        out_specs=pl.BlockSpec(memory_space=pltpu.VMEM),
        scratch_shapes=[pltpu.VMEM((N_DEV, r, c_), F32), pltpu.SemaphoreType.DMA((7,)), pltpu.SemaphoreType.DMA((7,))],
    )(x)


def _rs_pair_exchange(g8, *, name):
    _, t, c_ = g8.shape

    def body(g_ref, r_ref, send_sems, recv_sems):
        x_, y_, c = _position()
        copies = [pltpu.make_async_remote_copy(
            src_ref=g_ref.at[2 * ch + (1 - c)], dst_ref=r_ref.at[ch],
            send_sem=send_sems.at[ch], recv_sem=recv_sems.at[ch],
            device_id=(x_, y_, 1 - c), device_id_type=MESH) for ch in range(4)]
        for cp in copies:
            cp.start()
        for cp in copies:
            cp.wait()

    return pl.pallas_call(
        body, name=name,
        out_shape=jax.ShapeDtypeStruct((4, t, c_), g8.dtype),
        in_specs=[pl.BlockSpec(memory_space=pl.ANY)],
        out_specs=pl.BlockSpec(memory_space=pl.ANY),
        scratch_shapes=[pltpu.SemaphoreType.DMA((4,)), pltpu.SemaphoreType.DMA((4,))],
    )(g8)


def _pair_add(core, g8, recv, *, name):
    _, t, c_ = g8.shape
    tr = _pick(t, 512, 16)

    def body(core_ref, g_ref, r_ref, o_ref):
        o_ref[...] = (g_ref[...].astype(F32) + r_ref[...].astype(F32)).astype(o_ref.dtype)

    grid_spec = pltpu.PrefetchScalarGridSpec(
        num_scalar_prefetch=1, grid=(4, t // tr),
        in_specs=[pl.BlockSpec((None, tr, c_), lambda ch, i, core_ref: (2 * ch + core_ref[0], i, 0)),
                  pl.BlockSpec((None, tr, c_), lambda ch, i, core_ref: (ch, i, 0))],
        out_specs=pl.BlockSpec((None, tr, c_), lambda ch, i, core_ref: (ch, i, 0)))
    return pl.pallas_call(
        body, name=name, grid_spec=grid_spec,
        out_shape=jax.ShapeDtypeStruct((4, t, c_), g8.dtype),
        compiler_params=_cparams("parallel", "parallel"),
    )(core, g8, recv)


def _rs_chip_exchange(part, *, name):
    _, t, c_ = part.shape

    def body(p_ref, r_ref, send_sems, recv_sems, local_sem):
        x_, y_, c = _position()
        mine = 2 * x_ + y_
        local = pltpu.make_async_copy(p_ref.at[mine], r_ref.at[mine], local_sem)
        local.start()
        chips = [(1 - x_, y_), (x_, 1 - y_), (1 - x_, 1 - y_)]
        copies = [pltpu.make_async_remote_copy(
            src_ref=p_ref.at[2 * px + py], dst_ref=r_ref.at[mine],
            send_sem=send_sems.at[k], recv_sem=recv_sems.at[k],
            device_id=(px, py, c), device_id_type=MESH) for k, (px, py) in enumerate(chips)]
        for cp in copies:
            cp.start()
        for k, (px, py) in enumerate(chips):
            pltpu.make_async_remote_copy(
                src_ref=p_ref.at[mine], dst_ref=r_ref.at[2 * px + py],
                send_sem=send_sems.at[k], recv_sem=recv_sems.at[k],
                device_id=(x_, y_, c), device_id_type=MESH).wait_recv()
        for cp in copies:
            cp.wait_send()
        local.wait()

    return pl.pallas_call(
        body, name=name,
        out_shape=jax.ShapeDtypeStruct((4, t, c_), part.dtype),
        in_specs=[pl.BlockSpec(memory_space=pl.ANY)],
        out_specs=pl.BlockSpec(memory_space=pl.ANY),
        scratch_shapes=[pltpu.SemaphoreType.DMA((3,)), pltpu.SemaphoreType.DMA((3,)), pltpu.SemaphoreType.DMA],
    )(part)


def _sum_chips(r4, *, name):
    _, t, c_ = r4.shape
    tr = _pick(t, 512, 16)

    def body(r_ref, o_ref):
        acc = r_ref[0].astype(F32)
        for ch in range(1, 4):
            acc = acc + r_ref[ch].astype(F32)
        o_ref[...] = acc

    return pl.pallas_call(
        body, name=name, grid=(t // tr,),
        in_specs=[pl.BlockSpec((4, tr, c_), lambda i: (0, i, 0))],
        out_specs=pl.BlockSpec((tr, c_), lambda i: (i, 0)),
        out_shape=jax.ShapeDtypeStruct((t, c_), F32),
        compiler_params=_cparams("parallel"),
    )(r4)


BIG = (
    ("w_in", (IN_COLS // N_DEV, D_MODEL), 0),
    ("w_branch", (3, BRANCH_W, D_MODEL // N_DEV), 2),
    ("w_out", (D_MODEL // N_DEV, D_MODEL), 0),
    ("w_q_xa", (D_MODEL // N_DEV, D_MODEL), 0),
    ("w_k_xa", (D_MODEL // N_DEV, D_MODEL), 0),
    ("w_v_xa", (D_MODEL // N_DEV, D_MODEL), 0),
    ("w_o_xa", (D_MODEL // N_DEV, D_MODEL), 0),
    ("w_gate_ffn", (FFN // N_DEV, D_MODEL), 0),
    ("w_up_ffn", (FFN // N_DEV, D_MODEL), 0),
    ("w_down_ffn", (FFN // N_DEV, D_MODEL), 0),
)
TRANSPOSED = ("w_in", "w_gate_ffn", "w_up_ffn")
_BIG_LAYOUT = {n: (shp, ax) for n, shp, ax in BIG}
PACK_COLS = 1024


def _stored(name, shard):
    return shard.T if name in TRANSPOSED else shard


def _size(shape):
    n = 1
    for d in shape:
        n *= d
    return n


def _pack_shards(items, shards):
    return jnp.concatenate([shards[it].reshape(-1, PACK_COLS) for it in items], axis=0)


def _unpack_gathered(items, g):
    out = {}
    r0 = 0
    for it in items:
        shp, ax = _BIG_LAYOUT[it[0]]
        rows = _size(shp) // PACK_COLS
        blk = g[:, r0:r0 + rows].reshape((N_DEV,) + shp)
        r0 += rows
        blk = jnp.moveaxis(blk, 0, ax)
        full = list(shp)
        full[ax] = shp[ax] * N_DEV
        out[it] = blk.reshape(full)
    return out


def _pack_full(items, full):
    parts = []
    for it in items:
        shp, ax = _BIG_LAYOUT[it[0]]
        t = full[it].reshape(shp[:ax] + (N_DEV, shp[ax]) + shp[ax + 1:])
        t = jnp.moveaxis(t, ax, 0)
        parts.append(t.reshape(N_DEV, -1, PACK_COLS))
    rows = sum(part.shape[1] for part in parts)
    if rows % 128:
        parts.append(jnp.zeros((N_DEV, 128 - rows % 128, PACK_COLS), parts[0].dtype))
    return jnp.concatenate(parts, axis=1)


def _unpack_shard(items, flat):
    out = {}
    r0 = 0
    for it in items:
        shp, _ = _BIG_LAYOUT[it[0]]
        rows = _size(shp) // PACK_COLS
        out[it] = flat[r0:r0 + rows].reshape(shp)
        r0 += rows
    return out


SMALL = (
    ("norm_mix_g", (DEPTH, D_MODEL)),
    ("sgu_ln_g", (DEPTH, BRANCH_W)),
    ("sgu_ln_b", (DEPTH, BRANCH_W)),
    ("w_spatial", (DEPTH, SGU_GROUPS, SGU_LEN, SGU_LEN)),
    ("b_spatial", (DEPTH, SGU_GROUPS, SGU_LEN)),
    ("conv_w", (DEPTH, 3, BRANCH_W)),
    ("norm_xa_g", (DEPTH, D_MODEL)),
    ("mem_norm_g", (DEPTH, D_MODEL)),
    ("norm_ffn_g", (DEPTH, D_MODEL)),
    ("final_g", (D_MODEL,)),
)


def _pack_small(grads):
    flat = jnp.concatenate([grads[n].reshape(-1) for n, _ in SMALL])
    rows = -(-flat.shape[0] // PACK_COLS)
    rows = -(-rows // 8) * 8
    flat = jnp.pad(flat, (0, rows * PACK_COLS - flat.shape[0]))
    return flat.reshape(rows, PACK_COLS)


def _unpack_small(buf):
    flat = buf.reshape(-1)
    out = {}
    o = 0
    for n, shp in SMALL:
        out[n] = flat[o:o + _size(shp)].reshape(shp)
        o += _size(shp)
    return out


def _layer_fwd(l, x, mem, wt, sm, gather=None):
    t = f"l{l}_"
    sv = {"x0": x}
    h = _rms_fwd(x, sm["norm_mix_g"][l][None], name=t + "rms_mix")
    p = _mm(h, wt["w_in", l], tb=True, name=t + "in_proj", tm=2048)
    if gather is None:
        ya, *tables = _sb_fwd(p, name=t + "sb_fwd")
    else:
        ya, *tables, gathered = _sb_fwd(p, name=t + "sb_fwd", gather=gather[1])
        wt.update(_unpack_gathered(gather[0], gathered))
    w_sp = sm["w_spatial"][l]
    b_col = sm["b_spatial"][l][:, :, None]
    ln_g, ln_b = sm["sgu_ln_g"][l][None], sm["sgu_ln_b"][l][None]
    yb = _sgu_fwd(p, ln_g, ln_b, w_sp, b_col, name=t + "sgu_fwd")
    yc = _conv_fwd(p, sm["conv_w"][l], name=t + "conv_fwd")
    merged = _merge_fwd(ya, yb, yc, wt["w_branch", l], p, name=t + "merge_fwd")
    x1 = _mm(merged, wt["w_out", l], add=x, name=t + "out_proj")
    sv.update(h=h, p=p, ya=ya, yb=yb, yc=yc, merged=merged, x1=x1, tables=tables)

    h2 = _rms_fwd(x1, sm["norm_xa_g"][l][None], name=t + "rms_xa")
    mn = _rms_fwd(mem, sm["mem_norm_g"][l][None], name=t + "rms_mem")
    q = _mm(h2, wt["w_q_xa", l], out_dtype=BF16, name=t + "xa_q", tm=2048)
    k = _mm(mn, wt["w_k_xa", l], out_dtype=BF16, name=t + "xa_k")
    v = _mm(mn, wt["w_v_xa", l], out_dtype=BF16, name=t + "xa_v")
    o = _xa_fwd(q, k, v, name=t + "xa_fwd")
    x2 = _mm(o, wt["w_o_xa", l], add=x1, name=t + "xa_o")
    sv.update(h2=h2, mn=mn, q=q, k=k, v=v, o=o, x2=x2)

    h3 = _rms_fwd(x2, sm["norm_ffn_g"][l][None], name=t + "rms_ffn")
    a, b, hd = _ffn_in(h3, wt["w_gate_ffn", l], wt["w_up_ffn", l], name=t + "ffn_in")
    x3 = _mm(hd, wt["w_down_ffn", l], add=x2, name=t + "ffn_down", tk=FFN)
    sv.update(h3=h3, a=a, b=b, hd=hd)
    return x3, sv


def _layer_bwd(l, dx3, mem, wt, sm, sv, scatter=None, scatter_w_in=False):
    t = f"l{l}_b_"
    gb, gs = {}, {}
    gb["w_down_ffn"] = _mm(sv["hd"], dx3, ta=True, out_dtype=BF16, name=t + "ffn_down_dw", tm=1408, tk=1024)
    da, db = _ffn_in_bwd(dx3, wt["w_down_ffn", l], sv["a"], sv["b"], name=t + "ffn_in_bwd")
    gb["w_gate_ffn"] = _mm(da, sv["h3"], ta=True, out_dtype=BF16, name=t + "ffn_gate_dw", tm=1408, tk=1024)
    gb["w_up_ffn"] = _mm(db, sv["h3"], ta=True, out_dtype=BF16, name=t + "ffn_up_dw", tm=1408, tk=1024)
    dh3 = _mm(da, wt["w_gate_ffn", l], name=t + "ffn_gate_dx", tk=1408)
    dx2, dg = _mm(db, wt["w_up_ffn", l], add=dh3, rms=(sv["x2"], sm["norm_ffn_g"][l][None], dx3),
                  name=t + "ffn_up_dx", tm=512, tk=1408)
    gs["norm_ffn_g"] = dg[0]
    do = _mm(dx2, wt["w_o_xa", l], tb=True, out_dtype=BF16, name=t + "xa_o_dx")
    gb["w_o_xa"] = _mm(sv["o"], dx2, ta=True, out_dtype=BF16, name=t + "xa_o_dw")
    dq, dk, dv = _xa_bwd(sv["q"], sv["k"], sv["v"], do, name=t + "xa_bwd")
    dx1, dg = _mm(dq, wt["w_q_xa", l], tb=True, rms=(sv["x1"], sm["norm_xa_g"][l][None], dx2),
                  name=t + "xa_q_dx", tm=1024)
    gs["norm_xa_g"] = dg[0]
    gb["w_q_xa"] = _mm(sv["h2"], dq, ta=True, out_dtype=BF16, name=t + "xa_q_dw")
    gb["w_k_xa"] = _mm(sv["mn"], dk, ta=True, out_dtype=BF16, name=t + "xa_k_dw")
    gb["w_v_xa"] = _mm(sv["mn"], dv, ta=True, out_dtype=BF16, name=t + "xa_v_dw")
    dmn = _mm(dk, wt["w_k_xa", l], tb=True, name=t + "xa_k_dx")
    dmn = _mm(dv, wt["w_v_xa", l], tb=True, add=dmn, name=t + "xa_v_dx")
    _, dg = _rms_bwd(mem, sm["mem_norm_g"][l][None], dmn, jnp.zeros_like(mem), name=t + "rms_mem")
    gs["mem_norm_g"] = dg[0]
    dm = _mm(dx1, wt["w_out", l], tb=True, name=t + "out_proj_dx")
    gb["w_out"] = _mm(sv["merged"], dx1, ta=True, out_dtype=BF16, name=t + "out_proj_dw")
    p = sv["p"]
    dya, dyb, dyc, dgates, *dbrd = _merge_bwd(dm, sv["ya"], sv["yb"], sv["yc"], wt["w_branch", l], p,
                                              name=t + "merge_bwd")
    gb["w_branch"] = jnp.stack([
        _mm(sv[y], dbrd[n], ta=True, out_dtype=BF16, name=t + f"branch{n}_dw")
        for n, y in enumerate(("ya", "yb", "yc"))])
    dcb, dcc, dcx, dcw = _conv_bwd(p, dyc, sm["conv_w"][l], name=t + "conv_bwd")
    gs["conv_w"] = dcw
    w_sp = sm["w_spatial"][l]
    dz, dlg, dlb, dwsp, dbsp = _sgu_bwd(p, dyb, sm["sgu_ln_g"][l][None], sm["sgu_ln_b"][l][None], w_sp,
                                        jnp.swapaxes(w_sp, 1, 2), sm["b_spatial"][l][:, :, None],
                                        name=t + "sgu_bwd")
    gs.update(sgu_ln_g=dlg[0], sgu_ln_b=dlb[0], w_spatial=dwsp, b_spatial=dbsp[:, :, 0])
    received = None
    if scatter is None:
        dq_a, dk_a, dv_a = _sb_bwd(p, dya, sv["tables"], name=t + "sb_bwd")
    else:
        items, earlier = scatter
        ready = {**earlier, **{(n, l): g for n, g in gb.items()}}
        dq_a, dk_a, dv_a, received = _sb_bwd(p, dya, sv["tables"], name=t + "sb_bwd",
                                             scatter=_pack_full(items, ready))
    dp = jnp.concatenate([dq_a, dk_a, dv_a, dz, dcb, dcc, dcx, dgates], axis=1)
    gb["w_in"] = _mm(dp, sv["h"], ta=True, out_dtype=BF16, name=t + "in_proj_dw")
    tail = None if not scatter_w_in else _pack_full([("w_in", l)], {("w_in", l): gb["w_in"]})
    dx, dg, *received_w_in = _mm(dp, wt["w_in", l], rms=(sv["x0"], sm["norm_mix_g"][l][None], dx1),
                                 scatter=tail, name=t + "in_proj_dx", tm=1024, tk=1792)
    gs["norm_mix_g"] = dg[0]
    return dx, gb, gs, received, (received_w_in[0] if scatter_w_in else None)


_WEIGHTS = ("norm_mix_g", "w_in", "sgu_ln_g", "sgu_ln_b", "w_spatial", "b_spatial", "conv_w", "w_branch", "w_out",
            "norm_xa_g", "mem_norm_g", "w_q_xa", "w_k_xa", "w_v_xa", "w_o_xa", "norm_ffn_g", "w_gate_ffn",
            "w_up_ffn", "w_down_ffn", "final_g")


def kernel(x, mem, norm_mix_g, w_in, sgu_ln_g, sgu_ln_b, w_spatial, b_spatial, conv_w, w_branch, w_out, norm_xa_g, mem_norm_g, w_q_xa, w_k_xa, w_v_xa, w_o_xa, norm_ffn_g, w_gate_ffn, w_up_ffn, w_down_ffn, final_g, loss_target, m_norm_mix_g, m_w_in, m_sgu_ln_g, m_sgu_ln_b, m_w_spatial, m_b_spatial, m_conv_w, m_w_branch, m_w_out, m_norm_xa_g, m_mem_norm_g, m_w_q_xa, m_w_k_xa, m_w_v_xa, m_w_o_xa, m_norm_ffn_g, m_w_gate_ffn, m_w_up_ffn, m_w_down_ffn, m_final_g, v_norm_mix_g, v_w_in, v_sgu_ln_g, v_sgu_ln_b, v_w_spatial, v_b_spatial, v_conv_w, v_w_branch, v_w_out, v_norm_xa_g, v_mem_norm_g, v_w_q_xa, v_w_k_xa, v_w_v_xa, v_w_o_xa, v_norm_ffn_g, v_w_gate_ffn, v_w_up_ffn, v_w_down_ffn, v_final_g):
    w = dict(norm_mix_g=norm_mix_g, w_in=w_in, sgu_ln_g=sgu_ln_g, sgu_ln_b=sgu_ln_b, w_spatial=w_spatial,
             b_spatial=b_spatial, conv_w=conv_w, w_branch=w_branch, w_out=w_out, norm_xa_g=norm_xa_g,
             mem_norm_g=mem_norm_g, w_q_xa=w_q_xa, w_k_xa=w_k_xa, w_v_xa=w_v_xa, w_o_xa=w_o_xa,
             norm_ffn_g=norm_ffn_g, w_gate_ffn=w_gate_ffn, w_up_ffn=w_up_ffn, w_down_ffn=w_down_ffn, final_g=final_g)
    m = dict(norm_mix_g=m_norm_mix_g, w_in=m_w_in, sgu_ln_g=m_sgu_ln_g, sgu_ln_b=m_sgu_ln_b, w_spatial=m_w_spatial,
             b_spatial=m_b_spatial, conv_w=m_conv_w, w_branch=m_w_branch, w_out=m_w_out, norm_xa_g=m_norm_xa_g,
             mem_norm_g=m_mem_norm_g, w_q_xa=m_w_q_xa, w_k_xa=m_w_k_xa, w_v_xa=m_w_v_xa, w_o_xa=m_w_o_xa,
             norm_ffn_g=m_norm_ffn_g, w_gate_ffn=m_w_gate_ffn, w_up_ffn=m_w_up_ffn, w_down_ffn=m_w_down_ffn,
             final_g=m_final_g)
    v = dict(norm_mix_g=v_norm_mix_g, w_in=v_w_in, sgu_ln_g=v_sgu_ln_g, sgu_ln_b=v_sgu_ln_b, w_spatial=v_w_spatial,
             b_spatial=v_b_spatial, conv_w=v_conv_w, w_branch=v_w_branch, w_out=v_w_out, norm_xa_g=v_norm_xa_g,
             mem_norm_g=v_mem_norm_g, w_q_xa=v_w_q_xa, w_k_xa=v_w_k_xa, w_v_xa=v_w_v_xa, w_o_xa=v_w_o_xa,
             norm_ffn_g=v_norm_ffn_g, w_gate_ffn=v_w_gate_ffn, w_up_ffn=v_w_up_ffn, w_down_ffn=v_w_down_ffn,
             final_g=v_final_g)

    names = [n for n, _, _ in BIG]
    shards = {(n, l): _stored(n, w[n][l].astype(BF16)) for n in names for l in range(DEPTH)}
    first_items = [("w_in", 0)]
    mid_items = [(n, 0) for n in names if n != "w_in"] + [("w_in", 1)]
    last_items = [(n, 1) for n in names if n != "w_in"]
    wt = _unpack_gathered(first_items, _all_gather(_pack_shards(first_items, shards), name="gather_w_in0"))
    cw_pad = jnp.zeros((8, 128), F32).at[:DEPTH * 3, :BRANCH_W // N_DEV].set(conv_w.reshape(DEPTH * 3, -1))
    cw_all = _all_gather(cw_pad, name="gather_conv_w")[:, :DEPTH * 3, :BRANCH_W // N_DEV]
    conv_full = jnp.moveaxis(cw_all.reshape(N_DEV, DEPTH, 3, BRANCH_W // N_DEV), 0, 2).reshape(DEPTH, 3, BRANCH_W)
    sm = {n: w[n] for n, _ in SMALL}
    sm["conv_w"] = conv_full

    xs, ms = x[0], mem[0]
    x1, saved0 = _layer_fwd(0, xs, ms, wt, sm, gather=(mid_items, _pack_shards(mid_items, shards)))
    x2, saved1 = _layer_fwd(1, x1, ms, wt, sm, gather=(last_items, _pack_shards(last_items, shards)))
    dcur, loss, dfinal = _final_loss(x2, sm["final_g"][None], loss_target[0], name="final_loss")
    loss = lax.psum(loss[0, 0], AXES)
    items_a = [(n, 1) for n in names if n != "w_in"]
    items_b = [("w_in", 1)] + [(n, 0) for n in names if n != "w_in"]
    items_c = [("w_in", 0)]
    dcur, gb1, gs1, recv_a, _ = _layer_bwd(1, dcur, ms, wt, sm, saved1, scatter=(items_a, {}))
    dx, gb0, gs0, recv_b, recv_c = _layer_bwd(0, dcur, ms, wt, sm, saved0,
                                              scatter=(items_b, {("w_in", 1): gb1["w_in"]}), scatter_w_in=True)

    shard_grads = _unpack_shard(items_a, _sum_devices(recv_a, name="rs_sum_a"))
    shard_grads.update(_unpack_shard(items_b, _sum_devices(recv_b, name="rs_sum_b")))
    shard_grads.update(_unpack_shard(items_c, _sum_devices(recv_c, name="rs_sum_c")))
    grads = {n: jnp.stack([_stored(n, shard_grads[n, l]) for l in range(DEPTH)]) for n in names}
    small = {n: jnp.stack([gs0[n], gs1[n]]) for n, _ in SMALL if n != "final_g"}
    small["final_g"] = dfinal[0]
    small_sum = _unpack_small(_all_reduce_small(_pack_small(small), name="all_reduce_small"))
    width = BRANCH_W // N_DEV
    dev = 4 * lax.axis_index("x") + 2 * lax.axis_index("y") + lax.axis_index("c")
    for n, _ in SMALL:
        grads[n] = small_sum[n]
    grads["conv_w"] = lax.dynamic_slice_in_dim(small_sum["conv_w"], dev * width, width, axis=2)

    delta, new_m, new_v = {}, {}, {}
    for n in _WEIGHTS:
        shp = w[n].shape
        two_d = (-1, shp[-1])
        d_, m_, v_ = _adamw(w[n].reshape(two_d), grads[n].reshape(two_d), m[n].reshape(two_d), v[n].reshape(two_d),
                            name="adamw_" + n)
        delta[n], new_m[n], new_v[n] = d_.reshape(shp), m_.reshape(shp), v_.reshape(shp)

    return (loss, dx[None], *[grads[n] for n in _WEIGHTS], *[delta[n] for n in _WEIGHTS],
            *[new_m[n] for n in _WEIGHTS], *[new_v[n] for n in _WEIGHTS])
```
